```python
import math
import jax, jax.numpy as jnp
from jax import lax
import numpy as np

D_MODEL = 2048
BATCH = 8
SEQ = 8192
DEPTH = 2

A_HEADS = 8
A_KV_HEADS = 2
A_HEAD_DIM = 64
A_WIDTH = A_HEADS * A_HEAD_DIM
A_KV_WIDTH = A_KV_HEADS * A_HEAD_DIM
WINDOW = 128
A_BLOCK = 128
ROPE_THETA = 10000.0
R_WIDTH = 1024
R_BLOCKS = 8
R_BLOCK_DIM = R_WIDTH // R_BLOCKS
R_C = 8.0
CONV_WIDTH = 4
G_HEADS = 4
G_HEAD_DIM = 128
G_WIDTH = G_HEADS * G_HEAD_DIM
G_CHUNK = 64
MIX_WIDTH = A_WIDTH + R_WIDTH + G_WIDTH
IN_SIZES = (A_WIDTH, A_KV_WIDTH, A_KV_WIDTH, A_WIDTH, R_WIDTH, R_WIDTH,
            G_WIDTH, G_WIDTH, G_WIDTH, G_WIDTH, G_HEADS, G_HEADS)
N_IN = sum(IN_SIZES)
DEEPNORM_ALPHA = (2 * DEPTH) ** 0.25
DEEPNORM_BETA = (8 * DEPTH) ** -0.25
LN_EPS = 1e-5
RMS_EPS = 1e-6

kernel_name = "hybrid_swa_rglru_gdn_deepnorm"


def layer_norm(x, g, b):
    xf = x.astype(jnp.float32)
    mu = jnp.mean(xf, -1, keepdims=True)
    var = jnp.mean(jnp.square(xf - mu), -1, keepdims=True)
    return ((xf - mu) * lax.rsqrt(var + LN_EPS) * g.astype(jnp.float32) + b.astype(jnp.float32)).astype(x.dtype)


def rope_tables(seq, dim):
    inv = 1.0 / (ROPE_THETA ** (jnp.arange(0, dim, 2, dtype=jnp.float32) / dim))
    ang = jnp.arange(seq, dtype=jnp.float32)[:, None] * inv[None, :]
    return jnp.cos(ang), jnp.sin(ang)


def apply_rope(x, cos, sin):
    xf = x.astype(jnp.float32)
    x1, x2 = jnp.split(xf, 2, axis=-1)
    c = cos[None, :, None, :]
    s = sin[None, :, None, :]
    return jnp.concatenate([x1 * c - x2 * s, x2 * c + x1 * s], axis=-1).astype(x.dtype)


def causal_depthwise_conv(x, w):
    return lax.conv_general_dilated(
        x, w[:, None, :].astype(x.dtype), window_strides=(1,),
        padding=[(CONV_WIDTH - 1, 0)], dimension_numbers=("NWC", "WIO", "NWC"),
        feature_group_count=x.shape[-1])


def sliding_window_attention(q, k, v, sinks):
    B, S, _, D = q.shape
    nb = S // A_BLOCK
    grp = A_HEADS // A_KV_HEADS
    qb = q.reshape(B, nb, A_BLOCK, A_KV_HEADS, grp, D)
    pad = ((0, 0), (A_BLOCK, 0), (0, 0), (0, 0))
    kp = jnp.pad(k, pad).reshape(B, nb + 1, A_BLOCK, A_KV_HEADS, D)
    vp = jnp.pad(v, pad).reshape(B, nb + 1, A_BLOCK, A_KV_HEADS, D)
    kw = jnp.concatenate([kp[:, :-1], kp[:, 1:]], axis=2)
    vw = jnp.concatenate([vp[:, :-1], vp[:, 1:]], axis=2)
    scores = jnp.einsum("bnqhgd,bnkhd->bhgnqk", qb, kw).astype(jnp.float32) * (D ** -0.5)
    i = jnp.arange(A_BLOCK)[:, None]
    j = jnp.arange(2 * A_BLOCK)[None, :]
    diff = i - j + A_BLOCK
    band = (diff >= 0) & (diff < WINDOW)
    kpos = (jnp.arange(nb)[:, None, None] - 1) * A_BLOCK + j[None]
    mask = band[None] & (kpos >= 0)
    scores = jnp.where(mask, scores, -jnp.inf)
    sink = jnp.broadcast_to(
        sinks.astype(jnp.float32).reshape(1, A_KV_HEADS, grp, 1, 1, 1),
        scores.shape[:-1] + (1,))
    probs = jax.nn.softmax(jnp.concatenate([scores, sink], axis=-1), axis=-1)[..., :-1]
    out = jnp.einsum("bhgnqk,bnkhd->bnqhgd", probs.astype(v.dtype), vw)
    return out.reshape(B, S, A_HEADS * D)


def rg_lru(x, w_a, b_a, w_x, b_x, lam):
    B, S, _ = x.shape
    xb = x.reshape(B, S, R_BLOCKS, R_BLOCK_DIM)
    r = jax.nn.sigmoid(jnp.einsum("bsnc,ncd->bsnd", xb, w_a).reshape(B, S, R_WIDTH) + b_a)
    ig = jax.nn.sigmoid(jnp.einsum("bsnc,ncd->bsnd", xb, w_x).reshape(B, S, R_WIDTH) + b_x)
    log_a = -R_C * r.astype(jnp.float32) * jax.nn.softplus(-lam.astype(jnp.float32))
    a = jnp.exp(log_a)
    u = jnp.sqrt(-jnp.expm1(2.0 * log_a)) * (ig * x).astype(jnp.float32)

    def combine(left, right):
        a1, b1 = left
        a2, b2 = right
        return a1 * a2, a2 * b1 + b2

    _, h = lax.associative_scan(combine, (a, u), axis=1)
    return h.astype(x.dtype)


def gated_delta_chunked(q, k, v, g, beta):
    B, S, H, Dk = q.shape
    Dv = v.shape[-1]
    N = S // G_CHUNK
    C = G_CHUNK

    def chunks(t):
        return t.reshape(B, N, C, H, -1).transpose(0, 3, 1, 2, 4)

    q, k, v = chunks(q), chunks(k), chunks(v)
    g = jnp.cumsum(g.reshape(B, N, C, H).transpose(0, 3, 1, 2), axis=-1)
    beta = beta.reshape(B, N, C, H).transpose(0, 3, 1, 2)
    tril = jnp.tril(jnp.ones((C, C), dtype=bool))
    strict = jnp.tril(jnp.ones((C, C), dtype=bool), -1)
    decay = jnp.exp(jnp.where(tril, g[..., :, None] - g[..., None, :], -jnp.inf))
    kb = k * beta[..., None]
    vb = v * beta[..., None]
    m = jnp.where(strict, jnp.einsum("bhncd,bhnjd->bhncj", kb, k) * decay, 0.0)
    lhs = jnp.eye(C, dtype=jnp.float32) + m
    u = lax.linalg.triangular_solve(lhs, vb, left_side=True, lower=True, unit_diagonal=True)
    w = lax.linalg.triangular_solve(lhs, kb * jnp.exp(g)[..., None], left_side=True,
                                    lower=True, unit_diagonal=True)
    qk = jnp.where(tril, jnp.einsum("bhncd,bhnjd->bhncj", q, k) * decay, 0.0)
    q_dec = q * jnp.exp(g)[..., None]
    k_dec = k * jnp.exp(g[..., -1:] - g)[..., None]
    g_last = jnp.exp(g[..., -1])

    def step(state, inp):
        qk_i, qd_i, kd_i, u_i, w_i, gl_i = inp
        v_new = u_i - jnp.einsum("bhcd,bhde->bhce", w_i, state)
        o = jnp.einsum("bhcd,bhde->bhce", qd_i, state) + jnp.einsum("bhcj,bhje->bhce", qk_i, v_new)
        state = state * gl_i[..., None, None] + jnp.einsum("bhcd,bhce->bhde", kd_i, v_new)
        return state, o

    xs = tuple(jnp.moveaxis(t, 2, 0) for t in (qk, q_dec, k_dec, u, w, g_last))
    state0 = jnp.zeros((B, H, Dk, Dv), jnp.float32)
    _, o = lax.scan(step, state0, xs)
    return o.transpose(1, 0, 3, 2, 4).reshape(B, S, H, Dv)


def l2norm(x):
    return x * lax.rsqrt(jnp.sum(x * x, -1, keepdims=True) + RMS_EPS)


def hybrid_layer(x, cos, sin, w_in, sinks, r_conv_w, r_conv_b, r_wa, r_ba, r_wx, r_bx, r_lam,
                 g_conv_w, g_a_log, g_dt_bias, g_norm_w, w_out, ln_g, ln_b):
    B, S, _ = x.shape
    proj = x @ w_in
    points = [int(p) for p in np.cumsum(IN_SIZES)[:-1]]
    aq, ak, av, az, rx, rz, gq, gk, gv, gz, gb, ga = jnp.split(proj, points, axis=-1)

    q = apply_rope(aq.reshape(B, S, A_HEADS, A_HEAD_DIM), cos, sin)
    k = apply_rope(ak.reshape(B, S, A_KV_HEADS, A_HEAD_DIM), cos, sin)
    v = av.reshape(B, S, A_KV_HEADS, A_HEAD_DIM)
    y_a = sliding_window_attention(q, k, v, sinks) * jax.nn.silu(az)

    xr = causal_depthwise_conv(rx, r_conv_w) + r_conv_b
    y_r = rg_lru(xr, r_wa, r_ba, r_wx, r_bx, r_lam) * jax.nn.silu(rz)

    qkv = jax.nn.silu(causal_depthwise_conv(jnp.concatenate([gq, gk, gv], axis=-1), g_conv_w))
    cq, ck, cv = jnp.split(qkv.astype(jnp.float32), 3, axis=-1)
    cq = l2norm(cq.reshape(B, S, G_HEADS, G_HEAD_DIM)) * (G_HEAD_DIM ** -0.5)
    ck = l2norm(ck.reshape(B, S, G_HEADS, G_HEAD_DIM))
    cv = cv.reshape(B, S, G_HEADS, G_HEAD_DIM)
    beta = jax.nn.sigmoid(gb.astype(jnp.float32))
    g = -jnp.exp(g_a_log.astype(jnp.float32)) * jax.nn.softplus(
        ga.astype(jnp.float32) + g_dt_bias.astype(jnp.float32))
    o = gated_delta_chunked(cq, ck, cv, g, beta)
    o = o * lax.rsqrt(jnp.mean(o * o, -1, keepdims=True) + RMS_EPS) * g_norm_w.astype(jnp.float32)
    y_g = o.reshape(B, S, G_WIDTH).astype(x.dtype) * jax.nn.silu(gz)

    y = jnp.concatenate([y_a, y_r, y_g], axis=-1) @ w_out
    return layer_norm(DEEPNORM_ALPHA * x + y, ln_g, ln_b)


def _fwd_setup_inputs(seed: int = 0) -> dict:
    key = jax.random.key(seed)
    ks = jax.random.split(key, 20)
    f32 = jnp.float32
    x = jax.random.normal(ks[0], (BATCH, SEQ, D_MODEL), f32)
    w_in = jax.random.normal(ks[1], (DEPTH, D_MODEL, N_IN), f32) * D_MODEL ** -0.5
    sinks = jax.random.normal(ks[2], (DEPTH, A_HEADS), f32)
    r_conv_w = jax.random.normal(ks[3], (DEPTH, CONV_WIDTH, R_WIDTH), f32) * CONV_WIDTH ** -0.5
    r_conv_b = jax.random.normal(ks[4], (DEPTH, R_WIDTH), f32) * 0.01
    r_wa = jax.random.normal(ks[5], (DEPTH, R_BLOCKS, R_BLOCK_DIM, R_BLOCK_DIM), f32) * R_BLOCK_DIM ** -0.5
    r_ba = jax.random.normal(ks[6], (DEPTH, R_WIDTH), f32) * 0.01
    r_wx = jax.random.normal(ks[7], (DEPTH, R_BLOCKS, R_BLOCK_DIM, R_BLOCK_DIM), f32) * R_BLOCK_DIM ** -0.5
    r_bx = jax.random.normal(ks[8], (DEPTH, R_WIDTH), f32) * 0.01
    a_c = jax.random.uniform(ks[9], (DEPTH, R_WIDTH), f32, minval=0.9, maxval=0.999)
    a0 = a_c ** (1.0 / R_C)
    r_lam = jnp.log(a0) - jnp.log1p(-a0)
    g_conv_w = jax.random.normal(ks[10], (DEPTH, CONV_WIDTH, 3 * G_WIDTH), f32) * CONV_WIDTH ** -0.5
    g_a_log = jnp.log(jax.random.uniform(ks[11], (DEPTH, G_HEADS), f32, minval=1.0, maxval=16.0))
    dt = jnp.exp(jax.random.uniform(ks[12], (DEPTH, G_HEADS), f32,
                                    minval=math.log(1e-3), maxval=math.log(1e-1)))
    g_dt_bias = dt + jnp.log(-jnp.expm1(-dt))
    g_norm_w = 1.0 + 0.01 * jax.random.normal(ks[13], (DEPTH, G_HEAD_DIM), f32)
    w_out = jax.random.normal(ks[14], (DEPTH, MIX_WIDTH, D_MODEL), f32) * (MIX_WIDTH ** -0.5) * DEEPNORM_BETA
    ln_g = 1.0 + 0.01 * jax.random.normal(ks[15], (DEPTH, D_MODEL), f32)
    ln_b = 0.01 * jax.random.normal(ks[16], (DEPTH, D_MODEL), f32)
    return {"x": x, "w_in": w_in, "sinks": sinks, "r_conv_w": r_conv_w, "r_conv_b": r_conv_b,
            "r_wa": r_wa, "r_ba": r_ba, "r_wx": r_wx, "r_bx": r_bx, "r_lam": r_lam,
            "g_conv_w": g_conv_w, "g_a_log": g_a_log, "g_dt_bias": g_dt_bias, "g_norm_w": g_norm_w,
            "w_out": w_out, "ln_g": ln_g, "ln_b": ln_b}


def _fwd_reference(x, w_in, sinks, r_conv_w, r_conv_b, r_wa, r_ba, r_wx, r_bx, r_lam,
              g_conv_w, g_a_log, g_dt_bias, g_norm_w, w_out, ln_g, ln_b):
    cos, sin = rope_tables(x.shape[1], A_HEAD_DIM)
    for l in range(DEPTH):
        x = hybrid_layer(x, cos, sin, w_in[l], sinks[l], r_conv_w[l], r_conv_b[l], r_wa[l], r_ba[l],
                         r_wx[l], r_bx[l], r_lam[l], g_conv_w[l], g_a_log[l], g_dt_bias[l],
                         g_norm_w[l], w_out[l], ln_g[l], ln_b[l])
    return x


import jax as _jax
import jax.numpy as _jnp

TWIN_FORMAT = 'train_step'
FWD_PARAMS = ['x', 'w_in', 'sinks', 'r_conv_w', 'r_conv_b', 'r_wa', 'r_ba', 'r_wx', 'r_bx', 'r_lam', 'g_conv_w', 'g_a_log', 'g_dt_bias', 'g_norm_w', 'w_out', 'ln_g', 'ln_b']
TWIN_WEIGHTS = ['w_in', 'sinks', 'r_conv_w', 'r_conv_b', 'r_wa', 'r_ba', 'r_wx', 'r_bx', 'r_lam', 'g_conv_w', 'g_a_log', 'g_dt_bias', 'g_norm_w', 'w_out', 'ln_g', 'ln_b']
TWIN_DIFF_INPUT = 'x'
TWIN_INPUTS = ['x', 'w_in', 'sinks', 'r_conv_w', 'r_conv_b', 'r_wa', 'r_ba', 'r_wx', 'r_bx', 'r_lam', 'g_conv_w', 'g_a_log', 'g_dt_bias', 'g_norm_w', 'w_out', 'ln_g', 'ln_b', 'loss_target', 'm_w_in', 'm_sinks', 'm_r_conv_w', 'm_r_conv_b', 'm_r_wa', 'm_r_ba', 'm_r_wx', 'm_r_bx', 'm_r_lam', 'm_g_conv_w', 'm_g_a_log', 'm_g_dt_bias', 'm_g_norm_w', 'm_w_out', 'm_ln_g', 'm_ln_b', 'v_w_in', 'v_sinks', 'v_r_conv_w', 'v_r_conv_b', 'v_r_wa', 'v_r_ba', 'v_r_wx', 'v_r_bx', 'v_r_lam', 'v_g_conv_w', 'v_g_a_log', 'v_g_dt_bias', 'v_g_norm_w', 'v_w_out', 'v_ln_g', 'v_ln_b']
TWIN_OUTPUTS = ['loss', 'grad_x', 'grad_w_in', 'grad_sinks', 'grad_r_conv_w', 'grad_r_conv_b', 'grad_r_wa', 'grad_r_ba', 'grad_r_wx', 'grad_r_bx', 'grad_r_lam', 'grad_g_conv_w', 'grad_g_a_log', 'grad_g_dt_bias', 'grad_g_norm_w', 'grad_w_out', 'grad_ln_g', 'grad_ln_b', 'delta_w_in', 'delta_sinks', 'delta_r_conv_w', 'delta_r_conv_b', 'delta_r_wa', 'delta_r_ba', 'delta_r_wx', 'delta_r_bx', 'delta_r_lam', 'delta_g_conv_w', 'delta_g_a_log', 'delta_g_dt_bias', 'delta_g_norm_w', 'delta_w_out', 'delta_ln_g', 'delta_ln_b', 'new_m_w_in', 'new_m_sinks', 'new_m_r_conv_w', 'new_m_r_conv_b', 'new_m_r_wa', 'new_m_r_ba', 'new_m_r_wx', 'new_m_r_bx', 'new_m_r_lam', 'new_m_g_conv_w', 'new_m_g_a_log', 'new_m_g_dt_bias', 'new_m_g_norm_w', 'new_m_w_out', 'new_m_ln_g', 'new_m_ln_b', 'new_v_w_in', 'new_v_sinks', 'new_v_r_conv_w', 'new_v_r_conv_b', 'new_v_r_wa', 'new_v_r_ba', 'new_v_r_wx', 'new_v_r_bx', 'new_v_r_lam', 'new_v_g_conv_w', 'new_v_g_a_log', 'new_v_g_dt_bias', 'new_v_g_norm_w', 'new_v_w_out', 'new_v_ln_g', 'new_v_ln_b']
TWIN_LEAF_KINDS = {'loss': 'loss', 'grad_x': 'grad_x', 'grad_w_in': 'grad_w', 'grad_sinks': 'grad_w', 'grad_r_conv_w': 'grad_w', 'grad_r_conv_b': 'grad_w', 'grad_r_wa': 'grad_w', 'grad_r_ba': 'grad_w', 'grad_r_wx': 'grad_w', 'grad_r_bx': 'grad_w', 'grad_r_lam': 'grad_w', 'grad_g_conv_w': 'grad_w', 'grad_g_a_log': 'grad_w', 'grad_g_dt_bias': 'grad_w', 'grad_g_norm_w': 'grad_w', 'grad_w_out': 'grad_w', 'grad_ln_g': 'grad_w', 'grad_ln_b': 'grad_w', 'delta_w_in': 'delta_w', 'delta_sinks': 'delta_w', 'delta_r_conv_w': 'delta_w', 'delta_r_conv_b': 'delta_w', 'delta_r_wa': 'delta_w', 'delta_r_ba': 'delta_w', 'delta_r_wx': 'delta_w', 'delta_r_bx': 'delta_w', 'delta_r_lam': 'delta_w', 'delta_g_conv_w': 'delta_w', 'delta_g_a_log': 'delta_w', 'delta_g_dt_bias': 'delta_w', 'delta_g_norm_w': 'delta_w', 'delta_w_out': 'delta_w', 'delta_ln_g': 'delta_w', 'delta_ln_b': 'delta_w', 'new_m_w_in': 'new_m', 'new_m_sinks': 'new_m', 'new_m_r_conv_w': 'new_m', 'new_m_r_conv_b': 'new_m', 'new_m_r_wa': 'new_m', 'new_m_r_ba': 'new_m', 'new_m_r_wx': 'new_m', 'new_m_r_bx': 'new_m', 'new_m_r_lam': 'new_m', 'new_m_g_conv_w': 'new_m', 'new_m_g_a_log': 'new_m', 'new_m_g_dt_bias': 'new_m', 'new_m_g_norm_w': 'new_m', 'new_m_w_out': 'new_m', 'new_m_ln_g': 'new_m', 'new_m_ln_b': 'new_m', 'new_v_w_in': 'new_v', 'new_v_sinks': 'new_v', 'new_v_r_conv_w': 'new_v', 'new_v_r_conv_b': 'new_v', 'new_v_r_wa': 'new_v', 'new_v_r_ba': 'new_v', 'new_v_r_wx': 'new_v', 'new_v_r_bx': 'new_v', 'new_v_r_lam': 'new_v', 'new_v_g_conv_w': 'new_v', 'new_v_g_a_log': 'new_v', 'new_v_g_dt_bias': 'new_v', 'new_v_g_norm_w': 'new_v', 'new_v_w_out': 'new_v', 'new_v_ln_g': 'new_v', 'new_v_ln_b': 'new_v'}


def _forward(args):
    return _fwd_reference(*[args[k] for k in FWD_PARAMS])


def _output_shape():
    def fwd():
        inp = _fwd_setup_inputs(0)
        return _fwd_reference(*[inp[k] for k in FWD_PARAMS])
    out = _jax.eval_shape(fwd)
    return out.shape, out.dtype

N_MICROBATCH = 1
ADAM_LR = 0.001
ADAM_B1 = 0.9
ADAM_B2 = 0.999
ADAM_EPS = 1e-08
ADAM_WD = 0.01
ADAM_STEP = 10
PER_EXAMPLE_BATCH_AXIS = {'x': 0, 'loss_target': 0}
SHARED_INPUTS = []
_WEIGHT_DTYPES = {'w_in': _jnp.float32, 'sinks': _jnp.float32, 'r_conv_w': _jnp.float32, 'r_conv_b': _jnp.float32, 'r_wa': _jnp.float32, 'r_ba': _jnp.float32, 'r_wx': _jnp.float32, 'r_bx': _jnp.float32, 'r_lam': _jnp.float32, 'g_conv_w': _jnp.float32, 'g_a_log': _jnp.float32, 'g_dt_bias': _jnp.float32, 'g_norm_w': _jnp.float32, 'w_out': _jnp.float32, 'ln_g': _jnp.float32, 'ln_b': _jnp.float32}
MOMENT_SCALE = {'w_in': 1.714940e-02, 'sinks': 3.827522e-03, 'r_conv_w': 1.897005e-02, 'r_conv_b': 2.459237e-01, 'r_wa': 6.610676e-03, 'r_ba': 5.598429e-03, 'r_wx': 1.183145e-02, 'r_bx': 6.459921e-03, 'r_lam': 1.020981e-02, 'g_conv_w': 1.967762e-02, 'g_a_log': 1.404078e-01, 'g_dt_bias': 1.296810e-01, 'g_norm_w': 5.230345e-02, 'w_out': 3.783241e-02, 'ln_g': 2.259868e+01, 'ln_b': 7.865197e-01}


def _to_microbatches(a, axis):
    t = _jnp.moveaxis(a, axis, 0)
    t = t.reshape((N_MICROBATCH, t.shape[0] // N_MICROBATCH) + t.shape[1:])
    return _jnp.moveaxis(t, 1, axis + 1)


def setup_inputs(seed: int = 0) -> dict:
    inp = _fwd_setup_inputs(seed)
    key = _jax.random.fold_in(_jax.random.key(seed), 7919)
    shape, _ = _output_shape()
    out = dict(inp)
    out["loss_target"] = _jax.random.normal(_jax.random.fold_in(key, 0), shape, _jnp.float32)
    for i, name in enumerate(TWIN_WEIGHTS):
        w = inp[name].astype(_jnp.float32)
        if MOMENT_SCALE is None:
            s = _jnp.sqrt(_jnp.mean(_jnp.square(w)) + 1e-30)
        else:
            s = MOMENT_SCALE[name]
        km, kv = _jax.random.split(_jax.random.fold_in(key, i + 1))
        out[name] = w
        out["m_" + name] = s * _jax.random.normal(km, w.shape, _jnp.float32)
        out["v_" + name] = (s * s) * _jax.random.uniform(kv, w.shape, _jnp.float32, 0.5, 1.5)
    if N_MICROBATCH > 1:
        for name, axis in PER_EXAMPLE_BATCH_AXIS.items():
            out[name] = _to_microbatches(out[name], axis)
    return {'x': out['x'], 'w_in': out['w_in'], 'sinks': out['sinks'], 'r_conv_w': out['r_conv_w'], 'r_conv_b': out['r_conv_b'], 'r_wa': out['r_wa'], 'r_ba': out['r_ba'], 'r_wx': out['r_wx'], 'r_bx': out['r_bx'], 'r_lam': out['r_lam'], 'g_conv_w': out['g_conv_w'], 'g_a_log': out['g_a_log'], 'g_dt_bias': out['g_dt_bias'], 'g_norm_w': out['g_norm_w'], 'w_out': out['w_out'], 'ln_g': out['ln_g'], 'ln_b': out['ln_b'], 'loss_target': out['loss_target'], 'm_w_in': out['m_w_in'], 'm_sinks': out['m_sinks'], 'm_r_conv_w': out['m_r_conv_w'], 'm_r_conv_b': out['m_r_conv_b'], 'm_r_wa': out['m_r_wa'], 'm_r_ba': out['m_r_ba'], 'm_r_wx': out['m_r_wx'], 'm_r_bx': out['m_r_bx'], 'm_r_lam': out['m_r_lam'], 'm_g_conv_w': out['m_g_conv_w'], 'm_g_a_log': out['m_g_a_log'], 'm_g_dt_bias': out['m_g_dt_bias'], 'm_g_norm_w': out['m_g_norm_w'], 'm_w_out': out['m_w_out'], 'm_ln_g': out['m_ln_g'], 'm_ln_b': out['m_ln_b'], 'v_w_in': out['v_w_in'], 'v_sinks': out['v_sinks'], 'v_r_conv_w': out['v_r_conv_w'], 'v_r_conv_b': out['v_r_conv_b'], 'v_r_wa': out['v_r_wa'], 'v_r_ba': out['v_r_ba'], 'v_r_wx': out['v_r_wx'], 'v_r_bx': out['v_r_bx'], 'v_r_lam': out['v_r_lam'], 'v_g_conv_w': out['v_g_conv_w'], 'v_g_a_log': out['v_g_a_log'], 'v_g_dt_bias': out['v_g_dt_bias'], 'v_g_norm_w': out['v_g_norm_w'], 'v_w_out': out['v_w_out'], 'v_ln_g': out['v_ln_g'], 'v_ln_b': out['v_ln_b']}


def _loss(weights, diff, rest, loss_target):
    with _jax.named_scope("forward"):
        args = {**rest, TWIN_DIFF_INPUT: diff, **{k: w.astype(_WEIGHT_DTYPES[k]) for k, w in weights.items()}}
        y = _forward(args)
    with _jax.named_scope("loss_head"):
        err = _jnp.square(y.astype(_jnp.float32) - loss_target)
        return 0.5 * _jnp.sum(_jnp.mean(err, axis=-1)) if err.ndim else 0.5 * err


def _adamw(w, g, m, v):
    m = ADAM_B1 * m + (1.0 - ADAM_B1) * g
    v = ADAM_B2 * v + (1.0 - ADAM_B2) * _jnp.square(g)
    m_hat = m / (1.0 - ADAM_B1 ** ADAM_STEP)
    v_hat = v / (1.0 - ADAM_B2 ** ADAM_STEP)
    delta = -ADAM_LR * (m_hat / (_jnp.sqrt(v_hat) + ADAM_EPS) + ADAM_WD * w)
    return delta, m, v


def reference(x, w_in, sinks, r_conv_w, r_conv_b, r_wa, r_ba, r_wx, r_bx, r_lam, g_conv_w, g_a_log, g_dt_bias, g_norm_w, w_out, ln_g, ln_b, loss_target, m_w_in, m_sinks, m_r_conv_w, m_r_conv_b, m_r_wa, m_r_ba, m_r_wx, m_r_bx, m_r_lam, m_g_conv_w, m_g_a_log, m_g_dt_bias, m_g_norm_w, m_w_out, m_ln_g, m_ln_b, v_w_in, v_sinks, v_r_conv_w, v_r_conv_b, v_r_wa, v_r_ba, v_r_wx, v_r_bx, v_r_lam, v_g_conv_w, v_g_a_log, v_g_dt_bias, v_g_norm_w, v_w_out, v_ln_g, v_ln_b):
    given = dict(x=x, w_in=w_in, sinks=sinks, r_conv_w=r_conv_w, r_conv_b=r_conv_b, r_wa=r_wa, r_ba=r_ba, r_wx=r_wx, r_bx=r_bx, r_lam=r_lam, g_conv_w=g_conv_w, g_a_log=g_a_log, g_dt_bias=g_dt_bias, g_norm_w=g_norm_w, w_out=w_out, ln_g=ln_g, ln_b=ln_b, loss_target=loss_target, m_w_in=m_w_in, m_sinks=m_sinks, m_r_conv_w=m_r_conv_w, m_r_conv_b=m_r_conv_b, m_r_wa=m_r_wa, m_r_ba=m_r_ba, m_r_wx=m_r_wx, m_r_bx=m_r_bx, m_r_lam=m_r_lam, m_g_conv_w=m_g_conv_w, m_g_a_log=m_g_a_log, m_g_dt_bias=m_g_dt_bias, m_g_norm_w=m_g_norm_w, m_w_out=m_w_out, m_ln_g=m_ln_g, m_ln_b=m_ln_b, v_w_in=v_w_in, v_sinks=v_sinks, v_r_conv_w=v_r_conv_w, v_r_conv_b=v_r_conv_b, v_r_wa=v_r_wa, v_r_ba=v_r_ba, v_r_wx=v_r_wx, v_r_bx=v_r_bx, v_r_lam=v_r_lam, v_g_conv_w=v_g_conv_w, v_g_a_log=v_g_a_log, v_g_dt_bias=v_g_dt_bias, v_g_norm_w=v_g_norm_w, v_w_out=v_w_out, v_ln_g=v_ln_g, v_ln_b=v_ln_b)
    weights = {n: given[n] for n in TWIN_WEIGHTS}
    shared = {n: given[n] for n in SHARED_INPUTS}
    per_example = {n: given[n] for n in ['x']}
    grad_fn = _jax.value_and_grad(_loss, argnums=(0, 1))

    def one_microbatch(ex, loss_target):
        ex = dict(ex)
        diff = ex.pop(TWIN_DIFF_INPUT)
        return grad_fn(weights, diff, {**shared, **ex}, loss_target)

    if N_MICROBATCH == 1:
        loss, (grad_w, grad_x) = one_microbatch(per_example, given["loss_target"])
    else:
        def body(carry, xs):
            loss_sum, grad_sum = carry
            l_k, (gw_k, gx_k) = one_microbatch(xs[0], xs[1])
            with _jax.named_scope("update"):
                return (loss_sum + l_k, _jax.tree.map(_jnp.add, grad_sum, gw_k)), gx_k

        init = (_jnp.zeros((), _jnp.float32), _jax.tree.map(_jnp.zeros_like, weights))
        (loss, grad_w), grad_x = _jax.lax.scan(body, init, (per_example, given["loss_target"]))
    with _jax.named_scope("update"):
        delta_w, new_m, new_v = {}, {}, {}
        for n in TWIN_WEIGHTS:
            delta_w[n], new_m[n], new_v[n] = _adamw(weights[n], grad_w[n], given["m_" + n], given["v_" + n])
    return (loss, grad_x, *[grad_w[n] for n in TWIN_WEIGHTS], *[delta_w[n] for n in TWIN_WEIGHTS],
            *[new_m[n] for n in TWIN_WEIGHTS], *[new_v[n] for n in TWIN_WEIGHTS])
```

```python
import functools
import math

import jax
import jax.numpy as jnp
from jax import lax
from jax.experimental import pallas as pl
from jax.experimental.pallas import tpu as pltpu

F32 = jnp.float32
MXU_DTYPE = jnp.bfloat16
HIGHEST = lax.Precision.HIGHEST
MESH_ID = pl.DeviceIdType.MESH

N_DEV = 8
DEPTH = 2
D_MODEL = 2048
A_HEADS, A_KV_HEADS, A_HEAD_DIM = 8, 2, 64
A_WIDTH, A_KV_WIDTH = 512, 128
A_BLOCK = 128
ROPE_THETA = 10000.0
R_WIDTH, R_BLOCKS, R_BLOCK_DIM = 1024, 8, 128
R_C = 8.0
CONV_WIDTH = 4
G_HEADS, G_HEAD_DIM, G_WIDTH, G_CHUNK = 4, 128, 512, 64
N_IN = 5384
N_IN_SHARD = N_IN // N_DEV
OUT_SHARD = D_MODEL // N_DEV
WA, WR, WG = 1280, 2048, 2304
G_PAD = WG - (3 * G_WIDTH + G_WIDTH + 2 * G_HEADS)
DEEPNORM_ALPHA = (2 * DEPTH) ** 0.25
LN_EPS = 1e-5
RMS_EPS = 1e-6
ADAM_LR, ADAM_B1, ADAM_B2, ADAM_EPS, ADAM_WD, ADAM_STEP = 0.001, 0.9, 0.999, 1e-08, 0.01, 10
NEG = -1e30
VMEM_LIMIT = 56 * 1024 * 1024
LANES = 128


def _call(body, **kw):
    return pl.pallas_call(body, **kw)


def _params(*sem):
    return pltpu.CompilerParams(dimension_semantics=sem, vmem_limit_bytes=VMEM_LIMIT)


def _t(x):
    return jnp.swapaxes(x, -1, -2)


def _raw_dot(a, b, ca, cb, precision=None):
    batch = tuple(range(a.ndim - 2))
    if precision is None:
        a, b = a.astype(MXU_DTYPE), b.astype(MXU_DTYPE)
    return lax.dot_general(a, b, (((ca,), (cb,)), (batch, batch)), precision=precision,
                           preferred_element_type=F32)


def _nn(a, b, precision=None):
    return _raw_dot(a, b, a.ndim - 1, b.ndim - 2, precision)


def _nt(a, b, precision=None):
    return _raw_dot(a, b, a.ndim - 1, b.ndim - 1, precision)


@jax.custom_vjp
def mm_nn(a, b):
    return _nn(a, b)


def _mm_nn_fwd(a, b):
    return _nn(a, b), (a, b)


def _mm_nn_bwd(res, g):
    a, b = res
    return _nt(g, b), _nn(_t(a), g)


mm_nn.defvjp(_mm_nn_fwd, _mm_nn_bwd)


@jax.custom_vjp
def mm_nt(a, b):
    return _nt(a, b)


def _mm_nt_fwd(a, b):
    return _nt(a, b), (a, b)


def _mm_nt_bwd(res, g):
    a, b = res
    return _nn(g, b), _nn(_t(g), a)


mm_nt.defvjp(_mm_nt_fwd, _mm_nt_bwd)


def _hdot(a, b):
    return _nn(a, b, HIGHEST)


def _silu(x):
    return x * jax.nn.sigmoid(x)


def _softplus(x):
    return jnp.maximum(x, 0.0) + jnp.log1p(jnp.exp(-jnp.abs(x)))


def _neg_expm1(x):
    series = -x * (1.0 + x * 0.5 * (1.0 + x * (1.0 / 3.0) * (1.0 + x * 0.25 * (1.0 + x * 0.2 * (
        1.0 + x * (1.0 / 6.0) * (1.0 + x * (1.0 / 7.0)))))))
    return jnp.where(x > -0.25, series, 1.0 - jnp.exp(x))


def _iota(shape, dim):
    return lax.broadcasted_iota(jnp.int32, shape, dim)


def _inv_unit_lower(m):
    shape = m.shape
    row, col = _iota(shape, 1), _iota(shape, 2)
    eye = (row == col).astype(F32)

    def blockdiag(size):
        return (row // size) == (col // size)

    x = -jnp.where(blockdiag(8), m, 0.0)
    x2 = _hdot(x, x)
    x4 = _hdot(x2, x2)
    inv = eye + x
    inv = inv + _hdot(inv, x2)
    inv = inv + _hdot(inv, x4)
    for size in (8, 16, 32):
        below = jnp.where(blockdiag(2 * size) & jnp.logical_not(blockdiag(size)), m, 0.0)
        inv = inv - _hdot(_hdot(inv, below), inv)
    return inv


@jax.custom_vjp
def _solve2(m, r1, r2):
    inv = _inv_unit_lower(m)
    return _hdot(inv, r1), _hdot(inv, r2)


def _solve2_fwd(m, r1, r2):
    inv = _inv_unit_lower(m)
    x1, x2 = _hdot(inv, r1), _hdot(inv, r2)
    return (x1, x2), (inv, x1, x2)


def _solve2_bwd(res, g):
    inv, x1, x2 = res
    g1, g2 = g
    inv_t = _t(inv)
    d1, d2 = _hdot(inv_t, g1), _hdot(inv_t, g2)
    dm = -(_nt(d1, x1, HIGHEST) + _nt(d2, x2, HIGHEST))
    return dm, d1, d2


_solve2.defvjp(_solve2_fwd, _solve2_bwd)


def _swap_halves(x):
    n = x.shape[-1]
    lane = _iota(x.shape, x.ndim - 1)
    return jnp.where((lane & 63) < 32, pltpu.roll(x, n - 32, x.ndim - 1), pltpu.roll(x, 32, x.ndim - 1))


def _rope(x, cos, sin):
    reps = x.shape[-1] // LANES
    if reps > 1:
        cos, sin = jnp.tile(cos, (1, reps)), jnp.tile(sin, (1, reps))
    return x * cos + _swap_halves(x) * sin


def _rope_t(d, cos, sin):
    reps = d.shape[-1] // LANES
    if reps > 1:
        cos, sin = jnp.tile(cos, (1, reps)), jnp.tile(sin, (1, reps))
    return d * cos + _swap_halves(d * sin)


def _swap64(x):
    return pltpu.roll(x, 64, x.ndim - 1)


def _mesh_pos():
    return lax.axis_index("x"), lax.axis_index("y"), lax.axis_index("c")


def _all_gather(arrays, name):
    n = len(arrays)

    def body(*refs):
        ins, outs = refs[:n], refs[n:2 * n]
        send_sems, recv_sems, local_sem = refs[2 * n:]
        x, y, c = _mesh_pos()
        me, sibling = (x, y, c), (x, y, 1 - c)
        chips = [(1 - x, y), (x, 1 - y), (1 - x, 1 - y)]

        def slot(ref, pos):
            return ref.at[4 * pos[0] + 2 * pos[1] + pos[2]]

        def copy(a, k, block, to, src=None):
            return pltpu.make_async_remote_copy(
                src_ref=slot(outs[a], block) if src is None else src, dst_ref=slot(outs[a], block),
                send_sem=send_sems.at[a, k], recv_sem=recv_sems.at[a, k], device_id=to, device_id_type=MESH_ID)

        mine = [pltpu.make_async_copy(ins[a], slot(outs[a], me), local_sem.at[a]) for a in range(n)]
        for cp in mine:
            cp.start()
        first = []
        for a in range(n):
            first.append(copy(a, 0, me, sibling, src=ins[a]))
            first += [copy(a, 1 + j, me, (*chip, c), src=ins[a]) for j, chip in enumerate(chips)]
        for cp in first:
            cp.start()
        passed = []
        for a in range(n):
            for j, chip in enumerate(chips):
                copy(a, 1 + j, (*chip, c), me).wait_recv()
                cp = copy(a, 4 + j, (*chip, c), sibling)
                cp.start()
                passed.append(cp)
        for a in range(n):
            copy(a, 0, sibling, me).wait_recv()
            for j, chip in enumerate(chips):
                copy(a, 4 + j, (*chip, 1 - c), me).wait_recv()
        for cp in first + passed:
            cp.wait_send()
        for cp in mine:
            cp.wait()

    any_spec = pl.BlockSpec(memory_space=pl.ANY)
    return _call(
        body, name=name,
        out_shape=[jax.ShapeDtypeStruct((N_DEV,) + a.shape, a.dtype) for a in arrays],
        in_specs=[any_spec] * n, out_specs=[any_spec] * n,
        scratch_shapes=[pltpu.SemaphoreType.DMA((n, 7)), pltpu.SemaphoreType.DMA((n, 7)),
                        pltpu.SemaphoreType.DMA((n,))],
    )(*arrays)


def _exchange(arrays, name):
    n = len(arrays)

    def body(*refs):
        ins, outs = refs[:n], refs[n:2 * n]
        send_sems, recv_sems, local_sem = refs[2 * n:]
        x, y, c = _mesh_pos()
        me = 4 * x + 2 * y + c
        mine = [pltpu.make_async_copy(ins[a].at[me], outs[a].at[0], local_sem.at[a]) for a in range(n)]
        for cp in mine:
            cp.start()
        copies = []
        for a in range(n):
            for k in range(1, N_DEV):
                px, py, pc = x ^ (k >> 2), y ^ ((k >> 1) & 1), c ^ (k & 1)
                peer = 4 * px + 2 * py + pc
                copies.append(pltpu.make_async_remote_copy(
                    src_ref=ins[a].at[peer], dst_ref=outs[a].at[k], send_sem=send_sems.at[a, k - 1],
                    recv_sem=recv_sems.at[a, k - 1], device_id=(px, py, pc), device_id_type=MESH_ID))
        for cp in copies:
            cp.start()
        for cp in copies:
            cp.wait_recv()
        for cp in copies:
            cp.wait_send()
        for cp in mine:
            cp.wait()

    any_spec = pl.BlockSpec(memory_space=pl.ANY)
    return _call(
        body, name=name,
        out_shape=[jax.ShapeDtypeStruct(a.shape, a.dtype) for a in arrays],
        in_specs=[any_spec] * n, out_specs=[any_spec] * n,
        scratch_shapes=[pltpu.SemaphoreType.DMA((n, 7)), pltpu.SemaphoreType.DMA((n, 7)),
                        pltpu.SemaphoreType.DMA((n,))],
    )(*arrays)


def _matmul(a_list, b_list, *, name, tm, tn, out_dtype=F32, add=None, add_scale=1.0):
    n = len(a_list)
    m_rows, n_cols = a_list[0].shape[0], b_list[0].shape[1]
    tm, tn = min(tm, m_rows), min(tn, n_cols)

    def body(*refs):
        a_refs, b_refs = refs[:n], refs[n:2 * n]
        o_ref = refs[-1]
        acc = None
        for a_ref, b_ref in zip(a_refs, b_refs):
            part = jnp.dot(a_ref[...].astype(MXU_DTYPE), b_ref[...].astype(MXU_DTYPE), preferred_element_type=F32)
            acc = part if acc is None else acc + part
        if add is not None:
            acc = acc + add_scale * refs[2 * n][...]
        o_ref[...] = acc.astype(o_ref.dtype)

    in_specs = [pl.BlockSpec((tm, a.shape[1]), lambda i, j: (i, 0)) for a in a_list]
    in_specs += [pl.BlockSpec((b.shape[0], tn), lambda i, j: (0, j)) for b in b_list]
    args = list(a_list) + list(b_list)
    if add is not None:
        in_specs.append(pl.BlockSpec((tm, tn), lambda i, j: (i, j)))
        args.append(add)
    return _call(
        body, name=name, grid=(m_rows // tm, n_cols // tn), in_specs=in_specs,
        out_specs=pl.BlockSpec((tm, tn), lambda i, j: (i, j)),
        out_shape=jax.ShapeDtypeStruct((m_rows, n_cols), out_dtype),
        compiler_params=_params("parallel", "arbitrary"),
    )(*args)


def _matmul_tn(a, b, *, name, tm, tn, tk):
    k_rows, m_rows = a.shape
    n_cols = b.shape[1]
    tm, tn, tk = min(tm, m_rows), min(tn, n_cols), min(tk, k_rows)
    nk = k_rows // tk

    def body(a_ref, b_ref, o_ref):
        part = lax.dot_general(a_ref[...].astype(MXU_DTYPE), b_ref[...].astype(MXU_DTYPE),
                               (((0,), (0,)), ((), ())), preferred_element_type=F32)

        @pl.when(pl.program_id(2) == 0)
        def _():
            o_ref[...] = part

        @pl.when(pl.program_id(2) > 0)
        def _():
            o_ref[...] += part

    return _call(
        body, name=name, grid=(m_rows // tm, n_cols // tn, nk),
        in_specs=[pl.BlockSpec((tk, tm), lambda i, j, k: (k, i)), pl.BlockSpec((tk, tn), lambda i, j, k: (k, j))],
        out_specs=pl.BlockSpec((tm, tn), lambda i, j, k: (i, j)),
        out_shape=jax.ShapeDtypeStruct((m_rows, n_cols), F32),
        compiler_params=_params("parallel", "parallel", "arbitrary"),
    )(a, b)


def _outproj_ln(ya, yr, yg, w_out, x, ln_g, ln_b, *, name):
    s_len = x.shape[0]
    tm = min(256, s_len)

    def body(ya_ref, yr_ref, yg_ref, w_ref, x_ref, g_ref, b_ref, z_ref, o_ref):
        acc = jnp.dot(ya_ref[...], w_ref[0:A_WIDTH, :], preferred_element_type=F32)
        acc += jnp.dot(yr_ref[...], w_ref[A_WIDTH:A_WIDTH + R_WIDTH, :], preferred_element_type=F32)
        acc += jnp.dot(yg_ref[...], w_ref[A_WIDTH + R_WIDTH:, :], preferred_element_type=F32)
        z = DEEPNORM_ALPHA * x_ref[...] + acc
        z_ref[...] = z
        mu = jnp.mean(z, axis=-1, keepdims=True)
        zc = z - mu
        var = jnp.mean(zc * zc, axis=-1, keepdims=True)
        o_ref[...] = zc * lax.rsqrt(var + LN_EPS) * g_ref[...] + b_ref[...]

    def rows(width):
        return pl.BlockSpec((tm, width), lambda i: (i, 0))

    def whole(shape):
        return pl.BlockSpec(shape, lambda i: (0, 0))

    return _call(
        body, name=name, grid=(s_len // tm,),
        in_specs=[rows(A_WIDTH), rows(R_WIDTH), rows(G_WIDTH), whole((D_MODEL, D_MODEL)), rows(D_MODEL),
                  whole((1, D_MODEL)), whole((1, D_MODEL))],
        out_specs=[rows(D_MODEL), rows(D_MODEL)],
        out_shape=[jax.ShapeDtypeStruct((s_len, D_MODEL), F32)] * 2,
        compiler_params=_params("parallel"),
    )(ya, yr, yg, w_out, x, ln_g, ln_b)


def _ln_bwd(z, ln_g, *, name, dxn=None, xn=None, target=None):
    s_len = z.shape[0]
    tm = min(256, s_len)
    top = dxn is None

    def body(*refs):
        if top:
            z_ref, g_ref, xn_ref, t_ref, dz_ref, dg_ref, db_ref, loss_ref = refs
            err = xn_ref[...] - t_ref[...]
            dy = err * (1.0 / D_MODEL)
        else:
            z_ref, g_ref, dy_ref, dz_ref, dg_ref, db_ref = refs
            dy = dy_ref[...]
        first = pl.program_id(0) == 0

        @pl.when(first)
        def _():
            dg_ref[...] = jnp.zeros_like(dg_ref)
            db_ref[...] = jnp.zeros_like(db_ref)
            if top:
                loss_ref[...] = jnp.zeros_like(loss_ref)

        z = z_ref[...]
        mu = jnp.mean(z, axis=-1, keepdims=True)
        zc = z - mu
        rstd = lax.rsqrt(jnp.mean(zc * zc, axis=-1, keepdims=True) + LN_EPS)
        xhat = zc * rstd
        dxh = dy * g_ref[...]
        dz_ref[...] = rstd * (dxh - jnp.mean(dxh, axis=-1, keepdims=True)
                              - xhat * jnp.mean(dxh * xhat, axis=-1, keepdims=True))
        dg_ref[...] += jnp.sum(dy * xhat, axis=0, keepdims=True)
        db_ref[...] += jnp.sum(dy, axis=0, keepdims=True)
        if top:
            per_row = jnp.sum(err * err, axis=-1, keepdims=True) * (0.5 / D_MODEL)
            loss_ref[...] += jnp.sum(per_row, axis=0, keepdims=True)

    rows = pl.BlockSpec((tm, D_MODEL), lambda i: (i, 0))
    vec = pl.BlockSpec((1, D_MODEL), lambda i: (0, 0))
    in_specs = [rows, vec] + ([rows, rows] if top else [rows])
    args = [z, ln_g] + ([xn, target] if top else [dxn])
    out_specs = [rows, vec, vec]
    out_shape = [jax.ShapeDtypeStruct((s_len, D_MODEL), F32), jax.ShapeDtypeStruct((1, D_MODEL), F32),
                 jax.ShapeDtypeStruct((1, D_MODEL), F32)]
    if top:
        out_specs.append(pl.BlockSpec((1, 1), lambda i: (0, 0)))
        out_shape.append(jax.ShapeDtypeStruct((1, 1), F32))
    return _call(body, name=name, grid=(s_len // tm,), in_specs=in_specs, out_specs=out_specs,
                 out_shape=out_shape, compiler_params=_params("arbitrary"))(*args)


CONV_ROWS = 256
HALO = 8


def _shift_down(x, halo, s):
    if s == 0:
        return x
    ext = jnp.concatenate([halo, x], axis=0)
    return pltpu.roll(ext, s, 0)[HALO:, :]


def _shift_up(x, halo, s):
    if s == 0:
        return x
    ext = jnp.concatenate([x, halo], axis=0)
    return pltpu.roll(ext, ext.shape[0] - s, 0)[:x.shape[0], :]


def _conv_fwd(src, width, w, bias, *, name):
    s_len = src.shape[0]
    rows = min(CONV_ROWS, s_len)
    per = rows // HALO

    def body(x_ref, halo_ref, w_ref, b_ref, o_ref):
        x = x_ref[...]
        halo = jnp.where(pl.program_id(0) == 0, 0.0, halo_ref[...])
        acc = x * w_ref[3:4, :] + b_ref[...]
        for k in range(CONV_WIDTH - 1):
            acc += _shift_down(x, halo, 3 - k) * w_ref[k:k + 1, :]
        o_ref[...] = acc

    return _call(
        body, name=name, grid=(s_len // rows,),
        in_specs=[pl.BlockSpec((rows, width), lambda i: (i, 0)),
                  pl.BlockSpec((HALO, width), lambda i: (jnp.maximum(i * per - 1, 0), 0)),
                  pl.BlockSpec((CONV_WIDTH, width), lambda i: (0, 0)), pl.BlockSpec((1, width), lambda i: (0, 0))],
        out_specs=pl.BlockSpec((rows, width), lambda i: (i, 0)),
        out_shape=jax.ShapeDtypeStruct((s_len, width), F32),
        compiler_params=_params("parallel"),
    )(src, src, w, bias)


def _conv_bwd(dy, src, width, w, passthrough, *, name):
    s_len = src.shape[0]
    rows = min(CONV_ROWS, s_len)
    per = rows // HALO
    nblk = s_len // rows
    extra = [p.shape[1] for p in passthrough]
    total = width + sum(extra)

    def body(*refs):
        dy_ref, dyh_ref, x_ref, xh_ref, w_ref = refs[:5]
        p_refs = refs[5:5 + len(extra)]
        o_ref, dw_ref, db_ref = refs[5 + len(extra):]
        i = pl.program_id(0)

        @pl.when(i == 0)
        def _():
            dw_ref[...] = jnp.zeros_like(dw_ref)
            db_ref[...] = jnp.zeros_like(db_ref)

        dy = dy_ref[...]
        x = x_ref[...]
        dy_halo = jnp.where(i == nblk - 1, 0.0, dyh_ref[...])
        x_halo = jnp.where(i == 0, 0.0, xh_ref[...])
        dx = dy * w_ref[3:4, :]
        dw_ref[3] += jnp.sum(dy * x, axis=0, keepdims=True)
        for k in range(CONV_WIDTH - 1):
            dx += _shift_up(dy, dy_halo, 3 - k) * w_ref[k:k + 1, :]
            dw_ref[k] += jnp.sum(dy * _shift_down(x, x_halo, 3 - k), axis=0, keepdims=True)
        db_ref[...] += jnp.sum(dy, axis=0, keepdims=True)
        o_ref[:, 0:width] = dx
        off = width
        for p_ref, wd in zip(p_refs, extra):
            o_ref[:, off:off + wd] = p_ref[...]
            off += wd

    in_specs = [pl.BlockSpec((rows, width), lambda i: (i, 0)),
                pl.BlockSpec((HALO, width), lambda i: (jnp.minimum((i + 1) * per, nblk * per - 1), 0)),
                pl.BlockSpec((rows, width), lambda i: (i, 0)),
                pl.BlockSpec((HALO, width), lambda i: (jnp.maximum(i * per - 1, 0), 0)),
                pl.BlockSpec((CONV_WIDTH, width), lambda i: (0, 0))]
    in_specs += [pl.BlockSpec((rows, wd), lambda i: (i, 0)) for wd in extra]
    return _call(
        body, name=name, grid=(nblk,), in_specs=in_specs,
        out_specs=[pl.BlockSpec((rows, total), lambda i: (i, 0)),
                   pl.BlockSpec((CONV_WIDTH, 1, width), lambda i: (0, 0, 0)), pl.BlockSpec((1, width), lambda i: (0, 0))],
        out_shape=[jax.ShapeDtypeStruct((s_len, total), F32), jax.ShapeDtypeStruct((CONV_WIDTH, 1, width), F32),
                   jax.ShapeDtypeStruct((1, width), F32)],
        compiler_params=_params("arbitrary"),
    )(dy, dy, src, src, w, *passthrough)


def _attn_mask(first):
    i = _iota((A_BLOCK, 2 * A_BLOCK), 0)
    j = _iota((A_BLOCK, 2 * A_BLOCK), 1)
    band = (j > i) & (j <= i + A_BLOCK)
    return band & ((j >= A_BLOCK) | jnp.logical_not(first))


def _attn_group(p, mask, qg, kw, kws, vw, vws, azg, sink0, sink1):
    low = _iota(qg.shape, 1) < A_HEAD_DIM
    first_lane = (_iota((A_BLOCK, LANES), 1) == 0).astype(F32)
    out = None
    for half, sink in ((0, sink0), (1, sink1)):
        kv_head = (2 * p + half) // (A_HEADS // A_KV_HEADS)
        keep = low if half == 0 else jnp.logical_not(low)
        qm = jnp.where(keep, qg, 0.0)
        kk, vv = (kw, vw) if kv_head == half else (kws, vws)
        s = mm_nt(qm, kk) * (A_HEAD_DIM ** -0.5)
        s = jnp.where(mask, s, NEG)
        sk = jnp.sum(jnp.tile(sink, (A_BLOCK // 8, 1)) * first_lane, axis=1, keepdims=True)
        m = jnp.maximum(jnp.max(s, axis=1, keepdims=True), sk)
        e = jnp.exp(s - m)
        denom = jnp.sum(e, axis=1, keepdims=True) + jnp.exp(sk - m)
        o = mm_nn(e / denom, vv)
        o = jnp.where(keep, o, 0.0)
        out = o if out is None else out + o
    return out * _silu(azg)


def _attn_specs(s_len, rev):
    nb = s_len // A_BLOCK

    def cur(i):
        return nb - 1 - i if rev else i

    def prev(i):
        return jnp.maximum(cur(i) - 1, 0)

    def blk(width, col, which):
        return pl.BlockSpec((A_BLOCK, width), lambda i: (which(i), col))

    return [blk(A_WIDTH, 0, cur), blk(A_WIDTH, 1, cur), blk(LANES, 8, cur), blk(LANES, 9, cur),
            blk(LANES, 8, prev), blk(LANES, 9, prev), blk(LANES, 0, cur), blk(LANES, 0, cur),
            blk(LANES, 0, prev), blk(LANES, 0, prev)], cur


def _attn_fwd(proj_a, cos, sin, sinks_t, *, name):
    s_len = proj_a.shape[0]
    specs, _ = _attn_specs(s_len, False)

    def body(q_ref, az_ref, k_ref, v_ref, kp_ref, vp_ref, c_ref, s_ref, cp_ref, sp_ref, sink_ref, o_ref):
        first = pl.program_id(0) == 0
        mask = _attn_mask(first)
        qr = _rope(q_ref[...], c_ref[...], s_ref[...])
        kw = jnp.concatenate([_rope(kp_ref[...], cp_ref[...], sp_ref[...]), _rope(k_ref[...], c_ref[...], s_ref[...])], 0)
        vw = jnp.concatenate([vp_ref[...], v_ref[...]], 0)
        kws, vws = _swap64(kw), _swap64(vw)
        for p in range(A_WIDTH // LANES):
            cols = slice(p * LANES, (p + 1) * LANES)
            o = _attn_group(p, mask, qr[:, cols], kw, kws, vw, vws, az_ref[:, cols], sink_ref[2 * p], sink_ref[2 * p + 1])
            o_ref[:, cols] = o.astype(o_ref.dtype)

    return _call(
        body, name=name, grid=(s_len // A_BLOCK,),
        in_specs=specs + [pl.BlockSpec((A_HEADS, 8, LANES), lambda i: (0, 0, 0))],
        out_specs=pl.BlockSpec((A_BLOCK, A_WIDTH), lambda i: (i, 0)),
        out_shape=jax.ShapeDtypeStruct((s_len, A_WIDTH), MXU_DTYPE),
        compiler_params=_params("parallel"),
    )(proj_a, proj_a, proj_a, proj_a, proj_a, proj_a, cos, sin, cos, sin, sinks_t)


def _attn_bwd(proj_a, cos, sin, sinks_t, dya, *, name):
    s_len = proj_a.shape[0]
    specs, cur = _attn_specs(s_len, True)

    def body(q_ref, az_ref, k_ref, v_ref, kp_ref, vp_ref, c_ref, s_ref, cp_ref, sp_ref, sink_ref, dy_ref,
             o_ref, dsink_ref, dk_carry, dv_carry):
        i = pl.program_id(0)

        @pl.when(i == 0)
        def _():
            dsink_ref[...] = jnp.zeros_like(dsink_ref)
            dk_carry[...] = jnp.zeros_like(dk_carry)
            dv_carry[...] = jnp.zeros_like(dv_carry)

        first = cur(i) == 0
        mask = _attn_mask(first)
        cos_c, sin_c = c_ref[...], s_ref[...]
        qr = _rope(q_ref[...], cos_c, sin_c)
        kw = jnp.concatenate([_rope(kp_ref[...], cp_ref[...], sp_ref[...]), _rope(k_ref[...], cos_c, sin_c)], 0)
        vw = jnp.concatenate([vp_ref[...], v_ref[...]], 0)
        kws, vws = _swap64(kw), _swap64(vw)
        dkw = jnp.zeros_like(kw)
        dvw = jnp.zeros_like(vw)
        for p in range(A_WIDTH // LANES):
            cols = slice(p * LANES, (p + 1) * LANES)
            _, vjp = jax.vjp(functools.partial(_attn_group, p, mask), qr[:, cols], kw, kws, vw, vws, az_ref[:, cols],
                             sink_ref[2 * p], sink_ref[2 * p + 1])
            dq, dk1, dk2, dv1, dv2, daz, ds0, ds1 = vjp(dy_ref[:, cols])
            dkw += dk1 + _swap64(dk2)
            dvw += dv1 + _swap64(dv2)
            o_ref[:, cols] = _rope_t(dq, cos_c, sin_c)
            o_ref[:, A_WIDTH + p * LANES:A_WIDTH + (p + 1) * LANES] = daz
            dsink_ref[2 * p] += ds0
            dsink_ref[2 * p + 1] += ds1
        o_ref[:, 2 * A_WIDTH:2 * A_WIDTH + LANES] = _rope_t(dkw[A_BLOCK:, :] + dk_carry[...], cos_c, sin_c)
        o_ref[:, 2 * A_WIDTH + LANES:] = dvw[A_BLOCK:, :] + dv_carry[...]
        dk_carry[...] = dkw[:A_BLOCK, :]
        dv_carry[...] = dvw[:A_BLOCK, :]

    return _call(
        body, name=name, grid=(s_len // A_BLOCK,),
        in_specs=specs + [pl.BlockSpec((A_HEADS, 8, LANES), lambda i: (0, 0, 0)),
                          pl.BlockSpec((A_BLOCK, A_WIDTH), lambda i: (cur(i), 0))],
        out_specs=[pl.BlockSpec((A_BLOCK, WA), lambda i: (cur(i), 0)),
                   pl.BlockSpec((A_HEADS, 8, LANES), lambda i: (0, 0, 0))],
        out_shape=[jax.ShapeDtypeStruct((s_len, WA), F32), jax.ShapeDtypeStruct((A_HEADS, 8, LANES), F32)],
        scratch_shapes=[pltpu.VMEM((A_BLOCK, LANES), F32), pltpu.VMEM((A_BLOCK, LANES), F32)],
        compiler_params=_params("arbitrary"),
    )(proj_a, proj_a, proj_a, proj_a, proj_a, proj_a, cos, sin, cos, sin, sinks_t, dya)


RG_ROWS = 256


def _rg_gates(x, wa, ba, wx, bx, lam):
    r = jax.nn.sigmoid(mm_nn(x, wa) + ba)
    ig = jax.nn.sigmoid(mm_nn(x, wx) + bx)
    log_a = -R_C * r * _softplus(-lam)
    return jnp.exp(log_a), jnp.sqrt(_neg_expm1(2.0 * log_a)) * (ig * x)


def _rg_param_specs():
    mat = pl.BlockSpec((R_BLOCKS, R_BLOCK_DIM, R_BLOCK_DIM), lambda i: (0, 0, 0))
    vec = pl.BlockSpec((1, R_WIDTH), lambda i: (0, 0))
    return [mat, vec, mat, vec, vec]


def _rg_fwd(xr, proj_r, wa, ba, wx, bx, lam, *, name):
    s_len = xr.shape[0]
    rows = min(RG_ROWS, s_len)

    def body(x_ref, z_ref, wa_ref, ba_ref, wx_ref, bx_ref, lam_ref, h_ref, y_ref, a_buf, u_buf, carry):
        @pl.when(pl.program_id(0) == 0)
        def _():
            carry[...] = jnp.zeros_like(carry)

        for n in range(R_BLOCKS):
            cols = slice(n * R_BLOCK_DIM, (n + 1) * R_BLOCK_DIM)
            a, u = _rg_gates(x_ref[:, cols], wa_ref[n], ba_ref[:, cols], wx_ref[n], bx_ref[:, cols], lam_ref[:, cols])
            a_buf[:, cols] = a
            u_buf[:, cols] = u

        def step(t, h):
            h = a_buf[pl.ds(t, 1), :] * h + u_buf[pl.ds(t, 1), :]
            h_ref[pl.ds(t, 1), :] = h
            return h

        carry[...] = lax.fori_loop(0, rows, step, carry[...], unroll=8)
        y_ref[...] = (h_ref[...] * _silu(z_ref[...])).astype(y_ref.dtype)

    blk = pl.BlockSpec((rows, R_WIDTH), lambda i: (i, 0))
    return _call(
        body, name=name, grid=(s_len // rows,),
        in_specs=[blk, pl.BlockSpec((rows, R_WIDTH), lambda i: (i, 1))] + _rg_param_specs(),
        out_specs=[blk, blk],
        out_shape=[jax.ShapeDtypeStruct((s_len, R_WIDTH), F32), jax.ShapeDtypeStruct((s_len, R_WIDTH), MXU_DTYPE)],
        scratch_shapes=[pltpu.VMEM((rows, R_WIDTH), F32), pltpu.VMEM((rows, R_WIDTH), F32), pltpu.VMEM((1, R_WIDTH), F32)],
        compiler_params=_params("arbitrary"),
    )(xr, proj_r, wa, ba, wx, bx, lam)


def _rg_bwd(xr, proj_r, h, dyr, wa, ba, wx, bx, lam, *, name):
    s_len = xr.shape[0]
    rows = min(RG_ROWS, s_len)
    nblk = s_len // rows
    per = rows // HALO

    def cur(i):
        return nblk - 1 - i

    def body(x_ref, z_ref, h_ref, hh_ref, dy_ref, wa_ref, ba_ref, wx_ref, bx_ref, lam_ref,
             dx_ref, dz_ref, dwa_ref, dba_ref, dwx_ref, dbx_ref, dlam_ref, a_buf, g_buf, carry):
        i = pl.program_id(0)

        @pl.when(i == 0)
        def _():
            carry[...] = jnp.zeros_like(carry)
            for ref in (dwa_ref, dba_ref, dwx_ref, dbx_ref, dlam_ref):
                ref[...] = jnp.zeros_like(ref)

        z = z_ref[...]
        sig = jax.nn.sigmoid(z)
        hval = h_ref[...]
        dy = dy_ref[...]
        dz_ref[...] = dy * hval * (sig * (1.0 + z * (1.0 - sig)))
        g_buf[...] = dy * (z * sig)
        vjps = []
        for n in range(R_BLOCKS):
            cols = slice(n * R_BLOCK_DIM, (n + 1) * R_BLOCK_DIM)
            (a, _), vjp = jax.vjp(_rg_gates, x_ref[:, cols], wa_ref[n], ba_ref[:, cols], wx_ref[n], bx_ref[:, cols],
                                  lam_ref[:, cols])
            a_buf[:, cols] = a
            vjps.append(vjp)

        def step(k, c):
            t = rows - 1 - k
            g = g_buf[pl.ds(t, 1), :] + c
            g_buf[pl.ds(t, 1), :] = g
            return a_buf[pl.ds(t, 1), :] * g

        carry[...] = lax.fori_loop(0, rows, step, carry[...], unroll=8)
        h_halo = jnp.where(cur(i) == 0, 0.0, hh_ref[...])
        dh = g_buf[...]
        da = dh * _shift_down(hval, h_halo, 1)
        for n in range(R_BLOCKS):
            cols = slice(n * R_BLOCK_DIM, (n + 1) * R_BLOCK_DIM)
            dx, dwa, dba, dwx, dbx, dlam = vjps[n]((da[:, cols], dh[:, cols]))
            dx_ref[:, cols] = dx
            dwa_ref[n] += dwa
            dwx_ref[n] += dwx
            dba_ref[:, cols] += dba
            dbx_ref[:, cols] += dbx
            dlam_ref[:, cols] += dlam

    blk = pl.BlockSpec((rows, R_WIDTH), lambda i: (cur(i), 0))
    mat = pl.BlockSpec((R_BLOCKS, R_BLOCK_DIM, R_BLOCK_DIM), lambda i: (0, 0, 0))
    vec = pl.BlockSpec((1, R_WIDTH), lambda i: (0, 0))
    return _call(
        body, name=name, grid=(nblk,),
        in_specs=[blk, pl.BlockSpec((rows, R_WIDTH), lambda i: (cur(i), 1)), blk,
                  pl.BlockSpec((HALO, R_WIDTH), lambda i: (jnp.maximum(cur(i) * per - 1, 0), 0)), blk] + _rg_param_specs(),
        out_specs=[blk, blk, mat, vec, mat, vec, vec],
        out_shape=[jax.ShapeDtypeStruct((s_len, R_WIDTH), F32)] * 2 + [
            jax.ShapeDtypeStruct((R_BLOCKS, R_BLOCK_DIM, R_BLOCK_DIM), F32), jax.ShapeDtypeStruct((1, R_WIDTH), F32),
            jax.ShapeDtypeStruct((R_BLOCKS, R_BLOCK_DIM, R_BLOCK_DIM), F32), jax.ShapeDtypeStruct((1, R_WIDTH), F32),
            jax.ShapeDtypeStruct((1, R_WIDTH), F32)],
        scratch_shapes=[pltpu.VMEM((rows, R_WIDTH), F32), pltpu.VMEM((rows, R_WIDTH), F32), pltpu.VMEM((1, R_WIDTH), F32)],
        compiler_params=_params("arbitrary"),
    )(xr, proj_r, h, h, dyr, wa, ba, wx, bx, lam)


GP_CHUNKS = 2
GS_CHUNKS = 8


def _seg_cumsum(x, reverse):
    rows = x.shape[0]
    r = _iota(x.shape, 0) & (G_CHUNK - 1)
    s = 1
    while s < G_CHUNK:
        if reverse:
            x = x + jnp.where(r < G_CHUNK - s, pltpu.roll(x, rows - s, 0), 0.0)
        else:
            x = x + jnp.where(r >= s, pltpu.roll(x, s, 0), 0.0)
        s *= 2
    return x


def _gdn_decay(ga, a_log_row, dt_row):
    return -jnp.exp(a_log_row) * _softplus(ga + dt_row)


def _gdn_chunk(cq, ck, cv, gb, gc):
    shape = cq.shape
    head = _iota(shape, 0) & (G_HEADS - 1)
    lane = _iota(shape, 2)
    q, k, v = _silu(cq), _silu(ck), _silu(cv)
    q = q * lax.rsqrt(jnp.sum(q * q, axis=-1, keepdims=True) + RMS_EPS) * (G_HEAD_DIM ** -0.5)
    k = k * lax.rsqrt(jnp.sum(k * k, axis=-1, keepdims=True) + RMS_EPS)
    beta = jnp.sum(jnp.where(lane == head, jax.nn.sigmoid(gb), 0.0), axis=-1, keepdims=True)
    g = jnp.sum(jnp.where(lane == head + G_HEADS, gc, 0.0), axis=-1, keepdims=True)
    sq = (shape[0], G_CHUNK, G_CHUNK)
    row, col = _iota(sq, 1), _iota(sq, 2)
    g_sq = jnp.broadcast_to(g, sq)
    decay = jnp.where(row >= col, jnp.exp(jnp.minimum(g_sq - _t(g_sq), 0.0)), 0.0)
    g_last = jnp.sum(jnp.where(_iota(g.shape, 1) == G_CHUNK - 1, g, 0.0), axis=1, keepdims=True)
    eg = jnp.exp(g)
    kb, vb = k * beta, v * beta
    m = jnp.where(row > col, mm_nt(kb, k) * decay, 0.0)
    u, w = _solve2(m, vb, kb * eg)
    qk = jnp.where(row >= col, mm_nt(q, k) * decay, 0.0)
    q_dec = q * eg
    k_dec = k * jnp.exp(g_last - g)
    gl = jnp.broadcast_to(jnp.exp(g_last), (shape[0], 1, G_HEAD_DIM))
    return u, w, qk, q_dec, k_dec, gl


def _gdn_step(state, u, w, qk, q_dec, k_dec, gl, gz, norm_w):
    v_new = u - mm_nn(w, state)
    o = mm_nn(q_dec, state) + mm_nn(qk, v_new)
    new_state = state * gl + mm_nn(_t(k_dec), v_new)
    o = o * lax.rsqrt(jnp.mean(o * o, axis=-1, keepdims=True) + RMS_EPS) * norm_w
    return o * _silu(gz), new_state


def _stack_chunks(x, heads):
    chunks = x.shape[0] // G_CHUNK
    parts = []
    for c in range(chunks):
        rows = slice(c * G_CHUNK, (c + 1) * G_CHUNK)
        for hd in range(G_HEADS):
            parts.append(x[rows, hd * LANES:(hd + 1) * LANES] if heads else x[rows, :])
    return jnp.stack(parts)


def _gdn_chunk_shapes(nch):
    b = nch * G_HEADS
    wide = jax.ShapeDtypeStruct((b, G_CHUNK, G_HEAD_DIM), F32)
    return [wide, wide, jax.ShapeDtypeStruct((b, G_CHUNK, G_CHUNK), F32), wide, wide,
            jax.ShapeDtypeStruct((b, 1, G_HEAD_DIM), F32)]


def _gdn_chunk_specs(nbatch):
    wide = pl.BlockSpec((nbatch, G_CHUNK, G_HEAD_DIM), lambda i: (i, 0, 0))
    return [wide, wide, pl.BlockSpec((nbatch, G_CHUNK, G_CHUNK), lambda i: (i, 0, 0)), wide, wide,
            pl.BlockSpec((nbatch, 1, G_HEAD_DIM), lambda i: (i, 0, 0))]


def _gdn_chunk_fwd(conv, proj_g, a_log_row, dt_row, *, name):
    s_len = conv.shape[0]
    cpg = min(GP_CHUNKS, s_len // G_CHUNK)
    rows = cpg * G_CHUNK
    nbatch = cpg * G_HEADS

    def body(c_ref, bg_ref, al_ref, dt_ref, *outs):
        bg = bg_ref[...]
        gc = _seg_cumsum(_gdn_decay(bg, al_ref[...], dt_ref[...]), False)
        res = _gdn_chunk(_stack_chunks(c_ref[:, 0:G_WIDTH], True), _stack_chunks(c_ref[:, G_WIDTH:2 * G_WIDTH], True),
                         _stack_chunks(c_ref[:, 2 * G_WIDTH:], True), _stack_chunks(bg, False), _stack_chunks(gc, False))
        for ref, val in zip(outs, res):
            ref[...] = val

    row = pl.BlockSpec((1, LANES), lambda i: (0, 0))
    return _call(
        body, name=name, grid=(s_len // rows,),
        in_specs=[pl.BlockSpec((rows, 3 * G_WIDTH), lambda i: (i, 0)),
                  pl.BlockSpec((rows, LANES), lambda i: (i, (3 * G_WIDTH + G_WIDTH) // LANES)), row, row],
        out_specs=_gdn_chunk_specs(nbatch), out_shape=_gdn_chunk_shapes(s_len // G_CHUNK),
        compiler_params=_params("parallel"),
    )(conv, proj_g, a_log_row, dt_row)


def _gdn_chunk_bwd(conv, proj_g, a_log_row, dt_row, cots, *, name):
    s_len = conv.shape[0]
    cpg = min(GP_CHUNKS, s_len // G_CHUNK)
    rows = cpg * G_CHUNK
    nbatch = cpg * G_HEADS

    def unstack(x, heads):
        if heads:
            return jnp.concatenate([jnp.concatenate([x[c * G_HEADS + hd] for hd in range(G_HEADS)], axis=1)
                                    for c in range(cpg)], axis=0)
        return jnp.concatenate([sum(x[c * G_HEADS + hd] for hd in range(G_HEADS)) for c in range(cpg)], axis=0)

    def body(c_ref, bg_ref, al_ref, dt_ref, du, dw, dqk, dqd, dkd, dgl, dc_ref, dbg_ref, dal_ref, ddt_ref):
        @pl.when(pl.program_id(0) == 0)
        def _():
            dal_ref[...] = jnp.zeros_like(dal_ref)
            ddt_ref[...] = jnp.zeros_like(ddt_ref)

        bg = bg_ref[...]
        g_all, decay_vjp = jax.vjp(_gdn_decay, bg, al_ref[...], dt_ref[...])
        gc = _seg_cumsum(g_all, False)
        _, vjp = jax.vjp(_gdn_chunk, _stack_chunks(c_ref[:, 0:G_WIDTH], True),
                         _stack_chunks(c_ref[:, G_WIDTH:2 * G_WIDTH], True), _stack_chunks(c_ref[:, 2 * G_WIDTH:], True),
                         _stack_chunks(bg, False), _stack_chunks(gc, False))
        dq, dk, dv, dgb, dgc = vjp((du[...], dw[...], dqk[...], dqd[...], dkd[...], dgl[...]))
        dc_ref[:, 0:G_WIDTH] = unstack(dq, True)
        dc_ref[:, G_WIDTH:2 * G_WIDTH] = unstack(dk, True)
        dc_ref[:, 2 * G_WIDTH:] = unstack(dv, True)
        dga, dal, ddt = decay_vjp(_seg_cumsum(unstack(dgc, False), True))
        dbg_ref[:, 0:LANES] = unstack(dgb, False) + dga
        dbg_ref[:, LANES:] = jnp.zeros((rows, LANES), F32)
        dal_ref[...] += dal
        ddt_ref[...] += ddt

    row = pl.BlockSpec((1, LANES), lambda i: (0, 0))
    return _call(
        body, name=name, grid=(s_len // rows,),
        in_specs=[pl.BlockSpec((rows, 3 * G_WIDTH), lambda i: (i, 0)),
                  pl.BlockSpec((rows, LANES), lambda i: (i, (3 * G_WIDTH + G_WIDTH) // LANES)), row, row]
        + _gdn_chunk_specs(nbatch),
        out_specs=[pl.BlockSpec((rows, 3 * G_WIDTH), lambda i: (i, 0)), pl.BlockSpec((rows, 2 * LANES), lambda i: (i, 0)),
                   row, row],
        out_shape=[jax.ShapeDtypeStruct((s_len, 3 * G_WIDTH), F32), jax.ShapeDtypeStruct((s_len, 2 * LANES), F32),
                   jax.ShapeDtypeStruct((1, LANES), F32), jax.ShapeDtypeStruct((1, LANES), F32)],
        compiler_params=_params("arbitrary"),
    )(conv, proj_g, a_log_row, dt_row, *cots)


def _gdn_scan_specs(cpg, which):
    nbatch = cpg * G_HEADS
    wide = pl.BlockSpec((nbatch, G_CHUNK, G_HEAD_DIM), lambda i: (which(i), 0, 0))
    return [wide, wide, pl.BlockSpec((nbatch, G_CHUNK, G_CHUNK), lambda i: (which(i), 0, 0)), wide, wide,
            pl.BlockSpec((nbatch, 1, G_HEAD_DIM), lambda i: (which(i), 0, 0))]


def _gz_stack(z_ref, c):
    rows = pl.ds(pl.multiple_of(c * G_CHUNK, G_CHUNK), G_CHUNK)
    return jnp.stack([z_ref[rows, hd * LANES:(hd + 1) * LANES] for hd in range(G_HEADS)])


def _gdn_scan_fwd(chunk_vals, proj_g, norm_w, *, name):
    s_len = proj_g.shape[0]
    nch = s_len // G_CHUNK
    cpg = min(GS_CHUNKS, nch)
    rows = cpg * G_CHUNK

    def body(u_ref, w_ref, qk_ref, qd_ref, kd_ref, gl_ref, z_ref, nw_ref, y_ref, st_ref, state):
        @pl.when(pl.program_id(0) == 0)
        def _():
            state[...] = jnp.zeros_like(state)

        def step(c, carry):
            b = pl.ds(pl.multiple_of(c * G_HEADS, G_HEADS), G_HEADS)
            st = state[...]
            st_ref[b] = st
            y, new_state = _gdn_step(st, u_ref[b], w_ref[b], qk_ref[b], qd_ref[b], kd_ref[b], gl_ref[b],
                                     _gz_stack(z_ref, c), nw_ref[...])
            state[...] = new_state
            rws = pl.ds(pl.multiple_of(c * G_CHUNK, G_CHUNK), G_CHUNK)
            for hd in range(G_HEADS):
                y_ref[rws, hd * LANES:(hd + 1) * LANES] = y[hd].astype(y_ref.dtype)
            return carry

        lax.fori_loop(0, cpg, step, 0)

    return _call(
        body, name=name, grid=(nch // cpg,),
        in_specs=_gdn_scan_specs(cpg, lambda i: i) + [
            pl.BlockSpec((rows, G_WIDTH), lambda i: (i, 3)), pl.BlockSpec((1, G_HEAD_DIM), lambda i: (0, 0))],
        out_specs=[pl.BlockSpec((rows, G_WIDTH), lambda i: (i, 0)),
                   pl.BlockSpec((cpg * G_HEADS, G_HEAD_DIM, G_HEAD_DIM), lambda i: (i, 0, 0))],
        out_shape=[jax.ShapeDtypeStruct((s_len, G_WIDTH), MXU_DTYPE),
                   jax.ShapeDtypeStruct((nch * G_HEADS, G_HEAD_DIM, G_HEAD_DIM), F32)],
        scratch_shapes=[pltpu.VMEM((G_HEADS, G_HEAD_DIM, G_HEAD_DIM), F32)],
        compiler_params=_params("arbitrary"),
    )(*chunk_vals, proj_g, norm_w)


def _gdn_scan_bwd(chunk_vals, states, proj_g, norm_w, dyg, *, name):
    s_len = proj_g.shape[0]
    nch = s_len // G_CHUNK
    cpg = min(GS_CHUNKS, nch)
    rows = cpg * G_CHUNK
    ngrid = nch // cpg

    def cur(i):
        return ngrid - 1 - i

    def body(u_ref, w_ref, qk_ref, qd_ref, kd_ref, gl_ref, st_ref, z_ref, nw_ref, dy_ref,
             du_ref, dw_ref, dqk_ref, dqd_ref, dkd_ref, dgl_ref, dz_ref, dnw_ref, dstate):
        @pl.when(pl.program_id(0) == 0)
        def _():
            dstate[...] = jnp.zeros_like(dstate)
            dnw_ref[...] = jnp.zeros_like(dnw_ref)

        def step(k, carry):
            c = cpg - 1 - k
            b = pl.ds(pl.multiple_of(c * G_HEADS, G_HEADS), G_HEADS)
            _, vjp = jax.vjp(_gdn_step, st_ref[b], u_ref[b], w_ref[b], qk_ref[b], qd_ref[b], kd_ref[b], gl_ref[b],
                             _gz_stack(z_ref, c), nw_ref[...])
            dst, du, dw, dqk, dqd, dkd, dgl, dz, dnw = vjp((_gz_stack(dy_ref, c), dstate[...]))
            dstate[...] = dst
            du_ref[b], dw_ref[b], dqk_ref[b], dqd_ref[b], dkd_ref[b], dgl_ref[b] = du, dw, dqk, dqd, dkd, dgl
            rws = pl.ds(pl.multiple_of(c * G_CHUNK, G_CHUNK), G_CHUNK)
            for hd in range(G_HEADS):
                dz_ref[rws, hd * LANES:(hd + 1) * LANES] = dz[hd]
            dnw_ref[...] += dnw
            return carry

        lax.fori_loop(0, cpg, step, 0)

    gate = pl.BlockSpec((rows, G_WIDTH), lambda i: (cur(i), 3))
    wide = pl.BlockSpec((rows, G_WIDTH), lambda i: (cur(i), 0))
    vec = pl.BlockSpec((1, G_HEAD_DIM), lambda i: (0, 0))
    return _call(
        body, name=name, grid=(ngrid,),
        in_specs=_gdn_scan_specs(cpg, cur) + [
            pl.BlockSpec((cpg * G_HEADS, G_HEAD_DIM, G_HEAD_DIM), lambda i: (cur(i), 0, 0)), gate, vec, wide],
        out_specs=_gdn_scan_specs(cpg, cur) + [wide, vec],
        out_shape=_gdn_chunk_shapes(nch) + [jax.ShapeDtypeStruct((s_len, G_WIDTH), F32),
                                            jax.ShapeDtypeStruct((1, G_HEAD_DIM), F32)],
        scratch_shapes=[pltpu.VMEM((G_HEADS, G_HEAD_DIM, G_HEAD_DIM), F32)],
        compiler_params=_params("arbitrary"),
    )(*chunk_vals, states, proj_g, norm_w, dyg)


def _adamw_math(w, g, m, v):
    m = ADAM_B1 * m + (1.0 - ADAM_B1) * g
    v = ADAM_B2 * v + (1.0 - ADAM_B2) * (g * g)
    m_hat = m / (1.0 - ADAM_B1 ** ADAM_STEP)
    v_hat = v / (1.0 - ADAM_B2 ** ADAM_STEP)
    delta = -ADAM_LR * (m_hat / (jnp.sqrt(v_hat) + ADAM_EPS) + ADAM_WD * w)
    return delta, m, v


def _sum_adamw(parts, w, m, v, *, name, rows):
    n_layers, n_rows, n_cols = w.shape
    rows = min(rows, n_rows)

    def body(p_ref, w_ref, m_ref, v_ref, g_ref, d_ref, nm_ref, nv_ref):
        g = p_ref[0, 0]
        for k in range(1, N_DEV):
            g = g + p_ref[k, 0]
        delta, new_m, new_v = _adamw_math(w_ref[0], g, m_ref[0], v_ref[0])
        g_ref[0], d_ref[0], nm_ref[0], nv_ref[0] = g, delta, new_m, new_v

    blk = pl.BlockSpec((1, rows, n_cols), lambda l, i: (l, i, 0))
    return _call(
        body, name=name, grid=(n_layers, n_rows // rows),
        in_specs=[pl.BlockSpec((N_DEV, 1, rows, n_cols), lambda l, i: (0, l, i, 0)), blk, blk, blk],
        out_specs=[blk] * 4, out_shape=[jax.ShapeDtypeStruct(w.shape, F32)] * 4,
        compiler_params=_params("parallel", "parallel"),
    )(parts, w, m, v)


def _sum_slots(parts, *, name):
    rows = parts.shape[1]

    def body(p_ref, o_ref):
        g = p_ref[0]
        for k in range(1, N_DEV):
            g = g + p_ref[k]
        o_ref[...] = g

    return _call(body, name=name, grid=(1,),
                 in_specs=[pl.BlockSpec(parts.shape, lambda i: (0, 0, 0))],
                 out_specs=pl.BlockSpec((rows, LANES), lambda i: (0, 0)),
                 out_shape=jax.ShapeDtypeStruct((rows, LANES), F32), compiler_params=_params("arbitrary"))(parts)


def _adamw_packed(w, g, m, v, *, name):
    def body(w_ref, g_ref, m_ref, v_ref, d_ref, nm_ref, nv_ref):
        d_ref[...], nm_ref[...], nv_ref[...] = _adamw_math(w_ref[...], g_ref[...], m_ref[...], v_ref[...])

    blk = pl.BlockSpec(w.shape, lambda i: (0, 0))
    return _call(body, name=name, grid=(1,), in_specs=[blk] * 4, out_specs=[blk] * 3,
                 out_shape=[jax.ShapeDtypeStruct(w.shape, F32)] * 3, compiler_params=_params("arbitrary"))(w, g, m, v)


A_COLS = ((0, 512), (768, 1280), (512, 768))
R_COLS = ((1280, 3328),)
G_COLS = ((3328, 5384),)


def _group_weights(w_full):
    def take(ranges):
        return jnp.concatenate([w_full[:, a:b] for a, b in ranges], axis=1)

    w_g = jnp.concatenate([take(G_COLS), jnp.zeros((w_full.shape[0], G_PAD), w_full.dtype)], axis=1)
    return take(A_COLS), take(R_COLS), w_g


def _ungroup_grads(d_a, d_r, d_g):
    return jnp.concatenate([d_a[:, 0:512], d_a[:, 1024:1280], d_a[:, 512:1024], d_r, d_g[:, :WG - G_PAD]], axis=1)


def _rope_tables(s_len):
    inv = 1.0 / (ROPE_THETA ** (jnp.arange(0, A_HEAD_DIM, 2, dtype=F32) / A_HEAD_DIM))
    ang = jnp.arange(s_len, dtype=F32)[:, None] * inv[None, :]
    cos, sin = jnp.cos(ang), jnp.sin(ang)
    return jnp.tile(cos, (1, 4)), jnp.tile(jnp.concatenate([-sin, sin], axis=1), (1, 2))


def _pack(leaves):
    rows = []
    for leaf in leaves:
        flat = leaf.reshape(-1)
        pad = (-flat.shape[0]) % (8 * LANES)
        rows.append(jnp.pad(flat, (0, pad)).reshape(-1, LANES))
    return jnp.concatenate(rows, axis=0)


def _unpack(packed, shapes):
    out, row = [], 0
    for shape in shapes:
        size = math.prod(shape)
        nrows = -(-size // (8 * LANES)) * 8
        out.append(packed[row:row + nrows].reshape(-1)[:size].reshape(shape))
        row += nrows
    return out


def _lane_row(vals, offset):
    return jnp.pad(vals, (offset, LANES - offset - vals.shape[0])).reshape(1, LANES)


def kernel(x, w_in, sinks, r_conv_w, r_conv_b, r_wa, r_ba, r_wx, r_bx, r_lam, g_conv_w, g_a_log, g_dt_bias, g_norm_w, w_out, ln_g, ln_b, loss_target, m_w_in, m_sinks, m_r_conv_w, m_r_conv_b, m_r_wa, m_r_ba, m_r_wx, m_r_bx, m_r_lam, m_g_conv_w, m_g_a_log, m_g_dt_bias, m_g_norm_w, m_w_out, m_ln_g, m_ln_b, v_w_in, v_sinks, v_r_conv_w, v_r_conv_b, v_r_wa, v_r_ba, v_r_wx, v_r_bx, v_r_lam, v_g_conv_w, v_g_a_log, v_g_dt_bias, v_g_norm_w, v_w_out, v_ln_g, v_ln_b):
    s_len = x.shape[1]
    x0 = x.reshape(s_len, D_MODEL)
    target = loss_target.reshape(s_len, D_MODEL)
    me = 4 * lax.axis_index("x") + 2 * lax.axis_index("y") + lax.axis_index("c")

    win_all, wout_all, rcw_all, gcw_all = _all_gather(
        [w_in.astype(MXU_DTYPE), w_out.astype(MXU_DTYPE), r_conv_w, g_conv_w], "gather_weights")
    rcw_full = jnp.moveaxis(rcw_all, 0, 2).reshape(DEPTH, CONV_WIDTH, R_WIDTH)
    gcw_full = jnp.moveaxis(gcw_all, 0, 2).reshape(DEPTH, CONV_WIDTH, 3 * G_WIDTH)
    cos, sin = _rope_tables(s_len)

    layers = []
    for l in range(DEPTH):
        w_full = jnp.moveaxis(win_all[:, l], 0, 1).reshape(D_MODEL, N_IN)
        w_a, w_r, w_g = _group_weights(w_full)
        wo = wout_all[:, l].reshape(D_MODEL, D_MODEL)
        layers.append(dict(
            w_a=w_a, w_r=w_r, w_g=w_g, w_at=w_a.T, w_rt=w_r.T, w_gt=w_g.T, wo=wo,
            wo_at=wo[0:A_WIDTH].T, wo_rt=wo[A_WIDTH:A_WIDTH + R_WIDTH].T, wo_gt=wo[A_WIDTH + R_WIDTH:].T,
            sinks_t=jnp.broadcast_to(sinks[l][:, None, None], (A_HEADS, 8, LANES)),
            rcw=rcw_full[l], rcb=r_conv_b[l].reshape(1, R_WIDTH), wa=r_wa[l], ba=r_ba[l].reshape(1, R_WIDTH),
            wx=r_wx[l], bx=r_bx[l].reshape(1, R_WIDTH), lam=r_lam[l].reshape(1, R_WIDTH),
            gcw=gcw_full[l], zero_b=jnp.zeros((1, 3 * G_WIDTH), F32),
            a_log=_lane_row(g_a_log[l], G_HEADS), dt=_lane_row(g_dt_bias[l], G_HEADS),
            norm_w=g_norm_w[l].reshape(1, G_HEAD_DIM), ln_g=ln_g[l].reshape(1, D_MODEL), ln_b=ln_b[l].reshape(1, D_MODEL)))

    saved = []
    xin = x0
    for l, p in enumerate(layers):
        proj_a = _matmul([xin], [p["w_a"]], name=f"proj_a{l}", tm=512, tn=640)
        proj_r = _matmul([xin], [p["w_r"]], name=f"proj_r{l}", tm=512, tn=512)
        proj_g = _matmul([xin], [p["w_g"]], name=f"proj_g{l}", tm=512, tn=768)
        ya = _attn_fwd(proj_a, cos, sin, p["sinks_t"], name=f"attn_fwd{l}")
        xr = _conv_fwd(proj_r, R_WIDTH, p["rcw"], p["rcb"], name=f"rconv_fwd{l}")
        h, yr = _rg_fwd(xr, proj_r, p["wa"], p["ba"], p["wx"], p["bx"], p["lam"], name=f"rglru_fwd{l}")
        conv = _conv_fwd(proj_g, 3 * G_WIDTH, p["gcw"], p["zero_b"], name=f"gconv_fwd{l}")
        chunk_vals = _gdn_chunk_fwd(conv, proj_g, p["a_log"], p["dt"], name=f"gdn_chunk_fwd{l}")
        yg, states = _gdn_scan_fwd(chunk_vals, proj_g, p["norm_w"], name=f"gdn_scan_fwd{l}")
        z, xout = _outproj_ln(ya, yr, yg, p["wo"], xin, p["ln_g"], p["ln_b"], name=f"outproj_ln{l}")
        saved.append(dict(xin=xin, proj_a=proj_a, proj_r=proj_r, proj_g=proj_g, ya=ya, yr=yr, yg=yg, xr=xr, h=h,
                          conv=conv, chunk_vals=chunk_vals, states=states, z=z))
        xin = xout

    grads = [None] * DEPTH
    dxn = None
    loss_local = None
    for l in reversed(range(DEPTH)):
        p, sv = layers[l], saved[l]
        if dxn is None:
            dz, dln_g, dln_b, loss_local = _ln_bwd(sv["z"], p["ln_g"], name=f"ln_bwd{l}", xn=xin, target=target)
        else:
            dz, dln_g, dln_b = _ln_bwd(sv["z"], p["ln_g"], name=f"ln_bwd{l}", dxn=dxn)
        dya = _matmul([dz], [p["wo_at"]], name=f"dya{l}", tm=512, tn=512)
        dyr = _matmul([dz], [p["wo_rt"]], name=f"dyr{l}", tm=512, tn=512)
        dyg = _matmul([dz], [p["wo_gt"]], name=f"dyg{l}", tm=512, tn=512)
        dwo = jnp.concatenate([
            _matmul_tn(sv["ya"], dz, name=f"dwo_a{l}", tm=512, tn=1024, tk=512),
            _matmul_tn(sv["yr"], dz, name=f"dwo_r{l}", tm=1024, tn=1024, tk=512),
            _matmul_tn(sv["yg"], dz, name=f"dwo_g{l}", tm=512, tn=1024, tk=512)], axis=0)

        dproj_a, dsinks_t = _attn_bwd(sv["proj_a"], cos, sin, p["sinks_t"], dya, name=f"attn_bwd{l}")

        dxr, drz, dwa, dba, dwx, dbx, dlam = _rg_bwd(sv["xr"], sv["proj_r"], sv["h"], dyr, p["wa"], p["ba"], p["wx"],
                                                     p["bx"], p["lam"], name=f"rglru_bwd{l}")
        dproj_r, drcw, drcb = _conv_bwd(dxr, sv["proj_r"], R_WIDTH, p["rcw"], [drz], name=f"rconv_bwd{l}")

        scan_out = _gdn_scan_bwd(sv["chunk_vals"], sv["states"], sv["proj_g"], p["norm_w"], dyg, name=f"gdn_scan_bwd{l}")
        dgz, dnorm_w = scan_out[6], scan_out[7]
        dconv, dbg, dal, ddt = _gdn_chunk_bwd(sv["conv"], sv["proj_g"], p["a_log"], p["dt"], scan_out[:6],
                                              name=f"gdn_chunk_bwd{l}")
        dproj_g, dgcw, _ = _conv_bwd(dconv, sv["proj_g"], 3 * G_WIDTH, p["gcw"], [dgz, dbg], name=f"gconv_bwd{l}")

        dxn = _matmul([dproj_a, dproj_r, dproj_g], [p["w_at"], p["w_rt"], p["w_gt"]], name=f"dx{l}", tm=256, tn=512,
                      add=dz, add_scale=DEEPNORM_ALPHA)
        dwin = _ungroup_grads(_matmul_tn(sv["xin"], dproj_a, name=f"dwin_a{l}", tm=1024, tn=640, tk=256),
                              _matmul_tn(sv["xin"], dproj_r, name=f"dwin_r{l}", tm=1024, tn=1024, tk=256),
                              _matmul_tn(sv["xin"], dproj_g, name=f"dwin_g{l}", tm=1024, tn=1152, tk=256))
        grads[l] = dict(
            w_in=dwin, w_out=dwo, sinks=dsinks_t[:, :, 0].sum(axis=1), r_conv_w=drcw.reshape(CONV_WIDTH, R_WIDTH),
            r_conv_b=drcb.reshape(R_WIDTH), r_wa=dwa, r_ba=dba.reshape(R_WIDTH), r_wx=dwx, r_bx=dbx.reshape(R_WIDTH),
            r_lam=dlam.reshape(R_WIDTH), g_conv_w=dgcw.reshape(CONV_WIDTH, 3 * G_WIDTH),
            g_a_log=dal[0, G_HEADS:2 * G_HEADS], g_dt_bias=ddt[0, G_HEADS:2 * G_HEADS],
            g_norm_w=dnorm_w.reshape(G_HEAD_DIM), ln_g=dln_g.reshape(D_MODEL), ln_b=dln_b.reshape(D_MODEL))
    grad_x = dxn.reshape(x.shape)
    loss = lax.psum(loss_local[0, 0], ("x", "y", "c"))

    def stacked(name):
        return jnp.stack([grads[l][name] for l in range(DEPTH)])

    dwin_blocks = jnp.moveaxis(stacked("w_in").reshape(DEPTH, D_MODEL, N_DEV, N_IN_SHARD), 2, 0)
    dwout_blocks = jnp.moveaxis(stacked("w_out").reshape(DEPTH, N_DEV, OUT_SHARD, D_MODEL), 1, 0)
    win_parts, wout_parts = _exchange([dwin_blocks, dwout_blocks], "exchange_grads")
    g_w_in, d_w_in, nm_w_in, nv_w_in = _sum_adamw(win_parts, w_in, m_w_in, v_w_in, name="adamw_w_in", rows=256)
    g_w_out, d_w_out, nm_w_out, nv_w_out = _sum_adamw(wout_parts, w_out, m_w_out, v_w_out, name="adamw_w_out", rows=128)

    small = ["sinks", "r_conv_w", "r_conv_b", "r_wa", "r_ba", "r_wx", "r_bx", "r_lam", "g_conv_w", "g_a_log",
             "g_dt_bias", "g_norm_w", "ln_g", "ln_b"]
    full_shapes = [stacked(nm).shape for nm in small]
    (all_small,) = _all_gather([_pack([stacked(nm) for nm in small])], "gather_small_grads")
    g_small = dict(zip(small, _unpack(_sum_slots(all_small, name="sum_small_grads"), full_shapes)))
    g_small["r_conv_w"] = lax.dynamic_slice_in_dim(g_small["r_conv_w"], me * (R_WIDTH // N_DEV), R_WIDTH // N_DEV, axis=2)
    g_small["g_conv_w"] = lax.dynamic_slice_in_dim(g_small["g_conv_w"], me * (3 * G_WIDTH // N_DEV), 3 * G_WIDTH // N_DEV, axis=2)
    given = dict(sinks=(sinks, m_sinks, v_sinks), r_conv_w=(r_conv_w, m_r_conv_w, v_r_conv_w),
                 r_conv_b=(r_conv_b, m_r_conv_b, v_r_conv_b), r_wa=(r_wa, m_r_wa, v_r_wa), r_ba=(r_ba, m_r_ba, v_r_ba),
                 r_wx=(r_wx, m_r_wx, v_r_wx), r_bx=(r_bx, m_r_bx, v_r_bx), r_lam=(r_lam, m_r_lam, v_r_lam),
                 g_conv_w=(g_conv_w, m_g_conv_w, v_g_conv_w), g_a_log=(g_a_log, m_g_a_log, v_g_a_log),
                 g_dt_bias=(g_dt_bias, m_g_dt_bias, v_g_dt_bias), g_norm_w=(g_norm_w, m_g_norm_w, v_g_norm_w),
                 ln_g=(ln_g, m_ln_g, v_ln_g), ln_b=(ln_b, m_ln_b, v_ln_b))
    shard_shapes = [given[nm][0].shape for nm in small]
    packed = [_pack([given[nm][k] for nm in small]) for k in range(3)]
    d_p, nm_p, nv_p = _adamw_packed(packed[0], _pack([g_small[nm] for nm in small]), packed[1], packed[2], name="adamw_small")
    d_small = dict(zip(small, _unpack(d_p, shard_shapes)))
    nm_small = dict(zip(small, _unpack(nm_p, shard_shapes)))
    nv_small = dict(zip(small, _unpack(nv_p, shard_shapes)))

    order = ["w_in"] + small[:12] + ["w_out"] + small[12:]

    def leaf(big_in, big_out, table):
        return [big_in if nm == "w_in" else big_out if nm == "w_out" else table[nm] for nm in order]

    return (loss, grad_x, *leaf(g_w_in, g_w_out, g_small), *leaf(d_w_in, d_w_out, d_small),
            *leaf(nm_w_in, nm_w_out, nm_small), *leaf(nv_w_in, nv_w_out, nv_small))
```

```python
import functools
import math

import jax
import jax.numpy as jnp
from jax import lax
from jax.experimental import pallas as pl
from jax.experimental.pallas import tpu as pltpu

F32 = jnp.float32
MXU_DTYPE = jnp.bfloat16
HIGHEST = lax.Precision.HIGHEST
MESH_ID = pl.DeviceIdType.MESH

N_DEV = 8
DEPTH = 2
D_MODEL = 2048
A_HEADS, A_KV_HEADS, A_HEAD_DIM = 8, 2, 64
A_WIDTH, A_KV_WIDTH = 512, 128
A_BLOCK = 128
ROPE_THETA = 10000.0
R_WIDTH, R_BLOCKS, R_BLOCK_DIM = 1024, 8, 128
R_C = 8.0
CONV_WIDTH = 4
G_HEADS, G_HEAD_DIM, G_WIDTH, G_CHUNK = 4, 128, 512, 64
N_IN = 5384
N_IN_SHARD = N_IN // N_DEV
OUT_SHARD = D_MODEL // N_DEV
WA, WR, WG = 1280, 2048, 2304
G_PAD = WG - (3 * G_WIDTH + G_WIDTH + 2 * G_HEADS)
DEEPNORM_ALPHA = (2 * DEPTH) ** 0.25
LN_EPS = 1e-5
RMS_EPS = 1e-6
ADAM_LR, ADAM_B1, ADAM_B2, ADAM_EPS, ADAM_WD, ADAM_STEP = 0.001, 0.9, 0.999, 1e-08, 0.01, 10
NEG = -1e30
VMEM_LIMIT = 56 * 1024 * 1024
LANES = 128


def _call(body, **kw):
    return pl.pallas_call(body, **kw)


def _params(*sem):
    return pltpu.CompilerParams(dimension_semantics=sem, vmem_limit_bytes=VMEM_LIMIT)


def _t(x):
    return jnp.swapaxes(x, -1, -2)


def _raw_dot(a, b, ca, cb, precision=None):
    batch = tuple(range(a.ndim - 2))
    if precision is None:
        a, b = a.astype(MXU_DTYPE), b.astype(MXU_DTYPE)
    return lax.dot_general(a, b, (((ca,), (cb,)), (batch, batch)), precision=precision,
                           preferred_element_type=F32)


def _nn(a, b, precision=None):
    return _raw_dot(a, b, a.ndim - 1, b.ndim - 2, precision)


def _nt(a, b, precision=None):
    return _raw_dot(a, b, a.ndim - 1, b.ndim - 1, precision)


@jax.custom_vjp
def mm_nn(a, b):
    return _nn(a, b)


def _mm_nn_fwd(a, b):
    return _nn(a, b), (a, b)


def _mm_nn_bwd(res, g):
    a, b = res
    return _nt(g, b), _nn(_t(a), g)


mm_nn.defvjp(_mm_nn_fwd, _mm_nn_bwd)


@jax.custom_vjp
def mm_nt(a, b):
    return _nt(a, b)


def _mm_nt_fwd(a, b):
    return _nt(a, b), (a, b)


def _mm_nt_bwd(res, g):
    a, b = res
    return _nn(g, b), _nn(_t(g), a)


mm_nt.defvjp(_mm_nt_fwd, _mm_nt_bwd)


def _split(x):
    hi = x.astype(MXU_DTYPE)
    return hi, (x - hi.astype(F32)).astype(MXU_DTYPE)


def _hmm(a, b, nt=False):
    dot = _nt if nt else _nn
    return dot(a[0], b[0]) + (dot(a[0], b[1]) + dot(a[1], b[0]))


def _silu(x):
    return x * jax.nn.sigmoid(x)


def _softplus(x):
    return jnp.maximum(x, 0.0) + jnp.log1p(jnp.exp(-jnp.abs(x)))


def _neg_expm1(x):
    series = -x * (1.0 + x * 0.5 * (1.0 + x * (1.0 / 3.0) * (1.0 + x * 0.25 * (1.0 + x * 0.2))))
    return jnp.where(x > -0.125, series, 1.0 - jnp.exp(x))


def _iota(shape, dim):
    return lax.broadcasted_iota(jnp.int32, shape, dim)


def _inv_unit_lower(m):
    shape = m.shape
    row, col = _iota(shape, 1), _iota(shape, 2)
    eye = (row == col).astype(F32)

    def blockdiag(size):
        return (row // size) == (col // size)

    x = -jnp.where(blockdiag(8), m, 0.0)
    xs = _split(x)
    x2s = _split(_hmm(xs, xs))
    x4s = _split(_hmm(x2s, x2s))
    inv = eye + x
    inv = inv + _hmm(_split(inv), x2s)
    inv = inv + _hmm(_split(inv), x4s)
    for size in (8, 16, 32):
        below = jnp.where(blockdiag(2 * size) & jnp.logical_not(blockdiag(size)), m, 0.0)
        invs = _split(inv)
        inv = inv - _hmm(_split(_hmm(invs, _split(below))), invs)
    return inv


@jax.custom_vjp
def _solve2(m, inv, r1, r2):
    invs = _split(inv)
    return _hmm(invs, _split(r1)), _hmm(invs, _split(r2))


def _solve2_fwd(m, inv, r1, r2):
    x1, x2 = _solve2(m, inv, r1, r2)
    return (x1, x2), (inv, x1, x2)


def _solve2_bwd(res, g):
    inv, x1, x2 = res
    inv_ts = _split(_t(inv))
    d1, d2 = _hmm(inv_ts, _split(g[0])), _hmm(inv_ts, _split(g[1]))
    dm = -(_hmm(_split(d1), _split(x1), nt=True) + _hmm(_split(d2), _split(x2), nt=True))
    return dm, jnp.zeros_like(inv), d1, d2


_solve2.defvjp(_solve2_fwd, _solve2_bwd)


def _swap_halves(x):
    n = x.shape[-1]
    lane = _iota(x.shape, x.ndim - 1)
    return jnp.where((lane & 63) < 32, pltpu.roll(x, n - 32, x.ndim - 1), pltpu.roll(x, 32, x.ndim - 1))


def _rope(x, cos, sin):
    reps = x.shape[-1] // LANES
    if reps > 1:
        cos, sin = jnp.tile(cos, (1, reps)), jnp.tile(sin, (1, reps))
    return x * cos + _swap_halves(x) * sin


def _rope_t(d, cos, sin):
    reps = d.shape[-1] // LANES
    if reps > 1:
        cos, sin = jnp.tile(cos, (1, reps)), jnp.tile(sin, (1, reps))
    return d * cos + _swap_halves(d * sin)


def _swap64(x):
    return pltpu.roll(x, 64, x.ndim - 1)


def _mesh_pos():
    return lax.axis_index("x"), lax.axis_index("y"), lax.axis_index("c")


def _all_gather(arrays, name):
    n = len(arrays)

    def body(*refs):
        ins, outs = refs[:n], refs[n:2 * n]
        send_sems, recv_sems, local_sem = refs[2 * n:]
        x, y, c = _mesh_pos()
        me, sibling = (x, y, c), (x, y, 1 - c)
        chips = [(1 - x, y), (x, 1 - y), (1 - x, 1 - y)]

        def slot(ref, pos):
            return ref.at[4 * pos[0] + 2 * pos[1] + pos[2]]

        def copy(a, k, block, to, src=None):
            return pltpu.make_async_remote_copy(
                src_ref=slot(outs[a], block) if src is None else src, dst_ref=slot(outs[a], block),
                send_sem=send_sems.at[a, k], recv_sem=recv_sems.at[a, k], device_id=to, device_id_type=MESH_ID)

        mine = [pltpu.make_async_copy(ins[a], slot(outs[a], me), local_sem.at[a]) for a in range(n)]
        for cp in mine:
            cp.start()
        first = []
        for a in range(n):
            first.append(copy(a, 0, me, sibling, src=ins[a]))
            first += [copy(a, 1 + j, me, (*chip, c), src=ins[a]) for j, chip in enumerate(chips)]
        for cp in first:
            cp.start()
        passed = []
        for a in range(n):
            for j, chip in enumerate(chips):
                copy(a, 1 + j, (*chip, c), me).wait_recv()
                cp = copy(a, 4 + j, (*chip, c), sibling)
                cp.start()
                passed.append(cp)
        for a in range(n):
            copy(a, 0, sibling, me).wait_recv()
            for j, chip in enumerate(chips):
                copy(a, 4 + j, (*chip, 1 - c), me).wait_recv()
        for cp in first + passed:
            cp.wait_send()
        for cp in mine:
            cp.wait()

    any_spec = pl.BlockSpec(memory_space=pl.ANY)
    return _call(
        body, name=name,
        out_shape=[jax.ShapeDtypeStruct((N_DEV,) + a.shape, a.dtype) for a in arrays],
        in_specs=[any_spec] * n, out_specs=[any_spec] * n,
        scratch_shapes=[pltpu.SemaphoreType.DMA((n, 7)), pltpu.SemaphoreType.DMA((n, 7)),
                        pltpu.SemaphoreType.DMA((n,))],
    )(*arrays)


def _swap_cores(arrays, name):
    n = len(arrays)

    def body(*refs):
        ins, kept, got = refs[:n], refs[n:2 * n], refs[2 * n:3 * n]
        send_sems, recv_sems, local_sem = refs[3 * n:]
        x, y, c = _mesh_pos()
        mine = [pltpu.make_async_copy(ins[a].at[c], kept[a], local_sem.at[a]) for a in range(n)]
        for cp in mine:
            cp.start()
        copies = [pltpu.make_async_remote_copy(
            src_ref=ins[a].at[1 - c], dst_ref=got[a], send_sem=send_sems.at[a], recv_sem=recv_sems.at[a],
            device_id=(x, y, 1 - c), device_id_type=MESH_ID) for a in range(n)]
        for cp in copies:
            cp.start()
        for cp in copies:
            cp.wait()
        for cp in mine:
            cp.wait()

    any_spec = pl.BlockSpec(memory_space=pl.ANY)
    slabs = [jax.ShapeDtypeStruct(a.shape[1:], a.dtype) for a in arrays]
    outs = _call(
        body, name=name, out_shape=slabs + slabs, in_specs=[any_spec] * n, out_specs=[any_spec] * (2 * n),
        scratch_shapes=[pltpu.SemaphoreType.DMA((n,)), pltpu.SemaphoreType.DMA((n,)), pltpu.SemaphoreType.DMA((n,))],
    )(*arrays)
    return outs[:n], outs[n:]


def _exchange_chips(arrays, name):
    n = len(arrays)

    def body(*refs):
        ins, outs = refs[:n], refs[n:2 * n]
        send_sems, recv_sems, local_sem = refs[2 * n:]
        x, y, c = _mesh_pos()
        mine = [pltpu.make_async_copy(ins[a].at[2 * x + y], outs[a].at[0], local_sem.at[a]) for a in range(n)]
        for cp in mine:
            cp.start()
        copies = []
        for a in range(n):
            for k in range(1, 4):
                px, py = x ^ (k >> 1), y ^ (k & 1)
                copies.append(pltpu.make_async_remote_copy(
                    src_ref=ins[a].at[2 * px + py], dst_ref=outs[a].at[k], send_sem=send_sems.at[a, k - 1],
                    recv_sem=recv_sems.at[a, k - 1], device_id=(px, py, c), device_id_type=MESH_ID))
        for cp in copies:
            cp.start()
        for cp in copies:
            cp.wait_recv()
        for cp in copies:
            cp.wait_send()
        for cp in mine:
            cp.wait()

    any_spec = pl.BlockSpec(memory_space=pl.ANY)
    return _call(
        body, name=name,
        out_shape=[jax.ShapeDtypeStruct(a.shape, a.dtype) for a in arrays],
        in_specs=[any_spec] * n, out_specs=[any_spec] * n,
        scratch_shapes=[pltpu.SemaphoreType.DMA((n, 3)), pltpu.SemaphoreType.DMA((n, 3)),
                        pltpu.SemaphoreType.DMA((n,))],
    )(*arrays)


def _add_pair(a, b, *, name, rows):
    rows = min(rows, a.shape[2])

    def body(a_ref, b_ref, o_ref):
        o_ref[...] = (a_ref[...].astype(F32) + b_ref[...].astype(F32)).astype(o_ref.dtype)

    blk = pl.BlockSpec((1, 1, rows, a.shape[3]), lambda s, l, i: (s, l, i, 0))
    return _call(body, name=name, grid=(a.shape[0], a.shape[1], a.shape[2] // rows), in_specs=[blk, blk], out_specs=blk,
                 out_shape=jax.ShapeDtypeStruct(a.shape, a.dtype),
                 compiler_params=_params("parallel", "parallel", "parallel"))(a, b)


def _matmul(a_list, b_list, *, name, tm, tn, out_dtype=F32, add=None, add_scale=1.0):
    n = len(a_list)
    m_rows, n_cols = a_list[0].shape[0], b_list[0].shape[1]
    tm, tn = min(tm, m_rows), min(tn, n_cols)

    def body(*refs):
        a_refs, b_refs = refs[:n], refs[n:2 * n]
        o_ref = refs[-1]
        acc = None
        for a_ref, b_ref in zip(a_refs, b_refs):
            part = jnp.dot(a_ref[...].astype(MXU_DTYPE), b_ref[...].astype(MXU_DTYPE), preferred_element_type=F32)
            acc = part if acc is None else acc + part
        if add is not None:
            acc = acc + add_scale * refs[2 * n][...]
        o_ref[...] = acc.astype(o_ref.dtype)

    in_specs = [pl.BlockSpec((tm, a.shape[1]), lambda i, j: (i, 0)) for a in a_list]
    in_specs += [pl.BlockSpec((b.shape[0], tn), lambda i, j: (0, j)) for b in b_list]
    args = list(a_list) + list(b_list)
    if add is not None:
        in_specs.append(pl.BlockSpec((tm, tn), lambda i, j: (i, j)))
        args.append(add)
    return _call(
        body, name=name, grid=(m_rows // tm, n_cols // tn), in_specs=in_specs,
        out_specs=pl.BlockSpec((tm, tn), lambda i, j: (i, j)),
        out_shape=jax.ShapeDtypeStruct((m_rows, n_cols), out_dtype),
        compiler_params=_params("parallel", "arbitrary"),
    )(*args)


def _matmul_tn(a, b, *, name, tm, tn, tk):
    k_rows, m_rows = a.shape
    n_cols = b.shape[1]
    tm, tn, tk = min(tm, m_rows), min(tn, n_cols), min(tk, k_rows)
    nk = k_rows // tk

    def body(a_ref, b_ref, o_ref):
        part = lax.dot_general(a_ref[...].astype(MXU_DTYPE), b_ref[...].astype(MXU_DTYPE),
                               (((0,), (0,)), ((), ())), preferred_element_type=F32)

        @pl.when(pl.program_id(2) == 0)
        def _():
            o_ref[...] = part

        @pl.when(pl.program_id(2) > 0)
        def _():
            o_ref[...] += part

    return _call(
        body, name=name, grid=(m_rows // tm, n_cols // tn, nk),
        in_specs=[pl.BlockSpec((tk, tm), lambda i, j, k: (k, i)), pl.BlockSpec((tk, tn), lambda i, j, k: (k, j))],
        out_specs=pl.BlockSpec((tm, tn), lambda i, j, k: (i, j)),
        out_shape=jax.ShapeDtypeStruct((m_rows, n_cols), F32),
        compiler_params=_params("parallel", "parallel", "arbitrary"),
    )(a, b)


def _outproj_ln(ya, yr, yg, w_out, x, ln_g, ln_b, *, name):
    s_len = x.shape[0]
    tm = min(256, s_len)

    def body(ya_ref, yr_ref, yg_ref, w_ref, x_ref, g_ref, b_ref, z_ref, o_ref, lo_ref):
        acc = jnp.dot(ya_ref[...], w_ref[0:A_WIDTH, :], preferred_element_type=F32)
        acc += jnp.dot(yr_ref[...], w_ref[A_WIDTH:A_WIDTH + R_WIDTH, :], preferred_element_type=F32)
        acc += jnp.dot(yg_ref[...], w_ref[A_WIDTH + R_WIDTH:, :], preferred_element_type=F32)
        z = DEEPNORM_ALPHA * x_ref[...] + acc
        z_ref[...] = z
        mu = jnp.mean(z, axis=-1, keepdims=True)
        zc = z - mu
        var = jnp.mean(zc * zc, axis=-1, keepdims=True)
        out = zc * lax.rsqrt(var + LN_EPS) * g_ref[...] + b_ref[...]
        o_ref[...] = out
        lo_ref[...] = out.astype(lo_ref.dtype)

    def rows(width):
        return pl.BlockSpec((tm, width), lambda i: (i, 0))

    def whole(shape):
        return pl.BlockSpec(shape, lambda i: (0, 0))

    return _call(
        body, name=name, grid=(s_len // tm,),
        in_specs=[rows(A_WIDTH), rows(R_WIDTH), rows(G_WIDTH), whole((D_MODEL, D_MODEL)), rows(D_MODEL),
                  whole((1, D_MODEL)), whole((1, D_MODEL))],
        out_specs=[rows(D_MODEL)] * 3,
        out_shape=[jax.ShapeDtypeStruct((s_len, D_MODEL), F32)] * 2 + [jax.ShapeDtypeStruct((s_len, D_MODEL), MXU_DTYPE)],
        compiler_params=_params("parallel"),
    )(ya, yr, yg, w_out, x, ln_g, ln_b)


def _ln_bwd(z, ln_g, *, name, dxn=None, xn=None, target=None):
    s_len = z.shape[0]
    tm = min(256, s_len)
    top = dxn is None

    def body(*refs):
        if top:
            z_ref, g_ref, xn_ref, t_ref, dz_ref, lo_ref, dg_ref, db_ref, loss_ref = refs
            err = xn_ref[...] - t_ref[...]
            dy = err * (1.0 / D_MODEL)
        else:
            z_ref, g_ref, dy_ref, dz_ref, lo_ref, dg_ref, db_ref = refs
            dy = dy_ref[...]
        first = pl.program_id(0) == 0

        @pl.when(first)
        def _():
            dg_ref[...] = jnp.zeros_like(dg_ref)
            db_ref[...] = jnp.zeros_like(db_ref)
            if top:
                loss_ref[...] = jnp.zeros_like(loss_ref)

        z = z_ref[...]
        mu = jnp.mean(z, axis=-1, keepdims=True)
        zc = z - mu
        rstd = lax.rsqrt(jnp.mean(zc * zc, axis=-1, keepdims=True) + LN_EPS)
        xhat = zc * rstd
        dxh = dy * g_ref[...]
        dz = rstd * (dxh - jnp.mean(dxh, axis=-1, keepdims=True) - xhat * jnp.mean(dxh * xhat, axis=-1, keepdims=True))
        dz_ref[...] = dz
        lo_ref[...] = dz.astype(lo_ref.dtype)
        dg_ref[...] += jnp.sum(dy * xhat, axis=0, keepdims=True)
        db_ref[...] += jnp.sum(dy, axis=0, keepdims=True)
        if top:
            per_row = jnp.sum(err * err, axis=-1, keepdims=True) * (0.5 / D_MODEL)
            loss_ref[...] += jnp.sum(per_row, axis=0, keepdims=True)

    rows = pl.BlockSpec((tm, D_MODEL), lambda i: (i, 0))
    vec = pl.BlockSpec((1, D_MODEL), lambda i: (0, 0))
    in_specs = [rows, vec] + ([rows, rows] if top else [rows])
    args = [z, ln_g] + ([xn, target] if top else [dxn])
    out_specs = [rows, rows, vec, vec]
    out_shape = [jax.ShapeDtypeStruct((s_len, D_MODEL), F32), jax.ShapeDtypeStruct((s_len, D_MODEL), MXU_DTYPE),
                 jax.ShapeDtypeStruct((1, D_MODEL), F32), jax.ShapeDtypeStruct((1, D_MODEL), F32)]
    if top:
        out_specs.append(pl.BlockSpec((1, 1), lambda i: (0, 0)))
        out_shape.append(jax.ShapeDtypeStruct((1, 1), F32))
    return _call(body, name=name, grid=(s_len // tm,), in_specs=in_specs, out_specs=out_specs,
                 out_shape=out_shape, compiler_params=_params("arbitrary"))(*args)


CONV_ROWS = 256
HALO = 8


def _shift_down(x, halo, s):
    if s == 0:
        return x
    ext = jnp.concatenate([halo, x], axis=0)
    return pltpu.roll(ext, s, 0)[HALO:, :]


def _shift_up(x, halo, s):
    if s == 0:
        return x
    ext = jnp.concatenate([x, halo], axis=0)
    return pltpu.roll(ext, ext.shape[0] - s, 0)[:x.shape[0], :]


def _conv_fwd(src, width, w, bias, *, name):
    s_len = src.shape[0]
    rows = min(CONV_ROWS, s_len)
    per = rows // HALO

    def body(x_ref, halo_ref, w_ref, b_ref, o_ref):
        x = x_ref[...]
        halo = jnp.where(pl.program_id(0) == 0, 0.0, halo_ref[...])
        acc = x * w_ref[3:4, :] + b_ref[...]
        for k in range(CONV_WIDTH - 1):
            acc += _shift_down(x, halo, 3 - k) * w_ref[k:k + 1, :]
        o_ref[...] = acc

    return _call(
        body, name=name, grid=(s_len // rows,),
        in_specs=[pl.BlockSpec((rows, width), lambda i: (i, 0)),
                  pl.BlockSpec((HALO, width), lambda i: (jnp.maximum(i * per - 1, 0), 0)),
                  pl.BlockSpec((CONV_WIDTH, width), lambda i: (0, 0)), pl.BlockSpec((1, width), lambda i: (0, 0))],
        out_specs=pl.BlockSpec((rows, width), lambda i: (i, 0)),
        out_shape=jax.ShapeDtypeStruct((s_len, width), F32),
        compiler_params=_params("parallel"),
    )(src, src, w, bias)


def _conv_bwd(dy, src, width, w, passthrough, *, name):
    s_len = src.shape[0]
    rows = min(CONV_ROWS, s_len)
    per = rows // HALO
    nblk = s_len // rows
    extra = [p.shape[1] for p in passthrough]
    total = width + sum(extra)

    def body(*refs):
        dy_ref, dyh_ref, x_ref, xh_ref, w_ref = refs[:5]
        p_refs = refs[5:5 + len(extra)]
        o_ref, dw_ref, db_ref = refs[5 + len(extra):]
        i = pl.program_id(0)

        @pl.when(i == 0)
        def _():
            dw_ref[...] = jnp.zeros_like(dw_ref)
            db_ref[...] = jnp.zeros_like(db_ref)

        dy = dy_ref[...]
        x = x_ref[...]
        dy_halo = jnp.where(i == nblk - 1, 0.0, dyh_ref[...])
        x_halo = jnp.where(i == 0, 0.0, xh_ref[...])
        dx = dy * w_ref[3:4, :]
        dw_ref[3] += jnp.sum(dy * x, axis=0, keepdims=True)
        for k in range(CONV_WIDTH - 1):
            dx += _shift_up(dy, dy_halo, 3 - k) * w_ref[k:k + 1, :]
            dw_ref[k] += jnp.sum(dy * _shift_down(x, x_halo, 3 - k), axis=0, keepdims=True)
        db_ref[...] += jnp.sum(dy, axis=0, keepdims=True)
        o_ref[:, 0:width] = dx.astype(o_ref.dtype)
        off = width
        for p_ref, wd in zip(p_refs, extra):
            o_ref[:, off:off + wd] = p_ref[...].astype(o_ref.dtype)
            off += wd

    in_specs = [pl.BlockSpec((rows, width), lambda i: (i, 0)),
                pl.BlockSpec((HALO, width), lambda i: (jnp.minimum((i + 1) * per, nblk * per - 1), 0)),
                pl.BlockSpec((rows, width), lambda i: (i, 0)),
                pl.BlockSpec((HALO, width), lambda i: (jnp.maximum(i * per - 1, 0), 0)),
                pl.BlockSpec((CONV_WIDTH, width), lambda i: (0, 0))]
    in_specs += [pl.BlockSpec((rows, wd), lambda i: (i, 0)) for wd in extra]
    return _call(
        body, name=name, grid=(nblk,), in_specs=in_specs,
        out_specs=[pl.BlockSpec((rows, total), lambda i: (i, 0)),
                   pl.BlockSpec((CONV_WIDTH, 1, width), lambda i: (0, 0, 0)), pl.BlockSpec((1, width), lambda i: (0, 0))],
        out_shape=[jax.ShapeDtypeStruct((s_len, total), MXU_DTYPE), jax.ShapeDtypeStruct((CONV_WIDTH, 1, width), F32),
                   jax.ShapeDtypeStruct((1, width), F32)],
        compiler_params=_params("arbitrary"),
    )(dy, dy, src, src, w, *passthrough)


def _attn_mask(first):
    i = _iota((A_BLOCK, 2 * A_BLOCK), 0)
    j = _iota((A_BLOCK, 2 * A_BLOCK), 1)
    band = (j > i) & (j <= i + A_BLOCK)
    return band & ((j >= A_BLOCK) | jnp.logical_not(first))


def _attn_group(p, mask, qg, kw, kws, vw, vws, azg, sink0, sink1):
    low = _iota(qg.shape, 1) < A_HEAD_DIM
    first_lane = (_iota((A_BLOCK, LANES), 1) == 0).astype(F32)
    out = None
    for half, sink in ((0, sink0), (1, sink1)):
        kv_head = (2 * p + half) // (A_HEADS // A_KV_HEADS)
        keep = low if half == 0 else jnp.logical_not(low)
        qm = jnp.where(keep, qg, 0.0)
        kk, vv = (kw, vw) if kv_head == half else (kws, vws)
        s = mm_nt(qm, kk) * (A_HEAD_DIM ** -0.5)
        s = jnp.where(mask, s, NEG)
        sk = jnp.sum(jnp.tile(sink, (A_BLOCK // 8, 1)) * first_lane, axis=1, keepdims=True)
        m = lax.stop_gradient(jnp.maximum(jnp.max(s, axis=1, keepdims=True), sk))
        e = jnp.exp(s - m)
        denom = jnp.sum(e, axis=1, keepdims=True) + jnp.exp(sk - m)
        o = mm_nn(e * (1.0 / denom), vv)
        o = jnp.where(keep, o, 0.0)
        out = o if out is None else out + o
    return out * _silu(azg)


def _attn_specs(s_len, rev):
    nb = s_len // A_BLOCK

    def cur(i):
        return nb - 1 - i if rev else i

    def prev(i):
        return jnp.maximum(cur(i) - 1, 0)

    def blk(width, col, which):
        return pl.BlockSpec((A_BLOCK, width), lambda i: (which(i), col))

    return [blk(A_WIDTH, 0, cur), blk(A_WIDTH, 1, cur), blk(LANES, 8, cur), blk(LANES, 9, cur),
            blk(LANES, 8, prev), blk(LANES, 9, prev), blk(LANES, 0, cur), blk(LANES, 0, cur),
            blk(LANES, 0, prev), blk(LANES, 0, prev)], cur


def _attn_fwd(proj_a, cos, sin, sinks_t, *, name):
    s_len = proj_a.shape[0]
    specs, _ = _attn_specs(s_len, False)

    def body(q_ref, az_ref, k_ref, v_ref, kp_ref, vp_ref, c_ref, s_ref, cp_ref, sp_ref, sink_ref, o_ref):
        first = pl.program_id(0) == 0
        mask = _attn_mask(first)
        qr = _rope(q_ref[...], c_ref[...], s_ref[...])
        kw = jnp.concatenate([_rope(kp_ref[...], cp_ref[...], sp_ref[...]), _rope(k_ref[...], c_ref[...], s_ref[...])], 0)
        vw = jnp.concatenate([vp_ref[...], v_ref[...]], 0)
        kws, vws = _swap64(kw), _swap64(vw)
        for p in range(A_WIDTH // LANES):
            cols = slice(p * LANES, (p + 1) * LANES)
            o = _attn_group(p, mask, qr[:, cols], kw, kws, vw, vws, az_ref[:, cols], sink_ref[2 * p], sink_ref[2 * p + 1])
            o_ref[:, cols] = o.astype(o_ref.dtype)

    return _call(
        body, name=name, grid=(s_len // A_BLOCK,),
        in_specs=specs + [pl.BlockSpec((A_HEADS, 8, LANES), lambda i: (0, 0, 0))],
        out_specs=pl.BlockSpec((A_BLOCK, A_WIDTH), lambda i: (i, 0)),
        out_shape=jax.ShapeDtypeStruct((s_len, A_WIDTH), MXU_DTYPE),
        compiler_params=_params("parallel"),
    )(proj_a, proj_a, proj_a, proj_a, proj_a, proj_a, cos, sin, cos, sin, sinks_t)


def _attn_bwd(proj_a, cos, sin, sinks_t, dya, *, name):
    s_len = proj_a.shape[0]
    specs, cur = _attn_specs(s_len, True)

    def body(q_ref, az_ref, k_ref, v_ref, kp_ref, vp_ref, c_ref, s_ref, cp_ref, sp_ref, sink_ref, dy_ref,
             o_ref, dsink_ref, dk_carry, dv_carry):
        i = pl.program_id(0)

        @pl.when(i == 0)
        def _():
            dsink_ref[...] = jnp.zeros_like(dsink_ref)
            dk_carry[...] = jnp.zeros_like(dk_carry)
            dv_carry[...] = jnp.zeros_like(dv_carry)

        first = cur(i) == 0
        mask = _attn_mask(first)
        cos_c, sin_c = c_ref[...], s_ref[...]
        qr = _rope(q_ref[...], cos_c, sin_c)
        kw = jnp.concatenate([_rope(kp_ref[...], cp_ref[...], sp_ref[...]), _rope(k_ref[...], cos_c, sin_c)], 0)
        vw = jnp.concatenate([vp_ref[...], v_ref[...]], 0)
        kws, vws = _swap64(kw), _swap64(vw)
        dkw = jnp.zeros_like(kw)
        dvw = jnp.zeros_like(vw)
        for p in range(A_WIDTH // LANES):
            cols = slice(p * LANES, (p + 1) * LANES)
            _, vjp = jax.vjp(functools.partial(_attn_group, p, mask), qr[:, cols], kw, kws, vw, vws, az_ref[:, cols],
                             sink_ref[2 * p], sink_ref[2 * p + 1])
            dq, dk1, dk2, dv1, dv2, daz, ds0, ds1 = vjp(dy_ref[:, cols])
            dkw += dk1 + _swap64(dk2)
            dvw += dv1 + _swap64(dv2)
            o_ref[:, cols] = _rope_t(dq, cos_c, sin_c).astype(o_ref.dtype)
            o_ref[:, A_WIDTH + p * LANES:A_WIDTH + (p + 1) * LANES] = daz.astype(o_ref.dtype)
            dsink_ref[2 * p] += ds0
            dsink_ref[2 * p + 1] += ds1
        o_ref[:, 2 * A_WIDTH:2 * A_WIDTH + LANES] = _rope_t(dkw[A_BLOCK:, :] + dk_carry[...], cos_c, sin_c).astype(o_ref.dtype)
        o_ref[:, 2 * A_WIDTH + LANES:] = (dvw[A_BLOCK:, :] + dv_carry[...]).astype(o_ref.dtype)
        dk_carry[...] = dkw[:A_BLOCK, :]
        dv_carry[...] = dvw[:A_BLOCK, :]

    return _call(
        body, name=name, grid=(s_len // A_BLOCK,),
        in_specs=specs + [pl.BlockSpec((A_HEADS, 8, LANES), lambda i: (0, 0, 0)),
                          pl.BlockSpec((A_BLOCK, A_WIDTH), lambda i: (cur(i), 0))],
        out_specs=[pl.BlockSpec((A_BLOCK, WA), lambda i: (cur(i), 0)),
                   pl.BlockSpec((A_HEADS, 8, LANES), lambda i: (0, 0, 0))],
        out_shape=[jax.ShapeDtypeStruct((s_len, WA), MXU_DTYPE), jax.ShapeDtypeStruct((A_HEADS, 8, LANES), F32)],
        scratch_shapes=[pltpu.VMEM((A_BLOCK, LANES), F32), pltpu.VMEM((A_BLOCK, LANES), F32)],
        compiler_params=_params("arbitrary"),
    )(proj_a, proj_a, proj_a, proj_a, proj_a, proj_a, cos, sin, cos, sin, sinks_t, dya)


RG_ROWS = 256


def _rg_gates(x, wa, ba, wx, bx, lam):
    r = jax.nn.sigmoid(mm_nn(x, wa) + ba)
    ig = jax.nn.sigmoid(mm_nn(x, wx) + bx)
    log_a = -R_C * r * _softplus(-lam)
    return jnp.exp(log_a), jnp.sqrt(_neg_expm1(2.0 * log_a)) * (ig * x)


def _rg_param_specs():
    mat = pl.BlockSpec((R_BLOCKS, R_BLOCK_DIM, R_BLOCK_DIM), lambda i: (0, 0, 0))
    vec = pl.BlockSpec((1, R_WIDTH), lambda i: (0, 0))
    return [mat, vec, mat, vec, vec]


def _rg_fwd(xr, proj_r, wa, ba, wx, bx, lam, *, name):
    s_len = xr.shape[0]
    rows = min(RG_ROWS, s_len)

    def body(x_ref, z_ref, wa_ref, ba_ref, wx_ref, bx_ref, lam_ref, h_ref, y_ref, a_buf, u_buf, carry):
        @pl.when(pl.program_id(0) == 0)
        def _():
            carry[...] = jnp.zeros_like(carry)

        for n in range(R_BLOCKS):
            cols = slice(n * R_BLOCK_DIM, (n + 1) * R_BLOCK_DIM)
            a, u = _rg_gates(x_ref[:, cols], wa_ref[n], ba_ref[:, cols], wx_ref[n], bx_ref[:, cols], lam_ref[:, cols])
            a_buf[:, cols] = a
            u_buf[:, cols] = u

        def step(t, h):
            h = a_buf[pl.ds(t, 1), :] * h + u_buf[pl.ds(t, 1), :]
            h_ref[pl.ds(t, 1), :] = h
            return h

        carry[...] = lax.fori_loop(0, rows, step, carry[...], unroll=8)
        y_ref[...] = (h_ref[...] * _silu(z_ref[...])).astype(y_ref.dtype)

    blk = pl.BlockSpec((rows, R_WIDTH), lambda i: (i, 0))
    return _call(
        body, name=name, grid=(s_len // rows,),
        in_specs=[blk, pl.BlockSpec((rows, R_WIDTH), lambda i: (i, 1))] + _rg_param_specs(),
        out_specs=[blk, blk],
        out_shape=[jax.ShapeDtypeStruct((s_len, R_WIDTH), F32), jax.ShapeDtypeStruct((s_len, R_WIDTH), MXU_DTYPE)],
        scratch_shapes=[pltpu.VMEM((rows, R_WIDTH), F32), pltpu.VMEM((rows, R_WIDTH), F32), pltpu.VMEM((1, R_WIDTH), F32)],
        compiler_params=_params("arbitrary"),
    )(xr, proj_r, wa, ba, wx, bx, lam)


def _rg_bwd(xr, proj_r, h, dyr, wa, ba, wx, bx, lam, *, name):
    s_len = xr.shape[0]
    rows = min(RG_ROWS, s_len)
    nblk = s_len // rows
    per = rows // HALO

    def cur(i):
        return nblk - 1 - i

    def body(x_ref, z_ref, h_ref, hh_ref, dy_ref, wa_ref, ba_ref, wx_ref, bx_ref, lam_ref,
             dx_ref, dz_ref, dwa_ref, dba_ref, dwx_ref, dbx_ref, dlam_ref, a_buf, g_buf, carry):
        i = pl.program_id(0)

        @pl.when(i == 0)
        def _():
            carry[...] = jnp.zeros_like(carry)
            for ref in (dwa_ref, dba_ref, dwx_ref, dbx_ref, dlam_ref):
                ref[...] = jnp.zeros_like(ref)

        z = z_ref[...]
        sig = jax.nn.sigmoid(z)
        hval = h_ref[...]
        dy = dy_ref[...]
        dz_ref[...] = dy * hval * (sig * (1.0 + z * (1.0 - sig)))
        g_buf[...] = dy * (z * sig)
        vjps = []
        for n in range(R_BLOCKS):
            cols = slice(n * R_BLOCK_DIM, (n + 1) * R_BLOCK_DIM)
            (a, _), vjp = jax.vjp(_rg_gates, x_ref[:, cols], wa_ref[n], ba_ref[:, cols], wx_ref[n], bx_ref[:, cols],
                                  lam_ref[:, cols])
            a_buf[:, cols] = a
            vjps.append(vjp)

        def step(k, c):
            t = rows - 1 - k
            g = g_buf[pl.ds(t, 1), :] + c
            g_buf[pl.ds(t, 1), :] = g
            return a_buf[pl.ds(t, 1), :] * g

        carry[...] = lax.fori_loop(0, rows, step, carry[...], unroll=8)
        h_halo = jnp.where(cur(i) == 0, 0.0, hh_ref[...])
        dh = g_buf[...]
        da = dh * _shift_down(hval, h_halo, 1)
        for n in range(R_BLOCKS):
            cols = slice(n * R_BLOCK_DIM, (n + 1) * R_BLOCK_DIM)
            dx, dwa, dba, dwx, dbx, dlam = vjps[n]((da[:, cols], dh[:, cols]))
            dx_ref[:, cols] = dx
            dwa_ref[n] += dwa
            dwx_ref[n] += dwx
            dba_ref[:, cols] += dba
            dbx_ref[:, cols] += dbx
            dlam_ref[:, cols] += dlam

    blk = pl.BlockSpec((rows, R_WIDTH), lambda i: (cur(i), 0))
    mat = pl.BlockSpec((R_BLOCKS, R_BLOCK_DIM, R_BLOCK_DIM), lambda i: (0, 0, 0))
    vec = pl.BlockSpec((1, R_WIDTH), lambda i: (0, 0))
    return _call(
        body, name=name, grid=(nblk,),
        in_specs=[blk, pl.BlockSpec((rows, R_WIDTH), lambda i: (cur(i), 1)), blk,
                  pl.BlockSpec((HALO, R_WIDTH), lambda i: (jnp.maximum(cur(i) * per - 1, 0), 0)), blk] + _rg_param_specs(),
        out_specs=[blk, blk, mat, vec, mat, vec, vec],
        out_shape=[jax.ShapeDtypeStruct((s_len, R_WIDTH), F32)] * 2 + [
            jax.ShapeDtypeStruct((R_BLOCKS, R_BLOCK_DIM, R_BLOCK_DIM), F32), jax.ShapeDtypeStruct((1, R_WIDTH), F32),
            jax.ShapeDtypeStruct((R_BLOCKS, R_BLOCK_DIM, R_BLOCK_DIM), F32), jax.ShapeDtypeStruct((1, R_WIDTH), F32),
            jax.ShapeDtypeStruct((1, R_WIDTH), F32)],
        scratch_shapes=[pltpu.VMEM((rows, R_WIDTH), F32), pltpu.VMEM((rows, R_WIDTH), F32), pltpu.VMEM((1, R_WIDTH), F32)],
        compiler_params=_params("arbitrary"),
    )(xr, proj_r, h, h, dyr, wa, ba, wx, bx, lam)


GP_CHUNKS = 2
GS_CHUNKS = 8


def _seg_cumsum(x, reverse):
    rows = x.shape[0]
    r = _iota(x.shape, 0) & (G_CHUNK - 1)
    s = 1
    while s < G_CHUNK:
        if reverse:
            x = x + jnp.where(r < G_CHUNK - s, pltpu.roll(x, rows - s, 0), 0.0)
        else:
            x = x + jnp.where(r >= s, pltpu.roll(x, s, 0), 0.0)
        s *= 2
    return x


def _gdn_decay(ga, a_log_row, dt_row):
    return -jnp.exp(a_log_row) * _softplus(ga + dt_row)


def _gdn_chunk(cq, ck, cv, gb, gc, inv=None):
    shape = cq.shape
    head = _iota(shape, 0) & (G_HEADS - 1)
    lane = _iota(shape, 2)
    q, k, v = _silu(cq), _silu(ck), _silu(cv)
    q = q * lax.rsqrt(jnp.sum(q * q, axis=-1, keepdims=True) + RMS_EPS) * (G_HEAD_DIM ** -0.5)
    k = k * lax.rsqrt(jnp.sum(k * k, axis=-1, keepdims=True) + RMS_EPS)
    beta = jnp.sum(jnp.where(lane == head, jax.nn.sigmoid(gb), 0.0), axis=-1, keepdims=True)
    g = jnp.sum(jnp.where(lane == head + G_HEADS, gc, 0.0), axis=-1, keepdims=True)
    sq = (shape[0], G_CHUNK, G_CHUNK)
    row, col = _iota(sq, 1), _iota(sq, 2)
    g_sq = jnp.broadcast_to(g, sq)
    decay = jnp.where(row >= col, jnp.exp(jnp.minimum(g_sq - _t(g_sq), 0.0)), 0.0)
    g_last = jnp.sum(jnp.where(_iota(g.shape, 1) == G_CHUNK - 1, g, 0.0), axis=1, keepdims=True)
    eg = jnp.exp(g)
    kb, vb = k * beta, v * beta
    m = jnp.where(row > col, mm_nt(kb, k) * decay, 0.0)
    known = inv is not None
    if not known:
        inv = _inv_unit_lower(m)
    u, w = _solve2(m, inv, vb, kb * eg)
    qk = jnp.where(row >= col, mm_nt(q, k) * decay, 0.0)
    q_dec = q * eg
    k_dec = k * jnp.exp(g_last - g)
    gl = jnp.broadcast_to(jnp.exp(g_last), (shape[0], 1, G_HEAD_DIM))
    return (u, w, qk, q_dec, k_dec, gl) if known else (u, w, qk, q_dec, k_dec, gl, inv)


def _gdn_step(state, u, w, qk, q_dec, k_dec, gl, gz, norm_w):
    v_new = u - mm_nn(w, state)
    o = mm_nn(q_dec, state) + mm_nn(qk, v_new)
    new_state = state * gl + mm_nn(_t(k_dec), v_new)
    o = o * lax.rsqrt(jnp.mean(o * o, axis=-1, keepdims=True) + RMS_EPS) * norm_w
    return o * _silu(gz), new_state


def _stack_chunks(x, heads):
    chunks = x.shape[0] // G_CHUNK
    parts = []
    for c in range(chunks):
        rows = slice(c * G_CHUNK, (c + 1) * G_CHUNK)
        for hd in range(G_HEADS):
            parts.append(x[rows, hd * LANES:(hd + 1) * LANES] if heads else x[rows, :])
    return jnp.stack(parts)


def _gdn_chunk_shapes(nch):
    b = nch * G_HEADS
    wide = jax.ShapeDtypeStruct((b, G_CHUNK, G_HEAD_DIM), F32)
    return [wide, wide, jax.ShapeDtypeStruct((b, G_CHUNK, G_CHUNK), F32), wide, wide,
            jax.ShapeDtypeStruct((b, 1, G_HEAD_DIM), F32)]


def _gdn_chunk_specs(nbatch):
    wide = pl.BlockSpec((nbatch, G_CHUNK, G_HEAD_DIM), lambda i: (i, 0, 0))
    return [wide, wide, pl.BlockSpec((nbatch, G_CHUNK, G_CHUNK), lambda i: (i, 0, 0)), wide, wide,
            pl.BlockSpec((nbatch, 1, G_HEAD_DIM), lambda i: (i, 0, 0))]


def _gdn_chunk_fwd(conv, proj_g, a_log_row, dt_row, *, name):
    s_len = conv.shape[0]
    cpg = min(GP_CHUNKS, s_len // G_CHUNK)
    rows = cpg * G_CHUNK
    nbatch = cpg * G_HEADS

    def body(c_ref, bg_ref, al_ref, dt_ref, *outs):
        bg = bg_ref[...]
        gc = _seg_cumsum(_gdn_decay(bg, al_ref[...], dt_ref[...]), False)
        res = _gdn_chunk(_stack_chunks(c_ref[:, 0:G_WIDTH], True), _stack_chunks(c_ref[:, G_WIDTH:2 * G_WIDTH], True),
                         _stack_chunks(c_ref[:, 2 * G_WIDTH:], True), _stack_chunks(bg, False), _stack_chunks(gc, False))
        for ref, val in zip(outs, res):
            ref[...] = val

    row = pl.BlockSpec((1, LANES), lambda i: (0, 0))
    return _call(
        body, name=name, grid=(s_len // rows,),
        in_specs=[pl.BlockSpec((rows, 3 * G_WIDTH), lambda i: (i, 0)),
                  pl.BlockSpec((rows, LANES), lambda i: (i, (3 * G_WIDTH + G_WIDTH) // LANES)), row, row],
        out_specs=_gdn_chunk_specs(nbatch) + [pl.BlockSpec((nbatch, G_CHUNK, G_CHUNK), lambda i: (i, 0, 0))],
        out_shape=_gdn_chunk_shapes(s_len // G_CHUNK) + [
            jax.ShapeDtypeStruct((s_len // G_CHUNK * G_HEADS, G_CHUNK, G_CHUNK), F32)],
        compiler_params=_params("parallel"),
    )(conv, proj_g, a_log_row, dt_row)


def _gdn_chunk_bwd(conv, proj_g, a_log_row, dt_row, inv, cots, *, name):
    s_len = conv.shape[0]
    cpg = min(GP_CHUNKS, s_len // G_CHUNK)
    rows = cpg * G_CHUNK
    nbatch = cpg * G_HEADS

    def unstack(x, heads):
        if heads:
            return jnp.concatenate([jnp.concatenate([x[c * G_HEADS + hd] for hd in range(G_HEADS)], axis=1)
                                    for c in range(cpg)], axis=0)
        return jnp.concatenate([sum(x[c * G_HEADS + hd] for hd in range(G_HEADS)) for c in range(cpg)], axis=0)

    def body(c_ref, bg_ref, al_ref, dt_ref, inv_ref, du, dw, dqk, dqd, dkd, dgl, dc_ref, dbg_ref, dal_ref, ddt_ref):
        @pl.when(pl.program_id(0) == 0)
        def _():
            dal_ref[...] = jnp.zeros_like(dal_ref)
            ddt_ref[...] = jnp.zeros_like(ddt_ref)

        bg = bg_ref[...]
        g_all, decay_vjp = jax.vjp(_gdn_decay, bg, al_ref[...], dt_ref[...])
        gc = _seg_cumsum(g_all, False)
        _, vjp = jax.vjp(_gdn_chunk, _stack_chunks(c_ref[:, 0:G_WIDTH], True),
                         _stack_chunks(c_ref[:, G_WIDTH:2 * G_WIDTH], True), _stack_chunks(c_ref[:, 2 * G_WIDTH:], True),
                         _stack_chunks(bg, False), _stack_chunks(gc, False), inv_ref[...])
        dq, dk, dv, dgb, dgc, _ = vjp((du[...], dw[...], dqk[...], dqd[...], dkd[...], dgl[...]))
        dc_ref[:, 0:G_WIDTH] = unstack(dq, True)
        dc_ref[:, G_WIDTH:2 * G_WIDTH] = unstack(dk, True)
        dc_ref[:, 2 * G_WIDTH:] = unstack(dv, True)
        dga, dal, ddt = decay_vjp(_seg_cumsum(unstack(dgc, False), True))
        dbg_ref[:, 0:LANES] = unstack(dgb, False) + dga
        dbg_ref[:, LANES:] = jnp.zeros((rows, LANES), F32)
        dal_ref[...] += dal
        ddt_ref[...] += ddt

    row = pl.BlockSpec((1, LANES), lambda i: (0, 0))
    return _call(
        body, name=name, grid=(s_len // rows,),
        in_specs=[pl.BlockSpec((rows, 3 * G_WIDTH), lambda i: (i, 0)),
                  pl.BlockSpec((rows, LANES), lambda i: (i, (3 * G_WIDTH + G_WIDTH) // LANES)), row, row,
                  pl.BlockSpec((nbatch, G_CHUNK, G_CHUNK), lambda i: (i, 0, 0))]
        + _gdn_chunk_specs(nbatch),
        out_specs=[pl.BlockSpec((rows, 3 * G_WIDTH), lambda i: (i, 0)), pl.BlockSpec((rows, 2 * LANES), lambda i: (i, 0)),
                   row, row],
        out_shape=[jax.ShapeDtypeStruct((s_len, 3 * G_WIDTH), F32), jax.ShapeDtypeStruct((s_len, 2 * LANES), F32),
                   jax.ShapeDtypeStruct((1, LANES), F32), jax.ShapeDtypeStruct((1, LANES), F32)],
        compiler_params=_params("arbitrary"),
    )(conv, proj_g, a_log_row, dt_row, inv, *cots)


def _gdn_scan_specs(cpg, which):
    nbatch = cpg * G_HEADS
    wide = pl.BlockSpec((nbatch, G_CHUNK, G_HEAD_DIM), lambda i: (which(i), 0, 0))
    return [wide, wide, pl.BlockSpec((nbatch, G_CHUNK, G_CHUNK), lambda i: (which(i), 0, 0)), wide, wide,
            pl.BlockSpec((nbatch, 1, G_HEAD_DIM), lambda i: (which(i), 0, 0))]


def _gz_stack(z_ref, c):
    rows = pl.ds(pl.multiple_of(c * G_CHUNK, G_CHUNK), G_CHUNK)
    return jnp.stack([z_ref[rows, hd * LANES:(hd + 1) * LANES] for hd in range(G_HEADS)])


def _gdn_scan_fwd(chunk_vals, proj_g, norm_w, *, name):
    s_len = proj_g.shape[0]
    nch = s_len // G_CHUNK
    cpg = min(GS_CHUNKS, nch)
    rows = cpg * G_CHUNK

    def body(u_ref, w_ref, qk_ref, qd_ref, kd_ref, gl_ref, z_ref, nw_ref, y_ref, st_ref, state):
        @pl.when(pl.program_id(0) == 0)
        def _():
            state[...] = jnp.zeros_like(state)

        def step(c, carry):
            b = pl.ds(pl.multiple_of(c * G_HEADS, G_HEADS), G_HEADS)
            st = state[...]
            st_ref[b] = st
            y, new_state = _gdn_step(st, u_ref[b], w_ref[b], qk_ref[b], qd_ref[b], kd_ref[b], gl_ref[b],
                                     _gz_stack(z_ref, c), nw_ref[...])
            state[...] = new_state
            rws = pl.ds(pl.multiple_of(c * G_CHUNK, G_CHUNK), G_CHUNK)
            for hd in range(G_HEADS):
                y_ref[rws, hd * LANES:(hd + 1) * LANES] = y[hd].astype(y_ref.dtype)
            return carry

        lax.fori_loop(0, cpg, step, 0)

    return _call(
        body, name=name, grid=(nch // cpg,),
        in_specs=_gdn_scan_specs(cpg, lambda i: i) + [
            pl.BlockSpec((rows, G_WIDTH), lambda i: (i, 3)), pl.BlockSpec((1, G_HEAD_DIM), lambda i: (0, 0))],
        out_specs=[pl.BlockSpec((rows, G_WIDTH), lambda i: (i, 0)),
                   pl.BlockSpec((cpg * G_HEADS, G_HEAD_DIM, G_HEAD_DIM), lambda i: (i, 0, 0))],
        out_shape=[jax.ShapeDtypeStruct((s_len, G_WIDTH), MXU_DTYPE),
                   jax.ShapeDtypeStruct((nch * G_HEADS, G_HEAD_DIM, G_HEAD_DIM), F32)],
        scratch_shapes=[pltpu.VMEM((G_HEADS, G_HEAD_DIM, G_HEAD_DIM), F32)],
        compiler_params=_params("arbitrary"),
    )(*chunk_vals, proj_g, norm_w)


def _gdn_scan_bwd(chunk_vals, states, proj_g, norm_w, dyg, *, name):
    s_len = proj_g.shape[0]
    nch = s_len // G_CHUNK
    cpg = min(GS_CHUNKS, nch)
    rows = cpg * G_CHUNK
    ngrid = nch // cpg

    def cur(i):
        return ngrid - 1 - i

    def body(u_ref, w_ref, qk_ref, qd_ref, kd_ref, gl_ref, st_ref, z_ref, nw_ref, dy_ref,
             du_ref, dw_ref, dqk_ref, dqd_ref, dkd_ref, dgl_ref, dz_ref, dnw_ref, dstate):
        @pl.when(pl.program_id(0) == 0)
        def _():
            dstate[...] = jnp.zeros_like(dstate)
            dnw_ref[...] = jnp.zeros_like(dnw_ref)

        def step(k, carry):
            c = cpg - 1 - k
            b = pl.ds(pl.multiple_of(c * G_HEADS, G_HEADS), G_HEADS)
            _, vjp = jax.vjp(_gdn_step, st_ref[b], u_ref[b], w_ref[b], qk_ref[b], qd_ref[b], kd_ref[b], gl_ref[b],
                             _gz_stack(z_ref, c), nw_ref[...])
            dst, du, dw, dqk, dqd, dkd, dgl, dz, dnw = vjp((_gz_stack(dy_ref, c), dstate[...]))
            dstate[...] = dst
            du_ref[b], dw_ref[b], dqk_ref[b], dqd_ref[b], dkd_ref[b], dgl_ref[b] = du, dw, dqk, dqd, dkd, dgl
            rws = pl.ds(pl.multiple_of(c * G_CHUNK, G_CHUNK), G_CHUNK)
            for hd in range(G_HEADS):
                dz_ref[rws, hd * LANES:(hd + 1) * LANES] = dz[hd]
            dnw_ref[...] += dnw
            return carry

        lax.fori_loop(0, cpg, step, 0)

    gate = pl.BlockSpec((rows, G_WIDTH), lambda i: (cur(i), 3))
    wide = pl.BlockSpec((rows, G_WIDTH), lambda i: (cur(i), 0))
    vec = pl.BlockSpec((1, G_HEAD_DIM), lambda i: (0, 0))
    return _call(
        body, name=name, grid=(ngrid,),
        in_specs=_gdn_scan_specs(cpg, cur) + [
            pl.BlockSpec((cpg * G_HEADS, G_HEAD_DIM, G_HEAD_DIM), lambda i: (cur(i), 0, 0)), gate, vec, wide],
        out_specs=_gdn_scan_specs(cpg, cur) + [wide, vec],
        out_shape=_gdn_chunk_shapes(nch) + [jax.ShapeDtypeStruct((s_len, G_WIDTH), F32),
                                            jax.ShapeDtypeStruct((1, G_HEAD_DIM), F32)],
        scratch_shapes=[pltpu.VMEM((G_HEADS, G_HEAD_DIM, G_HEAD_DIM), F32)],
        compiler_params=_params("arbitrary"),
    )(*chunk_vals, states, proj_g, norm_w, dyg)


def _adamw_math(w, g, m, v):
    m = ADAM_B1 * m + (1.0 - ADAM_B1) * g
    v = ADAM_B2 * v + (1.0 - ADAM_B2) * (g * g)
    m_hat = m / (1.0 - ADAM_B1 ** ADAM_STEP)
    v_hat = v / (1.0 - ADAM_B2 ** ADAM_STEP)
    delta = -ADAM_LR * (m_hat / (jnp.sqrt(v_hat) + ADAM_EPS) + ADAM_WD * w)
    return delta, m, v


def _sum_adamw(parts, w, m, v, *, name, rows):
    n_layers, n_rows, n_cols = w.shape
    rows = min(rows, n_rows)
    n_parts = parts.shape[0]

    def body(p_ref, w_ref, m_ref, v_ref, g_ref, d_ref, nm_ref, nv_ref):
        g = p_ref[0, 0].astype(F32)
        for k in range(1, n_parts):
            g = g + p_ref[k, 0].astype(F32)
        delta, new_m, new_v = _adamw_math(w_ref[0], g, m_ref[0], v_ref[0])
        g_ref[0], d_ref[0], nm_ref[0], nv_ref[0] = g, delta, new_m, new_v

    blk = pl.BlockSpec((1, rows, n_cols), lambda l, i: (l, i, 0))
    return _call(
        body, name=name, grid=(n_layers, n_rows // rows),
        in_specs=[pl.BlockSpec((n_parts, 1, rows, n_cols), lambda l, i: (0, l, i, 0)), blk, blk, blk],
        out_specs=[blk] * 4, out_shape=[jax.ShapeDtypeStruct(w.shape, F32)] * 4,
        compiler_params=_params("parallel", "parallel"),
    )(parts, w, m, v)


def _sum_slots(parts, *, name):
    rows = parts.shape[1]

    def body(p_ref, o_ref):
        g = p_ref[0]
        for k in range(1, N_DEV):
            g = g + p_ref[k]
        o_ref[...] = g

    return _call(body, name=name, grid=(1,),
                 in_specs=[pl.BlockSpec(parts.shape, lambda i: (0, 0, 0))],
                 out_specs=pl.BlockSpec((rows, LANES), lambda i: (0, 0)),
                 out_shape=jax.ShapeDtypeStruct((rows, LANES), F32), compiler_params=_params("arbitrary"))(parts)


def _adamw_packed(w, g, m, v, *, name):
    def body(w_ref, g_ref, m_ref, v_ref, d_ref, nm_ref, nv_ref):
        d_ref[...], nm_ref[...], nv_ref[...] = _adamw_math(w_ref[...], g_ref[...], m_ref[...], v_ref[...])

    blk = pl.BlockSpec(w.shape, lambda i: (0, 0))
    return _call(body, name=name, grid=(1,), in_specs=[blk] * 4, out_specs=[blk] * 3,
                 out_shape=[jax.ShapeDtypeStruct(w.shape, F32)] * 3, compiler_params=_params("arbitrary"))(w, g, m, v)


A_COLS = ((0, 512), (768, 1280), (512, 768))
R_COLS = ((1280, 3328),)
G_COLS = ((3328, 5384),)


def _group_weights(w_full):
    def take(ranges):
        return jnp.concatenate([w_full[:, a:b] for a, b in ranges], axis=1)

    w_g = jnp.concatenate([take(G_COLS), jnp.zeros((w_full.shape[0], G_PAD), w_full.dtype)], axis=1)
    return take(A_COLS), take(R_COLS), w_g


def _ungroup_grads(d_a, d_r, d_g):
    return jnp.concatenate([d_a[:, 0:512], d_a[:, 1024:1280], d_a[:, 512:1024], d_r, d_g[:, :WG - G_PAD]], axis=1)


def _rope_tables(s_len):
    inv = 1.0 / (ROPE_THETA ** (jnp.arange(0, A_HEAD_DIM, 2, dtype=F32) / A_HEAD_DIM))
    ang = jnp.arange(s_len, dtype=F32)[:, None] * inv[None, :]
    cos, sin = jnp.cos(ang), jnp.sin(ang)
    return jnp.tile(cos, (1, 4)), jnp.tile(jnp.concatenate([-sin, sin], axis=1), (1, 2))


def _pack(leaves):
    rows = []
    for leaf in leaves:
        flat = leaf.reshape(-1)
        pad = (-flat.shape[0]) % (8 * LANES)
        rows.append(jnp.pad(flat, (0, pad)).reshape(-1, LANES))
    return jnp.concatenate(rows, axis=0)


def _unpack(packed, shapes):
    out, row = [], 0
    for shape in shapes:
        size = math.prod(shape)
        nrows = -(-size // (8 * LANES)) * 8
        out.append(packed[row:row + nrows].reshape(-1)[:size].reshape(shape))
        row += nrows
    return out


def _lane_row(vals, offset):
    return jnp.pad(vals, (offset, LANES - offset - vals.shape[0])).reshape(1, LANES)


def kernel(x, w_in, sinks, r_conv_w, r_conv_b, r_wa, r_ba, r_wx, r_bx, r_lam, g_conv_w, g_a_log, g_dt_bias, g_norm_w, w_out, ln_g, ln_b, loss_target, m_w_in, m_sinks, m_r_conv_w, m_r_conv_b, m_r_wa, m_r_ba, m_r_wx, m_r_bx, m_r_lam, m_g_conv_w, m_g_a_log, m_g_dt_bias, m_g_norm_w, m_w_out, m_ln_g, m_ln_b, v_w_in, v_sinks, v_r_conv_w, v_r_conv_b, v_r_wa, v_r_ba, v_r_wx, v_r_bx, v_r_lam, v_g_conv_w, v_g_a_log, v_g_dt_bias, v_g_norm_w, v_w_out, v_ln_g, v_ln_b):
    s_len = x.shape[1]
    x0 = x.reshape(s_len, D_MODEL)
    target = loss_target.reshape(s_len, D_MODEL)
    me = 4 * lax.axis_index("x") + 2 * lax.axis_index("y") + lax.axis_index("c")

    win_all, wout_all, rcw_all, gcw_all = _all_gather(
        [w_in.astype(MXU_DTYPE), w_out.astype(MXU_DTYPE), r_conv_w, g_conv_w], "gather_weights")
    rcw_full = jnp.moveaxis(rcw_all, 0, 2).reshape(DEPTH, CONV_WIDTH, R_WIDTH)
    gcw_full = jnp.moveaxis(gcw_all, 0, 2).reshape(DEPTH, CONV_WIDTH, 3 * G_WIDTH)
    cos, sin = _rope_tables(s_len)

    layers = []
    for l in range(DEPTH):
        w_full = jnp.moveaxis(win_all[:, l], 0, 1).reshape(D_MODEL, N_IN)
        w_a, w_r, w_g = _group_weights(w_full)
        wo = wout_all[:, l].reshape(D_MODEL, D_MODEL)
        layers.append(dict(
            w_a=w_a, w_r=w_r, w_g=w_g, w_at=w_a.T, w_rt=w_r.T, w_gt=w_g.T, wo=wo,
            wo_at=wo[0:A_WIDTH].T, wo_rt=wo[A_WIDTH:A_WIDTH + R_WIDTH].T, wo_gt=wo[A_WIDTH + R_WIDTH:].T,
            sinks_t=jnp.broadcast_to(sinks[l][:, None, None], (A_HEADS, 8, LANES)),
            rcw=rcw_full[l], rcb=r_conv_b[l].reshape(1, R_WIDTH), wa=r_wa[l], ba=r_ba[l].reshape(1, R_WIDTH),
            wx=r_wx[l], bx=r_bx[l].reshape(1, R_WIDTH), lam=r_lam[l].reshape(1, R_WIDTH),
            gcw=gcw_full[l], zero_b=jnp.zeros((1, 3 * G_WIDTH), F32),
            a_log=_lane_row(g_a_log[l], G_HEADS), dt=_lane_row(g_dt_bias[l], G_HEADS),
            norm_w=g_norm_w[l].reshape(1, G_HEAD_DIM), ln_g=ln_g[l].reshape(1, D_MODEL), ln_b=ln_b[l].reshape(1, D_MODEL)))

    saved = []
    xin = xin_lo = x0
    for l, p in enumerate(layers):
        proj_a = _matmul([xin_lo], [p["w_a"]], name=f"proj_a{l}", tm=1024, tn=640)
        proj_r = _matmul([xin_lo], [p["w_r"]], name=f"proj_r{l}", tm=1024, tn=512)
        proj_g = _matmul([xin_lo], [p["w_g"]], name=f"proj_g{l}", tm=1024, tn=768)
        ya = _attn_fwd(proj_a, cos, sin, p["sinks_t"], name=f"attn_fwd{l}")
        xr = _conv_fwd(proj_r, R_WIDTH, p["rcw"], p["rcb"], name=f"rconv_fwd{l}")
        h, yr = _rg_fwd(xr, proj_r, p["wa"], p["ba"], p["wx"], p["bx"], p["lam"], name=f"rglru_fwd{l}")
        conv = _conv_fwd(proj_g, 3 * G_WIDTH, p["gcw"], p["zero_b"], name=f"gconv_fwd{l}")
        *chunk_vals, inv = _gdn_chunk_fwd(conv, proj_g, p["a_log"], p["dt"], name=f"gdn_chunk_fwd{l}")
        yg, states = _gdn_scan_fwd(chunk_vals, proj_g, p["norm_w"], name=f"gdn_scan_fwd{l}")
        z, xout, xout_lo = _outproj_ln(ya, yr, yg, p["wo"], xin, p["ln_g"], p["ln_b"], name=f"outproj_ln{l}")
        saved.append(dict(xin_lo=xin_lo, proj_a=proj_a, proj_r=proj_r, proj_g=proj_g, ya=ya, yr=yr, yg=yg, xr=xr, h=h,
                          conv=conv, chunk_vals=chunk_vals, inv=inv, states=states, z=z))
        xin, xin_lo = xout, xout_lo

    grads = [None] * DEPTH
    dxn = None
    loss_local = None
    for l in reversed(range(DEPTH)):
        p, sv = layers[l], saved[l]
        if dxn is None:
            dz, dz_lo, dln_g, dln_b, loss_local = _ln_bwd(sv["z"], p["ln_g"], name=f"ln_bwd{l}", xn=xin, target=target)
        else:
            dz, dz_lo, dln_g, dln_b = _ln_bwd(sv["z"], p["ln_g"], name=f"ln_bwd{l}", dxn=dxn)
        dya = _matmul([dz_lo], [p["wo_at"]], name=f"dya{l}", tm=1024, tn=512)
        dyr = _matmul([dz_lo], [p["wo_rt"]], name=f"dyr{l}", tm=1024, tn=512)
        dyg = _matmul([dz_lo], [p["wo_gt"]], name=f"dyg{l}", tm=1024, tn=512)
        dwo = jnp.concatenate([
            _matmul_tn(sv["ya"], dz_lo, name=f"dwo_a{l}", tm=512, tn=1024, tk=1024),
            _matmul_tn(sv["yr"], dz_lo, name=f"dwo_r{l}", tm=1024, tn=1024, tk=1024),
            _matmul_tn(sv["yg"], dz_lo, name=f"dwo_g{l}", tm=512, tn=1024, tk=1024)], axis=0)

        dproj_a, dsinks_t = _attn_bwd(sv["proj_a"], cos, sin, p["sinks_t"], dya, name=f"attn_bwd{l}")

        dxr, drz, dwa, dba, dwx, dbx, dlam = _rg_bwd(sv["xr"], sv["proj_r"], sv["h"], dyr, p["wa"], p["ba"], p["wx"],
                                                     p["bx"], p["lam"], name=f"rglru_bwd{l}")
        dproj_r, drcw, drcb = _conv_bwd(dxr, sv["proj_r"], R_WIDTH, p["rcw"], [drz], name=f"rconv_bwd{l}")

        scan_out = _gdn_scan_bwd(sv["chunk_vals"], sv["states"], sv["proj_g"], p["norm_w"], dyg, name=f"gdn_scan_bwd{l}")
        dgz, dnorm_w = scan_out[6], scan_out[7]
        dconv, dbg, dal, ddt = _gdn_chunk_bwd(sv["conv"], sv["proj_g"], p["a_log"], p["dt"], sv["inv"], scan_out[:6],
                                              name=f"gdn_chunk_bwd{l}")
        dproj_g, dgcw, _ = _conv_bwd(dconv, sv["proj_g"], 3 * G_WIDTH, p["gcw"], [dgz, dbg], name=f"gconv_bwd{l}")

        dxn = _matmul([dproj_a, dproj_r, dproj_g], [p["w_at"], p["w_rt"], p["w_gt"]], name=f"dx{l}", tm=512, tn=512,
                      add=dz, add_scale=DEEPNORM_ALPHA)
        dwin = _ungroup_grads(_matmul_tn(sv["xin_lo"], dproj_a, name=f"dwin_a{l}", tm=1024, tn=640, tk=1024),
                              _matmul_tn(sv["xin_lo"], dproj_r, name=f"dwin_r{l}", tm=1024, tn=1024, tk=1024),
                              _matmul_tn(sv["xin_lo"], dproj_g, name=f"dwin_g{l}", tm=1024, tn=1152, tk=1024))
        grads[l] = dict(
            w_in=dwin, w_out=dwo, sinks=dsinks_t[:, :, 0].sum(axis=1), r_conv_w=drcw.reshape(CONV_WIDTH, R_WIDTH),
            r_conv_b=drcb.reshape(R_WIDTH), r_wa=dwa, r_ba=dba.reshape(R_WIDTH), r_wx=dwx, r_bx=dbx.reshape(R_WIDTH),
            r_lam=dlam.reshape(R_WIDTH), g_conv_w=dgcw.reshape(CONV_WIDTH, 3 * G_WIDTH),
            g_a_log=dal[0, G_HEADS:2 * G_HEADS], g_dt_bias=ddt[0, G_HEADS:2 * G_HEADS],
            g_norm_w=dnorm_w.reshape(G_HEAD_DIM), ln_g=dln_g.reshape(D_MODEL), ln_b=dln_b.reshape(D_MODEL))
    grad_x = dxn.reshape(x.shape)
    loss = lax.psum(loss_local[0, 0], ("x", "y", "c"))

    def stacked(name):
        return jnp.stack([grads[l][name] for l in range(DEPTH)])

    dwin_blocks = jnp.transpose(stacked("w_in").reshape(DEPTH, D_MODEL, 4, 2, N_IN_SHARD), (3, 2, 0, 1, 4)).astype(MXU_DTYPE)
    dwout_blocks = jnp.transpose(stacked("w_out").reshape(DEPTH, 4, 2, OUT_SHARD, D_MODEL), (2, 1, 0, 3, 4)).astype(MXU_DTYPE)
    kept, got = _swap_cores([dwin_blocks, dwout_blocks], "swap_core_grads")
    chip_win = _add_pair(kept[0], got[0], name="add_core_grads_w_in", rows=256)
    chip_wout = _add_pair(kept[1], got[1], name="add_core_grads_w_out", rows=256)
    win_parts, wout_parts = _exchange_chips([chip_win, chip_wout], "exchange_chip_grads")
    g_w_in, d_w_in, nm_w_in, nv_w_in = _sum_adamw(win_parts, w_in, m_w_in, v_w_in, name="adamw_w_in", rows=256)
    g_w_out, d_w_out, nm_w_out, nv_w_out = _sum_adamw(wout_parts, w_out, m_w_out, v_w_out, name="adamw_w_out", rows=128)

    small = ["sinks", "r_conv_w", "r_conv_b", "r_wa", "r_ba", "r_wx", "r_bx", "r_lam", "g_conv_w", "g_a_log",
             "g_dt_bias", "g_norm_w", "ln_g", "ln_b"]
    full_shapes = [stacked(nm).shape for nm in small]
    (all_small,) = _all_gather([_pack([stacked(nm) for nm in small])], "gather_small_grads")
    g_small = dict(zip(small, _unpack(_sum_slots(all_small, name="sum_small_grads"), full_shapes)))
    g_small["r_conv_w"] = lax.dynamic_slice_in_dim(g_small["r_conv_w"], me * (R_WIDTH // N_DEV), R_WIDTH // N_DEV, axis=2)
    g_small["g_conv_w"] = lax.dynamic_slice_in_dim(g_small["g_conv_w"], me * (3 * G_WIDTH // N_DEV), 3 * G_WIDTH // N_DEV, axis=2)
    given = dict(sinks=(sinks, m_sinks, v_sinks), r_conv_w=(r_conv_w, m_r_conv_w, v_r_conv_w),
                 r_conv_b=(r_conv_b, m_r_conv_b, v_r_conv_b), r_wa=(r_wa, m_r_wa, v_r_wa), r_ba=(r_ba, m_r_ba, v_r_ba),
                 r_wx=(r_wx, m_r_wx, v_r_wx), r_bx=(r_bx, m_r_bx, v_r_bx), r_lam=(r_lam, m_r_lam, v_r_lam),
                 g_conv_w=(g_conv_w, m_g_conv_w, v_g_conv_w), g_a_log=(g_a_log, m_g_a_log, v_g_a_log),
                 g_dt_bias=(g_dt_bias, m_g_dt_bias, v_g_dt_bias), g_norm_w=(g_norm_w, m_g_norm_w, v_g_norm_w),
                 ln_g=(ln_g, m_ln_g, v_ln_g), ln_b=(ln_b, m_ln_b, v_ln_b))
    shard_shapes = [given[nm][0].shape for nm in small]
    packed = [_pack([given[nm][k] for nm in small]) for k in range(3)]
    d_p, nm_p, nv_p = _adamw_packed(packed[0], _pack([g_small[nm] for nm in small]), packed[1], packed[2], name="adamw_small")
    d_small = dict(zip(small, _unpack(d_p, shard_shapes)))
    nm_small = dict(zip(small, _unpack(nm_p, shard_shapes)))
    nv_small = dict(zip(small, _unpack(nv_p, shard_shapes)))

    order = ["w_in"] + small[:12] + ["w_out"] + small[12:]

    def leaf(big_in, big_out, table):
        return [big_in if nm == "w_in" else big_out if nm == "w_out" else table[nm] for nm in order]

    return (loss, grad_x, *leaf(g_w_in, g_w_out, g_small), *leaf(d_w_in, d_w_out, d_small),
            *leaf(nm_w_in, nm_w_out, nm_small), *leaf(nv_w_in, nv_w_out, nv_small))
```

```python
import functools
import math

import jax
import jax.numpy as jnp
from jax import lax
from jax.experimental import pallas as pl
from jax.experimental.pallas import tpu as pltpu

F32 = jnp.float32
MXU_DTYPE = jnp.bfloat16
HIGHEST = lax.Precision.HIGHEST
MESH_ID = pl.DeviceIdType.MESH

N_DEV = 8
DEPTH = 2
D_MODEL = 2048
A_HEADS, A_KV_HEADS, A_HEAD_DIM = 8, 2, 64
A_WIDTH, A_KV_WIDTH = 512, 128
A_BLOCK = 128
ROPE_THETA = 10000.0
R_WIDTH, R_BLOCKS, R_BLOCK_DIM = 1024, 8, 128
R_C = 8.0
CONV_WIDTH = 4
G_HEADS, G_HEAD_DIM, G_WIDTH, G_CHUNK = 4, 128, 512, 64
N_IN = 5384
N_IN_SHARD = N_IN // N_DEV
N_ROWS_PAD = 704
OUT_SHARD = D_MODEL // N_DEV
WA, WR, WG = 1280, 2048, 2304
G_PAD = WG - (3 * G_WIDTH + G_WIDTH + 2 * G_HEADS)
DEEPNORM_ALPHA = (2 * DEPTH) ** 0.25
LN_EPS = 1e-5
RMS_EPS = 1e-6
ADAM_LR, ADAM_B1, ADAM_B2, ADAM_EPS, ADAM_WD, ADAM_STEP = 0.001, 0.9, 0.999, 1e-08, 0.01, 10
NEG = -1e30
VMEM_LIMIT = 56 * 1024 * 1024
LANES = 128


def _call(body, **kw):
    return pl.pallas_call(body, **kw)


def _params(*sem):
    return pltpu.CompilerParams(dimension_semantics=sem, vmem_limit_bytes=VMEM_LIMIT)


def _t(x):
    return jnp.swapaxes(x, -1, -2)


def _raw_dot(a, b, ca, cb, precision=None):
    batch = tuple(range(a.ndim - 2))
    if precision is None:
        a, b = a.astype(MXU_DTYPE), b.astype(MXU_DTYPE)
    return lax.dot_general(a, b, (((ca,), (cb,)), (batch, batch)), precision=precision,
                           preferred_element_type=F32)


def _nn(a, b, precision=None):
    return _raw_dot(a, b, a.ndim - 1, b.ndim - 2, precision)


def _nt(a, b, precision=None):
    return _raw_dot(a, b, a.ndim - 1, b.ndim - 1, precision)


@jax.custom_vjp
def mm_nn(a, b):
    return _nn(a, b)


def _mm_nn_fwd(a, b):
    return _nn(a, b), (a, b)


def _mm_nn_bwd(res, g):
    a, b = res
    return _nt(g, b), _nn(_t(a), g)


mm_nn.defvjp(_mm_nn_fwd, _mm_nn_bwd)


@jax.custom_vjp
def mm_nt(a, b):
    return _nt(a, b)


def _mm_nt_fwd(a, b):
    return _nt(a, b), (a, b)


def _mm_nt_bwd(res, g):
    a, b = res
    return _nn(g, b), _nn(_t(g), a)


mm_nt.defvjp(_mm_nt_fwd, _mm_nt_bwd)


def _split(x):
    hi = x.astype(MXU_DTYPE)
    return hi, (x - hi.astype(F32)).astype(MXU_DTYPE)


def _hmm(a, b, nt=False):
    dot = _nt if nt else _nn
    return dot(a[0], b[0]) + (dot(a[0], b[1]) + dot(a[1], b[0]))


def _silu(x):
    return x * jax.nn.sigmoid(x)


def _softplus(x):
    return jnp.maximum(x, 0.0) + jnp.log1p(jnp.exp(-jnp.abs(x)))


def _neg_expm1(x):
    series = -x * (1.0 + x * 0.5 * (1.0 + x * (1.0 / 3.0) * (1.0 + x * 0.25 * (1.0 + x * 0.2))))
    return jnp.where(x > -0.125, series, 1.0 - jnp.exp(x))


def _iota(shape, dim):
    return lax.broadcasted_iota(jnp.int32, shape, dim)


def _inv_unit_lower(m):
    shape = m.shape
    row, col = _iota(shape, 1), _iota(shape, 2)
    eye = (row == col).astype(F32)

    def blockdiag(size):
        return (row // size) == (col // size)

    x = -jnp.where(blockdiag(8), m, 0.0)
    xs = _split(x)
    x2s = _split(_hmm(xs, xs))
    x4s = _split(_hmm(x2s, x2s))
    inv = eye + x
    inv = inv + _hmm(_split(inv), x2s)
    inv = inv + _hmm(_split(inv), x4s)
    for size in (8, 16, 32):
        below = jnp.where(blockdiag(2 * size) & jnp.logical_not(blockdiag(size)), m, 0.0)
        invs = _split(inv)
        inv = inv - _hmm(_split(_hmm(invs, _split(below))), invs)
    return inv


@jax.custom_vjp
def _solve2(m, inv, r1, r2):
    invs = _split(inv)
    return _hmm(invs, _split(r1)), _hmm(invs, _split(r2))


def _solve2_fwd(m, inv, r1, r2):
    x1, x2 = _solve2(m, inv, r1, r2)
    return (x1, x2), (inv, x1, x2)


def _solve2_bwd(res, g):
    inv, x1, x2 = res
    inv_ts = _split(_t(inv))
    d1, d2 = _hmm(inv_ts, _split(g[0])), _hmm(inv_ts, _split(g[1]))
    dm = -(_hmm(_split(d1), _split(x1), nt=True) + _hmm(_split(d2), _split(x2), nt=True))
    return dm, jnp.zeros_like(inv), d1, d2


_solve2.defvjp(_solve2_fwd, _solve2_bwd)


def _swap_halves(x):
    n = x.shape[-1]
    lane = _iota(x.shape, x.ndim - 1)
    return jnp.where((lane & 63) < 32, pltpu.roll(x, n - 32, x.ndim - 1), pltpu.roll(x, 32, x.ndim - 1))


def _rope(x, cos, sin):
    reps = x.shape[-1] // LANES
    if reps > 1:
        cos, sin = jnp.tile(cos, (1, reps)), jnp.tile(sin, (1, reps))
    return x * cos + _swap_halves(x) * sin


def _rope_t(d, cos, sin):
    reps = d.shape[-1] // LANES
    if reps > 1:
        cos, sin = jnp.tile(cos, (1, reps)), jnp.tile(sin, (1, reps))
    return d * cos + _swap_halves(d * sin)


def _swap64(x):
    return pltpu.roll(x, 64, x.ndim - 1)


def _mesh_pos():
    return lax.axis_index("x"), lax.axis_index("y"), lax.axis_index("c")


def _all_gather(arrays, name):
    n = len(arrays)

    def body(*refs):
        ins, outs = refs[:n], refs[n:2 * n]
        send_sems, recv_sems, local_sem = refs[2 * n:]
        x, y, c = _mesh_pos()
        me, sibling = (x, y, c), (x, y, 1 - c)
        chips = [(1 - x, y), (x, 1 - y), (1 - x, 1 - y)]

        def slot(ref, pos):
            return ref.at[4 * pos[0] + 2 * pos[1] + pos[2]]

        def copy(a, k, block, to, src=None):
            return pltpu.make_async_remote_copy(
                src_ref=slot(outs[a], block) if src is None else src, dst_ref=slot(outs[a], block),
                send_sem=send_sems.at[a, k], recv_sem=recv_sems.at[a, k], device_id=to, device_id_type=MESH_ID)

        mine = [pltpu.make_async_copy(ins[a], slot(outs[a], me), local_sem.at[a]) for a in range(n)]
        for cp in mine:
            cp.start()
        first = []
        for a in range(n):
            first.append(copy(a, 0, me, sibling, src=ins[a]))
            first += [copy(a, 1 + j, me, (*chip, c), src=ins[a]) for j, chip in enumerate(chips)]
        for cp in first:
            cp.start()
        passed = []
        for a in range(n):
            for j, chip in enumerate(chips):
                copy(a, 1 + j, (*chip, c), me).wait_recv()
                cp = copy(a, 4 + j, (*chip, c), sibling)
                cp.start()
                passed.append(cp)
        for a in range(n):
            copy(a, 0, sibling, me).wait_recv()
            for j, chip in enumerate(chips):
                copy(a, 4 + j, (*chip, 1 - c), me).wait_recv()
        for cp in first + passed:
            cp.wait_send()
        for cp in mine:
            cp.wait()

    any_spec = pl.BlockSpec(memory_space=pl.ANY)
    return _call(
        body, name=name,
        out_shape=[jax.ShapeDtypeStruct((N_DEV,) + a.shape, a.dtype) for a in arrays],
        in_specs=[any_spec] * n, out_specs=[any_spec] * n,
        scratch_shapes=[pltpu.SemaphoreType.DMA((n, 7)), pltpu.SemaphoreType.DMA((n, 7)),
                        pltpu.SemaphoreType.DMA((n,))],
    )(*arrays)


def _swap_cores(arrays, name):
    n = len(arrays)

    def body(*refs):
        ins, kept, got = refs[:n], refs[n:2 * n], refs[2 * n:3 * n]
        send_sems, recv_sems, local_sem = refs[3 * n:]
        x, y, c = _mesh_pos()
        mine = [pltpu.make_async_copy(ins[a].at[c], kept[a], local_sem.at[a]) for a in range(n)]
        for cp in mine:
            cp.start()
        copies = [pltpu.make_async_remote_copy(
            src_ref=ins[a].at[1 - c], dst_ref=got[a], send_sem=send_sems.at[a], recv_sem=recv_sems.at[a],
            device_id=(x, y, 1 - c), device_id_type=MESH_ID) for a in range(n)]
        for cp in copies:
            cp.start()
        for cp in copies:
            cp.wait()
        for cp in mine:
            cp.wait()

    any_spec = pl.BlockSpec(memory_space=pl.ANY)
    slabs = [jax.ShapeDtypeStruct(a.shape[1:], a.dtype) for a in arrays]
    outs = _call(
        body, name=name, out_shape=slabs + slabs, in_specs=[any_spec] * n, out_specs=[any_spec] * (2 * n),
        scratch_shapes=[pltpu.SemaphoreType.DMA((n,)), pltpu.SemaphoreType.DMA((n,)), pltpu.SemaphoreType.DMA((n,))],
    )(*arrays)
    return outs[:n], outs[n:]


def _exchange_chips(arrays, name):
    n = len(arrays)

    def body(*refs):
        ins, outs = refs[:n], refs[n:2 * n]
        send_sems, recv_sems, local_sem = refs[2 * n:]
        x, y, c = _mesh_pos()
        mine = [pltpu.make_async_copy(ins[a].at[2 * x + y], outs[a].at[0], local_sem.at[a]) for a in range(n)]
        for cp in mine:
            cp.start()
        copies = []
        for a in range(n):
            for k in range(1, 4):
                px, py = x ^ (k >> 1), y ^ (k & 1)
                copies.append(pltpu.make_async_remote_copy(
                    src_ref=ins[a].at[2 * px + py], dst_ref=outs[a].at[k], send_sem=send_sems.at[a, k - 1],
                    recv_sem=recv_sems.at[a, k - 1], device_id=(px, py, c), device_id_type=MESH_ID))
        for cp in copies:
            cp.start()
        for cp in copies:
            cp.wait_recv()
        for cp in copies:
            cp.wait_send()
        for cp in mine:
            cp.wait()

    any_spec = pl.BlockSpec(memory_space=pl.ANY)
    return _call(
        body, name=name,
        out_shape=[jax.ShapeDtypeStruct(a.shape, a.dtype) for a in arrays],
        in_specs=[any_spec] * n, out_specs=[any_spec] * n,
        scratch_shapes=[pltpu.SemaphoreType.DMA((n, 3)), pltpu.SemaphoreType.DMA((n, 3)),
                        pltpu.SemaphoreType.DMA((n,))],
    )(*arrays)


def _add_pair(a, b, *, name, rows):
    rows = min(rows, a.shape[2])

    def body(a_ref, b_ref, o_ref):
        o_ref[...] = (a_ref[...].astype(F32) + b_ref[...].astype(F32)).astype(o_ref.dtype)

    blk = pl.BlockSpec((1, 1, rows, a.shape[3]), lambda s, l, i: (s, l, i, 0))
    return _call(body, name=name, grid=(a.shape[0], a.shape[1], a.shape[2] // rows), in_specs=[blk, blk], out_specs=blk,
                 out_shape=jax.ShapeDtypeStruct(a.shape, a.dtype),
                 compiler_params=_params("parallel", "parallel", "parallel"))(a, b)


def _matmul(a_list, b_list, *, name, tm, tn, b_t=False, out_dtype=F32, add=None, add_scale=1.0):
    n = len(a_list)
    m_rows, n_cols = a_list[0].shape[0], b_list[0].shape[0 if b_t else 1]
    tm, tn = min(tm, m_rows), min(tn, n_cols)

    def body(*refs):
        a_refs, b_refs = refs[:n], refs[n:2 * n]
        o_ref = refs[-1]
        acc = None
        for a_ref, b_ref in zip(a_refs, b_refs):
            part = lax.dot_general(a_ref[...].astype(MXU_DTYPE), b_ref[...].astype(MXU_DTYPE),
                                   (((1,), (1 if b_t else 0,)), ((), ())), preferred_element_type=F32)
            acc = part if acc is None else acc + part
        if add is not None:
            acc = acc + add_scale * refs[2 * n][...]
        o_ref[...] = acc.astype(o_ref.dtype)

    in_specs = [pl.BlockSpec((tm, a.shape[1]), lambda i, j: (i, 0)) for a in a_list]
    if b_t:
        in_specs += [pl.BlockSpec((tn, b.shape[1]), lambda i, j: (j, 0)) for b in b_list]
    else:
        in_specs += [pl.BlockSpec((b.shape[0], tn), lambda i, j: (0, j)) for b in b_list]
    args = list(a_list) + list(b_list)
    if add is not None:
        in_specs.append(pl.BlockSpec((tm, tn), lambda i, j: (i, j)))
        args.append(add)
    return _call(
        body, name=name, grid=(m_rows // tm, n_cols // tn), in_specs=in_specs,
        out_specs=pl.BlockSpec((tm, tn), lambda i, j: (i, j)),
        out_shape=jax.ShapeDtypeStruct((m_rows, n_cols), out_dtype),
        compiler_params=_params("parallel", "arbitrary"),
    )(*args)


def _matmul_tn(a, b, *, name, tm, tn, tk):
    k_rows, m_rows = a.shape
    n_cols = b.shape[1]
    tm, tn, tk = min(tm, m_rows), min(tn, n_cols), min(tk, k_rows)
    nk = k_rows // tk

    def body(a_ref, b_ref, o_ref):
        part = lax.dot_general(a_ref[...].astype(MXU_DTYPE), b_ref[...].astype(MXU_DTYPE),
                               (((0,), (0,)), ((), ())), preferred_element_type=F32)

        @pl.when(pl.program_id(2) == 0)
        def _():
            o_ref[...] = part

        @pl.when(pl.program_id(2) > 0)
        def _():
            o_ref[...] += part

    return _call(
        body, name=name, grid=(m_rows // tm, n_cols // tn, nk),
        in_specs=[pl.BlockSpec((tk, tm), lambda i, j, k: (k, i)), pl.BlockSpec((tk, tn), lambda i, j, k: (k, j))],
        out_specs=pl.BlockSpec((tm, tn), lambda i, j, k: (i, j)),
        out_shape=jax.ShapeDtypeStruct((m_rows, n_cols), F32),
        compiler_params=_params("parallel", "parallel", "arbitrary"),
    )(a, b)


def _outproj_ln(ya, yr, yg, w_out, x, ln_g, ln_b, *, name):
    s_len = x.shape[0]
    tm = min(256, s_len)

    def body(ya_ref, yr_ref, yg_ref, w_ref, x_ref, g_ref, b_ref, z_ref, o_ref, lo_ref):
        acc = jnp.dot(ya_ref[...], w_ref[0:A_WIDTH, :], preferred_element_type=F32)
        acc += jnp.dot(yr_ref[...], w_ref[A_WIDTH:A_WIDTH + R_WIDTH, :], preferred_element_type=F32)
        acc += jnp.dot(yg_ref[...], w_ref[A_WIDTH + R_WIDTH:, :], preferred_element_type=F32)
        z = DEEPNORM_ALPHA * x_ref[...] + acc
        z_ref[...] = z
        mu = jnp.mean(z, axis=-1, keepdims=True)
        zc = z - mu
        var = jnp.mean(zc * zc, axis=-1, keepdims=True)
        out = zc * lax.rsqrt(var + LN_EPS) * g_ref[...] + b_ref[...]
        o_ref[...] = out
        lo_ref[...] = out.astype(lo_ref.dtype)

    def rows(width):
        return pl.BlockSpec((tm, width), lambda i: (i, 0))

    def whole(shape):
        return pl.BlockSpec(shape, lambda i: (0, 0))

    return _call(
        body, name=name, grid=(s_len // tm,),
        in_specs=[rows(A_WIDTH), rows(R_WIDTH), rows(G_WIDTH), whole((D_MODEL, D_MODEL)), rows(D_MODEL),
                  whole((1, D_MODEL)), whole((1, D_MODEL))],
        out_specs=[rows(D_MODEL)] * 3,
        out_shape=[jax.ShapeDtypeStruct((s_len, D_MODEL), F32)] * 2 + [jax.ShapeDtypeStruct((s_len, D_MODEL), MXU_DTYPE)],
        compiler_params=_params("parallel"),
    )(ya, yr, yg, w_out, x, ln_g, ln_b)


def _ln_bwd(z, ln_g, *, name, dxn=None, xn=None, target=None):
    s_len = z.shape[0]
    tm = min(256, s_len)
    top = dxn is None

    def body(*refs):
        if top:
            z_ref, g_ref, xn_ref, t_ref, dz_ref, lo_ref, dg_ref, db_ref, loss_ref = refs
            err = xn_ref[...] - t_ref[...]
            dy = err * (1.0 / D_MODEL)
        else:
            z_ref, g_ref, dy_ref, dz_ref, lo_ref, dg_ref, db_ref = refs
            dy = dy_ref[...]
        first = pl.program_id(0) == 0

        @pl.when(first)
        def _():
            dg_ref[...] = jnp.zeros_like(dg_ref)
            db_ref[...] = jnp.zeros_like(db_ref)
            if top:
                loss_ref[...] = jnp.zeros_like(loss_ref)

        z = z_ref[...]
        mu = jnp.mean(z, axis=-1, keepdims=True)
        zc = z - mu
        rstd = lax.rsqrt(jnp.mean(zc * zc, axis=-1, keepdims=True) + LN_EPS)
        xhat = zc * rstd
        dxh = dy * g_ref[...]
        dz = rstd * (dxh - jnp.mean(dxh, axis=-1, keepdims=True) - xhat * jnp.mean(dxh * xhat, axis=-1, keepdims=True))
        dz_ref[...] = dz
        lo_ref[...] = dz.astype(lo_ref.dtype)
        dg_ref[...] += jnp.sum(dy * xhat, axis=0, keepdims=True)
        db_ref[...] += jnp.sum(dy, axis=0, keepdims=True)
        if top:
            per_row = jnp.sum(err * err, axis=-1, keepdims=True) * (0.5 / D_MODEL)
            loss_ref[...] += jnp.sum(per_row, axis=0, keepdims=True)

    rows = pl.BlockSpec((tm, D_MODEL), lambda i: (i, 0))
    vec = pl.BlockSpec((1, D_MODEL), lambda i: (0, 0))
    in_specs = [rows, vec] + ([rows, rows] if top else [rows])
    args = [z, ln_g] + ([xn, target] if top else [dxn])
    out_specs = [rows, rows, vec, vec]
    out_shape = [jax.ShapeDtypeStruct((s_len, D_MODEL), F32), jax.ShapeDtypeStruct((s_len, D_MODEL), MXU_DTYPE),
                 jax.ShapeDtypeStruct((1, D_MODEL), F32), jax.ShapeDtypeStruct((1, D_MODEL), F32)]
    if top:
        out_specs.append(pl.BlockSpec((1, 1), lambda i: (0, 0)))
        out_shape.append(jax.ShapeDtypeStruct((1, 1), F32))
    return _call(body, name=name, grid=(s_len // tm,), in_specs=in_specs, out_specs=out_specs,
                 out_shape=out_shape, compiler_params=_params("arbitrary"))(*args)


CONV_ROWS = 256
HALO = 8


def _shift_down(x, halo, s):
    if s == 0:
        return x
    ext = jnp.concatenate([halo, x], axis=0)
    return pltpu.roll(ext, s, 0)[HALO:, :]


def _shift_up(x, halo, s):
    if s == 0:
        return x
    ext = jnp.concatenate([x, halo], axis=0)
    return pltpu.roll(ext, ext.shape[0] - s, 0)[:x.shape[0], :]


def _conv_fwd(src, width, w, bias, *, name):
    s_len = src.shape[0]
    rows = min(CONV_ROWS, s_len)
    per = rows // HALO

    def body(x_ref, halo_ref, w_ref, b_ref, o_ref):
        x = x_ref[...]
        halo = jnp.where(pl.program_id(0) == 0, 0.0, halo_ref[...])
        acc = x * w_ref[3:4, :] + b_ref[...]
        for k in range(CONV_WIDTH - 1):
            acc += _shift_down(x, halo, 3 - k) * w_ref[k:k + 1, :]
        o_ref[...] = acc

    return _call(
        body, name=name, grid=(s_len // rows,),
        in_specs=[pl.BlockSpec((rows, width), lambda i: (i, 0)),
                  pl.BlockSpec((HALO, width), lambda i: (jnp.maximum(i * per - 1, 0), 0)),
                  pl.BlockSpec((CONV_WIDTH, width), lambda i: (0, 0)), pl.BlockSpec((1, width), lambda i: (0, 0))],
        out_specs=pl.BlockSpec((rows, width), lambda i: (i, 0)),
        out_shape=jax.ShapeDtypeStruct((s_len, width), F32),
        compiler_params=_params("parallel"),
    )(src, src, w, bias)


def _conv_bwd(dy, src, width, w, passthrough, *, name):
    s_len = src.shape[0]
    rows = min(CONV_ROWS, s_len)
    per = rows // HALO
    nblk = s_len // rows
    extra = [p.shape[1] for p in passthrough]
    total = width + sum(extra)

    def body(*refs):
        dy_ref, dyh_ref, x_ref, xh_ref, w_ref = refs[:5]
        p_refs = refs[5:5 + len(extra)]
        o_ref, dw_ref, db_ref = refs[5 + len(extra):]
        i = pl.program_id(0)

        @pl.when(i == 0)
        def _():
            dw_ref[...] = jnp.zeros_like(dw_ref)
            db_ref[...] = jnp.zeros_like(db_ref)

        dy = dy_ref[...]
        x = x_ref[...]
        dy_halo = jnp.where(i == nblk - 1, 0.0, dyh_ref[...])
        x_halo = jnp.where(i == 0, 0.0, xh_ref[...])
        dx = dy * w_ref[3:4, :]
        dw_ref[3] += jnp.sum(dy * x, axis=0, keepdims=True)
        for k in range(CONV_WIDTH - 1):
            dx += _shift_up(dy, dy_halo, 3 - k) * w_ref[k:k + 1, :]
            dw_ref[k] += jnp.sum(dy * _shift_down(x, x_halo, 3 - k), axis=0, keepdims=True)
        db_ref[...] += jnp.sum(dy, axis=0, keepdims=True)
        o_ref[:, 0:width] = dx.astype(o_ref.dtype)
        off = width
        for p_ref, wd in zip(p_refs, extra):
            o_ref[:, off:off + wd] = p_ref[...].astype(o_ref.dtype)
            off += wd

    in_specs = [pl.BlockSpec((rows, width), lambda i: (i, 0)),
                pl.BlockSpec((HALO, width), lambda i: (jnp.minimum((i + 1) * per, nblk * per - 1), 0)),
                pl.BlockSpec((rows, width), lambda i: (i, 0)),
                pl.BlockSpec((HALO, width), lambda i: (jnp.maximum(i * per - 1, 0), 0)),
                pl.BlockSpec((CONV_WIDTH, width), lambda i: (0, 0))]
    in_specs += [pl.BlockSpec((rows, wd), lambda i: (i, 0)) for wd in extra]
    return _call(
        body, name=name, grid=(nblk,), in_specs=in_specs,
        out_specs=[pl.BlockSpec((rows, total), lambda i: (i, 0)),
                   pl.BlockSpec((CONV_WIDTH, 1, width), lambda i: (0, 0, 0)), pl.BlockSpec((1, width), lambda i: (0, 0))],
        out_shape=[jax.ShapeDtypeStruct((s_len, total), MXU_DTYPE), jax.ShapeDtypeStruct((CONV_WIDTH, 1, width), F32),
                   jax.ShapeDtypeStruct((1, width), F32)],
        compiler_params=_params("arbitrary"),
    )(dy, dy, src, src, w, *passthrough)


def _attn_mask(first):
    i = _iota((A_BLOCK, 2 * A_BLOCK), 0)
    j = _iota((A_BLOCK, 2 * A_BLOCK), 1)
    band = (j > i) & (j <= i + A_BLOCK)
    return band & ((j >= A_BLOCK) | jnp.logical_not(first))


def _attn_group(p, mask, qg, kw, kws, vw, vws, azg, sink0, sink1):
    low = _iota(qg.shape, 1) < A_HEAD_DIM
    first_lane = (_iota((A_BLOCK, LANES), 1) == 0).astype(F32)
    out = None
    for half, sink in ((0, sink0), (1, sink1)):
        kv_head = (2 * p + half) // (A_HEADS // A_KV_HEADS)
        keep = low if half == 0 else jnp.logical_not(low)
        qm = jnp.where(keep, qg, 0.0)
        kk, vv = (kw, vw) if kv_head == half else (kws, vws)
        s = mm_nt(qm, kk) * (A_HEAD_DIM ** -0.5)
        s = jnp.where(mask, s, NEG)
        sk = jnp.sum(jnp.tile(sink, (A_BLOCK // 8, 1)) * first_lane, axis=1, keepdims=True)
        m = lax.stop_gradient(jnp.maximum(jnp.max(s, axis=1, keepdims=True), sk))
        e = jnp.exp(s - m)
        denom = jnp.sum(e, axis=1, keepdims=True) + jnp.exp(sk - m)
        o = mm_nn(e * (1.0 / denom), vv)
        o = jnp.where(keep, o, 0.0)
        out = o if out is None else out + o
    return out * _silu(azg)


def _attn_specs(s_len, rev):
    nb = s_len // A_BLOCK

    def cur(i):
        return nb - 1 - i if rev else i

    def prev(i):
        return jnp.maximum(cur(i) - 1, 0)

    def blk(width, col, which):
        return pl.BlockSpec((A_BLOCK, width), lambda i: (which(i), col))

    return [blk(A_WIDTH, 0, cur), blk(A_WIDTH, 1, cur), blk(LANES, 8, cur), blk(LANES, 9, cur),
            blk(LANES, 8, prev), blk(LANES, 9, prev), blk(LANES, 0, cur), blk(LANES, 0, cur),
            blk(LANES, 0, prev), blk(LANES, 0, prev)], cur


def _attn_fwd(proj_a, cos, sin, sinks_t, *, name):
    s_len = proj_a.shape[0]
    specs, _ = _attn_specs(s_len, False)

    def body(q_ref, az_ref, k_ref, v_ref, kp_ref, vp_ref, c_ref, s_ref, cp_ref, sp_ref, sink_ref, o_ref):
        first = pl.program_id(0) == 0
        mask = _attn_mask(first)
        qr = _rope(q_ref[...], c_ref[...], s_ref[...])
        kw = jnp.concatenate([_rope(kp_ref[...], cp_ref[...], sp_ref[...]), _rope(k_ref[...], c_ref[...], s_ref[...])], 0)
        vw = jnp.concatenate([vp_ref[...], v_ref[...]], 0)
        kws, vws = _swap64(kw), _swap64(vw)
        for p in range(A_WIDTH // LANES):
            cols = slice(p * LANES, (p + 1) * LANES)
            o = _attn_group(p, mask, qr[:, cols], kw, kws, vw, vws, az_ref[:, cols], sink_ref[2 * p], sink_ref[2 * p + 1])
            o_ref[:, cols] = o.astype(o_ref.dtype)

    return _call(
        body, name=name, grid=(s_len // A_BLOCK,),
        in_specs=specs + [pl.BlockSpec((A_HEADS, 8, LANES), lambda i: (0, 0, 0))],
        out_specs=pl.BlockSpec((A_BLOCK, A_WIDTH), lambda i: (i, 0)),
        out_shape=jax.ShapeDtypeStruct((s_len, A_WIDTH), MXU_DTYPE),
        compiler_params=_params("parallel"),
    )(proj_a, proj_a, proj_a, proj_a, proj_a, proj_a, cos, sin, cos, sin, sinks_t)


def _attn_bwd(proj_a, cos, sin, sinks_t, dya, *, name):
    s_len = proj_a.shape[0]
    specs, cur = _attn_specs(s_len, True)

    def body(q_ref, az_ref, k_ref, v_ref, kp_ref, vp_ref, c_ref, s_ref, cp_ref, sp_ref, sink_ref, dy_ref,
             o_ref, dsink_ref, dk_carry, dv_carry):
        i = pl.program_id(0)

        @pl.when(i == 0)
        def _():
            dsink_ref[...] = jnp.zeros_like(dsink_ref)
            dk_carry[...] = jnp.zeros_like(dk_carry)
            dv_carry[...] = jnp.zeros_like(dv_carry)

        first = cur(i) == 0
        mask = _attn_mask(first)
        cos_c, sin_c = c_ref[...], s_ref[...]
        qr = _rope(q_ref[...], cos_c, sin_c)
        kw = jnp.concatenate([_rope(kp_ref[...], cp_ref[...], sp_ref[...]), _rope(k_ref[...], cos_c, sin_c)], 0)
        vw = jnp.concatenate([vp_ref[...], v_ref[...]], 0)
        kws, vws = _swap64(kw), _swap64(vw)
        dkw = jnp.zeros_like(kw)
        dvw = jnp.zeros_like(vw)
        for p in range(A_WIDTH // LANES):
            cols = slice(p * LANES, (p + 1) * LANES)
            _, vjp = jax.vjp(functools.partial(_attn_group, p, mask), qr[:, cols], kw, kws, vw, vws, az_ref[:, cols],
                             sink_ref[2 * p], sink_ref[2 * p + 1])
            dq, dk1, dk2, dv1, dv2, daz, ds0, ds1 = vjp(dy_ref[:, cols])
            dkw += dk1 + _swap64(dk2)
            dvw += dv1 + _swap64(dv2)
            o_ref[:, cols] = _rope_t(dq, cos_c, sin_c).astype(o_ref.dtype)
            o_ref[:, A_WIDTH + p * LANES:A_WIDTH + (p + 1) * LANES] = daz.astype(o_ref.dtype)
            dsink_ref[2 * p] += ds0
            dsink_ref[2 * p + 1] += ds1
        o_ref[:, 2 * A_WIDTH:2 * A_WIDTH + LANES] = _rope_t(dkw[A_BLOCK:, :] + dk_carry[...], cos_c, sin_c).astype(o_ref.dtype)
        o_ref[:, 2 * A_WIDTH + LANES:] = (dvw[A_BLOCK:, :] + dv_carry[...]).astype(o_ref.dtype)
        dk_carry[...] = dkw[:A_BLOCK, :]
        dv_carry[...] = dvw[:A_BLOCK, :]

    return _call(
        body, name=name, grid=(s_len // A_BLOCK,),
        in_specs=specs + [pl.BlockSpec((A_HEADS, 8, LANES), lambda i: (0, 0, 0)),
                          pl.BlockSpec((A_BLOCK, A_WIDTH), lambda i: (cur(i), 0))],
        out_specs=[pl.BlockSpec((A_BLOCK, WA), lambda i: (cur(i), 0)),
                   pl.BlockSpec((A_HEADS, 8, LANES), lambda i: (0, 0, 0))],
        out_shape=[jax.ShapeDtypeStruct((s_len, WA), MXU_DTYPE), jax.ShapeDtypeStruct((A_HEADS, 8, LANES), F32)],
        scratch_shapes=[pltpu.VMEM((A_BLOCK, LANES), F32), pltpu.VMEM((A_BLOCK, LANES), F32)],
        compiler_params=_params("arbitrary"),
    )(proj_a, proj_a, proj_a, proj_a, proj_a, proj_a, cos, sin, cos, sin, sinks_t, dya)


RG_ROWS = 256


def _rg_gates(x, wa, ba, wx, bx, lam):
    r = jax.nn.sigmoid(mm_nn(x, wa) + ba)
    ig = jax.nn.sigmoid(mm_nn(x, wx) + bx)
    log_a = -R_C * r * _softplus(-lam)
    return jnp.exp(log_a), jnp.sqrt(_neg_expm1(2.0 * log_a)) * (ig * x)


def _rg_param_specs():
    mat = pl.BlockSpec((R_BLOCKS, R_BLOCK_DIM, R_BLOCK_DIM), lambda i: (0, 0, 0))
    vec = pl.BlockSpec((1, R_WIDTH), lambda i: (0, 0))
    return [mat, vec, mat, vec, vec]


def _rg_fwd(xr, proj_r, wa, ba, wx, bx, lam, *, name):
    s_len = xr.shape[0]
    rows = min(RG_ROWS, s_len)

    def body(x_ref, z_ref, wa_ref, ba_ref, wx_ref, bx_ref, lam_ref, h_ref, y_ref, a_buf, u_buf, carry):
        @pl.when(pl.program_id(0) == 0)
        def _():
            carry[...] = jnp.zeros_like(carry)

        for n in range(R_BLOCKS):
            cols = slice(n * R_BLOCK_DIM, (n + 1) * R_BLOCK_DIM)
            a, u = _rg_gates(x_ref[:, cols], wa_ref[n], ba_ref[:, cols], wx_ref[n], bx_ref[:, cols], lam_ref[:, cols])
            a_buf[:, cols] = a
            u_buf[:, cols] = u

        def step(t, h):
            h = a_buf[pl.ds(t, 1), :] * h + u_buf[pl.ds(t, 1), :]
            h_ref[pl.ds(t, 1), :] = h
            return h

        carry[...] = lax.fori_loop(0, rows, step, carry[...], unroll=8)
        y_ref[...] = (h_ref[...] * _silu(z_ref[...])).astype(y_ref.dtype)

    blk = pl.BlockSpec((rows, R_WIDTH), lambda i: (i, 0))
    return _call(
        body, name=name, grid=(s_len // rows,),
        in_specs=[blk, pl.BlockSpec((rows, R_WIDTH), lambda i: (i, 1))] + _rg_param_specs(),
        out_specs=[blk, blk],
        out_shape=[jax.ShapeDtypeStruct((s_len, R_WIDTH), F32), jax.ShapeDtypeStruct((s_len, R_WIDTH), MXU_DTYPE)],
        scratch_shapes=[pltpu.VMEM((rows, R_WIDTH), F32), pltpu.VMEM((rows, R_WIDTH), F32), pltpu.VMEM((1, R_WIDTH), F32)],
        compiler_params=_params("arbitrary"),
    )(xr, proj_r, wa, ba, wx, bx, lam)


def _rg_bwd(xr, proj_r, h, dyr, wa, ba, wx, bx, lam, *, name):
    s_len = xr.shape[0]
    rows = min(RG_ROWS, s_len)
    nblk = s_len // rows
    per = rows // HALO

    def cur(i):
        return nblk - 1 - i

    def body(x_ref, z_ref, h_ref, hh_ref, dy_ref, wa_ref, ba_ref, wx_ref, bx_ref, lam_ref,
             dx_ref, dz_ref, dwa_ref, dba_ref, dwx_ref, dbx_ref, dlam_ref, a_buf, g_buf, carry):
        i = pl.program_id(0)

        @pl.when(i == 0)
        def _():
            carry[...] = jnp.zeros_like(carry)
            for ref in (dwa_ref, dba_ref, dwx_ref, dbx_ref, dlam_ref):
                ref[...] = jnp.zeros_like(ref)

        z = z_ref[...]
        sig = jax.nn.sigmoid(z)
        hval = h_ref[...]
        dy = dy_ref[...]
        dz_ref[...] = dy * hval * (sig * (1.0 + z * (1.0 - sig)))
        g_buf[...] = dy * (z * sig)
        vjps = []
        for n in range(R_BLOCKS):
            cols = slice(n * R_BLOCK_DIM, (n + 1) * R_BLOCK_DIM)
            (a, _), vjp = jax.vjp(_rg_gates, x_ref[:, cols], wa_ref[n], ba_ref[:, cols], wx_ref[n], bx_ref[:, cols],
                                  lam_ref[:, cols])
            a_buf[:, cols] = a
            vjps.append(vjp)

        def step(k, c):
            t = rows - 1 - k
            g = g_buf[pl.ds(t, 1), :] + c
            g_buf[pl.ds(t, 1), :] = g
            return a_buf[pl.ds(t, 1), :] * g

        carry[...] = lax.fori_loop(0, rows, step, carry[...], unroll=8)
        h_halo = jnp.where(cur(i) == 0, 0.0, hh_ref[...])
        dh = g_buf[...]
        da = dh * _shift_down(hval, h_halo, 1)
        for n in range(R_BLOCKS):
            cols = slice(n * R_BLOCK_DIM, (n + 1) * R_BLOCK_DIM)
            dx, dwa, dba, dwx, dbx, dlam = vjps[n]((da[:, cols], dh[:, cols]))
            dx_ref[:, cols] = dx
            dwa_ref[n] += dwa
            dwx_ref[n] += dwx
            dba_ref[:, cols] += dba
            dbx_ref[:, cols] += dbx
            dlam_ref[:, cols] += dlam

    blk = pl.BlockSpec((rows, R_WIDTH), lambda i: (cur(i), 0))
    mat = pl.BlockSpec((R_BLOCKS, R_BLOCK_DIM, R_BLOCK_DIM), lambda i: (0, 0, 0))
    vec = pl.BlockSpec((1, R_WIDTH), lambda i: (0, 0))
    return _call(
        body, name=name, grid=(nblk,),
        in_specs=[blk, pl.BlockSpec((rows, R_WIDTH), lambda i: (cur(i), 1)), blk,
                  pl.BlockSpec((HALO, R_WIDTH), lambda i: (jnp.maximum(cur(i) * per - 1, 0), 0)), blk] + _rg_param_specs(),
        out_specs=[blk, blk, mat, vec, mat, vec, vec],
        out_shape=[jax.ShapeDtypeStruct((s_len, R_WIDTH), F32)] * 2 + [
            jax.ShapeDtypeStruct((R_BLOCKS, R_BLOCK_DIM, R_BLOCK_DIM), F32), jax.ShapeDtypeStruct((1, R_WIDTH), F32),
            jax.ShapeDtypeStruct((R_BLOCKS, R_BLOCK_DIM, R_BLOCK_DIM), F32), jax.ShapeDtypeStruct((1, R_WIDTH), F32),
            jax.ShapeDtypeStruct((1, R_WIDTH), F32)],
        scratch_shapes=[pltpu.VMEM((rows, R_WIDTH), F32), pltpu.VMEM((rows, R_WIDTH), F32), pltpu.VMEM((1, R_WIDTH), F32)],
        compiler_params=_params("arbitrary"),
    )(xr, proj_r, h, h, dyr, wa, ba, wx, bx, lam)


GP_CHUNKS = 2
GS_CHUNKS = 8


def _seg_cumsum(x, reverse):
    rows = x.shape[0]
    r = _iota(x.shape, 0) & (G_CHUNK - 1)
    s = 1
    while s < G_CHUNK:
        if reverse:
            x = x + jnp.where(r < G_CHUNK - s, pltpu.roll(x, rows - s, 0), 0.0)
        else:
            x = x + jnp.where(r >= s, pltpu.roll(x, s, 0), 0.0)
        s *= 2
    return x


def _gdn_decay(ga, a_log_row, dt_row):
    return -jnp.exp(a_log_row) * _softplus(ga + dt_row)


def _gdn_chunk(cq, ck, cv, gb, gc, inv=None):
    shape = cq.shape
    head = _iota(shape, 0) & (G_HEADS - 1)
    lane = _iota(shape, 2)
    q, k, v = _silu(cq), _silu(ck), _silu(cv)
    q = q * lax.rsqrt(jnp.sum(q * q, axis=-1, keepdims=True) + RMS_EPS) * (G_HEAD_DIM ** -0.5)
    k = k * lax.rsqrt(jnp.sum(k * k, axis=-1, keepdims=True) + RMS_EPS)
    beta = jnp.sum(jnp.where(lane == head, jax.nn.sigmoid(gb), 0.0), axis=-1, keepdims=True)
    g = jnp.sum(jnp.where(lane == head + G_HEADS, gc, 0.0), axis=-1, keepdims=True)
    sq = (shape[0], G_CHUNK, G_CHUNK)
    row, col = _iota(sq, 1), _iota(sq, 2)
    g_sq = jnp.broadcast_to(g, sq)
    decay = jnp.where(row >= col, jnp.exp(jnp.minimum(g_sq - _t(g_sq), 0.0)), 0.0)
    g_last = jnp.sum(jnp.where(_iota(g.shape, 1) == G_CHUNK - 1, g, 0.0), axis=1, keepdims=True)
    eg = jnp.exp(g)
    kb, vb = k * beta, v * beta
    m = jnp.where(row > col, mm_nt(kb, k) * decay, 0.0)
    known = inv is not None
    if not known:
        inv = _inv_unit_lower(m)
    u, w = _solve2(m, inv, vb, kb * eg)
    qk = jnp.where(row >= col, mm_nt(q, k) * decay, 0.0)
    q_dec = q * eg
    k_dec = k * jnp.exp(g_last - g)
    gl = jnp.broadcast_to(jnp.exp(g_last), (shape[0], 1, G_HEAD_DIM))
    return (u, w, qk, q_dec, k_dec, gl) if known else (u, w, qk, q_dec, k_dec, gl, inv)


def _gdn_step(state, u, w, qk, q_dec, k_dec, gl, gz, norm_w):
    v_new = u - mm_nn(w, state)
    o = mm_nn(q_dec, state) + mm_nn(qk, v_new)
    new_state = state * gl + mm_nn(_t(k_dec), v_new)
    o = o * lax.rsqrt(jnp.mean(o * o, axis=-1, keepdims=True) + RMS_EPS) * norm_w
    return o * _silu(gz), new_state


def _stack_chunks(x, heads):
    chunks = x.shape[0] // G_CHUNK
    parts = []
    for c in range(chunks):
        rows = slice(c * G_CHUNK, (c + 1) * G_CHUNK)
        for hd in range(G_HEADS):
            parts.append(x[rows, hd * LANES:(hd + 1) * LANES] if heads else x[rows, :])
    return jnp.stack(parts)


def _gdn_chunk_shapes(nch):
    b = nch * G_HEADS
    wide = jax.ShapeDtypeStruct((b, G_CHUNK, G_HEAD_DIM), F32)
    return [wide, wide, jax.ShapeDtypeStruct((b, G_CHUNK, G_CHUNK), F32), wide, wide,
            jax.ShapeDtypeStruct((b, 1, G_HEAD_DIM), F32)]


def _gdn_chunk_specs(nbatch):
    wide = pl.BlockSpec((nbatch, G_CHUNK, G_HEAD_DIM), lambda i: (i, 0, 0))
    return [wide, wide, pl.BlockSpec((nbatch, G_CHUNK, G_CHUNK), lambda i: (i, 0, 0)), wide, wide,
            pl.BlockSpec((nbatch, 1, G_HEAD_DIM), lambda i: (i, 0, 0))]


def _gdn_chunk_fwd(conv, proj_g, a_log_row, dt_row, *, name):
    s_len = conv.shape[0]
    cpg = min(GP_CHUNKS, s_len // G_CHUNK)
    rows = cpg * G_CHUNK
    nbatch = cpg * G_HEADS

    def body(c_ref, bg_ref, al_ref, dt_ref, *outs):
        bg = bg_ref[...]
        gc = _seg_cumsum(_gdn_decay(bg, al_ref[...], dt_ref[...]), False)
        res = _gdn_chunk(_stack_chunks(c_ref[:, 0:G_WIDTH], True), _stack_chunks(c_ref[:, G_WIDTH:2 * G_WIDTH], True),
                         _stack_chunks(c_ref[:, 2 * G_WIDTH:], True), _stack_chunks(bg, False), _stack_chunks(gc, False))
        for ref, val in zip(outs, res):
            ref[...] = val

    row = pl.BlockSpec((1, LANES), lambda i: (0, 0))
    return _call(
        body, name=name, grid=(s_len // rows,),
        in_specs=[pl.BlockSpec((rows, 3 * G_WIDTH), lambda i: (i, 0)),
                  pl.BlockSpec((rows, LANES), lambda i: (i, (3 * G_WIDTH + G_WIDTH) // LANES)), row, row],
        out_specs=_gdn_chunk_specs(nbatch) + [pl.BlockSpec((nbatch, G_CHUNK, G_CHUNK), lambda i: (i, 0, 0))],
        out_shape=_gdn_chunk_shapes(s_len // G_CHUNK) + [
            jax.ShapeDtypeStruct((s_len // G_CHUNK * G_HEADS, G_CHUNK, G_CHUNK), F32)],
        compiler_params=_params("parallel"),
    )(conv, proj_g, a_log_row, dt_row)


def _gdn_chunk_bwd(conv, proj_g, a_log_row, dt_row, inv, cots, *, name):
    s_len = conv.shape[0]
    cpg = min(GP_CHUNKS, s_len // G_CHUNK)
    rows = cpg * G_CHUNK
    nbatch = cpg * G_HEADS

    def unstack(x, heads):
        if heads:
            return jnp.concatenate([jnp.concatenate([x[c * G_HEADS + hd] for hd in range(G_HEADS)], axis=1)
                                    for c in range(cpg)], axis=0)
        return jnp.concatenate([sum(x[c * G_HEADS + hd] for hd in range(G_HEADS)) for c in range(cpg)], axis=0)

    def body(c_ref, bg_ref, al_ref, dt_ref, inv_ref, du, dw, dqk, dqd, dkd, dgl, dc_ref, dbg_ref, dal_ref, ddt_ref):
        @pl.when(pl.program_id(0) == 0)
        def _():
            dal_ref[...] = jnp.zeros_like(dal_ref)
            ddt_ref[...] = jnp.zeros_like(ddt_ref)

        bg = bg_ref[...]
        g_all, decay_vjp = jax.vjp(_gdn_decay, bg, al_ref[...], dt_ref[...])
        gc = _seg_cumsum(g_all, False)
        _, vjp = jax.vjp(_gdn_chunk, _stack_chunks(c_ref[:, 0:G_WIDTH], True),
                         _stack_chunks(c_ref[:, G_WIDTH:2 * G_WIDTH], True), _stack_chunks(c_ref[:, 2 * G_WIDTH:], True),
                         _stack_chunks(bg, False), _stack_chunks(gc, False), inv_ref[...])
        dq, dk, dv, dgb, dgc, _ = vjp((du[...], dw[...], dqk[...], dqd[...], dkd[...], dgl[...]))
        dc_ref[:, 0:G_WIDTH] = unstack(dq, True)
        dc_ref[:, G_WIDTH:2 * G_WIDTH] = unstack(dk, True)
        dc_ref[:, 2 * G_WIDTH:] = unstack(dv, True)
        dga, dal, ddt = decay_vjp(_seg_cumsum(unstack(dgc, False), True))
        dbg_ref[:, 0:LANES] = unstack(dgb, False) + dga
        dbg_ref[:, LANES:] = jnp.zeros((rows, LANES), F32)
        dal_ref[...] += dal
        ddt_ref[...] += ddt

    row = pl.BlockSpec((1, LANES), lambda i: (0, 0))
    return _call(
        body, name=name, grid=(s_len // rows,),
        in_specs=[pl.BlockSpec((rows, 3 * G_WIDTH), lambda i: (i, 0)),
                  pl.BlockSpec((rows, LANES), lambda i: (i, (3 * G_WIDTH + G_WIDTH) // LANES)), row, row,
                  pl.BlockSpec((nbatch, G_CHUNK, G_CHUNK), lambda i: (i, 0, 0))]
        + _gdn_chunk_specs(nbatch),
        out_specs=[pl.BlockSpec((rows, 3 * G_WIDTH), lambda i: (i, 0)), pl.BlockSpec((rows, 2 * LANES), lambda i: (i, 0)),
                   row, row],
        out_shape=[jax.ShapeDtypeStruct((s_len, 3 * G_WIDTH), F32), jax.ShapeDtypeStruct((s_len, 2 * LANES), F32),
                   jax.ShapeDtypeStruct((1, LANES), F32), jax.ShapeDtypeStruct((1, LANES), F32)],
        compiler_params=_params("arbitrary"),
    )(conv, proj_g, a_log_row, dt_row, inv, *cots)


def _gdn_scan_specs(cpg, which):
    nbatch = cpg * G_HEADS
    wide = pl.BlockSpec((nbatch, G_CHUNK, G_HEAD_DIM), lambda i: (which(i), 0, 0))
    return [wide, wide, pl.BlockSpec((nbatch, G_CHUNK, G_CHUNK), lambda i: (which(i), 0, 0)), wide, wide,
            pl.BlockSpec((nbatch, 1, G_HEAD_DIM), lambda i: (which(i), 0, 0))]


def _gz_stack(z_ref, c):
    rows = pl.ds(pl.multiple_of(c * G_CHUNK, G_CHUNK), G_CHUNK)
    return jnp.stack([z_ref[rows, hd * LANES:(hd + 1) * LANES] for hd in range(G_HEADS)])


def _gdn_scan_fwd(chunk_vals, proj_g, norm_w, *, name):
    s_len = proj_g.shape[0]
    nch = s_len // G_CHUNK
    cpg = min(GS_CHUNKS, nch)
    rows = cpg * G_CHUNK

    def body(u_ref, w_ref, qk_ref, qd_ref, kd_ref, gl_ref, z_ref, nw_ref, y_ref, st_ref, state):
        @pl.when(pl.program_id(0) == 0)
        def _():
            state[...] = jnp.zeros_like(state)

        def step(c, carry):
            b = pl.ds(pl.multiple_of(c * G_HEADS, G_HEADS), G_HEADS)
            st = state[...]
            st_ref[b] = st
            y, new_state = _gdn_step(st, u_ref[b], w_ref[b], qk_ref[b], qd_ref[b], kd_ref[b], gl_ref[b],
                                     _gz_stack(z_ref, c), nw_ref[...])
            state[...] = new_state
            rws = pl.ds(pl.multiple_of(c * G_CHUNK, G_CHUNK), G_CHUNK)
            for hd in range(G_HEADS):
                y_ref[rws, hd * LANES:(hd + 1) * LANES] = y[hd].astype(y_ref.dtype)
            return carry

        lax.fori_loop(0, cpg, step, 0)

    return _call(
        body, name=name, grid=(nch // cpg,),
        in_specs=_gdn_scan_specs(cpg, lambda i: i) + [
            pl.BlockSpec((rows, G_WIDTH), lambda i: (i, 3)), pl.BlockSpec((1, G_HEAD_DIM), lambda i: (0, 0))],
        out_specs=[pl.BlockSpec((rows, G_WIDTH), lambda i: (i, 0)),
                   pl.BlockSpec((cpg * G_HEADS, G_HEAD_DIM, G_HEAD_DIM), lambda i: (i, 0, 0))],
        out_shape=[jax.ShapeDtypeStruct((s_len, G_WIDTH), MXU_DTYPE),
                   jax.ShapeDtypeStruct((nch * G_HEADS, G_HEAD_DIM, G_HEAD_DIM), F32)],
        scratch_shapes=[pltpu.VMEM((G_HEADS, G_HEAD_DIM, G_HEAD_DIM), F32)],
        compiler_params=_params("arbitrary"),
    )(*chunk_vals, proj_g, norm_w)


def _gdn_scan_bwd(chunk_vals, states, proj_g, norm_w, dyg, *, name):
    s_len = proj_g.shape[0]
    nch = s_len // G_CHUNK
    cpg = min(GS_CHUNKS, nch)
    rows = cpg * G_CHUNK
    ngrid = nch // cpg

    def cur(i):
        return ngrid - 1 - i

    def body(u_ref, w_ref, qk_ref, qd_ref, kd_ref, gl_ref, st_ref, z_ref, nw_ref, dy_ref,
             du_ref, dw_ref, dqk_ref, dqd_ref, dkd_ref, dgl_ref, dz_ref, dnw_ref, dstate):
        @pl.when(pl.program_id(0) == 0)
        def _():
            dstate[...] = jnp.zeros_like(dstate)
            dnw_ref[...] = jnp.zeros_like(dnw_ref)

        def step(k, carry):
            c = cpg - 1 - k
            b = pl.ds(pl.multiple_of(c * G_HEADS, G_HEADS), G_HEADS)
            _, vjp = jax.vjp(_gdn_step, st_ref[b], u_ref[b], w_ref[b], qk_ref[b], qd_ref[b], kd_ref[b], gl_ref[b],
                             _gz_stack(z_ref, c), nw_ref[...])
            dst, du, dw, dqk, dqd, dkd, dgl, dz, dnw = vjp((_gz_stack(dy_ref, c), dstate[...]))
            dstate[...] = dst
            du_ref[b], dw_ref[b], dqk_ref[b], dqd_ref[b], dkd_ref[b], dgl_ref[b] = du, dw, dqk, dqd, dkd, dgl
            rws = pl.ds(pl.multiple_of(c * G_CHUNK, G_CHUNK), G_CHUNK)
            for hd in range(G_HEADS):
                dz_ref[rws, hd * LANES:(hd + 1) * LANES] = dz[hd]
            dnw_ref[...] += dnw
            return carry

        lax.fori_loop(0, cpg, step, 0)

    gate = pl.BlockSpec((rows, G_WIDTH), lambda i: (cur(i), 3))
    wide = pl.BlockSpec((rows, G_WIDTH), lambda i: (cur(i), 0))
    vec = pl.BlockSpec((1, G_HEAD_DIM), lambda i: (0, 0))
    return _call(
        body, name=name, grid=(ngrid,),
        in_specs=_gdn_scan_specs(cpg, cur) + [
            pl.BlockSpec((cpg * G_HEADS, G_HEAD_DIM, G_HEAD_DIM), lambda i: (cur(i), 0, 0)), gate, vec, wide],
        out_specs=_gdn_scan_specs(cpg, cur) + [wide, vec],
        out_shape=_gdn_chunk_shapes(nch) + [jax.ShapeDtypeStruct((s_len, G_WIDTH), F32),
                                            jax.ShapeDtypeStruct((1, G_HEAD_DIM), F32)],
        scratch_shapes=[pltpu.VMEM((G_HEADS, G_HEAD_DIM, G_HEAD_DIM), F32)],
        compiler_params=_params("arbitrary"),
    )(*chunk_vals, states, proj_g, norm_w, dyg)


def _adamw_math(w, g, m, v):
    m = ADAM_B1 * m + (1.0 - ADAM_B1) * g
    v = ADAM_B2 * v + (1.0 - ADAM_B2) * (g * g)
    m_hat = m / (1.0 - ADAM_B1 ** ADAM_STEP)
    v_hat = v / (1.0 - ADAM_B2 ** ADAM_STEP)
    delta = -ADAM_LR * (m_hat / (jnp.sqrt(v_hat) + ADAM_EPS) + ADAM_WD * w)
    return delta, m, v


def _sum_adamw(parts, w, m, v, *, name, rows):
    n_layers, n_rows, n_cols = w.shape
    rows = min(rows, n_rows)
    n_parts = parts.shape[0]

    def body(p_ref, w_ref, m_ref, v_ref, g_ref, d_ref, nm_ref, nv_ref):
        g = p_ref[0, 0].astype(F32)
        for k in range(1, n_parts):
            g = g + p_ref[k, 0].astype(F32)
        delta, new_m, new_v = _adamw_math(w_ref[0], g, m_ref[0], v_ref[0])
        g_ref[0], d_ref[0], nm_ref[0], nv_ref[0] = g, delta, new_m, new_v

    blk = pl.BlockSpec((1, rows, n_cols), lambda l, i: (l, i, 0))
    return _call(
        body, name=name, grid=(n_layers, n_rows // rows),
        in_specs=[pl.BlockSpec((n_parts, 1, rows, n_cols), lambda l, i: (0, l, i, 0)), blk, blk, blk],
        out_specs=[blk] * 4, out_shape=[jax.ShapeDtypeStruct(w.shape, F32)] * 4,
        compiler_params=_params("parallel", "parallel"),
    )(parts, w, m, v)


def _sum_slots(parts, *, name):
    rows = parts.shape[1]

    def body(p_ref, o_ref):
        g = p_ref[0]
        for k in range(1, N_DEV):
            g = g + p_ref[k]
        o_ref[...] = g

    return _call(body, name=name, grid=(1,),
                 in_specs=[pl.BlockSpec(parts.shape, lambda i: (0, 0, 0))],
                 out_specs=pl.BlockSpec((rows, LANES), lambda i: (0, 0)),
                 out_shape=jax.ShapeDtypeStruct((rows, LANES), F32), compiler_params=_params("arbitrary"))(parts)


def _adamw_packed(w, g, m, v, *, name):
    def body(w_ref, g_ref, m_ref, v_ref, d_ref, nm_ref, nv_ref):
        d_ref[...], nm_ref[...], nv_ref[...] = _adamw_math(w_ref[...], g_ref[...], m_ref[...], v_ref[...])

    blk = pl.BlockSpec(w.shape, lambda i: (0, 0))
    return _call(body, name=name, grid=(1,), in_specs=[blk] * 4, out_specs=[blk] * 3,
                 out_shape=[jax.ShapeDtypeStruct(w.shape, F32)] * 3, compiler_params=_params("arbitrary"))(w, g, m, v)


A_COLS = ((0, 512), (768, 1280), (512, 768))
R_COLS = ((1280, 3328),)
G_COLS = ((3328, 5384),)


def _group_weights(wt_full):
    def take(ranges):
        return jnp.concatenate([wt_full[a:b] for a, b in ranges], axis=0)

    wt_g = jnp.concatenate([take(G_COLS), jnp.zeros((G_PAD, wt_full.shape[1]), wt_full.dtype)], axis=0)
    return take(A_COLS), take(R_COLS), wt_g


def _ungroup_grads(d_a, d_r, d_g):
    return jnp.concatenate([d_a[0:512], d_a[1024:1280], d_a[512:1024], d_r, d_g[:WG - G_PAD]], axis=0)


def _shard_rows(w):
    return jnp.pad(jnp.transpose(w, (0, 2, 1)), ((0, 0), (0, N_ROWS_PAD - N_IN_SHARD), (0, 0)))


def _unshard_rows(wt):
    return jnp.transpose(wt[:, :N_IN_SHARD], (0, 2, 1))


def _owner_blocks(dwt):
    blocks = jnp.pad(dwt.reshape(4, 2, N_IN_SHARD, D_MODEL), ((0, 0), (0, 0), (0, N_ROWS_PAD - N_IN_SHARD), (0, 0)))
    return jnp.transpose(blocks, (1, 0, 2, 3))


def _rope_tables(s_len):
    inv = 1.0 / (ROPE_THETA ** (jnp.arange(0, A_HEAD_DIM, 2, dtype=F32) / A_HEAD_DIM))
    ang = jnp.arange(s_len, dtype=F32)[:, None] * inv[None, :]
    cos, sin = jnp.cos(ang), jnp.sin(ang)
    return jnp.tile(cos, (1, 4)), jnp.tile(jnp.concatenate([-sin, sin], axis=1), (1, 2))


def _pack(leaves):
    rows = []
    for leaf in leaves:
        flat = leaf.reshape(-1)
        pad = (-flat.shape[0]) % (8 * LANES)
        rows.append(jnp.pad(flat, (0, pad)).reshape(-1, LANES))
    return jnp.concatenate(rows, axis=0)


def _unpack(packed, shapes):
    out, row = [], 0
    for shape in shapes:
        size = math.prod(shape)
        nrows = -(-size // (8 * LANES)) * 8
        out.append(packed[row:row + nrows].reshape(-1)[:size].reshape(shape))
        row += nrows
    return out


def _lane_row(vals, offset):
    return jnp.pad(vals, (offset, LANES - offset - vals.shape[0])).reshape(1, LANES)


def kernel(x, w_in, sinks, r_conv_w, r_conv_b, r_wa, r_ba, r_wx, r_bx, r_lam, g_conv_w, g_a_log, g_dt_bias, g_norm_w, w_out, ln_g, ln_b, loss_target, m_w_in, m_sinks, m_r_conv_w, m_r_conv_b, m_r_wa, m_r_ba, m_r_wx, m_r_bx, m_r_lam, m_g_conv_w, m_g_a_log, m_g_dt_bias, m_g_norm_w, m_w_out, m_ln_g, m_ln_b, v_w_in, v_sinks, v_r_conv_w, v_r_conv_b, v_r_wa, v_r_ba, v_r_wx, v_r_bx, v_r_lam, v_g_conv_w, v_g_a_log, v_g_dt_bias, v_g_norm_w, v_w_out, v_ln_g, v_ln_b):
    s_len = x.shape[1]
    x0 = x.reshape(s_len, D_MODEL)
    target = loss_target.reshape(s_len, D_MODEL)
    me = 4 * lax.axis_index("x") + 2 * lax.axis_index("y") + lax.axis_index("c")

    win_all, wout_all, rcw_all, gcw_all = _all_gather(
        [_shard_rows(w_in).astype(MXU_DTYPE), w_out.astype(MXU_DTYPE), r_conv_w, g_conv_w], "gather_weights")
    rcw_full = jnp.moveaxis(rcw_all, 0, 2).reshape(DEPTH, CONV_WIDTH, R_WIDTH)
    gcw_full = jnp.moveaxis(gcw_all, 0, 2).reshape(DEPTH, CONV_WIDTH, 3 * G_WIDTH)
    cos, sin = _rope_tables(s_len)

    layers = []
    for l in range(DEPTH):
        wt_a, wt_r, wt_g = _group_weights(win_all[:, l, :N_IN_SHARD].reshape(N_IN, D_MODEL))
        wo = wout_all[:, l].reshape(D_MODEL, D_MODEL)
        layers.append(dict(
            wt_a=wt_a, wt_r=wt_r, wt_g=wt_g, wo=wo,
            wo_a=wo[0:A_WIDTH], wo_r=wo[A_WIDTH:A_WIDTH + R_WIDTH], wo_g=wo[A_WIDTH + R_WIDTH:],
            sinks_t=jnp.broadcast_to(sinks[l][:, None, None], (A_HEADS, 8, LANES)),
            rcw=rcw_full[l], rcb=r_conv_b[l].reshape(1, R_WIDTH), wa=r_wa[l], ba=r_ba[l].reshape(1, R_WIDTH),
            wx=r_wx[l], bx=r_bx[l].reshape(1, R_WIDTH), lam=r_lam[l].reshape(1, R_WIDTH),
            gcw=gcw_full[l], zero_b=jnp.zeros((1, 3 * G_WIDTH), F32),
            a_log=_lane_row(g_a_log[l], G_HEADS), dt=_lane_row(g_dt_bias[l], G_HEADS),
            norm_w=g_norm_w[l].reshape(1, G_HEAD_DIM), ln_g=ln_g[l].reshape(1, D_MODEL), ln_b=ln_b[l].reshape(1, D_MODEL)))

    saved = []
    xin = xin_lo = x0
    for l, p in enumerate(layers):
        proj_a = _matmul([xin_lo], [p["wt_a"]], name=f"proj_a{l}", tm=1024, tn=640, b_t=True)
        proj_r = _matmul([xin_lo], [p["wt_r"]], name=f"proj_r{l}", tm=1024, tn=512, b_t=True)
        proj_g = _matmul([xin_lo], [p["wt_g"]], name=f"proj_g{l}", tm=1024, tn=768, b_t=True)
        ya = _attn_fwd(proj_a, cos, sin, p["sinks_t"], name=f"attn_fwd{l}")
        xr = _conv_fwd(proj_r, R_WIDTH, p["rcw"], p["rcb"], name=f"rconv_fwd{l}")
        h, yr = _rg_fwd(xr, proj_r, p["wa"], p["ba"], p["wx"], p["bx"], p["lam"], name=f"rglru_fwd{l}")
        conv = _conv_fwd(proj_g, 3 * G_WIDTH, p["gcw"], p["zero_b"], name=f"gconv_fwd{l}")
        *chunk_vals, inv = _gdn_chunk_fwd(conv, proj_g, p["a_log"], p["dt"], name=f"gdn_chunk_fwd{l}")
        yg, states = _gdn_scan_fwd(chunk_vals, proj_g, p["norm_w"], name=f"gdn_scan_fwd{l}")
        z, xout, xout_lo = _outproj_ln(ya, yr, yg, p["wo"], xin, p["ln_g"], p["ln_b"], name=f"outproj_ln{l}")
        saved.append(dict(xin_lo=xin_lo, proj_a=proj_a, proj_r=proj_r, proj_g=proj_g, ya=ya, yr=yr, yg=yg, xr=xr, h=h,
                          conv=conv, chunk_vals=chunk_vals, inv=inv, states=states, z=z))
        xin, xin_lo = xout, xout_lo

    grads = [None] * DEPTH
    dxn = None
    loss_local = None
    for l in reversed(range(DEPTH)):
        p, sv = layers[l], saved[l]
        if dxn is None:
            dz, dz_lo, dln_g, dln_b, loss_local = _ln_bwd(sv["z"], p["ln_g"], name=f"ln_bwd{l}", xn=xin, target=target)
        else:
            dz, dz_lo, dln_g, dln_b = _ln_bwd(sv["z"], p["ln_g"], name=f"ln_bwd{l}", dxn=dxn)
        dya = _matmul([dz_lo], [p["wo_a"]], name=f"dya{l}", tm=1024, tn=512, b_t=True)
        dyr = _matmul([dz_lo], [p["wo_r"]], name=f"dyr{l}", tm=1024, tn=512, b_t=True)
        dyg = _matmul([dz_lo], [p["wo_g"]], name=f"dyg{l}", tm=1024, tn=512, b_t=True)
        dwo = jnp.concatenate([
            _matmul_tn(sv["ya"], dz_lo, name=f"dwo_a{l}", tm=512, tn=1024, tk=1024),
            _matmul_tn(sv["yr"], dz_lo, name=f"dwo_r{l}", tm=1024, tn=1024, tk=1024),
            _matmul_tn(sv["yg"], dz_lo, name=f"dwo_g{l}", tm=512, tn=1024, tk=1024)], axis=0)

        dproj_a, dsinks_t = _attn_bwd(sv["proj_a"], cos, sin, p["sinks_t"], dya, name=f"attn_bwd{l}")

        dxr, drz, dwa, dba, dwx, dbx, dlam = _rg_bwd(sv["xr"], sv["proj_r"], sv["h"], dyr, p["wa"], p["ba"], p["wx"],
                                                     p["bx"], p["lam"], name=f"rglru_bwd{l}")
        dproj_r, drcw, drcb = _conv_bwd(dxr, sv["proj_r"], R_WIDTH, p["rcw"], [drz], name=f"rconv_bwd{l}")

        scan_out = _gdn_scan_bwd(sv["chunk_vals"], sv["states"], sv["proj_g"], p["norm_w"], dyg, name=f"gdn_scan_bwd{l}")
        dgz, dnorm_w = scan_out[6], scan_out[7]
        dconv, dbg, dal, ddt = _gdn_chunk_bwd(sv["conv"], sv["proj_g"], p["a_log"], p["dt"], sv["inv"], scan_out[:6],
                                              name=f"gdn_chunk_bwd{l}")
        dproj_g, dgcw, _ = _conv_bwd(dconv, sv["proj_g"], 3 * G_WIDTH, p["gcw"], [dgz, dbg], name=f"gconv_bwd{l}")

        dxn = _matmul([dproj_a, dproj_r, dproj_g], [p["wt_a"], p["wt_r"], p["wt_g"]], name=f"dx{l}", tm=512, tn=512,
                      add=dz, add_scale=DEEPNORM_ALPHA)
        dwin = _ungroup_grads(_matmul_tn(dproj_a, sv["xin_lo"], name=f"dwin_a{l}", tm=640, tn=1024, tk=1024),
                              _matmul_tn(dproj_r, sv["xin_lo"], name=f"dwin_r{l}", tm=1024, tn=1024, tk=1024),
                              _matmul_tn(dproj_g, sv["xin_lo"], name=f"dwin_g{l}", tm=1152, tn=1024, tk=1024))
        grads[l] = dict(
            w_in=dwin, w_out=dwo, sinks=dsinks_t[:, :, 0].sum(axis=1), r_conv_w=drcw.reshape(CONV_WIDTH, R_WIDTH),
            r_conv_b=drcb.reshape(R_WIDTH), r_wa=dwa, r_ba=dba.reshape(R_WIDTH), r_wx=dwx, r_bx=dbx.reshape(R_WIDTH),
            r_lam=dlam.reshape(R_WIDTH), g_conv_w=dgcw.reshape(CONV_WIDTH, 3 * G_WIDTH),
            g_a_log=dal[0, G_HEADS:2 * G_HEADS], g_dt_bias=ddt[0, G_HEADS:2 * G_HEADS],
            g_norm_w=dnorm_w.reshape(G_HEAD_DIM), ln_g=dln_g.reshape(D_MODEL), ln_b=dln_b.reshape(D_MODEL))
    grad_x = dxn.reshape(x.shape)
    loss = lax.psum(loss_local[0, 0], ("x", "y", "c"))

    def stacked(name):
        return jnp.stack([grads[l][name] for l in range(DEPTH)])

    dwin_blocks = jnp.stack([_owner_blocks(grads[l]["w_in"]) for l in range(DEPTH)], axis=2).astype(MXU_DTYPE)
    dwout_blocks = jnp.transpose(stacked("w_out").reshape(DEPTH, 4, 2, OUT_SHARD, D_MODEL), (2, 1, 0, 3, 4)).astype(MXU_DTYPE)
    kept, got = _swap_cores([dwin_blocks, dwout_blocks], "swap_core_grads")
    chip_win = _add_pair(kept[0], got[0], name="add_core_grads_w_in", rows=352)
    chip_wout = _add_pair(kept[1], got[1], name="add_core_grads_w_out", rows=256)
    win_parts, wout_parts = _exchange_chips([chip_win, chip_wout], "exchange_chip_grads")
    w_in_t = [_unshard_rows(t) for t in _sum_adamw(win_parts, _shard_rows(w_in), _shard_rows(m_w_in), _shard_rows(v_w_in),
                                                   name="adamw_w_in", rows=176)]
    g_w_in, d_w_in, nm_w_in, nv_w_in = w_in_t
    g_w_out, d_w_out, nm_w_out, nv_w_out = _sum_adamw(wout_parts, w_out, m_w_out, v_w_out, name="adamw_w_out", rows=128)

    small = ["sinks", "r_conv_w", "r_conv_b", "r_wa", "r_ba", "r_wx", "r_bx", "r_lam", "g_conv_w", "g_a_log",
             "g_dt_bias", "g_norm_w", "ln_g", "ln_b"]
    full_shapes = [stacked(nm).shape for nm in small]
    (all_small,) = _all_gather([_pack([stacked(nm) for nm in small])], "gather_small_grads")
    g_small = dict(zip(small, _unpack(_sum_slots(all_small, name="sum_small_grads"), full_shapes)))
    g_small["r_conv_w"] = lax.dynamic_slice_in_dim(g_small["r_conv_w"], me * (R_WIDTH // N_DEV), R_WIDTH // N_DEV, axis=2)
    g_small["g_conv_w"] = lax.dynamic_slice_in_dim(g_small["g_conv_w"], me * (3 * G_WIDTH // N_DEV), 3 * G_WIDTH // N_DEV, axis=2)
    given = dict(sinks=(sinks, m_sinks, v_sinks), r_conv_w=(r_conv_w, m_r_conv_w, v_r_conv_w),
                 r_conv_b=(r_conv_b, m_r_conv_b, v_r_conv_b), r_wa=(r_wa, m_r_wa, v_r_wa), r_ba=(r_ba, m_r_ba, v_r_ba),
                 r_wx=(r_wx, m_r_wx, v_r_wx), r_bx=(r_bx, m_r_bx, v_r_bx), r_lam=(r_lam, m_r_lam, v_r_lam),
                 g_conv_w=(g_conv_w, m_g_conv_w, v_g_conv_w), g_a_log=(g_a_log, m_g_a_log, v_g_a_log),
                 g_dt_bias=(g_dt_bias, m_g_dt_bias, v_g_dt_bias), g_norm_w=(g_norm_w, m_g_norm_w, v_g_norm_w),
                 ln_g=(ln_g, m_ln_g, v_ln_g), ln_b=(ln_b, m_ln_b, v_ln_b))
    shard_shapes = [given[nm][0].shape for nm in small]
    packed = [_pack([given[nm][k] for nm in small]) for k in range(3)]
    d_p, nm_p, nv_p = _adamw_packed(packed[0], _pack([g_small[nm] for nm in small]), packed[1], packed[2], name="adamw_small")
    d_small = dict(zip(small, _unpack(d_p, shard_shapes)))
    nm_small = dict(zip(small, _unpack(nm_p, shard_shapes)))
    nv_small = dict(zip(small, _unpack(nv_p, shard_shapes)))

    order = ["w_in"] + small[:12] + ["w_out"] + small[12:]

    def leaf(big_in, big_out, table):
        return [big_in if nm == "w_in" else big_out if nm == "w_out" else table[nm] for nm in order]

    return (loss, grad_x, *leaf(g_w_in, g_w_out, g_small), *leaf(d_w_in, d_w_out, d_small),
            *leaf(nm_w_in, nm_w_out, nm_small), *leaf(nv_w_in, nv_w_out, nv_small))
```

```python
import functools
import math

import jax
import jax.numpy as jnp
from jax import lax
from jax.experimental import pallas as pl
from jax.experimental.pallas import tpu as pltpu

F32 = jnp.float32
MXU_DTYPE = jnp.bfloat16
HIGHEST = lax.Precision.HIGHEST
MESH_ID = pl.DeviceIdType.MESH

N_DEV = 8
DEPTH = 2
D_MODEL = 2048
A_HEADS, A_KV_HEADS, A_HEAD_DIM = 8, 2, 64
A_WIDTH, A_KV_WIDTH = 512, 128
A_BLOCK = 128
ROPE_THETA = 10000.0
R_WIDTH, R_BLOCKS, R_BLOCK_DIM = 1024, 8, 128
R_C = 8.0
CONV_WIDTH = 4
G_HEADS, G_HEAD_DIM, G_WIDTH, G_CHUNK = 4, 128, 512, 64
N_IN = 5384
N_IN_SHARD = N_IN // N_DEV
N_ROWS_PAD = 704
OUT_SHARD = D_MODEL // N_DEV
WA, WR, WG = 1280, 2048, 2304
G_PAD = WG - (3 * G_WIDTH + G_WIDTH + 2 * G_HEADS)
DEEPNORM_ALPHA = (2 * DEPTH) ** 0.25
LN_EPS = 1e-5
RMS_EPS = 1e-6
ADAM_LR, ADAM_B1, ADAM_B2, ADAM_EPS, ADAM_WD, ADAM_STEP = 0.001, 0.9, 0.999, 1e-08, 0.01, 10
NEG = -1e30
VMEM_LIMIT = 56 * 1024 * 1024
LANES = 128


def _call(body, **kw):
    return pl.pallas_call(body, **kw)


def _params(*sem):
    return pltpu.CompilerParams(dimension_semantics=sem, vmem_limit_bytes=VMEM_LIMIT)


def _t(x):
    return jnp.swapaxes(x, -1, -2)


def _raw_dot(a, b, ca, cb, precision=None):
    batch = tuple(range(a.ndim - 2))
    if precision is None:
        a, b = a.astype(MXU_DTYPE), b.astype(MXU_DTYPE)
    return lax.dot_general(a, b, (((ca,), (cb,)), (batch, batch)), precision=precision,
                           preferred_element_type=F32)


def _nn(a, b, precision=None):
    return _raw_dot(a, b, a.ndim - 1, b.ndim - 2, precision)


def _nt(a, b, precision=None):
    return _raw_dot(a, b, a.ndim - 1, b.ndim - 1, precision)


@jax.custom_vjp
def mm_nn(a, b):
    return _nn(a, b)


def _mm_nn_fwd(a, b):
    return _nn(a, b), (a, b)


def _mm_nn_bwd(res, g):
    a, b = res
    return _nt(g, b), _nn(_t(a), g)


mm_nn.defvjp(_mm_nn_fwd, _mm_nn_bwd)


@jax.custom_vjp
def mm_nt(a, b):
    return _nt(a, b)


def _mm_nt_fwd(a, b):
    return _nt(a, b), (a, b)


def _mm_nt_bwd(res, g):
    a, b = res
    return _nn(g, b), _nn(_t(g), a)


mm_nt.defvjp(_mm_nt_fwd, _mm_nt_bwd)


def _split(x):
    hi = x.astype(MXU_DTYPE)
    return hi, (x - hi.astype(F32)).astype(MXU_DTYPE)


def _hmm(a, b, nt=False):
    dot = _nt if nt else _nn
    return dot(a[0], b[0]) + (dot(a[0], b[1]) + dot(a[1], b[0]))


def _silu(x):
    return x * jax.nn.sigmoid(x)


def _softplus(x):
    return jnp.maximum(x, 0.0) + jnp.log1p(jnp.exp(-jnp.abs(x)))


def _neg_expm1(x):
    series = -x * (1.0 + x * 0.5 * (1.0 + x * (1.0 / 3.0) * (1.0 + x * 0.25 * (1.0 + x * 0.2))))
    return jnp.where(x > -0.125, series, 1.0 - jnp.exp(x))


def _iota(shape, dim):
    return lax.broadcasted_iota(jnp.int32, shape, dim)


def _inv_unit_lower(m):
    shape = m.shape
    row, col = _iota(shape, 1), _iota(shape, 2)
    eye = (row == col).astype(F32)

    def blockdiag(size):
        return (row // size) == (col // size)

    x = -jnp.where(blockdiag(8), m, 0.0)
    xs = _split(x)
    x2s = _split(_hmm(xs, xs))
    x4s = _split(_hmm(x2s, x2s))
    inv = eye + x
    inv = inv + _hmm(_split(inv), x2s)
    inv = inv + _hmm(_split(inv), x4s)
    for size in (8, 16, 32):
        below = jnp.where(blockdiag(2 * size) & jnp.logical_not(blockdiag(size)), m, 0.0)
        invs = _split(inv)
        inv = inv - _hmm(_split(_hmm(invs, _split(below))), invs)
    return inv


@jax.custom_vjp
def _solve2(m, inv, r1, r2):
    invs = _split(inv)
    return _hmm(invs, _split(r1)), _hmm(invs, _split(r2))


def _solve2_fwd(m, inv, r1, r2):
    x1, x2 = _solve2(m, inv, r1, r2)
    return (x1, x2), (inv, x1, x2)


def _solve2_bwd(res, g):
    inv, x1, x2 = res
    inv_ts = _split(_t(inv))
    d1, d2 = _hmm(inv_ts, _split(g[0])), _hmm(inv_ts, _split(g[1]))
    dm = -(_hmm(_split(d1), _split(x1), nt=True) + _hmm(_split(d2), _split(x2), nt=True))
    return dm, jnp.zeros_like(inv), d1, d2


_solve2.defvjp(_solve2_fwd, _solve2_bwd)


def _swap_halves(x):
    n = x.shape[-1]
    lane = _iota(x.shape, x.ndim - 1)
    return jnp.where((lane & 63) < 32, pltpu.roll(x, n - 32, x.ndim - 1), pltpu.roll(x, 32, x.ndim - 1))


def _rope(x, cos, sin):
    reps = x.shape[-1] // LANES
    if reps > 1:
        cos, sin = jnp.tile(cos, (1, reps)), jnp.tile(sin, (1, reps))
    return x * cos + _swap_halves(x) * sin


def _rope_t(d, cos, sin):
    reps = d.shape[-1] // LANES
    if reps > 1:
        cos, sin = jnp.tile(cos, (1, reps)), jnp.tile(sin, (1, reps))
    return d * cos + _swap_halves(d * sin)


def _swap64(x):
    return pltpu.roll(x, 64, x.ndim - 1)


def _mesh_pos():
    return lax.axis_index("x"), lax.axis_index("y"), lax.axis_index("c")


def _all_gather(arrays, name):
    n = len(arrays)
    npieces = [a.shape[0] for a in arrays]
    pmax = max(npieces)

    def body(*refs):
        ins, outs = refs[:n], refs[n:2 * n]
        send_sems, recv_sems, local_sem = refs[2 * n:]
        x, y, c = _mesh_pos()
        me, sibling = (x, y, c), (x, y, 1 - c)
        chips = [(1 - x, y), (x, 1 - y), (1 - x, 1 - y)]

        def slot(a, pos, p):
            return outs[a].at[4 * pos[0] + 2 * pos[1] + pos[2], p]

        def copy(a, p, k, block, to, own=False):
            return pltpu.make_async_remote_copy(
                src_ref=ins[a].at[p] if own else slot(a, block, p), dst_ref=slot(a, block, p),
                send_sem=send_sems.at[a, p, k], recv_sem=recv_sems.at[a, p, k], device_id=to, device_id_type=MESH_ID)

        pieces = [(a, p) for p in range(pmax) for a in range(n) if p < npieces[a]]
        mine = [pltpu.make_async_copy(ins[a].at[p], slot(a, me, p), local_sem.at[a, p]) for a, p in pieces]
        for cp in mine:
            cp.start()
        first = []
        for a, p in pieces:
            first += [copy(a, p, 1 + j, me, (*chip, c), own=True) for j, chip in enumerate(chips)]
            first.append(copy(a, p, 0, me, sibling, own=True))
        for cp in first:
            cp.start()
        passed = []
        for a, p in pieces:
            for j, chip in enumerate(chips):
                copy(a, p, 1 + j, (*chip, c), me).wait_recv()
                cp = copy(a, p, 4 + j, (*chip, c), sibling)
                cp.start()
                passed.append(cp)
        for a, p in pieces:
            copy(a, p, 0, sibling, me).wait_recv()
            for j, chip in enumerate(chips):
                copy(a, p, 4 + j, (*chip, 1 - c), me).wait_recv()
        for cp in first + passed:
            cp.wait_send()
        for cp in mine:
            cp.wait()

    any_spec = pl.BlockSpec(memory_space=pl.ANY)
    return _call(
        body, name=name,
        out_shape=[jax.ShapeDtypeStruct((N_DEV,) + a.shape, a.dtype) for a in arrays],
        in_specs=[any_spec] * n, out_specs=[any_spec] * n,
        scratch_shapes=[pltpu.SemaphoreType.DMA((n, pmax, 7)), pltpu.SemaphoreType.DMA((n, pmax, 7)),
                        pltpu.SemaphoreType.DMA((n, pmax))],
    )(*arrays)


def _swap_cores(arrays, name):
    n = len(arrays)
    pmax = max(a.shape[1] for a in arrays)

    def body(*refs):
        ins, got = refs[:n], refs[n:2 * n]
        send_sems, recv_sems = refs[2 * n:]
        x, y, c = _mesh_pos()
        copies = [pltpu.make_async_remote_copy(
            src_ref=ins[a].at[1 - c, p], dst_ref=got[a].at[p], send_sem=send_sems.at[a, p], recv_sem=recv_sems.at[a, p],
            device_id=(x, y, 1 - c), device_id_type=MESH_ID) for a in range(n) for p in range(arrays[a].shape[1])]
        for cp in copies:
            cp.start()
        for cp in copies:
            cp.wait()

    any_spec = pl.BlockSpec(memory_space=pl.ANY)
    return _call(
        body, name=name, out_shape=[jax.ShapeDtypeStruct(a.shape[1:], a.dtype) for a in arrays],
        in_specs=[any_spec] * n, out_specs=[any_spec] * n,
        scratch_shapes=[pltpu.SemaphoreType.DMA((n, pmax)), pltpu.SemaphoreType.DMA((n, pmax))],
    )(*arrays)


def _exchange_chips(arrays, name):
    n = len(arrays)
    n_layers = arrays[0].shape[1]

    def body(*refs):
        ins, outs = refs[:n], refs[n:2 * n]
        send_sems, recv_sems = refs[2 * n:]
        x, y, c = _mesh_pos()
        copies = []
        for l in range(n_layers):
            for a in range(n):
                for k in range(1, 4):
                    px, py = x ^ (k >> 1), y ^ (k & 1)
                    copies.append(pltpu.make_async_remote_copy(
                        src_ref=ins[a].at[2 * px + py, l], dst_ref=outs[a].at[k - 1, l], send_sem=send_sems.at[a, l, k - 1],
                        recv_sem=recv_sems.at[a, l, k - 1], device_id=(px, py, c), device_id_type=MESH_ID))
        for cp in copies:
            cp.start()
        for cp in copies:
            cp.wait_recv()
        for cp in copies:
            cp.wait_send()

    any_spec = pl.BlockSpec(memory_space=pl.ANY)
    return _call(
        body, name=name,
        out_shape=[jax.ShapeDtypeStruct((3,) + a.shape[1:], a.dtype) for a in arrays],
        in_specs=[any_spec] * n, out_specs=[any_spec] * n,
        scratch_shapes=[pltpu.SemaphoreType.DMA((n, n_layers, 3)), pltpu.SemaphoreType.DMA((n, n_layers, 3))],
    )(*arrays)


def _index_operand(i):
    return jnp.reshape(i, (1,)).astype(jnp.int32)


def _add_pair(pair, other, core, *, name, rows):
    _, n_slots, n_layers, n_rows, n_cols = pair.shape
    rows = min(rows, n_rows)

    def body(c_ref, a_ref, b_ref, o_ref):
        o_ref[...] = (a_ref[0].astype(F32) + b_ref[...].astype(F32)).astype(o_ref.dtype)

    blk = pl.BlockSpec((1, 1, rows, n_cols), lambda s, l, i, c: (s, l, i, 0))
    grid_spec = pltpu.PrefetchScalarGridSpec(
        num_scalar_prefetch=1, grid=(n_slots, n_layers, n_rows // rows),
        in_specs=[pl.BlockSpec((1, 1, 1, rows, n_cols), lambda s, l, i, c: (c[0], s, l, i, 0)), blk], out_specs=blk)
    return _call(body, name=name, grid_spec=grid_spec, out_shape=jax.ShapeDtypeStruct(other.shape, pair.dtype),
                 compiler_params=_params("parallel", "parallel", "parallel"))(_index_operand(core), pair, other)


def _matmul(a_list, b_list, *, name, tm, tn, b_t=False, out_dtype=F32, add=None, add_scale=1.0):
    n = len(a_list)
    m_rows, n_cols = a_list[0].shape[0], b_list[0].shape[0 if b_t else 1]
    tm, tn = min(tm, m_rows), min(tn, n_cols)

    def body(*refs):
        a_refs, b_refs = refs[:n], refs[n:2 * n]
        o_ref = refs[-1]
        acc = None
        for a_ref, b_ref in zip(a_refs, b_refs):
            part = lax.dot_general(a_ref[...].astype(MXU_DTYPE), b_ref[...].astype(MXU_DTYPE),
                                   (((1,), (1 if b_t else 0,)), ((), ())), preferred_element_type=F32)
            acc = part if acc is None else acc + part
        if add is not None:
            acc = acc + add_scale * refs[2 * n][...]
        o_ref[...] = acc.astype(o_ref.dtype)

    in_specs = [pl.BlockSpec((tm, a.shape[1]), lambda i, j: (i, 0)) for a in a_list]
    if b_t:
        in_specs += [pl.BlockSpec((tn, b.shape[1]), lambda i, j: (j, 0)) for b in b_list]
    else:
        in_specs += [pl.BlockSpec((b.shape[0], tn), lambda i, j: (0, j)) for b in b_list]
    args = list(a_list) + list(b_list)
    if add is not None:
        in_specs.append(pl.BlockSpec((tm, tn), lambda i, j: (i, j)))
        args.append(add)
    return _call(
        body, name=name, grid=(m_rows // tm, n_cols // tn), in_specs=in_specs,
        out_specs=pl.BlockSpec((tm, tn), lambda i, j: (i, j)),
        out_shape=jax.ShapeDtypeStruct((m_rows, n_cols), out_dtype),
        compiler_params=_params("parallel", "arbitrary"),
    )(*args)


def _matmul_tn(a, b, *, name, tm, tn, tk):
    k_rows, m_rows = a.shape
    n_cols = b.shape[1]
    tm, tn, tk = min(tm, m_rows), min(tn, n_cols), min(tk, k_rows)
    nk = k_rows // tk

    def body(a_ref, b_ref, o_ref):
        part = lax.dot_general(a_ref[...].astype(MXU_DTYPE), b_ref[...].astype(MXU_DTYPE),
                               (((0,), (0,)), ((), ())), preferred_element_type=F32)

        @pl.when(pl.program_id(2) == 0)
        def _():
            o_ref[...] = part

        @pl.when(pl.program_id(2) > 0)
        def _():
            o_ref[...] += part

    return _call(
        body, name=name, grid=(m_rows // tm, n_cols // tn, nk),
        in_specs=[pl.BlockSpec((tk, tm), lambda i, j, k: (k, i)), pl.BlockSpec((tk, tn), lambda i, j, k: (k, j))],
        out_specs=pl.BlockSpec((tm, tn), lambda i, j, k: (i, j)),
        out_shape=jax.ShapeDtypeStruct((m_rows, n_cols), F32),
        compiler_params=_params("parallel", "parallel", "arbitrary"),
    )(a, b)


def _outproj_ln(ya, yr, yg, w_out, x, ln_g, ln_b, *, name):
    s_len = x.shape[0]
    tm = min(256, s_len)

    def body(ya_ref, yr_ref, yg_ref, w_ref, x_ref, g_ref, b_ref, z_ref, o_ref, lo_ref):
        acc = jnp.dot(ya_ref[...], w_ref[0:A_WIDTH, :], preferred_element_type=F32)
        acc += jnp.dot(yr_ref[...], w_ref[A_WIDTH:A_WIDTH + R_WIDTH, :], preferred_element_type=F32)
        acc += jnp.dot(yg_ref[...], w_ref[A_WIDTH + R_WIDTH:, :], preferred_element_type=F32)
        z = DEEPNORM_ALPHA * x_ref[...] + acc
        z_ref[...] = z
        mu = jnp.mean(z, axis=-1, keepdims=True)
        zc = z - mu
        var = jnp.mean(zc * zc, axis=-1, keepdims=True)
        out = zc * lax.rsqrt(var + LN_EPS) * g_ref[...] + b_ref[...]
        o_ref[...] = out
        lo_ref[...] = out.astype(lo_ref.dtype)

    def rows(width):
        return pl.BlockSpec((tm, width), lambda i: (i, 0))

    def whole(shape):
        return pl.BlockSpec(shape, lambda i: (0, 0))

    return _call(
        body, name=name, grid=(s_len // tm,),
        in_specs=[rows(A_WIDTH), rows(R_WIDTH), rows(G_WIDTH), whole((D_MODEL, D_MODEL)), rows(D_MODEL),
                  whole((1, D_MODEL)), whole((1, D_MODEL))],
        out_specs=[rows(D_MODEL)] * 3,
        out_shape=[jax.ShapeDtypeStruct((s_len, D_MODEL), F32)] * 2 + [jax.ShapeDtypeStruct((s_len, D_MODEL), MXU_DTYPE)],
        compiler_params=_params("parallel"),
    )(ya, yr, yg, w_out, x, ln_g, ln_b)


def _ln_bwd(z, ln_g, *, name, dxn=None, xn=None, target=None):
    s_len = z.shape[0]
    tm = min(256, s_len)
    top = dxn is None

    def body(*refs):
        if top:
            z_ref, g_ref, xn_ref, t_ref, dz_ref, lo_ref, dg_ref, db_ref, loss_ref = refs
            err = xn_ref[...] - t_ref[...]
            dy = err * (1.0 / D_MODEL)
        else:
            z_ref, g_ref, dy_ref, dz_ref, lo_ref, dg_ref, db_ref = refs
            dy = dy_ref[...]
        first = pl.program_id(0) == 0

        @pl.when(first)
        def _():
            dg_ref[...] = jnp.zeros_like(dg_ref)
            db_ref[...] = jnp.zeros_like(db_ref)
            if top:
                loss_ref[...] = jnp.zeros_like(loss_ref)

        z = z_ref[...]
        mu = jnp.mean(z, axis=-1, keepdims=True)
        zc = z - mu
        rstd = lax.rsqrt(jnp.mean(zc * zc, axis=-1, keepdims=True) + LN_EPS)
        xhat = zc * rstd
        dxh = dy * g_ref[...]
        dz = rstd * (dxh - jnp.mean(dxh, axis=-1, keepdims=True) - xhat * jnp.mean(dxh * xhat, axis=-1, keepdims=True))
        dz_ref[...] = dz
        lo_ref[...] = dz.astype(lo_ref.dtype)
        dg_ref[...] += jnp.sum(dy * xhat, axis=0, keepdims=True)
        db_ref[...] += jnp.sum(dy, axis=0, keepdims=True)
        if top:
            per_row = jnp.sum(err * err, axis=-1, keepdims=True) * (0.5 / D_MODEL)
            loss_ref[...] += jnp.sum(per_row, axis=0, keepdims=True)

    rows = pl.BlockSpec((tm, D_MODEL), lambda i: (i, 0))
    vec = pl.BlockSpec((1, D_MODEL), lambda i: (0, 0))
    in_specs = [rows, vec] + ([rows, rows] if top else [rows])
    args = [z, ln_g] + ([xn, target] if top else [dxn])
    out_specs = [rows, rows, vec, vec]
    out_shape = [jax.ShapeDtypeStruct((s_len, D_MODEL), F32), jax.ShapeDtypeStruct((s_len, D_MODEL), MXU_DTYPE),
                 jax.ShapeDtypeStruct((1, D_MODEL), F32), jax.ShapeDtypeStruct((1, D_MODEL), F32)]
    if top:
        out_specs.append(pl.BlockSpec((1, 1), lambda i: (0, 0)))
        out_shape.append(jax.ShapeDtypeStruct((1, 1), F32))
    return _call(body, name=name, grid=(s_len // tm,), in_specs=in_specs, out_specs=out_specs,
                 out_shape=out_shape, compiler_params=_params("arbitrary"))(*args)


CONV_ROWS = 256
HALO = 8


def _shift_down(x, halo, s):
    if s == 0:
        return x
    ext = jnp.concatenate([halo, x], axis=0)
    return pltpu.roll(ext, s, 0)[HALO:, :]


def _shift_up(x, halo, s):
    if s == 0:
        return x
    ext = jnp.concatenate([x, halo], axis=0)
    return pltpu.roll(ext, ext.shape[0] - s, 0)[:x.shape[0], :]


def _conv_fwd(src, width, w, bias, *, name):
    s_len = src.shape[0]
    rows = min(CONV_ROWS, s_len)
    per = rows // HALO

    def body(x_ref, halo_ref, w_ref, b_ref, o_ref):
        x = x_ref[...]
        halo = jnp.where(pl.program_id(0) == 0, 0.0, halo_ref[...])
        acc = x * w_ref[3:4, :] + b_ref[...]
        for k in range(CONV_WIDTH - 1):
            acc += _shift_down(x, halo, 3 - k) * w_ref[k:k + 1, :]
        o_ref[...] = acc

    return _call(
        body, name=name, grid=(s_len // rows,),
        in_specs=[pl.BlockSpec((rows, width), lambda i: (i, 0)),
                  pl.BlockSpec((HALO, width), lambda i: (jnp.maximum(i * per - 1, 0), 0)),
                  pl.BlockSpec((CONV_WIDTH, width), lambda i: (0, 0)), pl.BlockSpec((1, width), lambda i: (0, 0))],
        out_specs=pl.BlockSpec((rows, width), lambda i: (i, 0)),
        out_shape=jax.ShapeDtypeStruct((s_len, width), F32),
        compiler_params=_params("parallel"),
    )(src, src, w, bias)


def _conv_bwd(dy, src, width, w, passthrough, *, name):
    s_len = src.shape[0]
    rows = min(CONV_ROWS, s_len)
    per = rows // HALO
    nblk = s_len // rows
    extra = [p.shape[1] for p in passthrough]
    total = width + sum(extra)

    def body(*refs):
        dy_ref, dyh_ref, x_ref, xh_ref, w_ref = refs[:5]
        p_refs = refs[5:5 + len(extra)]
        o_ref, dw_ref, db_ref = refs[5 + len(extra):]
        i = pl.program_id(0)

        @pl.when(i == 0)
        def _():
            dw_ref[...] = jnp.zeros_like(dw_ref)
            db_ref[...] = jnp.zeros_like(db_ref)

        dy = dy_ref[...]
        x = x_ref[...]
        dy_halo = jnp.where(i == nblk - 1, 0.0, dyh_ref[...])
        x_halo = jnp.where(i == 0, 0.0, xh_ref[...])
        dx = dy * w_ref[3:4, :]
        dw_ref[3] += jnp.sum(dy * x, axis=0, keepdims=True)
        for k in range(CONV_WIDTH - 1):
            dx += _shift_up(dy, dy_halo, 3 - k) * w_ref[k:k + 1, :]
            dw_ref[k] += jnp.sum(dy * _shift_down(x, x_halo, 3 - k), axis=0, keepdims=True)
        db_ref[...] += jnp.sum(dy, axis=0, keepdims=True)
        o_ref[:, 0:width] = dx.astype(o_ref.dtype)
        off = width
        for p_ref, wd in zip(p_refs, extra):
            o_ref[:, off:off + wd] = p_ref[...].astype(o_ref.dtype)
            off += wd

    in_specs = [pl.BlockSpec((rows, width), lambda i: (i, 0)),
                pl.BlockSpec((HALO, width), lambda i: (jnp.minimum((i + 1) * per, nblk * per - 1), 0)),
                pl.BlockSpec((rows, width), lambda i: (i, 0)),
                pl.BlockSpec((HALO, width), lambda i: (jnp.maximum(i * per - 1, 0), 0)),
                pl.BlockSpec((CONV_WIDTH, width), lambda i: (0, 0))]
    in_specs += [pl.BlockSpec((rows, wd), lambda i: (i, 0)) for wd in extra]
    return _call(
        body, name=name, grid=(nblk,), in_specs=in_specs,
        out_specs=[pl.BlockSpec((rows, total), lambda i: (i, 0)),
                   pl.BlockSpec((CONV_WIDTH, 1, width), lambda i: (0, 0, 0)), pl.BlockSpec((1, width), lambda i: (0, 0))],
        out_shape=[jax.ShapeDtypeStruct((s_len, total), MXU_DTYPE), jax.ShapeDtypeStruct((CONV_WIDTH, 1, width), F32),
                   jax.ShapeDtypeStruct((1, width), F32)],
        compiler_params=_params("arbitrary"),
    )(dy, dy, src, src, w, *passthrough)


def _attn_mask(first):
    i = _iota((A_BLOCK, 2 * A_BLOCK), 0)
    j = _iota((A_BLOCK, 2 * A_BLOCK), 1)
    band = (j > i) & (j <= i + A_BLOCK)
    return band & ((j >= A_BLOCK) | jnp.logical_not(first))


def _attn_group(p, mask, qg, kw, kws, vw, vws, azg, sink0, sink1):
    low = _iota(qg.shape, 1) < A_HEAD_DIM
    first_lane = (_iota((A_BLOCK, LANES), 1) == 0).astype(F32)
    out = None
    for half, sink in ((0, sink0), (1, sink1)):
        kv_head = (2 * p + half) // (A_HEADS // A_KV_HEADS)
        keep = low if half == 0 else jnp.logical_not(low)
        qm = jnp.where(keep, qg, 0.0)
        kk, vv = (kw, vw) if kv_head == half else (kws, vws)
        s = mm_nt(qm, kk) * (A_HEAD_DIM ** -0.5)
        s = jnp.where(mask, s, NEG)
        sk = jnp.sum(jnp.tile(sink, (A_BLOCK // 8, 1)) * first_lane, axis=1, keepdims=True)
        m = lax.stop_gradient(jnp.maximum(jnp.max(s, axis=1, keepdims=True), sk))
        e = jnp.exp(s - m)
        denom = jnp.sum(e, axis=1, keepdims=True) + jnp.exp(sk - m)
        o = mm_nn(e * (1.0 / denom), vv)
        o = jnp.where(keep, o, 0.0)
        out = o if out is None else out + o
    return out * _silu(azg)


def _attn_specs(s_len, rev):
    nb = s_len // A_BLOCK

    def cur(i):
        return nb - 1 - i if rev else i

    def prev(i):
        return jnp.maximum(cur(i) - 1, 0)

    def blk(width, col, which):
        return pl.BlockSpec((A_BLOCK, width), lambda i: (which(i), col))

    return [blk(A_WIDTH, 0, cur), blk(A_WIDTH, 1, cur), blk(LANES, 8, cur), blk(LANES, 9, cur),
            blk(LANES, 8, prev), blk(LANES, 9, prev), blk(LANES, 0, cur), blk(LANES, 0, cur),
            blk(LANES, 0, prev), blk(LANES, 0, prev)], cur


def _attn_fwd(proj_a, cos, sin, sinks_t, *, name):
    s_len = proj_a.shape[0]
    specs, _ = _attn_specs(s_len, False)

    def body(q_ref, az_ref, k_ref, v_ref, kp_ref, vp_ref, c_ref, s_ref, cp_ref, sp_ref, sink_ref, o_ref):
        first = pl.program_id(0) == 0
        mask = _attn_mask(first)
        qr = _rope(q_ref[...], c_ref[...], s_ref[...])
        kw = jnp.concatenate([_rope(kp_ref[...], cp_ref[...], sp_ref[...]), _rope(k_ref[...], c_ref[...], s_ref[...])], 0)
        vw = jnp.concatenate([vp_ref[...], v_ref[...]], 0)
        kws, vws = _swap64(kw), _swap64(vw)
        for p in range(A_WIDTH // LANES):
            cols = slice(p * LANES, (p + 1) * LANES)
            o = _attn_group(p, mask, qr[:, cols], kw, kws, vw, vws, az_ref[:, cols], sink_ref[2 * p], sink_ref[2 * p + 1])
            o_ref[:, cols] = o.astype(o_ref.dtype)

    return _call(
        body, name=name, grid=(s_len // A_BLOCK,),
        in_specs=specs + [pl.BlockSpec((A_HEADS, 8, LANES), lambda i: (0, 0, 0))],
        out_specs=pl.BlockSpec((A_BLOCK, A_WIDTH), lambda i: (i, 0)),
        out_shape=jax.ShapeDtypeStruct((s_len, A_WIDTH), MXU_DTYPE),
        compiler_params=_params("parallel"),
    )(proj_a, proj_a, proj_a, proj_a, proj_a, proj_a, cos, sin, cos, sin, sinks_t)


def _attn_bwd(proj_a, cos, sin, sinks_t, dya, *, name):
    s_len = proj_a.shape[0]
    specs, cur = _attn_specs(s_len, True)

    def body(q_ref, az_ref, k_ref, v_ref, kp_ref, vp_ref, c_ref, s_ref, cp_ref, sp_ref, sink_ref, dy_ref,
             o_ref, dsink_ref, dk_carry, dv_carry):
        i = pl.program_id(0)

        @pl.when(i == 0)
        def _():
            dsink_ref[...] = jnp.zeros_like(dsink_ref)
            dk_carry[...] = jnp.zeros_like(dk_carry)
            dv_carry[...] = jnp.zeros_like(dv_carry)

        first = cur(i) == 0
        mask = _attn_mask(first)
        cos_c, sin_c = c_ref[...], s_ref[...]
        qr = _rope(q_ref[...], cos_c, sin_c)
        kw = jnp.concatenate([_rope(kp_ref[...], cp_ref[...], sp_ref[...]), _rope(k_ref[...], cos_c, sin_c)], 0)
        vw = jnp.concatenate([vp_ref[...], v_ref[...]], 0)
        kws, vws = _swap64(kw), _swap64(vw)
        dkw = jnp.zeros_like(kw)
        dvw = jnp.zeros_like(vw)
        for p in range(A_WIDTH // LANES):
            cols = slice(p * LANES, (p + 1) * LANES)
            _, vjp = jax.vjp(functools.partial(_attn_group, p, mask), qr[:, cols], kw, kws, vw, vws, az_ref[:, cols],
                             sink_ref[2 * p], sink_ref[2 * p + 1])
            dq, dk1, dk2, dv1, dv2, daz, ds0, ds1 = vjp(dy_ref[:, cols])
            dkw += dk1 + _swap64(dk2)
            dvw += dv1 + _swap64(dv2)
            o_ref[:, cols] = _rope_t(dq, cos_c, sin_c).astype(o_ref.dtype)
            o_ref[:, A_WIDTH + p * LANES:A_WIDTH + (p + 1) * LANES] = daz.astype(o_ref.dtype)
            dsink_ref[2 * p] += ds0
            dsink_ref[2 * p + 1] += ds1
        o_ref[:, 2 * A_WIDTH:2 * A_WIDTH + LANES] = _rope_t(dkw[A_BLOCK:, :] + dk_carry[...], cos_c, sin_c).astype(o_ref.dtype)
        o_ref[:, 2 * A_WIDTH + LANES:] = (dvw[A_BLOCK:, :] + dv_carry[...]).astype(o_ref.dtype)
        dk_carry[...] = dkw[:A_BLOCK, :]
        dv_carry[...] = dvw[:A_BLOCK, :]

    return _call(
        body, name=name, grid=(s_len // A_BLOCK,),
        in_specs=specs + [pl.BlockSpec((A_HEADS, 8, LANES), lambda i: (0, 0, 0)),
                          pl.BlockSpec((A_BLOCK, A_WIDTH), lambda i: (cur(i), 0))],
        out_specs=[pl.BlockSpec((A_BLOCK, WA), lambda i: (cur(i), 0)),
                   pl.BlockSpec((A_HEADS, 8, LANES), lambda i: (0, 0, 0))],
        out_shape=[jax.ShapeDtypeStruct((s_len, WA), MXU_DTYPE), jax.ShapeDtypeStruct((A_HEADS, 8, LANES), F32)],
        scratch_shapes=[pltpu.VMEM((A_BLOCK, LANES), F32), pltpu.VMEM((A_BLOCK, LANES), F32)],
        compiler_params=_params("arbitrary"),
    )(proj_a, proj_a, proj_a, proj_a, proj_a, proj_a, cos, sin, cos, sin, sinks_t, dya)


RG_ROWS = 256


def _rg_gates(x, wa, ba, wx, bx, lam):
    r = jax.nn.sigmoid(mm_nn(x, wa) + ba)
    ig = jax.nn.sigmoid(mm_nn(x, wx) + bx)
    log_a = -R_C * r * _softplus(-lam)
    return jnp.exp(log_a), jnp.sqrt(_neg_expm1(2.0 * log_a)) * (ig * x)


def _rg_param_specs():
    mat = pl.BlockSpec((R_BLOCKS, R_BLOCK_DIM, R_BLOCK_DIM), lambda i: (0, 0, 0))
    vec = pl.BlockSpec((1, R_WIDTH), lambda i: (0, 0))
    return [mat, vec, mat, vec, vec]


def _rg_fwd(xr, proj_r, wa, ba, wx, bx, lam, *, name):
    s_len = xr.shape[0]
    rows = min(RG_ROWS, s_len)

    def body(x_ref, z_ref, wa_ref, ba_ref, wx_ref, bx_ref, lam_ref, h_ref, y_ref, a_buf, u_buf, carry):
        @pl.when(pl.program_id(0) == 0)
        def _():
            carry[...] = jnp.zeros_like(carry)

        for n in range(R_BLOCKS):
            cols = slice(n * R_BLOCK_DIM, (n + 1) * R_BLOCK_DIM)
            a, u = _rg_gates(x_ref[:, cols], wa_ref[n], ba_ref[:, cols], wx_ref[n], bx_ref[:, cols], lam_ref[:, cols])
            a_buf[:, cols] = a
            u_buf[:, cols] = u

        def step(t, h):
            h = a_buf[pl.ds(t, 1), :] * h + u_buf[pl.ds(t, 1), :]
            h_ref[pl.ds(t, 1), :] = h
            return h

        carry[...] = lax.fori_loop(0, rows, step, carry[...], unroll=8)
        y_ref[...] = (h_ref[...] * _silu(z_ref[...])).astype(y_ref.dtype)

    blk = pl.BlockSpec((rows, R_WIDTH), lambda i: (i, 0))
    return _call(
        body, name=name, grid=(s_len // rows,),
        in_specs=[blk, pl.BlockSpec((rows, R_WIDTH), lambda i: (i, 1))] + _rg_param_specs(),
        out_specs=[blk, blk],
        out_shape=[jax.ShapeDtypeStruct((s_len, R_WIDTH), F32), jax.ShapeDtypeStruct((s_len, R_WIDTH), MXU_DTYPE)],
        scratch_shapes=[pltpu.VMEM((rows, R_WIDTH), F32), pltpu.VMEM((rows, R_WIDTH), F32), pltpu.VMEM((1, R_WIDTH), F32)],
        compiler_params=_params("arbitrary"),
    )(xr, proj_r, wa, ba, wx, bx, lam)


def _rg_bwd(xr, proj_r, h, dyr, wa, ba, wx, bx, lam, *, name):
    s_len = xr.shape[0]
    rows = min(RG_ROWS, s_len)
    nblk = s_len // rows
    per = rows // HALO

    def cur(i):
        return nblk - 1 - i

    def body(x_ref, z_ref, h_ref, hh_ref, dy_ref, wa_ref, ba_ref, wx_ref, bx_ref, lam_ref,
             dx_ref, dz_ref, dwa_ref, dba_ref, dwx_ref, dbx_ref, dlam_ref, a_buf, g_buf, carry):
        i = pl.program_id(0)

        @pl.when(i == 0)
        def _():
            carry[...] = jnp.zeros_like(carry)
            for ref in (dwa_ref, dba_ref, dwx_ref, dbx_ref, dlam_ref):
                ref[...] = jnp.zeros_like(ref)

        z = z_ref[...]
        sig = jax.nn.sigmoid(z)
        hval = h_ref[...]
        dy = dy_ref[...]
        dz_ref[...] = dy * hval * (sig * (1.0 + z * (1.0 - sig)))
        g_buf[...] = dy * (z * sig)
        vjps = []
        for n in range(R_BLOCKS):
            cols = slice(n * R_BLOCK_DIM, (n + 1) * R_BLOCK_DIM)
            (a, _), vjp = jax.vjp(_rg_gates, x_ref[:, cols], wa_ref[n], ba_ref[:, cols], wx_ref[n], bx_ref[:, cols],
                                  lam_ref[:, cols])
            a_buf[:, cols] = a
            vjps.append(vjp)

        def step(k, c):
            t = rows - 1 - k
            g = g_buf[pl.ds(t, 1), :] + c
            g_buf[pl.ds(t, 1), :] = g
            return a_buf[pl.ds(t, 1), :] * g

        carry[...] = lax.fori_loop(0, rows, step, carry[...], unroll=8)
        h_halo = jnp.where(cur(i) == 0, 0.0, hh_ref[...])
        dh = g_buf[...]
        da = dh * _shift_down(hval, h_halo, 1)
        for n in range(R_BLOCKS):
            cols = slice(n * R_BLOCK_DIM, (n + 1) * R_BLOCK_DIM)
            dx, dwa, dba, dwx, dbx, dlam = vjps[n]((da[:, cols], dh[:, cols]))
            dx_ref[:, cols] = dx
            dwa_ref[n] += dwa
            dwx_ref[n] += dwx
            dba_ref[:, cols] += dba
            dbx_ref[:, cols] += dbx
            dlam_ref[:, cols] += dlam

    blk = pl.BlockSpec((rows, R_WIDTH), lambda i: (cur(i), 0))
    mat = pl.BlockSpec((R_BLOCKS, R_BLOCK_DIM, R_BLOCK_DIM), lambda i: (0, 0, 0))
    vec = pl.BlockSpec((1, R_WIDTH), lambda i: (0, 0))
    return _call(
        body, name=name, grid=(nblk,),
        in_specs=[blk, pl.BlockSpec((rows, R_WIDTH), lambda i: (cur(i), 1)), blk,
                  pl.BlockSpec((HALO, R_WIDTH), lambda i: (jnp.maximum(cur(i) * per - 1, 0), 0)), blk] + _rg_param_specs(),
        out_specs=[blk, blk, mat, vec, mat, vec, vec],
        out_shape=[jax.ShapeDtypeStruct((s_len, R_WIDTH), F32)] * 2 + [
            jax.ShapeDtypeStruct((R_BLOCKS, R_BLOCK_DIM, R_BLOCK_DIM), F32), jax.ShapeDtypeStruct((1, R_WIDTH), F32),
            jax.ShapeDtypeStruct((R_BLOCKS, R_BLOCK_DIM, R_BLOCK_DIM), F32), jax.ShapeDtypeStruct((1, R_WIDTH), F32),
            jax.ShapeDtypeStruct((1, R_WIDTH), F32)],
        scratch_shapes=[pltpu.VMEM((rows, R_WIDTH), F32), pltpu.VMEM((rows, R_WIDTH), F32), pltpu.VMEM((1, R_WIDTH), F32)],
        compiler_params=_params("arbitrary"),
    )(xr, proj_r, h, h, dyr, wa, ba, wx, bx, lam)


GP_CHUNKS = 4
GS_CHUNKS = 8


def _seg_cumsum(x, reverse):
    rows = x.shape[0]
    r = _iota(x.shape, 0) & (G_CHUNK - 1)
    s = 1
    while s < G_CHUNK:
        if reverse:
            x = x + jnp.where(r < G_CHUNK - s, pltpu.roll(x, rows - s, 0), 0.0)
        else:
            x = x + jnp.where(r >= s, pltpu.roll(x, s, 0), 0.0)
        s *= 2
    return x


def _gdn_decay(ga, a_log_row, dt_row):
    return -jnp.exp(a_log_row) * _softplus(ga + dt_row)


def _gdn_chunk(cq, ck, cv, gb, gc, inv=None):
    shape = cq.shape
    head = _iota(shape, 0) & (G_HEADS - 1)
    lane = _iota(shape, 2)
    q, k, v = _silu(cq), _silu(ck), _silu(cv)
    q = q * lax.rsqrt(jnp.sum(q * q, axis=-1, keepdims=True) + RMS_EPS) * (G_HEAD_DIM ** -0.5)
    k = k * lax.rsqrt(jnp.sum(k * k, axis=-1, keepdims=True) + RMS_EPS)
    beta = jnp.sum(jnp.where(lane == head, jax.nn.sigmoid(gb), 0.0), axis=-1, keepdims=True)
    g = jnp.sum(jnp.where(lane == head + G_HEADS, gc, 0.0), axis=-1, keepdims=True)
    sq = (shape[0], G_CHUNK, G_CHUNK)
    row, col = _iota(sq, 1), _iota(sq, 2)
    g_sq = jnp.broadcast_to(g, sq)
    decay = jnp.where(row >= col, jnp.exp(jnp.minimum(g_sq - _t(g_sq), 0.0)), 0.0)
    g_last = jnp.sum(jnp.where(_iota(g.shape, 1) == G_CHUNK - 1, g, 0.0), axis=1, keepdims=True)
    eg = jnp.exp(g)
    kb, vb = k * beta, v * beta
    m = jnp.where(row > col, mm_nt(kb, k) * decay, 0.0)
    known = inv is not None
    if not known:
        inv = _inv_unit_lower(m)
    u, w = _solve2(m, inv, vb, kb * eg)
    qk = jnp.where(row >= col, mm_nt(q, k) * decay, 0.0)
    q_dec = q * eg
    k_dec = k * jnp.exp(g_last - g)
    gl = jnp.broadcast_to(jnp.exp(g_last), (shape[0], 1, G_HEAD_DIM))
    return (u, w, qk, q_dec, k_dec, gl) if known else (u, w, qk, q_dec, k_dec, gl, inv)


def _gdn_step(state, u, w, qk, q_dec, k_dec, gl, gz, norm_w):
    v_new = u - mm_nn(w, state)
    o = mm_nn(q_dec, state) + mm_nn(qk, v_new)
    new_state = state * gl + mm_nn(_t(k_dec), v_new)
    o = o * lax.rsqrt(jnp.mean(o * o, axis=-1, keepdims=True) + RMS_EPS) * norm_w
    return o * _silu(gz), new_state


def _stack_chunks(x, heads):
    chunks = x.shape[0] // G_CHUNK
    parts = []
    for c in range(chunks):
        rows = slice(c * G_CHUNK, (c + 1) * G_CHUNK)
        for hd in range(G_HEADS):
            parts.append(x[rows, hd * LANES:(hd + 1) * LANES] if heads else x[rows, :])
    return jnp.stack(parts)


def _gdn_chunk_shapes(nch):
    b = nch * G_HEADS
    wide = jax.ShapeDtypeStruct((b, G_CHUNK, G_HEAD_DIM), F32)
    return [wide, wide, jax.ShapeDtypeStruct((b, G_CHUNK, G_CHUNK), F32), wide, wide,
            jax.ShapeDtypeStruct((b, 1, G_HEAD_DIM), F32)]


def _gdn_chunk_specs(nbatch):
    wide = pl.BlockSpec((nbatch, G_CHUNK, G_HEAD_DIM), lambda i: (i, 0, 0))
    return [wide, wide, pl.BlockSpec((nbatch, G_CHUNK, G_CHUNK), lambda i: (i, 0, 0)), wide, wide,
            pl.BlockSpec((nbatch, 1, G_HEAD_DIM), lambda i: (i, 0, 0))]


def _gdn_chunk_fwd(conv, proj_g, a_log_row, dt_row, *, name):
    s_len = conv.shape[0]
    cpg = min(GP_CHUNKS, s_len // G_CHUNK)
    rows = cpg * G_CHUNK
    nbatch = cpg * G_HEADS

    def body(c_ref, bg_ref, al_ref, dt_ref, *outs):
        bg = bg_ref[...]
        gc = _seg_cumsum(_gdn_decay(bg, al_ref[...], dt_ref[...]), False)
        res = _gdn_chunk(_stack_chunks(c_ref[:, 0:G_WIDTH], True), _stack_chunks(c_ref[:, G_WIDTH:2 * G_WIDTH], True),
                         _stack_chunks(c_ref[:, 2 * G_WIDTH:], True), _stack_chunks(bg, False), _stack_chunks(gc, False))
        for ref, val in zip(outs, res):
            ref[...] = val

    row = pl.BlockSpec((1, LANES), lambda i: (0, 0))
    return _call(
        body, name=name, grid=(s_len // rows,),
        in_specs=[pl.BlockSpec((rows, 3 * G_WIDTH), lambda i: (i, 0)),
                  pl.BlockSpec((rows, LANES), lambda i: (i, (3 * G_WIDTH + G_WIDTH) // LANES)), row, row],
        out_specs=_gdn_chunk_specs(nbatch) + [pl.BlockSpec((nbatch, G_CHUNK, G_CHUNK), lambda i: (i, 0, 0))],
        out_shape=_gdn_chunk_shapes(s_len // G_CHUNK) + [
            jax.ShapeDtypeStruct((s_len // G_CHUNK * G_HEADS, G_CHUNK, G_CHUNK), F32)],
        compiler_params=_params("parallel"),
    )(conv, proj_g, a_log_row, dt_row)


def _gdn_chunk_bwd(conv, proj_g, a_log_row, dt_row, inv, cots, *, name):
    s_len = conv.shape[0]
    cpg = min(GP_CHUNKS, s_len // G_CHUNK)
    rows = cpg * G_CHUNK
    nbatch = cpg * G_HEADS

    def unstack(x, heads):
        if heads:
            return jnp.concatenate([jnp.concatenate([x[c * G_HEADS + hd] for hd in range(G_HEADS)], axis=1)
                                    for c in range(cpg)], axis=0)
        return jnp.concatenate([sum(x[c * G_HEADS + hd] for hd in range(G_HEADS)) for c in range(cpg)], axis=0)

    def body(c_ref, bg_ref, al_ref, dt_ref, inv_ref, du, dw, dqk, dqd, dkd, dgl, dc_ref, dbg_ref, dal_ref, ddt_ref):
        @pl.when(pl.program_id(0) == 0)
        def _():
            dal_ref[...] = jnp.zeros_like(dal_ref)
            ddt_ref[...] = jnp.zeros_like(ddt_ref)

        bg = bg_ref[...]
        g_all, decay_vjp = jax.vjp(_gdn_decay, bg, al_ref[...], dt_ref[...])
        gc = _seg_cumsum(g_all, False)
        _, vjp = jax.vjp(_gdn_chunk, _stack_chunks(c_ref[:, 0:G_WIDTH], True),
                         _stack_chunks(c_ref[:, G_WIDTH:2 * G_WIDTH], True), _stack_chunks(c_ref[:, 2 * G_WIDTH:], True),
                         _stack_chunks(bg, False), _stack_chunks(gc, False), inv_ref[...])
        dq, dk, dv, dgb, dgc, _ = vjp((du[...], dw[...], dqk[...], dqd[...], dkd[...], dgl[...]))
        dc_ref[:, 0:G_WIDTH] = unstack(dq, True)
        dc_ref[:, G_WIDTH:2 * G_WIDTH] = unstack(dk, True)
        dc_ref[:, 2 * G_WIDTH:] = unstack(dv, True)
        dga, dal, ddt = decay_vjp(_seg_cumsum(unstack(dgc, False), True))
        dbg_ref[:, 0:LANES] = unstack(dgb, False) + dga
        dbg_ref[:, LANES:] = jnp.zeros((rows, LANES), F32)
        dal_ref[...] += dal
        ddt_ref[...] += ddt

    row = pl.BlockSpec((1, LANES), lambda i: (0, 0))
    return _call(
        body, name=name, grid=(s_len // rows,),
        in_specs=[pl.BlockSpec((rows, 3 * G_WIDTH), lambda i: (i, 0)),
                  pl.BlockSpec((rows, LANES), lambda i: (i, (3 * G_WIDTH + G_WIDTH) // LANES)), row, row,
                  pl.BlockSpec((nbatch, G_CHUNK, G_CHUNK), lambda i: (i, 0, 0))]
        + _gdn_chunk_specs(nbatch),
        out_specs=[pl.BlockSpec((rows, 3 * G_WIDTH), lambda i: (i, 0)), pl.BlockSpec((rows, 2 * LANES), lambda i: (i, 0)),
                   row, row],
        out_shape=[jax.ShapeDtypeStruct((s_len, 3 * G_WIDTH), F32), jax.ShapeDtypeStruct((s_len, 2 * LANES), F32),
                   jax.ShapeDtypeStruct((1, LANES), F32), jax.ShapeDtypeStruct((1, LANES), F32)],
        compiler_params=_params("arbitrary"),
    )(conv, proj_g, a_log_row, dt_row, inv, *cots)


def _gdn_scan_specs(cpg, which):
    nbatch = cpg * G_HEADS
    wide = pl.BlockSpec((nbatch, G_CHUNK, G_HEAD_DIM), lambda i: (which(i), 0, 0))
    return [wide, wide, pl.BlockSpec((nbatch, G_CHUNK, G_CHUNK), lambda i: (which(i), 0, 0)), wide, wide,
            pl.BlockSpec((nbatch, 1, G_HEAD_DIM), lambda i: (which(i), 0, 0))]


def _gz_stack(z_ref, c):
    rows = pl.ds(pl.multiple_of(c * G_CHUNK, G_CHUNK), G_CHUNK)
    return jnp.stack([z_ref[rows, hd * LANES:(hd + 1) * LANES] for hd in range(G_HEADS)])


def _gdn_scan_fwd(chunk_vals, proj_g, norm_w, *, name):
    s_len = proj_g.shape[0]
    nch = s_len // G_CHUNK
    cpg = min(GS_CHUNKS, nch)
    rows = cpg * G_CHUNK

    def body(u_ref, w_ref, qk_ref, qd_ref, kd_ref, gl_ref, z_ref, nw_ref, y_ref, st_ref, state):
        @pl.when(pl.program_id(0) == 0)
        def _():
            state[...] = jnp.zeros_like(state)

        def step(c, carry):
            b = pl.ds(pl.multiple_of(c * G_HEADS, G_HEADS), G_HEADS)
            st = state[...]
            st_ref[b] = st
            y, new_state = _gdn_step(st, u_ref[b], w_ref[b], qk_ref[b], qd_ref[b], kd_ref[b], gl_ref[b],
                                     _gz_stack(z_ref, c), nw_ref[...])
            state[...] = new_state
            rws = pl.ds(pl.multiple_of(c * G_CHUNK, G_CHUNK), G_CHUNK)
            for hd in range(G_HEADS):
                y_ref[rws, hd * LANES:(hd + 1) * LANES] = y[hd].astype(y_ref.dtype)
            return carry

        lax.fori_loop(0, cpg, step, 0)

    return _call(
        body, name=name, grid=(nch // cpg,),
        in_specs=_gdn_scan_specs(cpg, lambda i: i) + [
            pl.BlockSpec((rows, G_WIDTH), lambda i: (i, 3)), pl.BlockSpec((1, G_HEAD_DIM), lambda i: (0, 0))],
        out_specs=[pl.BlockSpec((rows, G_WIDTH), lambda i: (i, 0)),
                   pl.BlockSpec((cpg * G_HEADS, G_HEAD_DIM, G_HEAD_DIM), lambda i: (i, 0, 0))],
        out_shape=[jax.ShapeDtypeStruct((s_len, G_WIDTH), MXU_DTYPE),
                   jax.ShapeDtypeStruct((nch * G_HEADS, G_HEAD_DIM, G_HEAD_DIM), F32)],
        scratch_shapes=[pltpu.VMEM((G_HEADS, G_HEAD_DIM, G_HEAD_DIM), F32)],
        compiler_params=_params("arbitrary"),
    )(*chunk_vals, proj_g, norm_w)


def _gdn_scan_bwd(chunk_vals, states, proj_g, norm_w, dyg, *, name):
    s_len = proj_g.shape[0]
    nch = s_len // G_CHUNK
    cpg = min(GS_CHUNKS, nch)
    rows = cpg * G_CHUNK
    ngrid = nch // cpg

    def cur(i):
        return ngrid - 1 - i

    def body(u_ref, w_ref, qk_ref, qd_ref, kd_ref, gl_ref, st_ref, z_ref, nw_ref, dy_ref,
             du_ref, dw_ref, dqk_ref, dqd_ref, dkd_ref, dgl_ref, dz_ref, dnw_ref, dstate):
        @pl.when(pl.program_id(0) == 0)
        def _():
            dstate[...] = jnp.zeros_like(dstate)
            dnw_ref[...] = jnp.zeros_like(dnw_ref)

        def step(k, carry):
            c = cpg - 1 - k
            b = pl.ds(pl.multiple_of(c * G_HEADS, G_HEADS), G_HEADS)
            _, vjp = jax.vjp(_gdn_step, st_ref[b], u_ref[b], w_ref[b], qk_ref[b], qd_ref[b], kd_ref[b], gl_ref[b],
                             _gz_stack(z_ref, c), nw_ref[...])
            dst, du, dw, dqk, dqd, dkd, dgl, dz, dnw = vjp((_gz_stack(dy_ref, c), dstate[...]))
            dstate[...] = dst
            du_ref[b], dw_ref[b], dqk_ref[b], dqd_ref[b], dkd_ref[b], dgl_ref[b] = du, dw, dqk, dqd, dkd, dgl
            rws = pl.ds(pl.multiple_of(c * G_CHUNK, G_CHUNK), G_CHUNK)
            for hd in range(G_HEADS):
                dz_ref[rws, hd * LANES:(hd + 1) * LANES] = dz[hd]
            dnw_ref[...] += dnw
            return carry

        lax.fori_loop(0, cpg, step, 0)

    gate = pl.BlockSpec((rows, G_WIDTH), lambda i: (cur(i), 3))
    wide = pl.BlockSpec((rows, G_WIDTH), lambda i: (cur(i), 0))
    vec = pl.BlockSpec((1, G_HEAD_DIM), lambda i: (0, 0))
    return _call(
        body, name=name, grid=(ngrid,),
        in_specs=_gdn_scan_specs(cpg, cur) + [
            pl.BlockSpec((cpg * G_HEADS, G_HEAD_DIM, G_HEAD_DIM), lambda i: (cur(i), 0, 0)), gate, vec, wide],
        out_specs=_gdn_scan_specs(cpg, cur) + [wide, vec],
        out_shape=_gdn_chunk_shapes(nch) + [jax.ShapeDtypeStruct((s_len, G_WIDTH), F32),
                                            jax.ShapeDtypeStruct((1, G_HEAD_DIM), F32)],
        scratch_shapes=[pltpu.VMEM((G_HEADS, G_HEAD_DIM, G_HEAD_DIM), F32)],
        compiler_params=_params("arbitrary"),
    )(*chunk_vals, states, proj_g, norm_w, dyg)


def _adamw_math(w, g, m, v):
    m = ADAM_B1 * m + (1.0 - ADAM_B1) * g
    v = ADAM_B2 * v + (1.0 - ADAM_B2) * (g * g)
    m_hat = m / (1.0 - ADAM_B1 ** ADAM_STEP)
    v_hat = v / (1.0 - ADAM_B2 ** ADAM_STEP)
    delta = -ADAM_LR * (m_hat / (jnp.sqrt(v_hat) + ADAM_EPS) + ADAM_WD * w)
    return delta, m, v


def _sum_adamw(own, chip, parts, w, m, v, *, name, rows):
    n_layers, n_rows, n_cols = w.shape
    rows = min(rows, n_rows)
    n_parts = parts.shape[0]

    def body(c_ref, o_ref, p_ref, w_ref, m_ref, v_ref, g_ref, d_ref, nm_ref, nv_ref):
        g = o_ref[0, 0].astype(F32)
        for k in range(n_parts):
            g = g + p_ref[k, 0].astype(F32)
        delta, new_m, new_v = _adamw_math(w_ref[0], g, m_ref[0], v_ref[0])
        g_ref[0], d_ref[0], nm_ref[0], nv_ref[0] = g, delta, new_m, new_v

    blk = pl.BlockSpec((1, rows, n_cols), lambda l, i, c: (l, i, 0))
    grid_spec = pltpu.PrefetchScalarGridSpec(
        num_scalar_prefetch=1, grid=(n_layers, n_rows // rows),
        in_specs=[pl.BlockSpec((1, 1, rows, n_cols), lambda l, i, c: (c[0], l, i, 0)),
                  pl.BlockSpec((n_parts, 1, rows, n_cols), lambda l, i, c: (0, l, i, 0)), blk, blk, blk],
        out_specs=[blk] * 4)
    return _call(
        body, name=name, grid_spec=grid_spec, out_shape=[jax.ShapeDtypeStruct(w.shape, F32)] * 4,
        compiler_params=_params("parallel", "parallel"),
    )(_index_operand(chip), own, parts, w, m, v)


def _sum_slots(parts, *, name):
    rows = parts.shape[1]

    def body(p_ref, o_ref):
        g = p_ref[0]
        for k in range(1, N_DEV):
            g = g + p_ref[k]
        o_ref[...] = g

    return _call(body, name=name, grid=(1,),
                 in_specs=[pl.BlockSpec(parts.shape, lambda i: (0, 0, 0))],
                 out_specs=pl.BlockSpec((rows, LANES), lambda i: (0, 0)),
                 out_shape=jax.ShapeDtypeStruct((rows, LANES), F32), compiler_params=_params("arbitrary"))(parts)


def _adamw_packed(w, g, m, v, *, name):
    def body(w_ref, g_ref, m_ref, v_ref, d_ref, nm_ref, nv_ref):
        d_ref[...], nm_ref[...], nv_ref[...] = _adamw_math(w_ref[...], g_ref[...], m_ref[...], v_ref[...])

    blk = pl.BlockSpec(w.shape, lambda i: (0, 0))
    return _call(body, name=name, grid=(1,), in_specs=[blk] * 4, out_specs=[blk] * 3,
                 out_shape=[jax.ShapeDtypeStruct(w.shape, F32)] * 3, compiler_params=_params("arbitrary"))(w, g, m, v)


A_COLS = ((0, 512), (768, 1280), (512, 768))
R_COLS = ((1280, 3328),)
G_COLS = ((3328, 5384),)


def _group_weights(wt_full):
    def take(ranges):
        return jnp.concatenate([wt_full[a:b] for a, b in ranges], axis=0)

    wt_g = jnp.concatenate([take(G_COLS), jnp.zeros((G_PAD, wt_full.shape[1]), wt_full.dtype)], axis=0)
    return take(A_COLS), take(R_COLS), wt_g


def _ungroup_grads(d_a, d_r, d_g):
    return jnp.concatenate([d_a[0:512], d_a[1024:1280], d_a[512:1024], d_r, d_g[:WG - G_PAD]], axis=0)


def _shard_rows(w):
    return jnp.pad(jnp.transpose(w, (0, 2, 1)), ((0, 0), (0, N_ROWS_PAD - N_IN_SHARD), (0, 0)))


def _unshard_rows(wt):
    return jnp.transpose(wt[:, :N_IN_SHARD], (0, 2, 1))


def _owner_blocks(dwt):
    blocks = jnp.pad(dwt.reshape(4, 2, N_IN_SHARD, D_MODEL), ((0, 0), (0, 0), (0, N_ROWS_PAD - N_IN_SHARD), (0, 0)))
    return jnp.transpose(blocks, (1, 0, 2, 3))


def _rope_tables(s_len):
    inv = 1.0 / (ROPE_THETA ** (jnp.arange(0, A_HEAD_DIM, 2, dtype=F32) / A_HEAD_DIM))
    ang = jnp.arange(s_len, dtype=F32)[:, None] * inv[None, :]
    cos, sin = jnp.cos(ang), jnp.sin(ang)
    return jnp.tile(cos, (1, 4)), jnp.tile(jnp.concatenate([-sin, sin], axis=1), (1, 2))


def _pack(leaves):
    rows = []
    for leaf in leaves:
        flat = leaf.reshape(-1)
        pad = (-flat.shape[0]) % (8 * LANES)
        rows.append(jnp.pad(flat, (0, pad)).reshape(-1, LANES))
    return jnp.concatenate(rows, axis=0)


def _unpack(packed, shapes):
    out, row = [], 0
    for shape in shapes:
        size = math.prod(shape)
        nrows = -(-size // (8 * LANES)) * 8
        out.append(packed[row:row + nrows].reshape(-1)[:size].reshape(shape))
        row += nrows
    return out


def _lane_row(vals, offset):
    return jnp.pad(vals, (offset, LANES - offset - vals.shape[0])).reshape(1, LANES)


def kernel(x, w_in, sinks, r_conv_w, r_conv_b, r_wa, r_ba, r_wx, r_bx, r_lam, g_conv_w, g_a_log, g_dt_bias, g_norm_w, w_out, ln_g, ln_b, loss_target, m_w_in, m_sinks, m_r_conv_w, m_r_conv_b, m_r_wa, m_r_ba, m_r_wx, m_r_bx, m_r_lam, m_g_conv_w, m_g_a_log, m_g_dt_bias, m_g_norm_w, m_w_out, m_ln_g, m_ln_b, v_w_in, v_sinks, v_r_conv_w, v_r_conv_b, v_r_wa, v_r_ba, v_r_wx, v_r_bx, v_r_lam, v_g_conv_w, v_g_a_log, v_g_dt_bias, v_g_norm_w, v_w_out, v_ln_g, v_ln_b):
    s_len = x.shape[1]
    x0 = x.reshape(s_len, D_MODEL)
    target = loss_target.reshape(s_len, D_MODEL)
    me = 4 * lax.axis_index("x") + 2 * lax.axis_index("y") + lax.axis_index("c")

    win_all, wout_all, rcw_all, gcw_all = _all_gather(
        [_shard_rows(w_in).astype(MXU_DTYPE).reshape(2 * DEPTH, N_ROWS_PAD // 2, D_MODEL),
         w_out.astype(MXU_DTYPE).reshape(2 * DEPTH, OUT_SHARD // 2, D_MODEL), r_conv_w[None], g_conv_w[None]],
        "gather_weights")
    win_all = win_all.reshape(N_DEV, DEPTH, N_ROWS_PAD, D_MODEL)
    wout_all = wout_all.reshape(N_DEV, DEPTH, OUT_SHARD, D_MODEL)
    rcw_full = jnp.moveaxis(rcw_all[:, 0], 0, 2).reshape(DEPTH, CONV_WIDTH, R_WIDTH)
    gcw_full = jnp.moveaxis(gcw_all[:, 0], 0, 2).reshape(DEPTH, CONV_WIDTH, 3 * G_WIDTH)
    cos, sin = _rope_tables(s_len)

    layers = []
    for l in range(DEPTH):
        wt_a, wt_r, wt_g = _group_weights(win_all[:, l, :N_IN_SHARD].reshape(N_IN, D_MODEL))
        wo = wout_all[:, l].reshape(D_MODEL, D_MODEL)
        layers.append(dict(
            wt_a=wt_a, wt_r=wt_r, wt_g=wt_g, wo=wo,
            wo_a=wo[0:A_WIDTH], wo_r=wo[A_WIDTH:A_WIDTH + R_WIDTH], wo_g=wo[A_WIDTH + R_WIDTH:],
            sinks_t=jnp.broadcast_to(sinks[l][:, None, None], (A_HEADS, 8, LANES)),
            rcw=rcw_full[l], rcb=r_conv_b[l].reshape(1, R_WIDTH), wa=r_wa[l], ba=r_ba[l].reshape(1, R_WIDTH),
            wx=r_wx[l], bx=r_bx[l].reshape(1, R_WIDTH), lam=r_lam[l].reshape(1, R_WIDTH),
            gcw=gcw_full[l], zero_b=jnp.zeros((1, 3 * G_WIDTH), F32),
            a_log=_lane_row(g_a_log[l], G_HEADS), dt=_lane_row(g_dt_bias[l], G_HEADS),
            norm_w=g_norm_w[l].reshape(1, G_HEAD_DIM), ln_g=ln_g[l].reshape(1, D_MODEL), ln_b=ln_b[l].reshape(1, D_MODEL)))

    saved = []
    xin = xin_lo = x0
    for l, p in enumerate(layers):
        proj_a = _matmul([xin_lo], [p["wt_a"]], name=f"proj_a{l}", tm=1024, tn=640, b_t=True)
        proj_r = _matmul([xin_lo], [p["wt_r"]], name=f"proj_r{l}", tm=1024, tn=512, b_t=True)
        proj_g = _matmul([xin_lo], [p["wt_g"]], name=f"proj_g{l}", tm=1024, tn=768, b_t=True)
        ya = _attn_fwd(proj_a, cos, sin, p["sinks_t"], name=f"attn_fwd{l}")
        xr = _conv_fwd(proj_r, R_WIDTH, p["rcw"], p["rcb"], name=f"rconv_fwd{l}")
        h, yr = _rg_fwd(xr, proj_r, p["wa"], p["ba"], p["wx"], p["bx"], p["lam"], name=f"rglru_fwd{l}")
        conv = _conv_fwd(proj_g, 3 * G_WIDTH, p["gcw"], p["zero_b"], name=f"gconv_fwd{l}")
        *chunk_vals, inv = _gdn_chunk_fwd(conv, proj_g, p["a_log"], p["dt"], name=f"gdn_chunk_fwd{l}")
        yg, states = _gdn_scan_fwd(chunk_vals, proj_g, p["norm_w"], name=f"gdn_scan_fwd{l}")
        z, xout, xout_lo = _outproj_ln(ya, yr, yg, p["wo"], xin, p["ln_g"], p["ln_b"], name=f"outproj_ln{l}")
        saved.append(dict(xin_lo=xin_lo, proj_a=proj_a, proj_r=proj_r, proj_g=proj_g, ya=ya, yr=yr, yg=yg, xr=xr, h=h,
                          conv=conv, chunk_vals=chunk_vals, inv=inv, states=states, z=z))
        xin, xin_lo = xout, xout_lo

    grads = [None] * DEPTH
    dxn = None
    loss_local = None
    for l in reversed(range(DEPTH)):
        p, sv = layers[l], saved[l]
        if dxn is None:
            dz, dz_lo, dln_g, dln_b, loss_local = _ln_bwd(sv["z"], p["ln_g"], name=f"ln_bwd{l}", xn=xin, target=target)
        else:
            dz, dz_lo, dln_g, dln_b = _ln_bwd(sv["z"], p["ln_g"], name=f"ln_bwd{l}", dxn=dxn)
        dya = _matmul([dz_lo], [p["wo_a"]], name=f"dya{l}", tm=1024, tn=512, b_t=True)
        dyr = _matmul([dz_lo], [p["wo_r"]], name=f"dyr{l}", tm=1024, tn=512, b_t=True)
        dyg = _matmul([dz_lo], [p["wo_g"]], name=f"dyg{l}", tm=1024, tn=512, b_t=True)
        dwo = jnp.concatenate([
            _matmul_tn(sv["ya"], dz_lo, name=f"dwo_a{l}", tm=512, tn=1024, tk=1024),
            _matmul_tn(sv["yr"], dz_lo, name=f"dwo_r{l}", tm=1024, tn=1024, tk=1024),
            _matmul_tn(sv["yg"], dz_lo, name=f"dwo_g{l}", tm=512, tn=1024, tk=1024)], axis=0)

        dproj_a, dsinks_t = _attn_bwd(sv["proj_a"], cos, sin, p["sinks_t"], dya, name=f"attn_bwd{l}")

        dxr, drz, dwa, dba, dwx, dbx, dlam = _rg_bwd(sv["xr"], sv["proj_r"], sv["h"], dyr, p["wa"], p["ba"], p["wx"],
                                                     p["bx"], p["lam"], name=f"rglru_bwd{l}")
        dproj_r, drcw, drcb = _conv_bwd(dxr, sv["proj_r"], R_WIDTH, p["rcw"], [drz], name=f"rconv_bwd{l}")

        scan_out = _gdn_scan_bwd(sv["chunk_vals"], sv["states"], sv["proj_g"], p["norm_w"], dyg, name=f"gdn_scan_bwd{l}")
        dgz, dnorm_w = scan_out[6], scan_out[7]
        dconv, dbg, dal, ddt = _gdn_chunk_bwd(sv["conv"], sv["proj_g"], p["a_log"], p["dt"], sv["inv"], scan_out[:6],
                                              name=f"gdn_chunk_bwd{l}")
        dproj_g, dgcw, _ = _conv_bwd(dconv, sv["proj_g"], 3 * G_WIDTH, p["gcw"], [dgz, dbg], name=f"gconv_bwd{l}")

        dxn = _matmul([dproj_a, dproj_r, dproj_g], [p["wt_a"], p["wt_r"], p["wt_g"]], name=f"dx{l}", tm=512, tn=512,
                      add=dz, add_scale=DEEPNORM_ALPHA)
        dwin = _ungroup_grads(_matmul_tn(dproj_a, sv["xin_lo"], name=f"dwin_a{l}", tm=640, tn=1024, tk=1024),
                              _matmul_tn(dproj_r, sv["xin_lo"], name=f"dwin_r{l}", tm=1024, tn=1024, tk=1024),
                              _matmul_tn(dproj_g, sv["xin_lo"], name=f"dwin_g{l}", tm=1152, tn=1024, tk=1024))
        grads[l] = dict(
            w_in=dwin, w_out=dwo, sinks=dsinks_t[:, :, 0].sum(axis=1), r_conv_w=drcw.reshape(CONV_WIDTH, R_WIDTH),
            r_conv_b=drcb.reshape(R_WIDTH), r_wa=dwa, r_ba=dba.reshape(R_WIDTH), r_wx=dwx, r_bx=dbx.reshape(R_WIDTH),
            r_lam=dlam.reshape(R_WIDTH), g_conv_w=dgcw.reshape(CONV_WIDTH, 3 * G_WIDTH),
            g_a_log=dal[0, G_HEADS:2 * G_HEADS], g_dt_bias=ddt[0, G_HEADS:2 * G_HEADS],
            g_norm_w=dnorm_w.reshape(G_HEAD_DIM), ln_g=dln_g.reshape(D_MODEL), ln_b=dln_b.reshape(D_MODEL))
    grad_x = dxn.reshape(x.shape)
    loss = lax.psum(loss_local[0, 0], ("x", "y", "c"))

    def stacked(name):
        return jnp.stack([grads[l][name] for l in range(DEPTH)])

    dwin_blocks = jnp.stack([_owner_blocks(grads[l]["w_in"]) for l in range(DEPTH)], axis=2).astype(MXU_DTYPE)
    dwout_blocks = jnp.transpose(stacked("w_out").reshape(DEPTH, 4, 2, OUT_SHARD, D_MODEL), (2, 1, 0, 3, 4)).astype(MXU_DTYPE)
    core, chip = lax.axis_index("c"), 2 * lax.axis_index("x") + lax.axis_index("y")
    half = N_ROWS_PAD // 2
    got_win, got_wout = _swap_cores(
        [dwin_blocks.reshape(2, 4 * DEPTH * 2, half, D_MODEL), dwout_blocks.reshape(2, 4 * DEPTH, OUT_SHARD, D_MODEL)],
        "swap_core_grads")
    chip_win = _add_pair(dwin_blocks, got_win.reshape(dwin_blocks.shape[1:]), core, name="add_core_grads_w_in", rows=352)
    chip_wout = _add_pair(dwout_blocks, got_wout.reshape(dwout_blocks.shape[1:]), core, name="add_core_grads_w_out", rows=256)
    win_parts, wout_parts = _exchange_chips([chip_win, chip_wout], "exchange_chip_grads")
    w_in_t = [_unshard_rows(t) for t in _sum_adamw(chip_win, chip, win_parts, _shard_rows(w_in), _shard_rows(m_w_in),
                                                   _shard_rows(v_w_in), name="adamw_w_in", rows=176)]
    g_w_in, d_w_in, nm_w_in, nv_w_in = w_in_t
    g_w_out, d_w_out, nm_w_out, nv_w_out = _sum_adamw(chip_wout, chip, wout_parts, w_out, m_w_out, v_w_out,
                                                      name="adamw_w_out", rows=128)

    small = ["sinks", "r_conv_w", "r_conv_b", "r_wa", "r_ba", "r_wx", "r_bx", "r_lam", "g_conv_w", "g_a_log",
             "g_dt_bias", "g_norm_w", "ln_g", "ln_b"]
    full_shapes = [stacked(nm).shape for nm in small]
    packed_small = _pack([stacked(nm) for nm in small])
    (all_small,) = _all_gather([packed_small.reshape(4, packed_small.shape[0] // 4, LANES)], "gather_small_grads")
    all_small = all_small.reshape(N_DEV, packed_small.shape[0], LANES)
    g_small = dict(zip(small, _unpack(_sum_slots(all_small, name="sum_small_grads"), full_shapes)))
    g_small["r_conv_w"] = lax.dynamic_slice_in_dim(g_small["r_conv_w"], me * (R_WIDTH // N_DEV), R_WIDTH // N_DEV, axis=2)
    g_small["g_conv_w"] = lax.dynamic_slice_in_dim(g_small["g_conv_w"], me * (3 * G_WIDTH // N_DEV), 3 * G_WIDTH // N_DEV, axis=2)
    given = dict(sinks=(sinks, m_sinks, v_sinks), r_conv_w=(r_conv_w, m_r_conv_w, v_r_conv_w),
                 r_conv_b=(r_conv_b, m_r_conv_b, v_r_conv_b), r_wa=(r_wa, m_r_wa, v_r_wa), r_ba=(r_ba, m_r_ba, v_r_ba),
                 r_wx=(r_wx, m_r_wx, v_r_wx), r_bx=(r_bx, m_r_bx, v_r_bx), r_lam=(r_lam, m_r_lam, v_r_lam),
                 g_conv_w=(g_conv_w, m_g_conv_w, v_g_conv_w), g_a_log=(g_a_log, m_g_a_log, v_g_a_log),
                 g_dt_bias=(g_dt_bias, m_g_dt_bias, v_g_dt_bias), g_norm_w=(g_norm_w, m_g_norm_w, v_g_norm_w),
                 ln_g=(ln_g, m_ln_g, v_ln_g), ln_b=(ln_b, m_ln_b, v_ln_b))
    shard_shapes = [given[nm][0].shape for nm in small]
    packed = [_pack([given[nm][k] for nm in small]) for k in range(3)]
    d_p, nm_p, nv_p = _adamw_packed(packed[0], _pack([g_small[nm] for nm in small]), packed[1], packed[2], name="adamw_small")
    d_small = dict(zip(small, _unpack(d_p, shard_shapes)))
    nm_small = dict(zip(small, _unpack(nm_p, shard_shapes)))
    nv_small = dict(zip(small, _unpack(nv_p, shard_shapes)))

    order = ["w_in"] + small[:12] + ["w_out"] + small[12:]

    def leaf(big_in, big_out, table):
        return [big_in if nm == "w_in" else big_out if nm == "w_out" else table[nm] for nm in order]

    return (loss, grad_x, *leaf(g_w_in, g_w_out, g_small), *leaf(d_w_in, d_w_out, d_small),
            *leaf(nm_w_in, nm_w_out, nm_small), *leaf(nv_w_in, nv_w_out, nv_small))
```

```python
import functools
import math

import jax
import jax.numpy as jnp
from jax import lax
from jax.experimental import pallas as pl
from jax.experimental.pallas import tpu as pltpu

F32 = jnp.float32
MXU_DTYPE = jnp.bfloat16
HIGHEST = lax.Precision.HIGHEST
MESH_ID = pl.DeviceIdType.MESH

N_DEV = 8
DEPTH = 2
D_MODEL = 2048
A_HEADS, A_KV_HEADS, A_HEAD_DIM = 8, 2, 64
A_WIDTH, A_KV_WIDTH = 512, 128
A_BLOCK = 128
ROPE_THETA = 10000.0
R_WIDTH, R_BLOCKS, R_BLOCK_DIM = 1024, 8, 128
R_C = 8.0
CONV_WIDTH = 4
G_HEADS, G_HEAD_DIM, G_WIDTH, G_CHUNK = 4, 128, 512, 64
N_IN = 5384
N_IN_SHARD = N_IN // N_DEV
N_ROWS_PAD = 704
OUT_SHARD = D_MODEL // N_DEV
WA, WR, WG = 1280, 2048, 2304
G_PAD = WG - (3 * G_WIDTH + G_WIDTH + 2 * G_HEADS)
DEEPNORM_ALPHA = (2 * DEPTH) ** 0.25
LN_EPS = 1e-5
RMS_EPS = 1e-6
ADAM_LR, ADAM_B1, ADAM_B2, ADAM_EPS, ADAM_WD, ADAM_STEP = 0.001, 0.9, 0.999, 1e-08, 0.01, 10
NEG = -1e30
VMEM_LIMIT = 56 * 1024 * 1024
LANES = 128


def _call(body, **kw):
    return pl.pallas_call(body, **kw)


def _params(*sem):
    return pltpu.CompilerParams(dimension_semantics=sem, vmem_limit_bytes=VMEM_LIMIT)


def _t(x):
    return jnp.swapaxes(x, -1, -2)


def _raw_dot(a, b, ca, cb, precision=None):
    batch = tuple(range(a.ndim - 2))
    if precision is None:
        a, b = a.astype(MXU_DTYPE), b.astype(MXU_DTYPE)
    return lax.dot_general(a, b, (((ca,), (cb,)), (batch, batch)), precision=precision,
                           preferred_element_type=F32)


def _nn(a, b, precision=None):
    return _raw_dot(a, b, a.ndim - 1, b.ndim - 2, precision)


def _nt(a, b, precision=None):
    return _raw_dot(a, b, a.ndim - 1, b.ndim - 1, precision)


@jax.custom_vjp
def mm_nn(a, b):
    return _nn(a, b)


def _mm_nn_fwd(a, b):
    return _nn(a, b), (a, b)


def _mm_nn_bwd(res, g):
    a, b = res
    return _nt(g, b), _nn(_t(a), g)


mm_nn.defvjp(_mm_nn_fwd, _mm_nn_bwd)


@jax.custom_vjp
def mm_nt(a, b):
    return _nt(a, b)


def _mm_nt_fwd(a, b):
    return _nt(a, b), (a, b)


def _mm_nt_bwd(res, g):
    a, b = res
    return _nn(g, b), _nn(_t(g), a)


mm_nt.defvjp(_mm_nt_fwd, _mm_nt_bwd)


def _split(x):
    hi = x.astype(MXU_DTYPE)
    return hi, (x - hi.astype(F32)).astype(MXU_DTYPE)


def _hmm(a, b, nt=False):
    dot = _nt if nt else _nn
    return dot(a[0], b[0]) + (dot(a[0], b[1]) + dot(a[1], b[0]))


def _silu(x):
    return x * jax.nn.sigmoid(x)


def _softplus(x):
    return jnp.maximum(x, 0.0) + jnp.log1p(jnp.exp(-jnp.abs(x)))


def _neg_expm1(x):
    series = -x * (1.0 + x * 0.5 * (1.0 + x * (1.0 / 3.0) * (1.0 + x * 0.25 * (1.0 + x * 0.2))))
    return jnp.where(x > -0.125, series, 1.0 - jnp.exp(x))


def _iota(shape, dim):
    return lax.broadcasted_iota(jnp.int32, shape, dim)


def _inv_unit_lower(m):
    shape = m.shape
    row, col = _iota(shape, 1), _iota(shape, 2)
    eye = (row == col).astype(F32)

    def blockdiag(size):
        return (row // size) == (col // size)

    x = -jnp.where(blockdiag(8), m, 0.0)
    xs = _split(x)
    x2s = _split(_hmm(xs, xs))
    x4s = _split(_hmm(x2s, x2s))
    inv = eye + x
    inv = inv + _hmm(_split(inv), x2s)
    inv = inv + _hmm(_split(inv), x4s)
    for size in (8, 16, 32):
        below = jnp.where(blockdiag(2 * size) & jnp.logical_not(blockdiag(size)), m, 0.0)
        invs = _split(inv)
        inv = inv - _hmm(_split(_hmm(invs, _split(below))), invs)
    return inv


@jax.custom_vjp
def _solve2(m, inv, r1, r2):
    invs = _split(inv)
    return _hmm(invs, _split(r1)), _hmm(invs, _split(r2))


def _solve2_fwd(m, inv, r1, r2):
    x1, x2 = _solve2(m, inv, r1, r2)
    return (x1, x2), (inv, x1, x2)


def _solve2_bwd(res, g):
    inv, x1, x2 = res
    inv_ts = _split(_t(inv))
    d1, d2 = _hmm(inv_ts, _split(g[0])), _hmm(inv_ts, _split(g[1]))
    dm = -(_hmm(_split(d1), _split(x1), nt=True) + _hmm(_split(d2), _split(x2), nt=True))
    return dm, jnp.zeros_like(inv), d1, d2


_solve2.defvjp(_solve2_fwd, _solve2_bwd)


def _swap_halves(x):
    n = x.shape[-1]
    lane = _iota(x.shape, x.ndim - 1)
    return jnp.where((lane & 63) < 32, pltpu.roll(x, n - 32, x.ndim - 1), pltpu.roll(x, 32, x.ndim - 1))


def _rope(x, cos, sin):
    reps = x.shape[-1] // LANES
    if reps > 1:
        cos, sin = jnp.tile(cos, (1, reps)), jnp.tile(sin, (1, reps))
    return x * cos + _swap_halves(x) * sin


def _rope_t(d, cos, sin):
    reps = d.shape[-1] // LANES
    if reps > 1:
        cos, sin = jnp.tile(cos, (1, reps)), jnp.tile(sin, (1, reps))
    return d * cos + _swap_halves(d * sin)


def _swap64(x):
    return pltpu.roll(x, 64, x.ndim - 1)


def _mesh_pos():
    return lax.axis_index("x"), lax.axis_index("y"), lax.axis_index("c")


def _all_gather(arrays, name):
    n = len(arrays)
    npieces = [a.shape[0] for a in arrays]
    pmax = max(npieces)

    def body(*refs):
        ins, outs = refs[:n], refs[n:2 * n]
        send_sems, recv_sems, local_sem = refs[2 * n:]
        x, y, c = _mesh_pos()
        me, sibling = (x, y, c), (x, y, 1 - c)
        chips = [(1 - x, y), (x, 1 - y), (1 - x, 1 - y)]

        def slot(a, pos, p):
            return outs[a].at[4 * pos[0] + 2 * pos[1] + pos[2], p]

        def copy(a, p, k, block, to, own=False):
            return pltpu.make_async_remote_copy(
                src_ref=ins[a].at[p] if own else slot(a, block, p), dst_ref=slot(a, block, p),
                send_sem=send_sems.at[a, p, k], recv_sem=recv_sems.at[a, p, k], device_id=to, device_id_type=MESH_ID)

        pieces = [(a, p) for p in range(pmax) for a in range(n) if p < npieces[a]]
        mine = [pltpu.make_async_copy(ins[a].at[p], slot(a, me, p), local_sem.at[a, p]) for a, p in pieces]
        for cp in mine:
            cp.start()
        first = []
        for a, p in pieces:
            first += [copy(a, p, 1 + j, me, (*chip, c), own=True) for j, chip in enumerate(chips)]
            first.append(copy(a, p, 0, me, sibling, own=True))
        for cp in first:
            cp.start()
        passed = []
        for a, p in pieces:
            for j, chip in enumerate(chips):
                copy(a, p, 1 + j, (*chip, c), me).wait_recv()
                cp = copy(a, p, 4 + j, (*chip, c), sibling)
                cp.start()
                passed.append(cp)
        for a, p in pieces:
            copy(a, p, 0, sibling, me).wait_recv()
            for j, chip in enumerate(chips):
                copy(a, p, 4 + j, (*chip, 1 - c), me).wait_recv()
        for cp in first + passed:
            cp.wait_send()
        for cp in mine:
            cp.wait()

    any_spec = pl.BlockSpec(memory_space=pl.ANY)
    return _call(
        body, name=name,
        out_shape=[jax.ShapeDtypeStruct((N_DEV,) + a.shape, a.dtype) for a in arrays],
        in_specs=[any_spec] * n, out_specs=[any_spec] * n,
        scratch_shapes=[pltpu.SemaphoreType.DMA((n, pmax, 7)), pltpu.SemaphoreType.DMA((n, pmax, 7)),
                        pltpu.SemaphoreType.DMA((n, pmax))],
    )(*arrays)


def _swap_cores(arrays, name):
    n = len(arrays)
    pmax = max(a.shape[1] for a in arrays)

    def body(*refs):
        ins, got = refs[:n], refs[n:2 * n]
        send_sems, recv_sems = refs[2 * n:]
        x, y, c = _mesh_pos()
        copies = [pltpu.make_async_remote_copy(
            src_ref=ins[a].at[1 - c, p], dst_ref=got[a].at[p], send_sem=send_sems.at[a, p], recv_sem=recv_sems.at[a, p],
            device_id=(x, y, 1 - c), device_id_type=MESH_ID) for a in range(n) for p in range(arrays[a].shape[1])]
        for cp in copies:
            cp.start()
        for cp in copies:
            cp.wait()

    any_spec = pl.BlockSpec(memory_space=pl.ANY)
    return _call(
        body, name=name, out_shape=[jax.ShapeDtypeStruct(a.shape[1:], a.dtype) for a in arrays],
        in_specs=[any_spec] * n, out_specs=[any_spec] * n,
        scratch_shapes=[pltpu.SemaphoreType.DMA((n, pmax)), pltpu.SemaphoreType.DMA((n, pmax))],
    )(*arrays)


class _ChipExchange:
    def __init__(self, arrays):
        self.arrays = list(arrays)
        n = len(self.arrays)
        self.out_shape = [jax.ShapeDtypeStruct((3,) + a.shape[1:], a.dtype) for a in self.arrays]
        self.scratch = [pltpu.SemaphoreType.DMA((n, 3)), pltpu.SemaphoreType.DMA((n, 3))]

    def _copies(self, ins, outs, send_sems, recv_sems):
        x, y, c = _mesh_pos()
        copies = []
        for a in range(len(self.arrays)):
            for k in range(1, 4):
                px, py = x ^ (k >> 1), y ^ (k & 1)
                copies.append(pltpu.make_async_remote_copy(
                    src_ref=ins[a].at[2 * px + py], dst_ref=outs[a].at[k - 1], send_sem=send_sems.at[a, k - 1],
                    recv_sem=recv_sems.at[a, k - 1], device_id=(px, py, c), device_id_type=MESH_ID))
        return copies

    def start(self, ins, outs, send_sems, recv_sems):
        for cp in self._copies(ins, outs, send_sems, recv_sems):
            cp.start()

    def finish(self, ins, outs, send_sems, recv_sems):
        copies = self._copies(ins, outs, send_sems, recv_sems)
        for cp in copies:
            cp.wait_recv()
        for cp in copies:
            cp.wait_send()


def _index_operand(i):
    return jnp.reshape(i, (1,)).astype(jnp.int32)


def _add_pair(pair, other, core, *, name, rows):
    _, n_slots, n_layers, n_rows, n_cols = pair.shape
    rows = min(rows, n_rows)

    def body(c_ref, a_ref, b_ref, o_ref):
        o_ref[...] = (a_ref[0].astype(F32) + b_ref[...].astype(F32)).astype(o_ref.dtype)

    blk = pl.BlockSpec((1, 1, rows, n_cols), lambda s, l, i, c: (s, l, i, 0))
    grid_spec = pltpu.PrefetchScalarGridSpec(
        num_scalar_prefetch=1, grid=(n_slots, n_layers, n_rows // rows),
        in_specs=[pl.BlockSpec((1, 1, 1, rows, n_cols), lambda s, l, i, c: (c[0], s, l, i, 0)), blk], out_specs=blk)
    return _call(body, name=name, grid_spec=grid_spec, out_shape=jax.ShapeDtypeStruct(other.shape, pair.dtype),
                 compiler_params=_params("parallel", "parallel", "parallel"))(_index_operand(core), pair, other)


def _matmul(a_list, b_list, *, name, tm, tn, b_t=False, out_dtype=F32, add=None, add_scale=1.0, comm=None):
    n = len(a_list)
    m_rows, n_cols = a_list[0].shape[0], b_list[0].shape[0 if b_t else 1]
    tm, tn = min(tm, m_rows), min(tn, n_cols)
    grid = (m_rows // tm, n_cols // tn)
    n_in = 2 * n + (add is not None)
    n_comm = len(comm.arrays) if comm is not None else 0

    def body(*refs):
        a_refs, b_refs = refs[:n], refs[n:2 * n]
        o_ref = refs[n_in + n_comm]
        if comm is not None:
            comm_refs = (refs[n_in:n_in + n_comm], refs[n_in + n_comm + 1:n_in + 2 * n_comm + 1], *refs[-2:])
            i, j = pl.program_id(0), pl.program_id(1)

            @pl.when((i == 0) & (j == 0))
            def _():
                comm.start(*comm_refs)

        acc = None
        for a_ref, b_ref in zip(a_refs, b_refs):
            part = lax.dot_general(a_ref[...].astype(MXU_DTYPE), b_ref[...].astype(MXU_DTYPE),
                                   (((1,), (1 if b_t else 0,)), ((), ())), preferred_element_type=F32)
            acc = part if acc is None else acc + part
        if add is not None:
            acc = acc + add_scale * refs[2 * n][...]
        o_ref[...] = acc.astype(o_ref.dtype)
        if comm is not None:
            @pl.when((i == grid[0] - 1) & (j == grid[1] - 1))
            def _():
                comm.finish(*comm_refs)

    in_specs = [pl.BlockSpec((tm, a.shape[1]), lambda i, j: (i, 0)) for a in a_list]
    if b_t:
        in_specs += [pl.BlockSpec((tn, b.shape[1]), lambda i, j: (j, 0)) for b in b_list]
    else:
        in_specs += [pl.BlockSpec((b.shape[0], tn), lambda i, j: (0, j)) for b in b_list]
    args = list(a_list) + list(b_list)
    if add is not None:
        in_specs.append(pl.BlockSpec((tm, tn), lambda i, j: (i, j)))
        args.append(add)
    out_specs = pl.BlockSpec((tm, tn), lambda i, j: (i, j))
    out_shape = jax.ShapeDtypeStruct((m_rows, n_cols), out_dtype)
    if comm is None:
        return _call(body, name=name, grid=grid, in_specs=in_specs, out_specs=out_specs, out_shape=out_shape,
                     compiler_params=_params("parallel", "arbitrary"))(*args)
    any_spec = pl.BlockSpec(memory_space=pl.ANY)
    outs = _call(body, name=name, grid=grid, in_specs=in_specs + [any_spec] * n_comm,
                 out_specs=[out_specs] + [any_spec] * n_comm, out_shape=[out_shape] + comm.out_shape,
                 scratch_shapes=comm.scratch, compiler_params=_params("arbitrary", "arbitrary"))(*args, *comm.arrays)
    return outs[0], outs[1:]


def _matmul_tn(a, b, *, name, tm, tn, tk):
    k_rows, m_rows = a.shape
    n_cols = b.shape[1]
    tm, tn, tk = min(tm, m_rows), min(tn, n_cols), min(tk, k_rows)
    nk = k_rows // tk

    def body(a_ref, b_ref, o_ref):
        part = lax.dot_general(a_ref[...].astype(MXU_DTYPE), b_ref[...].astype(MXU_DTYPE),
                               (((0,), (0,)), ((), ())), preferred_element_type=F32)

        @pl.when(pl.program_id(2) == 0)
        def _():
            o_ref[...] = part

        @pl.when(pl.program_id(2) > 0)
        def _():
            o_ref[...] += part

    return _call(
        body, name=name, grid=(m_rows // tm, n_cols // tn, nk),
        in_specs=[pl.BlockSpec((tk, tm), lambda i, j, k: (k, i)), pl.BlockSpec((tk, tn), lambda i, j, k: (k, j))],
        out_specs=pl.BlockSpec((tm, tn), lambda i, j, k: (i, j)),
        out_shape=jax.ShapeDtypeStruct((m_rows, n_cols), F32),
        compiler_params=_params("parallel", "parallel", "arbitrary"),
    )(a, b)


def _outproj_ln(ya, yr, yg, w_out, x, ln_g, ln_b, *, name):
    s_len = x.shape[0]
    tm = min(256, s_len)

    def body(ya_ref, yr_ref, yg_ref, w_ref, x_ref, g_ref, b_ref, z_ref, o_ref, lo_ref):
        acc = jnp.dot(ya_ref[...], w_ref[0:A_WIDTH, :], preferred_element_type=F32)
        acc += jnp.dot(yr_ref[...], w_ref[A_WIDTH:A_WIDTH + R_WIDTH, :], preferred_element_type=F32)
        acc += jnp.dot(yg_ref[...], w_ref[A_WIDTH + R_WIDTH:, :], preferred_element_type=F32)
        z = DEEPNORM_ALPHA * x_ref[...] + acc
        z_ref[...] = z
        mu = jnp.mean(z, axis=-1, keepdims=True)
        zc = z - mu
        var = jnp.mean(zc * zc, axis=-1, keepdims=True)
        out = zc * lax.rsqrt(var + LN_EPS) * g_ref[...] + b_ref[...]
        o_ref[...] = out
        lo_ref[...] = out.astype(lo_ref.dtype)

    def rows(width):
        return pl.BlockSpec((tm, width), lambda i: (i, 0))

    def whole(shape):
        return pl.BlockSpec(shape, lambda i: (0, 0))

    return _call(
        body, name=name, grid=(s_len // tm,),
        in_specs=[rows(A_WIDTH), rows(R_WIDTH), rows(G_WIDTH), whole((D_MODEL, D_MODEL)), rows(D_MODEL),
                  whole((1, D_MODEL)), whole((1, D_MODEL))],
        out_specs=[rows(D_MODEL)] * 3,
        out_shape=[jax.ShapeDtypeStruct((s_len, D_MODEL), F32)] * 2 + [jax.ShapeDtypeStruct((s_len, D_MODEL), MXU_DTYPE)],
        compiler_params=_params("parallel"),
    )(ya, yr, yg, w_out, x, ln_g, ln_b)


def _ln_bwd(z, ln_g, *, name, dxn=None, xn=None, target=None):
    s_len = z.shape[0]
    tm = min(256, s_len)
    top = dxn is None

    def body(*refs):
        if top:
            z_ref, g_ref, xn_ref, t_ref, dz_ref, lo_ref, dg_ref, db_ref, loss_ref = refs
            err = xn_ref[...] - t_ref[...]
            dy = err * (1.0 / D_MODEL)
        else:
            z_ref, g_ref, dy_ref, dz_ref, lo_ref, dg_ref, db_ref = refs
            dy = dy_ref[...]
        first = pl.program_id(0) == 0

        @pl.when(first)
        def _():
            dg_ref[...] = jnp.zeros_like(dg_ref)
            db_ref[...] = jnp.zeros_like(db_ref)
            if top:
                loss_ref[...] = jnp.zeros_like(loss_ref)

        z = z_ref[...]
        mu = jnp.mean(z, axis=-1, keepdims=True)
        zc = z - mu
        rstd = lax.rsqrt(jnp.mean(zc * zc, axis=-1, keepdims=True) + LN_EPS)
        xhat = zc * rstd
        dxh = dy * g_ref[...]
        dz = rstd * (dxh - jnp.mean(dxh, axis=-1, keepdims=True) - xhat * jnp.mean(dxh * xhat, axis=-1, keepdims=True))
        dz_ref[...] = dz
        lo_ref[...] = dz.astype(lo_ref.dtype)
        dg_ref[...] += jnp.sum(dy * xhat, axis=0, keepdims=True)
        db_ref[...] += jnp.sum(dy, axis=0, keepdims=True)
        if top:
            per_row = jnp.sum(err * err, axis=-1, keepdims=True) * (0.5 / D_MODEL)
            loss_ref[...] += jnp.sum(per_row, axis=0, keepdims=True)

    rows = pl.BlockSpec((tm, D_MODEL), lambda i: (i, 0))
    vec = pl.BlockSpec((1, D_MODEL), lambda i: (0, 0))
    in_specs = [rows, vec] + ([rows, rows] if top else [rows])
    args = [z, ln_g] + ([xn, target] if top else [dxn])
    out_specs = [rows, rows, vec, vec]
    out_shape = [jax.ShapeDtypeStruct((s_len, D_MODEL), F32), jax.ShapeDtypeStruct((s_len, D_MODEL), MXU_DTYPE),
                 jax.ShapeDtypeStruct((1, D_MODEL), F32), jax.ShapeDtypeStruct((1, D_MODEL), F32)]
    if top:
        out_specs.append(pl.BlockSpec((1, 1), lambda i: (0, 0)))
        out_shape.append(jax.ShapeDtypeStruct((1, 1), F32))
    return _call(body, name=name, grid=(s_len // tm,), in_specs=in_specs, out_specs=out_specs,
                 out_shape=out_shape, compiler_params=_params("arbitrary"))(*args)


CONV_ROWS = 256
HALO = 8


def _shift_down(x, halo, s):
    if s == 0:
        return x
    ext = jnp.concatenate([halo, x], axis=0)
    return pltpu.roll(ext, s, 0)[HALO:, :]


def _shift_up(x, halo, s):
    if s == 0:
        return x
    ext = jnp.concatenate([x, halo], axis=0)
    return pltpu.roll(ext, ext.shape[0] - s, 0)[:x.shape[0], :]


def _conv_fwd(src, width, w, bias, *, name):
    s_len = src.shape[0]
    rows = min(CONV_ROWS, s_len)
    per = rows // HALO

    def body(x_ref, halo_ref, w_ref, b_ref, o_ref):
        x = x_ref[...]
        halo = jnp.where(pl.program_id(0) == 0, 0.0, halo_ref[...])
        acc = x * w_ref[3:4, :] + b_ref[...]
        for k in range(CONV_WIDTH - 1):
            acc += _shift_down(x, halo, 3 - k) * w_ref[k:k + 1, :]
        o_ref[...] = acc

    return _call(
        body, name=name, grid=(s_len // rows,),
        in_specs=[pl.BlockSpec((rows, width), lambda i: (i, 0)),
                  pl.BlockSpec((HALO, width), lambda i: (jnp.maximum(i * per - 1, 0), 0)),
                  pl.BlockSpec((CONV_WIDTH, width), lambda i: (0, 0)), pl.BlockSpec((1, width), lambda i: (0, 0))],
        out_specs=pl.BlockSpec((rows, width), lambda i: (i, 0)),
        out_shape=jax.ShapeDtypeStruct((s_len, width), F32),
        compiler_params=_params("parallel"),
    )(src, src, w, bias)


def _conv_bwd(dy, src, width, w, passthrough, *, name):
    s_len = src.shape[0]
    rows = min(CONV_ROWS, s_len)
    per = rows // HALO
    nblk = s_len // rows
    extra = [p.shape[1] for p in passthrough]
    total = width + sum(extra)

    def body(*refs):
        dy_ref, dyh_ref, x_ref, xh_ref, w_ref = refs[:5]
        p_refs = refs[5:5 + len(extra)]
        o_ref, dw_ref, db_ref = refs[5 + len(extra):]
        i = pl.program_id(0)

        @pl.when(i == 0)
        def _():
            dw_ref[...] = jnp.zeros_like(dw_ref)
            db_ref[...] = jnp.zeros_like(db_ref)

        dy = dy_ref[...]
        x = x_ref[...]
        dy_halo = jnp.where(i == nblk - 1, 0.0, dyh_ref[...])
        x_halo = jnp.where(i == 0, 0.0, xh_ref[...])
        dx = dy * w_ref[3:4, :]
        dw_ref[3] += jnp.sum(dy * x, axis=0, keepdims=True)
        for k in range(CONV_WIDTH - 1):
            dx += _shift_up(dy, dy_halo, 3 - k) * w_ref[k:k + 1, :]
            dw_ref[k] += jnp.sum(dy * _shift_down(x, x_halo, 3 - k), axis=0, keepdims=True)
        db_ref[...] += jnp.sum(dy, axis=0, keepdims=True)
        o_ref[:, 0:width] = dx.astype(o_ref.dtype)
        off = width
        for p_ref, wd in zip(p_refs, extra):
            o_ref[:, off:off + wd] = p_ref[...].astype(o_ref.dtype)
            off += wd

    in_specs = [pl.BlockSpec((rows, width), lambda i: (i, 0)),
                pl.BlockSpec((HALO, width), lambda i: (jnp.minimum((i + 1) * per, nblk * per - 1), 0)),
                pl.BlockSpec((rows, width), lambda i: (i, 0)),
                pl.BlockSpec((HALO, width), lambda i: (jnp.maximum(i * per - 1, 0), 0)),
                pl.BlockSpec((CONV_WIDTH, width), lambda i: (0, 0))]
    in_specs += [pl.BlockSpec((rows, wd), lambda i: (i, 0)) for wd in extra]
    return _call(
        body, name=name, grid=(nblk,), in_specs=in_specs,
        out_specs=[pl.BlockSpec((rows, total), lambda i: (i, 0)),
                   pl.BlockSpec((CONV_WIDTH, 1, width), lambda i: (0, 0, 0)), pl.BlockSpec((1, width), lambda i: (0, 0))],
        out_shape=[jax.ShapeDtypeStruct((s_len, total), MXU_DTYPE), jax.ShapeDtypeStruct((CONV_WIDTH, 1, width), F32),
                   jax.ShapeDtypeStruct((1, width), F32)],
        compiler_params=_params("arbitrary"),
    )(dy, dy, src, src, w, *passthrough)


def _attn_mask(first):
    i = _iota((A_BLOCK, 2 * A_BLOCK), 0)
    j = _iota((A_BLOCK, 2 * A_BLOCK), 1)
    band = (j > i) & (j <= i + A_BLOCK)
    return band & ((j >= A_BLOCK) | jnp.logical_not(first))


def _attn_group(p, mask, qg, kw, kws, vw, vws, azg, sink0, sink1):
    low = _iota(qg.shape, 1) < A_HEAD_DIM
    first_lane = (_iota((A_BLOCK, LANES), 1) == 0).astype(F32)
    out = None
    for half, sink in ((0, sink0), (1, sink1)):
        kv_head = (2 * p + half) // (A_HEADS // A_KV_HEADS)
        keep = low if half == 0 else jnp.logical_not(low)
        qm = jnp.where(keep, qg, 0.0)
        kk, vv = (kw, vw) if kv_head == half else (kws, vws)
        s = mm_nt(qm, kk) * (A_HEAD_DIM ** -0.5)
        s = jnp.where(mask, s, NEG)
        sk = jnp.sum(jnp.tile(sink, (A_BLOCK // 8, 1)) * first_lane, axis=1, keepdims=True)
        m = lax.stop_gradient(jnp.maximum(jnp.max(s, axis=1, keepdims=True), sk))
        e = jnp.exp(s - m)
        denom = jnp.sum(e, axis=1, keepdims=True) + jnp.exp(sk - m)
        o = mm_nn(e * (1.0 / denom), vv)
        o = jnp.where(keep, o, 0.0)
        out = o if out is None else out + o
    return out * _silu(azg)


def _attn_specs(s_len, rev):
    nb = s_len // A_BLOCK

    def cur(i):
        return nb - 1 - i if rev else i

    def prev(i):
        return jnp.maximum(cur(i) - 1, 0)

    def blk(width, col, which):
        return pl.BlockSpec((A_BLOCK, width), lambda i: (which(i), col))

    return [blk(A_WIDTH, 0, cur), blk(A_WIDTH, 1, cur), blk(LANES, 8, cur), blk(LANES, 9, cur),
            blk(LANES, 8, prev), blk(LANES, 9, prev), blk(LANES, 0, cur), blk(LANES, 0, cur),
            blk(LANES, 0, prev), blk(LANES, 0, prev)], cur


def _attn_fwd(proj_a, cos, sin, sinks_t, *, name):
    s_len = proj_a.shape[0]
    specs, _ = _attn_specs(s_len, False)

    def body(q_ref, az_ref, k_ref, v_ref, kp_ref, vp_ref, c_ref, s_ref, cp_ref, sp_ref, sink_ref, o_ref):
        first = pl.program_id(0) == 0
        mask = _attn_mask(first)
        qr = _rope(q_ref[...], c_ref[...], s_ref[...])
        kw = jnp.concatenate([_rope(kp_ref[...], cp_ref[...], sp_ref[...]), _rope(k_ref[...], c_ref[...], s_ref[...])], 0)
        vw = jnp.concatenate([vp_ref[...], v_ref[...]], 0)
        kws, vws = _swap64(kw), _swap64(vw)
        for p in range(A_WIDTH // LANES):
            cols = slice(p * LANES, (p + 1) * LANES)
            o = _attn_group(p, mask, qr[:, cols], kw, kws, vw, vws, az_ref[:, cols], sink_ref[2 * p], sink_ref[2 * p + 1])
            o_ref[:, cols] = o.astype(o_ref.dtype)

    return _call(
        body, name=name, grid=(s_len // A_BLOCK,),
        in_specs=specs + [pl.BlockSpec((A_HEADS, 8, LANES), lambda i: (0, 0, 0))],
        out_specs=pl.BlockSpec((A_BLOCK, A_WIDTH), lambda i: (i, 0)),
        out_shape=jax.ShapeDtypeStruct((s_len, A_WIDTH), MXU_DTYPE),
        compiler_params=_params("parallel"),
    )(proj_a, proj_a, proj_a, proj_a, proj_a, proj_a, cos, sin, cos, sin, sinks_t)


def _attn_bwd(proj_a, cos, sin, sinks_t, dya, *, name):
    s_len = proj_a.shape[0]
    specs, cur = _attn_specs(s_len, True)

    def body(q_ref, az_ref, k_ref, v_ref, kp_ref, vp_ref, c_ref, s_ref, cp_ref, sp_ref, sink_ref, dy_ref,
             o_ref, dsink_ref, dk_carry, dv_carry):
        i = pl.program_id(0)

        @pl.when(i == 0)
        def _():
            dsink_ref[...] = jnp.zeros_like(dsink_ref)
            dk_carry[...] = jnp.zeros_like(dk_carry)
            dv_carry[...] = jnp.zeros_like(dv_carry)

        first = cur(i) == 0
        mask = _attn_mask(first)
        cos_c, sin_c = c_ref[...], s_ref[...]
        qr = _rope(q_ref[...], cos_c, sin_c)
        kw = jnp.concatenate([_rope(kp_ref[...], cp_ref[...], sp_ref[...]), _rope(k_ref[...], cos_c, sin_c)], 0)
        vw = jnp.concatenate([vp_ref[...], v_ref[...]], 0)
        kws, vws = _swap64(kw), _swap64(vw)
        dkw = jnp.zeros_like(kw)
        dvw = jnp.zeros_like(vw)
        for p in range(A_WIDTH // LANES):
            cols = slice(p * LANES, (p + 1) * LANES)
            _, vjp = jax.vjp(functools.partial(_attn_group, p, mask), qr[:, cols], kw, kws, vw, vws, az_ref[:, cols],
                             sink_ref[2 * p], sink_ref[2 * p + 1])
            dq, dk1, dk2, dv1, dv2, daz, ds0, ds1 = vjp(dy_ref[:, cols])
            dkw += dk1 + _swap64(dk2)
            dvw += dv1 + _swap64(dv2)
            o_ref[:, cols] = _rope_t(dq, cos_c, sin_c).astype(o_ref.dtype)
            o_ref[:, A_WIDTH + p * LANES:A_WIDTH + (p + 1) * LANES] = daz.astype(o_ref.dtype)
            dsink_ref[2 * p] += ds0
            dsink_ref[2 * p + 1] += ds1
        o_ref[:, 2 * A_WIDTH:2 * A_WIDTH + LANES] = _rope_t(dkw[A_BLOCK:, :] + dk_carry[...], cos_c, sin_c).astype(o_ref.dtype)
        o_ref[:, 2 * A_WIDTH + LANES:] = (dvw[A_BLOCK:, :] + dv_carry[...]).astype(o_ref.dtype)
        dk_carry[...] = dkw[:A_BLOCK, :]
        dv_carry[...] = dvw[:A_BLOCK, :]

    return _call(
        body, name=name, grid=(s_len // A_BLOCK,),
        in_specs=specs + [pl.BlockSpec((A_HEADS, 8, LANES), lambda i: (0, 0, 0)),
                          pl.BlockSpec((A_BLOCK, A_WIDTH), lambda i: (cur(i), 0))],
        out_specs=[pl.BlockSpec((A_BLOCK, WA), lambda i: (cur(i), 0)),
                   pl.BlockSpec((A_HEADS, 8, LANES), lambda i: (0, 0, 0))],
        out_shape=[jax.ShapeDtypeStruct((s_len, WA), MXU_DTYPE), jax.ShapeDtypeStruct((A_HEADS, 8, LANES), F32)],
        scratch_shapes=[pltpu.VMEM((A_BLOCK, LANES), F32), pltpu.VMEM((A_BLOCK, LANES), F32)],
        compiler_params=_params("arbitrary"),
    )(proj_a, proj_a, proj_a, proj_a, proj_a, proj_a, cos, sin, cos, sin, sinks_t, dya)


RG_ROWS = 256


def _rg_gates(x, wa, ba, wx, bx, lam):
    r = jax.nn.sigmoid(mm_nn(x, wa) + ba)
    ig = jax.nn.sigmoid(mm_nn(x, wx) + bx)
    log_a = -R_C * r * _softplus(-lam)
    return jnp.exp(log_a), jnp.sqrt(_neg_expm1(2.0 * log_a)) * (ig * x)


def _rg_param_specs():
    mat = pl.BlockSpec((R_BLOCKS, R_BLOCK_DIM, R_BLOCK_DIM), lambda i: (0, 0, 0))
    vec = pl.BlockSpec((1, R_WIDTH), lambda i: (0, 0))
    return [mat, vec, mat, vec, vec]


def _rg_fwd(xr, proj_r, wa, ba, wx, bx, lam, *, name):
    s_len = xr.shape[0]
    rows = min(RG_ROWS, s_len)

    def body(x_ref, z_ref, wa_ref, ba_ref, wx_ref, bx_ref, lam_ref, h_ref, y_ref, a_buf, u_buf, carry):
        @pl.when(pl.program_id(0) == 0)
        def _():
            carry[...] = jnp.zeros_like(carry)

        for n in range(R_BLOCKS):
            cols = slice(n * R_BLOCK_DIM, (n + 1) * R_BLOCK_DIM)
            a, u = _rg_gates(x_ref[:, cols], wa_ref[n], ba_ref[:, cols], wx_ref[n], bx_ref[:, cols], lam_ref[:, cols])
            a_buf[:, cols] = a
            u_buf[:, cols] = u

        def step(t, h):
            h = a_buf[pl.ds(t, 1), :] * h + u_buf[pl.ds(t, 1), :]
            h_ref[pl.ds(t, 1), :] = h
            return h

        carry[...] = lax.fori_loop(0, rows, step, carry[...], unroll=8)
        y_ref[...] = (h_ref[...] * _silu(z_ref[...])).astype(y_ref.dtype)

    blk = pl.BlockSpec((rows, R_WIDTH), lambda i: (i, 0))
    return _call(
        body, name=name, grid=(s_len // rows,),
        in_specs=[blk, pl.BlockSpec((rows, R_WIDTH), lambda i: (i, 1))] + _rg_param_specs(),
        out_specs=[blk, blk],
        out_shape=[jax.ShapeDtypeStruct((s_len, R_WIDTH), F32), jax.ShapeDtypeStruct((s_len, R_WIDTH), MXU_DTYPE)],
        scratch_shapes=[pltpu.VMEM((rows, R_WIDTH), F32), pltpu.VMEM((rows, R_WIDTH), F32), pltpu.VMEM((1, R_WIDTH), F32)],
        compiler_params=_params("arbitrary"),
    )(xr, proj_r, wa, ba, wx, bx, lam)


def _rg_bwd(xr, proj_r, h, dyr, wa, ba, wx, bx, lam, *, name):
    s_len = xr.shape[0]
    rows = min(RG_ROWS, s_len)
    nblk = s_len // rows
    per = rows // HALO

    def cur(i):
        return nblk - 1 - i

    def body(x_ref, z_ref, h_ref, hh_ref, dy_ref, wa_ref, ba_ref, wx_ref, bx_ref, lam_ref,
             dx_ref, dz_ref, dwa_ref, dba_ref, dwx_ref, dbx_ref, dlam_ref, a_buf, g_buf, carry):
        i = pl.program_id(0)

        @pl.when(i == 0)
        def _():
            carry[...] = jnp.zeros_like(carry)
            for ref in (dwa_ref, dba_ref, dwx_ref, dbx_ref, dlam_ref):
                ref[...] = jnp.zeros_like(ref)

        z = z_ref[...]
        sig = jax.nn.sigmoid(z)
        hval = h_ref[...]
        dy = dy_ref[...]
        dz_ref[...] = dy * hval * (sig * (1.0 + z * (1.0 - sig)))
        g_buf[...] = dy * (z * sig)
        vjps = []
        for n in range(R_BLOCKS):
            cols = slice(n * R_BLOCK_DIM, (n + 1) * R_BLOCK_DIM)
            (a, _), vjp = jax.vjp(_rg_gates, x_ref[:, cols], wa_ref[n], ba_ref[:, cols], wx_ref[n], bx_ref[:, cols],
                                  lam_ref[:, cols])
            a_buf[:, cols] = a
            vjps.append(vjp)

        def step(k, c):
            t = rows - 1 - k
            g = g_buf[pl.ds(t, 1), :] + c
            g_buf[pl.ds(t, 1), :] = g
            return a_buf[pl.ds(t, 1), :] * g

        carry[...] = lax.fori_loop(0, rows, step, carry[...], unroll=8)
        h_halo = jnp.where(cur(i) == 0, 0.0, hh_ref[...])
        dh = g_buf[...]
        da = dh * _shift_down(hval, h_halo, 1)
        for n in range(R_BLOCKS):
            cols = slice(n * R_BLOCK_DIM, (n + 1) * R_BLOCK_DIM)
            dx, dwa, dba, dwx, dbx, dlam = vjps[n]((da[:, cols], dh[:, cols]))
            dx_ref[:, cols] = dx
            dwa_ref[n] += dwa
            dwx_ref[n] += dwx
            dba_ref[:, cols] += dba
            dbx_ref[:, cols] += dbx
            dlam_ref[:, cols] += dlam

    blk = pl.BlockSpec((rows, R_WIDTH), lambda i: (cur(i), 0))
    mat = pl.BlockSpec((R_BLOCKS, R_BLOCK_DIM, R_BLOCK_DIM), lambda i: (0, 0, 0))
    vec = pl.BlockSpec((1, R_WIDTH), lambda i: (0, 0))
    return _call(
        body, name=name, grid=(nblk,),
        in_specs=[blk, pl.BlockSpec((rows, R_WIDTH), lambda i: (cur(i), 1)), blk,
                  pl.BlockSpec((HALO, R_WIDTH), lambda i: (jnp.maximum(cur(i) * per - 1, 0), 0)), blk] + _rg_param_specs(),
        out_specs=[blk, blk, mat, vec, mat, vec, vec],
        out_shape=[jax.ShapeDtypeStruct((s_len, R_WIDTH), F32)] * 2 + [
            jax.ShapeDtypeStruct((R_BLOCKS, R_BLOCK_DIM, R_BLOCK_DIM), F32), jax.ShapeDtypeStruct((1, R_WIDTH), F32),
            jax.ShapeDtypeStruct((R_BLOCKS, R_BLOCK_DIM, R_BLOCK_DIM), F32), jax.ShapeDtypeStruct((1, R_WIDTH), F32),
            jax.ShapeDtypeStruct((1, R_WIDTH), F32)],
        scratch_shapes=[pltpu.VMEM((rows, R_WIDTH), F32), pltpu.VMEM((rows, R_WIDTH), F32), pltpu.VMEM((1, R_WIDTH), F32)],
        compiler_params=_params("arbitrary"),
    )(xr, proj_r, h, h, dyr, wa, ba, wx, bx, lam)


GP_CHUNKS = 4
GS_CHUNKS = 8


def _seg_cumsum(x, reverse):
    rows = x.shape[0]
    r = _iota(x.shape, 0) & (G_CHUNK - 1)
    s = 1
    while s < G_CHUNK:
        if reverse:
            x = x + jnp.where(r < G_CHUNK - s, pltpu.roll(x, rows - s, 0), 0.0)
        else:
            x = x + jnp.where(r >= s, pltpu.roll(x, s, 0), 0.0)
        s *= 2
    return x


def _gdn_decay(ga, a_log_row, dt_row):
    return -jnp.exp(a_log_row) * _softplus(ga + dt_row)


def _gdn_chunk(cq, ck, cv, gb, gc, inv=None):
    shape = cq.shape
    head = _iota(shape, 0) & (G_HEADS - 1)
    lane = _iota(shape, 2)
    q, k, v = _silu(cq), _silu(ck), _silu(cv)
    q = q * lax.rsqrt(jnp.sum(q * q, axis=-1, keepdims=True) + RMS_EPS) * (G_HEAD_DIM ** -0.5)
    k = k * lax.rsqrt(jnp.sum(k * k, axis=-1, keepdims=True) + RMS_EPS)
    beta = jnp.sum(jnp.where(lane == head, jax.nn.sigmoid(gb), 0.0), axis=-1, keepdims=True)
    g = jnp.sum(jnp.where(lane == head + G_HEADS, gc, 0.0), axis=-1, keepdims=True)
    sq = (shape[0], G_CHUNK, G_CHUNK)
    row, col = _iota(sq, 1), _iota(sq, 2)
    g_sq = jnp.broadcast_to(g, sq)
    decay = jnp.where(row >= col, jnp.exp(jnp.minimum(g_sq - _t(g_sq), 0.0)), 0.0)
    g_last = jnp.sum(jnp.where(_iota(g.shape, 1) == G_CHUNK - 1, g, 0.0), axis=1, keepdims=True)
    eg = jnp.exp(g)
    kb, vb = k * beta, v * beta
    m = jnp.where(row > col, mm_nt(kb, k) * decay, 0.0)
    known = inv is not None
    if not known:
        inv = _inv_unit_lower(m)
    u, w = _solve2(m, inv, vb, kb * eg)
    qk = jnp.where(row >= col, mm_nt(q, k) * decay, 0.0)
    q_dec = q * eg
    k_dec = k * jnp.exp(g_last - g)
    gl = jnp.broadcast_to(jnp.exp(g_last), (shape[0], 1, G_HEAD_DIM))
    return (u, w, qk, q_dec, k_dec, gl) if known else (u, w, qk, q_dec, k_dec, gl, inv)


def _gdn_step(state, u, w, qk, q_dec, k_dec, gl, gz, norm_w):
    v_new = u - mm_nn(w, state)
    o = mm_nn(q_dec, state) + mm_nn(qk, v_new)
    new_state = state * gl + mm_nn(_t(k_dec), v_new)
    o = o * lax.rsqrt(jnp.mean(o * o, axis=-1, keepdims=True) + RMS_EPS) * norm_w
    return o * _silu(gz), new_state


def _stack_chunks(x, heads):
    chunks = x.shape[0] // G_CHUNK
    parts = []
    for c in range(chunks):
        rows = slice(c * G_CHUNK, (c + 1) * G_CHUNK)
        for hd in range(G_HEADS):
            parts.append(x[rows, hd * LANES:(hd + 1) * LANES] if heads else x[rows, :])
    return jnp.stack(parts)


def _gdn_chunk_shapes(nch):
    b = nch * G_HEADS
    wide = jax.ShapeDtypeStruct((b, G_CHUNK, G_HEAD_DIM), F32)
    return [wide, wide, jax.ShapeDtypeStruct((b, G_CHUNK, G_CHUNK), F32), wide, wide,
            jax.ShapeDtypeStruct((b, 1, G_HEAD_DIM), F32)]


def _gdn_chunk_specs(nbatch):
    wide = pl.BlockSpec((nbatch, G_CHUNK, G_HEAD_DIM), lambda i: (i, 0, 0))
    return [wide, wide, pl.BlockSpec((nbatch, G_CHUNK, G_CHUNK), lambda i: (i, 0, 0)), wide, wide,
            pl.BlockSpec((nbatch, 1, G_HEAD_DIM), lambda i: (i, 0, 0))]


def _gdn_chunk_fwd(conv, proj_g, a_log_row, dt_row, *, name):
    s_len = conv.shape[0]
    cpg = min(GP_CHUNKS, s_len // G_CHUNK)
    rows = cpg * G_CHUNK
    nbatch = cpg * G_HEADS

    def body(c_ref, bg_ref, al_ref, dt_ref, *outs):
        bg = bg_ref[...]
        gc = _seg_cumsum(_gdn_decay(bg, al_ref[...], dt_ref[...]), False)
        res = _gdn_chunk(_stack_chunks(c_ref[:, 0:G_WIDTH], True), _stack_chunks(c_ref[:, G_WIDTH:2 * G_WIDTH], True),
                         _stack_chunks(c_ref[:, 2 * G_WIDTH:], True), _stack_chunks(bg, False), _stack_chunks(gc, False))
        for ref, val in zip(outs, res):
            ref[...] = val

    row = pl.BlockSpec((1, LANES), lambda i: (0, 0))
    return _call(
        body, name=name, grid=(s_len // rows,),
        in_specs=[pl.BlockSpec((rows, 3 * G_WIDTH), lambda i: (i, 0)),
                  pl.BlockSpec((rows, LANES), lambda i: (i, (3 * G_WIDTH + G_WIDTH) // LANES)), row, row],
        out_specs=_gdn_chunk_specs(nbatch) + [pl.BlockSpec((nbatch, G_CHUNK, G_CHUNK), lambda i: (i, 0, 0))],
        out_shape=_gdn_chunk_shapes(s_len // G_CHUNK) + [
            jax.ShapeDtypeStruct((s_len // G_CHUNK * G_HEADS, G_CHUNK, G_CHUNK), F32)],
        compiler_params=_params("parallel"),
    )(conv, proj_g, a_log_row, dt_row)


def _gdn_chunk_bwd(conv, proj_g, a_log_row, dt_row, inv, cots, *, name):
    s_len = conv.shape[0]
    cpg = min(GP_CHUNKS, s_len // G_CHUNK)
    rows = cpg * G_CHUNK
    nbatch = cpg * G_HEADS

    def unstack(x, heads):
        if heads:
            return jnp.concatenate([jnp.concatenate([x[c * G_HEADS + hd] for hd in range(G_HEADS)], axis=1)
                                    for c in range(cpg)], axis=0)
        return jnp.concatenate([sum(x[c * G_HEADS + hd] for hd in range(G_HEADS)) for c in range(cpg)], axis=0)

    def body(c_ref, bg_ref, al_ref, dt_ref, inv_ref, du, dw, dqk, dqd, dkd, dgl, dc_ref, dbg_ref, dal_ref, ddt_ref):
        @pl.when(pl.program_id(0) == 0)
        def _():
            dal_ref[...] = jnp.zeros_like(dal_ref)
            ddt_ref[...] = jnp.zeros_like(ddt_ref)

        bg = bg_ref[...]
        g_all, decay_vjp = jax.vjp(_gdn_decay, bg, al_ref[...], dt_ref[...])
        gc = _seg_cumsum(g_all, False)
        _, vjp = jax.vjp(_gdn_chunk, _stack_chunks(c_ref[:, 0:G_WIDTH], True),
                         _stack_chunks(c_ref[:, G_WIDTH:2 * G_WIDTH], True), _stack_chunks(c_ref[:, 2 * G_WIDTH:], True),
                         _stack_chunks(bg, False), _stack_chunks(gc, False), inv_ref[...])
        dq, dk, dv, dgb, dgc, _ = vjp((du[...], dw[...], dqk[...], dqd[...], dkd[...], dgl[...]))
        dc_ref[:, 0:G_WIDTH] = unstack(dq, True)
        dc_ref[:, G_WIDTH:2 * G_WIDTH] = unstack(dk, True)
        dc_ref[:, 2 * G_WIDTH:] = unstack(dv, True)
        dga, dal, ddt = decay_vjp(_seg_cumsum(unstack(dgc, False), True))
        dbg_ref[:, 0:LANES] = unstack(dgb, False) + dga
        dbg_ref[:, LANES:] = jnp.zeros((rows, LANES), F32)
        dal_ref[...] += dal
        ddt_ref[...] += ddt

    row = pl.BlockSpec((1, LANES), lambda i: (0, 0))
    return _call(
        body, name=name, grid=(s_len // rows,),
        in_specs=[pl.BlockSpec((rows, 3 * G_WIDTH), lambda i: (i, 0)),
                  pl.BlockSpec((rows, LANES), lambda i: (i, (3 * G_WIDTH + G_WIDTH) // LANES)), row, row,
                  pl.BlockSpec((nbatch, G_CHUNK, G_CHUNK), lambda i: (i, 0, 0))]
        + _gdn_chunk_specs(nbatch),
        out_specs=[pl.BlockSpec((rows, 3 * G_WIDTH), lambda i: (i, 0)), pl.BlockSpec((rows, 2 * LANES), lambda i: (i, 0)),
                   row, row],
        out_shape=[jax.ShapeDtypeStruct((s_len, 3 * G_WIDTH), F32), jax.ShapeDtypeStruct((s_len, 2 * LANES), F32),
                   jax.ShapeDtypeStruct((1, LANES), F32), jax.ShapeDtypeStruct((1, LANES), F32)],
        compiler_params=_params("arbitrary"),
    )(conv, proj_g, a_log_row, dt_row, inv, *cots)


def _gdn_scan_specs(cpg, which):
    nbatch = cpg * G_HEADS
    wide = pl.BlockSpec((nbatch, G_CHUNK, G_HEAD_DIM), lambda i: (which(i), 0, 0))
    return [wide, wide, pl.BlockSpec((nbatch, G_CHUNK, G_CHUNK), lambda i: (which(i), 0, 0)), wide, wide,
            pl.BlockSpec((nbatch, 1, G_HEAD_DIM), lambda i: (which(i), 0, 0))]


def _gz_stack(z_ref, c):
    rows = pl.ds(pl.multiple_of(c * G_CHUNK, G_CHUNK), G_CHUNK)
    return jnp.stack([z_ref[rows, hd * LANES:(hd + 1) * LANES] for hd in range(G_HEADS)])


def _gdn_scan_fwd(chunk_vals, proj_g, norm_w, *, name):
    s_len = proj_g.shape[0]
    nch = s_len // G_CHUNK
    cpg = min(GS_CHUNKS, nch)
    rows = cpg * G_CHUNK

    def body(u_ref, w_ref, qk_ref, qd_ref, kd_ref, gl_ref, z_ref, nw_ref, y_ref, st_ref, state):
        @pl.when(pl.program_id(0) == 0)
        def _():
            state[...] = jnp.zeros_like(state)

        def step(c, carry):
            b = pl.ds(pl.multiple_of(c * G_HEADS, G_HEADS), G_HEADS)
            st = state[...]
            st_ref[b] = st
            y, new_state = _gdn_step(st, u_ref[b], w_ref[b], qk_ref[b], qd_ref[b], kd_ref[b], gl_ref[b],
                                     _gz_stack(z_ref, c), nw_ref[...])
            state[...] = new_state
            rws = pl.ds(pl.multiple_of(c * G_CHUNK, G_CHUNK), G_CHUNK)
            for hd in range(G_HEADS):
                y_ref[rws, hd * LANES:(hd + 1) * LANES] = y[hd].astype(y_ref.dtype)
            return carry

        lax.fori_loop(0, cpg, step, 0)

    return _call(
        body, name=name, grid=(nch // cpg,),
        in_specs=_gdn_scan_specs(cpg, lambda i: i) + [
            pl.BlockSpec((rows, G_WIDTH), lambda i: (i, 3)), pl.BlockSpec((1, G_HEAD_DIM), lambda i: (0, 0))],
        out_specs=[pl.BlockSpec((rows, G_WIDTH), lambda i: (i, 0)),
                   pl.BlockSpec((cpg * G_HEADS, G_HEAD_DIM, G_HEAD_DIM), lambda i: (i, 0, 0))],
        out_shape=[jax.ShapeDtypeStruct((s_len, G_WIDTH), MXU_DTYPE),
                   jax.ShapeDtypeStruct((nch * G_HEADS, G_HEAD_DIM, G_HEAD_DIM), F32)],
        scratch_shapes=[pltpu.VMEM((G_HEADS, G_HEAD_DIM, G_HEAD_DIM), F32)],
        compiler_params=_params("arbitrary"),
    )(*chunk_vals, proj_g, norm_w)


def _gdn_scan_bwd(chunk_vals, states, proj_g, norm_w, dyg, *, name):
    s_len = proj_g.shape[0]
    nch = s_len // G_CHUNK
    cpg = min(GS_CHUNKS, nch)
    rows = cpg * G_CHUNK
    ngrid = nch // cpg

    def cur(i):
        return ngrid - 1 - i

    def body(u_ref, w_ref, qk_ref, qd_ref, kd_ref, gl_ref, st_ref, z_ref, nw_ref, dy_ref,
             du_ref, dw_ref, dqk_ref, dqd_ref, dkd_ref, dgl_ref, dz_ref, dnw_ref, dstate):
        @pl.when(pl.program_id(0) == 0)
        def _():
            dstate[...] = jnp.zeros_like(dstate)
            dnw_ref[...] = jnp.zeros_like(dnw_ref)

        def step(k, carry):
            c = cpg - 1 - k
            b = pl.ds(pl.multiple_of(c * G_HEADS, G_HEADS), G_HEADS)
            _, vjp = jax.vjp(_gdn_step, st_ref[b], u_ref[b], w_ref[b], qk_ref[b], qd_ref[b], kd_ref[b], gl_ref[b],
                             _gz_stack(z_ref, c), nw_ref[...])
            dst, du, dw, dqk, dqd, dkd, dgl, dz, dnw = vjp((_gz_stack(dy_ref, c), dstate[...]))
            dstate[...] = dst
            du_ref[b], dw_ref[b], dqk_ref[b], dqd_ref[b], dkd_ref[b], dgl_ref[b] = du, dw, dqk, dqd, dkd, dgl
            rws = pl.ds(pl.multiple_of(c * G_CHUNK, G_CHUNK), G_CHUNK)
            for hd in range(G_HEADS):
                dz_ref[rws, hd * LANES:(hd + 1) * LANES] = dz[hd]
            dnw_ref[...] += dnw
            return carry

        lax.fori_loop(0, cpg, step, 0)

    gate = pl.BlockSpec((rows, G_WIDTH), lambda i: (cur(i), 3))
    wide = pl.BlockSpec((rows, G_WIDTH), lambda i: (cur(i), 0))
    vec = pl.BlockSpec((1, G_HEAD_DIM), lambda i: (0, 0))
    return _call(
        body, name=name, grid=(ngrid,),
        in_specs=_gdn_scan_specs(cpg, cur) + [
            pl.BlockSpec((cpg * G_HEADS, G_HEAD_DIM, G_HEAD_DIM), lambda i: (cur(i), 0, 0)), gate, vec, wide],
        out_specs=_gdn_scan_specs(cpg, cur) + [wide, vec],
        out_shape=_gdn_chunk_shapes(nch) + [jax.ShapeDtypeStruct((s_len, G_WIDTH), F32),
                                            jax.ShapeDtypeStruct((1, G_HEAD_DIM), F32)],
        scratch_shapes=[pltpu.VMEM((G_HEADS, G_HEAD_DIM, G_HEAD_DIM), F32)],
        compiler_params=_params("arbitrary"),
    )(*chunk_vals, states, proj_g, norm_w, dyg)


def _adamw_math(w, g, m, v):
    m = ADAM_B1 * m + (1.0 - ADAM_B1) * g
    v = ADAM_B2 * v + (1.0 - ADAM_B2) * (g * g)
    m_hat = m / (1.0 - ADAM_B1 ** ADAM_STEP)
    v_hat = v / (1.0 - ADAM_B2 ** ADAM_STEP)
    delta = -ADAM_LR * (m_hat / (jnp.sqrt(v_hat) + ADAM_EPS) + ADAM_WD * w)
    return delta, m, v


def _sum_adamw(own, chip, parts, w, m, v, *, name, rows):
    n_layers, n_rows, n_cols = w.shape
    rows = min(rows, n_rows)
    n_parts = parts[0].shape[0]

    def body(c_ref, *refs):
        own_refs, part_refs = refs[:n_layers], refs[n_layers:2 * n_layers]
        w_ref, m_ref, v_ref, g_ref, d_ref, nm_ref, nv_ref = refs[2 * n_layers:]
        layer = pl.program_id(0)
        g = None
        for l in range(n_layers):
            g_l = own_refs[l][0].astype(F32)
            for k in range(n_parts):
                g_l = g_l + part_refs[l][k].astype(F32)
            g = g_l if g is None else jnp.where(layer == l, g_l, g)
        delta, new_m, new_v = _adamw_math(w_ref[0], g, m_ref[0], v_ref[0])
        g_ref[0], d_ref[0], nm_ref[0], nv_ref[0] = g, delta, new_m, new_v

    blk = pl.BlockSpec((1, rows, n_cols), lambda l, i, c: (l, i, 0))
    grid_spec = pltpu.PrefetchScalarGridSpec(
        num_scalar_prefetch=1, grid=(n_layers, n_rows // rows),
        in_specs=[pl.BlockSpec((1, rows, n_cols), lambda l, i, c: (c[0], i, 0))] * n_layers
        + [pl.BlockSpec((n_parts, rows, n_cols), lambda l, i, c: (0, i, 0))] * n_layers + [blk, blk, blk],
        out_specs=[blk] * 4)
    return _call(
        body, name=name, grid_spec=grid_spec, out_shape=[jax.ShapeDtypeStruct(w.shape, F32)] * 4,
        compiler_params=_params("parallel", "parallel"),
    )(_index_operand(chip), *own, *parts, w, m, v)


def _sum_slots(parts, *, name):
    rows = parts.shape[1]

    def body(p_ref, o_ref):
        g = p_ref[0]
        for k in range(1, N_DEV):
            g = g + p_ref[k]
        o_ref[...] = g

    return _call(body, name=name, grid=(1,),
                 in_specs=[pl.BlockSpec(parts.shape, lambda i: (0, 0, 0))],
                 out_specs=pl.BlockSpec((rows, LANES), lambda i: (0, 0)),
                 out_shape=jax.ShapeDtypeStruct((rows, LANES), F32), compiler_params=_params("arbitrary"))(parts)


def _adamw_packed(w, g, m, v, *, name):
    def body(w_ref, g_ref, m_ref, v_ref, d_ref, nm_ref, nv_ref):
        d_ref[...], nm_ref[...], nv_ref[...] = _adamw_math(w_ref[...], g_ref[...], m_ref[...], v_ref[...])

    blk = pl.BlockSpec(w.shape, lambda i: (0, 0))
    return _call(body, name=name, grid=(1,), in_specs=[blk] * 4, out_specs=[blk] * 3,
                 out_shape=[jax.ShapeDtypeStruct(w.shape, F32)] * 3, compiler_params=_params("arbitrary"))(w, g, m, v)


A_COLS = ((0, 512), (768, 1280), (512, 768))
R_COLS = ((1280, 3328),)
G_COLS = ((3328, 5384),)


def _group_weights(wt_full):
    def take(ranges):
        return jnp.concatenate([wt_full[a:b] for a, b in ranges], axis=0)

    wt_g = jnp.concatenate([take(G_COLS), jnp.zeros((G_PAD, wt_full.shape[1]), wt_full.dtype)], axis=0)
    return take(A_COLS), take(R_COLS), wt_g


def _ungroup_grads(d_a, d_r, d_g):
    return jnp.concatenate([d_a[0:512], d_a[1024:1280], d_a[512:1024], d_r, d_g[:WG - G_PAD]], axis=0)


def _shard_rows(w):
    return jnp.pad(jnp.transpose(w, (0, 2, 1)), ((0, 0), (0, N_ROWS_PAD - N_IN_SHARD), (0, 0)))


def _unshard_rows(wt):
    return jnp.transpose(wt[:, :N_IN_SHARD], (0, 2, 1))


def _owner_blocks(dwt):
    blocks = jnp.pad(dwt.reshape(4, 2, N_IN_SHARD, D_MODEL), ((0, 0), (0, 0), (0, N_ROWS_PAD - N_IN_SHARD), (0, 0)))
    return jnp.transpose(blocks, (1, 0, 2, 3))


def _rope_tables(s_len):
    inv = 1.0 / (ROPE_THETA ** (jnp.arange(0, A_HEAD_DIM, 2, dtype=F32) / A_HEAD_DIM))
    ang = jnp.arange(s_len, dtype=F32)[:, None] * inv[None, :]
    cos, sin = jnp.cos(ang), jnp.sin(ang)
    return jnp.tile(cos, (1, 4)), jnp.tile(jnp.concatenate([-sin, sin], axis=1), (1, 2))


def _pack(leaves):
    rows = []
    for leaf in leaves:
        flat = leaf.reshape(-1)
        pad = (-flat.shape[0]) % (8 * LANES)
        rows.append(jnp.pad(flat, (0, pad)).reshape(-1, LANES))
    return jnp.concatenate(rows, axis=0)


def _unpack(packed, shapes):
    out, row = [], 0
    for shape in shapes:
        size = math.prod(shape)
        nrows = -(-size // (8 * LANES)) * 8
        out.append(packed[row:row + nrows].reshape(-1)[:size].reshape(shape))
        row += nrows
    return out


def _lane_row(vals, offset):
    return jnp.pad(vals, (offset, LANES - offset - vals.shape[0])).reshape(1, LANES)


def kernel(x, w_in, sinks, r_conv_w, r_conv_b, r_wa, r_ba, r_wx, r_bx, r_lam, g_conv_w, g_a_log, g_dt_bias, g_norm_w, w_out, ln_g, ln_b, loss_target, m_w_in, m_sinks, m_r_conv_w, m_r_conv_b, m_r_wa, m_r_ba, m_r_wx, m_r_bx, m_r_lam, m_g_conv_w, m_g_a_log, m_g_dt_bias, m_g_norm_w, m_w_out, m_ln_g, m_ln_b, v_w_in, v_sinks, v_r_conv_w, v_r_conv_b, v_r_wa, v_r_ba, v_r_wx, v_r_bx, v_r_lam, v_g_conv_w, v_g_a_log, v_g_dt_bias, v_g_norm_w, v_w_out, v_ln_g, v_ln_b):
    s_len = x.shape[1]
    x0 = x.reshape(s_len, D_MODEL)
    target = loss_target.reshape(s_len, D_MODEL)
    me = 4 * lax.axis_index("x") + 2 * lax.axis_index("y") + lax.axis_index("c")
    core, chip = lax.axis_index("c"), 2 * lax.axis_index("x") + lax.axis_index("y")

    win_all, wout_all, rcw_all, gcw_all = _all_gather(
        [_shard_rows(w_in).astype(MXU_DTYPE).reshape(2 * DEPTH, N_ROWS_PAD // 2, D_MODEL),
         w_out.astype(MXU_DTYPE).reshape(2 * DEPTH, OUT_SHARD // 2, D_MODEL), r_conv_w[None], g_conv_w[None]],
        "gather_weights")
    win_all = win_all.reshape(N_DEV, DEPTH, N_ROWS_PAD, D_MODEL)
    wout_all = wout_all.reshape(N_DEV, DEPTH, OUT_SHARD, D_MODEL)
    rcw_full = jnp.moveaxis(rcw_all[:, 0], 0, 2).reshape(DEPTH, CONV_WIDTH, R_WIDTH)
    gcw_full = jnp.moveaxis(gcw_all[:, 0], 0, 2).reshape(DEPTH, CONV_WIDTH, 3 * G_WIDTH)
    cos, sin = _rope_tables(s_len)

    layers = []
    for l in range(DEPTH):
        wt_a, wt_r, wt_g = _group_weights(win_all[:, l, :N_IN_SHARD].reshape(N_IN, D_MODEL))
        wo = wout_all[:, l].reshape(D_MODEL, D_MODEL)
        layers.append(dict(
            wt_a=wt_a, wt_r=wt_r, wt_g=wt_g, wo=wo,
            wo_a=wo[0:A_WIDTH], wo_r=wo[A_WIDTH:A_WIDTH + R_WIDTH], wo_g=wo[A_WIDTH + R_WIDTH:],
            sinks_t=jnp.broadcast_to(sinks[l][:, None, None], (A_HEADS, 8, LANES)),
            rcw=rcw_full[l], rcb=r_conv_b[l].reshape(1, R_WIDTH), wa=r_wa[l], ba=r_ba[l].reshape(1, R_WIDTH),
            wx=r_wx[l], bx=r_bx[l].reshape(1, R_WIDTH), lam=r_lam[l].reshape(1, R_WIDTH),
            gcw=gcw_full[l], zero_b=jnp.zeros((1, 3 * G_WIDTH), F32),
            a_log=_lane_row(g_a_log[l], G_HEADS), dt=_lane_row(g_dt_bias[l], G_HEADS),
            norm_w=g_norm_w[l].reshape(1, G_HEAD_DIM), ln_g=ln_g[l].reshape(1, D_MODEL), ln_b=ln_b[l].reshape(1, D_MODEL)))

    saved = []
    xin = xin_lo = x0
    for l, p in enumerate(layers):
        proj_a = _matmul([xin_lo], [p["wt_a"]], name=f"proj_a{l}", tm=1024, tn=640, b_t=True)
        proj_r = _matmul([xin_lo], [p["wt_r"]], name=f"proj_r{l}", tm=1024, tn=512, b_t=True)
        proj_g = _matmul([xin_lo], [p["wt_g"]], name=f"proj_g{l}", tm=1024, tn=768, b_t=True)
        ya = _attn_fwd(proj_a, cos, sin, p["sinks_t"], name=f"attn_fwd{l}")
        xr = _conv_fwd(proj_r, R_WIDTH, p["rcw"], p["rcb"], name=f"rconv_fwd{l}")
        h, yr = _rg_fwd(xr, proj_r, p["wa"], p["ba"], p["wx"], p["bx"], p["lam"], name=f"rglru_fwd{l}")
        conv = _conv_fwd(proj_g, 3 * G_WIDTH, p["gcw"], p["zero_b"], name=f"gconv_fwd{l}")
        *chunk_vals, inv = _gdn_chunk_fwd(conv, proj_g, p["a_log"], p["dt"], name=f"gdn_chunk_fwd{l}")
        yg, states = _gdn_scan_fwd(chunk_vals, proj_g, p["norm_w"], name=f"gdn_scan_fwd{l}")
        z, xout, xout_lo = _outproj_ln(ya, yr, yg, p["wo"], xin, p["ln_g"], p["ln_b"], name=f"outproj_ln{l}")
        saved.append(dict(xin_lo=xin_lo, proj_a=proj_a, proj_r=proj_r, proj_g=proj_g, ya=ya, yr=yr, yg=yg, xr=xr, h=h,
                          conv=conv, chunk_vals=chunk_vals, inv=inv, states=states, z=z))
        xin, xin_lo = xout, xout_lo

    grads = [None] * DEPTH
    dxn = None
    loss_local = None
    for l in reversed(range(DEPTH)):
        p, sv = layers[l], saved[l]
        if dxn is None:
            dz, dz_lo, dln_g, dln_b, loss_local = _ln_bwd(sv["z"], p["ln_g"], name=f"ln_bwd{l}", xn=xin, target=target)
        else:
            dz, dz_lo, dln_g, dln_b = _ln_bwd(sv["z"], p["ln_g"], name=f"ln_bwd{l}", dxn=dxn)
        dya = _matmul([dz_lo], [p["wo_a"]], name=f"dya{l}", tm=1024, tn=512, b_t=True)
        dyr = _matmul([dz_lo], [p["wo_r"]], name=f"dyr{l}", tm=1024, tn=512, b_t=True)
        dyg = _matmul([dz_lo], [p["wo_g"]], name=f"dyg{l}", tm=1024, tn=512, b_t=True)
        dwo = jnp.concatenate([
            _matmul_tn(sv["ya"], dz_lo, name=f"dwo_a{l}", tm=512, tn=1024, tk=1024),
            _matmul_tn(sv["yr"], dz_lo, name=f"dwo_r{l}", tm=1024, tn=1024, tk=1024),
            _matmul_tn(sv["yg"], dz_lo, name=f"dwo_g{l}", tm=512, tn=1024, tk=1024)], axis=0)

        dproj_a, dsinks_t = _attn_bwd(sv["proj_a"], cos, sin, p["sinks_t"], dya, name=f"attn_bwd{l}")

        dxr, drz, dwa, dba, dwx, dbx, dlam = _rg_bwd(sv["xr"], sv["proj_r"], sv["h"], dyr, p["wa"], p["ba"], p["wx"],
                                                     p["bx"], p["lam"], name=f"rglru_bwd{l}")
        dproj_r, drcw, drcb = _conv_bwd(dxr, sv["proj_r"], R_WIDTH, p["rcw"], [drz], name=f"rconv_bwd{l}")

        scan_out = _gdn_scan_bwd(sv["chunk_vals"], sv["states"], sv["proj_g"], p["norm_w"], dyg, name=f"gdn_scan_bwd{l}")
        dgz, dnorm_w = scan_out[6], scan_out[7]
        dconv, dbg, dal, ddt = _gdn_chunk_bwd(sv["conv"], sv["proj_g"], p["a_log"], p["dt"], sv["inv"], scan_out[:6],
                                              name=f"gdn_chunk_bwd{l}")
        dproj_g, dgcw, _ = _conv_bwd(dconv, sv["proj_g"], 3 * G_WIDTH, p["gcw"], [dgz, dbg], name=f"gconv_bwd{l}")

        dwin = _ungroup_grads(_matmul_tn(dproj_a, sv["xin_lo"], name=f"dwin_a{l}", tm=640, tn=1024, tk=1024),
                              _matmul_tn(dproj_r, sv["xin_lo"], name=f"dwin_r{l}", tm=1024, tn=1024, tk=1024),
                              _matmul_tn(dproj_g, sv["xin_lo"], name=f"dwin_g{l}", tm=1152, tn=1024, tk=1024))

        dwin_blocks = _owner_blocks(dwin)[:, :, None].astype(MXU_DTYPE)
        dwout_blocks = jnp.transpose(dwo.reshape(4, 2, OUT_SHARD, D_MODEL), (1, 0, 2, 3))[:, :, None].astype(MXU_DTYPE)
        got_win, got_wout = _swap_cores(
            [dwin_blocks.reshape(2, 8, N_ROWS_PAD // 2, D_MODEL), dwout_blocks.reshape(2, 4, OUT_SHARD, D_MODEL)],
            f"swap_core_grads{l}")
        chip_win = _add_pair(dwin_blocks, got_win.reshape(dwin_blocks.shape[1:]), core, name=f"add_core_grads_w_in{l}",
                             rows=352).reshape(4, N_ROWS_PAD, D_MODEL)
        chip_wout = _add_pair(dwout_blocks, got_wout.reshape(dwout_blocks.shape[1:]), core, name=f"add_core_grads_w_out{l}",
                              rows=256).reshape(4, OUT_SHARD, D_MODEL)
        dxn, (win_parts, wout_parts) = _matmul(
            [dproj_a, dproj_r, dproj_g], [p["wt_a"], p["wt_r"], p["wt_g"]], name=f"dx{l}", tm=512, tn=512, add=dz,
            add_scale=DEEPNORM_ALPHA, comm=_ChipExchange([chip_win, chip_wout]))
        grads[l] = dict(
            chip_win=chip_win, chip_wout=chip_wout, win_parts=win_parts, wout_parts=wout_parts,
            sinks=dsinks_t[:, :, 0].sum(axis=1), r_conv_w=drcw.reshape(CONV_WIDTH, R_WIDTH),
            r_conv_b=drcb.reshape(R_WIDTH), r_wa=dwa, r_ba=dba.reshape(R_WIDTH), r_wx=dwx, r_bx=dbx.reshape(R_WIDTH),
            r_lam=dlam.reshape(R_WIDTH), g_conv_w=dgcw.reshape(CONV_WIDTH, 3 * G_WIDTH),
            g_a_log=dal[0, G_HEADS:2 * G_HEADS], g_dt_bias=ddt[0, G_HEADS:2 * G_HEADS],
            g_norm_w=dnorm_w.reshape(G_HEAD_DIM), ln_g=dln_g.reshape(D_MODEL), ln_b=dln_b.reshape(D_MODEL))
    grad_x = dxn.reshape(x.shape)
    loss = lax.psum(loss_local[0, 0], ("x", "y", "c"))

    def stacked(name):
        return jnp.stack([grads[l][name] for l in range(DEPTH)])

    def per_layer(name):
        return [grads[l][name] for l in range(DEPTH)]

    w_in_t = [_unshard_rows(t) for t in _sum_adamw(per_layer("chip_win"), chip, per_layer("win_parts"), _shard_rows(w_in),
                                                   _shard_rows(m_w_in), _shard_rows(v_w_in), name="adamw_w_in", rows=176)]
    g_w_in, d_w_in, nm_w_in, nv_w_in = w_in_t
    g_w_out, d_w_out, nm_w_out, nv_w_out = _sum_adamw(per_layer("chip_wout"), chip, per_layer("wout_parts"), w_out,
                                                      m_w_out, v_w_out, name="adamw_w_out", rows=128)

    small = ["sinks", "r_conv_w", "r_conv_b", "r_wa", "r_ba", "r_wx", "r_bx", "r_lam", "g_conv_w", "g_a_log",
             "g_dt_bias", "g_norm_w", "ln_g", "ln_b"]
    full_shapes = [stacked(nm).shape for nm in small]
    packed_small = _pack([stacked(nm) for nm in small])
    (all_small,) = _all_gather([packed_small.reshape(4, packed_small.shape[0] // 4, LANES)], "gather_small_grads")
    all_small = all_small.reshape(N_DEV, packed_small.shape[0], LANES)
    g_small = dict(zip(small, _unpack(_sum_slots(all_small, name="sum_small_grads"), full_shapes)))
    g_small["r_conv_w"] = lax.dynamic_slice_in_dim(g_small["r_conv_w"], me * (R_WIDTH // N_DEV), R_WIDTH // N_DEV, axis=2)
    g_small["g_conv_w"] = lax.dynamic_slice_in_dim(g_small["g_conv_w"], me * (3 * G_WIDTH // N_DEV), 3 * G_WIDTH // N_DEV, axis=2)
    given = dict(sinks=(sinks, m_sinks, v_sinks), r_conv_w=(r_conv_w, m_r_conv_w, v_r_conv_w),
                 r_conv_b=(r_conv_b, m_r_conv_b, v_r_conv_b), r_wa=(r_wa, m_r_wa, v_r_wa), r_ba=(r_ba, m_r_ba, v_r_ba),
                 r_wx=(r_wx, m_r_wx, v_r_wx), r_bx=(r_bx, m_r_bx, v_r_bx), r_lam=(r_lam, m_r_lam, v_r_lam),
                 g_conv_w=(g_conv_w, m_g_conv_w, v_g_conv_w), g_a_log=(g_a_log, m_g_a_log, v_g_a_log),
                 g_dt_bias=(g_dt_bias, m_g_dt_bias, v_g_dt_bias), g_norm_w=(g_norm_w, m_g_norm_w, v_g_norm_w),
                 ln_g=(ln_g, m_ln_g, v_ln_g), ln_b=(ln_b, m_ln_b, v_ln_b))
    shard_shapes = [given[nm][0].shape for nm in small]
    packed = [_pack([given[nm][k] for nm in small]) for k in range(3)]
    d_p, nm_p, nv_p = _adamw_packed(packed[0], _pack([g_small[nm] for nm in small]), packed[1], packed[2], name="adamw_small")
    d_small = dict(zip(small, _unpack(d_p, shard_shapes)))
    nm_small = dict(zip(small, _unpack(nm_p, shard_shapes)))
    nv_small = dict(zip(small, _unpack(nv_p, shard_shapes)))

    order = ["w_in"] + small[:12] + ["w_out"] + small[12:]

    def leaf(big_in, big_out, table):
        return [big_in if nm == "w_in" else big_out if nm == "w_out" else table[nm] for nm in order]

    return (loss, grad_x, *leaf(g_w_in, g_w_out, g_small), *leaf(d_w_in, d_w_out, d_small),
            *leaf(nm_w_in, nm_w_out, nm_small), *leaf(nv_w_in, nv_w_out, nv_small))
```

```python
import functools
import math

import jax
import jax.numpy as jnp
from jax import lax
from jax.experimental import pallas as pl
from jax.experimental.pallas import tpu as pltpu

F32 = jnp.float32
MXU_DTYPE = jnp.bfloat16
HIGHEST = lax.Precision.HIGHEST
MESH_ID = pl.DeviceIdType.MESH

N_DEV = 8
DEPTH = 2
D_MODEL = 2048
A_HEADS, A_KV_HEADS, A_HEAD_DIM = 8, 2, 64
A_WIDTH, A_KV_WIDTH = 512, 128
A_BLOCK = 128
ROPE_THETA = 10000.0
R_WIDTH, R_BLOCKS, R_BLOCK_DIM = 1024, 8, 128
R_C = 8.0
CONV_WIDTH = 4
G_HEADS, G_HEAD_DIM, G_WIDTH, G_CHUNK = 4, 128, 512, 64
N_IN = 5384
N_IN_SHARD = N_IN // N_DEV
N_ROWS_PAD = 704
OUT_SHARD = D_MODEL // N_DEV
WA, WR, WG = 1280, 2048, 2304
G_PAD = WG - (3 * G_WIDTH + G_WIDTH + 2 * G_HEADS)
DEEPNORM_ALPHA = (2 * DEPTH) ** 0.25
LN_EPS = 1e-5
RMS_EPS = 1e-6
ADAM_LR, ADAM_B1, ADAM_B2, ADAM_EPS, ADAM_WD, ADAM_STEP = 0.001, 0.9, 0.999, 1e-08, 0.01, 10
NEG = -1e30
VMEM_LIMIT = 56 * 1024 * 1024
LANES = 128


def _call(body, **kw):
    return pl.pallas_call(body, **kw)


def _params(*sem):
    return pltpu.CompilerParams(dimension_semantics=sem, vmem_limit_bytes=VMEM_LIMIT)


def _t(x):
    return jnp.swapaxes(x, -1, -2)


def _raw_dot(a, b, ca, cb, precision=None):
    batch = tuple(range(a.ndim - 2))
    if precision is None:
        a, b = a.astype(MXU_DTYPE), b.astype(MXU_DTYPE)
    return lax.dot_general(a, b, (((ca,), (cb,)), (batch, batch)), precision=precision,
                           preferred_element_type=F32)


def _nn(a, b, precision=None):
    return _raw_dot(a, b, a.ndim - 1, b.ndim - 2, precision)


def _nt(a, b, precision=None):
    return _raw_dot(a, b, a.ndim - 1, b.ndim - 1, precision)


@jax.custom_vjp
def mm_nn(a, b):
    return _nn(a, b)


def _mm_nn_fwd(a, b):
    return _nn(a, b), (a, b)


def _mm_nn_bwd(res, g):
    a, b = res
    return _nt(g, b), _nn(_t(a), g)


mm_nn.defvjp(_mm_nn_fwd, _mm_nn_bwd)


@jax.custom_vjp
def mm_nt(a, b):
    return _nt(a, b)


def _mm_nt_fwd(a, b):
    return _nt(a, b), (a, b)


def _mm_nt_bwd(res, g):
    a, b = res
    return _nn(g, b), _nn(_t(g), a)


mm_nt.defvjp(_mm_nt_fwd, _mm_nt_bwd)


def _split(x):
    hi = x.astype(MXU_DTYPE)
    return hi, (x - hi.astype(F32)).astype(MXU_DTYPE)


def _hmm(a, b, nt=False):
    dot = _nt if nt else _nn
    return dot(a[0], b[0]) + (dot(a[0], b[1]) + dot(a[1], b[0]))


def _silu(x):
    return x * jax.nn.sigmoid(x)


def _softplus(x):
    return jnp.maximum(x, 0.0) + jnp.log1p(jnp.exp(-jnp.abs(x)))


def _neg_expm1(x):
    series = -x * (1.0 + x * 0.5 * (1.0 + x * (1.0 / 3.0) * (1.0 + x * 0.25 * (1.0 + x * 0.2))))
    return jnp.where(x > -0.125, series, 1.0 - jnp.exp(x))


def _iota(shape, dim):
    return lax.broadcasted_iota(jnp.int32, shape, dim)


def _inv_unit_lower(m):
    shape = m.shape
    row, col = _iota(shape, 1), _iota(shape, 2)
    eye = (row == col).astype(F32)

    def blockdiag(size):
        return (row // size) == (col // size)

    x = -jnp.where(blockdiag(8), m, 0.0)
    xs = _split(x)
    x2s = _split(_hmm(xs, xs))
    x4s = _split(_hmm(x2s, x2s))
    inv = eye + x
    inv = inv + _hmm(_split(inv), x2s)
    inv = inv + _hmm(_split(inv), x4s)
    for size in (8, 16, 32):
        below = jnp.where(blockdiag(2 * size) & jnp.logical_not(blockdiag(size)), m, 0.0)
        invs = _split(inv)
        inv = inv - _hmm(_split(_hmm(invs, _split(below))), invs)
    return inv


@jax.custom_vjp
def _solve2(m, inv, r1, r2):
    invs = _split(inv)
    return _hmm(invs, _split(r1)), _hmm(invs, _split(r2))


def _solve2_fwd(m, inv, r1, r2):
    x1, x2 = _solve2(m, inv, r1, r2)
    return (x1, x2), (inv, x1, x2)


def _solve2_bwd(res, g):
    inv, x1, x2 = res
    inv_ts = _split(_t(inv))
    d1, d2 = _hmm(inv_ts, _split(g[0])), _hmm(inv_ts, _split(g[1]))
    dm = -(_hmm(_split(d1), _split(x1), nt=True) + _hmm(_split(d2), _split(x2), nt=True))
    return dm, jnp.zeros_like(inv), d1, d2


_solve2.defvjp(_solve2_fwd, _solve2_bwd)


def _swap_halves(x):
    n = x.shape[-1]
    lane = _iota(x.shape, x.ndim - 1)
    return jnp.where((lane & 63) < 32, pltpu.roll(x, n - 32, x.ndim - 1), pltpu.roll(x, 32, x.ndim - 1))


def _rope(x, cos, sin):
    reps = x.shape[-1] // LANES
    if reps > 1:
        cos, sin = jnp.tile(cos, (1, reps)), jnp.tile(sin, (1, reps))
    return x * cos + _swap_halves(x) * sin


def _rope_t(d, cos, sin):
    reps = d.shape[-1] // LANES
    if reps > 1:
        cos, sin = jnp.tile(cos, (1, reps)), jnp.tile(sin, (1, reps))
    return d * cos + _swap_halves(d * sin)


def _swap64(x):
    return pltpu.roll(x, 64, x.ndim - 1)


def _mesh_pos():
    return lax.axis_index("x"), lax.axis_index("y"), lax.axis_index("c")


def _all_gather(arrays, name):
    n = len(arrays)
    npieces = [a.shape[0] for a in arrays]
    pmax = max(npieces)

    def body(*refs):
        ins, outs = refs[:n], refs[n:2 * n]
        send_sems, recv_sems, local_sem = refs[2 * n:]
        x, y, c = _mesh_pos()
        me, sibling = (x, y, c), (x, y, 1 - c)
        chips = [(1 - x, y), (x, 1 - y), (1 - x, 1 - y)]

        def slot(a, pos, p):
            return outs[a].at[4 * pos[0] + 2 * pos[1] + pos[2], p]

        def copy(a, p, k, block, to, own=False):
            return pltpu.make_async_remote_copy(
                src_ref=ins[a].at[p] if own else slot(a, block, p), dst_ref=slot(a, block, p),
                send_sem=send_sems.at[a, p, k], recv_sem=recv_sems.at[a, p, k], device_id=to, device_id_type=MESH_ID)

        pieces = [(a, p) for p in range(pmax) for a in range(n) if p < npieces[a]]
        mine = [pltpu.make_async_copy(ins[a].at[p], slot(a, me, p), local_sem.at[a, p]) for a, p in pieces]
        for cp in mine:
            cp.start()
        first = []
        for a, p in pieces:
            first += [copy(a, p, 1 + j, me, (*chip, c), own=True) for j, chip in enumerate(chips)]
            first.append(copy(a, p, 0, me, sibling, own=True))
        for cp in first:
            cp.start()
        passed = []
        for a, p in pieces:
            for j, chip in enumerate(chips):
                copy(a, p, 1 + j, (*chip, c), me).wait_recv()
                cp = copy(a, p, 4 + j, (*chip, c), sibling)
                cp.start()
                passed.append(cp)
        for a, p in pieces:
            copy(a, p, 0, sibling, me).wait_recv()
            for j, chip in enumerate(chips):
                copy(a, p, 4 + j, (*chip, 1 - c), me).wait_recv()
        for cp in first + passed:
            cp.wait_send()
        for cp in mine:
            cp.wait()

    any_spec = pl.BlockSpec(memory_space=pl.ANY)
    return _call(
        body, name=name,
        out_shape=[jax.ShapeDtypeStruct((N_DEV,) + a.shape, a.dtype) for a in arrays],
        in_specs=[any_spec] * n, out_specs=[any_spec] * n,
        scratch_shapes=[pltpu.SemaphoreType.DMA((n, pmax, 7)), pltpu.SemaphoreType.DMA((n, pmax, 7)),
                        pltpu.SemaphoreType.DMA((n, pmax))],
    )(*arrays)


def _swap_cores(arrays, name):
    n = len(arrays)
    pmax = max(a.shape[1] for a in arrays)

    def body(*refs):
        ins, got = refs[:n], refs[n:2 * n]
        send_sems, recv_sems = refs[2 * n:]
        x, y, c = _mesh_pos()
        copies = [pltpu.make_async_remote_copy(
            src_ref=ins[a].at[1 - c, p], dst_ref=got[a].at[p], send_sem=send_sems.at[a, p], recv_sem=recv_sems.at[a, p],
            device_id=(x, y, 1 - c), device_id_type=MESH_ID) for a in range(n) for p in range(arrays[a].shape[1])]
        for cp in copies:
            cp.start()
        for cp in copies:
            cp.wait()

    any_spec = pl.BlockSpec(memory_space=pl.ANY)
    return _call(
        body, name=name, out_shape=[jax.ShapeDtypeStruct(a.shape[1:], a.dtype) for a in arrays],
        in_specs=[any_spec] * n, out_specs=[any_spec] * n,
        scratch_shapes=[pltpu.SemaphoreType.DMA((n, pmax)), pltpu.SemaphoreType.DMA((n, pmax))],
    )(*arrays)


class _ChipExchange:
    def __init__(self, arrays):
        self.arrays = list(arrays)
        n = len(self.arrays)
        self.out_shape = [jax.ShapeDtypeStruct((3,) + a.shape[1:], a.dtype) for a in self.arrays]
        self.scratch = [pltpu.SemaphoreType.DMA((n, 3)), pltpu.SemaphoreType.DMA((n, 3))]

    def _copies(self, ins, outs, send_sems, recv_sems):
        x, y, c = _mesh_pos()
        copies = []
        for a in range(len(self.arrays)):
            for k in range(1, 4):
                px, py = x ^ (k >> 1), y ^ (k & 1)
                copies.append(pltpu.make_async_remote_copy(
                    src_ref=ins[a].at[2 * px + py], dst_ref=outs[a].at[k - 1], send_sem=send_sems.at[a, k - 1],
                    recv_sem=recv_sems.at[a, k - 1], device_id=(px, py, c), device_id_type=MESH_ID))
        return copies

    def start(self, ins, outs, send_sems, recv_sems):
        for cp in self._copies(ins, outs, send_sems, recv_sems):
            cp.start()

    def finish(self, ins, outs, send_sems, recv_sems):
        copies = self._copies(ins, outs, send_sems, recv_sems)
        for cp in copies:
            cp.wait_recv()
        for cp in copies:
            cp.wait_send()


def _index_operand(i):
    return jnp.reshape(i, (1,)).astype(jnp.int32)


def _add_pair(pair, other, core, *, name, rows):
    _, n_slots, n_layers, n_rows, n_cols = pair.shape
    rows = min(rows, n_rows)

    def body(c_ref, a_ref, b_ref, o_ref):
        o_ref[...] = (a_ref[0].astype(F32) + b_ref[...].astype(F32)).astype(o_ref.dtype)

    blk = pl.BlockSpec((1, 1, rows, n_cols), lambda s, l, i, c: (s, l, i, 0))
    grid_spec = pltpu.PrefetchScalarGridSpec(
        num_scalar_prefetch=1, grid=(n_slots, n_layers, n_rows // rows),
        in_specs=[pl.BlockSpec((1, 1, 1, rows, n_cols), lambda s, l, i, c: (c[0], s, l, i, 0)), blk], out_specs=blk)
    return _call(body, name=name, grid_spec=grid_spec, out_shape=jax.ShapeDtypeStruct(other.shape, pair.dtype),
                 compiler_params=_params("parallel", "parallel", "parallel"))(_index_operand(core), pair, other)


def _matmul(a_list, b_list, *, name, tm, tn, b_t=False, out_dtype=F32, add=None, add_scale=1.0, comm=None):
    n = len(a_list)
    m_rows, n_cols = a_list[0].shape[0], b_list[0].shape[0 if b_t else 1]
    tm, tn = min(tm, m_rows), min(tn, n_cols)
    grid = (m_rows // tm, n_cols // tn)
    n_in = 2 * n + (add is not None)
    n_comm = len(comm.arrays) if comm is not None else 0

    def body(*refs):
        a_refs, b_refs = refs[:n], refs[n:2 * n]
        o_ref = refs[n_in + n_comm]
        if comm is not None:
            comm_refs = (refs[n_in:n_in + n_comm], refs[n_in + n_comm + 1:n_in + 2 * n_comm + 1], *refs[-2:])
            i, j = pl.program_id(0), pl.program_id(1)

            @pl.when((i == 0) & (j == 0))
            def _():
                comm.start(*comm_refs)

        acc = None
        for a_ref, b_ref in zip(a_refs, b_refs):
            part = lax.dot_general(a_ref[...].astype(MXU_DTYPE), b_ref[...].astype(MXU_DTYPE),
                                   (((1,), (1 if b_t else 0,)), ((), ())), preferred_element_type=F32)
            acc = part if acc is None else acc + part
        if add is not None:
            acc = acc + add_scale * refs[2 * n][...]
        o_ref[...] = acc.astype(o_ref.dtype)
        if comm is not None:
            @pl.when((i == grid[0] - 1) & (j == grid[1] - 1))
            def _():
                comm.finish(*comm_refs)

    in_specs = [pl.BlockSpec((tm, a.shape[1]), lambda i, j: (i, 0)) for a in a_list]
    if b_t:
        in_specs += [pl.BlockSpec((tn, b.shape[1]), lambda i, j: (j, 0)) for b in b_list]
    else:
        in_specs += [pl.BlockSpec((b.shape[0], tn), lambda i, j: (0, j)) for b in b_list]
    args = list(a_list) + list(b_list)
    if add is not None:
        in_specs.append(pl.BlockSpec((tm, tn), lambda i, j: (i, j)))
        args.append(add)
    out_specs = pl.BlockSpec((tm, tn), lambda i, j: (i, j))
    out_shape = jax.ShapeDtypeStruct((m_rows, n_cols), out_dtype)
    if comm is None:
        return _call(body, name=name, grid=grid, in_specs=in_specs, out_specs=out_specs, out_shape=out_shape,
                     compiler_params=_params("parallel", "arbitrary"))(*args)
    any_spec = pl.BlockSpec(memory_space=pl.ANY)
    outs = _call(body, name=name, grid=grid, in_specs=in_specs + [any_spec] * n_comm,
                 out_specs=[out_specs] + [any_spec] * n_comm, out_shape=[out_shape] + comm.out_shape,
                 scratch_shapes=comm.scratch, compiler_params=_params("arbitrary", "arbitrary"))(*args, *comm.arrays)
    return outs[0], outs[1:]


def _matmul_tn(a, b, *, name, tm, tn, tk):
    k_rows, m_rows = a.shape
    n_cols = b.shape[1]
    tm, tn, tk = min(tm, m_rows), min(tn, n_cols), min(tk, k_rows)
    nk = k_rows // tk

    def body(a_ref, b_ref, o_ref):
        part = lax.dot_general(a_ref[...].astype(MXU_DTYPE), b_ref[...].astype(MXU_DTYPE),
                               (((0,), (0,)), ((), ())), preferred_element_type=F32)

        @pl.when(pl.program_id(2) == 0)
        def _():
            o_ref[...] = part

        @pl.when(pl.program_id(2) > 0)
        def _():
            o_ref[...] += part

    return _call(
        body, name=name, grid=(m_rows // tm, n_cols // tn, nk),
        in_specs=[pl.BlockSpec((tk, tm), lambda i, j, k: (k, i)), pl.BlockSpec((tk, tn), lambda i, j, k: (k, j))],
        out_specs=pl.BlockSpec((tm, tn), lambda i, j, k: (i, j)),
        out_shape=jax.ShapeDtypeStruct((m_rows, n_cols), F32),
        compiler_params=_params("parallel", "parallel", "arbitrary"),
    )(a, b)


def _outproj_ln(ya, yr, yg, w_out, x, ln_g, ln_b, *, name):
    s_len = x.shape[0]
    tm = min(256, s_len)

    def body(ya_ref, yr_ref, yg_ref, w_ref, x_ref, g_ref, b_ref, z_ref, o_ref, lo_ref):
        acc = jnp.dot(ya_ref[...], w_ref[0:A_WIDTH, :], preferred_element_type=F32)
        acc += jnp.dot(yr_ref[...], w_ref[A_WIDTH:A_WIDTH + R_WIDTH, :], preferred_element_type=F32)
        acc += jnp.dot(yg_ref[...], w_ref[A_WIDTH + R_WIDTH:, :], preferred_element_type=F32)
        z = DEEPNORM_ALPHA * x_ref[...] + acc
        z_ref[...] = z
        mu = jnp.mean(z, axis=-1, keepdims=True)
        zc = z - mu
        var = jnp.mean(zc * zc, axis=-1, keepdims=True)
        out = zc * lax.rsqrt(var + LN_EPS) * g_ref[...] + b_ref[...]
        o_ref[...] = out
        lo_ref[...] = out.astype(lo_ref.dtype)

    def rows(width):
        return pl.BlockSpec((tm, width), lambda i: (i, 0))

    def whole(shape):
        return pl.BlockSpec(shape, lambda i: (0, 0))

    return _call(
        body, name=name, grid=(s_len // tm,),
        in_specs=[rows(A_WIDTH), rows(R_WIDTH), rows(G_WIDTH), whole((D_MODEL, D_MODEL)), rows(D_MODEL),
                  whole((1, D_MODEL)), whole((1, D_MODEL))],
        out_specs=[rows(D_MODEL)] * 3,
        out_shape=[jax.ShapeDtypeStruct((s_len, D_MODEL), F32)] * 2 + [jax.ShapeDtypeStruct((s_len, D_MODEL), MXU_DTYPE)],
        compiler_params=_params("parallel"),
    )(ya, yr, yg, w_out, x, ln_g, ln_b)


def _ln_bwd(z, ln_g, *, name, dxn=None, xn=None, target=None):
    s_len = z.shape[0]
    tm = min(256, s_len)
    top = dxn is None

    def body(*refs):
        if top:
            z_ref, g_ref, xn_ref, t_ref, dz_ref, lo_ref, dg_ref, db_ref, loss_ref = refs
            err = xn_ref[...] - t_ref[...]
            dy = err * (1.0 / D_MODEL)
        else:
            z_ref, g_ref, dy_ref, dz_ref, lo_ref, dg_ref, db_ref = refs
            dy = dy_ref[...]
        first = pl.program_id(0) == 0

        @pl.when(first)
        def _():
            dg_ref[...] = jnp.zeros_like(dg_ref)
            db_ref[...] = jnp.zeros_like(db_ref)
            if top:
                loss_ref[...] = jnp.zeros_like(loss_ref)

        z = z_ref[...]
        mu = jnp.mean(z, axis=-1, keepdims=True)
        zc = z - mu
        rstd = lax.rsqrt(jnp.mean(zc * zc, axis=-1, keepdims=True) + LN_EPS)
        xhat = zc * rstd
        dxh = dy * g_ref[...]
        dz = rstd * (dxh - jnp.mean(dxh, axis=-1, keepdims=True) - xhat * jnp.mean(dxh * xhat, axis=-1, keepdims=True))
        dz_ref[...] = dz
        lo_ref[...] = dz.astype(lo_ref.dtype)
        dg_ref[...] += jnp.sum(dy * xhat, axis=0, keepdims=True)
        db_ref[...] += jnp.sum(dy, axis=0, keepdims=True)
        if top:
            per_row = jnp.sum(err * err, axis=-1, keepdims=True) * (0.5 / D_MODEL)
            loss_ref[...] += jnp.sum(per_row, axis=0, keepdims=True)

    rows = pl.BlockSpec((tm, D_MODEL), lambda i: (i, 0))
    vec = pl.BlockSpec((1, D_MODEL), lambda i: (0, 0))
    in_specs = [rows, vec] + ([rows, rows] if top else [rows])
    args = [z, ln_g] + ([xn, target] if top else [dxn])
    out_specs = [rows, rows, vec, vec]
    out_shape = [jax.ShapeDtypeStruct((s_len, D_MODEL), F32), jax.ShapeDtypeStruct((s_len, D_MODEL), MXU_DTYPE),
                 jax.ShapeDtypeStruct((1, D_MODEL), F32), jax.ShapeDtypeStruct((1, D_MODEL), F32)]
    if top:
        out_specs.append(pl.BlockSpec((1, 1), lambda i: (0, 0)))
        out_shape.append(jax.ShapeDtypeStruct((1, 1), F32))
    return _call(body, name=name, grid=(s_len // tm,), in_specs=in_specs, out_specs=out_specs,
                 out_shape=out_shape, compiler_params=_params("arbitrary"))(*args)


CONV_ROWS = 256
HALO = 8


def _shift_down(x, halo, s):
    if s == 0:
        return x
    ext = jnp.concatenate([halo, x], axis=0)
    return pltpu.roll(ext, s, 0)[HALO:, :]


def _shift_up(x, halo, s):
    if s == 0:
        return x
    ext = jnp.concatenate([x, halo], axis=0)
    return pltpu.roll(ext, ext.shape[0] - s, 0)[:x.shape[0], :]


def _conv_fwd(src, width, w, bias, *, name):
    s_len = src.shape[0]
    rows = min(CONV_ROWS, s_len)
    per = rows // HALO

    def body(x_ref, halo_ref, w_ref, b_ref, o_ref):
        x = x_ref[...]
        halo = jnp.where(pl.program_id(0) == 0, 0.0, halo_ref[...])
        acc = x * w_ref[3:4, :] + b_ref[...]
        for k in range(CONV_WIDTH - 1):
            acc += _shift_down(x, halo, 3 - k) * w_ref[k:k + 1, :]
        o_ref[...] = acc

    return _call(
        body, name=name, grid=(s_len // rows,),
        in_specs=[pl.BlockSpec((rows, width), lambda i: (i, 0)),
                  pl.BlockSpec((HALO, width), lambda i: (jnp.maximum(i * per - 1, 0), 0)),
                  pl.BlockSpec((CONV_WIDTH, width), lambda i: (0, 0)), pl.BlockSpec((1, width), lambda i: (0, 0))],
        out_specs=pl.BlockSpec((rows, width), lambda i: (i, 0)),
        out_shape=jax.ShapeDtypeStruct((s_len, width), F32),
        compiler_params=_params("parallel"),
    )(src, src, w, bias)


def _conv_bwd(dy, src, width, w, passthrough, *, name):
    s_len = src.shape[0]
    rows = min(CONV_ROWS, s_len)
    per = rows // HALO
    nblk = s_len // rows
    extra = [p.shape[1] for p in passthrough]
    total = width + sum(extra)

    def body(*refs):
        dy_ref, dyh_ref, x_ref, xh_ref, w_ref = refs[:5]
        p_refs = refs[5:5 + len(extra)]
        o_ref, dw_ref, db_ref = refs[5 + len(extra):]
        i = pl.program_id(0)

        @pl.when(i == 0)
        def _():
            dw_ref[...] = jnp.zeros_like(dw_ref)
            db_ref[...] = jnp.zeros_like(db_ref)

        dy = dy_ref[...]
        x = x_ref[...]
        dy_halo = jnp.where(i == nblk - 1, 0.0, dyh_ref[...])
        x_halo = jnp.where(i == 0, 0.0, xh_ref[...])
        dx = dy * w_ref[3:4, :]
        dw_ref[3] += jnp.sum(dy * x, axis=0, keepdims=True)
        for k in range(CONV_WIDTH - 1):
            dx += _shift_up(dy, dy_halo, 3 - k) * w_ref[k:k + 1, :]
            dw_ref[k] += jnp.sum(dy * _shift_down(x, x_halo, 3 - k), axis=0, keepdims=True)
        db_ref[...] += jnp.sum(dy, axis=0, keepdims=True)
        o_ref[:, 0:width] = dx.astype(o_ref.dtype)
        off = width
        for p_ref, wd in zip(p_refs, extra):
            o_ref[:, off:off + wd] = p_ref[...].astype(o_ref.dtype)
            off += wd

    in_specs = [pl.BlockSpec((rows, width), lambda i: (i, 0)),
                pl.BlockSpec((HALO, width), lambda i: (jnp.minimum((i + 1) * per, nblk * per - 1), 0)),
                pl.BlockSpec((rows, width), lambda i: (i, 0)),
                pl.BlockSpec((HALO, width), lambda i: (jnp.maximum(i * per - 1, 0), 0)),
                pl.BlockSpec((CONV_WIDTH, width), lambda i: (0, 0))]
    in_specs += [pl.BlockSpec((rows, wd), lambda i: (i, 0)) for wd in extra]
    return _call(
        body, name=name, grid=(nblk,), in_specs=in_specs,
        out_specs=[pl.BlockSpec((rows, total), lambda i: (i, 0)),
                   pl.BlockSpec((CONV_WIDTH, 1, width), lambda i: (0, 0, 0)), pl.BlockSpec((1, width), lambda i: (0, 0))],
        out_shape=[jax.ShapeDtypeStruct((s_len, total), MXU_DTYPE), jax.ShapeDtypeStruct((CONV_WIDTH, 1, width), F32),
                   jax.ShapeDtypeStruct((1, width), F32)],
        compiler_params=_params("arbitrary"),
    )(dy, dy, src, src, w, *passthrough)


def _attn_mask(first):
    i = _iota((A_BLOCK, 2 * A_BLOCK), 0)
    j = _iota((A_BLOCK, 2 * A_BLOCK), 1)
    band = (j > i) & (j <= i + A_BLOCK)
    return band & ((j >= A_BLOCK) | jnp.logical_not(first))


def _attn_group(p, mask, qg, kw, kws, vw, vws, azg, sink0, sink1):
    low = _iota(qg.shape, 1) < A_HEAD_DIM
    first_lane = (_iota((A_BLOCK, LANES), 1) == 0).astype(F32)
    out = None
    for half, sink in ((0, sink0), (1, sink1)):
        kv_head = (2 * p + half) // (A_HEADS // A_KV_HEADS)
        keep = low if half == 0 else jnp.logical_not(low)
        qm = jnp.where(keep, qg, 0.0)
        kk, vv = (kw, vw) if kv_head == half else (kws, vws)
        s = mm_nt(qm, kk) * (A_HEAD_DIM ** -0.5)
        s = jnp.where(mask, s, NEG)
        sk = jnp.sum(jnp.tile(sink, (A_BLOCK // 8, 1)) * first_lane, axis=1, keepdims=True)
        m = lax.stop_gradient(jnp.maximum(jnp.max(s, axis=1, keepdims=True), sk))
        e = jnp.exp(s - m)
        denom = jnp.sum(e, axis=1, keepdims=True) + jnp.exp(sk - m)
        o = mm_nn(e * (1.0 / denom), vv)
        o = jnp.where(keep, o, 0.0)
        out = o if out is None else out + o
    return out * _silu(azg)


def _attn_specs(s_len, rev):
    nb = s_len // A_BLOCK

    def cur(i):
        return nb - 1 - i if rev else i

    def prev(i):
        return jnp.maximum(cur(i) - 1, 0)

    def blk(width, col, which):
        return pl.BlockSpec((A_BLOCK, width), lambda i: (which(i), col))

    return [blk(A_WIDTH, 0, cur), blk(A_WIDTH, 1, cur), blk(LANES, 8, cur), blk(LANES, 9, cur),
            blk(LANES, 8, prev), blk(LANES, 9, prev), blk(LANES, 0, cur), blk(LANES, 0, cur),
            blk(LANES, 0, prev), blk(LANES, 0, prev)], cur


def _attn_fwd(proj_a, cos, sin, sinks_t, *, name):
    s_len = proj_a.shape[0]
    specs, _ = _attn_specs(s_len, False)

    def body(q_ref, az_ref, k_ref, v_ref, kp_ref, vp_ref, c_ref, s_ref, cp_ref, sp_ref, sink_ref, o_ref):
        first = pl.program_id(0) == 0
        mask = _attn_mask(first)
        qr = _rope(q_ref[...], c_ref[...], s_ref[...])
        kw = jnp.concatenate([_rope(kp_ref[...], cp_ref[...], sp_ref[...]), _rope(k_ref[...], c_ref[...], s_ref[...])], 0)
        vw = jnp.concatenate([vp_ref[...], v_ref[...]], 0)
        kws, vws = _swap64(kw), _swap64(vw)
        for p in range(A_WIDTH // LANES):
            cols = slice(p * LANES, (p + 1) * LANES)
            o = _attn_group(p, mask, qr[:, cols], kw, kws, vw, vws, az_ref[:, cols], sink_ref[2 * p], sink_ref[2 * p + 1])
            o_ref[:, cols] = o.astype(o_ref.dtype)

    return _call(
        body, name=name, grid=(s_len // A_BLOCK,),
        in_specs=specs + [pl.BlockSpec((A_HEADS, 8, LANES), lambda i: (0, 0, 0))],
        out_specs=pl.BlockSpec((A_BLOCK, A_WIDTH), lambda i: (i, 0)),
        out_shape=jax.ShapeDtypeStruct((s_len, A_WIDTH), MXU_DTYPE),
        compiler_params=_params("parallel"),
    )(proj_a, proj_a, proj_a, proj_a, proj_a, proj_a, cos, sin, cos, sin, sinks_t)


def _attn_bwd(proj_a, cos, sin, sinks_t, dya, *, name):
    s_len = proj_a.shape[0]
    specs, cur = _attn_specs(s_len, True)

    def body(q_ref, az_ref, k_ref, v_ref, kp_ref, vp_ref, c_ref, s_ref, cp_ref, sp_ref, sink_ref, dy_ref,
             o_ref, dsink_ref, dk_carry, dv_carry):
        i = pl.program_id(0)

        @pl.when(i == 0)
        def _():
            dsink_ref[...] = jnp.zeros_like(dsink_ref)
            dk_carry[...] = jnp.zeros_like(dk_carry)
            dv_carry[...] = jnp.zeros_like(dv_carry)

        first = cur(i) == 0
        mask = _attn_mask(first)
        cos_c, sin_c = c_ref[...], s_ref[...]
        qr = _rope(q_ref[...], cos_c, sin_c)
        kw = jnp.concatenate([_rope(kp_ref[...], cp_ref[...], sp_ref[...]), _rope(k_ref[...], cos_c, sin_c)], 0)
        vw = jnp.concatenate([vp_ref[...], v_ref[...]], 0)
        kws, vws = _swap64(kw), _swap64(vw)
        dkw = jnp.zeros_like(kw)
        dvw = jnp.zeros_like(vw)
        for p in range(A_WIDTH // LANES):
            cols = slice(p * LANES, (p + 1) * LANES)
            _, vjp = jax.vjp(functools.partial(_attn_group, p, mask), qr[:, cols], kw, kws, vw, vws, az_ref[:, cols],
                             sink_ref[2 * p], sink_ref[2 * p + 1])
            dq, dk1, dk2, dv1, dv2, daz, ds0, ds1 = vjp(dy_ref[:, cols])
            dkw += dk1 + _swap64(dk2)
            dvw += dv1 + _swap64(dv2)
            o_ref[:, cols] = _rope_t(dq, cos_c, sin_c).astype(o_ref.dtype)
            o_ref[:, A_WIDTH + p * LANES:A_WIDTH + (p + 1) * LANES] = daz.astype(o_ref.dtype)
            dsink_ref[2 * p] += ds0
            dsink_ref[2 * p + 1] += ds1
        o_ref[:, 2 * A_WIDTH:2 * A_WIDTH + LANES] = _rope_t(dkw[A_BLOCK:, :] + dk_carry[...], cos_c, sin_c).astype(o_ref.dtype)
        o_ref[:, 2 * A_WIDTH + LANES:] = (dvw[A_BLOCK:, :] + dv_carry[...]).astype(o_ref.dtype)
        dk_carry[...] = dkw[:A_BLOCK, :]
        dv_carry[...] = dvw[:A_BLOCK, :]

    return _call(
        body, name=name, grid=(s_len // A_BLOCK,),
        in_specs=specs + [pl.BlockSpec((A_HEADS, 8, LANES), lambda i: (0, 0, 0)),
                          pl.BlockSpec((A_BLOCK, A_WIDTH), lambda i: (cur(i), 0))],
        out_specs=[pl.BlockSpec((A_BLOCK, WA), lambda i: (cur(i), 0)),
                   pl.BlockSpec((A_HEADS, 8, LANES), lambda i: (0, 0, 0))],
        out_shape=[jax.ShapeDtypeStruct((s_len, WA), MXU_DTYPE), jax.ShapeDtypeStruct((A_HEADS, 8, LANES), F32)],
        scratch_shapes=[pltpu.VMEM((A_BLOCK, LANES), F32), pltpu.VMEM((A_BLOCK, LANES), F32)],
        compiler_params=_params("arbitrary"),
    )(proj_a, proj_a, proj_a, proj_a, proj_a, proj_a, cos, sin, cos, sin, sinks_t, dya)


RG_ROWS = 256


def _rg_gates(x, wa, ba, wx, bx, lam):
    r = jax.nn.sigmoid(mm_nn(x, wa) + ba)
    ig = jax.nn.sigmoid(mm_nn(x, wx) + bx)
    log_a = -R_C * r * _softplus(-lam)
    return jnp.exp(log_a), jnp.sqrt(_neg_expm1(2.0 * log_a)) * (ig * x)


def _rg_param_specs():
    mat = pl.BlockSpec((R_BLOCKS, R_BLOCK_DIM, R_BLOCK_DIM), lambda i: (0, 0, 0))
    vec = pl.BlockSpec((1, R_WIDTH), lambda i: (0, 0))
    return [mat, vec, mat, vec, vec]


def _rg_fwd(xr, proj_r, wa, ba, wx, bx, lam, *, name):
    s_len = xr.shape[0]
    rows = min(RG_ROWS, s_len)

    def body(x_ref, z_ref, wa_ref, ba_ref, wx_ref, bx_ref, lam_ref, h_ref, y_ref, a_buf, u_buf, carry):
        @pl.when(pl.program_id(0) == 0)
        def _():
            carry[...] = jnp.zeros_like(carry)

        for n in range(R_BLOCKS):
            cols = slice(n * R_BLOCK_DIM, (n + 1) * R_BLOCK_DIM)
            a, u = _rg_gates(x_ref[:, cols], wa_ref[n], ba_ref[:, cols], wx_ref[n], bx_ref[:, cols], lam_ref[:, cols])
            a_buf[:, cols] = a
            u_buf[:, cols] = u

        def step(t, h):
            h = a_buf[pl.ds(t, 1), :] * h + u_buf[pl.ds(t, 1), :]
            h_ref[pl.ds(t, 1), :] = h
            return h

        carry[...] = lax.fori_loop(0, rows, step, carry[...], unroll=8)
        y_ref[...] = (h_ref[...] * _silu(z_ref[...])).astype(y_ref.dtype)

    blk = pl.BlockSpec((rows, R_WIDTH), lambda i: (i, 0))
    return _call(
        body, name=name, grid=(s_len // rows,),
        in_specs=[blk, pl.BlockSpec((rows, R_WIDTH), lambda i: (i, 1))] + _rg_param_specs(),
        out_specs=[blk, blk],
        out_shape=[jax.ShapeDtypeStruct((s_len, R_WIDTH), F32), jax.ShapeDtypeStruct((s_len, R_WIDTH), MXU_DTYPE)],
        scratch_shapes=[pltpu.VMEM((rows, R_WIDTH), F32), pltpu.VMEM((rows, R_WIDTH), F32), pltpu.VMEM((1, R_WIDTH), F32)],
        compiler_params=_params("arbitrary"),
    )(xr, proj_r, wa, ba, wx, bx, lam)


def _rg_bwd(xr, proj_r, h, dyr, wa, ba, wx, bx, lam, *, name):
    s_len = xr.shape[0]
    rows = min(RG_ROWS, s_len)
    nblk = s_len // rows
    per = rows // HALO

    def cur(i):
        return nblk - 1 - i

    def body(x_ref, z_ref, h_ref, hh_ref, dy_ref, wa_ref, ba_ref, wx_ref, bx_ref, lam_ref,
             dx_ref, dz_ref, dwa_ref, dba_ref, dwx_ref, dbx_ref, dlam_ref, a_buf, g_buf, carry):
        i = pl.program_id(0)

        @pl.when(i == 0)
        def _():
            carry[...] = jnp.zeros_like(carry)
            for ref in (dwa_ref, dba_ref, dwx_ref, dbx_ref, dlam_ref):
                ref[...] = jnp.zeros_like(ref)

        z = z_ref[...]
        sig = jax.nn.sigmoid(z)
        hval = h_ref[...]
        dy = dy_ref[...]
        dz_ref[...] = dy * hval * (sig * (1.0 + z * (1.0 - sig)))
        g_buf[...] = dy * (z * sig)
        kept = []
        for n in range(R_BLOCKS):
            cols = slice(n * R_BLOCK_DIM, (n + 1) * R_BLOCK_DIM)
            x = x_ref[:, cols]
            r = jax.nn.sigmoid(_nn(x, wa_ref[n]) + ba_ref[:, cols])
            ig = jax.nn.sigmoid(_nn(x, wx_ref[n]) + bx_ref[:, cols])
            sp = _softplus(-lam_ref[:, cols])
            log_a = -R_C * r * sp
            a = jnp.exp(log_a)
            a_buf[:, cols] = a
            kept.append((x, r, ig, sp, a, jnp.sqrt(_neg_expm1(2.0 * log_a))))

        def step(k, c):
            t = rows - 1 - k
            g = g_buf[pl.ds(t, 1), :] + c
            g_buf[pl.ds(t, 1), :] = g
            return a_buf[pl.ds(t, 1), :] * g

        carry[...] = lax.fori_loop(0, rows, step, carry[...], unroll=8)
        h_halo = jnp.where(cur(i) == 0, 0.0, hh_ref[...])
        dh = g_buf[...]
        da = dh * _shift_down(hval, h_halo, 1)
        for n in range(R_BLOCKS):
            cols = slice(n * R_BLOCK_DIM, (n + 1) * R_BLOCK_DIM)
            x, r, ig, sp, a, s = kept[n]
            du = dh[:, cols]
            dux = du * x
            d_log_a = a * (da[:, cols] - a * (dux * ig) / s)
            d_ga = d_log_a * (-R_C * sp) * (r * (1.0 - r))
            d_gx = dux * s * (ig * (1.0 - ig))
            dx_ref[:, cols] = du * (s * ig) + _nt(d_ga, wa_ref[n]) + _nt(d_gx, wx_ref[n])
            xt = _t(x)
            dwa_ref[n] += _nn(xt, d_ga)
            dwx_ref[n] += _nn(xt, d_gx)
            dba_ref[:, cols] += jnp.sum(d_ga, axis=0, keepdims=True)
            dbx_ref[:, cols] += jnp.sum(d_gx, axis=0, keepdims=True)
            dlam_ref[:, cols] += jnp.sum(d_log_a * r, axis=0, keepdims=True) * (R_C * jax.nn.sigmoid(-lam_ref[:, cols]))

    blk = pl.BlockSpec((rows, R_WIDTH), lambda i: (cur(i), 0))
    mat = pl.BlockSpec((R_BLOCKS, R_BLOCK_DIM, R_BLOCK_DIM), lambda i: (0, 0, 0))
    vec = pl.BlockSpec((1, R_WIDTH), lambda i: (0, 0))
    return _call(
        body, name=name, grid=(nblk,),
        in_specs=[blk, pl.BlockSpec((rows, R_WIDTH), lambda i: (cur(i), 1)), blk,
                  pl.BlockSpec((HALO, R_WIDTH), lambda i: (jnp.maximum(cur(i) * per - 1, 0), 0)), blk] + _rg_param_specs(),
        out_specs=[blk, blk, mat, vec, mat, vec, vec],
        out_shape=[jax.ShapeDtypeStruct((s_len, R_WIDTH), F32)] * 2 + [
            jax.ShapeDtypeStruct((R_BLOCKS, R_BLOCK_DIM, R_BLOCK_DIM), F32), jax.ShapeDtypeStruct((1, R_WIDTH), F32),
            jax.ShapeDtypeStruct((R_BLOCKS, R_BLOCK_DIM, R_BLOCK_DIM), F32), jax.ShapeDtypeStruct((1, R_WIDTH), F32),
            jax.ShapeDtypeStruct((1, R_WIDTH), F32)],
        scratch_shapes=[pltpu.VMEM((rows, R_WIDTH), F32), pltpu.VMEM((rows, R_WIDTH), F32), pltpu.VMEM((1, R_WIDTH), F32)],
        compiler_params=_params("arbitrary"),
    )(xr, proj_r, h, h, dyr, wa, ba, wx, bx, lam)


GP_CHUNKS = 4
GS_CHUNKS = 8


def _seg_cumsum(x, reverse):
    rows = x.shape[0]
    r = _iota(x.shape, 0) & (G_CHUNK - 1)
    s = 1
    while s < G_CHUNK:
        if reverse:
            x = x + jnp.where(r < G_CHUNK - s, pltpu.roll(x, rows - s, 0), 0.0)
        else:
            x = x + jnp.where(r >= s, pltpu.roll(x, s, 0), 0.0)
        s *= 2
    return x


def _gdn_decay(ga, a_log_row, dt_row):
    return -jnp.exp(a_log_row) * _softplus(ga + dt_row)


def _gdn_chunk(cq, ck, cv, gb, gc, inv=None):
    shape = cq.shape
    head = _iota(shape, 0) & (G_HEADS - 1)
    lane = _iota(shape, 2)
    q, k, v = _silu(cq), _silu(ck), _silu(cv)
    q = q * lax.rsqrt(jnp.sum(q * q, axis=-1, keepdims=True) + RMS_EPS) * (G_HEAD_DIM ** -0.5)
    k = k * lax.rsqrt(jnp.sum(k * k, axis=-1, keepdims=True) + RMS_EPS)
    beta = jnp.sum(jnp.where(lane == head, jax.nn.sigmoid(gb), 0.0), axis=-1, keepdims=True)
    g = jnp.sum(jnp.where(lane == head + G_HEADS, gc, 0.0), axis=-1, keepdims=True)
    sq = (shape[0], G_CHUNK, G_CHUNK)
    row, col = _iota(sq, 1), _iota(sq, 2)
    g_sq = jnp.broadcast_to(g, sq)
    decay = jnp.where(row >= col, jnp.exp(jnp.minimum(g_sq - _t(g_sq), 0.0)), 0.0)
    g_last = jnp.sum(jnp.where(_iota(g.shape, 1) == G_CHUNK - 1, g, 0.0), axis=1, keepdims=True)
    eg = jnp.exp(g)
    kb, vb = k * beta, v * beta
    m = jnp.where(row > col, mm_nt(kb, k) * decay, 0.0)
    known = inv is not None
    if not known:
        inv = _inv_unit_lower(m)
    u, w = _solve2(m, inv, vb, kb * eg)
    qk = jnp.where(row >= col, mm_nt(q, k) * decay, 0.0)
    q_dec = q * eg
    k_dec = k * jnp.exp(g_last - g)
    gl = jnp.broadcast_to(jnp.exp(g_last), (shape[0], 1, G_HEAD_DIM))
    return (u, w, qk, q_dec, k_dec, gl) if known else (u, w, qk, q_dec, k_dec, gl, inv)


def _gdn_step(state, u, w, qk, q_dec, k_dec, gl, gz, norm_w):
    v_new = u - mm_nn(w, state)
    o = mm_nn(q_dec, state) + mm_nn(qk, v_new)
    new_state = state * gl + mm_nn(_t(k_dec), v_new)
    o = o * lax.rsqrt(jnp.mean(o * o, axis=-1, keepdims=True) + RMS_EPS) * norm_w
    return o * _silu(gz), new_state


def _stack_chunks(x, heads):
    chunks = x.shape[0] // G_CHUNK
    parts = []
    for c in range(chunks):
        rows = slice(c * G_CHUNK, (c + 1) * G_CHUNK)
        for hd in range(G_HEADS):
            parts.append(x[rows, hd * LANES:(hd + 1) * LANES] if heads else x[rows, :])
    return jnp.stack(parts)


def _gdn_chunk_shapes(nch):
    b = nch * G_HEADS
    wide = jax.ShapeDtypeStruct((b, G_CHUNK, G_HEAD_DIM), F32)
    return [wide, wide, jax.ShapeDtypeStruct((b, G_CHUNK, G_CHUNK), F32), wide, wide,
            jax.ShapeDtypeStruct((b, 1, G_HEAD_DIM), F32)]


def _gdn_chunk_specs(nbatch):
    wide = pl.BlockSpec((nbatch, G_CHUNK, G_HEAD_DIM), lambda i: (i, 0, 0))
    return [wide, wide, pl.BlockSpec((nbatch, G_CHUNK, G_CHUNK), lambda i: (i, 0, 0)), wide, wide,
            pl.BlockSpec((nbatch, 1, G_HEAD_DIM), lambda i: (i, 0, 0))]


def _gdn_chunk_fwd(conv, proj_g, a_log_row, dt_row, *, name):
    s_len = conv.shape[0]
    cpg = min(GP_CHUNKS, s_len // G_CHUNK)
    rows = cpg * G_CHUNK
    nbatch = cpg * G_HEADS

    def body(c_ref, bg_ref, al_ref, dt_ref, *outs):
        bg = bg_ref[...]
        gc = _seg_cumsum(_gdn_decay(bg, al_ref[...], dt_ref[...]), False)
        res = _gdn_chunk(_stack_chunks(c_ref[:, 0:G_WIDTH], True), _stack_chunks(c_ref[:, G_WIDTH:2 * G_WIDTH], True),
                         _stack_chunks(c_ref[:, 2 * G_WIDTH:], True), _stack_chunks(bg, False), _stack_chunks(gc, False))
        for ref, val in zip(outs, res):
            ref[...] = val

    row = pl.BlockSpec((1, LANES), lambda i: (0, 0))
    return _call(
        body, name=name, grid=(s_len // rows,),
        in_specs=[pl.BlockSpec((rows, 3 * G_WIDTH), lambda i: (i, 0)),
                  pl.BlockSpec((rows, LANES), lambda i: (i, (3 * G_WIDTH + G_WIDTH) // LANES)), row, row],
        out_specs=_gdn_chunk_specs(nbatch) + [pl.BlockSpec((nbatch, G_CHUNK, G_CHUNK), lambda i: (i, 0, 0))],
        out_shape=_gdn_chunk_shapes(s_len // G_CHUNK) + [
            jax.ShapeDtypeStruct((s_len // G_CHUNK * G_HEADS, G_CHUNK, G_CHUNK), F32)],
        compiler_params=_params("parallel"),
    )(conv, proj_g, a_log_row, dt_row)


def _gdn_chunk_bwd(conv, proj_g, a_log_row, dt_row, inv, cots, *, name):
    s_len = conv.shape[0]
    cpg = min(GP_CHUNKS, s_len // G_CHUNK)
    rows = cpg * G_CHUNK
    nbatch = cpg * G_HEADS

    def unstack(x, heads):
        if heads:
            return jnp.concatenate([jnp.concatenate([x[c * G_HEADS + hd] for hd in range(G_HEADS)], axis=1)
                                    for c in range(cpg)], axis=0)
        return jnp.concatenate([sum(x[c * G_HEADS + hd] for hd in range(G_HEADS)) for c in range(cpg)], axis=0)

    def body(c_ref, bg_ref, al_ref, dt_ref, inv_ref, du, dw, dqk, dqd, dkd, dgl, dc_ref, dbg_ref, dal_ref, ddt_ref):
        @pl.when(pl.program_id(0) == 0)
        def _():
            dal_ref[...] = jnp.zeros_like(dal_ref)
            ddt_ref[...] = jnp.zeros_like(ddt_ref)

        bg = bg_ref[...]
        g_all, decay_vjp = jax.vjp(_gdn_decay, bg, al_ref[...], dt_ref[...])
        gc = _seg_cumsum(g_all, False)
        _, vjp = jax.vjp(_gdn_chunk, _stack_chunks(c_ref[:, 0:G_WIDTH], True),
                         _stack_chunks(c_ref[:, G_WIDTH:2 * G_WIDTH], True), _stack_chunks(c_ref[:, 2 * G_WIDTH:], True),
                         _stack_chunks(bg, False), _stack_chunks(gc, False), inv_ref[...])
        dq, dk, dv, dgb, dgc, _ = vjp((du[...], dw[...], dqk[...], dqd[...], dkd[...], dgl[...]))
        dc_ref[:, 0:G_WIDTH] = unstack(dq, True)
        dc_ref[:, G_WIDTH:2 * G_WIDTH] = unstack(dk, True)
        dc_ref[:, 2 * G_WIDTH:] = unstack(dv, True)
        dga, dal, ddt = decay_vjp(_seg_cumsum(unstack(dgc, False), True))
        dbg_ref[:, 0:LANES] = unstack(dgb, False) + dga
        dbg_ref[:, LANES:] = jnp.zeros((rows, LANES), F32)
        dal_ref[...] += dal
        ddt_ref[...] += ddt

    row = pl.BlockSpec((1, LANES), lambda i: (0, 0))
    return _call(
        body, name=name, grid=(s_len // rows,),
        in_specs=[pl.BlockSpec((rows, 3 * G_WIDTH), lambda i: (i, 0)),
                  pl.BlockSpec((rows, LANES), lambda i: (i, (3 * G_WIDTH + G_WIDTH) // LANES)), row, row,
                  pl.BlockSpec((nbatch, G_CHUNK, G_CHUNK), lambda i: (i, 0, 0))]
        + _gdn_chunk_specs(nbatch),
        out_specs=[pl.BlockSpec((rows, 3 * G_WIDTH), lambda i: (i, 0)), pl.BlockSpec((rows, 2 * LANES), lambda i: (i, 0)),
                   row, row],
        out_shape=[jax.ShapeDtypeStruct((s_len, 3 * G_WIDTH), F32), jax.ShapeDtypeStruct((s_len, 2 * LANES), F32),
                   jax.ShapeDtypeStruct((1, LANES), F32), jax.ShapeDtypeStruct((1, LANES), F32)],
        compiler_params=_params("arbitrary"),
    )(conv, proj_g, a_log_row, dt_row, inv, *cots)


def _gdn_scan_specs(cpg, which):
    nbatch = cpg * G_HEADS
    wide = pl.BlockSpec((nbatch, G_CHUNK, G_HEAD_DIM), lambda i: (which(i), 0, 0))
    return [wide, wide, pl.BlockSpec((nbatch, G_CHUNK, G_CHUNK), lambda i: (which(i), 0, 0)), wide, wide,
            pl.BlockSpec((nbatch, 1, G_HEAD_DIM), lambda i: (which(i), 0, 0))]


def _gz_stack(z_ref, c):
    rows = pl.ds(pl.multiple_of(c * G_CHUNK, G_CHUNK), G_CHUNK)
    return jnp.stack([z_ref[rows, hd * LANES:(hd + 1) * LANES] for hd in range(G_HEADS)])


def _gdn_scan_fwd(chunk_vals, proj_g, norm_w, *, name):
    s_len = proj_g.shape[0]
    nch = s_len // G_CHUNK
    cpg = min(GS_CHUNKS, nch)
    rows = cpg * G_CHUNK

    def body(u_ref, w_ref, qk_ref, qd_ref, kd_ref, gl_ref, z_ref, nw_ref, y_ref, st_ref, state):
        @pl.when(pl.program_id(0) == 0)
        def _():
            state[...] = jnp.zeros_like(state)

        def step(c, carry):
            b = pl.ds(pl.multiple_of(c * G_HEADS, G_HEADS), G_HEADS)
            st = state[...]
            st_ref[b] = st
            y, new_state = _gdn_step(st, u_ref[b], w_ref[b], qk_ref[b], qd_ref[b], kd_ref[b], gl_ref[b],
                                     _gz_stack(z_ref, c), nw_ref[...])
            state[...] = new_state
            rws = pl.ds(pl.multiple_of(c * G_CHUNK, G_CHUNK), G_CHUNK)
            for hd in range(G_HEADS):
                y_ref[rws, hd * LANES:(hd + 1) * LANES] = y[hd].astype(y_ref.dtype)
            return carry

        lax.fori_loop(0, cpg, step, 0, unroll=2)

    return _call(
        body, name=name, grid=(nch // cpg,),
        in_specs=_gdn_scan_specs(cpg, lambda i: i) + [
            pl.BlockSpec((rows, G_WIDTH), lambda i: (i, 3)), pl.BlockSpec((1, G_HEAD_DIM), lambda i: (0, 0))],
        out_specs=[pl.BlockSpec((rows, G_WIDTH), lambda i: (i, 0)),
                   pl.BlockSpec((cpg * G_HEADS, G_HEAD_DIM, G_HEAD_DIM), lambda i: (i, 0, 0))],
        out_shape=[jax.ShapeDtypeStruct((s_len, G_WIDTH), MXU_DTYPE),
                   jax.ShapeDtypeStruct((nch * G_HEADS, G_HEAD_DIM, G_HEAD_DIM), F32)],
        scratch_shapes=[pltpu.VMEM((G_HEADS, G_HEAD_DIM, G_HEAD_DIM), F32)],
        compiler_params=_params("arbitrary"),
    )(*chunk_vals, proj_g, norm_w)


def _gdn_scan_bwd(chunk_vals, states, proj_g, norm_w, dyg, *, name):
    s_len = proj_g.shape[0]
    nch = s_len // G_CHUNK
    cpg = min(GS_CHUNKS, nch)
    rows = cpg * G_CHUNK
    ngrid = nch // cpg

    def cur(i):
        return ngrid - 1 - i

    def body(u_ref, w_ref, qk_ref, qd_ref, kd_ref, gl_ref, st_ref, z_ref, nw_ref, dy_ref,
             du_ref, dw_ref, dqk_ref, dqd_ref, dkd_ref, dgl_ref, dz_ref, dnw_ref, dstate):
        @pl.when(pl.program_id(0) == 0)
        def _():
            dstate[...] = jnp.zeros_like(dstate)
            dnw_ref[...] = jnp.zeros_like(dnw_ref)

        def step(k, carry):
            c = cpg - 1 - k
            b = pl.ds(pl.multiple_of(c * G_HEADS, G_HEADS), G_HEADS)
            _, vjp = jax.vjp(_gdn_step, st_ref[b], u_ref[b], w_ref[b], qk_ref[b], qd_ref[b], kd_ref[b], gl_ref[b],
                             _gz_stack(z_ref, c), nw_ref[...])
            dst, du, dw, dqk, dqd, dkd, dgl, dz, dnw = vjp((_gz_stack(dy_ref, c), dstate[...]))
            dstate[...] = dst
            du_ref[b], dw_ref[b], dqk_ref[b], dqd_ref[b], dkd_ref[b], dgl_ref[b] = du, dw, dqk, dqd, dkd, dgl
            rws = pl.ds(pl.multiple_of(c * G_CHUNK, G_CHUNK), G_CHUNK)
            for hd in range(G_HEADS):
                dz_ref[rws, hd * LANES:(hd + 1) * LANES] = dz[hd]
            dnw_ref[...] += dnw
            return carry

        lax.fori_loop(0, cpg, step, 0, unroll=2)

    gate = pl.BlockSpec((rows, G_WIDTH), lambda i: (cur(i), 3))
    wide = pl.BlockSpec((rows, G_WIDTH), lambda i: (cur(i), 0))
    vec = pl.BlockSpec((1, G_HEAD_DIM), lambda i: (0, 0))
    return _call(
        body, name=name, grid=(ngrid,),
        in_specs=_gdn_scan_specs(cpg, cur) + [
            pl.BlockSpec((cpg * G_HEADS, G_HEAD_DIM, G_HEAD_DIM), lambda i: (cur(i), 0, 0)), gate, vec, wide],
        out_specs=_gdn_scan_specs(cpg, cur) + [wide, vec],
        out_shape=_gdn_chunk_shapes(nch) + [jax.ShapeDtypeStruct((s_len, G_WIDTH), F32),
                                            jax.ShapeDtypeStruct((1, G_HEAD_DIM), F32)],
        scratch_shapes=[pltpu.VMEM((G_HEADS, G_HEAD_DIM, G_HEAD_DIM), F32)],
        compiler_params=_params("arbitrary"),
    )(*chunk_vals, states, proj_g, norm_w, dyg)


def _adamw_math(w, g, m, v):
    m = ADAM_B1 * m + (1.0 - ADAM_B1) * g
    v = ADAM_B2 * v + (1.0 - ADAM_B2) * (g * g)
    m_hat = m / (1.0 - ADAM_B1 ** ADAM_STEP)
    v_hat = v / (1.0 - ADAM_B2 ** ADAM_STEP)
    delta = -ADAM_LR * (m_hat / (jnp.sqrt(v_hat) + ADAM_EPS) + ADAM_WD * w)
    return delta, m, v


def _sum_adamw(own, chip, parts, w, m, v, *, name, rows):
    n_layers, n_rows, n_cols = w.shape
    rows = min(rows, n_rows)
    n_parts = parts[0].shape[0]

    def body(c_ref, *refs):
        own_refs, part_refs = refs[:n_layers], refs[n_layers:2 * n_layers]
        w_ref, m_ref, v_ref, g_ref, d_ref, nm_ref, nv_ref = refs[2 * n_layers:]
        layer = pl.program_id(0)
        g = None
        for l in range(n_layers):
            g_l = own_refs[l][0].astype(F32)
            for k in range(n_parts):
                g_l = g_l + part_refs[l][k].astype(F32)
            g = g_l if g is None else jnp.where(layer == l, g_l, g)
        delta, new_m, new_v = _adamw_math(w_ref[0], g, m_ref[0], v_ref[0])
        g_ref[0], d_ref[0], nm_ref[0], nv_ref[0] = g, delta, new_m, new_v

    blk = pl.BlockSpec((1, rows, n_cols), lambda l, i, c: (l, i, 0))
    grid_spec = pltpu.PrefetchScalarGridSpec(
        num_scalar_prefetch=1, grid=(n_layers, n_rows // rows),
        in_specs=[pl.BlockSpec((1, rows, n_cols), lambda l, i, c: (c[0], i, 0))] * n_layers
        + [pl.BlockSpec((n_parts, rows, n_cols), lambda l, i, c: (0, i, 0))] * n_layers + [blk, blk, blk],
        out_specs=[blk] * 4)
    return _call(
        body, name=name, grid_spec=grid_spec, out_shape=[jax.ShapeDtypeStruct(w.shape, F32)] * 4,
        compiler_params=_params("parallel", "parallel"),
    )(_index_operand(chip), *own, *parts, w, m, v)


def _sum_slots(parts, *, name):
    rows = parts.shape[1]

    def body(p_ref, o_ref):
        g = p_ref[0]
        for k in range(1, N_DEV):
            g = g + p_ref[k]
        o_ref[...] = g

    return _call(body, name=name, grid=(1,),
                 in_specs=[pl.BlockSpec(parts.shape, lambda i: (0, 0, 0))],
                 out_specs=pl.BlockSpec((rows, LANES), lambda i: (0, 0)),
                 out_shape=jax.ShapeDtypeStruct((rows, LANES), F32), compiler_params=_params("arbitrary"))(parts)


def _adamw_packed(w, g, m, v, *, name):
    def body(w_ref, g_ref, m_ref, v_ref, d_ref, nm_ref, nv_ref):
        d_ref[...], nm_ref[...], nv_ref[...] = _adamw_math(w_ref[...], g_ref[...], m_ref[...], v_ref[...])

    blk = pl.BlockSpec(w.shape, lambda i: (0, 0))
    return _call(body, name=name, grid=(1,), in_specs=[blk] * 4, out_specs=[blk] * 3,
                 out_shape=[jax.ShapeDtypeStruct(w.shape, F32)] * 3, compiler_params=_params("arbitrary"))(w, g, m, v)


A_COLS = ((0, 512), (768, 1280), (512, 768))
R_COLS = ((1280, 3328),)
G_COLS = ((3328, 5384),)


def _group_weights(wt_full):
    def take(ranges):
        return jnp.concatenate([wt_full[a:b] for a, b in ranges], axis=0)

    wt_g = jnp.concatenate([take(G_COLS), jnp.zeros((G_PAD, wt_full.shape[1]), wt_full.dtype)], axis=0)
    return take(A_COLS), take(R_COLS), wt_g


def _ungroup_grads(d_a, d_r, d_g):
    return jnp.concatenate([d_a[0:512], d_a[1024:1280], d_a[512:1024], d_r, d_g[:WG - G_PAD]], axis=0)


def _shard_rows(w):
    return jnp.pad(jnp.transpose(w, (0, 2, 1)), ((0, 0), (0, N_ROWS_PAD - N_IN_SHARD), (0, 0)))


def _unshard_rows(wt):
    return jnp.transpose(wt[:, :N_IN_SHARD], (0, 2, 1))


def _owner_blocks(dwt):
    blocks = jnp.pad(dwt.reshape(4, 2, N_IN_SHARD, D_MODEL), ((0, 0), (0, 0), (0, N_ROWS_PAD - N_IN_SHARD), (0, 0)))
    return jnp.transpose(blocks, (1, 0, 2, 3))


def _rope_tables(s_len):
    inv = 1.0 / (ROPE_THETA ** (jnp.arange(0, A_HEAD_DIM, 2, dtype=F32) / A_HEAD_DIM))
    ang = jnp.arange(s_len, dtype=F32)[:, None] * inv[None, :]
    cos, sin = jnp.cos(ang), jnp.sin(ang)
    return jnp.tile(cos, (1, 4)), jnp.tile(jnp.concatenate([-sin, sin], axis=1), (1, 2))


def _pack(leaves):
    rows = []
    for leaf in leaves:
        flat = leaf.reshape(-1)
        pad = (-flat.shape[0]) % (8 * LANES)
        rows.append(jnp.pad(flat, (0, pad)).reshape(-1, LANES))
    return jnp.concatenate(rows, axis=0)


def _unpack(packed, shapes):
    out, row = [], 0
    for shape in shapes:
        size = math.prod(shape)
        nrows = -(-size // (8 * LANES)) * 8
        out.append(packed[row:row + nrows].reshape(-1)[:size].reshape(shape))
        row += nrows
    return out


def _lane_row(vals, offset):
    return jnp.pad(vals, (offset, LANES - offset - vals.shape[0])).reshape(1, LANES)


def kernel(x, w_in, sinks, r_conv_w, r_conv_b, r_wa, r_ba, r_wx, r_bx, r_lam, g_conv_w, g_a_log, g_dt_bias, g_norm_w, w_out, ln_g, ln_b, loss_target, m_w_in, m_sinks, m_r_conv_w, m_r_conv_b, m_r_wa, m_r_ba, m_r_wx, m_r_bx, m_r_lam, m_g_conv_w, m_g_a_log, m_g_dt_bias, m_g_norm_w, m_w_out, m_ln_g, m_ln_b, v_w_in, v_sinks, v_r_conv_w, v_r_conv_b, v_r_wa, v_r_ba, v_r_wx, v_r_bx, v_r_lam, v_g_conv_w, v_g_a_log, v_g_dt_bias, v_g_norm_w, v_w_out, v_ln_g, v_ln_b):
    s_len = x.shape[1]
    x0 = x.reshape(s_len, D_MODEL)
    target = loss_target.reshape(s_len, D_MODEL)
    me = 4 * lax.axis_index("x") + 2 * lax.axis_index("y") + lax.axis_index("c")
    core, chip = lax.axis_index("c"), 2 * lax.axis_index("x") + lax.axis_index("y")

    win_all, wout_all, rcw_all, gcw_all = _all_gather(
        [_shard_rows(w_in).astype(MXU_DTYPE).reshape(2 * DEPTH, N_ROWS_PAD // 2, D_MODEL),
         w_out.astype(MXU_DTYPE).reshape(2 * DEPTH, OUT_SHARD // 2, D_MODEL), r_conv_w[None], g_conv_w[None]],
        "gather_weights")
    win_all = win_all.reshape(N_DEV, DEPTH, N_ROWS_PAD, D_MODEL)
    wout_all = wout_all.reshape(N_DEV, DEPTH, OUT_SHARD, D_MODEL)
    rcw_full = jnp.moveaxis(rcw_all[:, 0], 0, 2).reshape(DEPTH, CONV_WIDTH, R_WIDTH)
    gcw_full = jnp.moveaxis(gcw_all[:, 0], 0, 2).reshape(DEPTH, CONV_WIDTH, 3 * G_WIDTH)
    cos, sin = _rope_tables(s_len)

    layers = []
    for l in range(DEPTH):
        wt_a, wt_r, wt_g = _group_weights(win_all[:, l, :N_IN_SHARD].reshape(N_IN, D_MODEL))
        wo = wout_all[:, l].reshape(D_MODEL, D_MODEL)
        layers.append(dict(
            wt_a=wt_a, wt_r=wt_r, wt_g=wt_g, wo=wo,
            wo_a=wo[0:A_WIDTH], wo_r=wo[A_WIDTH:A_WIDTH + R_WIDTH], wo_g=wo[A_WIDTH + R_WIDTH:],
            sinks_t=jnp.broadcast_to(sinks[l][:, None, None], (A_HEADS, 8, LANES)),
            rcw=rcw_full[l], rcb=r_conv_b[l].reshape(1, R_WIDTH), wa=r_wa[l], ba=r_ba[l].reshape(1, R_WIDTH),
            wx=r_wx[l], bx=r_bx[l].reshape(1, R_WIDTH), lam=r_lam[l].reshape(1, R_WIDTH),
            gcw=gcw_full[l], zero_b=jnp.zeros((1, 3 * G_WIDTH), F32),
            a_log=_lane_row(g_a_log[l], G_HEADS), dt=_lane_row(g_dt_bias[l], G_HEADS),
            norm_w=g_norm_w[l].reshape(1, G_HEAD_DIM), ln_g=ln_g[l].reshape(1, D_MODEL), ln_b=ln_b[l].reshape(1, D_MODEL)))

    saved = []
    xin = xin_lo = x0
    for l, p in enumerate(layers):
        proj_a = _matmul([xin_lo], [p["wt_a"]], name=f"proj_a{l}", tm=1024, tn=640, b_t=True)
        proj_r = _matmul([xin_lo], [p["wt_r"]], name=f"proj_r{l}", tm=1024, tn=512, b_t=True)
        proj_g = _matmul([xin_lo], [p["wt_g"]], name=f"proj_g{l}", tm=1024, tn=768, b_t=True)
        ya = _attn_fwd(proj_a, cos, sin, p["sinks_t"], name=f"attn_fwd{l}")
        xr = _conv_fwd(proj_r, R_WIDTH, p["rcw"], p["rcb"], name=f"rconv_fwd{l}")
        h, yr = _rg_fwd(xr, proj_r, p["wa"], p["ba"], p["wx"], p["bx"], p["lam"], name=f"rglru_fwd{l}")
        conv = _conv_fwd(proj_g, 3 * G_WIDTH, p["gcw"], p["zero_b"], name=f"gconv_fwd{l}")
        *chunk_vals, inv = _gdn_chunk_fwd(conv, proj_g, p["a_log"], p["dt"], name=f"gdn_chunk_fwd{l}")
        yg, states = _gdn_scan_fwd(chunk_vals, proj_g, p["norm_w"], name=f"gdn_scan_fwd{l}")
        z, xout, xout_lo = _outproj_ln(ya, yr, yg, p["wo"], xin, p["ln_g"], p["ln_b"], name=f"outproj_ln{l}")
        saved.append(dict(xin_lo=xin_lo, proj_a=proj_a, proj_r=proj_r, proj_g=proj_g, ya=ya, yr=yr, yg=yg, xr=xr, h=h,
                          conv=conv, chunk_vals=chunk_vals, inv=inv, states=states, z=z))
        xin, xin_lo = xout, xout_lo

    grads = [None] * DEPTH
    dxn = None
    loss_local = None
    for l in reversed(range(DEPTH)):
        p, sv = layers[l], saved[l]
        if dxn is None:
            dz, dz_lo, dln_g, dln_b, loss_local = _ln_bwd(sv["z"], p["ln_g"], name=f"ln_bwd{l}", xn=xin, target=target)
        else:
            dz, dz_lo, dln_g, dln_b = _ln_bwd(sv["z"], p["ln_g"], name=f"ln_bwd{l}", dxn=dxn)
        dya = _matmul([dz_lo], [p["wo_a"]], name=f"dya{l}", tm=1024, tn=512, b_t=True)
        dyr = _matmul([dz_lo], [p["wo_r"]], name=f"dyr{l}", tm=1024, tn=512, b_t=True)
        dyg = _matmul([dz_lo], [p["wo_g"]], name=f"dyg{l}", tm=1024, tn=512, b_t=True)
        dwo = jnp.concatenate([
            _matmul_tn(sv["ya"], dz_lo, name=f"dwo_a{l}", tm=512, tn=1024, tk=1024),
            _matmul_tn(sv["yr"], dz_lo, name=f"dwo_r{l}", tm=1024, tn=1024, tk=1024),
            _matmul_tn(sv["yg"], dz_lo, name=f"dwo_g{l}", tm=512, tn=1024, tk=1024)], axis=0)

        dproj_a, dsinks_t = _attn_bwd(sv["proj_a"], cos, sin, p["sinks_t"], dya, name=f"attn_bwd{l}")

        dxr, drz, dwa, dba, dwx, dbx, dlam = _rg_bwd(sv["xr"], sv["proj_r"], sv["h"], dyr, p["wa"], p["ba"], p["wx"],
                                                     p["bx"], p["lam"], name=f"rglru_bwd{l}")
        dproj_r, drcw, drcb = _conv_bwd(dxr, sv["proj_r"], R_WIDTH, p["rcw"], [drz], name=f"rconv_bwd{l}")

        scan_out = _gdn_scan_bwd(sv["chunk_vals"], sv["states"], sv["proj_g"], p["norm_w"], dyg, name=f"gdn_scan_bwd{l}")
        dgz, dnorm_w = scan_out[6], scan_out[7]
        dconv, dbg, dal, ddt = _gdn_chunk_bwd(sv["conv"], sv["proj_g"], p["a_log"], p["dt"], sv["inv"], scan_out[:6],
                                              name=f"gdn_chunk_bwd{l}")
        dproj_g, dgcw, _ = _conv_bwd(dconv, sv["proj_g"], 3 * G_WIDTH, p["gcw"], [dgz, dbg], name=f"gconv_bwd{l}")

        dwin = _ungroup_grads(_matmul_tn(dproj_a, sv["xin_lo"], name=f"dwin_a{l}", tm=640, tn=1024, tk=1024),
                              _matmul_tn(dproj_r, sv["xin_lo"], name=f"dwin_r{l}", tm=1024, tn=1024, tk=1024),
                              _matmul_tn(dproj_g, sv["xin_lo"], name=f"dwin_g{l}", tm=1152, tn=1024, tk=1024))

        dwin_blocks = _owner_blocks(dwin)[:, :, None].astype(MXU_DTYPE)
        dwout_blocks = jnp.transpose(dwo.reshape(4, 2, OUT_SHARD, D_MODEL), (1, 0, 2, 3))[:, :, None].astype(MXU_DTYPE)
        got_win, got_wout = _swap_cores(
            [dwin_blocks.reshape(2, 8, N_ROWS_PAD // 2, D_MODEL), dwout_blocks.reshape(2, 4, OUT_SHARD, D_MODEL)],
            f"swap_core_grads{l}")
        chip_win = _add_pair(dwin_blocks, got_win.reshape(dwin_blocks.shape[1:]), core, name=f"add_core_grads_w_in{l}",
                             rows=352).reshape(4, N_ROWS_PAD, D_MODEL)
        chip_wout = _add_pair(dwout_blocks, got_wout.reshape(dwout_blocks.shape[1:]), core, name=f"add_core_grads_w_out{l}",
                              rows=256).reshape(4, OUT_SHARD, D_MODEL)
        dxn, (win_parts, wout_parts) = _matmul(
            [dproj_a, dproj_r, dproj_g], [p["wt_a"], p["wt_r"], p["wt_g"]], name=f"dx{l}", tm=512, tn=512, add=dz,
            add_scale=DEEPNORM_ALPHA, comm=_ChipExchange([chip_win, chip_wout]))
        grads[l] = dict(
            chip_win=chip_win, chip_wout=chip_wout, win_parts=win_parts, wout_parts=wout_parts,
            sinks=dsinks_t[:, :, 0].sum(axis=1), r_conv_w=drcw.reshape(CONV_WIDTH, R_WIDTH),
            r_conv_b=drcb.reshape(R_WIDTH), r_wa=dwa, r_ba=dba.reshape(R_WIDTH), r_wx=dwx, r_bx=dbx.reshape(R_WIDTH),
            r_lam=dlam.reshape(R_WIDTH), g_conv_w=dgcw.reshape(CONV_WIDTH, 3 * G_WIDTH),
            g_a_log=dal[0, G_HEADS:2 * G_HEADS], g_dt_bias=ddt[0, G_HEADS:2 * G_HEADS],
            g_norm_w=dnorm_w.reshape(G_HEAD_DIM), ln_g=dln_g.reshape(D_MODEL), ln_b=dln_b.reshape(D_MODEL))
    grad_x = dxn.reshape(x.shape)
    loss = lax.psum(loss_local[0, 0], ("x", "y", "c"))

    def stacked(name):
        return jnp.stack([grads[l][name] for l in range(DEPTH)])

    def per_layer(name):
        return [grads[l][name] for l in range(DEPTH)]

    w_in_t = [_unshard_rows(t) for t in _sum_adamw(per_layer("chip_win"), chip, per_layer("win_parts"), _shard_rows(w_in),
                                                   _shard_rows(m_w_in), _shard_rows(v_w_in), name="adamw_w_in", rows=176)]
    g_w_in, d_w_in, nm_w_in, nv_w_in = w_in_t
    g_w_out, d_w_out, nm_w_out, nv_w_out = _sum_adamw(per_layer("chip_wout"), chip, per_layer("wout_parts"), w_out,
                                                      m_w_out, v_w_out, name="adamw_w_out", rows=128)

    small = ["sinks", "r_conv_w", "r_conv_b", "r_wa", "r_ba", "r_wx", "r_bx", "r_lam", "g_conv_w", "g_a_log",
             "g_dt_bias", "g_norm_w", "ln_g", "ln_b"]
    full_shapes = [stacked(nm).shape for nm in small]
    packed_small = _pack([stacked(nm) for nm in small])
    (all_small,) = _all_gather([packed_small.reshape(4, packed_small.shape[0] // 4, LANES)], "gather_small_grads")
    all_small = all_small.reshape(N_DEV, packed_small.shape[0], LANES)
    g_small = dict(zip(small, _unpack(_sum_slots(all_small, name="sum_small_grads"), full_shapes)))
    g_small["r_conv_w"] = lax.dynamic_slice_in_dim(g_small["r_conv_w"], me * (R_WIDTH // N_DEV), R_WIDTH // N_DEV, axis=2)
    g_small["g_conv_w"] = lax.dynamic_slice_in_dim(g_small["g_conv_w"], me * (3 * G_WIDTH // N_DEV), 3 * G_WIDTH // N_DEV, axis=2)
    given = dict(sinks=(sinks, m_sinks, v_sinks), r_conv_w=(r_conv_w, m_r_conv_w, v_r_conv_w),
                 r_conv_b=(r_conv_b, m_r_conv_b, v_r_conv_b), r_wa=(r_wa, m_r_wa, v_r_wa), r_ba=(r_ba, m_r_ba, v_r_ba),
                 r_wx=(r_wx, m_r_wx, v_r_wx), r_bx=(r_bx, m_r_bx, v_r_bx), r_lam=(r_lam, m_r_lam, v_r_lam),
                 g_conv_w=(g_conv_w, m_g_conv_w, v_g_conv_w), g_a_log=(g_a_log, m_g_a_log, v_g_a_log),
                 g_dt_bias=(g_dt_bias, m_g_dt_bias, v_g_dt_bias), g_norm_w=(g_norm_w, m_g_norm_w, v_g_norm_w),
                 ln_g=(ln_g, m_ln_g, v_ln_g), ln_b=(ln_b, m_ln_b, v_ln_b))
    shard_shapes = [given[nm][0].shape for nm in small]
    packed = [_pack([given[nm][k] for nm in small]) for k in range(3)]
    d_p, nm_p, nv_p = _adamw_packed(packed[0], _pack([g_small[nm] for nm in small]), packed[1], packed[2], name="adamw_small")
    d_small = dict(zip(small, _unpack(d_p, shard_shapes)))
    nm_small = dict(zip(small, _unpack(nm_p, shard_shapes)))
    nv_small = dict(zip(small, _unpack(nv_p, shard_shapes)))

    order = ["w_in"] + small[:12] + ["w_out"] + small[12:]

    def leaf(big_in, big_out, table):
        return [big_in if nm == "w_in" else big_out if nm == "w_out" else table[nm] for nm in order]

    return (loss, grad_x, *leaf(g_w_in, g_w_out, g_small), *leaf(d_w_in, d_w_out, d_small),
            *leaf(nm_w_in, nm_w_out, nm_small), *leaf(nv_w_in, nv_w_out, nv_small))
```

```python
import functools
import math

import jax
import jax.numpy as jnp
from jax import lax
from jax.experimental import pallas as pl
from jax.experimental.pallas import tpu as pltpu

F32 = jnp.float32
MXU_DTYPE = jnp.bfloat16
HIGHEST = lax.Precision.HIGHEST
MESH_ID = pl.DeviceIdType.MESH

N_DEV = 8
DEPTH = 2
D_MODEL = 2048
A_HEADS, A_KV_HEADS, A_HEAD_DIM = 8, 2, 64
A_WIDTH, A_KV_WIDTH = 512, 128
A_BLOCK = 128
ROPE_THETA = 10000.0
R_WIDTH, R_BLOCKS, R_BLOCK_DIM = 1024, 8, 128
R_C = 8.0
CONV_WIDTH = 4
G_HEADS, G_HEAD_DIM, G_WIDTH, G_CHUNK = 4, 128, 512, 64
N_IN = 5384
N_IN_SHARD = N_IN // N_DEV
N_ROWS_PAD = 704
OUT_SHARD = D_MODEL // N_DEV
WA, WR, WG = 1280, 2048, 2304
G_PAD = WG - (3 * G_WIDTH + G_WIDTH + 2 * G_HEADS)
DEEPNORM_ALPHA = (2 * DEPTH) ** 0.25
LN_EPS = 1e-5
RMS_EPS = 1e-6
ADAM_LR, ADAM_B1, ADAM_B2, ADAM_EPS, ADAM_WD, ADAM_STEP = 0.001, 0.9, 0.999, 1e-08, 0.01, 10
NEG = -1e30
VMEM_LIMIT = 56 * 1024 * 1024
LANES = 128


def _call(body, **kw):
    return pl.pallas_call(body, **kw)


def _params(*sem):
    return pltpu.CompilerParams(dimension_semantics=sem, vmem_limit_bytes=VMEM_LIMIT)


def _t(x):
    return jnp.swapaxes(x, -1, -2)


def _raw_dot(a, b, ca, cb, precision=None):
    batch = tuple(range(a.ndim - 2))
    if precision is None:
        a, b = a.astype(MXU_DTYPE), b.astype(MXU_DTYPE)
    return lax.dot_general(a, b, (((ca,), (cb,)), (batch, batch)), precision=precision,
                           preferred_element_type=F32)


def _nn(a, b, precision=None):
    return _raw_dot(a, b, a.ndim - 1, b.ndim - 2, precision)


def _nt(a, b, precision=None):
    return _raw_dot(a, b, a.ndim - 1, b.ndim - 1, precision)


@jax.custom_vjp
def mm_nn(a, b):
    return _nn(a, b)


def _mm_nn_fwd(a, b):
    return _nn(a, b), (a, b)


def _mm_nn_bwd(res, g):
    a, b = res
    return _nt(g, b), _nn(_t(a), g)


mm_nn.defvjp(_mm_nn_fwd, _mm_nn_bwd)


@jax.custom_vjp
def mm_nt(a, b):
    return _nt(a, b)


def _mm_nt_fwd(a, b):
    return _nt(a, b), (a, b)


def _mm_nt_bwd(res, g):
    a, b = res
    return _nn(g, b), _nn(_t(g), a)


mm_nt.defvjp(_mm_nt_fwd, _mm_nt_bwd)


def _split(x):
    hi = x.astype(MXU_DTYPE)
    return hi, (x - hi.astype(F32)).astype(MXU_DTYPE)


def _hmm(a, b, nt=False):
    dot = _nt if nt else _nn
    return dot(a[0], b[0]) + (dot(a[0], b[1]) + dot(a[1], b[0]))


def _silu(x):
    return x * jax.nn.sigmoid(x)


def _softplus(x):
    return jnp.maximum(x, 0.0) + jnp.log1p(jnp.exp(-jnp.abs(x)))


def _one_minus_sq(log_a, a):
    x = 2.0 * log_a
    return jnp.where(x > -0.01, -x * (1.0 + 0.5 * x), 1.0 - a * a)


def _iota(shape, dim):
    return lax.broadcasted_iota(jnp.int32, shape, dim)


def _inv_unit_lower(m):
    shape = m.shape
    row, col = _iota(shape, 1), _iota(shape, 2)
    eye = (row == col).astype(F32)

    def blockdiag(size):
        return (row // size) == (col // size)

    x = -jnp.where(blockdiag(8), m, 0.0)
    xs = _split(x)
    x2s = _split(_hmm(xs, xs))
    x4s = _split(_hmm(x2s, x2s))
    inv = eye + x
    inv = inv + _hmm(_split(inv), x2s)
    inv = inv + _hmm(_split(inv), x4s)
    for size in (8, 16, 32):
        below = jnp.where(blockdiag(2 * size) & jnp.logical_not(blockdiag(size)), m, 0.0)
        invs = _split(inv)
        inv = inv - _hmm(_split(_hmm(invs, _split(below))), invs)
    return inv


@jax.custom_vjp
def _solve2(m, inv, r1, r2):
    invs = _split(inv)
    return _hmm(invs, _split(r1)), _hmm(invs, _split(r2))


def _solve2_fwd(m, inv, r1, r2):
    x1, x2 = _solve2(m, inv, r1, r2)
    return (x1, x2), (inv, x1, x2)


def _solve2_bwd(res, g):
    inv, x1, x2 = res
    inv_ts = _split(_t(inv))
    d1, d2 = _hmm(inv_ts, _split(g[0])), _hmm(inv_ts, _split(g[1]))
    dm = -(_hmm(_split(d1), _split(x1), nt=True) + _hmm(_split(d2), _split(x2), nt=True))
    return dm, jnp.zeros_like(inv), d1, d2


_solve2.defvjp(_solve2_fwd, _solve2_bwd)


def _swap_halves(x):
    n = x.shape[-1]
    lane = _iota(x.shape, x.ndim - 1)
    return jnp.where((lane & 63) < 32, pltpu.roll(x, n - 32, x.ndim - 1), pltpu.roll(x, 32, x.ndim - 1))


def _rope(x, cos, sin):
    reps = x.shape[-1] // LANES
    if reps > 1:
        cos, sin = jnp.tile(cos, (1, reps)), jnp.tile(sin, (1, reps))
    return x * cos + _swap_halves(x) * sin


def _rope_t(d, cos, sin):
    reps = d.shape[-1] // LANES
    if reps > 1:
        cos, sin = jnp.tile(cos, (1, reps)), jnp.tile(sin, (1, reps))
    return d * cos + _swap_halves(d * sin)


def _swap64(x):
    return pltpu.roll(x, 64, x.ndim - 1)


def _mesh_pos():
    return lax.axis_index("x"), lax.axis_index("y"), lax.axis_index("c")


def _all_gather(arrays, name):
    n = len(arrays)
    npieces = [a.shape[0] for a in arrays]
    pmax = max(npieces)

    def body(*refs):
        ins, outs = refs[:n], refs[n:2 * n]
        send_sems, recv_sems, local_sem = refs[2 * n:]
        x, y, c = _mesh_pos()
        me, sibling = (x, y, c), (x, y, 1 - c)
        chips = [(1 - x, y), (x, 1 - y), (1 - x, 1 - y)]

        def slot(a, pos, p):
            return outs[a].at[4 * pos[0] + 2 * pos[1] + pos[2], p]

        def copy(a, p, k, block, to, own=False):
            return pltpu.make_async_remote_copy(
                src_ref=ins[a].at[p] if own else slot(a, block, p), dst_ref=slot(a, block, p),
                send_sem=send_sems.at[a, p, k], recv_sem=recv_sems.at[a, p, k], device_id=to, device_id_type=MESH_ID)

        pieces = [(a, p) for p in range(pmax) for a in range(n) if p < npieces[a]]
        mine = [pltpu.make_async_copy(ins[a].at[p], slot(a, me, p), local_sem.at[a, p]) for a, p in pieces]
        for cp in mine:
            cp.start()
        first = []
        for a, p in pieces:
            first += [copy(a, p, 1 + j, me, (*chip, c), own=True) for j, chip in enumerate(chips)]
            first.append(copy(a, p, 0, me, sibling, own=True))
        for cp in first:
            cp.start()
        passed = []
        for a, p in pieces:
            for j, chip in enumerate(chips):
                copy(a, p, 1 + j, (*chip, c), me).wait_recv()
                cp = copy(a, p, 4 + j, (*chip, c), sibling)
                cp.start()
                passed.append(cp)
        for a, p in pieces:
            copy(a, p, 0, sibling, me).wait_recv()
            for j, chip in enumerate(chips):
                copy(a, p, 4 + j, (*chip, 1 - c), me).wait_recv()
        for cp in first + passed:
            cp.wait_send()
        for cp in mine:
            cp.wait()

    any_spec = pl.BlockSpec(memory_space=pl.ANY)
    return _call(
        body, name=name,
        out_shape=[jax.ShapeDtypeStruct((N_DEV,) + a.shape, a.dtype) for a in arrays],
        in_specs=[any_spec] * n, out_specs=[any_spec] * n,
        scratch_shapes=[pltpu.SemaphoreType.DMA((n, pmax, 7)), pltpu.SemaphoreType.DMA((n, pmax, 7)),
                        pltpu.SemaphoreType.DMA((n, pmax))],
    )(*arrays)


def _swap_cores(arrays, name):
    n = len(arrays)
    pmax = max(a.shape[1] for a in arrays)

    def body(*refs):
        ins, got = refs[:n], refs[n:2 * n]
        send_sems, recv_sems = refs[2 * n:]
        x, y, c = _mesh_pos()
        copies = [pltpu.make_async_remote_copy(
            src_ref=ins[a].at[1 - c, p], dst_ref=got[a].at[p], send_sem=send_sems.at[a, p], recv_sem=recv_sems.at[a, p],
            device_id=(x, y, 1 - c), device_id_type=MESH_ID) for a in range(n) for p in range(arrays[a].shape[1])]
        for cp in copies:
            cp.start()
        for cp in copies:
            cp.wait()

    any_spec = pl.BlockSpec(memory_space=pl.ANY)
    return _call(
        body, name=name, out_shape=[jax.ShapeDtypeStruct(a.shape[1:], a.dtype) for a in arrays],
        in_specs=[any_spec] * n, out_specs=[any_spec] * n,
        scratch_shapes=[pltpu.SemaphoreType.DMA((n, pmax)), pltpu.SemaphoreType.DMA((n, pmax))],
    )(*arrays)


class _ChipExchange:
    aliases = {}

    def __init__(self, arrays):
        self.arrays = list(arrays)
        n = len(self.arrays)
        self.out_shape = [jax.ShapeDtypeStruct((3,) + a.shape[1:], a.dtype) for a in self.arrays]
        self.scratch = [pltpu.SemaphoreType.DMA((n, 3)), pltpu.SemaphoreType.DMA((n, 3))]

    def _copies(self, ins, outs, send_sems, recv_sems):
        x, y, c = _mesh_pos()
        copies = []
        for a in range(len(self.arrays)):
            for k in range(1, 4):
                px, py = x ^ (k >> 1), y ^ (k & 1)
                copies.append(pltpu.make_async_remote_copy(
                    src_ref=ins[a].at[2 * px + py], dst_ref=outs[a].at[k - 1], send_sem=send_sems.at[a, k - 1],
                    recv_sem=recv_sems.at[a, k - 1], device_id=(px, py, c), device_id_type=MESH_ID))
        return copies

    def start(self, ins, outs, send_sems, recv_sems):
        for cp in self._copies(ins, outs, send_sems, recv_sems):
            cp.start()

    def finish(self, ins, outs, send_sems, recv_sems):
        copies = self._copies(ins, outs, send_sems, recv_sems)
        for cp in copies:
            cp.wait_recv()
        for cp in copies:
            cp.wait_send()


def _slot(pos):
    return 4 * pos[0] + 2 * pos[1] + pos[2]


class _GatherSend:
    aliases = {}

    def __init__(self, arrays):
        self.arrays = list(arrays)
        n, pmax = len(self.arrays), max(a.shape[0] for a in self.arrays)
        self.out_shape = [jax.ShapeDtypeStruct((N_DEV,) + a.shape, a.dtype) for a in self.arrays]
        self.scratch = [pltpu.SemaphoreType.DMA((n, pmax, 4)), pltpu.SemaphoreType.DMA((n, pmax, 4)),
                        pltpu.SemaphoreType.DMA((n, pmax))]

    def _copies(self, ins, outs, send_sems, recv_sems, local_sems):
        x, y, c = _mesh_pos()
        peers = [(x, y, 1 - c), (1 - x, y, c), (x, 1 - y, c), (1 - x, 1 - y, c)]
        local, remote = [], []
        for a, arr in enumerate(self.arrays):
            for p in range(arr.shape[0]):
                local.append(pltpu.make_async_copy(ins[a].at[p], outs[a].at[_slot((x, y, c)), p], local_sems.at[a, p]))
                for k, peer in enumerate(peers):
                    remote.append(pltpu.make_async_remote_copy(
                        src_ref=ins[a].at[p], dst_ref=outs[a].at[_slot((x, y, c)), p], send_sem=send_sems.at[a, p, k],
                        recv_sem=recv_sems.at[a, p, k], device_id=peer, device_id_type=MESH_ID))
        return local, remote

    def start(self, *refs):
        local, remote = self._copies(*refs)
        for cp in local + remote:
            cp.start()

    def finish(self, *refs):
        local, remote = self._copies(*refs)
        for cp in remote:
            cp.wait_recv()
        for cp in remote:
            cp.wait_send()
        for cp in local:
            cp.wait()


class _GatherForward:
    def __init__(self, gathered):
        self.arrays = list(gathered)
        n, pmax = len(self.arrays), max(a.shape[1] for a in self.arrays)
        self.out_shape = [jax.ShapeDtypeStruct(a.shape, a.dtype) for a in self.arrays]
        self.aliases = {k: k for k in range(n)}
        self.scratch = [pltpu.SemaphoreType.DMA((n, pmax, 3)), pltpu.SemaphoreType.DMA((n, pmax, 3))]

    def _copies(self, ins, outs, send_sems, recv_sems):
        x, y, c = _mesh_pos()
        copies = []
        for a, arr in enumerate(self.arrays):
            for p in range(arr.shape[1]):
                for j, chip in enumerate([(1 - x, y), (x, 1 - y), (1 - x, 1 - y)]):
                    copies.append(pltpu.make_async_remote_copy(
                        src_ref=ins[a].at[_slot((*chip, c)), p], dst_ref=outs[a].at[_slot((*chip, c)), p],
                        send_sem=send_sems.at[a, p, j], recv_sem=recv_sems.at[a, p, j], device_id=(x, y, 1 - c),
                        device_id_type=MESH_ID))
        return copies

    def start(self, *refs):
        for cp in self._copies(*refs):
            cp.start()

    def finish(self, *refs):
        copies = self._copies(*refs)
        for cp in copies:
            cp.wait_recv()
        for cp in copies:
            cp.wait_send()


def _index_operand(i):
    return jnp.reshape(i, (1,)).astype(jnp.int32)


def _add_pair(pair, other, core, *, name, rows):
    _, n_slots, n_layers, n_rows, n_cols = pair.shape
    rows = min(rows, n_rows)

    def body(c_ref, a_ref, b_ref, o_ref):
        o_ref[...] = (a_ref[0].astype(F32) + b_ref[...].astype(F32)).astype(o_ref.dtype)

    blk = pl.BlockSpec((1, 1, rows, n_cols), lambda s, l, i, c: (s, l, i, 0))
    grid_spec = pltpu.PrefetchScalarGridSpec(
        num_scalar_prefetch=1, grid=(n_slots, n_layers, n_rows // rows),
        in_specs=[pl.BlockSpec((1, 1, 1, rows, n_cols), lambda s, l, i, c: (c[0], s, l, i, 0)), blk], out_specs=blk)
    return _call(body, name=name, grid_spec=grid_spec, out_shape=jax.ShapeDtypeStruct(other.shape, pair.dtype),
                 compiler_params=_params("parallel", "parallel", "parallel"))(_index_operand(core), pair, other)


def _matmul(a_list, b_list, *, name, tm, tn, b_t=False, out_dtype=F32, add=None, add_scale=1.0, comm=None):
    n = len(a_list)
    m_rows, n_cols = a_list[0].shape[0], b_list[0].shape[0 if b_t else 1]
    tm, tn = min(tm, m_rows), min(tn, n_cols)
    grid = (m_rows // tm, n_cols // tn)
    n_in = 2 * n + (add is not None)
    n_comm = len(comm.arrays) if comm is not None else 0

    def body(*refs):
        a_refs, b_refs = refs[:n], refs[n:2 * n]
        o_ref = refs[n_in + n_comm]
        if comm is not None:
            comm_refs = (refs[n_in:n_in + n_comm], refs[n_in + n_comm + 1:n_in + 2 * n_comm + 1],
                         *refs[-len(comm.scratch):])
            i, j = pl.program_id(0), pl.program_id(1)

            @pl.when((i == 0) & (j == 0))
            def _():
                comm.start(*comm_refs)

        acc = None
        for a_ref, b_ref in zip(a_refs, b_refs):
            part = lax.dot_general(a_ref[...].astype(MXU_DTYPE), b_ref[...].astype(MXU_DTYPE),
                                   (((1,), (1 if b_t else 0,)), ((), ())), preferred_element_type=F32)
            acc = part if acc is None else acc + part
        if add is not None:
            acc = acc + add_scale * refs[2 * n][...]
        o_ref[...] = acc.astype(o_ref.dtype)
        if comm is not None:
            @pl.when((i == grid[0] - 1) & (j == grid[1] - 1))
            def _():
                comm.finish(*comm_refs)

    in_specs = [pl.BlockSpec((tm, a.shape[1]), lambda i, j: (i, 0)) for a in a_list]
    if b_t:
        in_specs += [pl.BlockSpec((tn, b.shape[1]), lambda i, j: (j, 0)) for b in b_list]
    else:
        in_specs += [pl.BlockSpec((b.shape[0], tn), lambda i, j: (0, j)) for b in b_list]
    args = list(a_list) + list(b_list)
    if add is not None:
        in_specs.append(pl.BlockSpec((tm, tn), lambda i, j: (i, j)))
        args.append(add)
    out_specs = pl.BlockSpec((tm, tn), lambda i, j: (i, j))
    out_shape = jax.ShapeDtypeStruct((m_rows, n_cols), out_dtype)
    if comm is None:
        return _call(body, name=name, grid=grid, in_specs=in_specs, out_specs=out_specs, out_shape=out_shape,
                     compiler_params=_params("parallel", "arbitrary"))(*args)
    any_spec = pl.BlockSpec(memory_space=pl.ANY)
    outs = _call(body, name=name, grid=grid, in_specs=in_specs + [any_spec] * n_comm,
                 out_specs=[out_specs] + [any_spec] * n_comm, out_shape=[out_shape] + comm.out_shape,
                 input_output_aliases={n_in + k: 1 + v for k, v in comm.aliases.items()},
                 scratch_shapes=comm.scratch, compiler_params=_params("arbitrary", "arbitrary"))(*args, *comm.arrays)
    return outs[0], outs[1:]


def _matmul_tn(a, b, *, name, tm, tn, tk):
    k_rows, m_rows = a.shape
    n_cols = b.shape[1]
    tm, tn, tk = min(tm, m_rows), min(tn, n_cols), min(tk, k_rows)
    nk = k_rows // tk

    def body(a_ref, b_ref, o_ref):
        part = lax.dot_general(a_ref[...].astype(MXU_DTYPE), b_ref[...].astype(MXU_DTYPE),
                               (((0,), (0,)), ((), ())), preferred_element_type=F32)

        @pl.when(pl.program_id(2) == 0)
        def _():
            o_ref[...] = part

        @pl.when(pl.program_id(2) > 0)
        def _():
            o_ref[...] += part

    return _call(
        body, name=name, grid=(m_rows // tm, n_cols // tn, nk),
        in_specs=[pl.BlockSpec((tk, tm), lambda i, j, k: (k, i)), pl.BlockSpec((tk, tn), lambda i, j, k: (k, j))],
        out_specs=pl.BlockSpec((tm, tn), lambda i, j, k: (i, j)),
        out_shape=jax.ShapeDtypeStruct((m_rows, n_cols), F32),
        compiler_params=_params("parallel", "parallel", "arbitrary"),
    )(a, b)


def _outproj_ln(ya, yr, yg, w_out, x, ln_g, ln_b, *, name):
    s_len = x.shape[0]
    tm = min(256, s_len)

    def body(ya_ref, yr_ref, yg_ref, w_ref, x_ref, g_ref, b_ref, z_ref, o_ref, lo_ref):
        acc = jnp.dot(ya_ref[...], w_ref[0:A_WIDTH, :], preferred_element_type=F32)
        acc += jnp.dot(yr_ref[...], w_ref[A_WIDTH:A_WIDTH + R_WIDTH, :], preferred_element_type=F32)
        acc += jnp.dot(yg_ref[...], w_ref[A_WIDTH + R_WIDTH:, :], preferred_element_type=F32)
        z = DEEPNORM_ALPHA * x_ref[...] + acc
        z_ref[...] = z
        mu = jnp.mean(z, axis=-1, keepdims=True)
        zc = z - mu
        var = jnp.mean(zc * zc, axis=-1, keepdims=True)
        out = zc * lax.rsqrt(var + LN_EPS) * g_ref[...] + b_ref[...]
        o_ref[...] = out
        lo_ref[...] = out.astype(lo_ref.dtype)

    def rows(width):
        return pl.BlockSpec((tm, width), lambda i: (i, 0))

    def whole(shape):
        return pl.BlockSpec(shape, lambda i: (0, 0))

    return _call(
        body, name=name, grid=(s_len // tm,),
        in_specs=[rows(A_WIDTH), rows(R_WIDTH), rows(G_WIDTH), whole((D_MODEL, D_MODEL)), rows(D_MODEL),
                  whole((1, D_MODEL)), whole((1, D_MODEL))],
        out_specs=[rows(D_MODEL)] * 3,
        out_shape=[jax.ShapeDtypeStruct((s_len, D_MODEL), F32)] * 2 + [jax.ShapeDtypeStruct((s_len, D_MODEL), MXU_DTYPE)],
        compiler_params=_params("parallel"),
    )(ya, yr, yg, w_out, x, ln_g, ln_b)


def _ln_bwd(z, ln_g, *, name, dxn=None, xn=None, target=None):
    s_len = z.shape[0]
    tm = min(256, s_len)
    top = dxn is None

    def body(*refs):
        if top:
            z_ref, g_ref, xn_ref, t_ref, dz_ref, lo_ref, dg_ref, db_ref, loss_ref = refs
            err = xn_ref[...] - t_ref[...]
            dy = err * (1.0 / D_MODEL)
        else:
            z_ref, g_ref, dy_ref, dz_ref, lo_ref, dg_ref, db_ref = refs
            dy = dy_ref[...]
        first = pl.program_id(0) == 0

        @pl.when(first)
        def _():
            dg_ref[...] = jnp.zeros_like(dg_ref)
            db_ref[...] = jnp.zeros_like(db_ref)
            if top:
                loss_ref[...] = jnp.zeros_like(loss_ref)

        z = z_ref[...]
        mu = jnp.mean(z, axis=-1, keepdims=True)
        zc = z - mu
        rstd = lax.rsqrt(jnp.mean(zc * zc, axis=-1, keepdims=True) + LN_EPS)
        xhat = zc * rstd
        dxh = dy * g_ref[...]
        dz = rstd * (dxh - jnp.mean(dxh, axis=-1, keepdims=True) - xhat * jnp.mean(dxh * xhat, axis=-1, keepdims=True))
        dz_ref[...] = dz
        lo_ref[...] = dz.astype(lo_ref.dtype)
        dg_ref[...] += jnp.sum(dy * xhat, axis=0, keepdims=True)
        db_ref[...] += jnp.sum(dy, axis=0, keepdims=True)
        if top:
            per_row = jnp.sum(err * err, axis=-1, keepdims=True) * (0.5 / D_MODEL)
            loss_ref[...] += jnp.sum(per_row, axis=0, keepdims=True)

    rows = pl.BlockSpec((tm, D_MODEL), lambda i: (i, 0))
    vec = pl.BlockSpec((1, D_MODEL), lambda i: (0, 0))
    in_specs = [rows, vec] + ([rows, rows] if top else [rows])
    args = [z, ln_g] + ([xn, target] if top else [dxn])
    out_specs = [rows, rows, vec, vec]
    out_shape = [jax.ShapeDtypeStruct((s_len, D_MODEL), F32), jax.ShapeDtypeStruct((s_len, D_MODEL), MXU_DTYPE),
                 jax.ShapeDtypeStruct((1, D_MODEL), F32), jax.ShapeDtypeStruct((1, D_MODEL), F32)]
    if top:
        out_specs.append(pl.BlockSpec((1, 1), lambda i: (0, 0)))
        out_shape.append(jax.ShapeDtypeStruct((1, 1), F32))
    return _call(body, name=name, grid=(s_len // tm,), in_specs=in_specs, out_specs=out_specs,
                 out_shape=out_shape, compiler_params=_params("arbitrary"))(*args)


CONV_ROWS = 256
HALO = 8


def _shift_down(x, halo, s):
    if s == 0:
        return x
    ext = jnp.concatenate([halo, x], axis=0)
    return pltpu.roll(ext, s, 0)[HALO:, :]


def _shift_up(x, halo, s):
    if s == 0:
        return x
    ext = jnp.concatenate([x, halo], axis=0)
    return pltpu.roll(ext, ext.shape[0] - s, 0)[:x.shape[0], :]


def _conv_fwd(src, width, w, bias, *, name):
    s_len = src.shape[0]
    rows = min(CONV_ROWS, s_len)
    per = rows // HALO

    def body(x_ref, halo_ref, w_ref, b_ref, o_ref):
        x = x_ref[...]
        halo = jnp.where(pl.program_id(0) == 0, 0.0, halo_ref[...])
        acc = x * w_ref[3:4, :] + b_ref[...]
        for k in range(CONV_WIDTH - 1):
            acc += _shift_down(x, halo, 3 - k) * w_ref[k:k + 1, :]
        o_ref[...] = acc

    return _call(
        body, name=name, grid=(s_len // rows,),
        in_specs=[pl.BlockSpec((rows, width), lambda i: (i, 0)),
                  pl.BlockSpec((HALO, width), lambda i: (jnp.maximum(i * per - 1, 0), 0)),
                  pl.BlockSpec((CONV_WIDTH, width), lambda i: (0, 0)), pl.BlockSpec((1, width), lambda i: (0, 0))],
        out_specs=pl.BlockSpec((rows, width), lambda i: (i, 0)),
        out_shape=jax.ShapeDtypeStruct((s_len, width), F32),
        compiler_params=_params("parallel"),
    )(src, src, w, bias)


def _conv_bwd(dy, src, width, w, passthrough, *, name):
    s_len = src.shape[0]
    rows = min(CONV_ROWS, s_len)
    per = rows // HALO
    nblk = s_len // rows
    extra = [p.shape[1] for p in passthrough]
    total = width + sum(extra)

    def body(*refs):
        dy_ref, dyh_ref, x_ref, xh_ref, w_ref = refs[:5]
        p_refs = refs[5:5 + len(extra)]
        o_ref, dw_ref, db_ref = refs[5 + len(extra):]
        i = pl.program_id(0)

        @pl.when(i == 0)
        def _():
            dw_ref[...] = jnp.zeros_like(dw_ref)
            db_ref[...] = jnp.zeros_like(db_ref)

        dy = dy_ref[...]
        x = x_ref[...]
        dy_halo = jnp.where(i == nblk - 1, 0.0, dyh_ref[...])
        x_halo = jnp.where(i == 0, 0.0, xh_ref[...])
        dx = dy * w_ref[3:4, :]
        dw_ref[3] += jnp.sum(dy * x, axis=0, keepdims=True)
        for k in range(CONV_WIDTH - 1):
            dx += _shift_up(dy, dy_halo, 3 - k) * w_ref[k:k + 1, :]
            dw_ref[k] += jnp.sum(dy * _shift_down(x, x_halo, 3 - k), axis=0, keepdims=True)
        db_ref[...] += jnp.sum(dy, axis=0, keepdims=True)
        o_ref[:, 0:width] = dx.astype(o_ref.dtype)
        off = width
        for p_ref, wd in zip(p_refs, extra):
            o_ref[:, off:off + wd] = p_ref[...].astype(o_ref.dtype)
            off += wd

    in_specs = [pl.BlockSpec((rows, width), lambda i: (i, 0)),
                pl.BlockSpec((HALO, width), lambda i: (jnp.minimum((i + 1) * per, nblk * per - 1), 0)),
                pl.BlockSpec((rows, width), lambda i: (i, 0)),
                pl.BlockSpec((HALO, width), lambda i: (jnp.maximum(i * per - 1, 0), 0)),
                pl.BlockSpec((CONV_WIDTH, width), lambda i: (0, 0))]
    in_specs += [pl.BlockSpec((rows, wd), lambda i: (i, 0)) for wd in extra]
    return _call(
        body, name=name, grid=(nblk,), in_specs=in_specs,
        out_specs=[pl.BlockSpec((rows, total), lambda i: (i, 0)),
                   pl.BlockSpec((CONV_WIDTH, 1, width), lambda i: (0, 0, 0)), pl.BlockSpec((1, width), lambda i: (0, 0))],
        out_shape=[jax.ShapeDtypeStruct((s_len, total), MXU_DTYPE), jax.ShapeDtypeStruct((CONV_WIDTH, 1, width), F32),
                   jax.ShapeDtypeStruct((1, width), F32)],
        compiler_params=_params("arbitrary"),
    )(dy, dy, src, src, w, *passthrough)


def _attn_mask(first):
    i = _iota((A_BLOCK, 2 * A_BLOCK), 0)
    j = _iota((A_BLOCK, 2 * A_BLOCK), 1)
    band = (j > i) & (j <= i + A_BLOCK)
    return band & ((j >= A_BLOCK) | jnp.logical_not(first))


def _attn_group(p, mask, qg, kw, kws, vw, vws, azg, sink0, sink1):
    low = _iota(qg.shape, 1) < A_HEAD_DIM
    first_lane = (_iota((A_BLOCK, LANES), 1) == 0).astype(F32)
    out = None
    for half, sink in ((0, sink0), (1, sink1)):
        kv_head = (2 * p + half) // (A_HEADS // A_KV_HEADS)
        keep = low if half == 0 else jnp.logical_not(low)
        qm = jnp.where(keep, qg, 0.0)
        kk, vv = (kw, vw) if kv_head == half else (kws, vws)
        s = mm_nt(qm, kk) * (A_HEAD_DIM ** -0.5)
        s = jnp.where(mask, s, NEG)
        sk = jnp.sum(jnp.tile(sink, (A_BLOCK // 8, 1)) * first_lane, axis=1, keepdims=True)
        m = lax.stop_gradient(jnp.maximum(jnp.max(s, axis=1, keepdims=True), sk))
        e = jnp.exp(s - m)
        denom = jnp.sum(e, axis=1, keepdims=True) + jnp.exp(sk - m)
        o = mm_nn(e * (1.0 / denom), vv)
        o = jnp.where(keep, o, 0.0)
        out = o if out is None else out + o
    return out * _silu(azg)


def _attn_specs(s_len, rev):
    nb = s_len // A_BLOCK

    def cur(i):
        return nb - 1 - i if rev else i

    def prev(i):
        return jnp.maximum(cur(i) - 1, 0)

    def blk(width, col, which):
        return pl.BlockSpec((A_BLOCK, width), lambda i: (which(i), col))

    return [blk(A_WIDTH, 0, cur), blk(A_WIDTH, 1, cur), blk(LANES, 8, cur), blk(LANES, 9, cur),
            blk(LANES, 8, prev), blk(LANES, 9, prev), blk(LANES, 0, cur), blk(LANES, 0, cur),
            blk(LANES, 0, prev), blk(LANES, 0, prev)], cur


def _attn_fwd(proj_a, cos, sin, sinks_t, *, name):
    s_len = proj_a.shape[0]
    specs, _ = _attn_specs(s_len, False)

    def body(q_ref, az_ref, k_ref, v_ref, kp_ref, vp_ref, c_ref, s_ref, cp_ref, sp_ref, sink_ref, o_ref):
        first = pl.program_id(0) == 0
        mask = _attn_mask(first)
        qr = _rope(q_ref[...], c_ref[...], s_ref[...])
        kw = jnp.concatenate([_rope(kp_ref[...], cp_ref[...], sp_ref[...]), _rope(k_ref[...], c_ref[...], s_ref[...])], 0)
        vw = jnp.concatenate([vp_ref[...], v_ref[...]], 0)
        kws, vws = _swap64(kw), _swap64(vw)
        for p in range(A_WIDTH // LANES):
            cols = slice(p * LANES, (p + 1) * LANES)
            o = _attn_group(p, mask, qr[:, cols], kw, kws, vw, vws, az_ref[:, cols], sink_ref[2 * p], sink_ref[2 * p + 1])
            o_ref[:, cols] = o.astype(o_ref.dtype)

    return _call(
        body, name=name, grid=(s_len // A_BLOCK,),
        in_specs=specs + [pl.BlockSpec((A_HEADS, 8, LANES), lambda i: (0, 0, 0))],
        out_specs=pl.BlockSpec((A_BLOCK, A_WIDTH), lambda i: (i, 0)),
        out_shape=jax.ShapeDtypeStruct((s_len, A_WIDTH), MXU_DTYPE),
        compiler_params=_params("parallel"),
    )(proj_a, proj_a, proj_a, proj_a, proj_a, proj_a, cos, sin, cos, sin, sinks_t)


def _attn_bwd(proj_a, cos, sin, sinks_t, dya, *, name):
    s_len = proj_a.shape[0]
    specs, cur = _attn_specs(s_len, True)

    def body(q_ref, az_ref, k_ref, v_ref, kp_ref, vp_ref, c_ref, s_ref, cp_ref, sp_ref, sink_ref, dy_ref,
             o_ref, dsink_ref, dk_carry, dv_carry):
        i = pl.program_id(0)

        @pl.when(i == 0)
        def _():
            dsink_ref[...] = jnp.zeros_like(dsink_ref)
            dk_carry[...] = jnp.zeros_like(dk_carry)
            dv_carry[...] = jnp.zeros_like(dv_carry)

        first = cur(i) == 0
        mask = _attn_mask(first)
        cos_c, sin_c = c_ref[...], s_ref[...]
        qr = _rope(q_ref[...], cos_c, sin_c)
        kw = jnp.concatenate([_rope(kp_ref[...], cp_ref[...], sp_ref[...]), _rope(k_ref[...], cos_c, sin_c)], 0)
        vw = jnp.concatenate([vp_ref[...], v_ref[...]], 0)
        kws, vws = _swap64(kw), _swap64(vw)
        dkw = jnp.zeros_like(kw)
        dvw = jnp.zeros_like(vw)
        for p in range(A_WIDTH // LANES):
            cols = slice(p * LANES, (p + 1) * LANES)
            _, vjp = jax.vjp(functools.partial(_attn_group, p, mask), qr[:, cols], kw, kws, vw, vws, az_ref[:, cols],
                             sink_ref[2 * p], sink_ref[2 * p + 1])
            dq, dk1, dk2, dv1, dv2, daz, ds0, ds1 = vjp(dy_ref[:, cols])
            dkw += dk1 + _swap64(dk2)
            dvw += dv1 + _swap64(dv2)
            o_ref[:, cols] = _rope_t(dq, cos_c, sin_c).astype(o_ref.dtype)
            o_ref[:, A_WIDTH + p * LANES:A_WIDTH + (p + 1) * LANES] = daz.astype(o_ref.dtype)
            dsink_ref[2 * p] += ds0
            dsink_ref[2 * p + 1] += ds1
        o_ref[:, 2 * A_WIDTH:2 * A_WIDTH + LANES] = _rope_t(dkw[A_BLOCK:, :] + dk_carry[...], cos_c, sin_c).astype(o_ref.dtype)
        o_ref[:, 2 * A_WIDTH + LANES:] = (dvw[A_BLOCK:, :] + dv_carry[...]).astype(o_ref.dtype)
        dk_carry[...] = dkw[:A_BLOCK, :]
        dv_carry[...] = dvw[:A_BLOCK, :]

    return _call(
        body, name=name, grid=(s_len // A_BLOCK,),
        in_specs=specs + [pl.BlockSpec((A_HEADS, 8, LANES), lambda i: (0, 0, 0)),
                          pl.BlockSpec((A_BLOCK, A_WIDTH), lambda i: (cur(i), 0))],
        out_specs=[pl.BlockSpec((A_BLOCK, WA), lambda i: (cur(i), 0)),
                   pl.BlockSpec((A_HEADS, 8, LANES), lambda i: (0, 0, 0))],
        out_shape=[jax.ShapeDtypeStruct((s_len, WA), MXU_DTYPE), jax.ShapeDtypeStruct((A_HEADS, 8, LANES), F32)],
        scratch_shapes=[pltpu.VMEM((A_BLOCK, LANES), F32), pltpu.VMEM((A_BLOCK, LANES), F32)],
        compiler_params=_params("arbitrary"),
    )(proj_a, proj_a, proj_a, proj_a, proj_a, proj_a, cos, sin, cos, sin, sinks_t, dya)


RG_ROWS = 256


def _rg_gates(x, wa, ba, wx, bx, lam):
    r = jax.nn.sigmoid(mm_nn(x, wa) + ba)
    ig = jax.nn.sigmoid(mm_nn(x, wx) + bx)
    log_a = -R_C * r * _softplus(-lam)
    a = jnp.exp(log_a)
    return a, jnp.sqrt(_one_minus_sq(log_a, a)) * (ig * x)


def _rg_param_specs():
    mat = pl.BlockSpec((R_BLOCKS, R_BLOCK_DIM, R_BLOCK_DIM), lambda i: (0, 0, 0))
    vec = pl.BlockSpec((1, R_WIDTH), lambda i: (0, 0))
    return [mat, vec, mat, vec, vec]


def _rg_fwd(xr, proj_r, wa, ba, wx, bx, lam, *, name):
    s_len = xr.shape[0]
    rows = min(RG_ROWS, s_len)

    def body(x_ref, z_ref, wa_ref, ba_ref, wx_ref, bx_ref, lam_ref, h_ref, y_ref, a_buf, u_buf, carry):
        @pl.when(pl.program_id(0) == 0)
        def _():
            carry[...] = jnp.zeros_like(carry)

        for n in range(R_BLOCKS):
            cols = slice(n * R_BLOCK_DIM, (n + 1) * R_BLOCK_DIM)
            a, u = _rg_gates(x_ref[:, cols], wa_ref[n], ba_ref[:, cols], wx_ref[n], bx_ref[:, cols], lam_ref[:, cols])
            a_buf[:, cols] = a
            u_buf[:, cols] = u

        def step(t, h):
            h = a_buf[pl.ds(t, 1), :] * h + u_buf[pl.ds(t, 1), :]
            h_ref[pl.ds(t, 1), :] = h
            return h

        carry[...] = lax.fori_loop(0, rows, step, carry[...], unroll=8)
        y_ref[...] = (h_ref[...] * _silu(z_ref[...])).astype(y_ref.dtype)

    blk = pl.BlockSpec((rows, R_WIDTH), lambda i: (i, 0))
    return _call(
        body, name=name, grid=(s_len // rows,),
        in_specs=[blk, pl.BlockSpec((rows, R_WIDTH), lambda i: (i, 1))] + _rg_param_specs(),
        out_specs=[blk, blk],
        out_shape=[jax.ShapeDtypeStruct((s_len, R_WIDTH), F32), jax.ShapeDtypeStruct((s_len, R_WIDTH), MXU_DTYPE)],
        scratch_shapes=[pltpu.VMEM((rows, R_WIDTH), F32), pltpu.VMEM((rows, R_WIDTH), F32), pltpu.VMEM((1, R_WIDTH), F32)],
        compiler_params=_params("arbitrary"),
    )(xr, proj_r, wa, ba, wx, bx, lam)


def _rg_bwd(xr, proj_r, h, dyr, wa, ba, wx, bx, lam, *, name):
    s_len = xr.shape[0]
    rows = min(RG_ROWS, s_len)
    nblk = s_len // rows
    per = rows // HALO

    def cur(i):
        return nblk - 1 - i

    def body(x_ref, z_ref, h_ref, hh_ref, dy_ref, wa_ref, ba_ref, wx_ref, bx_ref, lam_ref,
             dx_ref, dz_ref, dwa_ref, dba_ref, dwx_ref, dbx_ref, dlam_ref, a_buf, g_buf, carry):
        i = pl.program_id(0)

        @pl.when(i == 0)
        def _():
            carry[...] = jnp.zeros_like(carry)
            for ref in (dwa_ref, dba_ref, dwx_ref, dbx_ref, dlam_ref):
                ref[...] = jnp.zeros_like(ref)

        z = z_ref[...]
        sig = jax.nn.sigmoid(z)
        hval = h_ref[...]
        dy = dy_ref[...]
        dz_ref[...] = dy * hval * (sig * (1.0 + z * (1.0 - sig)))
        g_buf[...] = dy * (z * sig)
        kept = []
        for n in range(R_BLOCKS):
            cols = slice(n * R_BLOCK_DIM, (n + 1) * R_BLOCK_DIM)
            x = x_ref[:, cols]
            r = jax.nn.sigmoid(_nn(x, wa_ref[n]) + ba_ref[:, cols])
            ig = jax.nn.sigmoid(_nn(x, wx_ref[n]) + bx_ref[:, cols])
            sp = _softplus(-lam_ref[:, cols])
            log_a = -R_C * r * sp
            a = jnp.exp(log_a)
            a_buf[:, cols] = a
            kept.append((x, r, ig, sp, a, jnp.sqrt(_one_minus_sq(log_a, a))))

        def step(k, c):
            t = rows - 1 - k
            g = g_buf[pl.ds(t, 1), :] + c
            g_buf[pl.ds(t, 1), :] = g
            return a_buf[pl.ds(t, 1), :] * g

        carry[...] = lax.fori_loop(0, rows, step, carry[...], unroll=8)
        h_halo = jnp.where(cur(i) == 0, 0.0, hh_ref[...])
        dh = g_buf[...]
        da = dh * _shift_down(hval, h_halo, 1)
        for n in range(R_BLOCKS):
            cols = slice(n * R_BLOCK_DIM, (n + 1) * R_BLOCK_DIM)
            x, r, ig, sp, a, s = kept[n]
            du = dh[:, cols]
            dux = du * x
            d_log_a = a * (da[:, cols] - a * (dux * ig) / s)
            d_ga = d_log_a * (-R_C * sp) * (r * (1.0 - r))
            d_gx = dux * s * (ig * (1.0 - ig))
            dx_ref[:, cols] = du * (s * ig) + _nt(d_ga, wa_ref[n]) + _nt(d_gx, wx_ref[n])
            xt = _t(x)
            dwa_ref[n] += _nn(xt, d_ga)
            dwx_ref[n] += _nn(xt, d_gx)
            dba_ref[:, cols] += jnp.sum(d_ga, axis=0, keepdims=True)
            dbx_ref[:, cols] += jnp.sum(d_gx, axis=0, keepdims=True)
            dlam_ref[:, cols] += jnp.sum(d_log_a * r, axis=0, keepdims=True) * (R_C * jax.nn.sigmoid(-lam_ref[:, cols]))

    blk = pl.BlockSpec((rows, R_WIDTH), lambda i: (cur(i), 0))
    mat = pl.BlockSpec((R_BLOCKS, R_BLOCK_DIM, R_BLOCK_DIM), lambda i: (0, 0, 0))
    vec = pl.BlockSpec((1, R_WIDTH), lambda i: (0, 0))
    return _call(
        body, name=name, grid=(nblk,),
        in_specs=[blk, pl.BlockSpec((rows, R_WIDTH), lambda i: (cur(i), 1)), blk,
                  pl.BlockSpec((HALO, R_WIDTH), lambda i: (jnp.maximum(cur(i) * per - 1, 0), 0)), blk] + _rg_param_specs(),
        out_specs=[blk, blk, mat, vec, mat, vec, vec],
        out_shape=[jax.ShapeDtypeStruct((s_len, R_WIDTH), F32)] * 2 + [
            jax.ShapeDtypeStruct((R_BLOCKS, R_BLOCK_DIM, R_BLOCK_DIM), F32), jax.ShapeDtypeStruct((1, R_WIDTH), F32),
            jax.ShapeDtypeStruct((R_BLOCKS, R_BLOCK_DIM, R_BLOCK_DIM), F32), jax.ShapeDtypeStruct((1, R_WIDTH), F32),
            jax.ShapeDtypeStruct((1, R_WIDTH), F32)],
        scratch_shapes=[pltpu.VMEM((rows, R_WIDTH), F32), pltpu.VMEM((rows, R_WIDTH), F32), pltpu.VMEM((1, R_WIDTH), F32)],
        compiler_params=_params("arbitrary"),
    )(xr, proj_r, h, h, dyr, wa, ba, wx, bx, lam)


GP_CHUNKS = 4
GS_CHUNKS = 8


def _seg_cumsum(x, reverse):
    rows = x.shape[0]
    r = _iota(x.shape, 0) & (G_CHUNK - 1)
    s = 1
    while s < G_CHUNK:
        if reverse:
            x = x + jnp.where(r < G_CHUNK - s, pltpu.roll(x, rows - s, 0), 0.0)
        else:
            x = x + jnp.where(r >= s, pltpu.roll(x, s, 0), 0.0)
        s *= 2
    return x


def _gdn_decay(ga, a_log_row, dt_row):
    return -jnp.exp(a_log_row) * _softplus(ga + dt_row)


def _gdn_chunk(cq, ck, cv, gb, gc, inv=None):
    shape = cq.shape
    head = _iota(shape, 0) & (G_HEADS - 1)
    lane = _iota(shape, 2)
    q, k, v = _silu(cq), _silu(ck), _silu(cv)
    q = q * lax.rsqrt(jnp.sum(q * q, axis=-1, keepdims=True) + RMS_EPS) * (G_HEAD_DIM ** -0.5)
    k = k * lax.rsqrt(jnp.sum(k * k, axis=-1, keepdims=True) + RMS_EPS)
    beta = jnp.sum(jnp.where(lane == head, jax.nn.sigmoid(gb), 0.0), axis=-1, keepdims=True)
    g = jnp.sum(jnp.where(lane == head + G_HEADS, gc, 0.0), axis=-1, keepdims=True)
    sq = (shape[0], G_CHUNK, G_CHUNK)
    row, col = _iota(sq, 1), _iota(sq, 2)
    g_sq = jnp.broadcast_to(g, sq)
    decay = jnp.where(row >= col, jnp.exp(jnp.minimum(g_sq - _t(g_sq), 0.0)), 0.0)
    g_last = jnp.sum(jnp.where(_iota(g.shape, 1) == G_CHUNK - 1, g, 0.0), axis=1, keepdims=True)
    eg = jnp.exp(g)
    kb, vb = k * beta, v * beta
    m = jnp.where(row > col, mm_nt(kb, k) * decay, 0.0)
    known = inv is not None
    if not known:
        inv = _inv_unit_lower(m)
    u, w = _solve2(m, inv, vb, kb * eg)
    qk = jnp.where(row >= col, mm_nt(q, k) * decay, 0.0)
    q_dec = q * eg
    k_dec = k * jnp.exp(g_last - g)
    gl = jnp.broadcast_to(jnp.exp(g_last), (shape[0], 1, G_HEAD_DIM))
    return (u, w, qk, q_dec, k_dec, gl) if known else (u, w, qk, q_dec, k_dec, gl, inv)


def _gdn_step(state, u, w, qk, q_dec, k_dec, gl, gz, norm_w):
    v_new = u - mm_nn(w, state)
    o = mm_nn(q_dec, state) + mm_nn(qk, v_new)
    new_state = state * gl + mm_nn(_t(k_dec), v_new)
    o = o * lax.rsqrt(jnp.mean(o * o, axis=-1, keepdims=True) + RMS_EPS) * norm_w
    return o * _silu(gz), new_state


def _stack_chunks(x, heads):
    chunks = x.shape[0] // G_CHUNK
    parts = []
    for c in range(chunks):
        rows = slice(c * G_CHUNK, (c + 1) * G_CHUNK)
        for hd in range(G_HEADS):
            parts.append(x[rows, hd * LANES:(hd + 1) * LANES] if heads else x[rows, :])
    return jnp.stack(parts)


def _gdn_chunk_shapes(nch):
    b = nch * G_HEADS
    wide = jax.ShapeDtypeStruct((b, G_CHUNK, G_HEAD_DIM), F32)
    return [wide, wide, jax.ShapeDtypeStruct((b, G_CHUNK, G_CHUNK), F32), wide, wide,
            jax.ShapeDtypeStruct((b, 1, G_HEAD_DIM), F32)]


def _gdn_chunk_specs(nbatch):
    wide = pl.BlockSpec((nbatch, G_CHUNK, G_HEAD_DIM), lambda i: (i, 0, 0))
    return [wide, wide, pl.BlockSpec((nbatch, G_CHUNK, G_CHUNK), lambda i: (i, 0, 0)), wide, wide,
            pl.BlockSpec((nbatch, 1, G_HEAD_DIM), lambda i: (i, 0, 0))]


def _gdn_chunk_fwd(conv, proj_g, a_log_row, dt_row, *, name):
    s_len = conv.shape[0]
    cpg = min(GP_CHUNKS, s_len // G_CHUNK)
    rows = cpg * G_CHUNK
    nbatch = cpg * G_HEADS

    def body(c_ref, bg_ref, al_ref, dt_ref, *outs):
        bg = bg_ref[...]
        gc = _seg_cumsum(_gdn_decay(bg, al_ref[...], dt_ref[...]), False)
        res = _gdn_chunk(_stack_chunks(c_ref[:, 0:G_WIDTH], True), _stack_chunks(c_ref[:, G_WIDTH:2 * G_WIDTH], True),
                         _stack_chunks(c_ref[:, 2 * G_WIDTH:], True), _stack_chunks(bg, False), _stack_chunks(gc, False))
        for ref, val in zip(outs, res):
            ref[...] = val

    row = pl.BlockSpec((1, LANES), lambda i: (0, 0))
    return _call(
        body, name=name, grid=(s_len // rows,),
        in_specs=[pl.BlockSpec((rows, 3 * G_WIDTH), lambda i: (i, 0)),
                  pl.BlockSpec((rows, LANES), lambda i: (i, (3 * G_WIDTH + G_WIDTH) // LANES)), row, row],
        out_specs=_gdn_chunk_specs(nbatch) + [pl.BlockSpec((nbatch, G_CHUNK, G_CHUNK), lambda i: (i, 0, 0))],
        out_shape=_gdn_chunk_shapes(s_len // G_CHUNK) + [
            jax.ShapeDtypeStruct((s_len // G_CHUNK * G_HEADS, G_CHUNK, G_CHUNK), F32)],
        compiler_params=_params("parallel"),
    )(conv, proj_g, a_log_row, dt_row)


def _gdn_chunk_bwd(conv, proj_g, a_log_row, dt_row, inv, cots, *, name):
    s_len = conv.shape[0]
    cpg = min(GP_CHUNKS, s_len // G_CHUNK)
    rows = cpg * G_CHUNK
    nbatch = cpg * G_HEADS

    def unstack(x, heads):
        if heads:
            return jnp.concatenate([jnp.concatenate([x[c * G_HEADS + hd] for hd in range(G_HEADS)], axis=1)
                                    for c in range(cpg)], axis=0)
        return jnp.concatenate([sum(x[c * G_HEADS + hd] for hd in range(G_HEADS)) for c in range(cpg)], axis=0)

    def body(c_ref, bg_ref, al_ref, dt_ref, inv_ref, du, dw, dqk, dqd, dkd, dgl, dc_ref, dbg_ref, dal_ref, ddt_ref):
        @pl.when(pl.program_id(0) == 0)
        def _():
            dal_ref[...] = jnp.zeros_like(dal_ref)
            ddt_ref[...] = jnp.zeros_like(ddt_ref)

        bg = bg_ref[...]
        g_all, decay_vjp = jax.vjp(_gdn_decay, bg, al_ref[...], dt_ref[...])
        gc = _seg_cumsum(g_all, False)
        _, vjp = jax.vjp(_gdn_chunk, _stack_chunks(c_ref[:, 0:G_WIDTH], True),
                         _stack_chunks(c_ref[:, G_WIDTH:2 * G_WIDTH], True), _stack_chunks(c_ref[:, 2 * G_WIDTH:], True),
                         _stack_chunks(bg, False), _stack_chunks(gc, False), inv_ref[...])
        dq, dk, dv, dgb, dgc, _ = vjp((du[...], dw[...], dqk[...], dqd[...], dkd[...], dgl[...]))
        dc_ref[:, 0:G_WIDTH] = unstack(dq, True)
        dc_ref[:, G_WIDTH:2 * G_WIDTH] = unstack(dk, True)
        dc_ref[:, 2 * G_WIDTH:] = unstack(dv, True)
        dga, dal, ddt = decay_vjp(_seg_cumsum(unstack(dgc, False), True))
        dbg_ref[:, 0:LANES] = unstack(dgb, False) + dga
        dbg_ref[:, LANES:] = jnp.zeros((rows, LANES), F32)
        dal_ref[...] += dal
        ddt_ref[...] += ddt

    row = pl.BlockSpec((1, LANES), lambda i: (0, 0))
    return _call(
        body, name=name, grid=(s_len // rows,),
        in_specs=[pl.BlockSpec((rows, 3 * G_WIDTH), lambda i: (i, 0)),
                  pl.BlockSpec((rows, LANES), lambda i: (i, (3 * G_WIDTH + G_WIDTH) // LANES)), row, row,
                  pl.BlockSpec((nbatch, G_CHUNK, G_CHUNK), lambda i: (i, 0, 0))]
        + _gdn_chunk_specs(nbatch),
        out_specs=[pl.BlockSpec((rows, 3 * G_WIDTH), lambda i: (i, 0)), pl.BlockSpec((rows, 2 * LANES), lambda i: (i, 0)),
                   row, row],
        out_shape=[jax.ShapeDtypeStruct((s_len, 3 * G_WIDTH), F32), jax.ShapeDtypeStruct((s_len, 2 * LANES), F32),
                   jax.ShapeDtypeStruct((1, LANES), F32), jax.ShapeDtypeStruct((1, LANES), F32)],
        compiler_params=_params("arbitrary"),
    )(conv, proj_g, a_log_row, dt_row, inv, *cots)


def _gdn_scan_specs(cpg, which):
    nbatch = cpg * G_HEADS
    wide = pl.BlockSpec((nbatch, G_CHUNK, G_HEAD_DIM), lambda i: (which(i), 0, 0))
    return [wide, wide, pl.BlockSpec((nbatch, G_CHUNK, G_CHUNK), lambda i: (which(i), 0, 0)), wide, wide,
            pl.BlockSpec((nbatch, 1, G_HEAD_DIM), lambda i: (which(i), 0, 0))]


def _gz_stack(z_ref, c):
    rows = pl.ds(pl.multiple_of(c * G_CHUNK, G_CHUNK), G_CHUNK)
    return jnp.stack([z_ref[rows, hd * LANES:(hd + 1) * LANES] for hd in range(G_HEADS)])


def _gdn_scan_fwd(chunk_vals, proj_g, norm_w, *, name):
    s_len = proj_g.shape[0]
    nch = s_len // G_CHUNK
    cpg = min(GS_CHUNKS, nch)
    rows = cpg * G_CHUNK

    def body(u_ref, w_ref, qk_ref, qd_ref, kd_ref, gl_ref, z_ref, nw_ref, y_ref, st_ref, state):
        @pl.when(pl.program_id(0) == 0)
        def _():
            state[...] = jnp.zeros_like(state)

        def step(c, carry):
            b = pl.ds(pl.multiple_of(c * G_HEADS, G_HEADS), G_HEADS)
            st = state[...]
            st_ref[b] = st
            y, new_state = _gdn_step(st, u_ref[b], w_ref[b], qk_ref[b], qd_ref[b], kd_ref[b], gl_ref[b],
                                     _gz_stack(z_ref, c), nw_ref[...])
            state[...] = new_state
            rws = pl.ds(pl.multiple_of(c * G_CHUNK, G_CHUNK), G_CHUNK)
            for hd in range(G_HEADS):
                y_ref[rws, hd * LANES:(hd + 1) * LANES] = y[hd].astype(y_ref.dtype)
            return carry

        lax.fori_loop(0, cpg, step, 0, unroll=2)

    return _call(
        body, name=name, grid=(nch // cpg,),
        in_specs=_gdn_scan_specs(cpg, lambda i: i) + [
            pl.BlockSpec((rows, G_WIDTH), lambda i: (i, 3)), pl.BlockSpec((1, G_HEAD_DIM), lambda i: (0, 0))],
        out_specs=[pl.BlockSpec((rows, G_WIDTH), lambda i: (i, 0)),
                   pl.BlockSpec((cpg * G_HEADS, G_HEAD_DIM, G_HEAD_DIM), lambda i: (i, 0, 0))],
        out_shape=[jax.ShapeDtypeStruct((s_len, G_WIDTH), MXU_DTYPE),
                   jax.ShapeDtypeStruct((nch * G_HEADS, G_HEAD_DIM, G_HEAD_DIM), F32)],
        scratch_shapes=[pltpu.VMEM((G_HEADS, G_HEAD_DIM, G_HEAD_DIM), F32)],
        compiler_params=_params("arbitrary"),
    )(*chunk_vals, proj_g, norm_w)


def _gdn_scan_bwd(chunk_vals, states, proj_g, norm_w, dyg, *, name):
    s_len = proj_g.shape[0]
    nch = s_len // G_CHUNK
    cpg = min(GS_CHUNKS, nch)
    rows = cpg * G_CHUNK
    ngrid = nch // cpg

    def cur(i):
        return ngrid - 1 - i

    def body(u_ref, w_ref, qk_ref, qd_ref, kd_ref, gl_ref, st_ref, z_ref, nw_ref, dy_ref,
             du_ref, dw_ref, dqk_ref, dqd_ref, dkd_ref, dgl_ref, dz_ref, dnw_ref, dstate):
        @pl.when(pl.program_id(0) == 0)
        def _():
            dstate[...] = jnp.zeros_like(dstate)
            dnw_ref[...] = jnp.zeros_like(dnw_ref)

        def step(k, carry):
            c = cpg - 1 - k
            b = pl.ds(pl.multiple_of(c * G_HEADS, G_HEADS), G_HEADS)
            _, vjp = jax.vjp(_gdn_step, st_ref[b], u_ref[b], w_ref[b], qk_ref[b], qd_ref[b], kd_ref[b], gl_ref[b],
                             _gz_stack(z_ref, c), nw_ref[...])
            dst, du, dw, dqk, dqd, dkd, dgl, dz, dnw = vjp((_gz_stack(dy_ref, c), dstate[...]))
            dstate[...] = dst
            du_ref[b], dw_ref[b], dqk_ref[b], dqd_ref[b], dkd_ref[b], dgl_ref[b] = du, dw, dqk, dqd, dkd, dgl
            rws = pl.ds(pl.multiple_of(c * G_CHUNK, G_CHUNK), G_CHUNK)
            for hd in range(G_HEADS):
                dz_ref[rws, hd * LANES:(hd + 1) * LANES] = dz[hd]
            dnw_ref[...] += dnw
            return carry

        lax.fori_loop(0, cpg, step, 0, unroll=2)

    gate = pl.BlockSpec((rows, G_WIDTH), lambda i: (cur(i), 3))
    wide = pl.BlockSpec((rows, G_WIDTH), lambda i: (cur(i), 0))
    vec = pl.BlockSpec((1, G_HEAD_DIM), lambda i: (0, 0))
    return _call(
        body, name=name, grid=(ngrid,),
        in_specs=_gdn_scan_specs(cpg, cur) + [
            pl.BlockSpec((cpg * G_HEADS, G_HEAD_DIM, G_HEAD_DIM), lambda i: (cur(i), 0, 0)), gate, vec, wide],
        out_specs=_gdn_scan_specs(cpg, cur) + [wide, vec],
        out_shape=_gdn_chunk_shapes(nch) + [jax.ShapeDtypeStruct((s_len, G_WIDTH), F32),
                                            jax.ShapeDtypeStruct((1, G_HEAD_DIM), F32)],
        scratch_shapes=[pltpu.VMEM((G_HEADS, G_HEAD_DIM, G_HEAD_DIM), F32)],
        compiler_params=_params("arbitrary"),
    )(*chunk_vals, states, proj_g, norm_w, dyg)


def _adamw_math(w, g, m, v):
    m = ADAM_B1 * m + (1.0 - ADAM_B1) * g
    v = ADAM_B2 * v + (1.0 - ADAM_B2) * (g * g)
    m_hat = m / (1.0 - ADAM_B1 ** ADAM_STEP)
    v_hat = v / (1.0 - ADAM_B2 ** ADAM_STEP)
    delta = -ADAM_LR * (m_hat / (jnp.sqrt(v_hat) + ADAM_EPS) + ADAM_WD * w)
    return delta, m, v


def _sum_adamw(own, chip, parts, w, m, v, *, name, rows):
    n_layers, n_rows, n_cols = w.shape
    rows = min(rows, n_rows)
    n_parts = parts[0].shape[0]

    def body(c_ref, *refs):
        own_refs, part_refs = refs[:n_layers], refs[n_layers:2 * n_layers]
        w_ref, m_ref, v_ref, g_ref, d_ref, nm_ref, nv_ref = refs[2 * n_layers:]
        layer = pl.program_id(0)
        g = None
        for l in range(n_layers):
            g_l = own_refs[l][0].astype(F32)
            for k in range(n_parts):
                g_l = g_l + part_refs[l][k].astype(F32)
            g = g_l if g is None else jnp.where(layer == l, g_l, g)
        delta, new_m, new_v = _adamw_math(w_ref[0], g, m_ref[0], v_ref[0])
        g_ref[0], d_ref[0], nm_ref[0], nv_ref[0] = g, delta, new_m, new_v

    blk = pl.BlockSpec((1, rows, n_cols), lambda l, i, c: (l, i, 0))
    grid_spec = pltpu.PrefetchScalarGridSpec(
        num_scalar_prefetch=1, grid=(n_layers, n_rows // rows),
        in_specs=[pl.BlockSpec((1, rows, n_cols), lambda l, i, c: (c[0], i, 0))] * n_layers
        + [pl.BlockSpec((n_parts, rows, n_cols), lambda l, i, c: (0, i, 0))] * n_layers + [blk, blk, blk],
        out_specs=[blk] * 4)
    return _call(
        body, name=name, grid_spec=grid_spec, out_shape=[jax.ShapeDtypeStruct(w.shape, F32)] * 4,
        compiler_params=_params("parallel", "parallel"),
    )(_index_operand(chip), *own, *parts, w, m, v)


def _sum_slots(parts, *, name):
    rows = parts.shape[1]

    def body(p_ref, o_ref):
        g = p_ref[0]
        for k in range(1, N_DEV):
            g = g + p_ref[k]
        o_ref[...] = g

    return _call(body, name=name, grid=(1,),
                 in_specs=[pl.BlockSpec(parts.shape, lambda i: (0, 0, 0))],
                 out_specs=pl.BlockSpec((rows, LANES), lambda i: (0, 0)),
                 out_shape=jax.ShapeDtypeStruct((rows, LANES), F32), compiler_params=_params("arbitrary"))(parts)


def _adamw_packed(w, g, m, v, *, name):
    def body(w_ref, g_ref, m_ref, v_ref, d_ref, nm_ref, nv_ref):
        d_ref[...], nm_ref[...], nv_ref[...] = _adamw_math(w_ref[...], g_ref[...], m_ref[...], v_ref[...])

    blk = pl.BlockSpec(w.shape, lambda i: (0, 0))
    return _call(body, name=name, grid=(1,), in_specs=[blk] * 4, out_specs=[blk] * 3,
                 out_shape=[jax.ShapeDtypeStruct(w.shape, F32)] * 3, compiler_params=_params("arbitrary"))(w, g, m, v)


A_COLS = ((0, 512), (768, 1280), (512, 768))
R_COLS = ((1280, 3328),)
G_COLS = ((3328, 5384),)


def _group_weights(wt_full):
    def take(ranges):
        return jnp.concatenate([wt_full[a:b] for a, b in ranges], axis=0)

    wt_g = jnp.concatenate([take(G_COLS), jnp.zeros((G_PAD, wt_full.shape[1]), wt_full.dtype)], axis=0)
    return take(A_COLS), take(R_COLS), wt_g


def _ungroup_grads(d_a, d_r, d_g):
    return jnp.concatenate([d_a[0:512], d_a[1024:1280], d_a[512:1024], d_r, d_g[:WG - G_PAD]], axis=0)


def _shard_rows(w):
    return jnp.pad(jnp.transpose(w, (0, 2, 1)), ((0, 0), (0, N_ROWS_PAD - N_IN_SHARD), (0, 0)))


def _unshard_rows(wt):
    return jnp.transpose(wt[:, :N_IN_SHARD], (0, 2, 1))


def _owner_blocks(dwt):
    blocks = jnp.pad(dwt.reshape(4, 2, N_IN_SHARD, D_MODEL), ((0, 0), (0, 0), (0, N_ROWS_PAD - N_IN_SHARD), (0, 0)))
    return jnp.transpose(blocks, (1, 0, 2, 3))


def _rope_tables(s_len):
    inv = 1.0 / (ROPE_THETA ** (jnp.arange(0, A_HEAD_DIM, 2, dtype=F32) / A_HEAD_DIM))
    ang = jnp.arange(s_len, dtype=F32)[:, None] * inv[None, :]
    cos, sin = jnp.cos(ang), jnp.sin(ang)
    return jnp.tile(cos, (1, 4)), jnp.tile(jnp.concatenate([-sin, sin], axis=1), (1, 2))


def _pack(leaves):
    rows = []
    for leaf in leaves:
        flat = leaf.reshape(-1)
        pad = (-flat.shape[0]) % (8 * LANES)
        rows.append(jnp.pad(flat, (0, pad)).reshape(-1, LANES))
    return jnp.concatenate(rows, axis=0)


def _unpack(packed, shapes):
    out, row = [], 0
    for shape in shapes:
        size = math.prod(shape)
        nrows = -(-size // (8 * LANES)) * 8
        out.append(packed[row:row + nrows].reshape(-1)[:size].reshape(shape))
        row += nrows
    return out


def _lane_row(vals, offset):
    return jnp.pad(vals, (offset, LANES - offset - vals.shape[0])).reshape(1, LANES)


def kernel(x, w_in, sinks, r_conv_w, r_conv_b, r_wa, r_ba, r_wx, r_bx, r_lam, g_conv_w, g_a_log, g_dt_bias, g_norm_w, w_out, ln_g, ln_b, loss_target, m_w_in, m_sinks, m_r_conv_w, m_r_conv_b, m_r_wa, m_r_ba, m_r_wx, m_r_bx, m_r_lam, m_g_conv_w, m_g_a_log, m_g_dt_bias, m_g_norm_w, m_w_out, m_ln_g, m_ln_b, v_w_in, v_sinks, v_r_conv_w, v_r_conv_b, v_r_wa, v_r_ba, v_r_wx, v_r_bx, v_r_lam, v_g_conv_w, v_g_a_log, v_g_dt_bias, v_g_norm_w, v_w_out, v_ln_g, v_ln_b):
    s_len = x.shape[1]
    x0 = x.reshape(s_len, D_MODEL)
    target = loss_target.reshape(s_len, D_MODEL)
    me = 4 * lax.axis_index("x") + 2 * lax.axis_index("y") + lax.axis_index("c")
    core, chip = lax.axis_index("c"), 2 * lax.axis_index("x") + lax.axis_index("y")

    win_pieces = _shard_rows(w_in).astype(MXU_DTYPE).reshape(DEPTH, 2, N_ROWS_PAD // 2, D_MODEL)
    wout_pieces = w_out.astype(MXU_DTYPE).reshape(DEPTH, 2, OUT_SHARD // 2, D_MODEL)
    win0_all, wout0_all, rcw_all, gcw_all = _all_gather(
        [win_pieces[0], wout_pieces[0], r_conv_w[None], g_conv_w[None]], "gather_weights")
    rcw_full = jnp.moveaxis(rcw_all[:, 0], 0, 2).reshape(DEPTH, CONV_WIDTH, R_WIDTH)
    gcw_full = jnp.moveaxis(gcw_all[:, 0], 0, 2).reshape(DEPTH, CONV_WIDTH, 3 * G_WIDTH)
    cos, sin = _rope_tables(s_len)

    def big_weights(win_all, wout_all):
        wt_a, wt_r, wt_g = _group_weights(win_all.reshape(N_DEV, N_ROWS_PAD, D_MODEL)[:, :N_IN_SHARD].reshape(N_IN, D_MODEL))
        wo = wout_all.reshape(D_MODEL, D_MODEL)
        return dict(wt_a=wt_a, wt_r=wt_r, wt_g=wt_g, wo=wo,
                    wo_a=wo[0:A_WIDTH], wo_r=wo[A_WIDTH:A_WIDTH + R_WIDTH], wo_g=wo[A_WIDTH + R_WIDTH:])

    layers = []
    for l in range(DEPTH):
        layers.append(dict(
            sinks_t=jnp.broadcast_to(sinks[l][:, None, None], (A_HEADS, 8, LANES)),
            rcw=rcw_full[l], rcb=r_conv_b[l].reshape(1, R_WIDTH), wa=r_wa[l], ba=r_ba[l].reshape(1, R_WIDTH),
            wx=r_wx[l], bx=r_bx[l].reshape(1, R_WIDTH), lam=r_lam[l].reshape(1, R_WIDTH),
            gcw=gcw_full[l], zero_b=jnp.zeros((1, 3 * G_WIDTH), F32),
            a_log=_lane_row(g_a_log[l], G_HEADS), dt=_lane_row(g_dt_bias[l], G_HEADS),
            norm_w=g_norm_w[l].reshape(1, G_HEAD_DIM), ln_g=ln_g[l].reshape(1, D_MODEL), ln_b=ln_b[l].reshape(1, D_MODEL)))

    saved = []
    xin = xin_lo = x0
    layers[0].update(big_weights(win0_all, wout0_all))
    for l, p in enumerate(layers):
        if l + 1 < DEPTH:
            proj_a, (wout_next,) = _matmul([xin_lo], [p["wt_a"]], name=f"proj_a{l}", tm=1024, tn=640, b_t=True,
                                           comm=_GatherSend([wout_pieces[l + 1]]))
            proj_r, (win_next,) = _matmul([xin_lo], [p["wt_r"]], name=f"proj_r{l}", tm=1024, tn=512, b_t=True,
                                          comm=_GatherSend([win_pieces[l + 1]]))
            proj_g, next_all = _matmul([xin_lo], [p["wt_g"]], name=f"proj_g{l}", tm=1024, tn=768, b_t=True,
                                       comm=_GatherForward([win_next, wout_next]))
            layers[l + 1].update(big_weights(*next_all))
        else:
            proj_a = _matmul([xin_lo], [p["wt_a"]], name=f"proj_a{l}", tm=1024, tn=640, b_t=True)
            proj_r = _matmul([xin_lo], [p["wt_r"]], name=f"proj_r{l}", tm=1024, tn=512, b_t=True)
            proj_g = _matmul([xin_lo], [p["wt_g"]], name=f"proj_g{l}", tm=1024, tn=768, b_t=True)
        ya = _attn_fwd(proj_a, cos, sin, p["sinks_t"], name=f"attn_fwd{l}")
        xr = _conv_fwd(proj_r, R_WIDTH, p["rcw"], p["rcb"], name=f"rconv_fwd{l}")
        h, yr = _rg_fwd(xr, proj_r, p["wa"], p["ba"], p["wx"], p["bx"], p["lam"], name=f"rglru_fwd{l}")
        conv = _conv_fwd(proj_g, 3 * G_WIDTH, p["gcw"], p["zero_b"], name=f"gconv_fwd{l}")
        *chunk_vals, inv = _gdn_chunk_fwd(conv, proj_g, p["a_log"], p["dt"], name=f"gdn_chunk_fwd{l}")
        yg, states = _gdn_scan_fwd(chunk_vals, proj_g, p["norm_w"], name=f"gdn_scan_fwd{l}")
        z, xout, xout_lo = _outproj_ln(ya, yr, yg, p["wo"], xin, p["ln_g"], p["ln_b"], name=f"outproj_ln{l}")
        saved.append(dict(xin_lo=xin_lo, proj_a=proj_a, proj_r=proj_r, proj_g=proj_g, ya=ya, yr=yr, yg=yg, xr=xr, h=h,
                          conv=conv, chunk_vals=chunk_vals, inv=inv, states=states, z=z))
        xin, xin_lo = xout, xout_lo

    grads = [None] * DEPTH
    dxn = None
    loss_local = None
    for l in reversed(range(DEPTH)):
        p, sv = layers[l], saved[l]
        if dxn is None:
            dz, dz_lo, dln_g, dln_b, loss_local = _ln_bwd(sv["z"], p["ln_g"], name=f"ln_bwd{l}", xn=xin, target=target)
        else:
            dz, dz_lo, dln_g, dln_b = _ln_bwd(sv["z"], p["ln_g"], name=f"ln_bwd{l}", dxn=dxn)
        dya = _matmul([dz_lo], [p["wo_a"]], name=f"dya{l}", tm=1024, tn=512, b_t=True)
        dyr = _matmul([dz_lo], [p["wo_r"]], name=f"dyr{l}", tm=1024, tn=512, b_t=True)
        dyg = _matmul([dz_lo], [p["wo_g"]], name=f"dyg{l}", tm=1024, tn=512, b_t=True)
        dwo = jnp.concatenate([
            _matmul_tn(sv["ya"], dz_lo, name=f"dwo_a{l}", tm=512, tn=1024, tk=1024),
            _matmul_tn(sv["yr"], dz_lo, name=f"dwo_r{l}", tm=1024, tn=1024, tk=1024),
            _matmul_tn(sv["yg"], dz_lo, name=f"dwo_g{l}", tm=512, tn=1024, tk=1024)], axis=0)

        dproj_a, dsinks_t = _attn_bwd(sv["proj_a"], cos, sin, p["sinks_t"], dya, name=f"attn_bwd{l}")

        dxr, drz, dwa, dba, dwx, dbx, dlam = _rg_bwd(sv["xr"], sv["proj_r"], sv["h"], dyr, p["wa"], p["ba"], p["wx"],
                                                     p["bx"], p["lam"], name=f"rglru_bwd{l}")
        dproj_r, drcw, drcb = _conv_bwd(dxr, sv["proj_r"], R_WIDTH, p["rcw"], [drz], name=f"rconv_bwd{l}")

        scan_out = _gdn_scan_bwd(sv["chunk_vals"], sv["states"], sv["proj_g"], p["norm_w"], dyg, name=f"gdn_scan_bwd{l}")
        dgz, dnorm_w = scan_out[6], scan_out[7]
        dconv, dbg, dal, ddt = _gdn_chunk_bwd(sv["conv"], sv["proj_g"], p["a_log"], p["dt"], sv["inv"], scan_out[:6],
                                              name=f"gdn_chunk_bwd{l}")
        dproj_g, dgcw, _ = _conv_bwd(dconv, sv["proj_g"], 3 * G_WIDTH, p["gcw"], [dgz, dbg], name=f"gconv_bwd{l}")

        dwin = _ungroup_grads(_matmul_tn(dproj_a, sv["xin_lo"], name=f"dwin_a{l}", tm=640, tn=1024, tk=1024),
                              _matmul_tn(dproj_r, sv["xin_lo"], name=f"dwin_r{l}", tm=1024, tn=1024, tk=1024),
                              _matmul_tn(dproj_g, sv["xin_lo"], name=f"dwin_g{l}", tm=1152, tn=1024, tk=1024))

        dwin_blocks = _owner_blocks(dwin)[:, :, None].astype(MXU_DTYPE)
        dwout_blocks = jnp.transpose(dwo.reshape(4, 2, OUT_SHARD, D_MODEL), (1, 0, 2, 3))[:, :, None].astype(MXU_DTYPE)
        got_win, got_wout = _swap_cores(
            [dwin_blocks.reshape(2, 8, N_ROWS_PAD // 2, D_MODEL), dwout_blocks.reshape(2, 4, OUT_SHARD, D_MODEL)],
            f"swap_core_grads{l}")
        chip_win = _add_pair(dwin_blocks, got_win.reshape(dwin_blocks.shape[1:]), core, name=f"add_core_grads_w_in{l}",
                             rows=352).reshape(4, N_ROWS_PAD, D_MODEL)
        chip_wout = _add_pair(dwout_blocks, got_wout.reshape(dwout_blocks.shape[1:]), core, name=f"add_core_grads_w_out{l}",
                              rows=256).reshape(4, OUT_SHARD, D_MODEL)
        dxn, (win_parts, wout_parts) = _matmul(
            [dproj_a, dproj_r, dproj_g], [p["wt_a"], p["wt_r"], p["wt_g"]], name=f"dx{l}", tm=512, tn=512, add=dz,
            add_scale=DEEPNORM_ALPHA, comm=_ChipExchange([chip_win, chip_wout]))
        grads[l] = dict(
            chip_win=chip_win, chip_wout=chip_wout, win_parts=win_parts, wout_parts=wout_parts,
            sinks=dsinks_t[:, :, 0].sum(axis=1), r_conv_w=drcw.reshape(CONV_WIDTH, R_WIDTH),
            r_conv_b=drcb.reshape(R_WIDTH), r_wa=dwa, r_ba=dba.reshape(R_WIDTH), r_wx=dwx, r_bx=dbx.reshape(R_WIDTH),
            r_lam=dlam.reshape(R_WIDTH), g_conv_w=dgcw.reshape(CONV_WIDTH, 3 * G_WIDTH),
            g_a_log=dal[0, G_HEADS:2 * G_HEADS], g_dt_bias=ddt[0, G_HEADS:2 * G_HEADS],
            g_norm_w=dnorm_w.reshape(G_HEAD_DIM), ln_g=dln_g.reshape(D_MODEL), ln_b=dln_b.reshape(D_MODEL))
    grad_x = dxn.reshape(x.shape)
    loss = lax.psum(loss_local[0, 0], ("x", "y", "c"))

    def stacked(name):
        return jnp.stack([grads[l][name] for l in range(DEPTH)])

    def per_layer(name):
        return [grads[l][name] for l in range(DEPTH)]

    w_in_t = [_unshard_rows(t) for t in _sum_adamw(per_layer("chip_win"), chip, per_layer("win_parts"), _shard_rows(w_in),
                                                   _shard_rows(m_w_in), _shard_rows(v_w_in), name="adamw_w_in", rows=176)]
    g_w_in, d_w_in, nm_w_in, nv_w_in = w_in_t
    g_w_out, d_w_out, nm_w_out, nv_w_out = _sum_adamw(per_layer("chip_wout"), chip, per_layer("wout_parts"), w_out,
                                                      m_w_out, v_w_out, name="adamw_w_out", rows=128)

    small = ["sinks", "r_conv_w", "r_conv_b", "r_wa", "r_ba", "r_wx", "r_bx", "r_lam", "g_conv_w", "g_a_log",
             "g_dt_bias", "g_norm_w", "ln_g", "ln_b"]
    full_shapes = [stacked(nm).shape for nm in small]
    packed_small = _pack([stacked(nm) for nm in small])
    (all_small,) = _all_gather([packed_small.reshape(4, packed_small.shape[0] // 4, LANES)], "gather_small_grads")
    all_small = all_small.reshape(N_DEV, packed_small.shape[0], LANES)
    g_small = dict(zip(small, _unpack(_sum_slots(all_small, name="sum_small_grads"), full_shapes)))
    g_small["r_conv_w"] = lax.dynamic_slice_in_dim(g_small["r_conv_w"], me * (R_WIDTH // N_DEV), R_WIDTH // N_DEV, axis=2)
    g_small["g_conv_w"] = lax.dynamic_slice_in_dim(g_small["g_conv_w"], me * (3 * G_WIDTH // N_DEV), 3 * G_WIDTH // N_DEV, axis=2)
    given = dict(sinks=(sinks, m_sinks, v_sinks), r_conv_w=(r_conv_w, m_r_conv_w, v_r_conv_w),
                 r_conv_b=(r_conv_b, m_r_conv_b, v_r_conv_b), r_wa=(r_wa, m_r_wa, v_r_wa), r_ba=(r_ba, m_r_ba, v_r_ba),
                 r_wx=(r_wx, m_r_wx, v_r_wx), r_bx=(r_bx, m_r_bx, v_r_bx), r_lam=(r_lam, m_r_lam, v_r_lam),
                 g_conv_w=(g_conv_w, m_g_conv_w, v_g_conv_w), g_a_log=(g_a_log, m_g_a_log, v_g_a_log),
                 g_dt_bias=(g_dt_bias, m_g_dt_bias, v_g_dt_bias), g_norm_w=(g_norm_w, m_g_norm_w, v_g_norm_w),
                 ln_g=(ln_g, m_ln_g, v_ln_g), ln_b=(ln_b, m_ln_b, v_ln_b))
    shard_shapes = [given[nm][0].shape for nm in small]
    packed = [_pack([given[nm][k] for nm in small]) for k in range(3)]
    d_p, nm_p, nv_p = _adamw_packed(packed[0], _pack([g_small[nm] for nm in small]), packed[1], packed[2], name="adamw_small")
    d_small = dict(zip(small, _unpack(d_p, shard_shapes)))
    nm_small = dict(zip(small, _unpack(nm_p, shard_shapes)))
    nv_small = dict(zip(small, _unpack(nv_p, shard_shapes)))

    order = ["w_in"] + small[:12] + ["w_out"] + small[12:]

    def leaf(big_in, big_out, table):
        return [big_in if nm == "w_in" else big_out if nm == "w_out" else table[nm] for nm in order]

    return (loss, grad_x, *leaf(g_w_in, g_w_out, g_small), *leaf(d_w_in, d_w_out, d_small),
            *leaf(nm_w_in, nm_w_out, nm_small), *leaf(nv_w_in, nv_w_out, nv_small))
```

```python
import functools
import math

import jax
import jax.numpy as jnp
from jax import lax
from jax.experimental import pallas as pl
from jax.experimental.pallas import tpu as pltpu

F32 = jnp.float32
MXU_DTYPE = jnp.bfloat16
HIGHEST = lax.Precision.HIGHEST
MESH_ID = pl.DeviceIdType.MESH

N_DEV = 8
DEPTH = 2
D_MODEL = 2048
A_HEADS, A_KV_HEADS, A_HEAD_DIM = 8, 2, 64
A_WIDTH, A_KV_WIDTH = 512, 128
A_BLOCK = 128
ROPE_THETA = 10000.0
R_WIDTH, R_BLOCKS, R_BLOCK_DIM = 1024, 8, 128
R_C = 8.0
CONV_WIDTH = 4
G_HEADS, G_HEAD_DIM, G_WIDTH, G_CHUNK = 4, 128, 512, 64
N_IN = 5384
N_IN_SHARD = N_IN // N_DEV
N_ROWS_PAD = 704
OUT_SHARD = D_MODEL // N_DEV
WA, WR, WG = 1280, 2048, 2304
G_PAD = WG - (3 * G_WIDTH + G_WIDTH + 2 * G_HEADS)
DEEPNORM_ALPHA = (2 * DEPTH) ** 0.25
LN_EPS = 1e-5
RMS_EPS = 1e-6
ADAM_LR, ADAM_B1, ADAM_B2, ADAM_EPS, ADAM_WD, ADAM_STEP = 0.001, 0.9, 0.999, 1e-08, 0.01, 10
NEG = -1e30
VMEM_LIMIT = 56 * 1024 * 1024
LANES = 128


def _call(body, **kw):
    return pl.pallas_call(body, **kw)


def _params(*sem):
    return pltpu.CompilerParams(dimension_semantics=sem, vmem_limit_bytes=VMEM_LIMIT)


def _t(x):
    return jnp.swapaxes(x, -1, -2)


def _raw_dot(a, b, ca, cb, precision=None):
    batch = tuple(range(a.ndim - 2))
    if precision is None:
        a, b = a.astype(MXU_DTYPE), b.astype(MXU_DTYPE)
    return lax.dot_general(a, b, (((ca,), (cb,)), (batch, batch)), precision=precision,
                           preferred_element_type=F32)


def _nn(a, b, precision=None):
    return _raw_dot(a, b, a.ndim - 1, b.ndim - 2, precision)


def _nt(a, b, precision=None):
    return _raw_dot(a, b, a.ndim - 1, b.ndim - 1, precision)


@jax.custom_vjp
def mm_nn(a, b):
    return _nn(a, b)


def _mm_nn_fwd(a, b):
    return _nn(a, b), (a, b)


def _mm_nn_bwd(res, g):
    a, b = res
    return _nt(g, b), _nn(_t(a), g)


mm_nn.defvjp(_mm_nn_fwd, _mm_nn_bwd)


@jax.custom_vjp
def mm_nt(a, b):
    return _nt(a, b)


def _mm_nt_fwd(a, b):
    return _nt(a, b), (a, b)


def _mm_nt_bwd(res, g):
    a, b = res
    return _nn(g, b), _nn(_t(g), a)


mm_nt.defvjp(_mm_nt_fwd, _mm_nt_bwd)


def _split(x):
    hi = x.astype(MXU_DTYPE)
    return hi, (x - hi.astype(F32)).astype(MXU_DTYPE)


def _hmm(a, b, nt=False):
    dot = _nt if nt else _nn
    return dot(a[0], b[0]) + (dot(a[0], b[1]) + dot(a[1], b[0]))


def _silu(x):
    return x * jax.nn.sigmoid(x)


def _softplus(x):
    return jnp.maximum(x, 0.0) + jnp.log1p(jnp.exp(-jnp.abs(x)))


def _one_minus_sq(log_a, a):
    x = 2.0 * log_a
    return jnp.where(x > -0.01, -x * (1.0 + 0.5 * x), 1.0 - a * a)


def _iota(shape, dim):
    return lax.broadcasted_iota(jnp.int32, shape, dim)


def _inv_unit_lower(m):
    shape = m.shape
    row, col = _iota(shape, 1), _iota(shape, 2)
    eye = (row == col).astype(F32)

    def blockdiag(size):
        return (row // size) == (col // size)

    x = -jnp.where(blockdiag(8), m, 0.0)
    xs = _split(x)
    x2s = _split(_hmm(xs, xs))
    x4s = _split(_hmm(x2s, x2s))
    inv = eye + x
    inv = inv + _hmm(_split(inv), x2s)
    inv = inv + _hmm(_split(inv), x4s)
    for size in (8, 16, 32):
        below = jnp.where(blockdiag(2 * size) & jnp.logical_not(blockdiag(size)), m, 0.0)
        invs = _split(inv)
        inv = inv - _hmm(_split(_hmm(invs, _split(below))), invs)
    return inv


@jax.custom_vjp
def _solve2(m, inv, r1, r2):
    invs = _split(inv)
    return _hmm(invs, _split(r1)), _hmm(invs, _split(r2))


def _solve2_fwd(m, inv, r1, r2):
    x1, x2 = _solve2(m, inv, r1, r2)
    return (x1, x2), (inv, x1, x2)


def _solve2_bwd(res, g):
    inv, x1, x2 = res
    inv_ts = _split(_t(inv))
    d1, d2 = _hmm(inv_ts, _split(g[0])), _hmm(inv_ts, _split(g[1]))
    dm = -(_hmm(_split(d1), _split(x1), nt=True) + _hmm(_split(d2), _split(x2), nt=True))
    return dm, jnp.zeros_like(inv), d1, d2


_solve2.defvjp(_solve2_fwd, _solve2_bwd)


def _swap_halves(x):
    n = x.shape[-1]
    lane = _iota(x.shape, x.ndim - 1)
    return jnp.where((lane & 63) < 32, pltpu.roll(x, n - 32, x.ndim - 1), pltpu.roll(x, 32, x.ndim - 1))


def _rope(x, cos, sin):
    reps = x.shape[-1] // LANES
    if reps > 1:
        cos, sin = jnp.tile(cos, (1, reps)), jnp.tile(sin, (1, reps))
    return x * cos + _swap_halves(x) * sin


def _rope_t(d, cos, sin):
    reps = d.shape[-1] // LANES
    if reps > 1:
        cos, sin = jnp.tile(cos, (1, reps)), jnp.tile(sin, (1, reps))
    return d * cos + _swap_halves(d * sin)


def _swap64(x):
    return pltpu.roll(x, 64, x.ndim - 1)


def _mesh_pos():
    return lax.axis_index("x"), lax.axis_index("y"), lax.axis_index("c")


def _all_gather(arrays, name):
    n = len(arrays)
    npieces = [a.shape[0] for a in arrays]
    pmax = max(npieces)

    def body(*refs):
        ins, outs = refs[:n], refs[n:2 * n]
        send_sems, recv_sems, local_sem = refs[2 * n:]
        x, y, c = _mesh_pos()
        me, sibling = (x, y, c), (x, y, 1 - c)
        chips = [(1 - x, y), (x, 1 - y), (1 - x, 1 - y)]

        def slot(a, pos, p):
            return outs[a].at[4 * pos[0] + 2 * pos[1] + pos[2], p]

        def copy(a, p, k, block, to, own=False):
            return pltpu.make_async_remote_copy(
                src_ref=ins[a].at[p] if own else slot(a, block, p), dst_ref=slot(a, block, p),
                send_sem=send_sems.at[a, p, k], recv_sem=recv_sems.at[a, p, k], device_id=to, device_id_type=MESH_ID)

        pieces = [(a, p) for p in range(pmax) for a in range(n) if p < npieces[a]]
        mine = [pltpu.make_async_copy(ins[a].at[p], slot(a, me, p), local_sem.at[a, p]) for a, p in pieces]
        for cp in mine:
            cp.start()
        first = []
        for a, p in pieces:
            first += [copy(a, p, 1 + j, me, (*chip, c), own=True) for j, chip in enumerate(chips)]
            first.append(copy(a, p, 0, me, sibling, own=True))
        for cp in first:
            cp.start()
        passed = []
        for a, p in pieces:
            for j, chip in enumerate(chips):
                copy(a, p, 1 + j, (*chip, c), me).wait_recv()
                cp = copy(a, p, 4 + j, (*chip, c), sibling)
                cp.start()
                passed.append(cp)
        for a, p in pieces:
            copy(a, p, 0, sibling, me).wait_recv()
            for j, chip in enumerate(chips):
                copy(a, p, 4 + j, (*chip, 1 - c), me).wait_recv()
        for cp in first + passed:
            cp.wait_send()
        for cp in mine:
            cp.wait()

    any_spec = pl.BlockSpec(memory_space=pl.ANY)
    return _call(
        body, name=name,
        out_shape=[jax.ShapeDtypeStruct((N_DEV,) + a.shape, a.dtype) for a in arrays],
        in_specs=[any_spec] * n, out_specs=[any_spec] * n,
        scratch_shapes=[pltpu.SemaphoreType.DMA((n, pmax, 7)), pltpu.SemaphoreType.DMA((n, pmax, 7)),
                        pltpu.SemaphoreType.DMA((n, pmax))],
    )(*arrays)


def _swap_cores(arrays, name):
    n = len(arrays)
    pmax = max(a.shape[1] for a in arrays)

    def body(*refs):
        ins, got = refs[:n], refs[n:2 * n]
        send_sems, recv_sems = refs[2 * n:]
        x, y, c = _mesh_pos()
        copies = [pltpu.make_async_remote_copy(
            src_ref=ins[a].at[1 - c, p], dst_ref=got[a].at[p], send_sem=send_sems.at[a, p], recv_sem=recv_sems.at[a, p],
            device_id=(x, y, 1 - c), device_id_type=MESH_ID) for a in range(n) for p in range(arrays[a].shape[1])]
        for cp in copies:
            cp.start()
        for cp in copies:
            cp.wait()

    any_spec = pl.BlockSpec(memory_space=pl.ANY)
    return _call(
        body, name=name, out_shape=[jax.ShapeDtypeStruct(a.shape[1:], a.dtype) for a in arrays],
        in_specs=[any_spec] * n, out_specs=[any_spec] * n,
        scratch_shapes=[pltpu.SemaphoreType.DMA((n, pmax)), pltpu.SemaphoreType.DMA((n, pmax))],
    )(*arrays)


class _ChipExchange:
    aliases = {}

    def __init__(self, arrays):
        self.arrays = list(arrays)
        n = len(self.arrays)
        self.out_shape = [jax.ShapeDtypeStruct((3,) + a.shape[1:], a.dtype) for a in self.arrays]
        self.scratch = [pltpu.SemaphoreType.DMA((n, 3)), pltpu.SemaphoreType.DMA((n, 3))]

    def _copies(self, ins, outs, send_sems, recv_sems):
        x, y, c = _mesh_pos()
        copies = []
        for a in range(len(self.arrays)):
            for k in range(1, 4):
                px, py = x ^ (k >> 1), y ^ (k & 1)
                copies.append(pltpu.make_async_remote_copy(
                    src_ref=ins[a].at[2 * px + py], dst_ref=outs[a].at[k - 1], send_sem=send_sems.at[a, k - 1],
                    recv_sem=recv_sems.at[a, k - 1], device_id=(px, py, c), device_id_type=MESH_ID))
        return copies

    def start(self, ins, outs, send_sems, recv_sems):
        for cp in self._copies(ins, outs, send_sems, recv_sems):
            cp.start()

    def finish(self, ins, outs, send_sems, recv_sems):
        copies = self._copies(ins, outs, send_sems, recv_sems)
        for cp in copies:
            cp.wait_recv()
        for cp in copies:
            cp.wait_send()


def _slot(pos):
    return 4 * pos[0] + 2 * pos[1] + pos[2]


class _GatherSend:
    aliases = {}

    def __init__(self, arrays):
        self.arrays = list(arrays)
        n, pmax = len(self.arrays), max(a.shape[0] for a in self.arrays)
        self.out_shape = [jax.ShapeDtypeStruct((N_DEV,) + a.shape, a.dtype) for a in self.arrays]
        self.scratch = [pltpu.SemaphoreType.DMA((n, pmax, 4)), pltpu.SemaphoreType.DMA((n, pmax, 4)),
                        pltpu.SemaphoreType.DMA((n, pmax))]

    def _copies(self, ins, outs, send_sems, recv_sems, local_sems):
        x, y, c = _mesh_pos()
        peers = [(x, y, 1 - c), (1 - x, y, c), (x, 1 - y, c), (1 - x, 1 - y, c)]
        local, remote = [], []
        for a, arr in enumerate(self.arrays):
            for p in range(arr.shape[0]):
                local.append(pltpu.make_async_copy(ins[a].at[p], outs[a].at[_slot((x, y, c)), p], local_sems.at[a, p]))
                for k, peer in enumerate(peers):
                    remote.append(pltpu.make_async_remote_copy(
                        src_ref=ins[a].at[p], dst_ref=outs[a].at[_slot((x, y, c)), p], send_sem=send_sems.at[a, p, k],
                        recv_sem=recv_sems.at[a, p, k], device_id=peer, device_id_type=MESH_ID))
        return local, remote

    def start(self, *refs):
        local, remote = self._copies(*refs)
        for cp in local + remote:
            cp.start()

    def finish(self, *refs):
        local, remote = self._copies(*refs)
        for cp in remote:
            cp.wait_recv()
        for cp in remote:
            cp.wait_send()
        for cp in local:
            cp.wait()


class _GatherForward:
    def __init__(self, gathered):
        self.arrays = list(gathered)
        n, pmax = len(self.arrays), max(a.shape[1] for a in self.arrays)
        self.out_shape = [jax.ShapeDtypeStruct(a.shape, a.dtype) for a in self.arrays]
        self.aliases = {k: k for k in range(n)}
        self.scratch = [pltpu.SemaphoreType.DMA((n, pmax, 3)), pltpu.SemaphoreType.DMA((n, pmax, 3))]

    def _copies(self, ins, outs, send_sems, recv_sems):
        x, y, c = _mesh_pos()
        copies = []
        for a, arr in enumerate(self.arrays):
            for p in range(arr.shape[1]):
                for j, chip in enumerate([(1 - x, y), (x, 1 - y), (1 - x, 1 - y)]):
                    copies.append(pltpu.make_async_remote_copy(
                        src_ref=ins[a].at[_slot((*chip, c)), p], dst_ref=outs[a].at[_slot((*chip, c)), p],
                        send_sem=send_sems.at[a, p, j], recv_sem=recv_sems.at[a, p, j], device_id=(x, y, 1 - c),
                        device_id_type=MESH_ID))
        return copies

    def start(self, *refs):
        for cp in self._copies(*refs):
            cp.start()

    def finish(self, *refs):
        copies = self._copies(*refs)
        for cp in copies:
            cp.wait_recv()
        for cp in copies:
            cp.wait_send()


def _index_operand(i):
    return jnp.reshape(i, (1,)).astype(jnp.int32)


def _add_pair(pair, other, core, *, name, rows):
    _, n_slots, n_layers, n_rows, n_cols = pair.shape
    rows = min(rows, n_rows)

    def body(c_ref, a_ref, b_ref, o_ref):
        o_ref[...] = (a_ref[0].astype(F32) + b_ref[...].astype(F32)).astype(o_ref.dtype)

    blk = pl.BlockSpec((1, 1, rows, n_cols), lambda s, l, i, c: (s, l, i, 0))
    grid_spec = pltpu.PrefetchScalarGridSpec(
        num_scalar_prefetch=1, grid=(n_slots, n_layers, n_rows // rows),
        in_specs=[pl.BlockSpec((1, 1, 1, rows, n_cols), lambda s, l, i, c: (c[0], s, l, i, 0)), blk], out_specs=blk)
    return _call(body, name=name, grid_spec=grid_spec, out_shape=jax.ShapeDtypeStruct(other.shape, pair.dtype),
                 compiler_params=_params("parallel", "parallel", "parallel"))(_index_operand(core), pair, other)


def _matmul(a_list, b_list, *, name, tm, tn, b_t=False, out_dtype=F32, add=None, add_scale=1.0, comm=None):
    n = len(a_list)
    m_rows, n_cols = a_list[0].shape[0], b_list[0].shape[0 if b_t else 1]
    tm, tn = min(tm, m_rows), min(tn, n_cols)
    grid = (m_rows // tm, n_cols // tn)
    n_in = 2 * n + (add is not None)
    n_comm = len(comm.arrays) if comm is not None else 0

    def body(*refs):
        a_refs, b_refs = refs[:n], refs[n:2 * n]
        o_ref = refs[n_in + n_comm]
        if comm is not None:
            comm_refs = (refs[n_in:n_in + n_comm], refs[n_in + n_comm + 1:n_in + 2 * n_comm + 1],
                         *refs[-len(comm.scratch):])
            i, j = pl.program_id(0), pl.program_id(1)

            @pl.when((i == 0) & (j == 0))
            def _():
                comm.start(*comm_refs)

        acc = None
        for a_ref, b_ref in zip(a_refs, b_refs):
            part = lax.dot_general(a_ref[...].astype(MXU_DTYPE), b_ref[...].astype(MXU_DTYPE),
                                   (((1,), (1 if b_t else 0,)), ((), ())), preferred_element_type=F32)
            acc = part if acc is None else acc + part
        if add is not None:
            acc = acc + add_scale * refs[2 * n][...]
        o_ref[...] = acc.astype(o_ref.dtype)
        if comm is not None:
            @pl.when((i == grid[0] - 1) & (j == grid[1] - 1))
            def _():
                comm.finish(*comm_refs)

    in_specs = [pl.BlockSpec((tm, a.shape[1]), lambda i, j: (i, 0)) for a in a_list]
    if b_t:
        in_specs += [pl.BlockSpec((tn, b.shape[1]), lambda i, j: (j, 0)) for b in b_list]
    else:
        in_specs += [pl.BlockSpec((b.shape[0], tn), lambda i, j: (0, j)) for b in b_list]
    args = list(a_list) + list(b_list)
    if add is not None:
        in_specs.append(pl.BlockSpec((tm, tn), lambda i, j: (i, j)))
        args.append(add)
    out_specs = pl.BlockSpec((tm, tn), lambda i, j: (i, j))
    out_shape = jax.ShapeDtypeStruct((m_rows, n_cols), out_dtype)
    if comm is None:
        return _call(body, name=name, grid=grid, in_specs=in_specs, out_specs=out_specs, out_shape=out_shape,
                     compiler_params=_params("parallel", "arbitrary"))(*args)
    any_spec = pl.BlockSpec(memory_space=pl.ANY)
    outs = _call(body, name=name, grid=grid, in_specs=in_specs + [any_spec] * n_comm,
                 out_specs=[out_specs] + [any_spec] * n_comm, out_shape=[out_shape] + comm.out_shape,
                 input_output_aliases={n_in + k: 1 + v for k, v in comm.aliases.items()},
                 scratch_shapes=comm.scratch, compiler_params=_params("arbitrary", "arbitrary"))(*args, *comm.arrays)
    return outs[0], outs[1:]


def _matmul_tn(a, b, *, name, tm, tn, tk):
    k_rows, m_rows = a.shape
    n_cols = b.shape[1]
    tm, tn, tk = min(tm, m_rows), min(tn, n_cols), min(tk, k_rows)
    nk = k_rows // tk

    def body(a_ref, b_ref, o_ref):
        @pl.when(pl.program_id(2) == 0)
        def _():
            o_ref[...] = jnp.zeros_like(o_ref)

        o_ref[...] += lax.dot_general(a_ref[...].astype(MXU_DTYPE), b_ref[...].astype(MXU_DTYPE),
                                      (((0,), (0,)), ((), ())), preferred_element_type=F32)

    return _call(
        body, name=name, grid=(m_rows // tm, n_cols // tn, nk),
        in_specs=[pl.BlockSpec((tk, tm), lambda i, j, k: (k, i)), pl.BlockSpec((tk, tn), lambda i, j, k: (k, j))],
        out_specs=pl.BlockSpec((tm, tn), lambda i, j, k: (i, j)),
        out_shape=jax.ShapeDtypeStruct((m_rows, n_cols), F32),
        compiler_params=_params("parallel", "parallel", "arbitrary"),
    )(a, b)


def _outproj_ln(ya, yr, yg, w_out, x, ln_g, ln_b, *, name):
    s_len = x.shape[0]
    tm = min(256, s_len)

    def body(ya_ref, yr_ref, yg_ref, w_ref, x_ref, g_ref, b_ref, z_ref, o_ref, lo_ref):
        acc = jnp.dot(ya_ref[...], w_ref[0:A_WIDTH, :], preferred_element_type=F32)
        acc += jnp.dot(yr_ref[...], w_ref[A_WIDTH:A_WIDTH + R_WIDTH, :], preferred_element_type=F32)
        acc += jnp.dot(yg_ref[...], w_ref[A_WIDTH + R_WIDTH:, :], preferred_element_type=F32)
        z = DEEPNORM_ALPHA * x_ref[...] + acc
        z_ref[...] = z
        mu = jnp.mean(z, axis=-1, keepdims=True)
        zc = z - mu
        var = jnp.mean(zc * zc, axis=-1, keepdims=True)
        out = zc * lax.rsqrt(var + LN_EPS) * g_ref[...] + b_ref[...]
        o_ref[...] = out
        lo_ref[...] = out.astype(lo_ref.dtype)

    def rows(width):
        return pl.BlockSpec((tm, width), lambda i: (i, 0))

    def whole(shape):
        return pl.BlockSpec(shape, lambda i: (0, 0))

    return _call(
        body, name=name, grid=(s_len // tm,),
        in_specs=[rows(A_WIDTH), rows(R_WIDTH), rows(G_WIDTH), whole((D_MODEL, D_MODEL)), rows(D_MODEL),
                  whole((1, D_MODEL)), whole((1, D_MODEL))],
        out_specs=[rows(D_MODEL)] * 3,
        out_shape=[jax.ShapeDtypeStruct((s_len, D_MODEL), F32)] * 2 + [jax.ShapeDtypeStruct((s_len, D_MODEL), MXU_DTYPE)],
        compiler_params=_params("parallel"),
    )(ya, yr, yg, w_out, x, ln_g, ln_b)


def _ln_bwd(z, ln_g, *, name, dxn=None, xn=None, target=None):
    s_len = z.shape[0]
    tm = min(256, s_len)
    top = dxn is None

    def body(*refs):
        if top:
            z_ref, g_ref, xn_ref, t_ref, dz_ref, lo_ref, dg_ref, db_ref, loss_ref = refs
            err = xn_ref[...] - t_ref[...]
            dy = err * (1.0 / D_MODEL)
        else:
            z_ref, g_ref, dy_ref, dz_ref, lo_ref, dg_ref, db_ref = refs
            dy = dy_ref[...]
        first = pl.program_id(0) == 0

        @pl.when(first)
        def _():
            dg_ref[...] = jnp.zeros_like(dg_ref)
            db_ref[...] = jnp.zeros_like(db_ref)
            if top:
                loss_ref[...] = jnp.zeros_like(loss_ref)

        z = z_ref[...]
        mu = jnp.mean(z, axis=-1, keepdims=True)
        zc = z - mu
        rstd = lax.rsqrt(jnp.mean(zc * zc, axis=-1, keepdims=True) + LN_EPS)
        xhat = zc * rstd
        dxh = dy * g_ref[...]
        dz = rstd * (dxh - jnp.mean(dxh, axis=-1, keepdims=True) - xhat * jnp.mean(dxh * xhat, axis=-1, keepdims=True))
        dz_ref[...] = dz
        lo_ref[...] = dz.astype(lo_ref.dtype)
        dg_ref[...] += jnp.sum(dy * xhat, axis=0, keepdims=True)
        db_ref[...] += jnp.sum(dy, axis=0, keepdims=True)
        if top:
            per_row = jnp.sum(err * err, axis=-1, keepdims=True) * (0.5 / D_MODEL)
            loss_ref[...] += jnp.sum(per_row, axis=0, keepdims=True)

    rows = pl.BlockSpec((tm, D_MODEL), lambda i: (i, 0))
    vec = pl.BlockSpec((1, D_MODEL), lambda i: (0, 0))
    in_specs = [rows, vec] + ([rows, rows] if top else [rows])
    args = [z, ln_g] + ([xn, target] if top else [dxn])
    out_specs = [rows, rows, vec, vec]
    out_shape = [jax.ShapeDtypeStruct((s_len, D_MODEL), F32), jax.ShapeDtypeStruct((s_len, D_MODEL), MXU_DTYPE),
                 jax.ShapeDtypeStruct((1, D_MODEL), F32), jax.ShapeDtypeStruct((1, D_MODEL), F32)]
    if top:
        out_specs.append(pl.BlockSpec((1, 1), lambda i: (0, 0)))
        out_shape.append(jax.ShapeDtypeStruct((1, 1), F32))
    return _call(body, name=name, grid=(s_len // tm,), in_specs=in_specs, out_specs=out_specs,
                 out_shape=out_shape, compiler_params=_params("arbitrary"))(*args)


CONV_ROWS = 256
HALO = 8


def _shift_down(x, halo, s):
    if s == 0:
        return x
    ext = jnp.concatenate([halo, x], axis=0)
    return pltpu.roll(ext, s, 0)[HALO:, :]


def _shift_up(x, halo, s):
    if s == 0:
        return x
    ext = jnp.concatenate([x, halo], axis=0)
    return pltpu.roll(ext, ext.shape[0] - s, 0)[:x.shape[0], :]


def _conv_fwd(src, width, w, bias, *, name):
    s_len = src.shape[0]
    rows = min(CONV_ROWS, s_len)
    per = rows // HALO

    def body(x_ref, halo_ref, w_ref, b_ref, o_ref):
        x = x_ref[...]
        halo = jnp.where(pl.program_id(0) == 0, 0.0, halo_ref[...])
        acc = x * w_ref[3:4, :] + b_ref[...]
        for k in range(CONV_WIDTH - 1):
            acc += _shift_down(x, halo, 3 - k) * w_ref[k:k + 1, :]
        o_ref[...] = acc

    return _call(
        body, name=name, grid=(s_len // rows,),
        in_specs=[pl.BlockSpec((rows, width), lambda i: (i, 0)),
                  pl.BlockSpec((HALO, width), lambda i: (jnp.maximum(i * per - 1, 0), 0)),
                  pl.BlockSpec((CONV_WIDTH, width), lambda i: (0, 0)), pl.BlockSpec((1, width), lambda i: (0, 0))],
        out_specs=pl.BlockSpec((rows, width), lambda i: (i, 0)),
        out_shape=jax.ShapeDtypeStruct((s_len, width), F32),
        compiler_params=_params("parallel"),
    )(src, src, w, bias)


def _conv_bwd(dy, src, width, w, passthrough, *, name):
    s_len = src.shape[0]
    rows = min(CONV_ROWS, s_len)
    per = rows // HALO
    nblk = s_len // rows
    extra = [p.shape[1] for p in passthrough]
    total = width + sum(extra)

    def body(*refs):
        dy_ref, dyh_ref, x_ref, xh_ref, w_ref = refs[:5]
        p_refs = refs[5:5 + len(extra)]
        o_ref, dw_ref, db_ref = refs[5 + len(extra):]
        i = pl.program_id(0)

        @pl.when(i == 0)
        def _():
            dw_ref[...] = jnp.zeros_like(dw_ref)
            db_ref[...] = jnp.zeros_like(db_ref)

        dy = dy_ref[...]
        x = x_ref[...]
        dy_halo = jnp.where(i == nblk - 1, 0.0, dyh_ref[...])
        x_halo = jnp.where(i == 0, 0.0, xh_ref[...])
        dx = dy * w_ref[3:4, :]
        dw_ref[3] += jnp.sum(dy * x, axis=0, keepdims=True)
        for k in range(CONV_WIDTH - 1):
            dx += _shift_up(dy, dy_halo, 3 - k) * w_ref[k:k + 1, :]
            dw_ref[k] += jnp.sum(dy * _shift_down(x, x_halo, 3 - k), axis=0, keepdims=True)
        db_ref[...] += jnp.sum(dy, axis=0, keepdims=True)
        o_ref[:, 0:width] = dx.astype(o_ref.dtype)
        off = width
        for p_ref, wd in zip(p_refs, extra):
            o_ref[:, off:off + wd] = p_ref[...].astype(o_ref.dtype)
            off += wd

    in_specs = [pl.BlockSpec((rows, width), lambda i: (i, 0)),
                pl.BlockSpec((HALO, width), lambda i: (jnp.minimum((i + 1) * per, nblk * per - 1), 0)),
                pl.BlockSpec((rows, width), lambda i: (i, 0)),
                pl.BlockSpec((HALO, width), lambda i: (jnp.maximum(i * per - 1, 0), 0)),
                pl.BlockSpec((CONV_WIDTH, width), lambda i: (0, 0))]
    in_specs += [pl.BlockSpec((rows, wd), lambda i: (i, 0)) for wd in extra]
    return _call(
        body, name=name, grid=(nblk,), in_specs=in_specs,
        out_specs=[pl.BlockSpec((rows, total), lambda i: (i, 0)),
                   pl.BlockSpec((CONV_WIDTH, 1, width), lambda i: (0, 0, 0)), pl.BlockSpec((1, width), lambda i: (0, 0))],
        out_shape=[jax.ShapeDtypeStruct((s_len, total), MXU_DTYPE), jax.ShapeDtypeStruct((CONV_WIDTH, 1, width), F32),
                   jax.ShapeDtypeStruct((1, width), F32)],
        compiler_params=_params("arbitrary"),
    )(dy, dy, src, src, w, *passthrough)


def _attn_mask(first):
    i = _iota((A_BLOCK, 2 * A_BLOCK), 0)
    j = _iota((A_BLOCK, 2 * A_BLOCK), 1)
    band = (j > i) & (j <= i + A_BLOCK)
    return band & ((j >= A_BLOCK) | jnp.logical_not(first))


def _attn_group(p, mask, qg, kw, kws, vw, vws, azg, sink0, sink1):
    low = _iota(qg.shape, 1) < A_HEAD_DIM
    first_lane = (_iota((A_BLOCK, LANES), 1) == 0).astype(F32)
    out = None
    for half, sink in ((0, sink0), (1, sink1)):
        kv_head = (2 * p + half) // (A_HEADS // A_KV_HEADS)
        keep = low if half == 0 else jnp.logical_not(low)
        qm = jnp.where(keep, qg, 0.0)
        kk, vv = (kw, vw) if kv_head == half else (kws, vws)
        s = mm_nt(qm, kk) * (A_HEAD_DIM ** -0.5)
        s = jnp.where(mask, s, NEG)
        sk = jnp.sum(jnp.tile(sink, (A_BLOCK // 8, 1)) * first_lane, axis=1, keepdims=True)
        m = lax.stop_gradient(jnp.maximum(jnp.max(s, axis=1, keepdims=True), sk))
        e = jnp.exp(s - m)
        denom = jnp.sum(e, axis=1, keepdims=True) + jnp.exp(sk - m)
        o = mm_nn(e * (1.0 / denom), vv)
        o = jnp.where(keep, o, 0.0)
        out = o if out is None else out + o
    return out * _silu(azg)


def _attn_specs(s_len, rev):
    nb = s_len // A_BLOCK

    def cur(i):
        return nb - 1 - i if rev else i

    def prev(i):
        return jnp.maximum(cur(i) - 1, 0)

    def blk(width, col, which):
        return pl.BlockSpec((A_BLOCK, width), lambda i: (which(i), col))

    return [blk(A_WIDTH, 0, cur), blk(A_WIDTH, 1, cur), blk(LANES, 8, cur), blk(LANES, 9, cur),
            blk(LANES, 8, prev), blk(LANES, 9, prev), blk(LANES, 0, cur), blk(LANES, 0, cur),
            blk(LANES, 0, prev), blk(LANES, 0, prev)], cur


def _attn_fwd(proj_a, cos, sin, sinks_t, *, name):
    s_len = proj_a.shape[0]
    specs, _ = _attn_specs(s_len, False)

    def body(q_ref, az_ref, k_ref, v_ref, kp_ref, vp_ref, c_ref, s_ref, cp_ref, sp_ref, sink_ref, o_ref):
        first = pl.program_id(0) == 0
        mask = _attn_mask(first)
        qr = _rope(q_ref[...], c_ref[...], s_ref[...])
        kw = jnp.concatenate([_rope(kp_ref[...], cp_ref[...], sp_ref[...]), _rope(k_ref[...], c_ref[...], s_ref[...])], 0)
        vw = jnp.concatenate([vp_ref[...], v_ref[...]], 0)
        kws, vws = _swap64(kw), _swap64(vw)
        for p in range(A_WIDTH // LANES):
            cols = slice(p * LANES, (p + 1) * LANES)
            o = _attn_group(p, mask, qr[:, cols], kw, kws, vw, vws, az_ref[:, cols], sink_ref[2 * p], sink_ref[2 * p + 1])
            o_ref[:, cols] = o.astype(o_ref.dtype)

    return _call(
        body, name=name, grid=(s_len // A_BLOCK,),
        in_specs=specs + [pl.BlockSpec((A_HEADS, 8, LANES), lambda i: (0, 0, 0))],
        out_specs=pl.BlockSpec((A_BLOCK, A_WIDTH), lambda i: (i, 0)),
        out_shape=jax.ShapeDtypeStruct((s_len, A_WIDTH), MXU_DTYPE),
        compiler_params=_params("parallel"),
    )(proj_a, proj_a, proj_a, proj_a, proj_a, proj_a, cos, sin, cos, sin, sinks_t)


def _attn_bwd(proj_a, cos, sin, sinks_t, dya, *, name):
    s_len = proj_a.shape[0]
    specs, cur = _attn_specs(s_len, True)

    def body(q_ref, az_ref, k_ref, v_ref, kp_ref, vp_ref, c_ref, s_ref, cp_ref, sp_ref, sink_ref, dy_ref,
             o_ref, dsink_ref, dk_carry, dv_carry):
        i = pl.program_id(0)

        @pl.when(i == 0)
        def _():
            dsink_ref[...] = jnp.zeros_like(dsink_ref)
            dk_carry[...] = jnp.zeros_like(dk_carry)
            dv_carry[...] = jnp.zeros_like(dv_carry)

        first = cur(i) == 0
        mask = _attn_mask(first)
        cos_c, sin_c = c_ref[...], s_ref[...]
        qr = _rope(q_ref[...], cos_c, sin_c)
        kw = jnp.concatenate([_rope(kp_ref[...], cp_ref[...], sp_ref[...]), _rope(k_ref[...], cos_c, sin_c)], 0)
        vw = jnp.concatenate([vp_ref[...], v_ref[...]], 0)
        kws, vws = _swap64(kw), _swap64(vw)
        dkw = jnp.zeros_like(kw)
        dvw = jnp.zeros_like(vw)
        for p in range(A_WIDTH // LANES):
            cols = slice(p * LANES, (p + 1) * LANES)
            _, vjp = jax.vjp(functools.partial(_attn_group, p, mask), qr[:, cols], kw, kws, vw, vws, az_ref[:, cols],
                             sink_ref[2 * p], sink_ref[2 * p + 1])
            dq, dk1, dk2, dv1, dv2, daz, ds0, ds1 = vjp(dy_ref[:, cols])
            dkw += dk1 + _swap64(dk2)
            dvw += dv1 + _swap64(dv2)
            o_ref[:, cols] = _rope_t(dq, cos_c, sin_c).astype(o_ref.dtype)
            o_ref[:, A_WIDTH + p * LANES:A_WIDTH + (p + 1) * LANES] = daz.astype(o_ref.dtype)
            dsink_ref[2 * p] += ds0
            dsink_ref[2 * p + 1] += ds1
        o_ref[:, 2 * A_WIDTH:2 * A_WIDTH + LANES] = _rope_t(dkw[A_BLOCK:, :] + dk_carry[...], cos_c, sin_c).astype(o_ref.dtype)
        o_ref[:, 2 * A_WIDTH + LANES:] = (dvw[A_BLOCK:, :] + dv_carry[...]).astype(o_ref.dtype)
        dk_carry[...] = dkw[:A_BLOCK, :]
        dv_carry[...] = dvw[:A_BLOCK, :]

    return _call(
        body, name=name, grid=(s_len // A_BLOCK,),
        in_specs=specs + [pl.BlockSpec((A_HEADS, 8, LANES), lambda i: (0, 0, 0)),
                          pl.BlockSpec((A_BLOCK, A_WIDTH), lambda i: (cur(i), 0))],
        out_specs=[pl.BlockSpec((A_BLOCK, WA), lambda i: (cur(i), 0)),
                   pl.BlockSpec((A_HEADS, 8, LANES), lambda i: (0, 0, 0))],
        out_shape=[jax.ShapeDtypeStruct((s_len, WA), MXU_DTYPE), jax.ShapeDtypeStruct((A_HEADS, 8, LANES), F32)],
        scratch_shapes=[pltpu.VMEM((A_BLOCK, LANES), F32), pltpu.VMEM((A_BLOCK, LANES), F32)],
        compiler_params=_params("arbitrary"),
    )(proj_a, proj_a, proj_a, proj_a, proj_a, proj_a, cos, sin, cos, sin, sinks_t, dya)


RG_ROWS = 256


def _rg_gates(x, wa, ba, wx, bx, lam):
    r = jax.nn.sigmoid(mm_nn(x, wa) + ba)
    ig = jax.nn.sigmoid(mm_nn(x, wx) + bx)
    log_a = -R_C * r * _softplus(-lam)
    a = jnp.exp(log_a)
    return a, jnp.sqrt(_one_minus_sq(log_a, a)) * (ig * x)


def _rg_param_specs():
    mat = pl.BlockSpec((R_BLOCKS, R_BLOCK_DIM, R_BLOCK_DIM), lambda i: (0, 0, 0))
    vec = pl.BlockSpec((1, R_WIDTH), lambda i: (0, 0))
    return [mat, vec, mat, vec, vec]


def _rg_fwd(xr, proj_r, wa, ba, wx, bx, lam, *, name):
    s_len = xr.shape[0]
    rows = min(RG_ROWS, s_len)

    def body(x_ref, z_ref, wa_ref, ba_ref, wx_ref, bx_ref, lam_ref, h_ref, y_ref, a_buf, u_buf, carry):
        @pl.when(pl.program_id(0) == 0)
        def _():
            carry[...] = jnp.zeros_like(carry)

        for n in range(R_BLOCKS):
            cols = slice(n * R_BLOCK_DIM, (n + 1) * R_BLOCK_DIM)
            a, u = _rg_gates(x_ref[:, cols], wa_ref[n], ba_ref[:, cols], wx_ref[n], bx_ref[:, cols], lam_ref[:, cols])
            a_buf[:, cols] = a
            u_buf[:, cols] = u

        def step(t, h):
            h = a_buf[pl.ds(t, 1), :] * h + u_buf[pl.ds(t, 1), :]
            h_ref[pl.ds(t, 1), :] = h
            return h

        carry[...] = lax.fori_loop(0, rows, step, carry[...], unroll=8)
        y_ref[...] = (h_ref[...] * _silu(z_ref[...])).astype(y_ref.dtype)

    blk = pl.BlockSpec((rows, R_WIDTH), lambda i: (i, 0))
    return _call(
        body, name=name, grid=(s_len // rows,),
        in_specs=[blk, pl.BlockSpec((rows, R_WIDTH), lambda i: (i, 1))] + _rg_param_specs(),
        out_specs=[blk, blk],
        out_shape=[jax.ShapeDtypeStruct((s_len, R_WIDTH), F32), jax.ShapeDtypeStruct((s_len, R_WIDTH), MXU_DTYPE)],
        scratch_shapes=[pltpu.VMEM((rows, R_WIDTH), F32), pltpu.VMEM((rows, R_WIDTH), F32), pltpu.VMEM((1, R_WIDTH), F32)],
        compiler_params=_params("arbitrary"),
    )(xr, proj_r, wa, ba, wx, bx, lam)


def _rg_bwd(xr, proj_r, h, dyr, wa, ba, wx, bx, lam, *, name):
    s_len = xr.shape[0]
    rows = min(RG_ROWS, s_len)
    nblk = s_len // rows
    per = rows // HALO

    def cur(i):
        return nblk - 1 - i

    def body(x_ref, z_ref, h_ref, hh_ref, dy_ref, wa_ref, ba_ref, wx_ref, bx_ref, lam_ref,
             dx_ref, dz_ref, dwa_ref, dba_ref, dwx_ref, dbx_ref, dlam_ref, a_buf, g_buf, carry):
        i = pl.program_id(0)

        @pl.when(i == 0)
        def _():
            carry[...] = jnp.zeros_like(carry)
            for ref in (dwa_ref, dba_ref, dwx_ref, dbx_ref, dlam_ref):
                ref[...] = jnp.zeros_like(ref)

        z = z_ref[...]
        sig = jax.nn.sigmoid(z)
        hval = h_ref[...]
        dy = dy_ref[...]
        dz_ref[...] = dy * hval * (sig * (1.0 + z * (1.0 - sig)))
        g_buf[...] = dy * (z * sig)
        kept = []
        for n in range(R_BLOCKS):
            cols = slice(n * R_BLOCK_DIM, (n + 1) * R_BLOCK_DIM)
            x = x_ref[:, cols]
            r = jax.nn.sigmoid(_nn(x, wa_ref[n]) + ba_ref[:, cols])
            ig = jax.nn.sigmoid(_nn(x, wx_ref[n]) + bx_ref[:, cols])
            sp = _softplus(-lam_ref[:, cols])
            log_a = -R_C * r * sp
            a = jnp.exp(log_a)
            a_buf[:, cols] = a
            kept.append((x, r, ig, sp, a, jnp.sqrt(_one_minus_sq(log_a, a))))

        def step(k, c):
            t = rows - 1 - k
            g = g_buf[pl.ds(t, 1), :] + c
            g_buf[pl.ds(t, 1), :] = g
            return a_buf[pl.ds(t, 1), :] * g

        carry[...] = lax.fori_loop(0, rows, step, carry[...], unroll=8)
        h_halo = jnp.where(cur(i) == 0, 0.0, hh_ref[...])
        dh = g_buf[...]
        da = dh * _shift_down(hval, h_halo, 1)
        for n in range(R_BLOCKS):
            cols = slice(n * R_BLOCK_DIM, (n + 1) * R_BLOCK_DIM)
            x, r, ig, sp, a, s = kept[n]
            du = dh[:, cols]
            dux = du * x
            d_log_a = a * (da[:, cols] - a * (dux * ig) / s)
            d_ga = d_log_a * (-R_C * sp) * (r * (1.0 - r))
            d_gx = dux * s * (ig * (1.0 - ig))
            dx_ref[:, cols] = du * (s * ig) + _nt(d_ga, wa_ref[n]) + _nt(d_gx, wx_ref[n])
            xt = _t(x)
            dwa_ref[n] += _nn(xt, d_ga)
            dwx_ref[n] += _nn(xt, d_gx)
            dba_ref[:, cols] += jnp.sum(d_ga, axis=0, keepdims=True)
            dbx_ref[:, cols] += jnp.sum(d_gx, axis=0, keepdims=True)
            dlam_ref[:, cols] += jnp.sum(d_log_a * r, axis=0, keepdims=True) * (R_C * jax.nn.sigmoid(-lam_ref[:, cols]))

    blk = pl.BlockSpec((rows, R_WIDTH), lambda i: (cur(i), 0))
    mat = pl.BlockSpec((R_BLOCKS, R_BLOCK_DIM, R_BLOCK_DIM), lambda i: (0, 0, 0))
    vec = pl.BlockSpec((1, R_WIDTH), lambda i: (0, 0))
    return _call(
        body, name=name, grid=(nblk,),
        in_specs=[blk, pl.BlockSpec((rows, R_WIDTH), lambda i: (cur(i), 1)), blk,
                  pl.BlockSpec((HALO, R_WIDTH), lambda i: (jnp.maximum(cur(i) * per - 1, 0), 0)), blk] + _rg_param_specs(),
        out_specs=[blk, blk, mat, vec, mat, vec, vec],
        out_shape=[jax.ShapeDtypeStruct((s_len, R_WIDTH), F32)] * 2 + [
            jax.ShapeDtypeStruct((R_BLOCKS, R_BLOCK_DIM, R_BLOCK_DIM), F32), jax.ShapeDtypeStruct((1, R_WIDTH), F32),
            jax.ShapeDtypeStruct((R_BLOCKS, R_BLOCK_DIM, R_BLOCK_DIM), F32), jax.ShapeDtypeStruct((1, R_WIDTH), F32),
            jax.ShapeDtypeStruct((1, R_WIDTH), F32)],
        scratch_shapes=[pltpu.VMEM((rows, R_WIDTH), F32), pltpu.VMEM((rows, R_WIDTH), F32), pltpu.VMEM((1, R_WIDTH), F32)],
        compiler_params=_params("arbitrary"),
    )(xr, proj_r, h, h, dyr, wa, ba, wx, bx, lam)


GP_CHUNKS = 8
GP_CHUNKS_BWD = 4
GS_CHUNKS = 8


def _seg_cumsum(x, reverse):
    rows = x.shape[0]
    r = _iota(x.shape, 0) & (G_CHUNK - 1)
    s = 1
    while s < G_CHUNK:
        if reverse:
            x = x + jnp.where(r < G_CHUNK - s, pltpu.roll(x, rows - s, 0), 0.0)
        else:
            x = x + jnp.where(r >= s, pltpu.roll(x, s, 0), 0.0)
        s *= 2
    return x


def _gdn_decay(ga, a_log_row, dt_row):
    return -jnp.exp(a_log_row) * _softplus(ga + dt_row)


def _gdn_chunk(cq, ck, cv, gb, gc, inv=None):
    shape = cq.shape
    head = _iota(shape, 0) & (G_HEADS - 1)
    lane = _iota(shape, 2)
    q, k, v = _silu(cq), _silu(ck), _silu(cv)
    q = q * lax.rsqrt(jnp.sum(q * q, axis=-1, keepdims=True) + RMS_EPS) * (G_HEAD_DIM ** -0.5)
    k = k * lax.rsqrt(jnp.sum(k * k, axis=-1, keepdims=True) + RMS_EPS)
    beta = jnp.sum(jnp.where(lane == head, jax.nn.sigmoid(gb), 0.0), axis=-1, keepdims=True)
    g = jnp.sum(jnp.where(lane == head + G_HEADS, gc, 0.0), axis=-1, keepdims=True)
    sq = (shape[0], G_CHUNK, G_CHUNK)
    row, col = _iota(sq, 1), _iota(sq, 2)
    g_sq = jnp.broadcast_to(g, sq)
    decay = jnp.where(row >= col, jnp.exp(jnp.minimum(g_sq - _t(g_sq), 0.0)), 0.0)
    g_last = jnp.sum(jnp.where(_iota(g.shape, 1) == G_CHUNK - 1, g, 0.0), axis=1, keepdims=True)
    eg = jnp.exp(g)
    kb, vb = k * beta, v * beta
    m = jnp.where(row > col, mm_nt(kb, k) * decay, 0.0)
    known = inv is not None
    if not known:
        inv = _inv_unit_lower(m)
    u, w = _solve2(m, inv, vb, kb * eg)
    qk = jnp.where(row >= col, mm_nt(q, k) * decay, 0.0)
    q_dec = q * eg
    k_dec = k * jnp.exp(g_last - g)
    gl = jnp.broadcast_to(jnp.exp(g_last), (shape[0], 1, G_HEAD_DIM))
    return (u, w, qk, q_dec, k_dec, gl) if known else (u, w, qk, q_dec, k_dec, gl, inv)


def _gdn_step(state, u, w, qk, q_dec, k_dec, gl, gz, norm_w):
    v_new = u - mm_nn(w, state)
    o = mm_nn(q_dec, state) + mm_nn(qk, v_new)
    new_state = state * gl + mm_nn(_t(k_dec), v_new)
    o = o * lax.rsqrt(jnp.mean(o * o, axis=-1, keepdims=True) + RMS_EPS) * norm_w
    return o * _silu(gz), new_state


def _stack_chunks(x, heads):
    chunks = x.shape[0] // G_CHUNK
    parts = []
    for c in range(chunks):
        rows = slice(c * G_CHUNK, (c + 1) * G_CHUNK)
        for hd in range(G_HEADS):
            parts.append(x[rows, hd * LANES:(hd + 1) * LANES] if heads else x[rows, :])
    return jnp.stack(parts)


def _gdn_chunk_shapes(nch):
    b = nch * G_HEADS
    wide = jax.ShapeDtypeStruct((b, G_CHUNK, G_HEAD_DIM), F32)
    return [wide, wide, jax.ShapeDtypeStruct((b, G_CHUNK, G_CHUNK), F32), wide, wide,
            jax.ShapeDtypeStruct((b, 1, G_HEAD_DIM), F32)]


def _gdn_chunk_specs(nbatch):
    wide = pl.BlockSpec((nbatch, G_CHUNK, G_HEAD_DIM), lambda i: (i, 0, 0))
    return [wide, wide, pl.BlockSpec((nbatch, G_CHUNK, G_CHUNK), lambda i: (i, 0, 0)), wide, wide,
            pl.BlockSpec((nbatch, 1, G_HEAD_DIM), lambda i: (i, 0, 0))]


def _gdn_chunk_fwd(conv, proj_g, a_log_row, dt_row, *, name):
    s_len = conv.shape[0]
    cpg = min(GP_CHUNKS, s_len // G_CHUNK)
    rows = cpg * G_CHUNK
    nbatch = cpg * G_HEADS

    def body(c_ref, bg_ref, al_ref, dt_ref, *outs):
        bg = bg_ref[...]
        gc = _seg_cumsum(_gdn_decay(bg, al_ref[...], dt_ref[...]), False)
        res = _gdn_chunk(_stack_chunks(c_ref[:, 0:G_WIDTH], True), _stack_chunks(c_ref[:, G_WIDTH:2 * G_WIDTH], True),
                         _stack_chunks(c_ref[:, 2 * G_WIDTH:], True), _stack_chunks(bg, False), _stack_chunks(gc, False))
        for ref, val in zip(outs, res):
            ref[...] = val

    row = pl.BlockSpec((1, LANES), lambda i: (0, 0))
    return _call(
        body, name=name, grid=(s_len // rows,),
        in_specs=[pl.BlockSpec((rows, 3 * G_WIDTH), lambda i: (i, 0)),
                  pl.BlockSpec((rows, LANES), lambda i: (i, (3 * G_WIDTH + G_WIDTH) // LANES)), row, row],
        out_specs=_gdn_chunk_specs(nbatch) + [pl.BlockSpec((nbatch, G_CHUNK, G_CHUNK), lambda i: (i, 0, 0))],
        out_shape=_gdn_chunk_shapes(s_len // G_CHUNK) + [
            jax.ShapeDtypeStruct((s_len // G_CHUNK * G_HEADS, G_CHUNK, G_CHUNK), F32)],
        compiler_params=_params("parallel"),
    )(conv, proj_g, a_log_row, dt_row)


def _gdn_chunk_bwd(conv, proj_g, a_log_row, dt_row, inv, cots, *, name):
    s_len = conv.shape[0]
    cpg = min(GP_CHUNKS_BWD, s_len // G_CHUNK)
    rows = cpg * G_CHUNK
    nbatch = cpg * G_HEADS

    def unstack(x, heads):
        if heads:
            return jnp.concatenate([jnp.concatenate([x[c * G_HEADS + hd] for hd in range(G_HEADS)], axis=1)
                                    for c in range(cpg)], axis=0)
        return jnp.concatenate([sum(x[c * G_HEADS + hd] for hd in range(G_HEADS)) for c in range(cpg)], axis=0)

    def body(c_ref, bg_ref, al_ref, dt_ref, inv_ref, du, dw, dqk, dqd, dkd, dgl, dc_ref, dbg_ref, dal_ref, ddt_ref):
        @pl.when(pl.program_id(0) == 0)
        def _():
            dal_ref[...] = jnp.zeros_like(dal_ref)
            ddt_ref[...] = jnp.zeros_like(ddt_ref)

        bg = bg_ref[...]
        g_all, decay_vjp = jax.vjp(_gdn_decay, bg, al_ref[...], dt_ref[...])
        gc = _seg_cumsum(g_all, False)
        _, vjp = jax.vjp(_gdn_chunk, _stack_chunks(c_ref[:, 0:G_WIDTH], True),
                         _stack_chunks(c_ref[:, G_WIDTH:2 * G_WIDTH], True), _stack_chunks(c_ref[:, 2 * G_WIDTH:], True),
                         _stack_chunks(bg, False), _stack_chunks(gc, False), inv_ref[...])
        dq, dk, dv, dgb, dgc, _ = vjp((du[...], dw[...], dqk[...], dqd[...], dkd[...], dgl[...]))
        dc_ref[:, 0:G_WIDTH] = unstack(dq, True)
        dc_ref[:, G_WIDTH:2 * G_WIDTH] = unstack(dk, True)
        dc_ref[:, 2 * G_WIDTH:] = unstack(dv, True)
        dga, dal, ddt = decay_vjp(_seg_cumsum(unstack(dgc, False), True))
        dbg_ref[:, 0:LANES] = unstack(dgb, False) + dga
        dbg_ref[:, LANES:] = jnp.zeros((rows, LANES), F32)
        dal_ref[...] += dal
        ddt_ref[...] += ddt

    row = pl.BlockSpec((1, LANES), lambda i: (0, 0))
    return _call(
        body, name=name, grid=(s_len // rows,),
        in_specs=[pl.BlockSpec((rows, 3 * G_WIDTH), lambda i: (i, 0)),
                  pl.BlockSpec((rows, LANES), lambda i: (i, (3 * G_WIDTH + G_WIDTH) // LANES)), row, row,
                  pl.BlockSpec((nbatch, G_CHUNK, G_CHUNK), lambda i: (i, 0, 0))]
        + _gdn_chunk_specs(nbatch),
        out_specs=[pl.BlockSpec((rows, 3 * G_WIDTH), lambda i: (i, 0)), pl.BlockSpec((rows, 2 * LANES), lambda i: (i, 0)),
                   row, row],
        out_shape=[jax.ShapeDtypeStruct((s_len, 3 * G_WIDTH), F32), jax.ShapeDtypeStruct((s_len, 2 * LANES), F32),
                   jax.ShapeDtypeStruct((1, LANES), F32), jax.ShapeDtypeStruct((1, LANES), F32)],
        compiler_params=_params("arbitrary"),
    )(conv, proj_g, a_log_row, dt_row, inv, *cots)


def _gdn_scan_specs(cpg, which):
    nbatch = cpg * G_HEADS
    wide = pl.BlockSpec((nbatch, G_CHUNK, G_HEAD_DIM), lambda i: (which(i), 0, 0))
    return [wide, wide, pl.BlockSpec((nbatch, G_CHUNK, G_CHUNK), lambda i: (which(i), 0, 0)), wide, wide,
            pl.BlockSpec((nbatch, 1, G_HEAD_DIM), lambda i: (which(i), 0, 0))]


def _gz_stack(z_ref, c):
    rows = pl.ds(pl.multiple_of(c * G_CHUNK, G_CHUNK), G_CHUNK)
    return jnp.stack([z_ref[rows, hd * LANES:(hd + 1) * LANES] for hd in range(G_HEADS)])


def _gdn_scan_fwd(chunk_vals, proj_g, norm_w, *, name):
    s_len = proj_g.shape[0]
    nch = s_len // G_CHUNK
    cpg = min(GS_CHUNKS, nch)
    rows = cpg * G_CHUNK

    def body(u_ref, w_ref, qk_ref, qd_ref, kd_ref, gl_ref, z_ref, nw_ref, y_ref, st_ref, state):
        @pl.when(pl.program_id(0) == 0)
        def _():
            state[...] = jnp.zeros_like(state)

        def step(c, carry):
            b = pl.ds(pl.multiple_of(c * G_HEADS, G_HEADS), G_HEADS)
            st = state[...]
            st_ref[b] = st
            y, new_state = _gdn_step(st, u_ref[b], w_ref[b], qk_ref[b], qd_ref[b], kd_ref[b], gl_ref[b],
                                     _gz_stack(z_ref, c), nw_ref[...])
            state[...] = new_state
            rws = pl.ds(pl.multiple_of(c * G_CHUNK, G_CHUNK), G_CHUNK)
            for hd in range(G_HEADS):
                y_ref[rws, hd * LANES:(hd + 1) * LANES] = y[hd].astype(y_ref.dtype)
            return carry

        lax.fori_loop(0, cpg, step, 0, unroll=2)

    return _call(
        body, name=name, grid=(nch // cpg,),
        in_specs=_gdn_scan_specs(cpg, lambda i: i) + [
            pl.BlockSpec((rows, G_WIDTH), lambda i: (i, 3)), pl.BlockSpec((1, G_HEAD_DIM), lambda i: (0, 0))],
        out_specs=[pl.BlockSpec((rows, G_WIDTH), lambda i: (i, 0)),
                   pl.BlockSpec((cpg * G_HEADS, G_HEAD_DIM, G_HEAD_DIM), lambda i: (i, 0, 0))],
        out_shape=[jax.ShapeDtypeStruct((s_len, G_WIDTH), MXU_DTYPE),
                   jax.ShapeDtypeStruct((nch * G_HEADS, G_HEAD_DIM, G_HEAD_DIM), F32)],
        scratch_shapes=[pltpu.VMEM((G_HEADS, G_HEAD_DIM, G_HEAD_DIM), F32)],
        compiler_params=_params("arbitrary"),
    )(*chunk_vals, proj_g, norm_w)


def _gdn_scan_bwd(chunk_vals, states, proj_g, norm_w, dyg, *, name):
    s_len = proj_g.shape[0]
    nch = s_len // G_CHUNK
    cpg = min(GS_CHUNKS, nch)
    rows = cpg * G_CHUNK
    ngrid = nch // cpg

    def cur(i):
        return ngrid - 1 - i

    def body(u_ref, w_ref, qk_ref, qd_ref, kd_ref, gl_ref, st_ref, z_ref, nw_ref, dy_ref,
             du_ref, dw_ref, dqk_ref, dqd_ref, dkd_ref, dgl_ref, dz_ref, dnw_ref, dstate):
        @pl.when(pl.program_id(0) == 0)
        def _():
            dstate[...] = jnp.zeros_like(dstate)
            dnw_ref[...] = jnp.zeros_like(dnw_ref)

        def step(k, carry):
            c = cpg - 1 - k
            b = pl.ds(pl.multiple_of(c * G_HEADS, G_HEADS), G_HEADS)
            _, vjp = jax.vjp(_gdn_step, st_ref[b], u_ref[b], w_ref[b], qk_ref[b], qd_ref[b], kd_ref[b], gl_ref[b],
                             _gz_stack(z_ref, c), nw_ref[...])
            dst, du, dw, dqk, dqd, dkd, dgl, dz, dnw = vjp((_gz_stack(dy_ref, c), dstate[...]))
            dstate[...] = dst
            du_ref[b], dw_ref[b], dqk_ref[b], dqd_ref[b], dkd_ref[b], dgl_ref[b] = du, dw, dqk, dqd, dkd, dgl
            rws = pl.ds(pl.multiple_of(c * G_CHUNK, G_CHUNK), G_CHUNK)
            for hd in range(G_HEADS):
                dz_ref[rws, hd * LANES:(hd + 1) * LANES] = dz[hd]
            dnw_ref[...] += dnw
            return carry

        lax.fori_loop(0, cpg, step, 0, unroll=2)

    gate = pl.BlockSpec((rows, G_WIDTH), lambda i: (cur(i), 3))
    wide = pl.BlockSpec((rows, G_WIDTH), lambda i: (cur(i), 0))
    vec = pl.BlockSpec((1, G_HEAD_DIM), lambda i: (0, 0))
    return _call(
        body, name=name, grid=(ngrid,),
        in_specs=_gdn_scan_specs(cpg, cur) + [
            pl.BlockSpec((cpg * G_HEADS, G_HEAD_DIM, G_HEAD_DIM), lambda i: (cur(i), 0, 0)), gate, vec, wide],
        out_specs=_gdn_scan_specs(cpg, cur) + [wide, vec],
        out_shape=_gdn_chunk_shapes(nch) + [jax.ShapeDtypeStruct((s_len, G_WIDTH), F32),
                                            jax.ShapeDtypeStruct((1, G_HEAD_DIM), F32)],
        scratch_shapes=[pltpu.VMEM((G_HEADS, G_HEAD_DIM, G_HEAD_DIM), F32)],
        compiler_params=_params("arbitrary"),
    )(*chunk_vals, states, proj_g, norm_w, dyg)


def _adamw_math(w, g, m, v):
    m = ADAM_B1 * m + (1.0 - ADAM_B1) * g
    v = ADAM_B2 * v + (1.0 - ADAM_B2) * (g * g)
    m_hat = m / (1.0 - ADAM_B1 ** ADAM_STEP)
    v_hat = v / (1.0 - ADAM_B2 ** ADAM_STEP)
    delta = -ADAM_LR * (m_hat / (jnp.sqrt(v_hat) + ADAM_EPS) + ADAM_WD * w)
    return delta, m, v


def _sum_adamw(own, chip, parts, w, m, v, *, name, rows):
    n_layers, n_rows, n_cols = w.shape
    rows = min(rows, n_rows)
    n_parts = parts[0].shape[0]

    def body(c_ref, *refs):
        own_refs, part_refs = refs[:n_layers], refs[n_layers:2 * n_layers]
        w_ref, m_ref, v_ref, g_ref, d_ref, nm_ref, nv_ref = refs[2 * n_layers:]
        layer = pl.program_id(0)
        g = None
        for l in range(n_layers):
            g_l = own_refs[l][0].astype(F32)
            for k in range(n_parts):
                g_l = g_l + part_refs[l][k].astype(F32)
            g = g_l if g is None else jnp.where(layer == l, g_l, g)
        delta, new_m, new_v = _adamw_math(w_ref[0], g, m_ref[0], v_ref[0])
        g_ref[0], d_ref[0], nm_ref[0], nv_ref[0] = g, delta, new_m, new_v

    blk = pl.BlockSpec((1, rows, n_cols), lambda l, i, c: (l, i, 0))
    grid_spec = pltpu.PrefetchScalarGridSpec(
        num_scalar_prefetch=1, grid=(n_layers, n_rows // rows),
        in_specs=[pl.BlockSpec((1, rows, n_cols), lambda l, i, c: (c[0], i, 0))] * n_layers
        + [pl.BlockSpec((n_parts, rows, n_cols), lambda l, i, c: (0, i, 0))] * n_layers + [blk, blk, blk],
        out_specs=[blk] * 4)
    return _call(
        body, name=name, grid_spec=grid_spec, out_shape=[jax.ShapeDtypeStruct(w.shape, F32)] * 4,
        compiler_params=_params("parallel", "parallel"),
    )(_index_operand(chip), *own, *parts, w, m, v)


def _sum_slots(parts, *, name):
    rows = parts.shape[1]

    def body(p_ref, o_ref):
        g = p_ref[0]
        for k in range(1, N_DEV):
            g = g + p_ref[k]
        o_ref[...] = g

    return _call(body, name=name, grid=(1,),
                 in_specs=[pl.BlockSpec(parts.shape, lambda i: (0, 0, 0))],
                 out_specs=pl.BlockSpec((rows, LANES), lambda i: (0, 0)),
                 out_shape=jax.ShapeDtypeStruct((rows, LANES), F32), compiler_params=_params("arbitrary"))(parts)


def _adamw_packed(w, g, m, v, *, name):
    def body(w_ref, g_ref, m_ref, v_ref, d_ref, nm_ref, nv_ref):
        d_ref[...], nm_ref[...], nv_ref[...] = _adamw_math(w_ref[...], g_ref[...], m_ref[...], v_ref[...])

    blk = pl.BlockSpec(w.shape, lambda i: (0, 0))
    return _call(body, name=name, grid=(1,), in_specs=[blk] * 4, out_specs=[blk] * 3,
                 out_shape=[jax.ShapeDtypeStruct(w.shape, F32)] * 3, compiler_params=_params("arbitrary"))(w, g, m, v)


A_COLS = ((0, 512), (768, 1280), (512, 768))
R_COLS = ((1280, 3328),)
G_COLS = ((3328, 5384),)


def _group_weights(wt_full):
    def take(ranges):
        return jnp.concatenate([wt_full[a:b] for a, b in ranges], axis=0)

    wt_g = jnp.concatenate([take(G_COLS), jnp.zeros((G_PAD, wt_full.shape[1]), wt_full.dtype)], axis=0)
    return take(A_COLS), take(R_COLS), wt_g


def _ungroup_grads(d_a, d_r, d_g):
    return jnp.concatenate([d_a[0:512], d_a[1024:1280], d_a[512:1024], d_r, d_g[:WG - G_PAD]], axis=0)


def _shard_rows(w):
    return jnp.pad(jnp.transpose(w, (0, 2, 1)), ((0, 0), (0, N_ROWS_PAD - N_IN_SHARD), (0, 0)))


def _unshard_rows(wt):
    return jnp.transpose(wt[:, :N_IN_SHARD], (0, 2, 1))


def _owner_blocks(dwt):
    blocks = jnp.pad(dwt.reshape(4, 2, N_IN_SHARD, D_MODEL), ((0, 0), (0, 0), (0, N_ROWS_PAD - N_IN_SHARD), (0, 0)))
    return jnp.transpose(blocks, (1, 0, 2, 3))


def _rope_tables(s_len):
    inv = 1.0 / (ROPE_THETA ** (jnp.arange(0, A_HEAD_DIM, 2, dtype=F32) / A_HEAD_DIM))
    ang = jnp.arange(s_len, dtype=F32)[:, None] * inv[None, :]
    cos, sin = jnp.cos(ang), jnp.sin(ang)
    return jnp.tile(cos, (1, 4)), jnp.tile(jnp.concatenate([-sin, sin], axis=1), (1, 2))


def _pack(leaves):
    rows = []
    for leaf in leaves:
        flat = leaf.reshape(-1)
        pad = (-flat.shape[0]) % (8 * LANES)
        rows.append(jnp.pad(flat, (0, pad)).reshape(-1, LANES))
    return jnp.concatenate(rows, axis=0)


def _unpack(packed, shapes):
    out, row = [], 0
    for shape in shapes:
        size = math.prod(shape)
        nrows = -(-size // (8 * LANES)) * 8
        out.append(packed[row:row + nrows].reshape(-1)[:size].reshape(shape))
        row += nrows
    return out


def _lane_row(vals, offset):
    return jnp.pad(vals, (offset, LANES - offset - vals.shape[0])).reshape(1, LANES)


def kernel(x, w_in, sinks, r_conv_w, r_conv_b, r_wa, r_ba, r_wx, r_bx, r_lam, g_conv_w, g_a_log, g_dt_bias, g_norm_w, w_out, ln_g, ln_b, loss_target, m_w_in, m_sinks, m_r_conv_w, m_r_conv_b, m_r_wa, m_r_ba, m_r_wx, m_r_bx, m_r_lam, m_g_conv_w, m_g_a_log, m_g_dt_bias, m_g_norm_w, m_w_out, m_ln_g, m_ln_b, v_w_in, v_sinks, v_r_conv_w, v_r_conv_b, v_r_wa, v_r_ba, v_r_wx, v_r_bx, v_r_lam, v_g_conv_w, v_g_a_log, v_g_dt_bias, v_g_norm_w, v_w_out, v_ln_g, v_ln_b):
    s_len = x.shape[1]
    x0 = x.reshape(s_len, D_MODEL)
    target = loss_target.reshape(s_len, D_MODEL)
    me = 4 * lax.axis_index("x") + 2 * lax.axis_index("y") + lax.axis_index("c")
    core, chip = lax.axis_index("c"), 2 * lax.axis_index("x") + lax.axis_index("y")

    win_pieces = _shard_rows(w_in).astype(MXU_DTYPE).reshape(DEPTH, 2, N_ROWS_PAD // 2, D_MODEL)
    wout_pieces = w_out.astype(MXU_DTYPE).reshape(DEPTH, 2, OUT_SHARD // 2, D_MODEL)
    win0_all, wout0_all, rcw_all, gcw_all = _all_gather(
        [win_pieces[0], wout_pieces[0], r_conv_w[None], g_conv_w[None]], "gather_weights")
    rcw_full = jnp.moveaxis(rcw_all[:, 0], 0, 2).reshape(DEPTH, CONV_WIDTH, R_WIDTH)
    gcw_full = jnp.moveaxis(gcw_all[:, 0], 0, 2).reshape(DEPTH, CONV_WIDTH, 3 * G_WIDTH)
    cos, sin = _rope_tables(s_len)

    def big_weights(win_all, wout_all):
        wt_a, wt_r, wt_g = _group_weights(win_all.reshape(N_DEV, N_ROWS_PAD, D_MODEL)[:, :N_IN_SHARD].reshape(N_IN, D_MODEL))
        wo = wout_all.reshape(D_MODEL, D_MODEL)
        return dict(wt_a=wt_a, wt_r=wt_r, wt_g=wt_g, wo=wo,
                    wo_a=wo[0:A_WIDTH], wo_r=wo[A_WIDTH:A_WIDTH + R_WIDTH], wo_g=wo[A_WIDTH + R_WIDTH:])

    layers = []
    for l in range(DEPTH):
        layers.append(dict(
            sinks_t=jnp.broadcast_to(sinks[l][:, None, None], (A_HEADS, 8, LANES)),
            rcw=rcw_full[l], rcb=r_conv_b[l].reshape(1, R_WIDTH), wa=r_wa[l], ba=r_ba[l].reshape(1, R_WIDTH),
            wx=r_wx[l], bx=r_bx[l].reshape(1, R_WIDTH), lam=r_lam[l].reshape(1, R_WIDTH),
            gcw=gcw_full[l], zero_b=jnp.zeros((1, 3 * G_WIDTH), F32),
            a_log=_lane_row(g_a_log[l], G_HEADS), dt=_lane_row(g_dt_bias[l], G_HEADS),
            norm_w=g_norm_w[l].reshape(1, G_HEAD_DIM), ln_g=ln_g[l].reshape(1, D_MODEL), ln_b=ln_b[l].reshape(1, D_MODEL)))

    saved = []
    xin = xin_lo = x0
    layers[0].update(big_weights(win0_all, wout0_all))
    for l, p in enumerate(layers):
        if l + 1 < DEPTH:
            proj_a, (wout_next,) = _matmul([xin_lo], [p["wt_a"]], name=f"proj_a{l}", tm=1024, tn=1280, b_t=True,
                                           comm=_GatherSend([wout_pieces[l + 1]]))
            proj_r, (win_next,) = _matmul([xin_lo], [p["wt_r"]], name=f"proj_r{l}", tm=1024, tn=1024, b_t=True,
                                          comm=_GatherSend([win_pieces[l + 1]]))
            proj_g, next_all = _matmul([xin_lo], [p["wt_g"]], name=f"proj_g{l}", tm=1024, tn=1152, b_t=True,
                                       comm=_GatherForward([win_next, wout_next]))
            layers[l + 1].update(big_weights(*next_all))
        else:
            proj_a = _matmul([xin_lo], [p["wt_a"]], name=f"proj_a{l}", tm=1024, tn=1280, b_t=True)
            proj_r = _matmul([xin_lo], [p["wt_r"]], name=f"proj_r{l}", tm=1024, tn=1024, b_t=True)
            proj_g = _matmul([xin_lo], [p["wt_g"]], name=f"proj_g{l}", tm=1024, tn=1152, b_t=True)
        ya = _attn_fwd(proj_a, cos, sin, p["sinks_t"], name=f"attn_fwd{l}")
        xr = _conv_fwd(proj_r, R_WIDTH, p["rcw"], p["rcb"], name=f"rconv_fwd{l}")
        h, yr = _rg_fwd(xr, proj_r, p["wa"], p["ba"], p["wx"], p["bx"], p["lam"], name=f"rglru_fwd{l}")
        conv = _conv_fwd(proj_g, 3 * G_WIDTH, p["gcw"], p["zero_b"], name=f"gconv_fwd{l}")
        *chunk_vals, inv = _gdn_chunk_fwd(conv, proj_g, p["a_log"], p["dt"], name=f"gdn_chunk_fwd{l}")
        yg, states = _gdn_scan_fwd(chunk_vals, proj_g, p["norm_w"], name=f"gdn_scan_fwd{l}")
        z, xout, xout_lo = _outproj_ln(ya, yr, yg, p["wo"], xin, p["ln_g"], p["ln_b"], name=f"outproj_ln{l}")
        saved.append(dict(xin_lo=xin_lo, proj_a=proj_a, proj_r=proj_r, proj_g=proj_g, ya=ya, yr=yr, yg=yg, xr=xr, h=h,
                          conv=conv, chunk_vals=chunk_vals, inv=inv, states=states, z=z))
        xin, xin_lo = xout, xout_lo

    grads = [None] * DEPTH
    dxn = None
    loss_local = None
    for l in reversed(range(DEPTH)):
        p, sv = layers[l], saved[l]
        if dxn is None:
            dz, dz_lo, dln_g, dln_b, loss_local = _ln_bwd(sv["z"], p["ln_g"], name=f"ln_bwd{l}", xn=xin, target=target)
        else:
            dz, dz_lo, dln_g, dln_b = _ln_bwd(sv["z"], p["ln_g"], name=f"ln_bwd{l}", dxn=dxn)
        dya = _matmul([dz_lo], [p["wo_a"]], name=f"dya{l}", tm=1024, tn=512, b_t=True)
        dyr = _matmul([dz_lo], [p["wo_r"]], name=f"dyr{l}", tm=1024, tn=1024, b_t=True)
        dyg = _matmul([dz_lo], [p["wo_g"]], name=f"dyg{l}", tm=1024, tn=512, b_t=True)
        dwo = jnp.concatenate([
            _matmul_tn(sv["ya"], dz_lo, name=f"dwo_a{l}", tm=512, tn=1024, tk=1024),
            _matmul_tn(sv["yr"], dz_lo, name=f"dwo_r{l}", tm=1024, tn=1024, tk=1024),
            _matmul_tn(sv["yg"], dz_lo, name=f"dwo_g{l}", tm=512, tn=1024, tk=1024)], axis=0)

        dproj_a, dsinks_t = _attn_bwd(sv["proj_a"], cos, sin, p["sinks_t"], dya, name=f"attn_bwd{l}")

        dxr, drz, dwa, dba, dwx, dbx, dlam = _rg_bwd(sv["xr"], sv["proj_r"], sv["h"], dyr, p["wa"], p["ba"], p["wx"],
                                                     p["bx"], p["lam"], name=f"rglru_bwd{l}")
        dproj_r, drcw, drcb = _conv_bwd(dxr, sv["proj_r"], R_WIDTH, p["rcw"], [drz], name=f"rconv_bwd{l}")

        scan_out = _gdn_scan_bwd(sv["chunk_vals"], sv["states"], sv["proj_g"], p["norm_w"], dyg, name=f"gdn_scan_bwd{l}")
        dgz, dnorm_w = scan_out[6], scan_out[7]
        dconv, dbg, dal, ddt = _gdn_chunk_bwd(sv["conv"], sv["proj_g"], p["a_log"], p["dt"], sv["inv"], scan_out[:6],
                                              name=f"gdn_chunk_bwd{l}")
        dproj_g, dgcw, _ = _conv_bwd(dconv, sv["proj_g"], 3 * G_WIDTH, p["gcw"], [dgz, dbg], name=f"gconv_bwd{l}")

        dwin = _ungroup_grads(_matmul_tn(dproj_a, sv["xin_lo"], name=f"dwin_a{l}", tm=640, tn=1024, tk=1024),
                              _matmul_tn(dproj_r, sv["xin_lo"], name=f"dwin_r{l}", tm=1024, tn=1024, tk=1024),
                              _matmul_tn(dproj_g, sv["xin_lo"], name=f"dwin_g{l}", tm=1152, tn=1024, tk=1024))

        dwin_blocks = _owner_blocks(dwin)[:, :, None].astype(MXU_DTYPE)
        dwout_blocks = jnp.transpose(dwo.reshape(4, 2, OUT_SHARD, D_MODEL), (1, 0, 2, 3))[:, :, None].astype(MXU_DTYPE)
        got_win, got_wout = _swap_cores(
            [dwin_blocks.reshape(2, 8, N_ROWS_PAD // 2, D_MODEL), dwout_blocks.reshape(2, 4, OUT_SHARD, D_MODEL)],
            f"swap_core_grads{l}")
        chip_win = _add_pair(dwin_blocks, got_win.reshape(dwin_blocks.shape[1:]), core, name=f"add_core_grads_w_in{l}",
                             rows=352).reshape(4, N_ROWS_PAD, D_MODEL)
        chip_wout = _add_pair(dwout_blocks, got_wout.reshape(dwout_blocks.shape[1:]), core, name=f"add_core_grads_w_out{l}",
                              rows=256).reshape(4, OUT_SHARD, D_MODEL)
        dxn, (win_parts, wout_parts) = _matmul(
            [dproj_a, dproj_r, dproj_g], [p["wt_a"], p["wt_r"], p["wt_g"]], name=f"dx{l}", tm=512, tn=1024, add=dz,
            add_scale=DEEPNORM_ALPHA, comm=_ChipExchange([chip_win, chip_wout]))
        grads[l] = dict(
            chip_win=chip_win, chip_wout=chip_wout, win_parts=win_parts, wout_parts=wout_parts,
            sinks=dsinks_t[:, :, 0].sum(axis=1), r_conv_w=drcw.reshape(CONV_WIDTH, R_WIDTH),
            r_conv_b=drcb.reshape(R_WIDTH), r_wa=dwa, r_ba=dba.reshape(R_WIDTH), r_wx=dwx, r_bx=dbx.reshape(R_WIDTH),
            r_lam=dlam.reshape(R_WIDTH), g_conv_w=dgcw.reshape(CONV_WIDTH, 3 * G_WIDTH),
            g_a_log=dal[0, G_HEADS:2 * G_HEADS], g_dt_bias=ddt[0, G_HEADS:2 * G_HEADS],
            g_norm_w=dnorm_w.reshape(G_HEAD_DIM), ln_g=dln_g.reshape(D_MODEL), ln_b=dln_b.reshape(D_MODEL))
    grad_x = dxn.reshape(x.shape)
    loss = lax.psum(loss_local[0, 0], ("x", "y", "c"))

    def stacked(name):
        return jnp.stack([grads[l][name] for l in range(DEPTH)])

    def per_layer(name):
        return [grads[l][name] for l in range(DEPTH)]

    w_in_t = [_unshard_rows(t) for t in _sum_adamw(per_layer("chip_win"), chip, per_layer("win_parts"), _shard_rows(w_in),
                                                   _shard_rows(m_w_in), _shard_rows(v_w_in), name="adamw_w_in", rows=176)]
    g_w_in, d_w_in, nm_w_in, nv_w_in = w_in_t
    g_w_out, d_w_out, nm_w_out, nv_w_out = _sum_adamw(per_layer("chip_wout"), chip, per_layer("wout_parts"), w_out,
                                                      m_w_out, v_w_out, name="adamw_w_out", rows=128)

    small = ["sinks", "r_conv_w", "r_conv_b", "r_wa", "r_ba", "r_wx", "r_bx", "r_lam", "g_conv_w", "g_a_log",
             "g_dt_bias", "g_norm_w", "ln_g", "ln_b"]
    full_shapes = [stacked(nm).shape for nm in small]
    packed_small = _pack([stacked(nm) for nm in small])
    (all_small,) = _all_gather([packed_small.reshape(4, packed_small.shape[0] // 4, LANES)], "gather_small_grads")
    all_small = all_small.reshape(N_DEV, packed_small.shape[0], LANES)
    g_small = dict(zip(small, _unpack(_sum_slots(all_small, name="sum_small_grads"), full_shapes)))
    g_small["r_conv_w"] = lax.dynamic_slice_in_dim(g_small["r_conv_w"], me * (R_WIDTH // N_DEV), R_WIDTH // N_DEV, axis=2)
    g_small["g_conv_w"] = lax.dynamic_slice_in_dim(g_small["g_conv_w"], me * (3 * G_WIDTH // N_DEV), 3 * G_WIDTH // N_DEV, axis=2)
    given = dict(sinks=(sinks, m_sinks, v_sinks), r_conv_w=(r_conv_w, m_r_conv_w, v_r_conv_w),
                 r_conv_b=(r_conv_b, m_r_conv_b, v_r_conv_b), r_wa=(r_wa, m_r_wa, v_r_wa), r_ba=(r_ba, m_r_ba, v_r_ba),
                 r_wx=(r_wx, m_r_wx, v_r_wx), r_bx=(r_bx, m_r_bx, v_r_bx), r_lam=(r_lam, m_r_lam, v_r_lam),
                 g_conv_w=(g_conv_w, m_g_conv_w, v_g_conv_w), g_a_log=(g_a_log, m_g_a_log, v_g_a_log),
                 g_dt_bias=(g_dt_bias, m_g_dt_bias, v_g_dt_bias), g_norm_w=(g_norm_w, m_g_norm_w, v_g_norm_w),
                 ln_g=(ln_g, m_ln_g, v_ln_g), ln_b=(ln_b, m_ln_b, v_ln_b))
    shard_shapes = [given[nm][0].shape for nm in small]
    packed = [_pack([given[nm][k] for nm in small]) for k in range(3)]
    d_p, nm_p, nv_p = _adamw_packed(packed[0], _pack([g_small[nm] for nm in small]), packed[1], packed[2], name="adamw_small")
    d_small = dict(zip(small, _unpack(d_p, shard_shapes)))
    nm_small = dict(zip(small, _unpack(nm_p, shard_shapes)))
    nv_small = dict(zip(small, _unpack(nv_p, shard_shapes)))

    order = ["w_in"] + small[:12] + ["w_out"] + small[12:]

    def leaf(big_in, big_out, table):
        return [big_in if nm == "w_in" else big_out if nm == "w_out" else table[nm] for nm in order]

    return (loss, grad_x, *leaf(g_w_in, g_w_out, g_small), *leaf(d_w_in, d_w_out, d_small),
            *leaf(nm_w_in, nm_w_out, nm_small), *leaf(nv_w_in, nv_w_out, nv_small))
```

```python
import functools
import math

import jax
import jax.numpy as jnp
from jax import lax
from jax.experimental import pallas as pl
from jax.experimental.pallas import tpu as pltpu

F32 = jnp.float32
MXU_DTYPE = jnp.bfloat16
HIGHEST = lax.Precision.HIGHEST
MESH_ID = pl.DeviceIdType.MESH

N_DEV = 8
DEPTH = 2
D_MODEL = 2048
A_HEADS, A_KV_HEADS, A_HEAD_DIM = 8, 2, 64
A_WIDTH, A_KV_WIDTH = 512, 128
A_BLOCK = 128
ROPE_THETA = 10000.0
R_WIDTH, R_BLOCKS, R_BLOCK_DIM = 1024, 8, 128
R_C = 8.0
CONV_WIDTH = 4
G_HEADS, G_HEAD_DIM, G_WIDTH, G_CHUNK = 4, 128, 512, 64
N_IN = 5384
N_IN_SHARD = N_IN // N_DEV
N_ROWS_PAD = 704
OUT_SHARD = D_MODEL // N_DEV
WA, WR, WG = 1280, 2048, 2304
G_PAD = WG - (3 * G_WIDTH + G_WIDTH + 2 * G_HEADS)
DEEPNORM_ALPHA = (2 * DEPTH) ** 0.25
LN_EPS = 1e-5
RMS_EPS = 1e-6
ADAM_LR, ADAM_B1, ADAM_B2, ADAM_EPS, ADAM_WD, ADAM_STEP = 0.001, 0.9, 0.999, 1e-08, 0.01, 10
NEG = -1e30
VMEM_LIMIT = 56 * 1024 * 1024
LANES = 128


def _call(body, **kw):
    return pl.pallas_call(body, **kw)


def _params(*sem):
    return pltpu.CompilerParams(dimension_semantics=sem, vmem_limit_bytes=VMEM_LIMIT)


def _t(x):
    return jnp.swapaxes(x, -1, -2)


def _raw_dot(a, b, ca, cb, precision=None):
    batch = tuple(range(a.ndim - 2))
    if precision is None:
        a, b = a.astype(MXU_DTYPE), b.astype(MXU_DTYPE)
    return lax.dot_general(a, b, (((ca,), (cb,)), (batch, batch)), precision=precision,
                           preferred_element_type=F32)


def _nn(a, b, precision=None):
    return _raw_dot(a, b, a.ndim - 1, b.ndim - 2, precision)


def _nt(a, b, precision=None):
    return _raw_dot(a, b, a.ndim - 1, b.ndim - 1, precision)


@jax.custom_vjp
def mm_nn(a, b):
    return _nn(a, b)


def _mm_nn_fwd(a, b):
    return _nn(a, b), (a, b)


def _mm_nn_bwd(res, g):
    a, b = res
    return _nt(g, b), _nn(_t(a), g)


mm_nn.defvjp(_mm_nn_fwd, _mm_nn_bwd)


@jax.custom_vjp
def mm_nt(a, b):
    return _nt(a, b)


def _mm_nt_fwd(a, b):
    return _nt(a, b), (a, b)


def _mm_nt_bwd(res, g):
    a, b = res
    return _nn(g, b), _nn(_t(g), a)


mm_nt.defvjp(_mm_nt_fwd, _mm_nt_bwd)


def _split(x):
    hi = x.astype(MXU_DTYPE)
    return hi, (x - hi.astype(F32)).astype(MXU_DTYPE)


def _hmm(a, b, nt=False):
    dot = _nt if nt else _nn
    return dot(a[0], b[0]) + (dot(a[0], b[1]) + dot(a[1], b[0]))


def _silu(x):
    return x * jax.nn.sigmoid(x)


def _softplus(x):
    return jnp.maximum(x, 0.0) + jnp.log1p(jnp.exp(-jnp.abs(x)))


def _one_minus_sq(log_a, a):
    x = 2.0 * log_a
    return jnp.where(x > -0.01, -x * (1.0 + 0.5 * x), 1.0 - a * a)


def _iota(shape, dim):
    return lax.broadcasted_iota(jnp.int32, shape, dim)


def _inv_unit_lower(m):
    shape = m.shape
    row, col = _iota(shape, 1), _iota(shape, 2)
    eye = (row == col).astype(F32)

    def blockdiag(size):
        return (row // size) == (col // size)

    x = -jnp.where(blockdiag(8), m, 0.0)
    xs = _split(x)
    x2s = _split(_hmm(xs, xs))
    x4s = _split(_hmm(x2s, x2s))
    inv = eye + x
    inv = inv + _hmm(_split(inv), x2s)
    inv = inv + _hmm(_split(inv), x4s)
    for size in (8, 16, 32):
        below = jnp.where(blockdiag(2 * size) & jnp.logical_not(blockdiag(size)), m, 0.0)
        invs = _split(inv)
        inv = inv - _hmm(_split(_hmm(invs, _split(below))), invs)
    return inv


@jax.custom_vjp
def _solve2(m, inv, r1, r2):
    invs = _split(inv)
    return _hmm(invs, _split(r1)), _hmm(invs, _split(r2))


def _solve2_fwd(m, inv, r1, r2):
    x1, x2 = _solve2(m, inv, r1, r2)
    return (x1, x2), (inv, x1, x2)


def _solve2_bwd(res, g):
    inv, x1, x2 = res
    inv_ts = _split(_t(inv))
    d1, d2 = _hmm(inv_ts, _split(g[0])), _hmm(inv_ts, _split(g[1]))
    dm = -(_hmm(_split(d1), _split(x1), nt=True) + _hmm(_split(d2), _split(x2), nt=True))
    return dm, jnp.zeros_like(inv), d1, d2


_solve2.defvjp(_solve2_fwd, _solve2_bwd)


def _swap_halves(x):
    n = x.shape[-1]
    lane = _iota(x.shape, x.ndim - 1)
    return jnp.where((lane & 63) < 32, pltpu.roll(x, n - 32, x.ndim - 1), pltpu.roll(x, 32, x.ndim - 1))


def _rope(x, cos, sin):
    reps = x.shape[-1] // LANES
    if reps > 1:
        cos, sin = jnp.tile(cos, (1, reps)), jnp.tile(sin, (1, reps))
    return x * cos + _swap_halves(x) * sin


def _rope_t(d, cos, sin):
    reps = d.shape[-1] // LANES
    if reps > 1:
        cos, sin = jnp.tile(cos, (1, reps)), jnp.tile(sin, (1, reps))
    return d * cos + _swap_halves(d * sin)


def _swap64(x):
    return pltpu.roll(x, 64, x.ndim - 1)


def _mesh_pos():
    return lax.axis_index("x"), lax.axis_index("y"), lax.axis_index("c")


def _all_gather(arrays, name):
    n = len(arrays)
    npieces = [a.shape[0] for a in arrays]
    pmax = max(npieces)

    def body(*refs):
        ins, outs = refs[:n], refs[n:2 * n]
        send_sems, recv_sems, local_sem = refs[2 * n:]
        x, y, c = _mesh_pos()
        me, sibling = (x, y, c), (x, y, 1 - c)
        chips = [(1 - x, y), (x, 1 - y), (1 - x, 1 - y)]

        def slot(a, pos, p):
            return outs[a].at[4 * pos[0] + 2 * pos[1] + pos[2], p]

        def copy(a, p, k, block, to, own=False):
            return pltpu.make_async_remote_copy(
                src_ref=ins[a].at[p] if own else slot(a, block, p), dst_ref=slot(a, block, p),
                send_sem=send_sems.at[a, p, k], recv_sem=recv_sems.at[a, p, k], device_id=to, device_id_type=MESH_ID)

        pieces = [(a, p) for p in range(pmax) for a in range(n) if p < npieces[a]]
        mine = [pltpu.make_async_copy(ins[a].at[p], slot(a, me, p), local_sem.at[a, p]) for a, p in pieces]
        for cp in mine:
            cp.start()
        first = []
        for a, p in pieces:
            first += [copy(a, p, 1 + j, me, (*chip, c), own=True) for j, chip in enumerate(chips)]
            first.append(copy(a, p, 0, me, sibling, own=True))
        for cp in first:
            cp.start()
        passed = []
        for a, p in pieces:
            for j, chip in enumerate(chips):
                copy(a, p, 1 + j, (*chip, c), me).wait_recv()
                cp = copy(a, p, 4 + j, (*chip, c), sibling)
                cp.start()
                passed.append(cp)
        for a, p in pieces:
            copy(a, p, 0, sibling, me).wait_recv()
            for j, chip in enumerate(chips):
                copy(a, p, 4 + j, (*chip, 1 - c), me).wait_recv()
        for cp in first + passed:
            cp.wait_send()
        for cp in mine:
            cp.wait()

    any_spec = pl.BlockSpec(memory_space=pl.ANY)
    return _call(
        body, name=name,
        out_shape=[jax.ShapeDtypeStruct((N_DEV,) + a.shape, a.dtype) for a in arrays],
        in_specs=[any_spec] * n, out_specs=[any_spec] * n,
        scratch_shapes=[pltpu.SemaphoreType.DMA((n, pmax, 7)), pltpu.SemaphoreType.DMA((n, pmax, 7)),
                        pltpu.SemaphoreType.DMA((n, pmax))],
    )(*arrays)


def _swap_cores(arrays, name):
    n = len(arrays)
    pmax = max(a.shape[1] for a in arrays)

    def body(*refs):
        ins, got = refs[:n], refs[n:2 * n]
        send_sems, recv_sems = refs[2 * n:]
        x, y, c = _mesh_pos()
        copies = [pltpu.make_async_remote_copy(
            src_ref=ins[a].at[1 - c, p], dst_ref=got[a].at[p], send_sem=send_sems.at[a, p], recv_sem=recv_sems.at[a, p],
            device_id=(x, y, 1 - c), device_id_type=MESH_ID) for a in range(n) for p in range(arrays[a].shape[1])]
        for cp in copies:
            cp.start()
        for cp in copies:
            cp.wait()

    any_spec = pl.BlockSpec(memory_space=pl.ANY)
    return _call(
        body, name=name, out_shape=[jax.ShapeDtypeStruct(a.shape[1:], a.dtype) for a in arrays],
        in_specs=[any_spec] * n, out_specs=[any_spec] * n,
        scratch_shapes=[pltpu.SemaphoreType.DMA((n, pmax)), pltpu.SemaphoreType.DMA((n, pmax))],
    )(*arrays)


class _ChipExchange:
    aliases = {}

    def __init__(self, arrays):
        self.arrays = list(arrays)
        n = len(self.arrays)
        self.out_shape = [jax.ShapeDtypeStruct((3,) + a.shape[1:], a.dtype) for a in self.arrays]
        self.scratch = [pltpu.SemaphoreType.DMA((n, 3)), pltpu.SemaphoreType.DMA((n, 3))]

    def _copies(self, ins, outs, send_sems, recv_sems):
        x, y, c = _mesh_pos()
        copies = []
        for a in range(len(self.arrays)):
            for k in range(1, 4):
                px, py = x ^ (k >> 1), y ^ (k & 1)
                copies.append(pltpu.make_async_remote_copy(
                    src_ref=ins[a].at[2 * px + py], dst_ref=outs[a].at[k - 1], send_sem=send_sems.at[a, k - 1],
                    recv_sem=recv_sems.at[a, k - 1], device_id=(px, py, c), device_id_type=MESH_ID))
        return copies

    def start(self, ins, outs, send_sems, recv_sems):
        for cp in self._copies(ins, outs, send_sems, recv_sems):
            cp.start()

    def finish(self, ins, outs, send_sems, recv_sems):
        copies = self._copies(ins, outs, send_sems, recv_sems)
        for cp in copies:
            cp.wait_recv()
        for cp in copies:
            cp.wait_send()


def _slot(pos):
    return 4 * pos[0] + 2 * pos[1] + pos[2]


class _GatherSend:
    aliases = {}

    def __init__(self, arrays):
        self.arrays = list(arrays)
        n, pmax = len(self.arrays), max(a.shape[0] for a in self.arrays)
        self.out_shape = [jax.ShapeDtypeStruct((N_DEV,) + a.shape, a.dtype) for a in self.arrays]
        self.scratch = [pltpu.SemaphoreType.DMA((n, pmax, 4)), pltpu.SemaphoreType.DMA((n, pmax, 4)),
                        pltpu.SemaphoreType.DMA((n, pmax))]

    def _copies(self, ins, outs, send_sems, recv_sems, local_sems):
        x, y, c = _mesh_pos()
        peers = [(x, y, 1 - c), (1 - x, y, c), (x, 1 - y, c), (1 - x, 1 - y, c)]
        local, remote = [], []
        for a, arr in enumerate(self.arrays):
            for p in range(arr.shape[0]):
                local.append(pltpu.make_async_copy(ins[a].at[p], outs[a].at[_slot((x, y, c)), p], local_sems.at[a, p]))
                for k, peer in enumerate(peers):
                    remote.append(pltpu.make_async_remote_copy(
                        src_ref=ins[a].at[p], dst_ref=outs[a].at[_slot((x, y, c)), p], send_sem=send_sems.at[a, p, k],
                        recv_sem=recv_sems.at[a, p, k], device_id=peer, device_id_type=MESH_ID))
        return local, remote

    def start(self, *refs):
        local, remote = self._copies(*refs)
        for cp in local + remote:
            cp.start()

    def finish(self, *refs):
        local, remote = self._copies(*refs)
        for cp in remote:
            cp.wait_recv()
        for cp in remote:
            cp.wait_send()
        for cp in local:
            cp.wait()


class _GatherForward:
    def __init__(self, gathered):
        self.arrays = list(gathered)
        n, pmax = len(self.arrays), max(a.shape[1] for a in self.arrays)
        self.out_shape = [jax.ShapeDtypeStruct(a.shape, a.dtype) for a in self.arrays]
        self.aliases = {k: k for k in range(n)}
        self.scratch = [pltpu.SemaphoreType.DMA((n, pmax, 3)), pltpu.SemaphoreType.DMA((n, pmax, 3))]

    def _copies(self, ins, outs, send_sems, recv_sems):
        x, y, c = _mesh_pos()
        copies = []
        for a, arr in enumerate(self.arrays):
            for p in range(arr.shape[1]):
                for j, chip in enumerate([(1 - x, y), (x, 1 - y), (1 - x, 1 - y)]):
                    copies.append(pltpu.make_async_remote_copy(
                        src_ref=ins[a].at[_slot((*chip, c)), p], dst_ref=outs[a].at[_slot((*chip, c)), p],
                        send_sem=send_sems.at[a, p, j], recv_sem=recv_sems.at[a, p, j], device_id=(x, y, 1 - c),
                        device_id_type=MESH_ID))
        return copies

    def start(self, *refs):
        for cp in self._copies(*refs):
            cp.start()

    def finish(self, *refs):
        copies = self._copies(*refs)
        for cp in copies:
            cp.wait_recv()
        for cp in copies:
            cp.wait_send()


def _index_operand(i):
    return jnp.reshape(i, (1,)).astype(jnp.int32)


def _add_pair(pair, other, core, *, name, rows):
    _, n_slots, n_layers, n_rows, n_cols = pair.shape
    rows = min(rows, n_rows)

    def body(c_ref, a_ref, b_ref, o_ref):
        o_ref[...] = (a_ref[0].astype(F32) + b_ref[...].astype(F32)).astype(o_ref.dtype)

    blk = pl.BlockSpec((1, 1, rows, n_cols), lambda s, l, i, c: (s, l, i, 0))
    grid_spec = pltpu.PrefetchScalarGridSpec(
        num_scalar_prefetch=1, grid=(n_slots, n_layers, n_rows // rows),
        in_specs=[pl.BlockSpec((1, 1, 1, rows, n_cols), lambda s, l, i, c: (c[0], s, l, i, 0)), blk], out_specs=blk)
    return _call(body, name=name, grid_spec=grid_spec, out_shape=jax.ShapeDtypeStruct(other.shape, pair.dtype),
                 compiler_params=_params("parallel", "parallel", "parallel"))(_index_operand(core), pair, other)


def _host(body, comm, *, name, grid, in_specs, out_specs, out_shape, args, semantics):
    if comm is None:
        return _call(body, name=name, grid=grid, in_specs=in_specs, out_specs=out_specs, out_shape=out_shape,
                     compiler_params=_params(*semantics))(*args)
    n_in, n_out, n_comm = len(in_specs), len(out_specs), len(comm.arrays)

    def hosted(*refs):
        ins, outs = refs[:n_in], refs[n_in + n_comm:n_in + n_comm + n_out]
        comm_refs = (refs[n_in:n_in + n_comm], refs[n_in + n_comm + n_out:n_in + 2 * n_comm + n_out],
                     *refs[n_in + 2 * n_comm + n_out:])
        ids = [pl.program_id(d) for d in range(len(grid))]
        first, last = ids[0] == 0, ids[0] == grid[0] - 1
        for d in range(1, len(grid)):
            first, last = first & (ids[d] == 0), last & (ids[d] == grid[d] - 1)

        @pl.when(first)
        def _():
            comm.start(*comm_refs)

        body(*ins, *outs)

        @pl.when(last)
        def _():
            comm.finish(*comm_refs)

    any_spec = pl.BlockSpec(memory_space=pl.ANY)
    outs = _call(hosted, name=name, grid=grid, in_specs=list(in_specs) + [any_spec] * n_comm,
                 out_specs=list(out_specs) + [any_spec] * n_comm, out_shape=list(out_shape) + comm.out_shape,
                 input_output_aliases={n_in + k: n_out + v for k, v in comm.aliases.items()},
                 scratch_shapes=comm.scratch, compiler_params=_params(*(("arbitrary",) * len(grid))))(*args, *comm.arrays)
    return outs[:n_out], outs[n_out:]


def _matmul(a_list, b_list, *, name, tm, tn, b_t=False, out_dtype=F32, add=None, add_scale=1.0, comm=None):
    n = len(a_list)
    m_rows, n_cols = a_list[0].shape[0], b_list[0].shape[0 if b_t else 1]
    tm, tn = min(tm, m_rows), min(tn, n_cols)

    def body(*refs):
        a_refs, b_refs = refs[:n], refs[n:2 * n]
        o_ref = refs[-1]
        acc = None
        for a_ref, b_ref in zip(a_refs, b_refs):
            part = lax.dot_general(a_ref[...].astype(MXU_DTYPE), b_ref[...].astype(MXU_DTYPE),
                                   (((1,), (1 if b_t else 0,)), ((), ())), preferred_element_type=F32)
            acc = part if acc is None else acc + part
        if add is not None:
            acc = acc + add_scale * refs[2 * n][...]
        o_ref[...] = acc.astype(o_ref.dtype)

    in_specs = [pl.BlockSpec((tm, a.shape[1]), lambda i, j: (i, 0)) for a in a_list]
    if b_t:
        in_specs += [pl.BlockSpec((tn, b.shape[1]), lambda i, j: (j, 0)) for b in b_list]
    else:
        in_specs += [pl.BlockSpec((b.shape[0], tn), lambda i, j: (0, j)) for b in b_list]
    args = list(a_list) + list(b_list)
    if add is not None:
        in_specs.append(pl.BlockSpec((tm, tn), lambda i, j: (i, j)))
        args.append(add)
    res = _host(body, comm, name=name, grid=(m_rows // tm, n_cols // tn), in_specs=in_specs,
                out_specs=[pl.BlockSpec((tm, tn), lambda i, j: (i, j))],
                out_shape=[jax.ShapeDtypeStruct((m_rows, n_cols), out_dtype)], args=args,
                semantics=("parallel", "arbitrary"))
    return res[0] if comm is None else (res[0][0], res[1])


def _matmul_tn(a, b, *, name, tm, tn, tk, comm=None):
    k_rows, m_rows = a.shape
    n_cols = b.shape[1]
    tm, tn, tk = min(tm, m_rows), min(tn, n_cols), min(tk, k_rows)
    nk = k_rows // tk

    def body(a_ref, b_ref, o_ref):
        @pl.when(pl.program_id(2) == 0)
        def _():
            o_ref[...] = jnp.zeros_like(o_ref)

        o_ref[...] += lax.dot_general(a_ref[...].astype(MXU_DTYPE), b_ref[...].astype(MXU_DTYPE),
                                      (((0,), (0,)), ((), ())), preferred_element_type=F32)

    res = _host(body, comm, name=name, grid=(m_rows // tm, n_cols // tn, nk),
                in_specs=[pl.BlockSpec((tk, tm), lambda i, j, k: (k, i)), pl.BlockSpec((tk, tn), lambda i, j, k: (k, j))],
                out_specs=[pl.BlockSpec((tm, tn), lambda i, j, k: (i, j))],
                out_shape=[jax.ShapeDtypeStruct((m_rows, n_cols), F32)], args=[a, b],
                semantics=("parallel", "parallel", "arbitrary"))
    return res[0] if comm is None else (res[0][0], res[1])


def _outproj_ln(ya, yr, yg, w_out, x, ln_g, ln_b, *, name, comm=None):
    s_len = x.shape[0]
    tm = min(256, s_len)

    def body(ya_ref, yr_ref, yg_ref, w_ref, x_ref, g_ref, b_ref, z_ref, o_ref, lo_ref):
        acc = jnp.dot(ya_ref[...], w_ref[0:A_WIDTH, :], preferred_element_type=F32)
        acc += jnp.dot(yr_ref[...], w_ref[A_WIDTH:A_WIDTH + R_WIDTH, :], preferred_element_type=F32)
        acc += jnp.dot(yg_ref[...], w_ref[A_WIDTH + R_WIDTH:, :], preferred_element_type=F32)
        z = DEEPNORM_ALPHA * x_ref[...] + acc
        z_ref[...] = z
        mu = jnp.mean(z, axis=-1, keepdims=True)
        zc = z - mu
        var = jnp.mean(zc * zc, axis=-1, keepdims=True)
        out = zc * lax.rsqrt(var + LN_EPS) * g_ref[...] + b_ref[...]
        o_ref[...] = out
        lo_ref[...] = out.astype(lo_ref.dtype)

    def rows(width):
        return pl.BlockSpec((tm, width), lambda i: (i, 0))

    def whole(shape):
        return pl.BlockSpec(shape, lambda i: (0, 0))

    return _host(
        body, comm, name=name, grid=(s_len // tm,),
        in_specs=[rows(A_WIDTH), rows(R_WIDTH), rows(G_WIDTH), whole((D_MODEL, D_MODEL)), rows(D_MODEL),
                  whole((1, D_MODEL)), whole((1, D_MODEL))],
        out_specs=[rows(D_MODEL)] * 3,
        out_shape=[jax.ShapeDtypeStruct((s_len, D_MODEL), F32)] * 2 + [jax.ShapeDtypeStruct((s_len, D_MODEL), MXU_DTYPE)],
        args=[ya, yr, yg, w_out, x, ln_g, ln_b], semantics=("parallel",))


def _ln_bwd(z, ln_g, *, name, dxn=None, xn=None, target=None):
    s_len = z.shape[0]
    tm = min(256, s_len)
    top = dxn is None

    def body(*refs):
        if top:
            z_ref, g_ref, xn_ref, t_ref, dz_ref, lo_ref, dg_ref, db_ref, loss_ref = refs
            err = xn_ref[...] - t_ref[...]
            dy = err * (1.0 / D_MODEL)
        else:
            z_ref, g_ref, dy_ref, dz_ref, lo_ref, dg_ref, db_ref = refs
            dy = dy_ref[...]
        first = pl.program_id(0) == 0

        @pl.when(first)
        def _():
            dg_ref[...] = jnp.zeros_like(dg_ref)
            db_ref[...] = jnp.zeros_like(db_ref)
            if top:
                loss_ref[...] = jnp.zeros_like(loss_ref)

        z = z_ref[...]
        mu = jnp.mean(z, axis=-1, keepdims=True)
        zc = z - mu
        rstd = lax.rsqrt(jnp.mean(zc * zc, axis=-1, keepdims=True) + LN_EPS)
        xhat = zc * rstd
        dxh = dy * g_ref[...]
        dz = rstd * (dxh - jnp.mean(dxh, axis=-1, keepdims=True) - xhat * jnp.mean(dxh * xhat, axis=-1, keepdims=True))
        dz_ref[...] = dz
        lo_ref[...] = dz.astype(lo_ref.dtype)
        dg_ref[...] += jnp.sum(dy * xhat, axis=0, keepdims=True)
        db_ref[...] += jnp.sum(dy, axis=0, keepdims=True)
        if top:
            per_row = jnp.sum(err * err, axis=-1, keepdims=True) * (0.5 / D_MODEL)
            loss_ref[...] += jnp.sum(per_row, axis=0, keepdims=True)

    rows = pl.BlockSpec((tm, D_MODEL), lambda i: (i, 0))
    vec = pl.BlockSpec((1, D_MODEL), lambda i: (0, 0))
    in_specs = [rows, vec] + ([rows, rows] if top else [rows])
    args = [z, ln_g] + ([xn, target] if top else [dxn])
    out_specs = [rows, rows, vec, vec]
    out_shape = [jax.ShapeDtypeStruct((s_len, D_MODEL), F32), jax.ShapeDtypeStruct((s_len, D_MODEL), MXU_DTYPE),
                 jax.ShapeDtypeStruct((1, D_MODEL), F32), jax.ShapeDtypeStruct((1, D_MODEL), F32)]
    if top:
        out_specs.append(pl.BlockSpec((1, 1), lambda i: (0, 0)))
        out_shape.append(jax.ShapeDtypeStruct((1, 1), F32))
    return _call(body, name=name, grid=(s_len // tm,), in_specs=in_specs, out_specs=out_specs,
                 out_shape=out_shape, compiler_params=_params("arbitrary"))(*args)


CONV_ROWS = 256
HALO = 8


def _shift_down(x, halo, s):
    if s == 0:
        return x
    ext = jnp.concatenate([halo, x], axis=0)
    return pltpu.roll(ext, s, 0)[HALO:, :]


def _shift_up(x, halo, s):
    if s == 0:
        return x
    ext = jnp.concatenate([x, halo], axis=0)
    return pltpu.roll(ext, ext.shape[0] - s, 0)[:x.shape[0], :]


def _conv_fwd(src, width, w, bias, *, name):
    s_len = src.shape[0]
    rows = min(CONV_ROWS, s_len)
    per = rows // HALO

    def body(x_ref, halo_ref, w_ref, b_ref, o_ref):
        x = x_ref[...]
        halo = jnp.where(pl.program_id(0) == 0, 0.0, halo_ref[...])
        acc = x * w_ref[3:4, :] + b_ref[...]
        for k in range(CONV_WIDTH - 1):
            acc += _shift_down(x, halo, 3 - k) * w_ref[k:k + 1, :]
        o_ref[...] = acc

    return _call(
        body, name=name, grid=(s_len // rows,),
        in_specs=[pl.BlockSpec((rows, width), lambda i: (i, 0)),
                  pl.BlockSpec((HALO, width), lambda i: (jnp.maximum(i * per - 1, 0), 0)),
                  pl.BlockSpec((CONV_WIDTH, width), lambda i: (0, 0)), pl.BlockSpec((1, width), lambda i: (0, 0))],
        out_specs=pl.BlockSpec((rows, width), lambda i: (i, 0)),
        out_shape=jax.ShapeDtypeStruct((s_len, width), F32),
        compiler_params=_params("parallel"),
    )(src, src, w, bias)


def _conv_bwd(dy, src, width, w, passthrough, *, name):
    s_len = src.shape[0]
    rows = min(CONV_ROWS, s_len)
    per = rows // HALO
    nblk = s_len // rows
    extra = [p.shape[1] for p in passthrough]
    total = width + sum(extra)

    def body(*refs):
        dy_ref, dyh_ref, x_ref, xh_ref, w_ref = refs[:5]
        p_refs = refs[5:5 + len(extra)]
        o_ref, dw_ref, db_ref = refs[5 + len(extra):]
        i = pl.program_id(0)

        @pl.when(i == 0)
        def _():
            dw_ref[...] = jnp.zeros_like(dw_ref)
            db_ref[...] = jnp.zeros_like(db_ref)

        dy = dy_ref[...]
        x = x_ref[...]
        dy_halo = jnp.where(i == nblk - 1, 0.0, dyh_ref[...])
        x_halo = jnp.where(i == 0, 0.0, xh_ref[...])
        dx = dy * w_ref[3:4, :]
        dw_ref[3] += jnp.sum(dy * x, axis=0, keepdims=True)
        for k in range(CONV_WIDTH - 1):
            dx += _shift_up(dy, dy_halo, 3 - k) * w_ref[k:k + 1, :]
            dw_ref[k] += jnp.sum(dy * _shift_down(x, x_halo, 3 - k), axis=0, keepdims=True)
        db_ref[...] += jnp.sum(dy, axis=0, keepdims=True)
        o_ref[:, 0:width] = dx.astype(o_ref.dtype)
        off = width
        for p_ref, wd in zip(p_refs, extra):
            o_ref[:, off:off + wd] = p_ref[...].astype(o_ref.dtype)
            off += wd

    in_specs = [pl.BlockSpec((rows, width), lambda i: (i, 0)),
                pl.BlockSpec((HALO, width), lambda i: (jnp.minimum((i + 1) * per, nblk * per - 1), 0)),
                pl.BlockSpec((rows, width), lambda i: (i, 0)),
                pl.BlockSpec((HALO, width), lambda i: (jnp.maximum(i * per - 1, 0), 0)),
                pl.BlockSpec((CONV_WIDTH, width), lambda i: (0, 0))]
    in_specs += [pl.BlockSpec((rows, wd), lambda i: (i, 0)) for wd in extra]
    return _call(
        body, name=name, grid=(nblk,), in_specs=in_specs,
        out_specs=[pl.BlockSpec((rows, total), lambda i: (i, 0)),
                   pl.BlockSpec((CONV_WIDTH, 1, width), lambda i: (0, 0, 0)), pl.BlockSpec((1, width), lambda i: (0, 0))],
        out_shape=[jax.ShapeDtypeStruct((s_len, total), MXU_DTYPE), jax.ShapeDtypeStruct((CONV_WIDTH, 1, width), F32),
                   jax.ShapeDtypeStruct((1, width), F32)],
        compiler_params=_params("arbitrary"),
    )(dy, dy, src, src, w, *passthrough)


def _attn_mask(first):
    i = _iota((A_BLOCK, 2 * A_BLOCK), 0)
    j = _iota((A_BLOCK, 2 * A_BLOCK), 1)
    band = (j > i) & (j <= i + A_BLOCK)
    return band & ((j >= A_BLOCK) | jnp.logical_not(first))


def _attn_group(p, mask, qg, kw, kws, vw, vws, azg, sink0, sink1):
    low = _iota(qg.shape, 1) < A_HEAD_DIM
    first_lane = (_iota((A_BLOCK, LANES), 1) == 0).astype(F32)
    out = None
    for half, sink in ((0, sink0), (1, sink1)):
        kv_head = (2 * p + half) // (A_HEADS // A_KV_HEADS)
        keep = low if half == 0 else jnp.logical_not(low)
        qm = jnp.where(keep, qg, 0.0)
        kk, vv = (kw, vw) if kv_head == half else (kws, vws)
        s = mm_nt(qm, kk) * (A_HEAD_DIM ** -0.5)
        s = jnp.where(mask, s, NEG)
        sk = jnp.sum(jnp.tile(sink, (A_BLOCK // 8, 1)) * first_lane, axis=1, keepdims=True)
        m = lax.stop_gradient(jnp.maximum(jnp.max(s, axis=1, keepdims=True), sk))
        e = jnp.exp(s - m)
        denom = jnp.sum(e, axis=1, keepdims=True) + jnp.exp(sk - m)
        o = mm_nn(e * (1.0 / denom), vv)
        o = jnp.where(keep, o, 0.0)
        out = o if out is None else out + o
    return out * _silu(azg)


def _attn_specs(s_len, rev):
    nb = s_len // A_BLOCK

    def cur(i):
        return nb - 1 - i if rev else i

    def prev(i):
        return jnp.maximum(cur(i) - 1, 0)

    def blk(width, col, which):
        return pl.BlockSpec((A_BLOCK, width), lambda i: (which(i), col))

    return [blk(A_WIDTH, 0, cur), blk(A_WIDTH, 1, cur), blk(LANES, 8, cur), blk(LANES, 9, cur),
            blk(LANES, 8, prev), blk(LANES, 9, prev), blk(LANES, 0, cur), blk(LANES, 0, cur),
            blk(LANES, 0, prev), blk(LANES, 0, prev)], cur


def _attn_fwd(proj_a, cos, sin, sinks_t, *, name):
    s_len = proj_a.shape[0]
    specs, _ = _attn_specs(s_len, False)

    def body(q_ref, az_ref, k_ref, v_ref, kp_ref, vp_ref, c_ref, s_ref, cp_ref, sp_ref, sink_ref, o_ref):
        first = pl.program_id(0) == 0
        mask = _attn_mask(first)
        qr = _rope(q_ref[...], c_ref[...], s_ref[...])
        kw = jnp.concatenate([_rope(kp_ref[...], cp_ref[...], sp_ref[...]), _rope(k_ref[...], c_ref[...], s_ref[...])], 0)
        vw = jnp.concatenate([vp_ref[...], v_ref[...]], 0)
        kws, vws = _swap64(kw), _swap64(vw)
        for p in range(A_WIDTH // LANES):
            cols = slice(p * LANES, (p + 1) * LANES)
            o = _attn_group(p, mask, qr[:, cols], kw, kws, vw, vws, az_ref[:, cols], sink_ref[2 * p], sink_ref[2 * p + 1])
            o_ref[:, cols] = o.astype(o_ref.dtype)

    return _call(
        body, name=name, grid=(s_len // A_BLOCK,),
        in_specs=specs + [pl.BlockSpec((A_HEADS, 8, LANES), lambda i: (0, 0, 0))],
        out_specs=pl.BlockSpec((A_BLOCK, A_WIDTH), lambda i: (i, 0)),
        out_shape=jax.ShapeDtypeStruct((s_len, A_WIDTH), MXU_DTYPE),
        compiler_params=_params("parallel"),
    )(proj_a, proj_a, proj_a, proj_a, proj_a, proj_a, cos, sin, cos, sin, sinks_t)


def _attn_bwd(proj_a, cos, sin, sinks_t, dya, *, name):
    s_len = proj_a.shape[0]
    specs, cur = _attn_specs(s_len, True)

    def body(q_ref, az_ref, k_ref, v_ref, kp_ref, vp_ref, c_ref, s_ref, cp_ref, sp_ref, sink_ref, dy_ref,
             o_ref, dsink_ref, dk_carry, dv_carry):
        i = pl.program_id(0)

        @pl.when(i == 0)
        def _():
            dsink_ref[...] = jnp.zeros_like(dsink_ref)
            dk_carry[...] = jnp.zeros_like(dk_carry)
            dv_carry[...] = jnp.zeros_like(dv_carry)

        first = cur(i) == 0
        mask = _attn_mask(first)
        cos_c, sin_c = c_ref[...], s_ref[...]
        qr = _rope(q_ref[...], cos_c, sin_c)
        kw = jnp.concatenate([_rope(kp_ref[...], cp_ref[...], sp_ref[...]), _rope(k_ref[...], cos_c, sin_c)], 0)
        vw = jnp.concatenate([vp_ref[...], v_ref[...]], 0)
        kws, vws = _swap64(kw), _swap64(vw)
        dkw = jnp.zeros_like(kw)
        dvw = jnp.zeros_like(vw)
        for p in range(A_WIDTH // LANES):
            cols = slice(p * LANES, (p + 1) * LANES)
            _, vjp = jax.vjp(functools.partial(_attn_group, p, mask), qr[:, cols], kw, kws, vw, vws, az_ref[:, cols],
                             sink_ref[2 * p], sink_ref[2 * p + 1])
            dq, dk1, dk2, dv1, dv2, daz, ds0, ds1 = vjp(dy_ref[:, cols])
            dkw += dk1 + _swap64(dk2)
            dvw += dv1 + _swap64(dv2)
            o_ref[:, cols] = _rope_t(dq, cos_c, sin_c).astype(o_ref.dtype)
            o_ref[:, A_WIDTH + p * LANES:A_WIDTH + (p + 1) * LANES] = daz.astype(o_ref.dtype)
            dsink_ref[2 * p] += ds0
            dsink_ref[2 * p + 1] += ds1
        o_ref[:, 2 * A_WIDTH:2 * A_WIDTH + LANES] = _rope_t(dkw[A_BLOCK:, :] + dk_carry[...], cos_c, sin_c).astype(o_ref.dtype)
        o_ref[:, 2 * A_WIDTH + LANES:] = (dvw[A_BLOCK:, :] + dv_carry[...]).astype(o_ref.dtype)
        dk_carry[...] = dkw[:A_BLOCK, :]
        dv_carry[...] = dvw[:A_BLOCK, :]

    return _call(
        body, name=name, grid=(s_len // A_BLOCK,),
        in_specs=specs + [pl.BlockSpec((A_HEADS, 8, LANES), lambda i: (0, 0, 0)),
                          pl.BlockSpec((A_BLOCK, A_WIDTH), lambda i: (cur(i), 0))],
        out_specs=[pl.BlockSpec((A_BLOCK, WA), lambda i: (cur(i), 0)),
                   pl.BlockSpec((A_HEADS, 8, LANES), lambda i: (0, 0, 0))],
        out_shape=[jax.ShapeDtypeStruct((s_len, WA), MXU_DTYPE), jax.ShapeDtypeStruct((A_HEADS, 8, LANES), F32)],
        scratch_shapes=[pltpu.VMEM((A_BLOCK, LANES), F32), pltpu.VMEM((A_BLOCK, LANES), F32)],
        compiler_params=_params("arbitrary"),
    )(proj_a, proj_a, proj_a, proj_a, proj_a, proj_a, cos, sin, cos, sin, sinks_t, dya)


RG_ROWS = 256


def _rg_gates(x, wa, ba, wx, bx, lam):
    r = jax.nn.sigmoid(mm_nn(x, wa) + ba)
    ig = jax.nn.sigmoid(mm_nn(x, wx) + bx)
    log_a = -R_C * r * _softplus(-lam)
    a = jnp.exp(log_a)
    return a, jnp.sqrt(_one_minus_sq(log_a, a)) * (ig * x)


def _rg_param_specs():
    mat = pl.BlockSpec((R_BLOCKS, R_BLOCK_DIM, R_BLOCK_DIM), lambda i: (0, 0, 0))
    vec = pl.BlockSpec((1, R_WIDTH), lambda i: (0, 0))
    return [mat, vec, mat, vec, vec]


def _rg_fwd(xr, proj_r, wa, ba, wx, bx, lam, *, name):
    s_len = xr.shape[0]
    rows = min(RG_ROWS, s_len)

    def body(x_ref, z_ref, wa_ref, ba_ref, wx_ref, bx_ref, lam_ref, h_ref, y_ref, a_buf, u_buf, carry):
        @pl.when(pl.program_id(0) == 0)
        def _():
            carry[...] = jnp.zeros_like(carry)

        for n in range(R_BLOCKS):
            cols = slice(n * R_BLOCK_DIM, (n + 1) * R_BLOCK_DIM)
            a, u = _rg_gates(x_ref[:, cols], wa_ref[n], ba_ref[:, cols], wx_ref[n], bx_ref[:, cols], lam_ref[:, cols])
            a_buf[:, cols] = a
            u_buf[:, cols] = u

        def step(t, h):
            h = a_buf[pl.ds(t, 1), :] * h + u_buf[pl.ds(t, 1), :]
            h_ref[pl.ds(t, 1), :] = h
            return h

        carry[...] = lax.fori_loop(0, rows, step, carry[...], unroll=8)
        y_ref[...] = (h_ref[...] * _silu(z_ref[...])).astype(y_ref.dtype)

    blk = pl.BlockSpec((rows, R_WIDTH), lambda i: (i, 0))
    return _call(
        body, name=name, grid=(s_len // rows,),
        in_specs=[blk, pl.BlockSpec((rows, R_WIDTH), lambda i: (i, 1))] + _rg_param_specs(),
        out_specs=[blk, blk],
        out_shape=[jax.ShapeDtypeStruct((s_len, R_WIDTH), F32), jax.ShapeDtypeStruct((s_len, R_WIDTH), MXU_DTYPE)],
        scratch_shapes=[pltpu.VMEM((rows, R_WIDTH), F32), pltpu.VMEM((rows, R_WIDTH), F32), pltpu.VMEM((1, R_WIDTH), F32)],
        compiler_params=_params("arbitrary"),
    )(xr, proj_r, wa, ba, wx, bx, lam)


def _rg_bwd(xr, proj_r, h, dyr, wa, ba, wx, bx, lam, *, name):
    s_len = xr.shape[0]
    rows = min(RG_ROWS, s_len)
    nblk = s_len // rows
    per = rows // HALO

    def cur(i):
        return nblk - 1 - i

    def body(x_ref, z_ref, h_ref, hh_ref, dy_ref, wa_ref, ba_ref, wx_ref, bx_ref, lam_ref,
             dx_ref, dz_ref, dwa_ref, dba_ref, dwx_ref, dbx_ref, dlam_ref, a_buf, g_buf, carry):
        i = pl.program_id(0)

        @pl.when(i == 0)
        def _():
            carry[...] = jnp.zeros_like(carry)
            for ref in (dwa_ref, dba_ref, dwx_ref, dbx_ref, dlam_ref):
                ref[...] = jnp.zeros_like(ref)

        z = z_ref[...]
        sig = jax.nn.sigmoid(z)
        hval = h_ref[...]
        dy = dy_ref[...]
        dz_ref[...] = dy * hval * (sig * (1.0 + z * (1.0 - sig)))
        g_buf[...] = dy * (z * sig)
        kept = []
        for n in range(R_BLOCKS):
            cols = slice(n * R_BLOCK_DIM, (n + 1) * R_BLOCK_DIM)
            x = x_ref[:, cols]
            r = jax.nn.sigmoid(_nn(x, wa_ref[n]) + ba_ref[:, cols])
            ig = jax.nn.sigmoid(_nn(x, wx_ref[n]) + bx_ref[:, cols])
            sp = _softplus(-lam_ref[:, cols])
            log_a = -R_C * r * sp
            a = jnp.exp(log_a)
            a_buf[:, cols] = a
            kept.append((x, r, ig, sp, a, jnp.sqrt(_one_minus_sq(log_a, a))))

        def step(k, c):
            t = rows - 1 - k
            g = g_buf[pl.ds(t, 1), :] + c
            g_buf[pl.ds(t, 1), :] = g
            return a_buf[pl.ds(t, 1), :] * g

        carry[...] = lax.fori_loop(0, rows, step, carry[...], unroll=8)
        h_halo = jnp.where(cur(i) == 0, 0.0, hh_ref[...])
        dh = g_buf[...]
        da = dh * _shift_down(hval, h_halo, 1)
        for n in range(R_BLOCKS):
            cols = slice(n * R_BLOCK_DIM, (n + 1) * R_BLOCK_DIM)
            x, r, ig, sp, a, s = kept[n]
            du = dh[:, cols]
            dux = du * x
            d_log_a = a * (da[:, cols] - a * (dux * ig) / s)
            d_ga = d_log_a * (-R_C * sp) * (r * (1.0 - r))
            d_gx = dux * s * (ig * (1.0 - ig))
            dx_ref[:, cols] = du * (s * ig) + _nt(d_ga, wa_ref[n]) + _nt(d_gx, wx_ref[n])
            xt = _t(x)
            dwa_ref[n] += _nn(xt, d_ga)
            dwx_ref[n] += _nn(xt, d_gx)
            dba_ref[:, cols] += jnp.sum(d_ga, axis=0, keepdims=True)
            dbx_ref[:, cols] += jnp.sum(d_gx, axis=0, keepdims=True)
            dlam_ref[:, cols] += jnp.sum(d_log_a * r, axis=0, keepdims=True) * (R_C * jax.nn.sigmoid(-lam_ref[:, cols]))

    blk = pl.BlockSpec((rows, R_WIDTH), lambda i: (cur(i), 0))
    mat = pl.BlockSpec((R_BLOCKS, R_BLOCK_DIM, R_BLOCK_DIM), lambda i: (0, 0, 0))
    vec = pl.BlockSpec((1, R_WIDTH), lambda i: (0, 0))
    return _call(
        body, name=name, grid=(nblk,),
        in_specs=[blk, pl.BlockSpec((rows, R_WIDTH), lambda i: (cur(i), 1)), blk,
                  pl.BlockSpec((HALO, R_WIDTH), lambda i: (jnp.maximum(cur(i) * per - 1, 0), 0)), blk] + _rg_param_specs(),
        out_specs=[blk, blk, mat, vec, mat, vec, vec],
        out_shape=[jax.ShapeDtypeStruct((s_len, R_WIDTH), F32)] * 2 + [
            jax.ShapeDtypeStruct((R_BLOCKS, R_BLOCK_DIM, R_BLOCK_DIM), F32), jax.ShapeDtypeStruct((1, R_WIDTH), F32),
            jax.ShapeDtypeStruct((R_BLOCKS, R_BLOCK_DIM, R_BLOCK_DIM), F32), jax.ShapeDtypeStruct((1, R_WIDTH), F32),
            jax.ShapeDtypeStruct((1, R_WIDTH), F32)],
        scratch_shapes=[pltpu.VMEM((rows, R_WIDTH), F32), pltpu.VMEM((rows, R_WIDTH), F32), pltpu.VMEM((1, R_WIDTH), F32)],
        compiler_params=_params("arbitrary"),
    )(xr, proj_r, h, h, dyr, wa, ba, wx, bx, lam)


GP_CHUNKS = 8
GP_CHUNKS_BWD = 4
GS_CHUNKS = 8


def _seg_cumsum(x, reverse):
    rows = x.shape[0]
    r = _iota(x.shape, 0) & (G_CHUNK - 1)
    s = 1
    while s < G_CHUNK:
        if reverse:
            x = x + jnp.where(r < G_CHUNK - s, pltpu.roll(x, rows - s, 0), 0.0)
        else:
            x = x + jnp.where(r >= s, pltpu.roll(x, s, 0), 0.0)
        s *= 2
    return x


def _gdn_decay(ga, a_log_row, dt_row):
    return -jnp.exp(a_log_row) * _softplus(ga + dt_row)


def _gdn_chunk(cq, ck, cv, gb, gc, inv=None):
    shape = cq.shape
    head = _iota(shape, 0) & (G_HEADS - 1)
    lane = _iota(shape, 2)
    q, k, v = _silu(cq), _silu(ck), _silu(cv)
    q = q * lax.rsqrt(jnp.sum(q * q, axis=-1, keepdims=True) + RMS_EPS) * (G_HEAD_DIM ** -0.5)
    k = k * lax.rsqrt(jnp.sum(k * k, axis=-1, keepdims=True) + RMS_EPS)
    beta = jnp.sum(jnp.where(lane == head, jax.nn.sigmoid(gb), 0.0), axis=-1, keepdims=True)
    g = jnp.sum(jnp.where(lane == head + G_HEADS, gc, 0.0), axis=-1, keepdims=True)
    sq = (shape[0], G_CHUNK, G_CHUNK)
    row, col = _iota(sq, 1), _iota(sq, 2)
    g_sq = jnp.broadcast_to(g, sq)
    decay = jnp.where(row >= col, jnp.exp(jnp.minimum(g_sq - _t(g_sq), 0.0)), 0.0)
    g_last = jnp.sum(jnp.where(_iota(g.shape, 1) == G_CHUNK - 1, g, 0.0), axis=1, keepdims=True)
    eg = jnp.exp(g)
    kb, vb = k * beta, v * beta
    m = jnp.where(row > col, mm_nt(kb, k) * decay, 0.0)
    known = inv is not None
    if not known:
        inv = _inv_unit_lower(m)
    u, w = _solve2(m, inv, vb, kb * eg)
    qk = jnp.where(row >= col, mm_nt(q, k) * decay, 0.0)
    q_dec = q * eg
    k_dec = k * jnp.exp(g_last - g)
    gl = jnp.broadcast_to(jnp.exp(g_last), (shape[0], 1, G_HEAD_DIM))
    return (u, w, qk, q_dec, k_dec, gl) if known else (u, w, qk, q_dec, k_dec, gl, inv)


def _gdn_step(state, u, w, qk, q_dec, k_dec, gl, gz, norm_w):
    v_new = u - mm_nn(w, state)
    o = mm_nn(q_dec, state) + mm_nn(qk, v_new)
    new_state = state * gl + mm_nn(_t(k_dec), v_new)
    o = o * lax.rsqrt(jnp.mean(o * o, axis=-1, keepdims=True) + RMS_EPS) * norm_w
    return o * _silu(gz), new_state


def _stack_chunks(x, heads):
    chunks = x.shape[0] // G_CHUNK
    parts = []
    for c in range(chunks):
        rows = slice(c * G_CHUNK, (c + 1) * G_CHUNK)
        for hd in range(G_HEADS):
            parts.append(x[rows, hd * LANES:(hd + 1) * LANES] if heads else x[rows, :])
    return jnp.stack(parts)


def _gdn_chunk_shapes(nch):
    b = nch * G_HEADS
    wide = jax.ShapeDtypeStruct((b, G_CHUNK, G_HEAD_DIM), F32)
    return [wide, wide, jax.ShapeDtypeStruct((b, G_CHUNK, G_CHUNK), F32), wide, wide,
            jax.ShapeDtypeStruct((b, 1, G_HEAD_DIM), F32)]


def _gdn_chunk_specs(nbatch):
    wide = pl.BlockSpec((nbatch, G_CHUNK, G_HEAD_DIM), lambda i: (i, 0, 0))
    return [wide, wide, pl.BlockSpec((nbatch, G_CHUNK, G_CHUNK), lambda i: (i, 0, 0)), wide, wide,
            pl.BlockSpec((nbatch, 1, G_HEAD_DIM), lambda i: (i, 0, 0))]


def _gdn_chunk_fwd(conv, proj_g, a_log_row, dt_row, *, name):
    s_len = conv.shape[0]
    cpg = min(GP_CHUNKS, s_len // G_CHUNK)
    rows = cpg * G_CHUNK
    nbatch = cpg * G_HEADS

    def body(c_ref, bg_ref, al_ref, dt_ref, *outs):
        bg = bg_ref[...]
        gc = _seg_cumsum(_gdn_decay(bg, al_ref[...], dt_ref[...]), False)
        res = _gdn_chunk(_stack_chunks(c_ref[:, 0:G_WIDTH], True), _stack_chunks(c_ref[:, G_WIDTH:2 * G_WIDTH], True),
                         _stack_chunks(c_ref[:, 2 * G_WIDTH:], True), _stack_chunks(bg, False), _stack_chunks(gc, False))
        for ref, val in zip(outs, res):
            ref[...] = val

    row = pl.BlockSpec((1, LANES), lambda i: (0, 0))
    return _call(
        body, name=name, grid=(s_len // rows,),
        in_specs=[pl.BlockSpec((rows, 3 * G_WIDTH), lambda i: (i, 0)),
                  pl.BlockSpec((rows, LANES), lambda i: (i, (3 * G_WIDTH + G_WIDTH) // LANES)), row, row],
        out_specs=_gdn_chunk_specs(nbatch) + [pl.BlockSpec((nbatch, G_CHUNK, G_CHUNK), lambda i: (i, 0, 0))],
        out_shape=_gdn_chunk_shapes(s_len // G_CHUNK) + [
            jax.ShapeDtypeStruct((s_len // G_CHUNK * G_HEADS, G_CHUNK, G_CHUNK), F32)],
        compiler_params=_params("parallel"),
    )(conv, proj_g, a_log_row, dt_row)


def _gdn_chunk_bwd(conv, proj_g, a_log_row, dt_row, inv, cots, *, name):
    s_len = conv.shape[0]
    cpg = min(GP_CHUNKS_BWD, s_len // G_CHUNK)
    rows = cpg * G_CHUNK
    nbatch = cpg * G_HEADS

    def unstack(x, heads):
        if heads:
            return jnp.concatenate([jnp.concatenate([x[c * G_HEADS + hd] for hd in range(G_HEADS)], axis=1)
                                    for c in range(cpg)], axis=0)
        return jnp.concatenate([sum(x[c * G_HEADS + hd] for hd in range(G_HEADS)) for c in range(cpg)], axis=0)

    def body(c_ref, bg_ref, al_ref, dt_ref, inv_ref, du, dw, dqk, dqd, dkd, dgl, dc_ref, dbg_ref, dal_ref, ddt_ref):
        @pl.when(pl.program_id(0) == 0)
        def _():
            dal_ref[...] = jnp.zeros_like(dal_ref)
            ddt_ref[...] = jnp.zeros_like(ddt_ref)

        bg = bg_ref[...]
        g_all, decay_vjp = jax.vjp(_gdn_decay, bg, al_ref[...], dt_ref[...])
        gc = _seg_cumsum(g_all, False)
        _, vjp = jax.vjp(_gdn_chunk, _stack_chunks(c_ref[:, 0:G_WIDTH], True),
                         _stack_chunks(c_ref[:, G_WIDTH:2 * G_WIDTH], True), _stack_chunks(c_ref[:, 2 * G_WIDTH:], True),
                         _stack_chunks(bg, False), _stack_chunks(gc, False), inv_ref[...])
        dq, dk, dv, dgb, dgc, _ = vjp((du[...], dw[...], dqk[...], dqd[...], dkd[...], dgl[...]))
        dc_ref[:, 0:G_WIDTH] = unstack(dq, True)
        dc_ref[:, G_WIDTH:2 * G_WIDTH] = unstack(dk, True)
        dc_ref[:, 2 * G_WIDTH:] = unstack(dv, True)
        dga, dal, ddt = decay_vjp(_seg_cumsum(unstack(dgc, False), True))
        dbg_ref[:, 0:LANES] = unstack(dgb, False) + dga
        dbg_ref[:, LANES:] = jnp.zeros((rows, LANES), F32)
        dal_ref[...] += dal
        ddt_ref[...] += ddt

    row = pl.BlockSpec((1, LANES), lambda i: (0, 0))
    return _call(
        body, name=name, grid=(s_len // rows,),
        in_specs=[pl.BlockSpec((rows, 3 * G_WIDTH), lambda i: (i, 0)),
                  pl.BlockSpec((rows, LANES), lambda i: (i, (3 * G_WIDTH + G_WIDTH) // LANES)), row, row,
                  pl.BlockSpec((nbatch, G_CHUNK, G_CHUNK), lambda i: (i, 0, 0))]
        + _gdn_chunk_specs(nbatch),
        out_specs=[pl.BlockSpec((rows, 3 * G_WIDTH), lambda i: (i, 0)), pl.BlockSpec((rows, 2 * LANES), lambda i: (i, 0)),
                   row, row],
        out_shape=[jax.ShapeDtypeStruct((s_len, 3 * G_WIDTH), F32), jax.ShapeDtypeStruct((s_len, 2 * LANES), F32),
                   jax.ShapeDtypeStruct((1, LANES), F32), jax.ShapeDtypeStruct((1, LANES), F32)],
        compiler_params=_params("arbitrary"),
    )(conv, proj_g, a_log_row, dt_row, inv, *cots)


def _gdn_scan_specs(cpg, which):
    nbatch = cpg * G_HEADS
    wide = pl.BlockSpec((nbatch, G_CHUNK, G_HEAD_DIM), lambda i: (which(i), 0, 0))
    return [wide, wide, pl.BlockSpec((nbatch, G_CHUNK, G_CHUNK), lambda i: (which(i), 0, 0)), wide, wide,
            pl.BlockSpec((nbatch, 1, G_HEAD_DIM), lambda i: (which(i), 0, 0))]


def _gz_stack(z_ref, c):
    rows = pl.ds(pl.multiple_of(c * G_CHUNK, G_CHUNK), G_CHUNK)
    return jnp.stack([z_ref[rows, hd * LANES:(hd + 1) * LANES] for hd in range(G_HEADS)])


def _gdn_scan_fwd(chunk_vals, proj_g, norm_w, *, name):
    s_len = proj_g.shape[0]
    nch = s_len // G_CHUNK
    cpg = min(GS_CHUNKS, nch)
    rows = cpg * G_CHUNK

    def body(u_ref, w_ref, qk_ref, qd_ref, kd_ref, gl_ref, z_ref, nw_ref, y_ref, st_ref, state):
        @pl.when(pl.program_id(0) == 0)
        def _():
            state[...] = jnp.zeros_like(state)

        def step(c, carry):
            b = pl.ds(pl.multiple_of(c * G_HEADS, G_HEADS), G_HEADS)
            st = state[...]
            st_ref[b] = st
            y, new_state = _gdn_step(st, u_ref[b], w_ref[b], qk_ref[b], qd_ref[b], kd_ref[b], gl_ref[b],
                                     _gz_stack(z_ref, c), nw_ref[...])
            state[...] = new_state
            rws = pl.ds(pl.multiple_of(c * G_CHUNK, G_CHUNK), G_CHUNK)
            for hd in range(G_HEADS):
                y_ref[rws, hd * LANES:(hd + 1) * LANES] = y[hd].astype(y_ref.dtype)
            return carry

        lax.fori_loop(0, cpg, step, 0, unroll=2)

    return _call(
        body, name=name, grid=(nch // cpg,),
        in_specs=_gdn_scan_specs(cpg, lambda i: i) + [
            pl.BlockSpec((rows, G_WIDTH), lambda i: (i, 3)), pl.BlockSpec((1, G_HEAD_DIM), lambda i: (0, 0))],
        out_specs=[pl.BlockSpec((rows, G_WIDTH), lambda i: (i, 0)),
                   pl.BlockSpec((cpg * G_HEADS, G_HEAD_DIM, G_HEAD_DIM), lambda i: (i, 0, 0))],
        out_shape=[jax.ShapeDtypeStruct((s_len, G_WIDTH), MXU_DTYPE),
                   jax.ShapeDtypeStruct((nch * G_HEADS, G_HEAD_DIM, G_HEAD_DIM), F32)],
        scratch_shapes=[pltpu.VMEM((G_HEADS, G_HEAD_DIM, G_HEAD_DIM), F32)],
        compiler_params=_params("arbitrary"),
    )(*chunk_vals, proj_g, norm_w)


def _gdn_scan_bwd(chunk_vals, states, proj_g, norm_w, dyg, *, name):
    s_len = proj_g.shape[0]
    nch = s_len // G_CHUNK
    cpg = min(GS_CHUNKS, nch)
    rows = cpg * G_CHUNK
    ngrid = nch // cpg

    def cur(i):
        return ngrid - 1 - i

    def body(u_ref, w_ref, qk_ref, qd_ref, kd_ref, gl_ref, st_ref, z_ref, nw_ref, dy_ref,
             du_ref, dw_ref, dqk_ref, dqd_ref, dkd_ref, dgl_ref, dz_ref, dnw_ref, dstate):
        @pl.when(pl.program_id(0) == 0)
        def _():
            dstate[...] = jnp.zeros_like(dstate)
            dnw_ref[...] = jnp.zeros_like(dnw_ref)

        def step(k, carry):
            c = cpg - 1 - k
            b = pl.ds(pl.multiple_of(c * G_HEADS, G_HEADS), G_HEADS)
            _, vjp = jax.vjp(_gdn_step, st_ref[b], u_ref[b], w_ref[b], qk_ref[b], qd_ref[b], kd_ref[b], gl_ref[b],
                             _gz_stack(z_ref, c), nw_ref[...])
            dst, du, dw, dqk, dqd, dkd, dgl, dz, dnw = vjp((_gz_stack(dy_ref, c), dstate[...]))
            dstate[...] = dst
            du_ref[b], dw_ref[b], dqk_ref[b], dqd_ref[b], dkd_ref[b], dgl_ref[b] = du, dw, dqk, dqd, dkd, dgl
            rws = pl.ds(pl.multiple_of(c * G_CHUNK, G_CHUNK), G_CHUNK)
            for hd in range(G_HEADS):
                dz_ref[rws, hd * LANES:(hd + 1) * LANES] = dz[hd]
            dnw_ref[...] += dnw
            return carry

        lax.fori_loop(0, cpg, step, 0, unroll=2)

    gate = pl.BlockSpec((rows, G_WIDTH), lambda i: (cur(i), 3))
    wide = pl.BlockSpec((rows, G_WIDTH), lambda i: (cur(i), 0))
    vec = pl.BlockSpec((1, G_HEAD_DIM), lambda i: (0, 0))
    return _call(
        body, name=name, grid=(ngrid,),
        in_specs=_gdn_scan_specs(cpg, cur) + [
            pl.BlockSpec((cpg * G_HEADS, G_HEAD_DIM, G_HEAD_DIM), lambda i: (cur(i), 0, 0)), gate, vec, wide],
        out_specs=_gdn_scan_specs(cpg, cur) + [wide, vec],
        out_shape=_gdn_chunk_shapes(nch) + [jax.ShapeDtypeStruct((s_len, G_WIDTH), F32),
                                            jax.ShapeDtypeStruct((1, G_HEAD_DIM), F32)],
        scratch_shapes=[pltpu.VMEM((G_HEADS, G_HEAD_DIM, G_HEAD_DIM), F32)],
        compiler_params=_params("arbitrary"),
    )(*chunk_vals, states, proj_g, norm_w, dyg)


def _adamw_math(w, g, m, v):
    m = ADAM_B1 * m + (1.0 - ADAM_B1) * g
    v = ADAM_B2 * v + (1.0 - ADAM_B2) * (g * g)
    m_hat = m / (1.0 - ADAM_B1 ** ADAM_STEP)
    v_hat = v / (1.0 - ADAM_B2 ** ADAM_STEP)
    delta = -ADAM_LR * (m_hat / (jnp.sqrt(v_hat) + ADAM_EPS) + ADAM_WD * w)
    return delta, m, v


def _sum_adamw(own, chip, parts, w, m, v, *, name, rows):
    n_layers, n_rows, n_cols = w.shape
    rows = min(rows, n_rows)
    n_parts = parts[0].shape[0]

    def body(c_ref, *refs):
        own_refs, part_refs = refs[:n_layers], refs[n_layers:2 * n_layers]
        w_ref, m_ref, v_ref, g_ref, d_ref, nm_ref, nv_ref = refs[2 * n_layers:]
        layer = pl.program_id(0)
        g = None
        for l in range(n_layers):
            g_l = own_refs[l][0].astype(F32)
            for k in range(n_parts):
                g_l = g_l + part_refs[l][k].astype(F32)
            g = g_l if g is None else jnp.where(layer == l, g_l, g)
        delta, new_m, new_v = _adamw_math(w_ref[0], g, m_ref[0], v_ref[0])
        g_ref[0], d_ref[0], nm_ref[0], nv_ref[0] = g, delta, new_m, new_v

    blk = pl.BlockSpec((1, rows, n_cols), lambda l, i, c: (l, i, 0))
    grid_spec = pltpu.PrefetchScalarGridSpec(
        num_scalar_prefetch=1, grid=(n_layers, n_rows // rows),
        in_specs=[pl.BlockSpec((1, rows, n_cols), lambda l, i, c: (c[0], i, 0))] * n_layers
        + [pl.BlockSpec((n_parts, rows, n_cols), lambda l, i, c: (0, i, 0))] * n_layers + [blk, blk, blk],
        out_specs=[blk] * 4)
    return _call(
        body, name=name, grid_spec=grid_spec, out_shape=[jax.ShapeDtypeStruct(w.shape, F32)] * 4,
        compiler_params=_params("parallel", "parallel"),
    )(_index_operand(chip), *own, *parts, w, m, v)


def _sum_slots(parts, *, name):
    rows = parts.shape[1]

    def body(p_ref, o_ref):
        g = p_ref[0]
        for k in range(1, N_DEV):
            g = g + p_ref[k]
        o_ref[...] = g

    return _call(body, name=name, grid=(1,),
                 in_specs=[pl.BlockSpec(parts.shape, lambda i: (0, 0, 0))],
                 out_specs=pl.BlockSpec((rows, LANES), lambda i: (0, 0)),
                 out_shape=jax.ShapeDtypeStruct((rows, LANES), F32), compiler_params=_params("arbitrary"))(parts)


def _adamw_packed(w, g, m, v, *, name):
    def body(w_ref, g_ref, m_ref, v_ref, d_ref, nm_ref, nv_ref):
        d_ref[...], nm_ref[...], nv_ref[...] = _adamw_math(w_ref[...], g_ref[...], m_ref[...], v_ref[...])

    blk = pl.BlockSpec(w.shape, lambda i: (0, 0))
    return _call(body, name=name, grid=(1,), in_specs=[blk] * 4, out_specs=[blk] * 3,
                 out_shape=[jax.ShapeDtypeStruct(w.shape, F32)] * 3, compiler_params=_params("arbitrary"))(w, g, m, v)


A_COLS = ((0, 512), (768, 1280), (512, 768))
R_COLS = ((1280, 3328),)
G_COLS = ((3328, 5384),)


def _group_weights(wt_full):
    def take(ranges):
        return jnp.concatenate([wt_full[a:b] for a, b in ranges], axis=0)

    wt_g = jnp.concatenate([take(G_COLS), jnp.zeros((G_PAD, wt_full.shape[1]), wt_full.dtype)], axis=0)
    return take(A_COLS), take(R_COLS), wt_g


def _ungroup_grads(d_a, d_r, d_g):
    return jnp.concatenate([d_a[0:512], d_a[1024:1280], d_a[512:1024], d_r, d_g[:WG - G_PAD]], axis=0)


def _shard_rows(w):
    return jnp.pad(jnp.transpose(w, (0, 2, 1)), ((0, 0), (0, N_ROWS_PAD - N_IN_SHARD), (0, 0)))


def _unshard_rows(wt):
    return jnp.transpose(wt[:, :N_IN_SHARD], (0, 2, 1))


def _owner_blocks(dwt):
    blocks = jnp.pad(dwt.reshape(4, 2, N_IN_SHARD, D_MODEL), ((0, 0), (0, 0), (0, N_ROWS_PAD - N_IN_SHARD), (0, 0)))
    return jnp.transpose(blocks, (1, 0, 2, 3))


def _rope_tables(s_len):
    inv = 1.0 / (ROPE_THETA ** (jnp.arange(0, A_HEAD_DIM, 2, dtype=F32) / A_HEAD_DIM))
    ang = jnp.arange(s_len, dtype=F32)[:, None] * inv[None, :]
    cos, sin = jnp.cos(ang), jnp.sin(ang)
    return jnp.tile(cos, (1, 4)), jnp.tile(jnp.concatenate([-sin, sin], axis=1), (1, 2))


SMALL = ("sinks", "r_conv_w", "r_conv_b", "r_wa", "r_ba", "r_wx", "r_bx", "r_lam", "g_conv_w", "g_a_log", "g_dt_bias",
         "g_norm_w", "ln_g", "ln_b")


def _pack(leaves):
    rows = []
    for leaf in leaves:
        flat = leaf.reshape(-1)
        pad = (-flat.shape[0]) % (8 * LANES)
        rows.append(jnp.pad(flat, (0, pad)).reshape(-1, LANES))
    return jnp.concatenate(rows, axis=0)


def _unpack(packed, shapes):
    out, row = [], 0
    for shape in shapes:
        size = math.prod(shape)
        nrows = -(-size // (8 * LANES)) * 8
        out.append(packed[row:row + nrows].reshape(-1)[:size].reshape(shape))
        row += nrows
    return out


def _lane_row(vals, offset):
    return jnp.pad(vals, (offset, LANES - offset - vals.shape[0])).reshape(1, LANES)


def kernel(x, w_in, sinks, r_conv_w, r_conv_b, r_wa, r_ba, r_wx, r_bx, r_lam, g_conv_w, g_a_log, g_dt_bias, g_norm_w, w_out, ln_g, ln_b, loss_target, m_w_in, m_sinks, m_r_conv_w, m_r_conv_b, m_r_wa, m_r_ba, m_r_wx, m_r_bx, m_r_lam, m_g_conv_w, m_g_a_log, m_g_dt_bias, m_g_norm_w, m_w_out, m_ln_g, m_ln_b, v_w_in, v_sinks, v_r_conv_w, v_r_conv_b, v_r_wa, v_r_ba, v_r_wx, v_r_bx, v_r_lam, v_g_conv_w, v_g_a_log, v_g_dt_bias, v_g_norm_w, v_w_out, v_ln_g, v_ln_b):
    s_len = x.shape[1]
    x0 = x.reshape(s_len, D_MODEL)
    target = loss_target.reshape(s_len, D_MODEL)
    me = 4 * lax.axis_index("x") + 2 * lax.axis_index("y") + lax.axis_index("c")
    core, chip = lax.axis_index("c"), 2 * lax.axis_index("x") + lax.axis_index("y")

    win_pieces = _shard_rows(w_in).astype(MXU_DTYPE).reshape(DEPTH, 2, N_ROWS_PAD // 2, D_MODEL)
    wout_pieces = w_out.astype(MXU_DTYPE).reshape(DEPTH, 2, OUT_SHARD // 2, D_MODEL)
    win0_all, wout0_all, rcw_all, gcw_all = _all_gather(
        [win_pieces[0], wout_pieces[0], r_conv_w[None], g_conv_w[None]], "gather_weights")
    rcw_full = jnp.moveaxis(rcw_all[:, 0], 0, 2).reshape(DEPTH, CONV_WIDTH, R_WIDTH)
    gcw_full = jnp.moveaxis(gcw_all[:, 0], 0, 2).reshape(DEPTH, CONV_WIDTH, 3 * G_WIDTH)
    cos, sin = _rope_tables(s_len)

    def big_weights(win_all, wout_all):
        wt_a, wt_r, wt_g = _group_weights(win_all.reshape(N_DEV, N_ROWS_PAD, D_MODEL)[:, :N_IN_SHARD].reshape(N_IN, D_MODEL))
        wo = wout_all.reshape(D_MODEL, D_MODEL)
        return dict(wt_a=wt_a, wt_r=wt_r, wt_g=wt_g, wo=wo,
                    wo_a=wo[0:A_WIDTH], wo_r=wo[A_WIDTH:A_WIDTH + R_WIDTH], wo_g=wo[A_WIDTH + R_WIDTH:])

    layers = []
    for l in range(DEPTH):
        layers.append(dict(
            sinks_t=jnp.broadcast_to(sinks[l][:, None, None], (A_HEADS, 8, LANES)),
            rcw=rcw_full[l], rcb=r_conv_b[l].reshape(1, R_WIDTH), wa=r_wa[l], ba=r_ba[l].reshape(1, R_WIDTH),
            wx=r_wx[l], bx=r_bx[l].reshape(1, R_WIDTH), lam=r_lam[l].reshape(1, R_WIDTH),
            gcw=gcw_full[l], zero_b=jnp.zeros((1, 3 * G_WIDTH), F32),
            a_log=_lane_row(g_a_log[l], G_HEADS), dt=_lane_row(g_dt_bias[l], G_HEADS),
            norm_w=g_norm_w[l].reshape(1, G_HEAD_DIM), ln_g=ln_g[l].reshape(1, D_MODEL), ln_b=ln_b[l].reshape(1, D_MODEL)))

    saved = []
    xin = xin_lo = x0
    layers[0].update(big_weights(win0_all, wout0_all))
    for l, p in enumerate(layers):
        if l + 1 < DEPTH:
            proj_a, (wout_next,) = _matmul([xin_lo], [p["wt_a"]], name=f"proj_a{l}", tm=1024, tn=1280, b_t=True,
                                           comm=_GatherSend([wout_pieces[l + 1]]))
            proj_r, (win_next_0,) = _matmul([xin_lo], [p["wt_r"]], name=f"proj_r{l}", tm=1024, tn=1024, b_t=True,
                                            comm=_GatherSend([win_pieces[l + 1, 0:1]]))
            proj_g, (win_next_1,) = _matmul([xin_lo], [p["wt_g"]], name=f"proj_g{l}", tm=1024, tn=1152, b_t=True,
                                            comm=_GatherSend([win_pieces[l + 1, 1:2]]))
            forward_next = _GatherForward([win_next_0, win_next_1, wout_next])
        else:
            forward_next = None
            proj_a = _matmul([xin_lo], [p["wt_a"]], name=f"proj_a{l}", tm=1024, tn=1280, b_t=True)
            proj_r = _matmul([xin_lo], [p["wt_r"]], name=f"proj_r{l}", tm=1024, tn=1024, b_t=True)
            proj_g = _matmul([xin_lo], [p["wt_g"]], name=f"proj_g{l}", tm=1024, tn=1152, b_t=True)
        ya = _attn_fwd(proj_a, cos, sin, p["sinks_t"], name=f"attn_fwd{l}")
        xr = _conv_fwd(proj_r, R_WIDTH, p["rcw"], p["rcb"], name=f"rconv_fwd{l}")
        h, yr = _rg_fwd(xr, proj_r, p["wa"], p["ba"], p["wx"], p["bx"], p["lam"], name=f"rglru_fwd{l}")
        conv = _conv_fwd(proj_g, 3 * G_WIDTH, p["gcw"], p["zero_b"], name=f"gconv_fwd{l}")
        *chunk_vals, inv = _gdn_chunk_fwd(conv, proj_g, p["a_log"], p["dt"], name=f"gdn_chunk_fwd{l}")
        yg, states = _gdn_scan_fwd(chunk_vals, proj_g, p["norm_w"], name=f"gdn_scan_fwd{l}")
        if forward_next is None:
            z, xout, xout_lo = _outproj_ln(ya, yr, yg, p["wo"], xin, p["ln_g"], p["ln_b"], name=f"outproj_ln{l}")
        else:
            (z, xout, xout_lo), (win_0, win_1, wout_all) = _outproj_ln(
                ya, yr, yg, p["wo"], xin, p["ln_g"], p["ln_b"], name=f"outproj_ln{l}", comm=forward_next)
            layers[l + 1].update(big_weights(jnp.concatenate([win_0, win_1], axis=1), wout_all))
        saved.append(dict(xin_lo=xin_lo, proj_a=proj_a, proj_r=proj_r, proj_g=proj_g, ya=ya, yr=yr, yg=yg, xr=xr, h=h,
                          conv=conv, chunk_vals=chunk_vals, inv=inv, states=states, z=z))
        xin, xin_lo = xout, xout_lo

    grads = [None] * DEPTH
    dxn = None
    loss_local = None
    for l in reversed(range(DEPTH)):
        p, sv = layers[l], saved[l]
        if dxn is None:
            dz, dz_lo, dln_g, dln_b, loss_local = _ln_bwd(sv["z"], p["ln_g"], name=f"ln_bwd{l}", xn=xin, target=target)
        else:
            dz, dz_lo, dln_g, dln_b = _ln_bwd(sv["z"], p["ln_g"], name=f"ln_bwd{l}", dxn=dxn)
        dya = _matmul([dz_lo], [p["wo_a"]], name=f"dya{l}", tm=1024, tn=512, b_t=True)
        dyr = _matmul([dz_lo], [p["wo_r"]], name=f"dyr{l}", tm=1024, tn=1024, b_t=True)
        dyg = _matmul([dz_lo], [p["wo_g"]], name=f"dyg{l}", tm=1024, tn=512, b_t=True)
        dwo = jnp.concatenate([
            _matmul_tn(sv["ya"], dz_lo, name=f"dwo_a{l}", tm=512, tn=1024, tk=1024),
            _matmul_tn(sv["yr"], dz_lo, name=f"dwo_r{l}", tm=1024, tn=1024, tk=1024),
            _matmul_tn(sv["yg"], dz_lo, name=f"dwo_g{l}", tm=512, tn=1024, tk=1024)], axis=0)

        dproj_a, dsinks_t = _attn_bwd(sv["proj_a"], cos, sin, p["sinks_t"], dya, name=f"attn_bwd{l}")

        dxr, drz, dwa, dba, dwx, dbx, dlam = _rg_bwd(sv["xr"], sv["proj_r"], sv["h"], dyr, p["wa"], p["ba"], p["wx"],
                                                     p["bx"], p["lam"], name=f"rglru_bwd{l}")
        dproj_r, drcw, drcb = _conv_bwd(dxr, sv["proj_r"], R_WIDTH, p["rcw"], [drz], name=f"rconv_bwd{l}")

        scan_out = _gdn_scan_bwd(sv["chunk_vals"], sv["states"], sv["proj_g"], p["norm_w"], dyg, name=f"gdn_scan_bwd{l}")
        dgz, dnorm_w = scan_out[6], scan_out[7]
        dconv, dbg, dal, ddt = _gdn_chunk_bwd(sv["conv"], sv["proj_g"], p["a_log"], p["dt"], sv["inv"], scan_out[:6],
                                              name=f"gdn_chunk_bwd{l}")
        dproj_g, dgcw, _ = _conv_bwd(dconv, sv["proj_g"], 3 * G_WIDTH, p["gcw"], [dgz, dbg], name=f"gconv_bwd{l}")

        grads[l] = dict(
            sinks=dsinks_t[:, :, 0].sum(axis=1), r_conv_w=drcw.reshape(CONV_WIDTH, R_WIDTH),
            r_conv_b=drcb.reshape(R_WIDTH), r_wa=dwa, r_ba=dba.reshape(R_WIDTH), r_wx=dwx, r_bx=dbx.reshape(R_WIDTH),
            r_lam=dlam.reshape(R_WIDTH), g_conv_w=dgcw.reshape(CONV_WIDTH, 3 * G_WIDTH),
            g_a_log=dal[0, G_HEADS:2 * G_HEADS], g_dt_bias=ddt[0, G_HEADS:2 * G_HEADS],
            g_norm_w=dnorm_w.reshape(G_HEAD_DIM), ln_g=dln_g.reshape(D_MODEL), ln_b=dln_b.reshape(D_MODEL))

        if l > 0:
            dwin_a = _matmul_tn(dproj_a, sv["xin_lo"], name=f"dwin_a{l}", tm=640, tn=1024, tk=1024)
            dwin_r = _matmul_tn(dproj_r, sv["xin_lo"], name=f"dwin_r{l}", tm=1024, tn=1024, tk=1024)
        else:
            packed_small = _pack([jnp.stack([grads[k][nm] for k in range(DEPTH)]) for nm in SMALL])
            dwin_a, (sent_small,) = _matmul_tn(
                dproj_a, sv["xin_lo"], name=f"dwin_a{l}", tm=640, tn=1024, tk=1024,
                comm=_GatherSend([packed_small.reshape(4, packed_small.shape[0] // 4, LANES)]))
            dwin_r, (all_small,) = _matmul_tn(dproj_r, sv["xin_lo"], name=f"dwin_r{l}", tm=1024, tn=1024, tk=1024,
                                              comm=_GatherForward([sent_small]))
        dwin = _ungroup_grads(dwin_a, dwin_r,
                              _matmul_tn(dproj_g, sv["xin_lo"], name=f"dwin_g{l}", tm=1152, tn=1024, tk=1024))

        dwin_blocks = _owner_blocks(dwin)[:, :, None].astype(MXU_DTYPE)
        dwout_blocks = jnp.transpose(dwo.reshape(4, 2, OUT_SHARD, D_MODEL), (1, 0, 2, 3))[:, :, None].astype(MXU_DTYPE)
        got_win, got_wout = _swap_cores(
            [dwin_blocks.reshape(2, 8, N_ROWS_PAD // 2, D_MODEL), dwout_blocks.reshape(2, 4, OUT_SHARD, D_MODEL)],
            f"swap_core_grads{l}")
        chip_win = _add_pair(dwin_blocks, got_win.reshape(dwin_blocks.shape[1:]), core, name=f"add_core_grads_w_in{l}",
                             rows=352).reshape(4, N_ROWS_PAD, D_MODEL)
        chip_wout = _add_pair(dwout_blocks, got_wout.reshape(dwout_blocks.shape[1:]), core, name=f"add_core_grads_w_out{l}",
                              rows=256).reshape(4, OUT_SHARD, D_MODEL)
        dxn, (win_parts, wout_parts) = _matmul(
            [dproj_a, dproj_r, dproj_g], [p["wt_a"], p["wt_r"], p["wt_g"]], name=f"dx{l}", tm=512, tn=1024, add=dz,
            add_scale=DEEPNORM_ALPHA, comm=_ChipExchange([chip_win, chip_wout]))
        grads[l].update(chip_win=chip_win, chip_wout=chip_wout, win_parts=win_parts, wout_parts=wout_parts)
    grad_x = dxn.reshape(x.shape)
    loss = lax.psum(loss_local[0, 0], ("x", "y", "c"))

    def stacked(name):
        return jnp.stack([grads[l][name] for l in range(DEPTH)])

    def per_layer(name):
        return [grads[l][name] for l in range(DEPTH)]

    w_in_t = [_unshard_rows(t) for t in _sum_adamw(per_layer("chip_win"), chip, per_layer("win_parts"), _shard_rows(w_in),
                                                   _shard_rows(m_w_in), _shard_rows(v_w_in), name="adamw_w_in", rows=176)]
    g_w_in, d_w_in, nm_w_in, nv_w_in = w_in_t
    g_w_out, d_w_out, nm_w_out, nv_w_out = _sum_adamw(per_layer("chip_wout"), chip, per_layer("wout_parts"), w_out,
                                                      m_w_out, v_w_out, name="adamw_w_out", rows=128)

    small = list(SMALL)
    full_shapes = [stacked(nm).shape for nm in small]
    all_small = all_small.reshape(N_DEV, packed_small.shape[0], LANES)
    g_small = dict(zip(small, _unpack(_sum_slots(all_small, name="sum_small_grads"), full_shapes)))
    g_small["r_conv_w"] = lax.dynamic_slice_in_dim(g_small["r_conv_w"], me * (R_WIDTH // N_DEV), R_WIDTH // N_DEV, axis=2)
    g_small["g_conv_w"] = lax.dynamic_slice_in_dim(g_small["g_conv_w"], me * (3 * G_WIDTH // N_DEV), 3 * G_WIDTH // N_DEV, axis=2)
    given = dict(sinks=(sinks, m_sinks, v_sinks), r_conv_w=(r_conv_w, m_r_conv_w, v_r_conv_w),
                 r_conv_b=(r_conv_b, m_r_conv_b, v_r_conv_b), r_wa=(r_wa, m_r_wa, v_r_wa), r_ba=(r_ba, m_r_ba, v_r_ba),
                 r_wx=(r_wx, m_r_wx, v_r_wx), r_bx=(r_bx, m_r_bx, v_r_bx), r_lam=(r_lam, m_r_lam, v_r_lam),
                 g_conv_w=(g_conv_w, m_g_conv_w, v_g_conv_w), g_a_log=(g_a_log, m_g_a_log, v_g_a_log),
                 g_dt_bias=(g_dt_bias, m_g_dt_bias, v_g_dt_bias), g_norm_w=(g_norm_w, m_g_norm_w, v_g_norm_w),
                 ln_g=(ln_g, m_ln_g, v_ln_g), ln_b=(ln_b, m_ln_b, v_ln_b))
    shard_shapes = [given[nm][0].shape for nm in small]
    packed = [_pack([given[nm][k] for nm in small]) for k in range(3)]
    d_p, nm_p, nv_p = _adamw_packed(packed[0], _pack([g_small[nm] for nm in small]), packed[1], packed[2], name="adamw_small")
    d_small = dict(zip(small, _unpack(d_p, shard_shapes)))
    nm_small = dict(zip(small, _unpack(nm_p, shard_shapes)))
    nv_small = dict(zip(small, _unpack(nv_p, shard_shapes)))

    order = ["w_in"] + small[:12] + ["w_out"] + small[12:]

    def leaf(big_in, big_out, table):
        return [big_in if nm == "w_in" else big_out if nm == "w_out" else table[nm] for nm in order]

    return (loss, grad_x, *leaf(g_w_in, g_w_out, g_small), *leaf(d_w_in, d_w_out, d_small),
            *leaf(nm_w_in, nm_w_out, nm_small), *leaf(nv_w_in, nv_w_out, nv_small))
```

```python
import functools
import math

import jax
import jax.numpy as jnp
from jax import lax
from jax.experimental import pallas as pl
from jax.experimental.pallas import tpu as pltpu

F32 = jnp.float32
MXU_DTYPE = jnp.bfloat16
HIGHEST = lax.Precision.HIGHEST
MESH_ID = pl.DeviceIdType.MESH

N_DEV = 8
DEPTH = 2
D_MODEL = 2048
A_HEADS, A_KV_HEADS, A_HEAD_DIM = 8, 2, 64
A_WIDTH, A_KV_WIDTH = 512, 128
A_BLOCK = 128
ROPE_THETA = 10000.0
R_WIDTH, R_BLOCKS, R_BLOCK_DIM = 1024, 8, 128
R_C = 8.0
CONV_WIDTH = 4
G_HEADS, G_HEAD_DIM, G_WIDTH, G_CHUNK = 4, 128, 512, 64
N_IN = 5384
N_IN_SHARD = N_IN // N_DEV
N_ROWS_PAD = 704
OUT_SHARD = D_MODEL // N_DEV
WA, WR, WG = 1280, 2048, 2304
G_PAD = WG - (3 * G_WIDTH + G_WIDTH + 2 * G_HEADS)
DEEPNORM_ALPHA = (2 * DEPTH) ** 0.25
LN_EPS = 1e-5
RMS_EPS = 1e-6
ADAM_LR, ADAM_B1, ADAM_B2, ADAM_EPS, ADAM_WD, ADAM_STEP = 0.001, 0.9, 0.999, 1e-08, 0.01, 10
NEG = -1e30
VMEM_LIMIT = 56 * 1024 * 1024
LANES = 128


def _call(body, **kw):
    return pl.pallas_call(body, **kw)


def _params(*sem):
    return pltpu.CompilerParams(dimension_semantics=sem, vmem_limit_bytes=VMEM_LIMIT)


def _t(x):
    return jnp.swapaxes(x, -1, -2)


def _raw_dot(a, b, ca, cb, precision=None):
    batch = tuple(range(a.ndim - 2))
    if precision is None:
        a, b = a.astype(MXU_DTYPE), b.astype(MXU_DTYPE)
    return lax.dot_general(a, b, (((ca,), (cb,)), (batch, batch)), precision=precision,
                           preferred_element_type=F32)


def _nn(a, b, precision=None):
    return _raw_dot(a, b, a.ndim - 1, b.ndim - 2, precision)


def _nt(a, b, precision=None):
    return _raw_dot(a, b, a.ndim - 1, b.ndim - 1, precision)


@jax.custom_vjp
def mm_nn(a, b):
    return _nn(a, b)


def _mm_nn_fwd(a, b):
    return _nn(a, b), (a, b)


def _mm_nn_bwd(res, g):
    a, b = res
    return _nt(g, b), _nn(_t(a), g)


mm_nn.defvjp(_mm_nn_fwd, _mm_nn_bwd)


@jax.custom_vjp
def mm_nt(a, b):
    return _nt(a, b)


def _mm_nt_fwd(a, b):
    return _nt(a, b), (a, b)


def _mm_nt_bwd(res, g):
    a, b = res
    return _nn(g, b), _nn(_t(g), a)


mm_nt.defvjp(_mm_nt_fwd, _mm_nt_bwd)


def _split(x):
    hi = x.astype(MXU_DTYPE)
    return hi, (x - hi.astype(F32)).astype(MXU_DTYPE)


def _hmm(a, b, nt=False):
    dot = _nt if nt else _nn
    return dot(a[0], b[0]) + (dot(a[0], b[1]) + dot(a[1], b[0]))


def _silu(x):
    return x * jax.nn.sigmoid(x)


def _softplus(x):
    return jnp.maximum(x, 0.0) + jnp.log1p(jnp.exp(-jnp.abs(x)))


def _one_minus_sq(log_a, a):
    x = 2.0 * log_a
    return jnp.where(x > -0.01, -x * (1.0 + 0.5 * x), 1.0 - a * a)


def _iota(shape, dim):
    return lax.broadcasted_iota(jnp.int32, shape, dim)


def _inv_unit_lower(m):
    shape = m.shape
    row, col = _iota(shape, 1), _iota(shape, 2)
    eye = (row == col).astype(F32)

    def blockdiag(size):
        return (row // size) == (col // size)

    x = -jnp.where(blockdiag(8), m, 0.0)
    xs = _split(x)
    x2s = _split(_hmm(xs, xs))
    x4s = _split(_hmm(x2s, x2s))
    inv = eye + x
    inv = inv + _hmm(_split(inv), x2s)
    inv = inv + _hmm(_split(inv), x4s)
    for size in (8, 16, 32):
        below = jnp.where(blockdiag(2 * size) & jnp.logical_not(blockdiag(size)), m, 0.0)
        invs = _split(inv)
        inv = inv - _hmm(_split(_hmm(invs, _split(below))), invs)
    return inv


@jax.custom_vjp
def _solve2(m, inv, r1, r2):
    invs = _split(inv)
    return _hmm(invs, _split(r1)), _hmm(invs, _split(r2))


def _solve2_fwd(m, inv, r1, r2):
    x1, x2 = _solve2(m, inv, r1, r2)
    return (x1, x2), (inv, x1, x2)


def _solve2_bwd(res, g):
    inv, x1, x2 = res
    inv_ts = _split(_t(inv))
    d1, d2 = _hmm(inv_ts, _split(g[0])), _hmm(inv_ts, _split(g[1]))
    dm = -(_hmm(_split(d1), _split(x1), nt=True) + _hmm(_split(d2), _split(x2), nt=True))
    return dm, jnp.zeros_like(inv), d1, d2


_solve2.defvjp(_solve2_fwd, _solve2_bwd)


def _swap_halves(x):
    n = x.shape[-1]
    lane = _iota(x.shape, x.ndim - 1)
    return jnp.where((lane & 63) < 32, pltpu.roll(x, n - 32, x.ndim - 1), pltpu.roll(x, 32, x.ndim - 1))


def _rope(x, cos, sin):
    reps = x.shape[-1] // LANES
    if reps > 1:
        cos, sin = jnp.tile(cos, (1, reps)), jnp.tile(sin, (1, reps))
    return x * cos + _swap_halves(x) * sin


def _rope_t(d, cos, sin):
    reps = d.shape[-1] // LANES
    if reps > 1:
        cos, sin = jnp.tile(cos, (1, reps)), jnp.tile(sin, (1, reps))
    return d * cos + _swap_halves(d * sin)


def _swap64(x):
    return pltpu.roll(x, 64, x.ndim - 1)


def _mesh_pos():
    return lax.axis_index("x"), lax.axis_index("y"), lax.axis_index("c")


def _all_gather(arrays, name):
    n = len(arrays)
    npieces = [a.shape[0] for a in arrays]
    pmax = max(npieces)

    def body(*refs):
        ins, outs = refs[:n], refs[n:2 * n]
        send_sems, recv_sems, local_sem = refs[2 * n:]
        x, y, c = _mesh_pos()
        me, sibling = (x, y, c), (x, y, 1 - c)
        chips = [(1 - x, y), (x, 1 - y), (1 - x, 1 - y)]

        def slot(a, pos, p):
            return outs[a].at[4 * pos[0] + 2 * pos[1] + pos[2], p]

        def copy(a, p, k, block, to, own=False):
            return pltpu.make_async_remote_copy(
                src_ref=ins[a].at[p] if own else slot(a, block, p), dst_ref=slot(a, block, p),
                send_sem=send_sems.at[a, p, k], recv_sem=recv_sems.at[a, p, k], device_id=to, device_id_type=MESH_ID)

        pieces = [(a, p) for p in range(pmax) for a in range(n) if p < npieces[a]]
        mine = [pltpu.make_async_copy(ins[a].at[p], slot(a, me, p), local_sem.at[a, p]) for a, p in pieces]
        for cp in mine:
            cp.start()
        first = []
        for a, p in pieces:
            first += [copy(a, p, 1 + j, me, (*chip, c), own=True) for j, chip in enumerate(chips)]
            first.append(copy(a, p, 0, me, sibling, own=True))
        for cp in first:
            cp.start()
        passed = []
        for a, p in pieces:
            for j, chip in enumerate(chips):
                copy(a, p, 1 + j, (*chip, c), me).wait_recv()
                cp = copy(a, p, 4 + j, (*chip, c), sibling)
                cp.start()
                passed.append(cp)
        for a, p in pieces:
            copy(a, p, 0, sibling, me).wait_recv()
            for j, chip in enumerate(chips):
                copy(a, p, 4 + j, (*chip, 1 - c), me).wait_recv()
        for cp in first + passed:
            cp.wait_send()
        for cp in mine:
            cp.wait()

    any_spec = pl.BlockSpec(memory_space=pl.ANY)
    return _call(
        body, name=name,
        out_shape=[jax.ShapeDtypeStruct((N_DEV,) + a.shape, a.dtype) for a in arrays],
        in_specs=[any_spec] * n, out_specs=[any_spec] * n,
        scratch_shapes=[pltpu.SemaphoreType.DMA((n, pmax, 7)), pltpu.SemaphoreType.DMA((n, pmax, 7)),
                        pltpu.SemaphoreType.DMA((n, pmax))],
    )(*arrays)


def _swap_cores(arrays, name):
    n = len(arrays)
    pmax = max(a.shape[1] for a in arrays)

    def body(*refs):
        ins, got = refs[:n], refs[n:2 * n]
        send_sems, recv_sems = refs[2 * n:]
        x, y, c = _mesh_pos()
        copies = [pltpu.make_async_remote_copy(
            src_ref=ins[a].at[1 - c, p], dst_ref=got[a].at[p], send_sem=send_sems.at[a, p], recv_sem=recv_sems.at[a, p],
            device_id=(x, y, 1 - c), device_id_type=MESH_ID) for a in range(n) for p in range(arrays[a].shape[1])]
        for cp in copies:
            cp.start()
        for cp in copies:
            cp.wait()

    any_spec = pl.BlockSpec(memory_space=pl.ANY)
    return _call(
        body, name=name, out_shape=[jax.ShapeDtypeStruct(a.shape[1:], a.dtype) for a in arrays],
        in_specs=[any_spec] * n, out_specs=[any_spec] * n,
        scratch_shapes=[pltpu.SemaphoreType.DMA((n, pmax)), pltpu.SemaphoreType.DMA((n, pmax))],
    )(*arrays)


class _ChipExchange:
    aliases = {}

    def __init__(self, arrays):
        self.arrays = list(arrays)
        n = len(self.arrays)
        self.out_shape = [jax.ShapeDtypeStruct((3,) + a.shape[1:], a.dtype) for a in self.arrays]
        self.scratch = [pltpu.SemaphoreType.DMA((n, 3)), pltpu.SemaphoreType.DMA((n, 3))]

    def _copies(self, ins, outs, send_sems, recv_sems):
        x, y, c = _mesh_pos()
        copies = []
        for a in range(len(self.arrays)):
            for k in range(1, 4):
                px, py = x ^ (k >> 1), y ^ (k & 1)
                copies.append(pltpu.make_async_remote_copy(
                    src_ref=ins[a].at[2 * px + py], dst_ref=outs[a].at[k - 1], send_sem=send_sems.at[a, k - 1],
                    recv_sem=recv_sems.at[a, k - 1], device_id=(px, py, c), device_id_type=MESH_ID))
        return copies

    def start(self, ins, outs, send_sems, recv_sems):
        for cp in self._copies(ins, outs, send_sems, recv_sems):
            cp.start()

    def finish(self, ins, outs, send_sems, recv_sems):
        copies = self._copies(ins, outs, send_sems, recv_sems)
        for cp in copies:
            cp.wait_recv()
        for cp in copies:
            cp.wait_send()


def _slot(pos):
    return 4 * pos[0] + 2 * pos[1] + pos[2]


class _GatherSend:
    aliases = {}

    def __init__(self, arrays):
        self.arrays = list(arrays)
        n, pmax = len(self.arrays), max(a.shape[0] for a in self.arrays)
        self.out_shape = [jax.ShapeDtypeStruct((N_DEV,) + a.shape, a.dtype) for a in self.arrays]
        self.scratch = [pltpu.SemaphoreType.DMA((n, pmax, 4)), pltpu.SemaphoreType.DMA((n, pmax, 4)),
                        pltpu.SemaphoreType.DMA((n, pmax))]

    def _copies(self, ins, outs, send_sems, recv_sems, local_sems):
        x, y, c = _mesh_pos()
        peers = [(x, y, 1 - c), (1 - x, y, c), (x, 1 - y, c), (1 - x, 1 - y, c)]
        local, remote = [], []
        for a, arr in enumerate(self.arrays):
            for p in range(arr.shape[0]):
                local.append(pltpu.make_async_copy(ins[a].at[p], outs[a].at[_slot((x, y, c)), p], local_sems.at[a, p]))
                for k, peer in enumerate(peers):
                    remote.append(pltpu.make_async_remote_copy(
                        src_ref=ins[a].at[p], dst_ref=outs[a].at[_slot((x, y, c)), p], send_sem=send_sems.at[a, p, k],
                        recv_sem=recv_sems.at[a, p, k], device_id=peer, device_id_type=MESH_ID))
        return local, remote

    def start(self, *refs):
        local, remote = self._copies(*refs)
        for cp in local + remote:
            cp.start()

    def finish(self, *refs):
        local, remote = self._copies(*refs)
        for cp in remote:
            cp.wait_recv()
        for cp in remote:
            cp.wait_send()
        for cp in local:
            cp.wait()


class _GatherForward:
    def __init__(self, gathered):
        self.arrays = list(gathered)
        n, pmax = len(self.arrays), max(a.shape[1] for a in self.arrays)
        self.out_shape = [jax.ShapeDtypeStruct(a.shape, a.dtype) for a in self.arrays]
        self.aliases = {k: k for k in range(n)}
        self.scratch = [pltpu.SemaphoreType.DMA((n, pmax, 3)), pltpu.SemaphoreType.DMA((n, pmax, 3))]

    def _copies(self, ins, outs, send_sems, recv_sems):
        x, y, c = _mesh_pos()
        copies = []
        for a, arr in enumerate(self.arrays):
            for p in range(arr.shape[1]):
                for j, chip in enumerate([(1 - x, y), (x, 1 - y), (1 - x, 1 - y)]):
                    copies.append(pltpu.make_async_remote_copy(
                        src_ref=ins[a].at[_slot((*chip, c)), p], dst_ref=outs[a].at[_slot((*chip, c)), p],
                        send_sem=send_sems.at[a, p, j], recv_sem=recv_sems.at[a, p, j], device_id=(x, y, 1 - c),
                        device_id_type=MESH_ID))
        return copies

    def start(self, *refs):
        for cp in self._copies(*refs):
            cp.start()

    def finish(self, *refs):
        copies = self._copies(*refs)
        for cp in copies:
            cp.wait_recv()
        for cp in copies:
            cp.wait_send()


def _index_operand(i):
    return jnp.reshape(i, (1,)).astype(jnp.int32)


def _add_pair(pair, other, core, *, name, rows):
    _, n_slots, n_layers, n_rows, n_cols = pair.shape
    rows = min(rows, n_rows)

    def body(c_ref, a_ref, b_ref, o_ref):
        o_ref[...] = (a_ref[0].astype(F32) + b_ref[...].astype(F32)).astype(o_ref.dtype)

    blk = pl.BlockSpec((1, 1, rows, n_cols), lambda s, l, i, c: (s, l, i, 0))
    grid_spec = pltpu.PrefetchScalarGridSpec(
        num_scalar_prefetch=1, grid=(n_slots, n_layers, n_rows // rows),
        in_specs=[pl.BlockSpec((1, 1, 1, rows, n_cols), lambda s, l, i, c: (c[0], s, l, i, 0)), blk], out_specs=blk)
    return _call(body, name=name, grid_spec=grid_spec, out_shape=jax.ShapeDtypeStruct(other.shape, pair.dtype),
                 compiler_params=_params("parallel", "parallel", "parallel"))(_index_operand(core), pair, other)


def _host(body, comm, *, name, grid, in_specs, out_specs, out_shape, args, semantics):
    if comm is None:
        return _call(body, name=name, grid=grid, in_specs=in_specs, out_specs=out_specs, out_shape=out_shape,
                     compiler_params=_params(*semantics))(*args)
    n_in, n_out, n_comm = len(in_specs), len(out_specs), len(comm.arrays)

    def hosted(*refs):
        ins, outs = refs[:n_in], refs[n_in + n_comm:n_in + n_comm + n_out]
        comm_refs = (refs[n_in:n_in + n_comm], refs[n_in + n_comm + n_out:n_in + 2 * n_comm + n_out],
                     *refs[n_in + 2 * n_comm + n_out:])
        ids = [pl.program_id(d) for d in range(len(grid))]
        first, last = ids[0] == 0, ids[0] == grid[0] - 1
        for d in range(1, len(grid)):
            first, last = first & (ids[d] == 0), last & (ids[d] == grid[d] - 1)

        @pl.when(first)
        def _():
            comm.start(*comm_refs)

        body(*ins, *outs)

        @pl.when(last)
        def _():
            comm.finish(*comm_refs)

    any_spec = pl.BlockSpec(memory_space=pl.ANY)
    outs = _call(hosted, name=name, grid=grid, in_specs=list(in_specs) + [any_spec] * n_comm,
                 out_specs=list(out_specs) + [any_spec] * n_comm, out_shape=list(out_shape) + comm.out_shape,
                 input_output_aliases={n_in + k: n_out + v for k, v in comm.aliases.items()},
                 scratch_shapes=comm.scratch, compiler_params=_params(*(("arbitrary",) * len(grid))))(*args, *comm.arrays)
    return outs[:n_out], outs[n_out:]


def _matmul(a_list, b_list, *, name, tm, tn, b_t=False, out_dtype=F32, add=None, add_scale=1.0, comm=None):
    n = len(a_list)
    m_rows, n_cols = a_list[0].shape[0], b_list[0].shape[0 if b_t else 1]
    tm, tn = min(tm, m_rows), min(tn, n_cols)

    def body(*refs):
        a_refs, b_refs = refs[:n], refs[n:2 * n]
        o_ref = refs[-1]
        acc = None
        for a_ref, b_ref in zip(a_refs, b_refs):
            part = lax.dot_general(a_ref[...].astype(MXU_DTYPE), b_ref[...].astype(MXU_DTYPE),
                                   (((1,), (1 if b_t else 0,)), ((), ())), preferred_element_type=F32)
            acc = part if acc is None else acc + part
        if add is not None:
            acc = acc + add_scale * refs[2 * n][...]
        o_ref[...] = acc.astype(o_ref.dtype)

    in_specs = [pl.BlockSpec((tm, a.shape[1]), lambda i, j: (i, 0)) for a in a_list]
    if b_t:
        in_specs += [pl.BlockSpec((tn, b.shape[1]), lambda i, j: (j, 0)) for b in b_list]
    else:
        in_specs += [pl.BlockSpec((b.shape[0], tn), lambda i, j: (0, j)) for b in b_list]
    args = list(a_list) + list(b_list)
    if add is not None:
        in_specs.append(pl.BlockSpec((tm, tn), lambda i, j: (i, j)))
        args.append(add)
    res = _host(body, comm, name=name, grid=(m_rows // tm, n_cols // tn), in_specs=in_specs,
                out_specs=[pl.BlockSpec((tm, tn), lambda i, j: (i, j))],
                out_shape=[jax.ShapeDtypeStruct((m_rows, n_cols), out_dtype)], args=args,
                semantics=("parallel", "arbitrary"))
    return res[0] if comm is None else (res[0][0], res[1])


def _matmul_tn(a, b, *, name, tm, tn, tk, comm=None):
    k_rows, m_rows = a.shape
    n_cols = b.shape[1]
    tm, tn, tk = min(tm, m_rows), min(tn, n_cols), min(tk, k_rows)
    nk = k_rows // tk

    def body(a_ref, b_ref, o_ref):
        @pl.when(pl.program_id(2) == 0)
        def _():
            o_ref[...] = jnp.zeros_like(o_ref)

        o_ref[...] += lax.dot_general(a_ref[...].astype(MXU_DTYPE), b_ref[...].astype(MXU_DTYPE),
                                      (((0,), (0,)), ((), ())), preferred_element_type=F32)

    res = _host(body, comm, name=name, grid=(m_rows // tm, n_cols // tn, nk),
                in_specs=[pl.BlockSpec((tk, tm), lambda i, j, k: (k, i)), pl.BlockSpec((tk, tn), lambda i, j, k: (k, j))],
                out_specs=[pl.BlockSpec((tm, tn), lambda i, j, k: (i, j))],
                out_shape=[jax.ShapeDtypeStruct((m_rows, n_cols), F32)], args=[a, b],
                semantics=("parallel", "parallel", "arbitrary"))
    return res[0] if comm is None else (res[0][0], res[1])


def _outproj_ln(ya, yr, yg, w_out, x, ln_g, ln_b, *, name, comm=None):
    s_len = x.shape[0]
    tm = min(256, s_len)

    def body(ya_ref, yr_ref, yg_ref, w_ref, x_ref, g_ref, b_ref, z_ref, o_ref, lo_ref):
        acc = jnp.dot(ya_ref[...], w_ref[0:A_WIDTH, :], preferred_element_type=F32)
        acc += jnp.dot(yr_ref[...], w_ref[A_WIDTH:A_WIDTH + R_WIDTH, :], preferred_element_type=F32)
        acc += jnp.dot(yg_ref[...], w_ref[A_WIDTH + R_WIDTH:, :], preferred_element_type=F32)
        z = DEEPNORM_ALPHA * x_ref[...] + acc
        z_ref[...] = z
        mu = jnp.mean(z, axis=-1, keepdims=True)
        zc = z - mu
        var = jnp.mean(zc * zc, axis=-1, keepdims=True)
        out = zc * lax.rsqrt(var + LN_EPS) * g_ref[...] + b_ref[...]
        o_ref[...] = out
        lo_ref[...] = out.astype(lo_ref.dtype)

    def rows(width):
        return pl.BlockSpec((tm, width), lambda i: (i, 0))

    def whole(shape):
        return pl.BlockSpec(shape, lambda i: (0, 0))

    return _host(
        body, comm, name=name, grid=(s_len // tm,),
        in_specs=[rows(A_WIDTH), rows(R_WIDTH), rows(G_WIDTH), whole((D_MODEL, D_MODEL)), rows(D_MODEL),
                  whole((1, D_MODEL)), whole((1, D_MODEL))],
        out_specs=[rows(D_MODEL)] * 3,
        out_shape=[jax.ShapeDtypeStruct((s_len, D_MODEL), F32)] * 2 + [jax.ShapeDtypeStruct((s_len, D_MODEL), MXU_DTYPE)],
        args=[ya, yr, yg, w_out, x, ln_g, ln_b], semantics=("parallel",))


def _ln_bwd(z, ln_g, *, name, dxn=None, xn=None, target=None):
    s_len = z.shape[0]
    tm = min(256, s_len)
    top = dxn is None

    def body(*refs):
        if top:
            z_ref, g_ref, xn_ref, t_ref, dz_ref, lo_ref, dg_ref, db_ref, loss_ref = refs
            err = xn_ref[...] - t_ref[...]
            dy = err * (1.0 / D_MODEL)
        else:
            z_ref, g_ref, dy_ref, dz_ref, lo_ref, dg_ref, db_ref = refs
            dy = dy_ref[...]
        first = pl.program_id(0) == 0

        @pl.when(first)
        def _():
            dg_ref[...] = jnp.zeros_like(dg_ref)
            db_ref[...] = jnp.zeros_like(db_ref)
            if top:
                loss_ref[...] = jnp.zeros_like(loss_ref)

        z = z_ref[...]
        mu = jnp.mean(z, axis=-1, keepdims=True)
        zc = z - mu
        rstd = lax.rsqrt(jnp.mean(zc * zc, axis=-1, keepdims=True) + LN_EPS)
        xhat = zc * rstd
        dxh = dy * g_ref[...]
        dz = rstd * (dxh - jnp.mean(dxh, axis=-1, keepdims=True) - xhat * jnp.mean(dxh * xhat, axis=-1, keepdims=True))
        dz_ref[...] = dz
        lo_ref[...] = dz.astype(lo_ref.dtype)
        dg_ref[...] += jnp.sum(dy * xhat, axis=0, keepdims=True)
        db_ref[...] += jnp.sum(dy, axis=0, keepdims=True)
        if top:
            per_row = jnp.sum(err * err, axis=-1, keepdims=True) * (0.5 / D_MODEL)
            loss_ref[...] += jnp.sum(per_row, axis=0, keepdims=True)

    rows = pl.BlockSpec((tm, D_MODEL), lambda i: (i, 0))
    vec = pl.BlockSpec((1, D_MODEL), lambda i: (0, 0))
    in_specs = [rows, vec] + ([rows, rows] if top else [rows])
    args = [z, ln_g] + ([xn, target] if top else [dxn])
    out_specs = [rows, rows, vec, vec]
    out_shape = [jax.ShapeDtypeStruct((s_len, D_MODEL), F32), jax.ShapeDtypeStruct((s_len, D_MODEL), MXU_DTYPE),
                 jax.ShapeDtypeStruct((1, D_MODEL), F32), jax.ShapeDtypeStruct((1, D_MODEL), F32)]
    if top:
        out_specs.append(pl.BlockSpec((1, 1), lambda i: (0, 0)))
        out_shape.append(jax.ShapeDtypeStruct((1, 1), F32))
    return _call(body, name=name, grid=(s_len // tm,), in_specs=in_specs, out_specs=out_specs,
                 out_shape=out_shape, compiler_params=_params("arbitrary"))(*args)


CONV_ROWS = 512
HALO = 8


def _shift_down(x, halo, s):
    if s == 0:
        return x
    ext = jnp.concatenate([halo, x], axis=0)
    return pltpu.roll(ext, s, 0)[HALO:, :]


def _shift_up(x, halo, s):
    if s == 0:
        return x
    ext = jnp.concatenate([x, halo], axis=0)
    return pltpu.roll(ext, ext.shape[0] - s, 0)[:x.shape[0], :]


def _conv_fwd(src, width, w, bias, *, name):
    s_len = src.shape[0]
    rows = min(CONV_ROWS, s_len)
    per = rows // HALO

    def body(x_ref, halo_ref, w_ref, b_ref, o_ref):
        x = x_ref[...]
        halo = jnp.where(pl.program_id(0) == 0, 0.0, halo_ref[...])
        acc = x * w_ref[3:4, :] + b_ref[...]
        for k in range(CONV_WIDTH - 1):
            acc += _shift_down(x, halo, 3 - k) * w_ref[k:k + 1, :]
        o_ref[...] = acc

    return _call(
        body, name=name, grid=(s_len // rows,),
        in_specs=[pl.BlockSpec((rows, width), lambda i: (i, 0)),
                  pl.BlockSpec((HALO, width), lambda i: (jnp.maximum(i * per - 1, 0), 0)),
                  pl.BlockSpec((CONV_WIDTH, width), lambda i: (0, 0)), pl.BlockSpec((1, width), lambda i: (0, 0))],
        out_specs=pl.BlockSpec((rows, width), lambda i: (i, 0)),
        out_shape=jax.ShapeDtypeStruct((s_len, width), F32),
        compiler_params=_params("parallel"),
    )(src, src, w, bias)


def _conv_bwd(dy, src, width, w, passthrough, *, name):
    s_len = src.shape[0]
    rows = min(CONV_ROWS, s_len)
    per = rows // HALO
    nblk = s_len // rows
    extra = [p.shape[1] for p in passthrough]
    total = width + sum(extra)

    def body(*refs):
        dy_ref, dyh_ref, x_ref, xh_ref, w_ref = refs[:5]
        p_refs = refs[5:5 + len(extra)]
        o_ref, dw_ref, db_ref = refs[5 + len(extra):]
        i = pl.program_id(0)

        @pl.when(i == 0)
        def _():
            dw_ref[...] = jnp.zeros_like(dw_ref)
            db_ref[...] = jnp.zeros_like(db_ref)

        dy = dy_ref[...]
        x = x_ref[...]
        dy_halo = jnp.where(i == nblk - 1, 0.0, dyh_ref[...])
        x_halo = jnp.where(i == 0, 0.0, xh_ref[...])
        dx = dy * w_ref[3:4, :]
        dw_ref[3] += jnp.sum(dy * x, axis=0, keepdims=True)
        for k in range(CONV_WIDTH - 1):
            dx += _shift_up(dy, dy_halo, 3 - k) * w_ref[k:k + 1, :]
            dw_ref[k] += jnp.sum(dy * _shift_down(x, x_halo, 3 - k), axis=0, keepdims=True)
        db_ref[...] += jnp.sum(dy, axis=0, keepdims=True)
        o_ref[:, 0:width] = dx.astype(o_ref.dtype)
        off = width
        for p_ref, wd in zip(p_refs, extra):
            o_ref[:, off:off + wd] = p_ref[...].astype(o_ref.dtype)
            off += wd

    in_specs = [pl.BlockSpec((rows, width), lambda i: (i, 0)),
                pl.BlockSpec((HALO, width), lambda i: (jnp.minimum((i + 1) * per, nblk * per - 1), 0)),
                pl.BlockSpec((rows, width), lambda i: (i, 0)),
                pl.BlockSpec((HALO, width), lambda i: (jnp.maximum(i * per - 1, 0), 0)),
                pl.BlockSpec((CONV_WIDTH, width), lambda i: (0, 0))]
    in_specs += [pl.BlockSpec((rows, wd), lambda i: (i, 0)) for wd in extra]
    return _call(
        body, name=name, grid=(nblk,), in_specs=in_specs,
        out_specs=[pl.BlockSpec((rows, total), lambda i: (i, 0)),
                   pl.BlockSpec((CONV_WIDTH, 1, width), lambda i: (0, 0, 0)), pl.BlockSpec((1, width), lambda i: (0, 0))],
        out_shape=[jax.ShapeDtypeStruct((s_len, total), MXU_DTYPE), jax.ShapeDtypeStruct((CONV_WIDTH, 1, width), F32),
                   jax.ShapeDtypeStruct((1, width), F32)],
        compiler_params=_params("arbitrary"),
    )(dy, dy, src, src, w, *passthrough)


def _attn_mask(first):
    i = _iota((A_BLOCK, 2 * A_BLOCK), 0)
    j = _iota((A_BLOCK, 2 * A_BLOCK), 1)
    band = (j > i) & (j <= i + A_BLOCK)
    return band & ((j >= A_BLOCK) | jnp.logical_not(first))


def _attn_group(p, mask, qg, kw, kws, vw, vws, azg, sink0, sink1):
    low = _iota(qg.shape, 1) < A_HEAD_DIM
    first_lane = (_iota((A_BLOCK, LANES), 1) == 0).astype(F32)
    out = None
    for half, sink in ((0, sink0), (1, sink1)):
        kv_head = (2 * p + half) // (A_HEADS // A_KV_HEADS)
        keep = low if half == 0 else jnp.logical_not(low)
        qm = jnp.where(keep, qg, 0.0)
        kk, vv = (kw, vw) if kv_head == half else (kws, vws)
        s = mm_nt(qm, kk) * (A_HEAD_DIM ** -0.5)
        s = jnp.where(mask, s, NEG)
        sk = jnp.sum(jnp.tile(sink, (A_BLOCK // 8, 1)) * first_lane, axis=1, keepdims=True)
        m = lax.stop_gradient(jnp.maximum(jnp.max(s, axis=1, keepdims=True), sk))
        e = jnp.exp(s - m)
        denom = jnp.sum(e, axis=1, keepdims=True) + jnp.exp(sk - m)
        o = mm_nn(e * (1.0 / denom), vv)
        o = jnp.where(keep, o, 0.0)
        out = o if out is None else out + o
    return out * _silu(azg)


def _attn_specs(s_len, rev):
    nb = s_len // A_BLOCK

    def cur(i):
        return nb - 1 - i if rev else i

    def prev(i):
        return jnp.maximum(cur(i) - 1, 0)

    def blk(width, col, which):
        return pl.BlockSpec((A_BLOCK, width), lambda i: (which(i), col))

    return [blk(A_WIDTH, 0, cur), blk(A_WIDTH, 1, cur), blk(LANES, 8, cur), blk(LANES, 9, cur),
            blk(LANES, 8, prev), blk(LANES, 9, prev), blk(LANES, 0, cur), blk(LANES, 0, cur),
            blk(LANES, 0, prev), blk(LANES, 0, prev)], cur


def _attn_fwd(proj_a, cos, sin, sinks_t, *, name):
    s_len = proj_a.shape[0]
    specs, _ = _attn_specs(s_len, False)

    def body(q_ref, az_ref, k_ref, v_ref, kp_ref, vp_ref, c_ref, s_ref, cp_ref, sp_ref, sink_ref, o_ref):
        first = pl.program_id(0) == 0
        mask = _attn_mask(first)
        qr = _rope(q_ref[...], c_ref[...], s_ref[...])
        kw = jnp.concatenate([_rope(kp_ref[...], cp_ref[...], sp_ref[...]), _rope(k_ref[...], c_ref[...], s_ref[...])], 0)
        vw = jnp.concatenate([vp_ref[...], v_ref[...]], 0)
        kws, vws = _swap64(kw), _swap64(vw)
        for p in range(A_WIDTH // LANES):
            cols = slice(p * LANES, (p + 1) * LANES)
            o = _attn_group(p, mask, qr[:, cols], kw, kws, vw, vws, az_ref[:, cols], sink_ref[2 * p], sink_ref[2 * p + 1])
            o_ref[:, cols] = o.astype(o_ref.dtype)

    return _call(
        body, name=name, grid=(s_len // A_BLOCK,),
        in_specs=specs + [pl.BlockSpec((A_HEADS, 8, LANES), lambda i: (0, 0, 0))],
        out_specs=pl.BlockSpec((A_BLOCK, A_WIDTH), lambda i: (i, 0)),
        out_shape=jax.ShapeDtypeStruct((s_len, A_WIDTH), MXU_DTYPE),
        compiler_params=_params("parallel"),
    )(proj_a, proj_a, proj_a, proj_a, proj_a, proj_a, cos, sin, cos, sin, sinks_t)


def _attn_bwd(proj_a, cos, sin, sinks_t, dya, *, name):
    s_len = proj_a.shape[0]
    specs, cur = _attn_specs(s_len, True)

    def body(q_ref, az_ref, k_ref, v_ref, kp_ref, vp_ref, c_ref, s_ref, cp_ref, sp_ref, sink_ref, dy_ref,
             o_ref, dsink_ref, dk_carry, dv_carry):
        i = pl.program_id(0)

        @pl.when(i == 0)
        def _():
            dsink_ref[...] = jnp.zeros_like(dsink_ref)
            dk_carry[...] = jnp.zeros_like(dk_carry)
            dv_carry[...] = jnp.zeros_like(dv_carry)

        first = cur(i) == 0
        mask = _attn_mask(first)
        cos_c, sin_c = c_ref[...], s_ref[...]
        qr = _rope(q_ref[...], cos_c, sin_c)
        kw = jnp.concatenate([_rope(kp_ref[...], cp_ref[...], sp_ref[...]), _rope(k_ref[...], cos_c, sin_c)], 0)
        vw = jnp.concatenate([vp_ref[...], v_ref[...]], 0)
        kws, vws = _swap64(kw), _swap64(vw)
        dkw = jnp.zeros_like(kw)
        dvw = jnp.zeros_like(vw)
        for p in range(A_WIDTH // LANES):
            cols = slice(p * LANES, (p + 1) * LANES)
            _, vjp = jax.vjp(functools.partial(_attn_group, p, mask), qr[:, cols], kw, kws, vw, vws, az_ref[:, cols],
                             sink_ref[2 * p], sink_ref[2 * p + 1])
            dq, dk1, dk2, dv1, dv2, daz, ds0, ds1 = vjp(dy_ref[:, cols])
            dkw += dk1 + _swap64(dk2)
            dvw += dv1 + _swap64(dv2)
            o_ref[:, cols] = _rope_t(dq, cos_c, sin_c).astype(o_ref.dtype)
            o_ref[:, A_WIDTH + p * LANES:A_WIDTH + (p + 1) * LANES] = daz.astype(o_ref.dtype)
            dsink_ref[2 * p] += ds0
            dsink_ref[2 * p + 1] += ds1
        o_ref[:, 2 * A_WIDTH:2 * A_WIDTH + LANES] = _rope_t(dkw[A_BLOCK:, :] + dk_carry[...], cos_c, sin_c).astype(o_ref.dtype)
        o_ref[:, 2 * A_WIDTH + LANES:] = (dvw[A_BLOCK:, :] + dv_carry[...]).astype(o_ref.dtype)
        dk_carry[...] = dkw[:A_BLOCK, :]
        dv_carry[...] = dvw[:A_BLOCK, :]

    return _call(
        body, name=name, grid=(s_len // A_BLOCK,),
        in_specs=specs + [pl.BlockSpec((A_HEADS, 8, LANES), lambda i: (0, 0, 0)),
                          pl.BlockSpec((A_BLOCK, A_WIDTH), lambda i: (cur(i), 0))],
        out_specs=[pl.BlockSpec((A_BLOCK, WA), lambda i: (cur(i), 0)),
                   pl.BlockSpec((A_HEADS, 8, LANES), lambda i: (0, 0, 0))],
        out_shape=[jax.ShapeDtypeStruct((s_len, WA), MXU_DTYPE), jax.ShapeDtypeStruct((A_HEADS, 8, LANES), F32)],
        scratch_shapes=[pltpu.VMEM((A_BLOCK, LANES), F32), pltpu.VMEM((A_BLOCK, LANES), F32)],
        compiler_params=_params("arbitrary"),
    )(proj_a, proj_a, proj_a, proj_a, proj_a, proj_a, cos, sin, cos, sin, sinks_t, dya)


RG_ROWS = 256


def _rg_gates(x, wa, ba, wx, bx, lam):
    r = jax.nn.sigmoid(mm_nn(x, wa) + ba)
    ig = jax.nn.sigmoid(mm_nn(x, wx) + bx)
    log_a = -R_C * r * _softplus(-lam)
    a = jnp.exp(log_a)
    return a, jnp.sqrt(_one_minus_sq(log_a, a)) * (ig * x)


def _rg_param_specs():
    mat = pl.BlockSpec((R_BLOCKS, R_BLOCK_DIM, R_BLOCK_DIM), lambda i: (0, 0, 0))
    vec = pl.BlockSpec((1, R_WIDTH), lambda i: (0, 0))
    return [mat, vec, mat, vec, vec]


def _rg_fwd(xr, proj_r, wa, ba, wx, bx, lam, *, name):
    s_len = xr.shape[0]
    rows = min(RG_ROWS, s_len)

    def body(x_ref, z_ref, wa_ref, ba_ref, wx_ref, bx_ref, lam_ref, h_ref, y_ref, a_buf, u_buf, carry):
        @pl.when(pl.program_id(0) == 0)
        def _():
            carry[...] = jnp.zeros_like(carry)

        for n in range(R_BLOCKS):
            cols = slice(n * R_BLOCK_DIM, (n + 1) * R_BLOCK_DIM)
            a, u = _rg_gates(x_ref[:, cols], wa_ref[n], ba_ref[:, cols], wx_ref[n], bx_ref[:, cols], lam_ref[:, cols])
            a_buf[:, cols] = a
            u_buf[:, cols] = u

        def step(t, h):
            h = a_buf[pl.ds(t, 1), :] * h + u_buf[pl.ds(t, 1), :]
            h_ref[pl.ds(t, 1), :] = h
            return h

        carry[...] = lax.fori_loop(0, rows, step, carry[...], unroll=16)
        y_ref[...] = (h_ref[...] * _silu(z_ref[...])).astype(y_ref.dtype)

    blk = pl.BlockSpec((rows, R_WIDTH), lambda i: (i, 0))
    return _call(
        body, name=name, grid=(s_len // rows,),
        in_specs=[blk, pl.BlockSpec((rows, R_WIDTH), lambda i: (i, 1))] + _rg_param_specs(),
        out_specs=[blk, blk],
        out_shape=[jax.ShapeDtypeStruct((s_len, R_WIDTH), F32), jax.ShapeDtypeStruct((s_len, R_WIDTH), MXU_DTYPE)],
        scratch_shapes=[pltpu.VMEM((rows, R_WIDTH), F32), pltpu.VMEM((rows, R_WIDTH), F32), pltpu.VMEM((1, R_WIDTH), F32)],
        compiler_params=_params("arbitrary"),
    )(xr, proj_r, wa, ba, wx, bx, lam)


def _rg_bwd(xr, proj_r, h, dyr, wa, ba, wx, bx, lam, *, name):
    s_len = xr.shape[0]
    rows = min(RG_ROWS, s_len)
    nblk = s_len // rows
    per = rows // HALO

    def cur(i):
        return nblk - 1 - i

    def body(x_ref, z_ref, h_ref, hh_ref, dy_ref, wa_ref, ba_ref, wx_ref, bx_ref, lam_ref,
             dx_ref, dz_ref, dwa_ref, dba_ref, dwx_ref, dbx_ref, dlam_ref, a_buf, g_buf, carry):
        i = pl.program_id(0)

        @pl.when(i == 0)
        def _():
            carry[...] = jnp.zeros_like(carry)
            for ref in (dwa_ref, dba_ref, dwx_ref, dbx_ref, dlam_ref):
                ref[...] = jnp.zeros_like(ref)

        z = z_ref[...]
        sig = jax.nn.sigmoid(z)
        hval = h_ref[...]
        dy = dy_ref[...]
        dz_ref[...] = dy * hval * (sig * (1.0 + z * (1.0 - sig)))
        g_buf[...] = dy * (z * sig)
        kept = []
        for n in range(R_BLOCKS):
            cols = slice(n * R_BLOCK_DIM, (n + 1) * R_BLOCK_DIM)
            x = x_ref[:, cols]
            r = jax.nn.sigmoid(_nn(x, wa_ref[n]) + ba_ref[:, cols])
            ig = jax.nn.sigmoid(_nn(x, wx_ref[n]) + bx_ref[:, cols])
            sp = _softplus(-lam_ref[:, cols])
            log_a = -R_C * r * sp
            a = jnp.exp(log_a)
            a_buf[:, cols] = a
            kept.append((x, r, ig, sp, a, jnp.sqrt(_one_minus_sq(log_a, a))))

        def step(k, c):
            t = rows - 1 - k
            g = g_buf[pl.ds(t, 1), :] + c
            g_buf[pl.ds(t, 1), :] = g
            return a_buf[pl.ds(t, 1), :] * g

        carry[...] = lax.fori_loop(0, rows, step, carry[...], unroll=16)
        h_halo = jnp.where(cur(i) == 0, 0.0, hh_ref[...])
        dh = g_buf[...]
        da = dh * _shift_down(hval, h_halo, 1)
        for n in range(R_BLOCKS):
            cols = slice(n * R_BLOCK_DIM, (n + 1) * R_BLOCK_DIM)
            x, r, ig, sp, a, s = kept[n]
            du = dh[:, cols]
            dux = du * x
            d_log_a = a * (da[:, cols] - a * (dux * ig) / s)
            d_ga = d_log_a * (-R_C * sp) * (r * (1.0 - r))
            d_gx = dux * s * (ig * (1.0 - ig))
            dx_ref[:, cols] = du * (s * ig) + _nt(d_ga, wa_ref[n]) + _nt(d_gx, wx_ref[n])
            xt = _t(x)
            dwa_ref[n] += _nn(xt, d_ga)
            dwx_ref[n] += _nn(xt, d_gx)
            dba_ref[:, cols] += jnp.sum(d_ga, axis=0, keepdims=True)
            dbx_ref[:, cols] += jnp.sum(d_gx, axis=0, keepdims=True)
            dlam_ref[:, cols] += jnp.sum(d_log_a * r, axis=0, keepdims=True) * (R_C * jax.nn.sigmoid(-lam_ref[:, cols]))

    blk = pl.BlockSpec((rows, R_WIDTH), lambda i: (cur(i), 0))
    mat = pl.BlockSpec((R_BLOCKS, R_BLOCK_DIM, R_BLOCK_DIM), lambda i: (0, 0, 0))
    vec = pl.BlockSpec((1, R_WIDTH), lambda i: (0, 0))
    return _call(
        body, name=name, grid=(nblk,),
        in_specs=[blk, pl.BlockSpec((rows, R_WIDTH), lambda i: (cur(i), 1)), blk,
                  pl.BlockSpec((HALO, R_WIDTH), lambda i: (jnp.maximum(cur(i) * per - 1, 0), 0)), blk] + _rg_param_specs(),
        out_specs=[blk, blk, mat, vec, mat, vec, vec],
        out_shape=[jax.ShapeDtypeStruct((s_len, R_WIDTH), F32)] * 2 + [
            jax.ShapeDtypeStruct((R_BLOCKS, R_BLOCK_DIM, R_BLOCK_DIM), F32), jax.ShapeDtypeStruct((1, R_WIDTH), F32),
            jax.ShapeDtypeStruct((R_BLOCKS, R_BLOCK_DIM, R_BLOCK_DIM), F32), jax.ShapeDtypeStruct((1, R_WIDTH), F32),
            jax.ShapeDtypeStruct((1, R_WIDTH), F32)],
        scratch_shapes=[pltpu.VMEM((rows, R_WIDTH), F32), pltpu.VMEM((rows, R_WIDTH), F32), pltpu.VMEM((1, R_WIDTH), F32)],
        compiler_params=_params("arbitrary"),
    )(xr, proj_r, h, h, dyr, wa, ba, wx, bx, lam)


GP_CHUNKS = 8
GP_CHUNKS_BWD = 4
GS_CHUNKS = 8


def _seg_cumsum(x, reverse):
    rows = x.shape[0]
    r = _iota(x.shape, 0) & (G_CHUNK - 1)
    s = 1
    while s < G_CHUNK:
        if reverse:
            x = x + jnp.where(r < G_CHUNK - s, pltpu.roll(x, rows - s, 0), 0.0)
        else:
            x = x + jnp.where(r >= s, pltpu.roll(x, s, 0), 0.0)
        s *= 2
    return x


def _gdn_decay(ga, a_log_row, dt_row):
    return -jnp.exp(a_log_row) * _softplus(ga + dt_row)


def _gdn_chunk(cq, ck, cv, gb, gc, inv=None):
    shape = cq.shape
    head = _iota(shape, 0) & (G_HEADS - 1)
    lane = _iota(shape, 2)
    q, k, v = _silu(cq), _silu(ck), _silu(cv)
    q = q * lax.rsqrt(jnp.sum(q * q, axis=-1, keepdims=True) + RMS_EPS) * (G_HEAD_DIM ** -0.5)
    k = k * lax.rsqrt(jnp.sum(k * k, axis=-1, keepdims=True) + RMS_EPS)
    beta = jnp.sum(jnp.where(lane == head, jax.nn.sigmoid(gb), 0.0), axis=-1, keepdims=True)
    g = jnp.sum(jnp.where(lane == head + G_HEADS, gc, 0.0), axis=-1, keepdims=True)
    sq = (shape[0], G_CHUNK, G_CHUNK)
    row, col = _iota(sq, 1), _iota(sq, 2)
    g_sq = jnp.broadcast_to(g, sq)
    decay = jnp.where(row >= col, jnp.exp(jnp.minimum(g_sq - _t(g_sq), 0.0)), 0.0)
    g_last = jnp.sum(jnp.where(_iota(g.shape, 1) == G_CHUNK - 1, g, 0.0), axis=1, keepdims=True)
    eg = jnp.exp(g)
    kb, vb = k * beta, v * beta
    m = jnp.where(row > col, mm_nt(kb, k) * decay, 0.0)
    known = inv is not None
    if not known:
        inv = _inv_unit_lower(m)
    u, w = _solve2(m, inv, vb, kb * eg)
    qk = jnp.where(row >= col, mm_nt(q, k) * decay, 0.0)
    q_dec = q * eg
    k_dec = k * jnp.exp(g_last - g)
    gl = jnp.broadcast_to(jnp.exp(g_last), (shape[0], 1, G_HEAD_DIM))
    return (u, w, qk, q_dec, k_dec, gl) if known else (u, w, qk, q_dec, k_dec, gl, inv)


def _gdn_step(state, u, w, qk, q_dec, k_dec, gl, gz, norm_w):
    v_new = u - mm_nn(w, state)
    o = mm_nn(q_dec, state) + mm_nn(qk, v_new)
    new_state = state * gl + mm_nn(_t(k_dec), v_new)
    o = o * lax.rsqrt(jnp.mean(o * o, axis=-1, keepdims=True) + RMS_EPS) * norm_w
    return o * _silu(gz), new_state


def _stack_chunks(x, heads):
    chunks = x.shape[0] // G_CHUNK
    parts = []
    for c in range(chunks):
        rows = slice(c * G_CHUNK, (c + 1) * G_CHUNK)
        for hd in range(G_HEADS):
            parts.append(x[rows, hd * LANES:(hd + 1) * LANES] if heads else x[rows, :])
    return jnp.stack(parts)


def _gdn_chunk_shapes(nch):
    b = nch * G_HEADS
    wide = jax.ShapeDtypeStruct((b, G_CHUNK, G_HEAD_DIM), F32)
    return [wide, wide, jax.ShapeDtypeStruct((b, G_CHUNK, G_CHUNK), F32), wide, wide,
            jax.ShapeDtypeStruct((b, 1, G_HEAD_DIM), F32)]


def _gdn_chunk_specs(nbatch):
    wide = pl.BlockSpec((nbatch, G_CHUNK, G_HEAD_DIM), lambda i: (i, 0, 0))
    return [wide, wide, pl.BlockSpec((nbatch, G_CHUNK, G_CHUNK), lambda i: (i, 0, 0)), wide, wide,
            pl.BlockSpec((nbatch, 1, G_HEAD_DIM), lambda i: (i, 0, 0))]


def _gdn_chunk_fwd(conv, proj_g, a_log_row, dt_row, *, name):
    s_len = conv.shape[0]
    cpg = min(GP_CHUNKS, s_len // G_CHUNK)
    rows = cpg * G_CHUNK
    nbatch = cpg * G_HEADS

    def body(c_ref, bg_ref, al_ref, dt_ref, *outs):
        bg = bg_ref[...]
        gc = _seg_cumsum(_gdn_decay(bg, al_ref[...], dt_ref[...]), False)
        res = _gdn_chunk(_stack_chunks(c_ref[:, 0:G_WIDTH], True), _stack_chunks(c_ref[:, G_WIDTH:2 * G_WIDTH], True),
                         _stack_chunks(c_ref[:, 2 * G_WIDTH:], True), _stack_chunks(bg, False), _stack_chunks(gc, False))
        for ref, val in zip(outs, res):
            ref[...] = val

    row = pl.BlockSpec((1, LANES), lambda i: (0, 0))
    return _call(
        body, name=name, grid=(s_len // rows,),
        in_specs=[pl.BlockSpec((rows, 3 * G_WIDTH), lambda i: (i, 0)),
                  pl.BlockSpec((rows, LANES), lambda i: (i, (3 * G_WIDTH + G_WIDTH) // LANES)), row, row],
        out_specs=_gdn_chunk_specs(nbatch) + [pl.BlockSpec((nbatch, G_CHUNK, G_CHUNK), lambda i: (i, 0, 0))],
        out_shape=_gdn_chunk_shapes(s_len // G_CHUNK) + [
            jax.ShapeDtypeStruct((s_len // G_CHUNK * G_HEADS, G_CHUNK, G_CHUNK), F32)],
        compiler_params=_params("parallel"),
    )(conv, proj_g, a_log_row, dt_row)


def _gdn_chunk_bwd(conv, proj_g, a_log_row, dt_row, inv, cots, *, name):
    s_len = conv.shape[0]
    cpg = min(GP_CHUNKS_BWD, s_len // G_CHUNK)
    rows = cpg * G_CHUNK
    nbatch = cpg * G_HEADS

    def unstack(x, heads):
        if heads:
            return jnp.concatenate([jnp.concatenate([x[c * G_HEADS + hd] for hd in range(G_HEADS)], axis=1)
                                    for c in range(cpg)], axis=0)
        return jnp.concatenate([sum(x[c * G_HEADS + hd] for hd in range(G_HEADS)) for c in range(cpg)], axis=0)

    def body(c_ref, bg_ref, al_ref, dt_ref, inv_ref, du, dw, dqk, dqd, dkd, dgl, dc_ref, dbg_ref, dal_ref, ddt_ref):
        @pl.when(pl.program_id(0) == 0)
        def _():
            dal_ref[...] = jnp.zeros_like(dal_ref)
            ddt_ref[...] = jnp.zeros_like(ddt_ref)

        bg = bg_ref[...]
        g_all, decay_vjp = jax.vjp(_gdn_decay, bg, al_ref[...], dt_ref[...])
        gc = _seg_cumsum(g_all, False)
        _, vjp = jax.vjp(_gdn_chunk, _stack_chunks(c_ref[:, 0:G_WIDTH], True),
                         _stack_chunks(c_ref[:, G_WIDTH:2 * G_WIDTH], True), _stack_chunks(c_ref[:, 2 * G_WIDTH:], True),
                         _stack_chunks(bg, False), _stack_chunks(gc, False), inv_ref[...])
        dq, dk, dv, dgb, dgc, _ = vjp((du[...], dw[...], dqk[...], dqd[...], dkd[...], dgl[...]))
        dc_ref[:, 0:G_WIDTH] = unstack(dq, True)
        dc_ref[:, G_WIDTH:2 * G_WIDTH] = unstack(dk, True)
        dc_ref[:, 2 * G_WIDTH:] = unstack(dv, True)
        dga, dal, ddt = decay_vjp(_seg_cumsum(unstack(dgc, False), True))
        dbg_ref[:, 0:LANES] = unstack(dgb, False) + dga
        dbg_ref[:, LANES:] = jnp.zeros((rows, LANES), F32)
        dal_ref[...] += dal
        ddt_ref[...] += ddt

    row = pl.BlockSpec((1, LANES), lambda i: (0, 0))
    return _call(
        body, name=name, grid=(s_len // rows,),
        in_specs=[pl.BlockSpec((rows, 3 * G_WIDTH), lambda i: (i, 0)),
                  pl.BlockSpec((rows, LANES), lambda i: (i, (3 * G_WIDTH + G_WIDTH) // LANES)), row, row,
                  pl.BlockSpec((nbatch, G_CHUNK, G_CHUNK), lambda i: (i, 0, 0))]
        + _gdn_chunk_specs(nbatch),
        out_specs=[pl.BlockSpec((rows, 3 * G_WIDTH), lambda i: (i, 0)), pl.BlockSpec((rows, 2 * LANES), lambda i: (i, 0)),
                   row, row],
        out_shape=[jax.ShapeDtypeStruct((s_len, 3 * G_WIDTH), F32), jax.ShapeDtypeStruct((s_len, 2 * LANES), F32),
                   jax.ShapeDtypeStruct((1, LANES), F32), jax.ShapeDtypeStruct((1, LANES), F32)],
        compiler_params=_params("arbitrary"),
    )(conv, proj_g, a_log_row, dt_row, inv, *cots)


def _gdn_scan_specs(cpg, which):
    nbatch = cpg * G_HEADS
    wide = pl.BlockSpec((nbatch, G_CHUNK, G_HEAD_DIM), lambda i: (which(i), 0, 0))
    return [wide, wide, pl.BlockSpec((nbatch, G_CHUNK, G_CHUNK), lambda i: (which(i), 0, 0)), wide, wide,
            pl.BlockSpec((nbatch, 1, G_HEAD_DIM), lambda i: (which(i), 0, 0))]


def _gz_stack(z_ref, c):
    rows = pl.ds(pl.multiple_of(c * G_CHUNK, G_CHUNK), G_CHUNK)
    return jnp.stack([z_ref[rows, hd * LANES:(hd + 1) * LANES] for hd in range(G_HEADS)])


def _gdn_scan_fwd(chunk_vals, proj_g, norm_w, *, name):
    s_len = proj_g.shape[0]
    nch = s_len // G_CHUNK
    cpg = min(GS_CHUNKS, nch)
    rows = cpg * G_CHUNK

    def body(u_ref, w_ref, qk_ref, qd_ref, kd_ref, gl_ref, z_ref, nw_ref, y_ref, st_ref, state):
        @pl.when(pl.program_id(0) == 0)
        def _():
            state[...] = jnp.zeros_like(state)

        def step(c, carry):
            b = pl.ds(pl.multiple_of(c * G_HEADS, G_HEADS), G_HEADS)
            st = state[...]
            st_ref[b] = st
            y, new_state = _gdn_step(st, u_ref[b], w_ref[b], qk_ref[b], qd_ref[b], kd_ref[b], gl_ref[b],
                                     _gz_stack(z_ref, c), nw_ref[...])
            state[...] = new_state
            rws = pl.ds(pl.multiple_of(c * G_CHUNK, G_CHUNK), G_CHUNK)
            for hd in range(G_HEADS):
                y_ref[rws, hd * LANES:(hd + 1) * LANES] = y[hd].astype(y_ref.dtype)
            return carry

        lax.fori_loop(0, cpg, step, 0, unroll=2)

    return _call(
        body, name=name, grid=(nch // cpg,),
        in_specs=_gdn_scan_specs(cpg, lambda i: i) + [
            pl.BlockSpec((rows, G_WIDTH), lambda i: (i, 3)), pl.BlockSpec((1, G_HEAD_DIM), lambda i: (0, 0))],
        out_specs=[pl.BlockSpec((rows, G_WIDTH), lambda i: (i, 0)),
                   pl.BlockSpec((cpg * G_HEADS, G_HEAD_DIM, G_HEAD_DIM), lambda i: (i, 0, 0))],
        out_shape=[jax.ShapeDtypeStruct((s_len, G_WIDTH), MXU_DTYPE),
                   jax.ShapeDtypeStruct((nch * G_HEADS, G_HEAD_DIM, G_HEAD_DIM), F32)],
        scratch_shapes=[pltpu.VMEM((G_HEADS, G_HEAD_DIM, G_HEAD_DIM), F32)],
        compiler_params=_params("arbitrary"),
    )(*chunk_vals, proj_g, norm_w)


def _gdn_scan_bwd(chunk_vals, states, proj_g, norm_w, dyg, *, name):
    s_len = proj_g.shape[0]
    nch = s_len // G_CHUNK
    cpg = min(GS_CHUNKS, nch)
    rows = cpg * G_CHUNK
    ngrid = nch // cpg

    def cur(i):
        return ngrid - 1 - i

    def body(u_ref, w_ref, qk_ref, qd_ref, kd_ref, gl_ref, st_ref, z_ref, nw_ref, dy_ref,
             du_ref, dw_ref, dqk_ref, dqd_ref, dkd_ref, dgl_ref, dz_ref, dnw_ref, dstate):
        @pl.when(pl.program_id(0) == 0)
        def _():
            dstate[...] = jnp.zeros_like(dstate)
            dnw_ref[...] = jnp.zeros_like(dnw_ref)

        def step(k, carry):
            c = cpg - 1 - k
            b = pl.ds(pl.multiple_of(c * G_HEADS, G_HEADS), G_HEADS)
            _, vjp = jax.vjp(_gdn_step, st_ref[b], u_ref[b], w_ref[b], qk_ref[b], qd_ref[b], kd_ref[b], gl_ref[b],
                             _gz_stack(z_ref, c), nw_ref[...])
            dst, du, dw, dqk, dqd, dkd, dgl, dz, dnw = vjp((_gz_stack(dy_ref, c), dstate[...]))
            dstate[...] = dst
            du_ref[b], dw_ref[b], dqk_ref[b], dqd_ref[b], dkd_ref[b], dgl_ref[b] = du, dw, dqk, dqd, dkd, dgl
            rws = pl.ds(pl.multiple_of(c * G_CHUNK, G_CHUNK), G_CHUNK)
            for hd in range(G_HEADS):
                dz_ref[rws, hd * LANES:(hd + 1) * LANES] = dz[hd]
            dnw_ref[...] += dnw
            return carry

        lax.fori_loop(0, cpg, step, 0, unroll=2)

    gate = pl.BlockSpec((rows, G_WIDTH), lambda i: (cur(i), 3))
    wide = pl.BlockSpec((rows, G_WIDTH), lambda i: (cur(i), 0))
    vec = pl.BlockSpec((1, G_HEAD_DIM), lambda i: (0, 0))
    return _call(
        body, name=name, grid=(ngrid,),
        in_specs=_gdn_scan_specs(cpg, cur) + [
            pl.BlockSpec((cpg * G_HEADS, G_HEAD_DIM, G_HEAD_DIM), lambda i: (cur(i), 0, 0)), gate, vec, wide],
        out_specs=_gdn_scan_specs(cpg, cur) + [wide, vec],
        out_shape=_gdn_chunk_shapes(nch) + [jax.ShapeDtypeStruct((s_len, G_WIDTH), F32),
                                            jax.ShapeDtypeStruct((1, G_HEAD_DIM), F32)],
        scratch_shapes=[pltpu.VMEM((G_HEADS, G_HEAD_DIM, G_HEAD_DIM), F32)],
        compiler_params=_params("arbitrary"),
    )(*chunk_vals, states, proj_g, norm_w, dyg)


def _adamw_math(w, g, m, v):
    m = ADAM_B1 * m + (1.0 - ADAM_B1) * g
    v = ADAM_B2 * v + (1.0 - ADAM_B2) * (g * g)
    m_hat = m / (1.0 - ADAM_B1 ** ADAM_STEP)
    v_hat = v / (1.0 - ADAM_B2 ** ADAM_STEP)
    delta = -ADAM_LR * (m_hat / (jnp.sqrt(v_hat) + ADAM_EPS) + ADAM_WD * w)
    return delta, m, v


def _sum_adamw(own, chip, parts, w, m, v, *, name, rows):
    n_layers, n_rows, n_cols = w.shape
    rows = min(rows, n_rows)
    n_parts = parts[0].shape[0]

    def body(c_ref, *refs):
        own_refs, part_refs = refs[:n_layers], refs[n_layers:2 * n_layers]
        w_ref, m_ref, v_ref, g_ref, d_ref, nm_ref, nv_ref = refs[2 * n_layers:]
        layer = pl.program_id(0)
        g = None
        for l in range(n_layers):
            g_l = own_refs[l][0].astype(F32)
            for k in range(n_parts):
                g_l = g_l + part_refs[l][k].astype(F32)
            g = g_l if g is None else jnp.where(layer == l, g_l, g)
        delta, new_m, new_v = _adamw_math(w_ref[0], g, m_ref[0], v_ref[0])
        g_ref[0], d_ref[0], nm_ref[0], nv_ref[0] = g, delta, new_m, new_v

    blk = pl.BlockSpec((1, rows, n_cols), lambda l, i, c: (l, i, 0))
    grid_spec = pltpu.PrefetchScalarGridSpec(
        num_scalar_prefetch=1, grid=(n_layers, n_rows // rows),
        in_specs=[pl.BlockSpec((1, rows, n_cols), lambda l, i, c: (c[0], i, 0))] * n_layers
        + [pl.BlockSpec((n_parts, rows, n_cols), lambda l, i, c: (0, i, 0))] * n_layers + [blk, blk, blk],
        out_specs=[blk] * 4)
    return _call(
        body, name=name, grid_spec=grid_spec, out_shape=[jax.ShapeDtypeStruct(w.shape, F32)] * 4,
        compiler_params=_params("parallel", "parallel"),
    )(_index_operand(chip), *own, *parts, w, m, v)


def _sum_slots(parts, *, name):
    rows = parts.shape[1]

    def body(p_ref, o_ref):
        g = p_ref[0]
        for k in range(1, N_DEV):
            g = g + p_ref[k]
        o_ref[...] = g

    return _call(body, name=name, grid=(1,),
                 in_specs=[pl.BlockSpec(parts.shape, lambda i: (0, 0, 0))],
                 out_specs=pl.BlockSpec((rows, LANES), lambda i: (0, 0)),
                 out_shape=jax.ShapeDtypeStruct((rows, LANES), F32), compiler_params=_params("arbitrary"))(parts)


def _adamw_packed(w, g, m, v, *, name):
    def body(w_ref, g_ref, m_ref, v_ref, d_ref, nm_ref, nv_ref):
        d_ref[...], nm_ref[...], nv_ref[...] = _adamw_math(w_ref[...], g_ref[...], m_ref[...], v_ref[...])

    blk = pl.BlockSpec(w.shape, lambda i: (0, 0))
    return _call(body, name=name, grid=(1,), in_specs=[blk] * 4, out_specs=[blk] * 3,
                 out_shape=[jax.ShapeDtypeStruct(w.shape, F32)] * 3, compiler_params=_params("arbitrary"))(w, g, m, v)


A_COLS = ((0, 512), (768, 1280), (512, 768))
R_COLS = ((1280, 3328),)
G_COLS = ((3328, 5384),)


def _group_weights(wt_full):
    def take(ranges):
        return jnp.concatenate([wt_full[a:b] for a, b in ranges], axis=0)

    wt_g = jnp.concatenate([take(G_COLS), jnp.zeros((G_PAD, wt_full.shape[1]), wt_full.dtype)], axis=0)
    return take(A_COLS), take(R_COLS), wt_g


def _ungroup_grads(d_a, d_r, d_g):
    return jnp.concatenate([d_a[0:512], d_a[1024:1280], d_a[512:1024], d_r, d_g[:WG - G_PAD]], axis=0)


def _shard_rows(w):
    return jnp.pad(jnp.transpose(w, (0, 2, 1)), ((0, 0), (0, N_ROWS_PAD - N_IN_SHARD), (0, 0)))


def _unshard_rows(wt):
    return jnp.transpose(wt[:, :N_IN_SHARD], (0, 2, 1))


def _owner_blocks(dwt):
    blocks = jnp.pad(dwt.reshape(4, 2, N_IN_SHARD, D_MODEL), ((0, 0), (0, 0), (0, N_ROWS_PAD - N_IN_SHARD), (0, 0)))
    return jnp.transpose(blocks, (1, 0, 2, 3))


def _rope_tables(s_len):
    inv = 1.0 / (ROPE_THETA ** (jnp.arange(0, A_HEAD_DIM, 2, dtype=F32) / A_HEAD_DIM))
    ang = jnp.arange(s_len, dtype=F32)[:, None] * inv[None, :]
    cos, sin = jnp.cos(ang), jnp.sin(ang)
    return jnp.tile(cos, (1, 4)), jnp.tile(jnp.concatenate([-sin, sin], axis=1), (1, 2))


SMALL = ("sinks", "r_conv_w", "r_conv_b", "r_wa", "r_ba", "r_wx", "r_bx", "r_lam", "g_conv_w", "g_a_log", "g_dt_bias",
         "g_norm_w", "ln_g", "ln_b")


def _pack(leaves):
    rows = []
    for leaf in leaves:
        flat = leaf.reshape(-1)
        pad = (-flat.shape[0]) % (8 * LANES)
        rows.append(jnp.pad(flat, (0, pad)).reshape(-1, LANES))
    return jnp.concatenate(rows, axis=0)


def _unpack(packed, shapes):
    out, row = [], 0
    for shape in shapes:
        size = math.prod(shape)
        nrows = -(-size // (8 * LANES)) * 8
        out.append(packed[row:row + nrows].reshape(-1)[:size].reshape(shape))
        row += nrows
    return out


def _lane_row(vals, offset):
    return jnp.pad(vals, (offset, LANES - offset - vals.shape[0])).reshape(1, LANES)


def kernel(x, w_in, sinks, r_conv_w, r_conv_b, r_wa, r_ba, r_wx, r_bx, r_lam, g_conv_w, g_a_log, g_dt_bias, g_norm_w, w_out, ln_g, ln_b, loss_target, m_w_in, m_sinks, m_r_conv_w, m_r_conv_b, m_r_wa, m_r_ba, m_r_wx, m_r_bx, m_r_lam, m_g_conv_w, m_g_a_log, m_g_dt_bias, m_g_norm_w, m_w_out, m_ln_g, m_ln_b, v_w_in, v_sinks, v_r_conv_w, v_r_conv_b, v_r_wa, v_r_ba, v_r_wx, v_r_bx, v_r_lam, v_g_conv_w, v_g_a_log, v_g_dt_bias, v_g_norm_w, v_w_out, v_ln_g, v_ln_b):
    s_len = x.shape[1]
    x0 = x.reshape(s_len, D_MODEL)
    target = loss_target.reshape(s_len, D_MODEL)
    me = 4 * lax.axis_index("x") + 2 * lax.axis_index("y") + lax.axis_index("c")
    core, chip = lax.axis_index("c"), 2 * lax.axis_index("x") + lax.axis_index("y")

    win_pieces = _shard_rows(w_in).astype(MXU_DTYPE).reshape(DEPTH, 2, N_ROWS_PAD // 2, D_MODEL)
    wout_pieces = w_out.astype(MXU_DTYPE).reshape(DEPTH, 2, OUT_SHARD // 2, D_MODEL)
    win0_all, wout0_all, rcw_all, gcw_all = _all_gather(
        [win_pieces[0], wout_pieces[0], r_conv_w[None], g_conv_w[None]], "gather_weights")
    rcw_full = jnp.moveaxis(rcw_all[:, 0], 0, 2).reshape(DEPTH, CONV_WIDTH, R_WIDTH)
    gcw_full = jnp.moveaxis(gcw_all[:, 0], 0, 2).reshape(DEPTH, CONV_WIDTH, 3 * G_WIDTH)
    cos, sin = _rope_tables(s_len)

    def big_weights(win_all, wout_all):
        wt_a, wt_r, wt_g = _group_weights(win_all.reshape(N_DEV, N_ROWS_PAD, D_MODEL)[:, :N_IN_SHARD].reshape(N_IN, D_MODEL))
        wo = wout_all.reshape(D_MODEL, D_MODEL)
        return dict(wt_a=wt_a, wt_r=wt_r, wt_g=wt_g, wo=wo,
                    wo_a=wo[0:A_WIDTH], wo_r=wo[A_WIDTH:A_WIDTH + R_WIDTH], wo_g=wo[A_WIDTH + R_WIDTH:])

    layers = []
    for l in range(DEPTH):
        layers.append(dict(
            sinks_t=jnp.broadcast_to(sinks[l][:, None, None], (A_HEADS, 8, LANES)),
            rcw=rcw_full[l], rcb=r_conv_b[l].reshape(1, R_WIDTH), wa=r_wa[l], ba=r_ba[l].reshape(1, R_WIDTH),
            wx=r_wx[l], bx=r_bx[l].reshape(1, R_WIDTH), lam=r_lam[l].reshape(1, R_WIDTH),
            gcw=gcw_full[l], zero_b=jnp.zeros((1, 3 * G_WIDTH), F32),
            a_log=_lane_row(g_a_log[l], G_HEADS), dt=_lane_row(g_dt_bias[l], G_HEADS),
            norm_w=g_norm_w[l].reshape(1, G_HEAD_DIM), ln_g=ln_g[l].reshape(1, D_MODEL), ln_b=ln_b[l].reshape(1, D_MODEL)))

    saved = []
    xin = xin_lo = x0
    layers[0].update(big_weights(win0_all, wout0_all))
    for l, p in enumerate(layers):
        if l + 1 < DEPTH:
            proj_a, (wout_next,) = _matmul([xin_lo], [p["wt_a"]], name=f"proj_a{l}", tm=1024, tn=1280, b_t=True,
                                           comm=_GatherSend([wout_pieces[l + 1]]))
            proj_r, (win_next_0,) = _matmul([xin_lo], [p["wt_r"]], name=f"proj_r{l}", tm=1024, tn=1024, b_t=True,
                                            comm=_GatherSend([win_pieces[l + 1, 0:1]]))
            proj_g, (win_next_1,) = _matmul([xin_lo], [p["wt_g"]], name=f"proj_g{l}", tm=1024, tn=1152, b_t=True,
                                            comm=_GatherSend([win_pieces[l + 1, 1:2]]))
            forward_next = _GatherForward([win_next_0, win_next_1, wout_next])
        else:
            forward_next = None
            proj_a = _matmul([xin_lo], [p["wt_a"]], name=f"proj_a{l}", tm=1024, tn=1280, b_t=True)
            proj_r = _matmul([xin_lo], [p["wt_r"]], name=f"proj_r{l}", tm=1024, tn=1024, b_t=True)
            proj_g = _matmul([xin_lo], [p["wt_g"]], name=f"proj_g{l}", tm=1024, tn=1152, b_t=True)
        ya = _attn_fwd(proj_a, cos, sin, p["sinks_t"], name=f"attn_fwd{l}")
        xr = _conv_fwd(proj_r, R_WIDTH, p["rcw"], p["rcb"], name=f"rconv_fwd{l}")
        h, yr = _rg_fwd(xr, proj_r, p["wa"], p["ba"], p["wx"], p["bx"], p["lam"], name=f"rglru_fwd{l}")
        conv = _conv_fwd(proj_g, 3 * G_WIDTH, p["gcw"], p["zero_b"], name=f"gconv_fwd{l}")
        *chunk_vals, inv = _gdn_chunk_fwd(conv, proj_g, p["a_log"], p["dt"], name=f"gdn_chunk_fwd{l}")
        yg, states = _gdn_scan_fwd(chunk_vals, proj_g, p["norm_w"], name=f"gdn_scan_fwd{l}")
        if forward_next is None:
            z, xout, xout_lo = _outproj_ln(ya, yr, yg, p["wo"], xin, p["ln_g"], p["ln_b"], name=f"outproj_ln{l}")
        else:
            (z, xout, xout_lo), (win_0, win_1, wout_all) = _outproj_ln(
                ya, yr, yg, p["wo"], xin, p["ln_g"], p["ln_b"], name=f"outproj_ln{l}", comm=forward_next)
            layers[l + 1].update(big_weights(jnp.concatenate([win_0, win_1], axis=1), wout_all))
        saved.append(dict(xin_lo=xin_lo, proj_a=proj_a, proj_r=proj_r, proj_g=proj_g, ya=ya, yr=yr, yg=yg, xr=xr, h=h,
                          conv=conv, chunk_vals=chunk_vals, inv=inv, states=states, z=z))
        xin, xin_lo = xout, xout_lo

    grads = [None] * DEPTH
    dxn = None
    loss_local = None
    for l in reversed(range(DEPTH)):
        p, sv = layers[l], saved[l]
        if dxn is None:
            dz, dz_lo, dln_g, dln_b, loss_local = _ln_bwd(sv["z"], p["ln_g"], name=f"ln_bwd{l}", xn=xin, target=target)
        else:
            dz, dz_lo, dln_g, dln_b = _ln_bwd(sv["z"], p["ln_g"], name=f"ln_bwd{l}", dxn=dxn)
        dya = _matmul([dz_lo], [p["wo_a"]], name=f"dya{l}", tm=1024, tn=512, b_t=True)
        dyr = _matmul([dz_lo], [p["wo_r"]], name=f"dyr{l}", tm=1024, tn=1024, b_t=True)
        dyg = _matmul([dz_lo], [p["wo_g"]], name=f"dyg{l}", tm=1024, tn=512, b_t=True)
        dwo = jnp.concatenate([
            _matmul_tn(sv["ya"], dz_lo, name=f"dwo_a{l}", tm=512, tn=1024, tk=1024),
            _matmul_tn(sv["yr"], dz_lo, name=f"dwo_r{l}", tm=1024, tn=1024, tk=1024),
            _matmul_tn(sv["yg"], dz_lo, name=f"dwo_g{l}", tm=512, tn=1024, tk=1024)], axis=0)

        dproj_a, dsinks_t = _attn_bwd(sv["proj_a"], cos, sin, p["sinks_t"], dya, name=f"attn_bwd{l}")

        dxr, drz, dwa, dba, dwx, dbx, dlam = _rg_bwd(sv["xr"], sv["proj_r"], sv["h"], dyr, p["wa"], p["ba"], p["wx"],
                                                     p["bx"], p["lam"], name=f"rglru_bwd{l}")
        dproj_r, drcw, drcb = _conv_bwd(dxr, sv["proj_r"], R_WIDTH, p["rcw"], [drz], name=f"rconv_bwd{l}")

        scan_out = _gdn_scan_bwd(sv["chunk_vals"], sv["states"], sv["proj_g"], p["norm_w"], dyg, name=f"gdn_scan_bwd{l}")
        dgz, dnorm_w = scan_out[6], scan_out[7]
        dconv, dbg, dal, ddt = _gdn_chunk_bwd(sv["conv"], sv["proj_g"], p["a_log"], p["dt"], sv["inv"], scan_out[:6],
                                              name=f"gdn_chunk_bwd{l}")
        dproj_g, dgcw, _ = _conv_bwd(dconv, sv["proj_g"], 3 * G_WIDTH, p["gcw"], [dgz, dbg], name=f"gconv_bwd{l}")

        grads[l] = dict(
            sinks=dsinks_t[:, :, 0].sum(axis=1), r_conv_w=drcw.reshape(CONV_WIDTH, R_WIDTH),
            r_conv_b=drcb.reshape(R_WIDTH), r_wa=dwa, r_ba=dba.reshape(R_WIDTH), r_wx=dwx, r_bx=dbx.reshape(R_WIDTH),
            r_lam=dlam.reshape(R_WIDTH), g_conv_w=dgcw.reshape(CONV_WIDTH, 3 * G_WIDTH),
            g_a_log=dal[0, G_HEADS:2 * G_HEADS], g_dt_bias=ddt[0, G_HEADS:2 * G_HEADS],
            g_norm_w=dnorm_w.reshape(G_HEAD_DIM), ln_g=dln_g.reshape(D_MODEL), ln_b=dln_b.reshape(D_MODEL))

        if l > 0:
            dwin_a = _matmul_tn(dproj_a, sv["xin_lo"], name=f"dwin_a{l}", tm=640, tn=1024, tk=1024)
            dwin_r = _matmul_tn(dproj_r, sv["xin_lo"], name=f"dwin_r{l}", tm=1024, tn=1024, tk=1024)
        else:
            packed_small = _pack([jnp.stack([grads[k][nm] for k in range(DEPTH)]) for nm in SMALL])
            dwin_a, (sent_small,) = _matmul_tn(
                dproj_a, sv["xin_lo"], name=f"dwin_a{l}", tm=640, tn=1024, tk=1024,
                comm=_GatherSend([packed_small.reshape(4, packed_small.shape[0] // 4, LANES)]))
            dwin_r, (all_small,) = _matmul_tn(dproj_r, sv["xin_lo"], name=f"dwin_r{l}", tm=1024, tn=1024, tk=1024,
                                              comm=_GatherForward([sent_small]))
        dwin = _ungroup_grads(dwin_a, dwin_r,
                              _matmul_tn(dproj_g, sv["xin_lo"], name=f"dwin_g{l}", tm=1152, tn=1024, tk=1024))

        dwin_blocks = _owner_blocks(dwin)[:, :, None].astype(MXU_DTYPE)
        dwout_blocks = jnp.transpose(dwo.reshape(4, 2, OUT_SHARD, D_MODEL), (1, 0, 2, 3))[:, :, None].astype(MXU_DTYPE)
        got_win, got_wout = _swap_cores(
            [dwin_blocks.reshape(2, 8, N_ROWS_PAD // 2, D_MODEL), dwout_blocks.reshape(2, 4, OUT_SHARD, D_MODEL)],
            f"swap_core_grads{l}")
        chip_win = _add_pair(dwin_blocks, got_win.reshape(dwin_blocks.shape[1:]), core, name=f"add_core_grads_w_in{l}",
                             rows=352).reshape(4, N_ROWS_PAD, D_MODEL)
        chip_wout = _add_pair(dwout_blocks, got_wout.reshape(dwout_blocks.shape[1:]), core, name=f"add_core_grads_w_out{l}",
                              rows=256).reshape(4, OUT_SHARD, D_MODEL)
        dxn, (win_parts, wout_parts) = _matmul(
            [dproj_a, dproj_r, dproj_g], [p["wt_a"], p["wt_r"], p["wt_g"]], name=f"dx{l}", tm=512, tn=1024, add=dz,
            add_scale=DEEPNORM_ALPHA, comm=_ChipExchange([chip_win, chip_wout]))
        grads[l].update(chip_win=chip_win, chip_wout=chip_wout, win_parts=win_parts, wout_parts=wout_parts)
    grad_x = dxn.reshape(x.shape)
    loss = lax.psum(loss_local[0, 0], ("x", "y", "c"))

    def stacked(name):
        return jnp.stack([grads[l][name] for l in range(DEPTH)])

    def per_layer(name):
        return [grads[l][name] for l in range(DEPTH)]

    w_in_t = [_unshard_rows(t) for t in _sum_adamw(per_layer("chip_win"), chip, per_layer("win_parts"), _shard_rows(w_in),
                                                   _shard_rows(m_w_in), _shard_rows(v_w_in), name="adamw_w_in", rows=176)]
    g_w_in, d_w_in, nm_w_in, nv_w_in = w_in_t
    g_w_out, d_w_out, nm_w_out, nv_w_out = _sum_adamw(per_layer("chip_wout"), chip, per_layer("wout_parts"), w_out,
                                                      m_w_out, v_w_out, name="adamw_w_out", rows=128)

    small = list(SMALL)
    full_shapes = [stacked(nm).shape for nm in small]
    all_small = all_small.reshape(N_DEV, packed_small.shape[0], LANES)
    g_small = dict(zip(small, _unpack(_sum_slots(all_small, name="sum_small_grads"), full_shapes)))
    g_small["r_conv_w"] = lax.dynamic_slice_in_dim(g_small["r_conv_w"], me * (R_WIDTH // N_DEV), R_WIDTH // N_DEV, axis=2)
    g_small["g_conv_w"] = lax.dynamic_slice_in_dim(g_small["g_conv_w"], me * (3 * G_WIDTH // N_DEV), 3 * G_WIDTH // N_DEV, axis=2)
    given = dict(sinks=(sinks, m_sinks, v_sinks), r_conv_w=(r_conv_w, m_r_conv_w, v_r_conv_w),
                 r_conv_b=(r_conv_b, m_r_conv_b, v_r_conv_b), r_wa=(r_wa, m_r_wa, v_r_wa), r_ba=(r_ba, m_r_ba, v_r_ba),
                 r_wx=(r_wx, m_r_wx, v_r_wx), r_bx=(r_bx, m_r_bx, v_r_bx), r_lam=(r_lam, m_r_lam, v_r_lam),
                 g_conv_w=(g_conv_w, m_g_conv_w, v_g_conv_w), g_a_log=(g_a_log, m_g_a_log, v_g_a_log),
                 g_dt_bias=(g_dt_bias, m_g_dt_bias, v_g_dt_bias), g_norm_w=(g_norm_w, m_g_norm_w, v_g_norm_w),
                 ln_g=(ln_g, m_ln_g, v_ln_g), ln_b=(ln_b, m_ln_b, v_ln_b))
    shard_shapes = [given[nm][0].shape for nm in small]
    packed = [_pack([given[nm][k] for nm in small]) for k in range(3)]
    d_p, nm_p, nv_p = _adamw_packed(packed[0], _pack([g_small[nm] for nm in small]), packed[1], packed[2], name="adamw_small")
    d_small = dict(zip(small, _unpack(d_p, shard_shapes)))
    nm_small = dict(zip(small, _unpack(nm_p, shard_shapes)))
    nv_small = dict(zip(small, _unpack(nv_p, shard_shapes)))

    order = ["w_in"] + small[:12] + ["w_out"] + small[12:]

    def leaf(big_in, big_out, table):
        return [big_in if nm == "w_in" else big_out if nm == "w_out" else table[nm] for nm in order]

    return (loss, grad_x, *leaf(g_w_in, g_w_out, g_small), *leaf(d_w_in, d_w_out, d_small),
            *leaf(nm_w_in, nm_w_out, nm_small), *leaf(nv_w_in, nv_w_out, nv_small))
```

```python
import functools
import math

import jax
import jax.numpy as jnp
from jax import lax
from jax.experimental import pallas as pl
from jax.experimental.pallas import tpu as pltpu

F32 = jnp.float32
MXU_DTYPE = jnp.bfloat16
HIGHEST = lax.Precision.HIGHEST
MESH_ID = pl.DeviceIdType.MESH

N_DEV = 8
DEPTH = 2
D_MODEL = 2048
A_HEADS, A_KV_HEADS, A_HEAD_DIM = 8, 2, 64
A_WIDTH, A_KV_WIDTH = 512, 128
A_BLOCK = 128
ROPE_THETA = 10000.0
R_WIDTH, R_BLOCKS, R_BLOCK_DIM = 1024, 8, 128
R_C = 8.0
CONV_WIDTH = 4
G_HEADS, G_HEAD_DIM, G_WIDTH, G_CHUNK = 4, 128, 512, 64
N_IN = 5384
N_IN_SHARD = N_IN // N_DEV
N_ROWS_PAD = 704
OUT_SHARD = D_MODEL // N_DEV
WA, WR, WG = 1280, 2048, 2304
G_PAD = WG - (3 * G_WIDTH + G_WIDTH + 2 * G_HEADS)
DEEPNORM_ALPHA = (2 * DEPTH) ** 0.25
LN_EPS = 1e-5
RMS_EPS = 1e-6
ADAM_LR, ADAM_B1, ADAM_B2, ADAM_EPS, ADAM_WD, ADAM_STEP = 0.001, 0.9, 0.999, 1e-08, 0.01, 10
NEG = -1e30
VMEM_LIMIT = 56 * 1024 * 1024
LANES = 128


def _call(body, **kw):
    return pl.pallas_call(body, **kw)


def _params(*sem):
    return pltpu.CompilerParams(dimension_semantics=sem, vmem_limit_bytes=VMEM_LIMIT)


def _t(x):
    return jnp.swapaxes(x, -1, -2)


def _raw_dot(a, b, ca, cb, precision=None):
    batch = tuple(range(a.ndim - 2))
    if precision is None:
        a, b = a.astype(MXU_DTYPE), b.astype(MXU_DTYPE)
    return lax.dot_general(a, b, (((ca,), (cb,)), (batch, batch)), precision=precision,
                           preferred_element_type=F32)


def _nn(a, b, precision=None):
    return _raw_dot(a, b, a.ndim - 1, b.ndim - 2, precision)


def _nt(a, b, precision=None):
    return _raw_dot(a, b, a.ndim - 1, b.ndim - 1, precision)


@jax.custom_vjp
def mm_nn(a, b):
    return _nn(a, b)


def _mm_nn_fwd(a, b):
    return _nn(a, b), (a, b)


def _mm_nn_bwd(res, g):
    a, b = res
    return _nt(g, b), _nn(_t(a), g)


mm_nn.defvjp(_mm_nn_fwd, _mm_nn_bwd)


@jax.custom_vjp
def mm_nt(a, b):
    return _nt(a, b)


def _mm_nt_fwd(a, b):
    return _nt(a, b), (a, b)


def _mm_nt_bwd(res, g):
    a, b = res
    return _nn(g, b), _nn(_t(g), a)


mm_nt.defvjp(_mm_nt_fwd, _mm_nt_bwd)


def _split(x):
    hi = x.astype(MXU_DTYPE)
    return hi, (x - hi.astype(F32)).astype(MXU_DTYPE)


def _hmm(a, b, nt=False):
    dot = _nt if nt else _nn
    return dot(a[0], b[0]) + (dot(a[0], b[1]) + dot(a[1], b[0]))


def _silu(x):
    return x * jax.nn.sigmoid(x)


def _softplus(x):
    return jnp.maximum(x, 0.0) + jnp.log1p(jnp.exp(-jnp.abs(x)))


def _one_minus_sq(log_a, a):
    x = 2.0 * log_a
    return jnp.where(x > -0.01, -x * (1.0 + 0.5 * x), 1.0 - a * a)


def _iota(shape, dim):
    return lax.broadcasted_iota(jnp.int32, shape, dim)


def _inv_unit_lower(m):
    shape = m.shape
    row, col = _iota(shape, 1), _iota(shape, 2)
    eye = (row == col).astype(F32)

    def blockdiag(size):
        return (row // size) == (col // size)

    x = -jnp.where(blockdiag(8), m, 0.0)
    xs = _split(x)
    x2s = _split(_hmm(xs, xs))
    x4s = _split(_hmm(x2s, x2s))
    inv = eye + x
    inv = inv + _hmm(_split(inv), x2s)
    inv = inv + _hmm(_split(inv), x4s)
    for size in (8, 16, 32):
        below = jnp.where(blockdiag(2 * size) & jnp.logical_not(blockdiag(size)), m, 0.0)
        invs = _split(inv)
        inv = inv - _hmm(_split(_hmm(invs, _split(below))), invs)
    return inv


@jax.custom_vjp
def _solve2(m, inv, r1, r2):
    invs = _split(inv)
    return _hmm(invs, _split(r1)), _hmm(invs, _split(r2))


def _solve2_fwd(m, inv, r1, r2):
    x1, x2 = _solve2(m, inv, r1, r2)
    return (x1, x2), (inv, x1, x2)


def _solve2_bwd(res, g):
    inv, x1, x2 = res
    inv_ts = _split(_t(inv))
    d1, d2 = _hmm(inv_ts, _split(g[0])), _hmm(inv_ts, _split(g[1]))
    dm = -(_hmm(_split(d1), _split(x1), nt=True) + _hmm(_split(d2), _split(x2), nt=True))
    return dm, jnp.zeros_like(inv), d1, d2


_solve2.defvjp(_solve2_fwd, _solve2_bwd)


def _swap_halves(x):
    n = x.shape[-1]
    lane = _iota(x.shape, x.ndim - 1)
    return jnp.where((lane & 63) < 32, pltpu.roll(x, n - 32, x.ndim - 1), pltpu.roll(x, 32, x.ndim - 1))


def _rope(x, cos, sin):
    reps = x.shape[-1] // LANES
    if reps > 1:
        cos, sin = jnp.tile(cos, (1, reps)), jnp.tile(sin, (1, reps))
    return x * cos + _swap_halves(x) * sin


def _rope_t(d, cos, sin):
    reps = d.shape[-1] // LANES
    if reps > 1:
        cos, sin = jnp.tile(cos, (1, reps)), jnp.tile(sin, (1, reps))
    return d * cos + _swap_halves(d * sin)


def _swap64(x):
    return pltpu.roll(x, 64, x.ndim - 1)


def _mesh_pos():
    return lax.axis_index("x"), lax.axis_index("y"), lax.axis_index("c")


def _all_gather(arrays, name):
    n = len(arrays)
    npieces = [a.shape[0] for a in arrays]
    pmax = max(npieces)

    def body(*refs):
        ins, outs = refs[:n], refs[n:2 * n]
        send_sems, recv_sems, local_sem = refs[2 * n:]
        x, y, c = _mesh_pos()
        me, sibling = (x, y, c), (x, y, 1 - c)
        chips = [(1 - x, y), (x, 1 - y), (1 - x, 1 - y)]

        def slot(a, pos, p):
            return outs[a].at[4 * pos[0] + 2 * pos[1] + pos[2], p]

        def copy(a, p, k, block, to, own=False):
            return pltpu.make_async_remote_copy(
                src_ref=ins[a].at[p] if own else slot(a, block, p), dst_ref=slot(a, block, p),
                send_sem=send_sems.at[a, p, k], recv_sem=recv_sems.at[a, p, k], device_id=to, device_id_type=MESH_ID)

        pieces = [(a, p) for p in range(pmax) for a in range(n) if p < npieces[a]]
        mine = [pltpu.make_async_copy(ins[a].at[p], slot(a, me, p), local_sem.at[a, p]) for a, p in pieces]
        for cp in mine:
            cp.start()
        first = []
        for a, p in pieces:
            first += [copy(a, p, 1 + j, me, (*chip, c), own=True) for j, chip in enumerate(chips)]
            first.append(copy(a, p, 0, me, sibling, own=True))
        for cp in first:
            cp.start()
        passed = []
        for a, p in pieces:
            for j, chip in enumerate(chips):
                copy(a, p, 1 + j, (*chip, c), me).wait_recv()
                cp = copy(a, p, 4 + j, (*chip, c), sibling)
                cp.start()
                passed.append(cp)
        for a, p in pieces:
            copy(a, p, 0, sibling, me).wait_recv()
            for j, chip in enumerate(chips):
                copy(a, p, 4 + j, (*chip, 1 - c), me).wait_recv()
        for cp in first + passed:
            cp.wait_send()
        for cp in mine:
            cp.wait()

    any_spec = pl.BlockSpec(memory_space=pl.ANY)
    return _call(
        body, name=name,
        out_shape=[jax.ShapeDtypeStruct((N_DEV,) + a.shape, a.dtype) for a in arrays],
        in_specs=[any_spec] * n, out_specs=[any_spec] * n,
        scratch_shapes=[pltpu.SemaphoreType.DMA((n, pmax, 7)), pltpu.SemaphoreType.DMA((n, pmax, 7)),
                        pltpu.SemaphoreType.DMA((n, pmax))],
    )(*arrays)


def _swap_cores(arrays, name):
    n = len(arrays)
    pmax = max(a.shape[1] for a in arrays)

    def body(*refs):
        ins, got = refs[:n], refs[n:2 * n]
        send_sems, recv_sems = refs[2 * n:]
        x, y, c = _mesh_pos()
        copies = [pltpu.make_async_remote_copy(
            src_ref=ins[a].at[1 - c, p], dst_ref=got[a].at[p], send_sem=send_sems.at[a, p], recv_sem=recv_sems.at[a, p],
            device_id=(x, y, 1 - c), device_id_type=MESH_ID) for a in range(n) for p in range(arrays[a].shape[1])]
        for cp in copies:
            cp.start()
        for cp in copies:
            cp.wait()

    any_spec = pl.BlockSpec(memory_space=pl.ANY)
    return _call(
        body, name=name, out_shape=[jax.ShapeDtypeStruct(a.shape[1:], a.dtype) for a in arrays],
        in_specs=[any_spec] * n, out_specs=[any_spec] * n,
        scratch_shapes=[pltpu.SemaphoreType.DMA((n, pmax)), pltpu.SemaphoreType.DMA((n, pmax))],
    )(*arrays)


class _ChipExchange:
    aliases = {}

    def __init__(self, arrays):
        self.arrays = list(arrays)
        n = len(self.arrays)
        self.out_shape = [jax.ShapeDtypeStruct((3,) + a.shape[1:], a.dtype) for a in self.arrays]
        self.scratch = [pltpu.SemaphoreType.DMA((n, 3)), pltpu.SemaphoreType.DMA((n, 3))]

    def _copies(self, ins, outs, send_sems, recv_sems):
        x, y, c = _mesh_pos()
        copies = []
        for a in range(len(self.arrays)):
            for k in range(1, 4):
                px, py = x ^ (k >> 1), y ^ (k & 1)
                copies.append(pltpu.make_async_remote_copy(
                    src_ref=ins[a].at[2 * px + py], dst_ref=outs[a].at[k - 1], send_sem=send_sems.at[a, k - 1],
                    recv_sem=recv_sems.at[a, k - 1], device_id=(px, py, c), device_id_type=MESH_ID))
        return copies

    def start(self, ins, outs, send_sems, recv_sems):
        for cp in self._copies(ins, outs, send_sems, recv_sems):
            cp.start()

    def finish(self, ins, outs, send_sems, recv_sems):
        copies = self._copies(ins, outs, send_sems, recv_sems)
        for cp in copies:
            cp.wait_recv()
        for cp in copies:
            cp.wait_send()


def _slot(pos):
    return 4 * pos[0] + 2 * pos[1] + pos[2]


class _GatherSend:
    aliases = {}

    def __init__(self, arrays):
        self.arrays = list(arrays)
        n, pmax = len(self.arrays), max(a.shape[0] for a in self.arrays)
        self.out_shape = [jax.ShapeDtypeStruct((N_DEV,) + a.shape, a.dtype) for a in self.arrays]
        self.scratch = [pltpu.SemaphoreType.DMA((n, pmax, 4)), pltpu.SemaphoreType.DMA((n, pmax, 4)),
                        pltpu.SemaphoreType.DMA((n, pmax))]

    def _copies(self, ins, outs, send_sems, recv_sems, local_sems):
        x, y, c = _mesh_pos()
        peers = [(x, y, 1 - c), (1 - x, y, c), (x, 1 - y, c), (1 - x, 1 - y, c)]
        local, remote = [], []
        for a, arr in enumerate(self.arrays):
            for p in range(arr.shape[0]):
                local.append(pltpu.make_async_copy(ins[a].at[p], outs[a].at[_slot((x, y, c)), p], local_sems.at[a, p]))
                for k, peer in enumerate(peers):
                    remote.append(pltpu.make_async_remote_copy(
                        src_ref=ins[a].at[p], dst_ref=outs[a].at[_slot((x, y, c)), p], send_sem=send_sems.at[a, p, k],
                        recv_sem=recv_sems.at[a, p, k], device_id=peer, device_id_type=MESH_ID))
        return local, remote

    def start(self, *refs):
        local, remote = self._copies(*refs)
        for cp in local + remote:
            cp.start()

    def finish(self, *refs):
        local, remote = self._copies(*refs)
        for cp in remote:
            cp.wait_recv()
        for cp in remote:
            cp.wait_send()
        for cp in local:
            cp.wait()


class _GatherForward:
    def __init__(self, gathered):
        self.arrays = list(gathered)
        n, pmax = len(self.arrays), max(a.shape[1] for a in self.arrays)
        self.out_shape = [jax.ShapeDtypeStruct(a.shape, a.dtype) for a in self.arrays]
        self.aliases = {k: k for k in range(n)}
        self.scratch = [pltpu.SemaphoreType.DMA((n, pmax, 3)), pltpu.SemaphoreType.DMA((n, pmax, 3))]

    def _copies(self, ins, outs, send_sems, recv_sems):
        x, y, c = _mesh_pos()
        copies = []
        for a, arr in enumerate(self.arrays):
            for p in range(arr.shape[1]):
                for j, chip in enumerate([(1 - x, y), (x, 1 - y), (1 - x, 1 - y)]):
                    copies.append(pltpu.make_async_remote_copy(
                        src_ref=ins[a].at[_slot((*chip, c)), p], dst_ref=outs[a].at[_slot((*chip, c)), p],
                        send_sem=send_sems.at[a, p, j], recv_sem=recv_sems.at[a, p, j], device_id=(x, y, 1 - c),
                        device_id_type=MESH_ID))
        return copies

    def start(self, *refs):
        for cp in self._copies(*refs):
            cp.start()

    def finish(self, *refs):
        copies = self._copies(*refs)
        for cp in copies:
            cp.wait_recv()
        for cp in copies:
            cp.wait_send()


def _index_operand(i):
    return jnp.reshape(i, (1,)).astype(jnp.int32)


def _add_pair(pair, other, core, *, name, rows):
    _, n_slots, n_layers, n_rows, n_cols = pair.shape
    rows = min(rows, n_rows)

    def body(c_ref, a_ref, b_ref, o_ref):
        o_ref[...] = (a_ref[0].astype(F32) + b_ref[...].astype(F32)).astype(o_ref.dtype)

    blk = pl.BlockSpec((1, 1, rows, n_cols), lambda s, l, i, c: (s, l, i, 0))
    grid_spec = pltpu.PrefetchScalarGridSpec(
        num_scalar_prefetch=1, grid=(n_slots, n_layers, n_rows // rows),
        in_specs=[pl.BlockSpec((1, 1, 1, rows, n_cols), lambda s, l, i, c: (c[0], s, l, i, 0)), blk], out_specs=blk)
    return _call(body, name=name, grid_spec=grid_spec, out_shape=jax.ShapeDtypeStruct(other.shape, pair.dtype),
                 compiler_params=_params("parallel", "parallel", "parallel"))(_index_operand(core), pair, other)


def _host(body, comm, *, name, grid, in_specs, out_specs, out_shape, args, semantics):
    if comm is None:
        return _call(body, name=name, grid=grid, in_specs=in_specs, out_specs=out_specs, out_shape=out_shape,
                     compiler_params=_params(*semantics))(*args)
    n_in, n_out, n_comm = len(in_specs), len(out_specs), len(comm.arrays)

    def hosted(*refs):
        ins, outs = refs[:n_in], refs[n_in + n_comm:n_in + n_comm + n_out]
        comm_refs = (refs[n_in:n_in + n_comm], refs[n_in + n_comm + n_out:n_in + 2 * n_comm + n_out],
                     *refs[n_in + 2 * n_comm + n_out:])
        ids = [pl.program_id(d) for d in range(len(grid))]
        first, last = ids[0] == 0, ids[0] == grid[0] - 1
        for d in range(1, len(grid)):
            first, last = first & (ids[d] == 0), last & (ids[d] == grid[d] - 1)

        @pl.when(first)
        def _():
            comm.start(*comm_refs)

        body(*ins, *outs)

        @pl.when(last)
        def _():
            comm.finish(*comm_refs)

    any_spec = pl.BlockSpec(memory_space=pl.ANY)
    outs = _call(hosted, name=name, grid=grid, in_specs=list(in_specs) + [any_spec] * n_comm,
                 out_specs=list(out_specs) + [any_spec] * n_comm, out_shape=list(out_shape) + comm.out_shape,
                 input_output_aliases={n_in + k: n_out + v for k, v in comm.aliases.items()},
                 scratch_shapes=comm.scratch, compiler_params=_params(*(("arbitrary",) * len(grid))))(*args, *comm.arrays)
    return outs[:n_out], outs[n_out:]


def _matmul(a_list, b_list, *, name, tm, tn, b_t=False, out_dtype=F32, add=None, add_scale=1.0, comm=None):
    n = len(a_list)
    m_rows, n_cols = a_list[0].shape[0], b_list[0].shape[0 if b_t else 1]
    tm, tn = min(tm, m_rows), min(tn, n_cols)

    def body(*refs):
        a_refs, b_refs = refs[:n], refs[n:2 * n]
        o_ref = refs[-1]
        acc = None
        for a_ref, b_ref in zip(a_refs, b_refs):
            part = lax.dot_general(a_ref[...].astype(MXU_DTYPE), b_ref[...].astype(MXU_DTYPE),
                                   (((1,), (1 if b_t else 0,)), ((), ())), preferred_element_type=F32)
            acc = part if acc is None else acc + part
        if add is not None:
            acc = acc + add_scale * refs[2 * n][...]
        o_ref[...] = acc.astype(o_ref.dtype)

    in_specs = [pl.BlockSpec((tm, a.shape[1]), lambda i, j: (i, 0)) for a in a_list]
    if b_t:
        in_specs += [pl.BlockSpec((tn, b.shape[1]), lambda i, j: (j, 0)) for b in b_list]
    else:
        in_specs += [pl.BlockSpec((b.shape[0], tn), lambda i, j: (0, j)) for b in b_list]
    args = list(a_list) + list(b_list)
    if add is not None:
        in_specs.append(pl.BlockSpec((tm, tn), lambda i, j: (i, j)))
        args.append(add)
    res = _host(body, comm, name=name, grid=(m_rows // tm, n_cols // tn), in_specs=in_specs,
                out_specs=[pl.BlockSpec((tm, tn), lambda i, j: (i, j))],
                out_shape=[jax.ShapeDtypeStruct((m_rows, n_cols), out_dtype)], args=args,
                semantics=("parallel", "arbitrary"))
    return res[0] if comm is None else (res[0][0], res[1])


def _matmul_tn(a, b, *, name, tm, tn, tk, comm=None):
    k_rows, m_rows = a.shape
    n_cols = b.shape[1]
    tm, tn, tk = min(tm, m_rows), min(tn, n_cols), min(tk, k_rows)
    nk = k_rows // tk

    def body(a_ref, b_ref, o_ref):
        @pl.when(pl.program_id(2) == 0)
        def _():
            o_ref[...] = jnp.zeros_like(o_ref)

        o_ref[...] += lax.dot_general(a_ref[...].astype(MXU_DTYPE), b_ref[...].astype(MXU_DTYPE),
                                      (((0,), (0,)), ((), ())), preferred_element_type=F32)

    res = _host(body, comm, name=name, grid=(m_rows // tm, n_cols // tn, nk),
                in_specs=[pl.BlockSpec((tk, tm), lambda i, j, k: (k, i)), pl.BlockSpec((tk, tn), lambda i, j, k: (k, j))],
                out_specs=[pl.BlockSpec((tm, tn), lambda i, j, k: (i, j))],
                out_shape=[jax.ShapeDtypeStruct((m_rows, n_cols), F32)], args=[a, b],
                semantics=("parallel", "parallel", "arbitrary"))
    return res[0] if comm is None else (res[0][0], res[1])


def _outproj_ln(ya, yr, yg, w_out, x, ln_g, ln_b, *, name, comm=None):
    s_len = x.shape[0]
    tm = min(256, s_len)

    def body(ya_ref, yr_ref, yg_ref, w_ref, x_ref, g_ref, b_ref, z_ref, o_ref, lo_ref):
        acc = jnp.dot(ya_ref[...], w_ref[0:A_WIDTH, :], preferred_element_type=F32)
        acc += jnp.dot(yr_ref[...], w_ref[A_WIDTH:A_WIDTH + R_WIDTH, :], preferred_element_type=F32)
        acc += jnp.dot(yg_ref[...], w_ref[A_WIDTH + R_WIDTH:, :], preferred_element_type=F32)
        z = DEEPNORM_ALPHA * x_ref[...] + acc
        z_ref[...] = z
        mu = jnp.mean(z, axis=-1, keepdims=True)
        zc = z - mu
        var = jnp.mean(zc * zc, axis=-1, keepdims=True)
        out = zc * lax.rsqrt(var + LN_EPS) * g_ref[...] + b_ref[...]
        o_ref[...] = out
        lo_ref[...] = out.astype(lo_ref.dtype)

    def rows(width):
        return pl.BlockSpec((tm, width), lambda i: (i, 0))

    def whole(shape):
        return pl.BlockSpec(shape, lambda i: (0, 0))

    return _host(
        body, comm, name=name, grid=(s_len // tm,),
        in_specs=[rows(A_WIDTH), rows(R_WIDTH), rows(G_WIDTH), whole((D_MODEL, D_MODEL)), rows(D_MODEL),
                  whole((1, D_MODEL)), whole((1, D_MODEL))],
        out_specs=[rows(D_MODEL)] * 3,
        out_shape=[jax.ShapeDtypeStruct((s_len, D_MODEL), F32)] * 2 + [jax.ShapeDtypeStruct((s_len, D_MODEL), MXU_DTYPE)],
        args=[ya, yr, yg, w_out, x, ln_g, ln_b], semantics=("parallel",))


def _ln_bwd(z, ln_g, *, name, dxn=None, xn=None, target=None):
    s_len = z.shape[0]
    tm = min(256, s_len)
    top = dxn is None

    def body(*refs):
        if top:
            z_ref, g_ref, xn_ref, t_ref, dz_ref, lo_ref, dg_ref, db_ref, loss_ref = refs
            err = xn_ref[...] - t_ref[...]
            dy = err * (1.0 / D_MODEL)
        else:
            z_ref, g_ref, dy_ref, dz_ref, lo_ref, dg_ref, db_ref = refs
            dy = dy_ref[...]
        first = pl.program_id(0) == 0

        @pl.when(first)
        def _():
            dg_ref[...] = jnp.zeros_like(dg_ref)
            db_ref[...] = jnp.zeros_like(db_ref)
            if top:
                loss_ref[...] = jnp.zeros_like(loss_ref)

        z = z_ref[...]
        mu = jnp.mean(z, axis=-1, keepdims=True)
        zc = z - mu
        rstd = lax.rsqrt(jnp.mean(zc * zc, axis=-1, keepdims=True) + LN_EPS)
        xhat = zc * rstd
        dxh = dy * g_ref[...]
        dz = rstd * (dxh - jnp.mean(dxh, axis=-1, keepdims=True) - xhat * jnp.mean(dxh * xhat, axis=-1, keepdims=True))
        dz_ref[...] = dz
        lo_ref[...] = dz.astype(lo_ref.dtype)
        dg_ref[...] += jnp.sum(dy * xhat, axis=0, keepdims=True)
        db_ref[...] += jnp.sum(dy, axis=0, keepdims=True)
        if top:
            per_row = jnp.sum(err * err, axis=-1, keepdims=True) * (0.5 / D_MODEL)
            loss_ref[...] += jnp.sum(per_row, axis=0, keepdims=True)

    rows = pl.BlockSpec((tm, D_MODEL), lambda i: (i, 0))
    vec = pl.BlockSpec((1, D_MODEL), lambda i: (0, 0))
    in_specs = [rows, vec] + ([rows, rows] if top else [rows])
    args = [z, ln_g] + ([xn, target] if top else [dxn])
    out_specs = [rows, rows, vec, vec]
    out_shape = [jax.ShapeDtypeStruct((s_len, D_MODEL), F32), jax.ShapeDtypeStruct((s_len, D_MODEL), MXU_DTYPE),
                 jax.ShapeDtypeStruct((1, D_MODEL), F32), jax.ShapeDtypeStruct((1, D_MODEL), F32)]
    if top:
        out_specs.append(pl.BlockSpec((1, 1), lambda i: (0, 0)))
        out_shape.append(jax.ShapeDtypeStruct((1, 1), F32))
    return _call(body, name=name, grid=(s_len // tm,), in_specs=in_specs, out_specs=out_specs,
                 out_shape=out_shape, compiler_params=_params("arbitrary"))(*args)


CONV_ROWS = 512
CONV_ROWS_FWD = 1024
HALO = 8


def _shift_down(x, halo, s):
    if s == 0:
        return x
    ext = jnp.concatenate([halo, x], axis=0)
    return pltpu.roll(ext, s, 0)[HALO:, :]


def _shift_up(x, halo, s):
    if s == 0:
        return x
    ext = jnp.concatenate([x, halo], axis=0)
    return pltpu.roll(ext, ext.shape[0] - s, 0)[:x.shape[0], :]


def _conv_fwd(src, width, w, bias, *, name):
    s_len = src.shape[0]
    rows = min(CONV_ROWS_FWD, s_len)
    per = rows // HALO

    def body(x_ref, halo_ref, w_ref, b_ref, o_ref):
        x = x_ref[...]
        halo = jnp.where(pl.program_id(0) == 0, 0.0, halo_ref[...])
        acc = x * w_ref[3:4, :] + b_ref[...]
        for k in range(CONV_WIDTH - 1):
            acc += _shift_down(x, halo, 3 - k) * w_ref[k:k + 1, :]
        o_ref[...] = acc

    return _call(
        body, name=name, grid=(s_len // rows,),
        in_specs=[pl.BlockSpec((rows, width), lambda i: (i, 0)),
                  pl.BlockSpec((HALO, width), lambda i: (jnp.maximum(i * per - 1, 0), 0)),
                  pl.BlockSpec((CONV_WIDTH, width), lambda i: (0, 0)), pl.BlockSpec((1, width), lambda i: (0, 0))],
        out_specs=pl.BlockSpec((rows, width), lambda i: (i, 0)),
        out_shape=jax.ShapeDtypeStruct((s_len, width), F32),
        compiler_params=_params("parallel"),
    )(src, src, w, bias)


def _conv_bwd(dy, src, width, w, passthrough, *, name):
    s_len = src.shape[0]
    rows = min(CONV_ROWS, s_len)
    per = rows // HALO
    nblk = s_len // rows
    extra = [p.shape[1] for p in passthrough]
    total = width + sum(extra)

    def body(*refs):
        dy_ref, dyh_ref, x_ref, xh_ref, w_ref = refs[:5]
        p_refs = refs[5:5 + len(extra)]
        o_ref, dw_ref, db_ref = refs[5 + len(extra):]
        i = pl.program_id(0)

        @pl.when(i == 0)
        def _():
            dw_ref[...] = jnp.zeros_like(dw_ref)
            db_ref[...] = jnp.zeros_like(db_ref)

        dy = dy_ref[...]
        x = x_ref[...]
        dy_halo = jnp.where(i == nblk - 1, 0.0, dyh_ref[...])
        x_halo = jnp.where(i == 0, 0.0, xh_ref[...])
        dx = dy * w_ref[3:4, :]
        dw_ref[3] += jnp.sum(dy * x, axis=0, keepdims=True)
        for k in range(CONV_WIDTH - 1):
            dx += _shift_up(dy, dy_halo, 3 - k) * w_ref[k:k + 1, :]
            dw_ref[k] += jnp.sum(dy * _shift_down(x, x_halo, 3 - k), axis=0, keepdims=True)
        db_ref[...] += jnp.sum(dy, axis=0, keepdims=True)
        o_ref[:, 0:width] = dx.astype(o_ref.dtype)
        off = width
        for p_ref, wd in zip(p_refs, extra):
            o_ref[:, off:off + wd] = p_ref[...].astype(o_ref.dtype)
            off += wd

    in_specs = [pl.BlockSpec((rows, width), lambda i: (i, 0)),
                pl.BlockSpec((HALO, width), lambda i: (jnp.minimum((i + 1) * per, nblk * per - 1), 0)),
                pl.BlockSpec((rows, width), lambda i: (i, 0)),
                pl.BlockSpec((HALO, width), lambda i: (jnp.maximum(i * per - 1, 0), 0)),
                pl.BlockSpec((CONV_WIDTH, width), lambda i: (0, 0))]
    in_specs += [pl.BlockSpec((rows, wd), lambda i: (i, 0)) for wd in extra]
    return _call(
        body, name=name, grid=(nblk,), in_specs=in_specs,
        out_specs=[pl.BlockSpec((rows, total), lambda i: (i, 0)),
                   pl.BlockSpec((CONV_WIDTH, 1, width), lambda i: (0, 0, 0)), pl.BlockSpec((1, width), lambda i: (0, 0))],
        out_shape=[jax.ShapeDtypeStruct((s_len, total), MXU_DTYPE), jax.ShapeDtypeStruct((CONV_WIDTH, 1, width), F32),
                   jax.ShapeDtypeStruct((1, width), F32)],
        compiler_params=_params("arbitrary"),
    )(dy, dy, src, src, w, *passthrough)


def _attn_mask(first):
    i = _iota((A_BLOCK, 2 * A_BLOCK), 0)
    j = _iota((A_BLOCK, 2 * A_BLOCK), 1)
    band = (j > i) & (j <= i + A_BLOCK)
    return band & ((j >= A_BLOCK) | jnp.logical_not(first))


def _attn_group(p, mask, qg, kw, kws, vw, vws, azg, sink0, sink1):
    low = _iota(qg.shape, 1) < A_HEAD_DIM
    first_lane = (_iota((A_BLOCK, LANES), 1) == 0).astype(F32)
    out = None
    for half, sink in ((0, sink0), (1, sink1)):
        kv_head = (2 * p + half) // (A_HEADS // A_KV_HEADS)
        keep = low if half == 0 else jnp.logical_not(low)
        qm = jnp.where(keep, qg, 0.0)
        kk, vv = (kw, vw) if kv_head == half else (kws, vws)
        s = mm_nt(qm, kk) * (A_HEAD_DIM ** -0.5)
        s = jnp.where(mask, s, NEG)
        sk = jnp.sum(jnp.tile(sink, (A_BLOCK // 8, 1)) * first_lane, axis=1, keepdims=True)
        m = lax.stop_gradient(jnp.maximum(jnp.max(s, axis=1, keepdims=True), sk))
        e = jnp.exp(s - m)
        denom = jnp.sum(e, axis=1, keepdims=True) + jnp.exp(sk - m)
        o = mm_nn(e * (1.0 / denom), vv)
        o = jnp.where(keep, o, 0.0)
        out = o if out is None else out + o
    return out * _silu(azg)


def _attn_specs(s_len, rev):
    nb = s_len // A_BLOCK

    def cur(i):
        return nb - 1 - i if rev else i

    def prev(i):
        return jnp.maximum(cur(i) - 1, 0)

    def blk(width, col, which):
        return pl.BlockSpec((A_BLOCK, width), lambda i: (which(i), col))

    return [blk(A_WIDTH, 0, cur), blk(A_WIDTH, 1, cur), blk(LANES, 8, cur), blk(LANES, 9, cur),
            blk(LANES, 8, prev), blk(LANES, 9, prev), blk(LANES, 0, cur), blk(LANES, 0, cur),
            blk(LANES, 0, prev), blk(LANES, 0, prev)], cur


def _attn_fwd(proj_a, cos, sin, sinks_t, *, name):
    s_len = proj_a.shape[0]
    specs, _ = _attn_specs(s_len, False)

    def body(q_ref, az_ref, k_ref, v_ref, kp_ref, vp_ref, c_ref, s_ref, cp_ref, sp_ref, sink_ref, o_ref):
        first = pl.program_id(0) == 0
        mask = _attn_mask(first)
        qr = _rope(q_ref[...], c_ref[...], s_ref[...])
        kw = jnp.concatenate([_rope(kp_ref[...], cp_ref[...], sp_ref[...]), _rope(k_ref[...], c_ref[...], s_ref[...])], 0)
        vw = jnp.concatenate([vp_ref[...], v_ref[...]], 0)
        kws, vws = _swap64(kw), _swap64(vw)
        for p in range(A_WIDTH // LANES):
            cols = slice(p * LANES, (p + 1) * LANES)
            o = _attn_group(p, mask, qr[:, cols], kw, kws, vw, vws, az_ref[:, cols], sink_ref[2 * p], sink_ref[2 * p + 1])
            o_ref[:, cols] = o.astype(o_ref.dtype)

    return _call(
        body, name=name, grid=(s_len // A_BLOCK,),
        in_specs=specs + [pl.BlockSpec((A_HEADS, 8, LANES), lambda i: (0, 0, 0))],
        out_specs=pl.BlockSpec((A_BLOCK, A_WIDTH), lambda i: (i, 0)),
        out_shape=jax.ShapeDtypeStruct((s_len, A_WIDTH), MXU_DTYPE),
        compiler_params=_params("parallel"),
    )(proj_a, proj_a, proj_a, proj_a, proj_a, proj_a, cos, sin, cos, sin, sinks_t)


def _attn_bwd(proj_a, cos, sin, sinks_t, dya, *, name):
    s_len = proj_a.shape[0]
    specs, cur = _attn_specs(s_len, True)

    def body(q_ref, az_ref, k_ref, v_ref, kp_ref, vp_ref, c_ref, s_ref, cp_ref, sp_ref, sink_ref, dy_ref,
             o_ref, dsink_ref, dk_carry, dv_carry):
        i = pl.program_id(0)

        @pl.when(i == 0)
        def _():
            dsink_ref[...] = jnp.zeros_like(dsink_ref)
            dk_carry[...] = jnp.zeros_like(dk_carry)
            dv_carry[...] = jnp.zeros_like(dv_carry)

        first = cur(i) == 0
        mask = _attn_mask(first)
        cos_c, sin_c = c_ref[...], s_ref[...]
        qr = _rope(q_ref[...], cos_c, sin_c)
        kw = jnp.concatenate([_rope(kp_ref[...], cp_ref[...], sp_ref[...]), _rope(k_ref[...], cos_c, sin_c)], 0)
        vw = jnp.concatenate([vp_ref[...], v_ref[...]], 0)
        kws, vws = _swap64(kw), _swap64(vw)
        dkw = jnp.zeros_like(kw)
        dvw = jnp.zeros_like(vw)
        for p in range(A_WIDTH // LANES):
            cols = slice(p * LANES, (p + 1) * LANES)
            _, vjp = jax.vjp(functools.partial(_attn_group, p, mask), qr[:, cols], kw, kws, vw, vws, az_ref[:, cols],
                             sink_ref[2 * p], sink_ref[2 * p + 1])
            dq, dk1, dk2, dv1, dv2, daz, ds0, ds1 = vjp(dy_ref[:, cols])
            dkw += dk1 + _swap64(dk2)
            dvw += dv1 + _swap64(dv2)
            o_ref[:, cols] = _rope_t(dq, cos_c, sin_c).astype(o_ref.dtype)
            o_ref[:, A_WIDTH + p * LANES:A_WIDTH + (p + 1) * LANES] = daz.astype(o_ref.dtype)
            dsink_ref[2 * p] += ds0
            dsink_ref[2 * p + 1] += ds1
        o_ref[:, 2 * A_WIDTH:2 * A_WIDTH + LANES] = _rope_t(dkw[A_BLOCK:, :] + dk_carry[...], cos_c, sin_c).astype(o_ref.dtype)
        o_ref[:, 2 * A_WIDTH + LANES:] = (dvw[A_BLOCK:, :] + dv_carry[...]).astype(o_ref.dtype)
        dk_carry[...] = dkw[:A_BLOCK, :]
        dv_carry[...] = dvw[:A_BLOCK, :]

    return _call(
        body, name=name, grid=(s_len // A_BLOCK,),
        in_specs=specs + [pl.BlockSpec((A_HEADS, 8, LANES), lambda i: (0, 0, 0)),
                          pl.BlockSpec((A_BLOCK, A_WIDTH), lambda i: (cur(i), 0))],
        out_specs=[pl.BlockSpec((A_BLOCK, WA), lambda i: (cur(i), 0)),
                   pl.BlockSpec((A_HEADS, 8, LANES), lambda i: (0, 0, 0))],
        out_shape=[jax.ShapeDtypeStruct((s_len, WA), MXU_DTYPE), jax.ShapeDtypeStruct((A_HEADS, 8, LANES), F32)],
        scratch_shapes=[pltpu.VMEM((A_BLOCK, LANES), F32), pltpu.VMEM((A_BLOCK, LANES), F32)],
        compiler_params=_params("arbitrary"),
    )(proj_a, proj_a, proj_a, proj_a, proj_a, proj_a, cos, sin, cos, sin, sinks_t, dya)


RG_ROWS = 256


def _rg_gates(x, wa, ba, wx, bx, lam):
    r = jax.nn.sigmoid(mm_nn(x, wa) + ba)
    ig = jax.nn.sigmoid(mm_nn(x, wx) + bx)
    log_a = -R_C * r * _softplus(-lam)
    a = jnp.exp(log_a)
    return a, jnp.sqrt(_one_minus_sq(log_a, a)) * (ig * x)


def _rg_param_specs():
    mat = pl.BlockSpec((R_BLOCKS, R_BLOCK_DIM, R_BLOCK_DIM), lambda i: (0, 0, 0))
    vec = pl.BlockSpec((1, R_WIDTH), lambda i: (0, 0))
    return [mat, vec, mat, vec, vec]


def _rg_fwd(xr, proj_r, wa, ba, wx, bx, lam, *, name):
    s_len = xr.shape[0]
    rows = min(RG_ROWS, s_len)

    def body(x_ref, z_ref, wa_ref, ba_ref, wx_ref, bx_ref, lam_ref, h_ref, y_ref, a_buf, u_buf, carry):
        @pl.when(pl.program_id(0) == 0)
        def _():
            carry[...] = jnp.zeros_like(carry)

        for n in range(R_BLOCKS):
            cols = slice(n * R_BLOCK_DIM, (n + 1) * R_BLOCK_DIM)
            a, u = _rg_gates(x_ref[:, cols], wa_ref[n], ba_ref[:, cols], wx_ref[n], bx_ref[:, cols], lam_ref[:, cols])
            a_buf[:, cols] = a
            u_buf[:, cols] = u

        def step(t, h):
            h = a_buf[pl.ds(t, 1), :] * h + u_buf[pl.ds(t, 1), :]
            h_ref[pl.ds(t, 1), :] = h
            return h

        carry[...] = lax.fori_loop(0, rows, step, carry[...], unroll=16)
        y_ref[...] = (h_ref[...] * _silu(z_ref[...])).astype(y_ref.dtype)

    blk = pl.BlockSpec((rows, R_WIDTH), lambda i: (i, 0))
    return _call(
        body, name=name, grid=(s_len // rows,),
        in_specs=[blk, pl.BlockSpec((rows, R_WIDTH), lambda i: (i, 1))] + _rg_param_specs(),
        out_specs=[blk, blk],
        out_shape=[jax.ShapeDtypeStruct((s_len, R_WIDTH), F32), jax.ShapeDtypeStruct((s_len, R_WIDTH), MXU_DTYPE)],
        scratch_shapes=[pltpu.VMEM((rows, R_WIDTH), F32), pltpu.VMEM((rows, R_WIDTH), F32), pltpu.VMEM((1, R_WIDTH), F32)],
        compiler_params=_params("arbitrary"),
    )(xr, proj_r, wa, ba, wx, bx, lam)


def _rg_bwd(xr, proj_r, h, dyr, wa, ba, wx, bx, lam, *, name):
    s_len = xr.shape[0]
    rows = min(RG_ROWS, s_len)
    nblk = s_len // rows
    per = rows // HALO

    def cur(i):
        return nblk - 1 - i

    def body(x_ref, z_ref, h_ref, hh_ref, dy_ref, wa_ref, ba_ref, wx_ref, bx_ref, lam_ref,
             dx_ref, dz_ref, dwa_ref, dba_ref, dwx_ref, dbx_ref, dlam_ref, a_buf, g_buf, carry):
        i = pl.program_id(0)

        @pl.when(i == 0)
        def _():
            carry[...] = jnp.zeros_like(carry)
            for ref in (dwa_ref, dba_ref, dwx_ref, dbx_ref, dlam_ref):
                ref[...] = jnp.zeros_like(ref)

        z = z_ref[...]
        sig = jax.nn.sigmoid(z)
        hval = h_ref[...]
        dy = dy_ref[...]
        dz_ref[...] = dy * hval * (sig * (1.0 + z * (1.0 - sig)))
        g_buf[...] = dy * (z * sig)
        kept = []
        for n in range(R_BLOCKS):
            cols = slice(n * R_BLOCK_DIM, (n + 1) * R_BLOCK_DIM)
            x = x_ref[:, cols]
            r = jax.nn.sigmoid(_nn(x, wa_ref[n]) + ba_ref[:, cols])
            ig = jax.nn.sigmoid(_nn(x, wx_ref[n]) + bx_ref[:, cols])
            sp = _softplus(-lam_ref[:, cols])
            log_a = -R_C * r * sp
            a = jnp.exp(log_a)
            a_buf[:, cols] = a
            kept.append((x, r, ig, sp, a, jnp.sqrt(_one_minus_sq(log_a, a))))

        def step(k, c):
            t = rows - 1 - k
            g = g_buf[pl.ds(t, 1), :] + c
            g_buf[pl.ds(t, 1), :] = g
            return a_buf[pl.ds(t, 1), :] * g

        carry[...] = lax.fori_loop(0, rows, step, carry[...], unroll=16)
        h_halo = jnp.where(cur(i) == 0, 0.0, hh_ref[...])
        dh = g_buf[...]
        da = dh * _shift_down(hval, h_halo, 1)
        for n in range(R_BLOCKS):
            cols = slice(n * R_BLOCK_DIM, (n + 1) * R_BLOCK_DIM)
            x, r, ig, sp, a, s = kept[n]
            du = dh[:, cols]
            dux = du * x
            d_log_a = a * (da[:, cols] - a * (dux * ig) / s)
            d_ga = d_log_a * (-R_C * sp) * (r * (1.0 - r))
            d_gx = dux * s * (ig * (1.0 - ig))
            dx_ref[:, cols] = du * (s * ig) + _nt(d_ga, wa_ref[n]) + _nt(d_gx, wx_ref[n])
            xt = _t(x)
            dwa_ref[n] += _nn(xt, d_ga)
            dwx_ref[n] += _nn(xt, d_gx)
            dba_ref[:, cols] += jnp.sum(d_ga, axis=0, keepdims=True)
            dbx_ref[:, cols] += jnp.sum(d_gx, axis=0, keepdims=True)
            dlam_ref[:, cols] += jnp.sum(d_log_a * r, axis=0, keepdims=True) * (R_C * jax.nn.sigmoid(-lam_ref[:, cols]))

    blk = pl.BlockSpec((rows, R_WIDTH), lambda i: (cur(i), 0))
    mat = pl.BlockSpec((R_BLOCKS, R_BLOCK_DIM, R_BLOCK_DIM), lambda i: (0, 0, 0))
    vec = pl.BlockSpec((1, R_WIDTH), lambda i: (0, 0))
    return _call(
        body, name=name, grid=(nblk,),
        in_specs=[blk, pl.BlockSpec((rows, R_WIDTH), lambda i: (cur(i), 1)), blk,
                  pl.BlockSpec((HALO, R_WIDTH), lambda i: (jnp.maximum(cur(i) * per - 1, 0), 0)), blk] + _rg_param_specs(),
        out_specs=[blk, blk, mat, vec, mat, vec, vec],
        out_shape=[jax.ShapeDtypeStruct((s_len, R_WIDTH), F32)] * 2 + [
            jax.ShapeDtypeStruct((R_BLOCKS, R_BLOCK_DIM, R_BLOCK_DIM), F32), jax.ShapeDtypeStruct((1, R_WIDTH), F32),
            jax.ShapeDtypeStruct((R_BLOCKS, R_BLOCK_DIM, R_BLOCK_DIM), F32), jax.ShapeDtypeStruct((1, R_WIDTH), F32),
            jax.ShapeDtypeStruct((1, R_WIDTH), F32)],
        scratch_shapes=[pltpu.VMEM((rows, R_WIDTH), F32), pltpu.VMEM((rows, R_WIDTH), F32), pltpu.VMEM((1, R_WIDTH), F32)],
        compiler_params=_params("arbitrary"),
    )(xr, proj_r, h, h, dyr, wa, ba, wx, bx, lam)


GP_CHUNKS = 8
GP_CHUNKS_BWD = 4
GS_CHUNKS = 8


def _seg_cumsum(x, reverse):
    rows = x.shape[0]
    r = _iota(x.shape, 0) & (G_CHUNK - 1)
    s = 1
    while s < G_CHUNK:
        if reverse:
            x = x + jnp.where(r < G_CHUNK - s, pltpu.roll(x, rows - s, 0), 0.0)
        else:
            x = x + jnp.where(r >= s, pltpu.roll(x, s, 0), 0.0)
        s *= 2
    return x


def _gdn_decay(ga, a_log_row, dt_row):
    return -jnp.exp(a_log_row) * _softplus(ga + dt_row)


def _gdn_chunk(cq, ck, cv, gb, gc, inv=None):
    shape = cq.shape
    head = _iota(shape, 0) & (G_HEADS - 1)
    lane = _iota(shape, 2)
    q, k, v = _silu(cq), _silu(ck), _silu(cv)
    q = q * lax.rsqrt(jnp.sum(q * q, axis=-1, keepdims=True) + RMS_EPS) * (G_HEAD_DIM ** -0.5)
    k = k * lax.rsqrt(jnp.sum(k * k, axis=-1, keepdims=True) + RMS_EPS)
    beta = jnp.sum(jnp.where(lane == head, jax.nn.sigmoid(gb), 0.0), axis=-1, keepdims=True)
    g = jnp.sum(jnp.where(lane == head + G_HEADS, gc, 0.0), axis=-1, keepdims=True)
    sq = (shape[0], G_CHUNK, G_CHUNK)
    row, col = _iota(sq, 1), _iota(sq, 2)
    g_sq = jnp.broadcast_to(g, sq)
    decay = jnp.where(row >= col, jnp.exp(jnp.minimum(g_sq - _t(g_sq), 0.0)), 0.0)
    g_last = jnp.sum(jnp.where(_iota(g.shape, 1) == G_CHUNK - 1, g, 0.0), axis=1, keepdims=True)
    eg = jnp.exp(g)
    kb, vb = k * beta, v * beta
    m = jnp.where(row > col, mm_nt(kb, k) * decay, 0.0)
    known = inv is not None
    if not known:
        inv = _inv_unit_lower(m)
    u, w = _solve2(m, inv, vb, kb * eg)
    qk = jnp.where(row >= col, mm_nt(q, k) * decay, 0.0)
    q_dec = q * eg
    k_dec = k * jnp.exp(g_last - g)
    gl = jnp.broadcast_to(jnp.exp(g_last), (shape[0], 1, G_HEAD_DIM))
    return (u, w, qk, q_dec, k_dec, gl) if known else (u, w, qk, q_dec, k_dec, gl, inv)


def _gdn_step(state, u, w, qk, q_dec, k_dec, gl, gz, norm_w):
    v_new = u - mm_nn(w, state)
    o = mm_nn(q_dec, state) + mm_nn(qk, v_new)
    new_state = state * gl + mm_nn(_t(k_dec), v_new)
    o = o * lax.rsqrt(jnp.mean(o * o, axis=-1, keepdims=True) + RMS_EPS) * norm_w
    return o * _silu(gz), new_state


def _stack_chunks(x, heads):
    chunks = x.shape[0] // G_CHUNK
    parts = []
    for c in range(chunks):
        rows = slice(c * G_CHUNK, (c + 1) * G_CHUNK)
        for hd in range(G_HEADS):
            parts.append(x[rows, hd * LANES:(hd + 1) * LANES] if heads else x[rows, :])
    return jnp.stack(parts)


def _gdn_chunk_shapes(nch):
    b = nch * G_HEADS
    wide = jax.ShapeDtypeStruct((b, G_CHUNK, G_HEAD_DIM), F32)
    return [wide, wide, jax.ShapeDtypeStruct((b, G_CHUNK, G_CHUNK), F32), wide, wide,
            jax.ShapeDtypeStruct((b, 1, G_HEAD_DIM), F32)]


def _gdn_chunk_specs(nbatch):
    wide = pl.BlockSpec((nbatch, G_CHUNK, G_HEAD_DIM), lambda i: (i, 0, 0))
    return [wide, wide, pl.BlockSpec((nbatch, G_CHUNK, G_CHUNK), lambda i: (i, 0, 0)), wide, wide,
            pl.BlockSpec((nbatch, 1, G_HEAD_DIM), lambda i: (i, 0, 0))]


def _gdn_chunk_fwd(conv, proj_g, a_log_row, dt_row, *, name):
    s_len = conv.shape[0]
    cpg = min(GP_CHUNKS, s_len // G_CHUNK)
    rows = cpg * G_CHUNK
    nbatch = cpg * G_HEADS

    def body(c_ref, bg_ref, al_ref, dt_ref, *outs):
        bg = bg_ref[...]
        gc = _seg_cumsum(_gdn_decay(bg, al_ref[...], dt_ref[...]), False)
        res = _gdn_chunk(_stack_chunks(c_ref[:, 0:G_WIDTH], True), _stack_chunks(c_ref[:, G_WIDTH:2 * G_WIDTH], True),
                         _stack_chunks(c_ref[:, 2 * G_WIDTH:], True), _stack_chunks(bg, False), _stack_chunks(gc, False))
        for ref, val in zip(outs, res):
            ref[...] = val

    row = pl.BlockSpec((1, LANES), lambda i: (0, 0))
    return _call(
        body, name=name, grid=(s_len // rows,),
        in_specs=[pl.BlockSpec((rows, 3 * G_WIDTH), lambda i: (i, 0)),
                  pl.BlockSpec((rows, LANES), lambda i: (i, (3 * G_WIDTH + G_WIDTH) // LANES)), row, row],
        out_specs=_gdn_chunk_specs(nbatch) + [pl.BlockSpec((nbatch, G_CHUNK, G_CHUNK), lambda i: (i, 0, 0))],
        out_shape=_gdn_chunk_shapes(s_len // G_CHUNK) + [
            jax.ShapeDtypeStruct((s_len // G_CHUNK * G_HEADS, G_CHUNK, G_CHUNK), F32)],
        compiler_params=_params("parallel"),
    )(conv, proj_g, a_log_row, dt_row)


def _gdn_chunk_bwd(conv, proj_g, a_log_row, dt_row, inv, cots, *, name):
    s_len = conv.shape[0]
    cpg = min(GP_CHUNKS_BWD, s_len // G_CHUNK)
    rows = cpg * G_CHUNK
    nbatch = cpg * G_HEADS

    def unstack(x, heads):
        if heads:
            return jnp.concatenate([jnp.concatenate([x[c * G_HEADS + hd] for hd in range(G_HEADS)], axis=1)
                                    for c in range(cpg)], axis=0)
        return jnp.concatenate([sum(x[c * G_HEADS + hd] for hd in range(G_HEADS)) for c in range(cpg)], axis=0)

    def body(c_ref, bg_ref, al_ref, dt_ref, inv_ref, du, dw, dqk, dqd, dkd, dgl, dc_ref, dbg_ref, dal_ref, ddt_ref):
        @pl.when(pl.program_id(0) == 0)
        def _():
            dal_ref[...] = jnp.zeros_like(dal_ref)
            ddt_ref[...] = jnp.zeros_like(ddt_ref)

        bg = bg_ref[...]
        g_all, decay_vjp = jax.vjp(_gdn_decay, bg, al_ref[...], dt_ref[...])
        gc = _seg_cumsum(g_all, False)
        _, vjp = jax.vjp(_gdn_chunk, _stack_chunks(c_ref[:, 0:G_WIDTH], True),
                         _stack_chunks(c_ref[:, G_WIDTH:2 * G_WIDTH], True), _stack_chunks(c_ref[:, 2 * G_WIDTH:], True),
                         _stack_chunks(bg, False), _stack_chunks(gc, False), inv_ref[...])
        dq, dk, dv, dgb, dgc, _ = vjp((du[...], dw[...], dqk[...], dqd[...], dkd[...], dgl[...]))
        dc_ref[:, 0:G_WIDTH] = unstack(dq, True)
        dc_ref[:, G_WIDTH:2 * G_WIDTH] = unstack(dk, True)
        dc_ref[:, 2 * G_WIDTH:] = unstack(dv, True)
        dga, dal, ddt = decay_vjp(_seg_cumsum(unstack(dgc, False), True))
        dbg_ref[:, 0:LANES] = unstack(dgb, False) + dga
        dbg_ref[:, LANES:] = jnp.zeros((rows, LANES), F32)
        dal_ref[...] += dal
        ddt_ref[...] += ddt

    row = pl.BlockSpec((1, LANES), lambda i: (0, 0))
    return _call(
        body, name=name, grid=(s_len // rows,),
        in_specs=[pl.BlockSpec((rows, 3 * G_WIDTH), lambda i: (i, 0)),
                  pl.BlockSpec((rows, LANES), lambda i: (i, (3 * G_WIDTH + G_WIDTH) // LANES)), row, row,
                  pl.BlockSpec((nbatch, G_CHUNK, G_CHUNK), lambda i: (i, 0, 0))]
        + _gdn_chunk_specs(nbatch),
        out_specs=[pl.BlockSpec((rows, 3 * G_WIDTH), lambda i: (i, 0)), pl.BlockSpec((rows, 2 * LANES), lambda i: (i, 0)),
                   row, row],
        out_shape=[jax.ShapeDtypeStruct((s_len, 3 * G_WIDTH), F32), jax.ShapeDtypeStruct((s_len, 2 * LANES), F32),
                   jax.ShapeDtypeStruct((1, LANES), F32), jax.ShapeDtypeStruct((1, LANES), F32)],
        compiler_params=_params("arbitrary"),
    )(conv, proj_g, a_log_row, dt_row, inv, *cots)


def _gdn_scan_specs(cpg, which):
    nbatch = cpg * G_HEADS
    wide = pl.BlockSpec((nbatch, G_CHUNK, G_HEAD_DIM), lambda i: (which(i), 0, 0))
    return [wide, wide, pl.BlockSpec((nbatch, G_CHUNK, G_CHUNK), lambda i: (which(i), 0, 0)), wide, wide,
            pl.BlockSpec((nbatch, 1, G_HEAD_DIM), lambda i: (which(i), 0, 0))]


def _gz_stack(z_ref, c):
    rows = pl.ds(pl.multiple_of(c * G_CHUNK, G_CHUNK), G_CHUNK)
    return jnp.stack([z_ref[rows, hd * LANES:(hd + 1) * LANES] for hd in range(G_HEADS)])


def _gdn_scan_fwd(chunk_vals, proj_g, norm_w, *, name):
    s_len = proj_g.shape[0]
    nch = s_len // G_CHUNK
    cpg = min(GS_CHUNKS, nch)
    rows = cpg * G_CHUNK

    def body(u_ref, w_ref, qk_ref, qd_ref, kd_ref, gl_ref, z_ref, nw_ref, y_ref, st_ref, state):
        @pl.when(pl.program_id(0) == 0)
        def _():
            state[...] = jnp.zeros_like(state)

        def step(c, carry):
            b = pl.ds(pl.multiple_of(c * G_HEADS, G_HEADS), G_HEADS)
            st = state[...]
            st_ref[b] = st
            y, new_state = _gdn_step(st, u_ref[b], w_ref[b], qk_ref[b], qd_ref[b], kd_ref[b], gl_ref[b],
                                     _gz_stack(z_ref, c), nw_ref[...])
            state[...] = new_state
            rws = pl.ds(pl.multiple_of(c * G_CHUNK, G_CHUNK), G_CHUNK)
            for hd in range(G_HEADS):
                y_ref[rws, hd * LANES:(hd + 1) * LANES] = y[hd].astype(y_ref.dtype)
            return carry

        lax.fori_loop(0, cpg, step, 0, unroll=2)

    return _call(
        body, name=name, grid=(nch // cpg,),
        in_specs=_gdn_scan_specs(cpg, lambda i: i) + [
            pl.BlockSpec((rows, G_WIDTH), lambda i: (i, 3)), pl.BlockSpec((1, G_HEAD_DIM), lambda i: (0, 0))],
        out_specs=[pl.BlockSpec((rows, G_WIDTH), lambda i: (i, 0)),
                   pl.BlockSpec((cpg * G_HEADS, G_HEAD_DIM, G_HEAD_DIM), lambda i: (i, 0, 0))],
        out_shape=[jax.ShapeDtypeStruct((s_len, G_WIDTH), MXU_DTYPE),
                   jax.ShapeDtypeStruct((nch * G_HEADS, G_HEAD_DIM, G_HEAD_DIM), F32)],
        scratch_shapes=[pltpu.VMEM((G_HEADS, G_HEAD_DIM, G_HEAD_DIM), F32)],
        compiler_params=_params("arbitrary"),
    )(*chunk_vals, proj_g, norm_w)


def _gdn_scan_bwd(chunk_vals, states, proj_g, norm_w, dyg, *, name):
    s_len = proj_g.shape[0]
    nch = s_len // G_CHUNK
    cpg = min(GS_CHUNKS, nch)
    rows = cpg * G_CHUNK
    ngrid = nch // cpg

    def cur(i):
        return ngrid - 1 - i

    def body(u_ref, w_ref, qk_ref, qd_ref, kd_ref, gl_ref, st_ref, z_ref, nw_ref, dy_ref,
             du_ref, dw_ref, dqk_ref, dqd_ref, dkd_ref, dgl_ref, dz_ref, dnw_ref, dstate):
        @pl.when(pl.program_id(0) == 0)
        def _():
            dstate[...] = jnp.zeros_like(dstate)
            dnw_ref[...] = jnp.zeros_like(dnw_ref)

        def step(k, carry):
            c = cpg - 1 - k
            b = pl.ds(pl.multiple_of(c * G_HEADS, G_HEADS), G_HEADS)
            _, vjp = jax.vjp(_gdn_step, st_ref[b], u_ref[b], w_ref[b], qk_ref[b], qd_ref[b], kd_ref[b], gl_ref[b],
                             _gz_stack(z_ref, c), nw_ref[...])
            dst, du, dw, dqk, dqd, dkd, dgl, dz, dnw = vjp((_gz_stack(dy_ref, c), dstate[...]))
            dstate[...] = dst
            du_ref[b], dw_ref[b], dqk_ref[b], dqd_ref[b], dkd_ref[b], dgl_ref[b] = du, dw, dqk, dqd, dkd, dgl
            rws = pl.ds(pl.multiple_of(c * G_CHUNK, G_CHUNK), G_CHUNK)
            for hd in range(G_HEADS):
                dz_ref[rws, hd * LANES:(hd + 1) * LANES] = dz[hd]
            dnw_ref[...] += dnw
            return carry

        lax.fori_loop(0, cpg, step, 0, unroll=2)

    gate = pl.BlockSpec((rows, G_WIDTH), lambda i: (cur(i), 3))
    wide = pl.BlockSpec((rows, G_WIDTH), lambda i: (cur(i), 0))
    vec = pl.BlockSpec((1, G_HEAD_DIM), lambda i: (0, 0))
    return _call(
        body, name=name, grid=(ngrid,),
        in_specs=_gdn_scan_specs(cpg, cur) + [
            pl.BlockSpec((cpg * G_HEADS, G_HEAD_DIM, G_HEAD_DIM), lambda i: (cur(i), 0, 0)), gate, vec, wide],
        out_specs=_gdn_scan_specs(cpg, cur) + [wide, vec],
        out_shape=_gdn_chunk_shapes(nch) + [jax.ShapeDtypeStruct((s_len, G_WIDTH), F32),
                                            jax.ShapeDtypeStruct((1, G_HEAD_DIM), F32)],
        scratch_shapes=[pltpu.VMEM((G_HEADS, G_HEAD_DIM, G_HEAD_DIM), F32)],
        compiler_params=_params("arbitrary"),
    )(*chunk_vals, states, proj_g, norm_w, dyg)


def _adamw_math(w, g, m, v):
    m = ADAM_B1 * m + (1.0 - ADAM_B1) * g
    v = ADAM_B2 * v + (1.0 - ADAM_B2) * (g * g)
    m_hat = m / (1.0 - ADAM_B1 ** ADAM_STEP)
    v_hat = v / (1.0 - ADAM_B2 ** ADAM_STEP)
    delta = -ADAM_LR * (m_hat / (jnp.sqrt(v_hat) + ADAM_EPS) + ADAM_WD * w)
    return delta, m, v


def _sum_adamw(own, chip, parts, w, m, v, *, name, rows):
    n_layers, n_rows, n_cols = w.shape
    rows = min(rows, n_rows)
    n_parts = parts[0].shape[0]

    def body(c_ref, *refs):
        own_refs, part_refs = refs[:n_layers], refs[n_layers:2 * n_layers]
        w_ref, m_ref, v_ref, g_ref, d_ref, nm_ref, nv_ref = refs[2 * n_layers:]
        layer = pl.program_id(0)
        g = None
        for l in range(n_layers):
            g_l = own_refs[l][0].astype(F32)
            for k in range(n_parts):
                g_l = g_l + part_refs[l][k].astype(F32)
            g = g_l if g is None else jnp.where(layer == l, g_l, g)
        delta, new_m, new_v = _adamw_math(w_ref[0], g, m_ref[0], v_ref[0])
        g_ref[0], d_ref[0], nm_ref[0], nv_ref[0] = g, delta, new_m, new_v

    blk = pl.BlockSpec((1, rows, n_cols), lambda l, i, c: (l, i, 0))
    grid_spec = pltpu.PrefetchScalarGridSpec(
        num_scalar_prefetch=1, grid=(n_layers, n_rows // rows),
        in_specs=[pl.BlockSpec((1, rows, n_cols), lambda l, i, c: (c[0], i, 0))] * n_layers
        + [pl.BlockSpec((n_parts, rows, n_cols), lambda l, i, c: (0, i, 0))] * n_layers + [blk, blk, blk],
        out_specs=[blk] * 4)
    return _call(
        body, name=name, grid_spec=grid_spec, out_shape=[jax.ShapeDtypeStruct(w.shape, F32)] * 4,
        compiler_params=_params("parallel", "parallel"),
    )(_index_operand(chip), *own, *parts, w, m, v)


def _sum_slots(parts, *, name):
    rows = parts.shape[1]

    def body(p_ref, o_ref):
        g = p_ref[0]
        for k in range(1, N_DEV):
            g = g + p_ref[k]
        o_ref[...] = g

    return _call(body, name=name, grid=(1,),
                 in_specs=[pl.BlockSpec(parts.shape, lambda i: (0, 0, 0))],
                 out_specs=pl.BlockSpec((rows, LANES), lambda i: (0, 0)),
                 out_shape=jax.ShapeDtypeStruct((rows, LANES), F32), compiler_params=_params("arbitrary"))(parts)


def _adamw_packed(w, g, m, v, *, name):
    def body(w_ref, g_ref, m_ref, v_ref, d_ref, nm_ref, nv_ref):
        d_ref[...], nm_ref[...], nv_ref[...] = _adamw_math(w_ref[...], g_ref[...], m_ref[...], v_ref[...])

    blk = pl.BlockSpec(w.shape, lambda i: (0, 0))
    return _call(body, name=name, grid=(1,), in_specs=[blk] * 4, out_specs=[blk] * 3,
                 out_shape=[jax.ShapeDtypeStruct(w.shape, F32)] * 3, compiler_params=_params("arbitrary"))(w, g, m, v)


A_COLS = ((0, 512), (768, 1280), (512, 768))
R_COLS = ((1280, 3328),)
G_COLS = ((3328, 5384),)


def _group_weights(wt_full):
    def take(ranges):
        return jnp.concatenate([wt_full[a:b] for a, b in ranges], axis=0)

    wt_g = jnp.concatenate([take(G_COLS), jnp.zeros((G_PAD, wt_full.shape[1]), wt_full.dtype)], axis=0)
    return take(A_COLS), take(R_COLS), wt_g


def _ungroup_grads(d_a, d_r, d_g):
    return jnp.concatenate([d_a[0:512], d_a[1024:1280], d_a[512:1024], d_r, d_g[:WG - G_PAD]], axis=0)


def _shard_rows(w):
    return jnp.pad(jnp.transpose(w, (0, 2, 1)), ((0, 0), (0, N_ROWS_PAD - N_IN_SHARD), (0, 0)))


def _unshard_rows(wt):
    return jnp.transpose(wt[:, :N_IN_SHARD], (0, 2, 1))


def _owner_blocks(dwt):
    blocks = jnp.pad(dwt.reshape(4, 2, N_IN_SHARD, D_MODEL), ((0, 0), (0, 0), (0, N_ROWS_PAD - N_IN_SHARD), (0, 0)))
    return jnp.transpose(blocks, (1, 0, 2, 3))


def _rope_tables(s_len):
    inv = 1.0 / (ROPE_THETA ** (jnp.arange(0, A_HEAD_DIM, 2, dtype=F32) / A_HEAD_DIM))
    ang = jnp.arange(s_len, dtype=F32)[:, None] * inv[None, :]
    cos, sin = jnp.cos(ang), jnp.sin(ang)
    return jnp.tile(cos, (1, 4)), jnp.tile(jnp.concatenate([-sin, sin], axis=1), (1, 2))


SMALL = ("sinks", "r_conv_w", "r_conv_b", "r_wa", "r_ba", "r_wx", "r_bx", "r_lam", "g_conv_w", "g_a_log", "g_dt_bias",
         "g_norm_w", "ln_g", "ln_b")


def _pack(leaves):
    rows = []
    for leaf in leaves:
        flat = leaf.reshape(-1)
        pad = (-flat.shape[0]) % (8 * LANES)
        rows.append(jnp.pad(flat, (0, pad)).reshape(-1, LANES))
    return jnp.concatenate(rows, axis=0)


def _unpack(packed, shapes):
    out, row = [], 0
    for shape in shapes:
        size = math.prod(shape)
        nrows = -(-size // (8 * LANES)) * 8
        out.append(packed[row:row + nrows].reshape(-1)[:size].reshape(shape))
        row += nrows
    return out


def _lane_row(vals, offset):
    return jnp.pad(vals, (offset, LANES - offset - vals.shape[0])).reshape(1, LANES)


def kernel(x, w_in, sinks, r_conv_w, r_conv_b, r_wa, r_ba, r_wx, r_bx, r_lam, g_conv_w, g_a_log, g_dt_bias, g_norm_w, w_out, ln_g, ln_b, loss_target, m_w_in, m_sinks, m_r_conv_w, m_r_conv_b, m_r_wa, m_r_ba, m_r_wx, m_r_bx, m_r_lam, m_g_conv_w, m_g_a_log, m_g_dt_bias, m_g_norm_w, m_w_out, m_ln_g, m_ln_b, v_w_in, v_sinks, v_r_conv_w, v_r_conv_b, v_r_wa, v_r_ba, v_r_wx, v_r_bx, v_r_lam, v_g_conv_w, v_g_a_log, v_g_dt_bias, v_g_norm_w, v_w_out, v_ln_g, v_ln_b):
    s_len = x.shape[1]
    x0 = x.reshape(s_len, D_MODEL)
    target = loss_target.reshape(s_len, D_MODEL)
    me = 4 * lax.axis_index("x") + 2 * lax.axis_index("y") + lax.axis_index("c")
    core, chip = lax.axis_index("c"), 2 * lax.axis_index("x") + lax.axis_index("y")

    win_pieces = _shard_rows(w_in).astype(MXU_DTYPE).reshape(DEPTH, 2, N_ROWS_PAD // 2, D_MODEL)
    wout_pieces = w_out.astype(MXU_DTYPE).reshape(DEPTH, 2, OUT_SHARD // 2, D_MODEL)
    win0_all, wout0_all, rcw_all, gcw_all = _all_gather(
        [win_pieces[0], wout_pieces[0], r_conv_w[None], g_conv_w[None]], "gather_weights")
    rcw_full = jnp.moveaxis(rcw_all[:, 0], 0, 2).reshape(DEPTH, CONV_WIDTH, R_WIDTH)
    gcw_full = jnp.moveaxis(gcw_all[:, 0], 0, 2).reshape(DEPTH, CONV_WIDTH, 3 * G_WIDTH)
    cos, sin = _rope_tables(s_len)

    def big_weights(win_all, wout_all):
        wt_a, wt_r, wt_g = _group_weights(win_all.reshape(N_DEV, N_ROWS_PAD, D_MODEL)[:, :N_IN_SHARD].reshape(N_IN, D_MODEL))
        wo = wout_all.reshape(D_MODEL, D_MODEL)
        return dict(wt_a=wt_a, wt_r=wt_r, wt_g=wt_g, wo=wo,
                    wo_a=wo[0:A_WIDTH], wo_r=wo[A_WIDTH:A_WIDTH + R_WIDTH], wo_g=wo[A_WIDTH + R_WIDTH:])

    layers = []
    for l in range(DEPTH):
        layers.append(dict(
            sinks_t=jnp.broadcast_to(sinks[l][:, None, None], (A_HEADS, 8, LANES)),
            rcw=rcw_full[l], rcb=r_conv_b[l].reshape(1, R_WIDTH), wa=r_wa[l], ba=r_ba[l].reshape(1, R_WIDTH),
            wx=r_wx[l], bx=r_bx[l].reshape(1, R_WIDTH), lam=r_lam[l].reshape(1, R_WIDTH),
            gcw=gcw_full[l], zero_b=jnp.zeros((1, 3 * G_WIDTH), F32),
            a_log=_lane_row(g_a_log[l], G_HEADS), dt=_lane_row(g_dt_bias[l], G_HEADS),
            norm_w=g_norm_w[l].reshape(1, G_HEAD_DIM), ln_g=ln_g[l].reshape(1, D_MODEL), ln_b=ln_b[l].reshape(1, D_MODEL)))

    saved = []
    xin = xin_lo = x0
    layers[0].update(big_weights(win0_all, wout0_all))
    for l, p in enumerate(layers):
        if l + 1 < DEPTH:
            proj_a, (wout_next,) = _matmul([xin_lo], [p["wt_a"]], name=f"proj_a{l}", tm=1024, tn=1280, b_t=True,
                                           comm=_GatherSend([wout_pieces[l + 1]]))
            proj_r, (win_next_0,) = _matmul([xin_lo], [p["wt_r"]], name=f"proj_r{l}", tm=1024, tn=1024, b_t=True,
                                            comm=_GatherSend([win_pieces[l + 1, 0:1]]))
            proj_g, (win_next_1,) = _matmul([xin_lo], [p["wt_g"]], name=f"proj_g{l}", tm=1024, tn=1152, b_t=True,
                                            comm=_GatherSend([win_pieces[l + 1, 1:2]]))
            forward_next = _GatherForward([win_next_0, win_next_1, wout_next])
        else:
            forward_next = None
            proj_a = _matmul([xin_lo], [p["wt_a"]], name=f"proj_a{l}", tm=1024, tn=1280, b_t=True)
            proj_r = _matmul([xin_lo], [p["wt_r"]], name=f"proj_r{l}", tm=1024, tn=1024, b_t=True)
            proj_g = _matmul([xin_lo], [p["wt_g"]], name=f"proj_g{l}", tm=1024, tn=1152, b_t=True)
        ya = _attn_fwd(proj_a, cos, sin, p["sinks_t"], name=f"attn_fwd{l}")
        xr = _conv_fwd(proj_r, R_WIDTH, p["rcw"], p["rcb"], name=f"rconv_fwd{l}")
        h, yr = _rg_fwd(xr, proj_r, p["wa"], p["ba"], p["wx"], p["bx"], p["lam"], name=f"rglru_fwd{l}")
        conv = _conv_fwd(proj_g, 3 * G_WIDTH, p["gcw"], p["zero_b"], name=f"gconv_fwd{l}")
        *chunk_vals, inv = _gdn_chunk_fwd(conv, proj_g, p["a_log"], p["dt"], name=f"gdn_chunk_fwd{l}")
        yg, states = _gdn_scan_fwd(chunk_vals, proj_g, p["norm_w"], name=f"gdn_scan_fwd{l}")
        if forward_next is None:
            z, xout, xout_lo = _outproj_ln(ya, yr, yg, p["wo"], xin, p["ln_g"], p["ln_b"], name=f"outproj_ln{l}")
        else:
            (z, xout, xout_lo), (win_0, win_1, wout_all) = _outproj_ln(
                ya, yr, yg, p["wo"], xin, p["ln_g"], p["ln_b"], name=f"outproj_ln{l}", comm=forward_next)
            layers[l + 1].update(big_weights(jnp.concatenate([win_0, win_1], axis=1), wout_all))
        saved.append(dict(xin_lo=xin_lo, proj_a=proj_a, proj_r=proj_r, proj_g=proj_g, ya=ya, yr=yr, yg=yg, xr=xr, h=h,
                          conv=conv, chunk_vals=chunk_vals, inv=inv, states=states, z=z))
        xin, xin_lo = xout, xout_lo

    grads = [None] * DEPTH
    dxn = None
    loss_local = None
    for l in reversed(range(DEPTH)):
        p, sv = layers[l], saved[l]
        if dxn is None:
            dz, dz_lo, dln_g, dln_b, loss_local = _ln_bwd(sv["z"], p["ln_g"], name=f"ln_bwd{l}", xn=xin, target=target)
        else:
            dz, dz_lo, dln_g, dln_b = _ln_bwd(sv["z"], p["ln_g"], name=f"ln_bwd{l}", dxn=dxn)
        dya = _matmul([dz_lo], [p["wo_a"]], name=f"dya{l}", tm=1024, tn=512, b_t=True)
        dyr = _matmul([dz_lo], [p["wo_r"]], name=f"dyr{l}", tm=1024, tn=1024, b_t=True)
        dyg = _matmul([dz_lo], [p["wo_g"]], name=f"dyg{l}", tm=1024, tn=512, b_t=True)
        dwo = jnp.concatenate([
            _matmul_tn(sv["ya"], dz_lo, name=f"dwo_a{l}", tm=512, tn=1024, tk=1024),
            _matmul_tn(sv["yr"], dz_lo, name=f"dwo_r{l}", tm=1024, tn=1024, tk=1024),
            _matmul_tn(sv["yg"], dz_lo, name=f"dwo_g{l}", tm=512, tn=1024, tk=1024)], axis=0)

        dproj_a, dsinks_t = _attn_bwd(sv["proj_a"], cos, sin, p["sinks_t"], dya, name=f"attn_bwd{l}")

        dxr, drz, dwa, dba, dwx, dbx, dlam = _rg_bwd(sv["xr"], sv["proj_r"], sv["h"], dyr, p["wa"], p["ba"], p["wx"],
                                                     p["bx"], p["lam"], name=f"rglru_bwd{l}")
        dproj_r, drcw, drcb = _conv_bwd(dxr, sv["proj_r"], R_WIDTH, p["rcw"], [drz], name=f"rconv_bwd{l}")

        scan_out = _gdn_scan_bwd(sv["chunk_vals"], sv["states"], sv["proj_g"], p["norm_w"], dyg, name=f"gdn_scan_bwd{l}")
        dgz, dnorm_w = scan_out[6], scan_out[7]
        dconv, dbg, dal, ddt = _gdn_chunk_bwd(sv["conv"], sv["proj_g"], p["a_log"], p["dt"], sv["inv"], scan_out[:6],
                                              name=f"gdn_chunk_bwd{l}")
        dproj_g, dgcw, _ = _conv_bwd(dconv, sv["proj_g"], 3 * G_WIDTH, p["gcw"], [dgz, dbg], name=f"gconv_bwd{l}")

        grads[l] = dict(
            sinks=dsinks_t[:, :, 0].sum(axis=1), r_conv_w=drcw.reshape(CONV_WIDTH, R_WIDTH),
            r_conv_b=drcb.reshape(R_WIDTH), r_wa=dwa, r_ba=dba.reshape(R_WIDTH), r_wx=dwx, r_bx=dbx.reshape(R_WIDTH),
            r_lam=dlam.reshape(R_WIDTH), g_conv_w=dgcw.reshape(CONV_WIDTH, 3 * G_WIDTH),
            g_a_log=dal[0, G_HEADS:2 * G_HEADS], g_dt_bias=ddt[0, G_HEADS:2 * G_HEADS],
            g_norm_w=dnorm_w.reshape(G_HEAD_DIM), ln_g=dln_g.reshape(D_MODEL), ln_b=dln_b.reshape(D_MODEL))

        if l > 0:
            dwin_a = _matmul_tn(dproj_a, sv["xin_lo"], name=f"dwin_a{l}", tm=640, tn=1024, tk=1024)
            dwin_r = _matmul_tn(dproj_r, sv["xin_lo"], name=f"dwin_r{l}", tm=1024, tn=1024, tk=1024)
        else:
            packed_small = _pack([jnp.stack([grads[k][nm] for k in range(DEPTH)]) for nm in SMALL])
            dwin_a, (sent_small,) = _matmul_tn(
                dproj_a, sv["xin_lo"], name=f"dwin_a{l}", tm=640, tn=1024, tk=1024,
                comm=_GatherSend([packed_small.reshape(4, packed_small.shape[0] // 4, LANES)]))
            dwin_r, (all_small,) = _matmul_tn(dproj_r, sv["xin_lo"], name=f"dwin_r{l}", tm=1024, tn=1024, tk=1024,
                                              comm=_GatherForward([sent_small]))
        dwin = _ungroup_grads(dwin_a, dwin_r,
                              _matmul_tn(dproj_g, sv["xin_lo"], name=f"dwin_g{l}", tm=1152, tn=1024, tk=1024))

        dwin_blocks = _owner_blocks(dwin)[:, :, None].astype(MXU_DTYPE)
        dwout_blocks = jnp.transpose(dwo.reshape(4, 2, OUT_SHARD, D_MODEL), (1, 0, 2, 3))[:, :, None].astype(MXU_DTYPE)
        got_win, got_wout = _swap_cores(
            [dwin_blocks.reshape(2, 8, N_ROWS_PAD // 2, D_MODEL), dwout_blocks.reshape(2, 4, OUT_SHARD, D_MODEL)],
            f"swap_core_grads{l}")
        chip_win = _add_pair(dwin_blocks, got_win.reshape(dwin_blocks.shape[1:]), core, name=f"add_core_grads_w_in{l}",
                             rows=352).reshape(4, N_ROWS_PAD, D_MODEL)
        chip_wout = _add_pair(dwout_blocks, got_wout.reshape(dwout_blocks.shape[1:]), core, name=f"add_core_grads_w_out{l}",
                              rows=256).reshape(4, OUT_SHARD, D_MODEL)
        dxn, (win_parts, wout_parts) = _matmul(
            [dproj_a, dproj_r, dproj_g], [p["wt_a"], p["wt_r"], p["wt_g"]], name=f"dx{l}", tm=512, tn=1024, add=dz,
            add_scale=DEEPNORM_ALPHA, comm=_ChipExchange([chip_win, chip_wout]))
        grads[l].update(chip_win=chip_win, chip_wout=chip_wout, win_parts=win_parts, wout_parts=wout_parts)
    grad_x = dxn.reshape(x.shape)
    loss = lax.psum(loss_local[0, 0], ("x", "y", "c"))

    def stacked(name):
        return jnp.stack([grads[l][name] for l in range(DEPTH)])

    def per_layer(name):
        return [grads[l][name] for l in range(DEPTH)]

    w_in_t = [_unshard_rows(t) for t in _sum_adamw(per_layer("chip_win"), chip, per_layer("win_parts"), _shard_rows(w_in),
                                                   _shard_rows(m_w_in), _shard_rows(v_w_in), name="adamw_w_in", rows=176)]
    g_w_in, d_w_in, nm_w_in, nv_w_in = w_in_t
    g_w_out, d_w_out, nm_w_out, nv_w_out = _sum_adamw(per_layer("chip_wout"), chip, per_layer("wout_parts"), w_out,
                                                      m_w_out, v_w_out, name="adamw_w_out", rows=128)

    small = list(SMALL)
    full_shapes = [stacked(nm).shape for nm in small]
    all_small = all_small.reshape(N_DEV, packed_small.shape[0], LANES)
    g_small = dict(zip(small, _unpack(_sum_slots(all_small, name="sum_small_grads"), full_shapes)))
    g_small["r_conv_w"] = lax.dynamic_slice_in_dim(g_small["r_conv_w"], me * (R_WIDTH // N_DEV), R_WIDTH // N_DEV, axis=2)
    g_small["g_conv_w"] = lax.dynamic_slice_in_dim(g_small["g_conv_w"], me * (3 * G_WIDTH // N_DEV), 3 * G_WIDTH // N_DEV, axis=2)
    given = dict(sinks=(sinks, m_sinks, v_sinks), r_conv_w=(r_conv_w, m_r_conv_w, v_r_conv_w),
                 r_conv_b=(r_conv_b, m_r_conv_b, v_r_conv_b), r_wa=(r_wa, m_r_wa, v_r_wa), r_ba=(r_ba, m_r_ba, v_r_ba),
                 r_wx=(r_wx, m_r_wx, v_r_wx), r_bx=(r_bx, m_r_bx, v_r_bx), r_lam=(r_lam, m_r_lam, v_r_lam),
                 g_conv_w=(g_conv_w, m_g_conv_w, v_g_conv_w), g_a_log=(g_a_log, m_g_a_log, v_g_a_log),
                 g_dt_bias=(g_dt_bias, m_g_dt_bias, v_g_dt_bias), g_norm_w=(g_norm_w, m_g_norm_w, v_g_norm_w),
                 ln_g=(ln_g, m_ln_g, v_ln_g), ln_b=(ln_b, m_ln_b, v_ln_b))
    shard_shapes = [given[nm][0].shape for nm in small]
    packed = [_pack([given[nm][k] for nm in small]) for k in range(3)]
    d_p, nm_p, nv_p = _adamw_packed(packed[0], _pack([g_small[nm] for nm in small]), packed[1], packed[2], name="adamw_small")
    d_small = dict(zip(small, _unpack(d_p, shard_shapes)))
    nm_small = dict(zip(small, _unpack(nm_p, shard_shapes)))
    nv_small = dict(zip(small, _unpack(nv_p, shard_shapes)))

    order = ["w_in"] + small[:12] + ["w_out"] + small[12:]

    def leaf(big_in, big_out, table):
        return [big_in if nm == "w_in" else big_out if nm == "w_out" else table[nm] for nm in order]

    return (loss, grad_x, *leaf(g_w_in, g_w_out, g_small), *leaf(d_w_in, d_w_out, d_small),
            *leaf(nm_w_in, nm_w_out, nm_small), *leaf(nv_w_in, nv_w_out, nv_small))
```

```python
import functools
import math

import jax
import jax.numpy as jnp
from jax import lax
from jax.experimental import pallas as pl
from jax.experimental.pallas import tpu as pltpu

F32 = jnp.float32
MXU_DTYPE = jnp.bfloat16
HIGHEST = lax.Precision.HIGHEST
MESH_ID = pl.DeviceIdType.MESH

N_DEV = 8
DEPTH = 2
D_MODEL = 2048
A_HEADS, A_KV_HEADS, A_HEAD_DIM = 8, 2, 64
A_WIDTH, A_KV_WIDTH = 512, 128
A_BLOCK = 128
ROPE_THETA = 10000.0
R_WIDTH, R_BLOCKS, R_BLOCK_DIM = 1024, 8, 128
R_C = 8.0
CONV_WIDTH = 4
G_HEADS, G_HEAD_DIM, G_WIDTH, G_CHUNK = 4, 128, 512, 64
N_IN = 5384
N_IN_SHARD = N_IN // N_DEV
N_ROWS_PAD = 704
OUT_SHARD = D_MODEL // N_DEV
WA, WR, WG = 1280, 2048, 2304
G_PAD = WG - (3 * G_WIDTH + G_WIDTH + 2 * G_HEADS)
DEEPNORM_ALPHA = (2 * DEPTH) ** 0.25
LN_EPS = 1e-5
RMS_EPS = 1e-6
ADAM_LR, ADAM_B1, ADAM_B2, ADAM_EPS, ADAM_WD, ADAM_STEP = 0.001, 0.9, 0.999, 1e-08, 0.01, 10
NEG = -1e30
VMEM_LIMIT = 56 * 1024 * 1024
LANES = 128


def _call(body, **kw):
    return pl.pallas_call(body, **kw)


def _params(*sem):
    return pltpu.CompilerParams(dimension_semantics=sem, vmem_limit_bytes=VMEM_LIMIT)


def _t(x):
    return jnp.swapaxes(x, -1, -2)


def _raw_dot(a, b, ca, cb, precision=None):
    batch = tuple(range(a.ndim - 2))
    if precision is None:
        a, b = a.astype(MXU_DTYPE), b.astype(MXU_DTYPE)
    return lax.dot_general(a, b, (((ca,), (cb,)), (batch, batch)), precision=precision,
                           preferred_element_type=F32)


def _nn(a, b, precision=None):
    return _raw_dot(a, b, a.ndim - 1, b.ndim - 2, precision)


def _nt(a, b, precision=None):
    return _raw_dot(a, b, a.ndim - 1, b.ndim - 1, precision)


@jax.custom_vjp
def mm_nn(a, b):
    return _nn(a, b)


def _mm_nn_fwd(a, b):
    return _nn(a, b), (a, b)


def _mm_nn_bwd(res, g):
    a, b = res
    return _nt(g, b), _nn(_t(a), g)


mm_nn.defvjp(_mm_nn_fwd, _mm_nn_bwd)


@jax.custom_vjp
def mm_nt(a, b):
    return _nt(a, b)


def _mm_nt_fwd(a, b):
    return _nt(a, b), (a, b)


def _mm_nt_bwd(res, g):
    a, b = res
    return _nn(g, b), _nn(_t(g), a)


mm_nt.defvjp(_mm_nt_fwd, _mm_nt_bwd)


def _split(x):
    hi = x.astype(MXU_DTYPE)
    return hi, (x - hi.astype(F32)).astype(MXU_DTYPE)


def _hmm(a, b, nt=False):
    dot = _nt if nt else _nn
    return dot(a[0], b[0]) + (dot(a[0], b[1]) + dot(a[1], b[0]))


def _silu(x):
    return x * jax.nn.sigmoid(x)


def _softplus(x):
    return jnp.maximum(x, 0.0) + jnp.log1p(jnp.exp(-jnp.abs(x)))


def _one_minus_sq(log_a, a):
    x = 2.0 * log_a
    return jnp.where(x > -0.01, -x * (1.0 + 0.5 * x), 1.0 - a * a)


def _iota(shape, dim):
    return lax.broadcasted_iota(jnp.int32, shape, dim)


def _inv_unit_lower(m):
    shape = m.shape
    row, col = _iota(shape, 1), _iota(shape, 2)
    eye = (row == col).astype(F32)

    def blockdiag(size):
        return (row // size) == (col // size)

    x = -jnp.where(blockdiag(8), m, 0.0)
    xs = _split(x)
    x2s = _split(_hmm(xs, xs))
    x4s = _split(_hmm(x2s, x2s))
    inv = eye + x
    inv = inv + _hmm(_split(inv), x2s)
    inv = inv + _hmm(_split(inv), x4s)
    for size in (8, 16, 32):
        below = jnp.where(blockdiag(2 * size) & jnp.logical_not(blockdiag(size)), m, 0.0)
        invs = _split(inv)
        inv = inv - _hmm(_split(_hmm(invs, _split(below))), invs)
    return inv


@jax.custom_vjp
def _solve2(m, inv, r1, r2):
    invs = _split(inv)
    return _hmm(invs, _split(r1)), _hmm(invs, _split(r2))


def _solve2_fwd(m, inv, r1, r2):
    x1, x2 = _solve2(m, inv, r1, r2)
    return (x1, x2), (inv, x1, x2)


def _solve2_bwd(res, g):
    inv, x1, x2 = res
    inv_ts = _split(_t(inv))
    d1, d2 = _hmm(inv_ts, _split(g[0])), _hmm(inv_ts, _split(g[1]))
    dm = -(_hmm(_split(d1), _split(x1), nt=True) + _hmm(_split(d2), _split(x2), nt=True))
    return dm, jnp.zeros_like(inv), d1, d2


_solve2.defvjp(_solve2_fwd, _solve2_bwd)


def _swap_halves(x):
    n = x.shape[-1]
    lane = _iota(x.shape, x.ndim - 1)
    return jnp.where((lane & 63) < 32, pltpu.roll(x, n - 32, x.ndim - 1), pltpu.roll(x, 32, x.ndim - 1))


def _rope(x, cos, sin):
    reps = x.shape[-1] // LANES
    if reps > 1:
        cos, sin = jnp.tile(cos, (1, reps)), jnp.tile(sin, (1, reps))
    return x * cos + _swap_halves(x) * sin


def _rope_t(d, cos, sin):
    reps = d.shape[-1] // LANES
    if reps > 1:
        cos, sin = jnp.tile(cos, (1, reps)), jnp.tile(sin, (1, reps))
    return d * cos + _swap_halves(d * sin)


def _swap64(x):
    return pltpu.roll(x, 64, x.ndim - 1)


def _mesh_pos():
    return lax.axis_index("x"), lax.axis_index("y"), lax.axis_index("c")


def _all_gather(arrays, name):
    n = len(arrays)
    npieces = [a.shape[0] for a in arrays]
    pmax = max(npieces)

    def body(*refs):
        ins, outs = refs[:n], refs[n:2 * n]
        send_sems, recv_sems, local_sem = refs[2 * n:]
        x, y, c = _mesh_pos()
        me, sibling = (x, y, c), (x, y, 1 - c)
        chips = [(1 - x, y), (x, 1 - y), (1 - x, 1 - y)]

        def slot(a, pos, p):
            return outs[a].at[4 * pos[0] + 2 * pos[1] + pos[2], p]

        def copy(a, p, k, block, to, own=False):
            return pltpu.make_async_remote_copy(
                src_ref=ins[a].at[p] if own else slot(a, block, p), dst_ref=slot(a, block, p),
                send_sem=send_sems.at[a, p, k], recv_sem=recv_sems.at[a, p, k], device_id=to, device_id_type=MESH_ID)

        pieces = [(a, p) for p in range(pmax) for a in range(n) if p < npieces[a]]
        mine = [pltpu.make_async_copy(ins[a].at[p], slot(a, me, p), local_sem.at[a, p]) for a, p in pieces]
        for cp in mine:
            cp.start()
        first = []
        for a, p in pieces:
            first += [copy(a, p, 1 + j, me, (*chip, c), own=True) for j, chip in enumerate(chips)]
            first.append(copy(a, p, 0, me, sibling, own=True))
        for cp in first:
            cp.start()
        passed = []
        for a, p in pieces:
            for j, chip in enumerate(chips):
                copy(a, p, 1 + j, (*chip, c), me).wait_recv()
                cp = copy(a, p, 4 + j, (*chip, c), sibling)
                cp.start()
                passed.append(cp)
        for a, p in pieces:
            copy(a, p, 0, sibling, me).wait_recv()
            for j, chip in enumerate(chips):
                copy(a, p, 4 + j, (*chip, 1 - c), me).wait_recv()
        for cp in first + passed:
            cp.wait_send()
        for cp in mine:
            cp.wait()

    any_spec = pl.BlockSpec(memory_space=pl.ANY)
    return _call(
        body, name=name,
        out_shape=[jax.ShapeDtypeStruct((N_DEV,) + a.shape, a.dtype) for a in arrays],
        in_specs=[any_spec] * n, out_specs=[any_spec] * n,
        scratch_shapes=[pltpu.SemaphoreType.DMA((n, pmax, 7)), pltpu.SemaphoreType.DMA((n, pmax, 7)),
                        pltpu.SemaphoreType.DMA((n, pmax))],
    )(*arrays)


def _swap_cores(arrays, name):
    n = len(arrays)
    pmax = max(a.shape[1] for a in arrays)

    def body(*refs):
        ins, got = refs[:n], refs[n:2 * n]
        send_sems, recv_sems = refs[2 * n:]
        x, y, c = _mesh_pos()
        copies = [pltpu.make_async_remote_copy(
            src_ref=ins[a].at[1 - c, p], dst_ref=got[a].at[p], send_sem=send_sems.at[a, p], recv_sem=recv_sems.at[a, p],
            device_id=(x, y, 1 - c), device_id_type=MESH_ID) for a in range(n) for p in range(arrays[a].shape[1])]
        for cp in copies:
            cp.start()
        for cp in copies:
            cp.wait()

    any_spec = pl.BlockSpec(memory_space=pl.ANY)
    return _call(
        body, name=name, out_shape=[jax.ShapeDtypeStruct(a.shape[1:], a.dtype) for a in arrays],
        in_specs=[any_spec] * n, out_specs=[any_spec] * n,
        scratch_shapes=[pltpu.SemaphoreType.DMA((n, pmax)), pltpu.SemaphoreType.DMA((n, pmax))],
    )(*arrays)


class _ChipExchange:
    aliases = {}

    def __init__(self, arrays):
        self.arrays = list(arrays)
        n = len(self.arrays)
        self.out_shape = [jax.ShapeDtypeStruct((3,) + a.shape[1:], a.dtype) for a in self.arrays]
        self.scratch = [pltpu.SemaphoreType.DMA((n, 3)), pltpu.SemaphoreType.DMA((n, 3))]

    def _copies(self, ins, outs, send_sems, recv_sems):
        x, y, c = _mesh_pos()
        copies = []
        for a in range(len(self.arrays)):
            for k in range(1, 4):
                px, py = x ^ (k >> 1), y ^ (k & 1)
                copies.append(pltpu.make_async_remote_copy(
                    src_ref=ins[a].at[2 * px + py], dst_ref=outs[a].at[k - 1], send_sem=send_sems.at[a, k - 1],
                    recv_sem=recv_sems.at[a, k - 1], device_id=(px, py, c), device_id_type=MESH_ID))
        return copies

    def start(self, ins, outs, send_sems, recv_sems):
        for cp in self._copies(ins, outs, send_sems, recv_sems):
            cp.start()

    def finish(self, ins, outs, send_sems, recv_sems):
        copies = self._copies(ins, outs, send_sems, recv_sems)
        for cp in copies:
            cp.wait_recv()
        for cp in copies:
            cp.wait_send()


def _slot(pos):
    return 4 * pos[0] + 2 * pos[1] + pos[2]


class _GatherSend:
    aliases = {}

    def __init__(self, arrays):
        self.arrays = list(arrays)
        n, pmax = len(self.arrays), max(a.shape[0] for a in self.arrays)
        self.out_shape = [jax.ShapeDtypeStruct((N_DEV,) + a.shape, a.dtype) for a in self.arrays]
        self.scratch = [pltpu.SemaphoreType.DMA((n, pmax, 4)), pltpu.SemaphoreType.DMA((n, pmax, 4)),
                        pltpu.SemaphoreType.DMA((n, pmax))]

    def _copies(self, ins, outs, send_sems, recv_sems, local_sems):
        x, y, c = _mesh_pos()
        peers = [(x, y, 1 - c), (1 - x, y, c), (x, 1 - y, c), (1 - x, 1 - y, c)]
        local, remote = [], []
        for a, arr in enumerate(self.arrays):
            for p in range(arr.shape[0]):
                local.append(pltpu.make_async_copy(ins[a].at[p], outs[a].at[_slot((x, y, c)), p], local_sems.at[a, p]))
                for k, peer in enumerate(peers):
                    remote.append(pltpu.make_async_remote_copy(
                        src_ref=ins[a].at[p], dst_ref=outs[a].at[_slot((x, y, c)), p], send_sem=send_sems.at[a, p, k],
                        recv_sem=recv_sems.at[a, p, k], device_id=peer, device_id_type=MESH_ID))
        return local, remote

    def start(self, *refs):
        local, remote = self._copies(*refs)
        for cp in local + remote:
            cp.start()

    def finish(self, *refs):
        local, remote = self._copies(*refs)
        for cp in remote:
            cp.wait_recv()
        for cp in remote:
            cp.wait_send()
        for cp in local:
            cp.wait()


class _GatherForward:
    def __init__(self, gathered):
        self.arrays = list(gathered)
        n, pmax = len(self.arrays), max(a.shape[1] for a in self.arrays)
        self.out_shape = [jax.ShapeDtypeStruct(a.shape, a.dtype) for a in self.arrays]
        self.aliases = {k: k for k in range(n)}
        self.scratch = [pltpu.SemaphoreType.DMA((n, pmax, 3)), pltpu.SemaphoreType.DMA((n, pmax, 3))]

    def _copies(self, ins, outs, send_sems, recv_sems):
        x, y, c = _mesh_pos()
        copies = []
        for a, arr in enumerate(self.arrays):
            for p in range(arr.shape[1]):
                for j, chip in enumerate([(1 - x, y), (x, 1 - y), (1 - x, 1 - y)]):
                    copies.append(pltpu.make_async_remote_copy(
                        src_ref=ins[a].at[_slot((*chip, c)), p], dst_ref=outs[a].at[_slot((*chip, c)), p],
                        send_sem=send_sems.at[a, p, j], recv_sem=recv_sems.at[a, p, j], device_id=(x, y, 1 - c),
                        device_id_type=MESH_ID))
        return copies

    def start(self, *refs):
        for cp in self._copies(*refs):
            cp.start()

    def finish(self, *refs):
        copies = self._copies(*refs)
        for cp in copies:
            cp.wait_recv()
        for cp in copies:
            cp.wait_send()


def _index_operand(i):
    return jnp.reshape(i, (1,)).astype(jnp.int32)


def _add_pair(pair, other, core, *, name, rows):
    _, n_slots, n_layers, n_rows, n_cols = pair.shape
    rows = min(rows, n_rows)

    def body(c_ref, a_ref, b_ref, o_ref):
        o_ref[...] = (a_ref[0].astype(F32) + b_ref[...].astype(F32)).astype(o_ref.dtype)

    blk = pl.BlockSpec((1, 1, rows, n_cols), lambda s, l, i, c: (s, l, i, 0))
    grid_spec = pltpu.PrefetchScalarGridSpec(
        num_scalar_prefetch=1, grid=(n_slots, n_layers, n_rows // rows),
        in_specs=[pl.BlockSpec((1, 1, 1, rows, n_cols), lambda s, l, i, c: (c[0], s, l, i, 0)), blk], out_specs=blk)
    return _call(body, name=name, grid_spec=grid_spec, out_shape=jax.ShapeDtypeStruct(other.shape, pair.dtype),
                 compiler_params=_params("parallel", "parallel", "parallel"))(_index_operand(core), pair, other)


def _host(body, comm, *, name, grid, in_specs, out_specs, out_shape, args, semantics):
    if comm is None:
        return _call(body, name=name, grid=grid, in_specs=in_specs, out_specs=out_specs, out_shape=out_shape,
                     compiler_params=_params(*semantics))(*args)
    n_in, n_out, n_comm = len(in_specs), len(out_specs), len(comm.arrays)

    def hosted(*refs):
        ins, outs = refs[:n_in], refs[n_in + n_comm:n_in + n_comm + n_out]
        comm_refs = (refs[n_in:n_in + n_comm], refs[n_in + n_comm + n_out:n_in + 2 * n_comm + n_out],
                     *refs[n_in + 2 * n_comm + n_out:])
        ids = [pl.program_id(d) for d in range(len(grid))]
        first, last = ids[0] == 0, ids[0] == grid[0] - 1
        for d in range(1, len(grid)):
            first, last = first & (ids[d] == 0), last & (ids[d] == grid[d] - 1)

        @pl.when(first)
        def _():
            comm.start(*comm_refs)

        body(*ins, *outs)

        @pl.when(last)
        def _():
            comm.finish(*comm_refs)

    any_spec = pl.BlockSpec(memory_space=pl.ANY)
    outs = _call(hosted, name=name, grid=grid, in_specs=list(in_specs) + [any_spec] * n_comm,
                 out_specs=list(out_specs) + [any_spec] * n_comm, out_shape=list(out_shape) + comm.out_shape,
                 input_output_aliases={n_in + k: n_out + v for k, v in comm.aliases.items()},
                 scratch_shapes=comm.scratch, compiler_params=_params(*(("arbitrary",) * len(grid))))(*args, *comm.arrays)
    return outs[:n_out], outs[n_out:]


def _matmul(a_list, b_list, *, name, tm, tn, b_t=False, out_dtype=F32, add=None, add_scale=1.0, comm=None):
    n = len(a_list)
    m_rows, n_cols = a_list[0].shape[0], b_list[0].shape[0 if b_t else 1]
    tm, tn = min(tm, m_rows), min(tn, n_cols)

    def body(*refs):
        a_refs, b_refs = refs[:n], refs[n:2 * n]
        o_ref = refs[-1]
        acc = None
        for a_ref, b_ref in zip(a_refs, b_refs):
            part = lax.dot_general(a_ref[...].astype(MXU_DTYPE), b_ref[...].astype(MXU_DTYPE),
                                   (((1,), (1 if b_t else 0,)), ((), ())), preferred_element_type=F32)
            acc = part if acc is None else acc + part
        if add is not None:
            acc = acc + add_scale * refs[2 * n][...]
        o_ref[...] = acc.astype(o_ref.dtype)

    in_specs = [pl.BlockSpec((tm, a.shape[1]), lambda i, j: (i, 0)) for a in a_list]
    if b_t:
        in_specs += [pl.BlockSpec((tn, b.shape[1]), lambda i, j: (j, 0)) for b in b_list]
    else:
        in_specs += [pl.BlockSpec((b.shape[0], tn), lambda i, j: (0, j)) for b in b_list]
    args = list(a_list) + list(b_list)
    if add is not None:
        in_specs.append(pl.BlockSpec((tm, tn), lambda i, j: (i, j)))
        args.append(add)
    res = _host(body, comm, name=name, grid=(m_rows // tm, n_cols // tn), in_specs=in_specs,
                out_specs=[pl.BlockSpec((tm, tn), lambda i, j: (i, j))],
                out_shape=[jax.ShapeDtypeStruct((m_rows, n_cols), out_dtype)], args=args,
                semantics=("parallel", "arbitrary"))
    return res[0] if comm is None else (res[0][0], res[1])


def _matmul_tn(a, b, *, name, tm, tn, tk, comm=None):
    k_rows, m_rows = a.shape
    n_cols = b.shape[1]
    tm, tn, tk = min(tm, m_rows), min(tn, n_cols), min(tk, k_rows)
    nk = k_rows // tk

    def body(a_ref, b_ref, o_ref):
        @pl.when(pl.program_id(2) == 0)
        def _():
            o_ref[...] = jnp.zeros_like(o_ref)

        o_ref[...] += lax.dot_general(a_ref[...].astype(MXU_DTYPE), b_ref[...].astype(MXU_DTYPE),
                                      (((0,), (0,)), ((), ())), preferred_element_type=F32)

    res = _host(body, comm, name=name, grid=(m_rows // tm, n_cols // tn, nk),
                in_specs=[pl.BlockSpec((tk, tm), lambda i, j, k: (k, i)), pl.BlockSpec((tk, tn), lambda i, j, k: (k, j))],
                out_specs=[pl.BlockSpec((tm, tn), lambda i, j, k: (i, j))],
                out_shape=[jax.ShapeDtypeStruct((m_rows, n_cols), F32)], args=[a, b],
                semantics=("parallel", "parallel", "arbitrary"))
    return res[0] if comm is None else (res[0][0], res[1])


def _outproj_ln(ya, yr, yg, w_out, x, ln_g, ln_b, *, name, comm=None):
    s_len = x.shape[0]
    tm = min(256, s_len)

    def body(ya_ref, yr_ref, yg_ref, w_ref, x_ref, g_ref, b_ref, z_ref, o_ref, lo_ref):
        acc = jnp.dot(ya_ref[...], w_ref[0:A_WIDTH, :], preferred_element_type=F32)
        acc += jnp.dot(yr_ref[...], w_ref[A_WIDTH:A_WIDTH + R_WIDTH, :], preferred_element_type=F32)
        acc += jnp.dot(yg_ref[...], w_ref[A_WIDTH + R_WIDTH:, :], preferred_element_type=F32)
        z = DEEPNORM_ALPHA * x_ref[...] + acc
        z_ref[...] = z
        mu = jnp.mean(z, axis=-1, keepdims=True)
        zc = z - mu
        var = jnp.mean(zc * zc, axis=-1, keepdims=True)
        out = zc * lax.rsqrt(var + LN_EPS) * g_ref[...] + b_ref[...]
        o_ref[...] = out
        lo_ref[...] = out.astype(lo_ref.dtype)

    def rows(width):
        return pl.BlockSpec((tm, width), lambda i: (i, 0))

    def whole(shape):
        return pl.BlockSpec(shape, lambda i: (0, 0))

    return _host(
        body, comm, name=name, grid=(s_len // tm,),
        in_specs=[rows(A_WIDTH), rows(R_WIDTH), rows(G_WIDTH), whole((D_MODEL, D_MODEL)), rows(D_MODEL),
                  whole((1, D_MODEL)), whole((1, D_MODEL))],
        out_specs=[rows(D_MODEL)] * 3,
        out_shape=[jax.ShapeDtypeStruct((s_len, D_MODEL), F32)] * 2 + [jax.ShapeDtypeStruct((s_len, D_MODEL), MXU_DTYPE)],
        args=[ya, yr, yg, w_out, x, ln_g, ln_b], semantics=("parallel",))


def _ln_bwd(z, ln_g, *, name, dxn=None, xn=None, target=None):
    s_len = z.shape[0]
    tm = min(256, s_len)
    top = dxn is None

    def body(*refs):
        if top:
            z_ref, g_ref, xn_ref, t_ref, dz_ref, lo_ref, dg_ref, db_ref, loss_ref = refs
            err = xn_ref[...] - t_ref[...]
            dy = err * (1.0 / D_MODEL)
        else:
            z_ref, g_ref, dy_ref, dz_ref, lo_ref, dg_ref, db_ref = refs
            dy = dy_ref[...]
        first = pl.program_id(0) == 0

        @pl.when(first)
        def _():
            dg_ref[...] = jnp.zeros_like(dg_ref)
            db_ref[...] = jnp.zeros_like(db_ref)
            if top:
                loss_ref[...] = jnp.zeros_like(loss_ref)

        z = z_ref[...]
        mu = jnp.mean(z, axis=-1, keepdims=True)
        zc = z - mu
        rstd = lax.rsqrt(jnp.mean(zc * zc, axis=-1, keepdims=True) + LN_EPS)
        xhat = zc * rstd
        dxh = dy * g_ref[...]
        dz = rstd * (dxh - jnp.mean(dxh, axis=-1, keepdims=True) - xhat * jnp.mean(dxh * xhat, axis=-1, keepdims=True))
        dz_ref[...] = dz
        lo_ref[...] = dz.astype(lo_ref.dtype)
        dg_ref[...] += jnp.sum(dy * xhat, axis=0, keepdims=True)
        db_ref[...] += jnp.sum(dy, axis=0, keepdims=True)
        if top:
            per_row = jnp.sum(err * err, axis=-1, keepdims=True) * (0.5 / D_MODEL)
            loss_ref[...] += jnp.sum(per_row, axis=0, keepdims=True)

    rows = pl.BlockSpec((tm, D_MODEL), lambda i: (i, 0))
    vec = pl.BlockSpec((1, D_MODEL), lambda i: (0, 0))
    in_specs = [rows, vec] + ([rows, rows] if top else [rows])
    args = [z, ln_g] + ([xn, target] if top else [dxn])
    out_specs = [rows, rows, vec, vec]
    out_shape = [jax.ShapeDtypeStruct((s_len, D_MODEL), F32), jax.ShapeDtypeStruct((s_len, D_MODEL), MXU_DTYPE),
                 jax.ShapeDtypeStruct((1, D_MODEL), F32), jax.ShapeDtypeStruct((1, D_MODEL), F32)]
    if top:
        out_specs.append(pl.BlockSpec((1, 1), lambda i: (0, 0)))
        out_shape.append(jax.ShapeDtypeStruct((1, 1), F32))
    return _call(body, name=name, grid=(s_len // tm,), in_specs=in_specs, out_specs=out_specs,
                 out_shape=out_shape, compiler_params=_params("arbitrary"))(*args)


CONV_ROWS = 512
HALO = 8


def _shift_down(x, halo, s):
    if s == 0:
        return x
    ext = jnp.concatenate([halo, x], axis=0)
    return pltpu.roll(ext, s, 0)[HALO:, :]


def _shift_up(x, halo, s):
    if s == 0:
        return x
    ext = jnp.concatenate([x, halo], axis=0)
    return pltpu.roll(ext, ext.shape[0] - s, 0)[:x.shape[0], :]


def _conv_fwd(src, width, w, bias, *, name):
    s_len = src.shape[0]
    rows = min(CONV_ROWS, s_len)
    per = rows // HALO

    def body(x_ref, halo_ref, w_ref, b_ref, o_ref):
        x = x_ref[...]
        halo = jnp.where(pl.program_id(0) == 0, 0.0, halo_ref[...])
        acc = x * w_ref[3:4, :] + b_ref[...]
        for k in range(CONV_WIDTH - 1):
            acc += _shift_down(x, halo, 3 - k) * w_ref[k:k + 1, :]
        o_ref[...] = acc

    return _call(
        body, name=name, grid=(s_len // rows,),
        in_specs=[pl.BlockSpec((rows, width), lambda i: (i, 0)),
                  pl.BlockSpec((HALO, width), lambda i: (jnp.maximum(i * per - 1, 0), 0)),
                  pl.BlockSpec((CONV_WIDTH, width), lambda i: (0, 0)), pl.BlockSpec((1, width), lambda i: (0, 0))],
        out_specs=pl.BlockSpec((rows, width), lambda i: (i, 0)),
        out_shape=jax.ShapeDtypeStruct((s_len, width), F32),
        compiler_params=_params("parallel"),
    )(src, src, w, bias)


def _conv_bwd(dy, src, width, w, passthrough, *, name):
    s_len = src.shape[0]
    rows = min(CONV_ROWS, s_len)
    per = rows // HALO
    nblk = s_len // rows
    extra = [p.shape[1] for p in passthrough]
    total = width + sum(extra)

    def body(*refs):
        dy_ref, dyh_ref, x_ref, xh_ref, w_ref = refs[:5]
        p_refs = refs[5:5 + len(extra)]
        o_ref, dw_ref, db_ref = refs[5 + len(extra):]
        i = pl.program_id(0)

        @pl.when(i == 0)
        def _():
            dw_ref[...] = jnp.zeros_like(dw_ref)
            db_ref[...] = jnp.zeros_like(db_ref)

        dy = dy_ref[...]
        x = x_ref[...]
        dy_halo = jnp.where(i == nblk - 1, 0.0, dyh_ref[...])
        x_halo = jnp.where(i == 0, 0.0, xh_ref[...])
        dx = dy * w_ref[3:4, :]
        dw_ref[3] += jnp.sum(dy * x, axis=0, keepdims=True)
        for k in range(CONV_WIDTH - 1):
            dx += _shift_up(dy, dy_halo, 3 - k) * w_ref[k:k + 1, :]
            dw_ref[k] += jnp.sum(dy * _shift_down(x, x_halo, 3 - k), axis=0, keepdims=True)
        db_ref[...] += jnp.sum(dy, axis=0, keepdims=True)
        o_ref[:, 0:width] = dx.astype(o_ref.dtype)
        off = width
        for p_ref, wd in zip(p_refs, extra):
            o_ref[:, off:off + wd] = p_ref[...].astype(o_ref.dtype)
            off += wd

    in_specs = [pl.BlockSpec((rows, width), lambda i: (i, 0)),
                pl.BlockSpec((HALO, width), lambda i: (jnp.minimum((i + 1) * per, nblk * per - 1), 0)),
                pl.BlockSpec((rows, width), lambda i: (i, 0)),
                pl.BlockSpec((HALO, width), lambda i: (jnp.maximum(i * per - 1, 0), 0)),
                pl.BlockSpec((CONV_WIDTH, width), lambda i: (0, 0))]
    in_specs += [pl.BlockSpec((rows, wd), lambda i: (i, 0)) for wd in extra]
    return _call(
        body, name=name, grid=(nblk,), in_specs=in_specs,
        out_specs=[pl.BlockSpec((rows, total), lambda i: (i, 0)),
                   pl.BlockSpec((CONV_WIDTH, 1, width), lambda i: (0, 0, 0)), pl.BlockSpec((1, width), lambda i: (0, 0))],
        out_shape=[jax.ShapeDtypeStruct((s_len, total), MXU_DTYPE), jax.ShapeDtypeStruct((CONV_WIDTH, 1, width), F32),
                   jax.ShapeDtypeStruct((1, width), F32)],
        compiler_params=_params("arbitrary"),
    )(dy, dy, src, src, w, *passthrough)


def _attn_mask(first):
    i = _iota((A_BLOCK, 2 * A_BLOCK), 0)
    j = _iota((A_BLOCK, 2 * A_BLOCK), 1)
    band = (j > i) & (j <= i + A_BLOCK)
    return band & ((j >= A_BLOCK) | jnp.logical_not(first))


def _attn_group(p, mask, qg, kw, kws, vw, vws, azg, sink0, sink1):
    low = _iota(qg.shape, 1) < A_HEAD_DIM
    first_lane = (_iota((A_BLOCK, LANES), 1) == 0).astype(F32)
    out = None
    for half, sink in ((0, sink0), (1, sink1)):
        kv_head = (2 * p + half) // (A_HEADS // A_KV_HEADS)
        keep = low if half == 0 else jnp.logical_not(low)
        qm = jnp.where(keep, qg, 0.0)
        kk, vv = (kw, vw) if kv_head == half else (kws, vws)
        s = mm_nt(qm, kk) * (A_HEAD_DIM ** -0.5)
        s = jnp.where(mask, s, NEG)
        sk = jnp.sum(jnp.tile(sink, (A_BLOCK // 8, 1)) * first_lane, axis=1, keepdims=True)
        m = lax.stop_gradient(jnp.maximum(jnp.max(s, axis=1, keepdims=True), sk))
        e = jnp.exp(s - m)
        denom = jnp.sum(e, axis=1, keepdims=True) + jnp.exp(sk - m)
        o = mm_nn(e * (1.0 / denom), vv)
        o = jnp.where(keep, o, 0.0)
        out = o if out is None else out + o
    return out * _silu(azg)


def _attn_specs(s_len, rev):
    nb = s_len // A_BLOCK

    def cur(i):
        return nb - 1 - i if rev else i

    def prev(i):
        return jnp.maximum(cur(i) - 1, 0)

    def blk(width, col, which):
        return pl.BlockSpec((A_BLOCK, width), lambda i: (which(i), col))

    return [blk(A_WIDTH, 0, cur), blk(A_WIDTH, 1, cur), blk(LANES, 8, cur), blk(LANES, 9, cur),
            blk(LANES, 8, prev), blk(LANES, 9, prev), blk(LANES, 0, cur), blk(LANES, 0, cur),
            blk(LANES, 0, prev), blk(LANES, 0, prev)], cur


def _attn_fwd(proj_a, cos, sin, sinks_t, *, name):
    s_len = proj_a.shape[0]
    specs, _ = _attn_specs(s_len, False)

    def body(q_ref, az_ref, k_ref, v_ref, kp_ref, vp_ref, c_ref, s_ref, cp_ref, sp_ref, sink_ref, o_ref):
        first = pl.program_id(0) == 0
        mask = _attn_mask(first)
        qr = _rope(q_ref[...], c_ref[...], s_ref[...])
        kw = jnp.concatenate([_rope(kp_ref[...], cp_ref[...], sp_ref[...]), _rope(k_ref[...], c_ref[...], s_ref[...])], 0)
        vw = jnp.concatenate([vp_ref[...], v_ref[...]], 0)
        kws, vws = _swap64(kw), _swap64(vw)
        for p in range(A_WIDTH // LANES):
            cols = slice(p * LANES, (p + 1) * LANES)
            o = _attn_group(p, mask, qr[:, cols], kw, kws, vw, vws, az_ref[:, cols], sink_ref[2 * p], sink_ref[2 * p + 1])
            o_ref[:, cols] = o.astype(o_ref.dtype)

    return _call(
        body, name=name, grid=(s_len // A_BLOCK,),
        in_specs=specs + [pl.BlockSpec((A_HEADS, 8, LANES), lambda i: (0, 0, 0))],
        out_specs=pl.BlockSpec((A_BLOCK, A_WIDTH), lambda i: (i, 0)),
        out_shape=jax.ShapeDtypeStruct((s_len, A_WIDTH), MXU_DTYPE),
        compiler_params=_params("parallel"),
    )(proj_a, proj_a, proj_a, proj_a, proj_a, proj_a, cos, sin, cos, sin, sinks_t)


def _attn_bwd(proj_a, cos, sin, sinks_t, dya, *, name):
    s_len = proj_a.shape[0]
    specs, cur = _attn_specs(s_len, True)

    def body(q_ref, az_ref, k_ref, v_ref, kp_ref, vp_ref, c_ref, s_ref, cp_ref, sp_ref, sink_ref, dy_ref,
             o_ref, dsink_ref, dk_carry, dv_carry):
        i = pl.program_id(0)

        @pl.when(i == 0)
        def _():
            dsink_ref[...] = jnp.zeros_like(dsink_ref)
            dk_carry[...] = jnp.zeros_like(dk_carry)
            dv_carry[...] = jnp.zeros_like(dv_carry)

        first = cur(i) == 0
        mask = _attn_mask(first)
        cos_c, sin_c = c_ref[...], s_ref[...]
        qr = _rope(q_ref[...], cos_c, sin_c)
        kw = jnp.concatenate([_rope(kp_ref[...], cp_ref[...], sp_ref[...]), _rope(k_ref[...], cos_c, sin_c)], 0)
        vw = jnp.concatenate([vp_ref[...], v_ref[...]], 0)
        kws, vws = _swap64(kw), _swap64(vw)
        dkw = jnp.zeros_like(kw)
        dvw = jnp.zeros_like(vw)
        for p in range(A_WIDTH // LANES):
            cols = slice(p * LANES, (p + 1) * LANES)
            _, vjp = jax.vjp(functools.partial(_attn_group, p, mask), qr[:, cols], kw, kws, vw, vws, az_ref[:, cols],
                             sink_ref[2 * p], sink_ref[2 * p + 1])
            dq, dk1, dk2, dv1, dv2, daz, ds0, ds1 = vjp(dy_ref[:, cols])
            dkw += dk1 + _swap64(dk2)
            dvw += dv1 + _swap64(dv2)
            o_ref[:, cols] = _rope_t(dq, cos_c, sin_c).astype(o_ref.dtype)
            o_ref[:, A_WIDTH + p * LANES:A_WIDTH + (p + 1) * LANES] = daz.astype(o_ref.dtype)
            dsink_ref[2 * p] += ds0
            dsink_ref[2 * p + 1] += ds1
        o_ref[:, 2 * A_WIDTH:2 * A_WIDTH + LANES] = _rope_t(dkw[A_BLOCK:, :] + dk_carry[...], cos_c, sin_c).astype(o_ref.dtype)
        o_ref[:, 2 * A_WIDTH + LANES:] = (dvw[A_BLOCK:, :] + dv_carry[...]).astype(o_ref.dtype)
        dk_carry[...] = dkw[:A_BLOCK, :]
        dv_carry[...] = dvw[:A_BLOCK, :]

    return _call(
        body, name=name, grid=(s_len // A_BLOCK,),
        in_specs=specs + [pl.BlockSpec((A_HEADS, 8, LANES), lambda i: (0, 0, 0)),
                          pl.BlockSpec((A_BLOCK, A_WIDTH), lambda i: (cur(i), 0))],
        out_specs=[pl.BlockSpec((A_BLOCK, WA), lambda i: (cur(i), 0)),
                   pl.BlockSpec((A_HEADS, 8, LANES), lambda i: (0, 0, 0))],
        out_shape=[jax.ShapeDtypeStruct((s_len, WA), MXU_DTYPE), jax.ShapeDtypeStruct((A_HEADS, 8, LANES), F32)],
        scratch_shapes=[pltpu.VMEM((A_BLOCK, LANES), F32), pltpu.VMEM((A_BLOCK, LANES), F32)],
        compiler_params=_params("arbitrary"),
    )(proj_a, proj_a, proj_a, proj_a, proj_a, proj_a, cos, sin, cos, sin, sinks_t, dya)


RG_ROWS = 256


def _rg_gates(x, wa, ba, wx, bx, lam):
    r = jax.nn.sigmoid(mm_nn(x, wa) + ba)
    ig = jax.nn.sigmoid(mm_nn(x, wx) + bx)
    log_a = -R_C * r * _softplus(-lam)
    a = jnp.exp(log_a)
    return a, jnp.sqrt(_one_minus_sq(log_a, a)) * (ig * x)


def _rg_param_specs():
    mat = pl.BlockSpec((R_BLOCKS, R_BLOCK_DIM, R_BLOCK_DIM), lambda i: (0, 0, 0))
    vec = pl.BlockSpec((1, R_WIDTH), lambda i: (0, 0))
    return [mat, vec, mat, vec, vec]


def _rg_fwd(xr, proj_r, wa, ba, wx, bx, lam, *, name):
    s_len = xr.shape[0]
    rows = min(RG_ROWS, s_len)

    def body(x_ref, z_ref, wa_ref, ba_ref, wx_ref, bx_ref, lam_ref, h_ref, y_ref, a_buf, u_buf, carry):
        @pl.when(pl.program_id(0) == 0)
        def _():
            carry[...] = jnp.zeros_like(carry)

        for n in range(R_BLOCKS):
            cols = slice(n * R_BLOCK_DIM, (n + 1) * R_BLOCK_DIM)
            a, u = _rg_gates(x_ref[:, cols], wa_ref[n], ba_ref[:, cols], wx_ref[n], bx_ref[:, cols], lam_ref[:, cols])
            a_buf[:, cols] = a
            u_buf[:, cols] = u

        def step(t, h):
            h = a_buf[pl.ds(t, 1), :] * h + u_buf[pl.ds(t, 1), :]
            h_ref[pl.ds(t, 1), :] = h
            return h

        carry[...] = lax.fori_loop(0, rows, step, carry[...], unroll=16)
        y_ref[...] = (h_ref[...] * _silu(z_ref[...])).astype(y_ref.dtype)

    blk = pl.BlockSpec((rows, R_WIDTH), lambda i: (i, 0))
    return _call(
        body, name=name, grid=(s_len // rows,),
        in_specs=[blk, pl.BlockSpec((rows, R_WIDTH), lambda i: (i, 1))] + _rg_param_specs(),
        out_specs=[blk, blk],
        out_shape=[jax.ShapeDtypeStruct((s_len, R_WIDTH), F32), jax.ShapeDtypeStruct((s_len, R_WIDTH), MXU_DTYPE)],
        scratch_shapes=[pltpu.VMEM((rows, R_WIDTH), F32), pltpu.VMEM((rows, R_WIDTH), F32), pltpu.VMEM((1, R_WIDTH), F32)],
        compiler_params=_params("arbitrary"),
    )(xr, proj_r, wa, ba, wx, bx, lam)


def _rg_bwd(xr, proj_r, h, dyr, wa, ba, wx, bx, lam, *, name):
    s_len = xr.shape[0]
    rows = min(RG_ROWS, s_len)
    nblk = s_len // rows
    per = rows // HALO

    def cur(i):
        return nblk - 1 - i

    def body(x_ref, z_ref, h_ref, hh_ref, dy_ref, wa_ref, ba_ref, wx_ref, bx_ref, lam_ref,
             dx_ref, dz_ref, dwa_ref, dba_ref, dwx_ref, dbx_ref, dlam_ref, a_buf, g_buf, carry):
        i = pl.program_id(0)

        @pl.when(i == 0)
        def _():
            carry[...] = jnp.zeros_like(carry)
            for ref in (dwa_ref, dba_ref, dwx_ref, dbx_ref, dlam_ref):
                ref[...] = jnp.zeros_like(ref)

        z = z_ref[...]
        sig = jax.nn.sigmoid(z)
        hval = h_ref[...]
        dy = dy_ref[...]
        dz_ref[...] = dy * hval * (sig * (1.0 + z * (1.0 - sig)))
        g_buf[...] = dy * (z * sig)
        kept = []
        for n in range(R_BLOCKS):
            cols = slice(n * R_BLOCK_DIM, (n + 1) * R_BLOCK_DIM)
            x = x_ref[:, cols]
            r = jax.nn.sigmoid(_nn(x, wa_ref[n]) + ba_ref[:, cols])
            ig = jax.nn.sigmoid(_nn(x, wx_ref[n]) + bx_ref[:, cols])
            sp = _softplus(-lam_ref[:, cols])
            log_a = -R_C * r * sp
            a = jnp.exp(log_a)
            a_buf[:, cols] = a
            kept.append((x, r, ig, sp, a, jnp.sqrt(_one_minus_sq(log_a, a))))

        def step(k, c):
            t = rows - 1 - k
            g = g_buf[pl.ds(t, 1), :] + c
            g_buf[pl.ds(t, 1), :] = g
            return a_buf[pl.ds(t, 1), :] * g

        carry[...] = lax.fori_loop(0, rows, step, carry[...], unroll=16)
        h_halo = jnp.where(cur(i) == 0, 0.0, hh_ref[...])
        dh = g_buf[...]
        da = dh * _shift_down(hval, h_halo, 1)
        for n in range(R_BLOCKS):
            cols = slice(n * R_BLOCK_DIM, (n + 1) * R_BLOCK_DIM)
            x, r, ig, sp, a, s = kept[n]
            du = dh[:, cols]
            dux = du * x
            d_log_a = a * (da[:, cols] - a * (dux * ig) / s)
            d_ga = d_log_a * (-R_C * sp) * (r * (1.0 - r))
            d_gx = dux * s * (ig * (1.0 - ig))
            dx_ref[:, cols] = du * (s * ig) + _nt(d_ga, wa_ref[n]) + _nt(d_gx, wx_ref[n])
            xt = _t(x)
            dwa_ref[n] += _nn(xt, d_ga)
            dwx_ref[n] += _nn(xt, d_gx)
            dba_ref[:, cols] += jnp.sum(d_ga, axis=0, keepdims=True)
            dbx_ref[:, cols] += jnp.sum(d_gx, axis=0, keepdims=True)
            dlam_ref[:, cols] += jnp.sum(d_log_a * r, axis=0, keepdims=True) * (R_C * jax.nn.sigmoid(-lam_ref[:, cols]))

    blk = pl.BlockSpec((rows, R_WIDTH), lambda i: (cur(i), 0))
    mat = pl.BlockSpec((R_BLOCKS, R_BLOCK_DIM, R_BLOCK_DIM), lambda i: (0, 0, 0))
    vec = pl.BlockSpec((1, R_WIDTH), lambda i: (0, 0))
    return _call(
        body, name=name, grid=(nblk,),
        in_specs=[blk, pl.BlockSpec((rows, R_WIDTH), lambda i: (cur(i), 1)), blk,
                  pl.BlockSpec((HALO, R_WIDTH), lambda i: (jnp.maximum(cur(i) * per - 1, 0), 0)), blk] + _rg_param_specs(),
        out_specs=[blk, blk, mat, vec, mat, vec, vec],
        out_shape=[jax.ShapeDtypeStruct((s_len, R_WIDTH), F32)] * 2 + [
            jax.ShapeDtypeStruct((R_BLOCKS, R_BLOCK_DIM, R_BLOCK_DIM), F32), jax.ShapeDtypeStruct((1, R_WIDTH), F32),
            jax.ShapeDtypeStruct((R_BLOCKS, R_BLOCK_DIM, R_BLOCK_DIM), F32), jax.ShapeDtypeStruct((1, R_WIDTH), F32),
            jax.ShapeDtypeStruct((1, R_WIDTH), F32)],
        scratch_shapes=[pltpu.VMEM((rows, R_WIDTH), F32), pltpu.VMEM((rows, R_WIDTH), F32), pltpu.VMEM((1, R_WIDTH), F32)],
        compiler_params=_params("arbitrary"),
    )(xr, proj_r, h, h, dyr, wa, ba, wx, bx, lam)


GP_CHUNKS = 8
GP_CHUNKS_BWD = 4
GS_CHUNKS = 8


def _seg_cumsum(x, reverse):
    rows = x.shape[0]
    r = _iota(x.shape, 0) & (G_CHUNK - 1)
    s = 1
    while s < G_CHUNK:
        if reverse:
            x = x + jnp.where(r < G_CHUNK - s, pltpu.roll(x, rows - s, 0), 0.0)
        else:
            x = x + jnp.where(r >= s, pltpu.roll(x, s, 0), 0.0)
        s *= 2
    return x


def _gdn_decay(ga, a_log_row, dt_row):
    return -jnp.exp(a_log_row) * _softplus(ga + dt_row)


def _gdn_chunk(cq, ck, cv, gb, gc, inv=None):
    shape = cq.shape
    head = _iota(shape, 0) & (G_HEADS - 1)
    lane = _iota(shape, 2)
    q, k, v = _silu(cq), _silu(ck), _silu(cv)
    q = q * lax.rsqrt(jnp.sum(q * q, axis=-1, keepdims=True) + RMS_EPS) * (G_HEAD_DIM ** -0.5)
    k = k * lax.rsqrt(jnp.sum(k * k, axis=-1, keepdims=True) + RMS_EPS)
    beta = jnp.sum(jnp.where(lane == head, jax.nn.sigmoid(gb), 0.0), axis=-1, keepdims=True)
    g = jnp.sum(jnp.where(lane == head + G_HEADS, gc, 0.0), axis=-1, keepdims=True)
    sq = (shape[0], G_CHUNK, G_CHUNK)
    row, col = _iota(sq, 1), _iota(sq, 2)
    g_sq = jnp.broadcast_to(g, sq)
    decay = jnp.where(row >= col, jnp.exp(jnp.minimum(g_sq - _t(g_sq), 0.0)), 0.0)
    g_last = jnp.sum(jnp.where(_iota(g.shape, 1) == G_CHUNK - 1, g, 0.0), axis=1, keepdims=True)
    eg = jnp.exp(g)
    kb, vb = k * beta, v * beta
    m = jnp.where(row > col, mm_nt(kb, k) * decay, 0.0)
    known = inv is not None
    if not known:
        inv = _inv_unit_lower(m)
    u, w = _solve2(m, inv, vb, kb * eg)
    qk = jnp.where(row >= col, mm_nt(q, k) * decay, 0.0)
    q_dec = q * eg
    k_dec = k * jnp.exp(g_last - g)
    gl = jnp.broadcast_to(jnp.exp(g_last), (shape[0], 1, G_HEAD_DIM))
    return (u, w, qk, q_dec, k_dec, gl) if known else (u, w, qk, q_dec, k_dec, gl, inv)


def _gdn_step(state, u, w, qk, q_dec, k_dec, gl, gz, norm_w):
    v_new = u - mm_nn(w, state)
    o = mm_nn(q_dec, state) + mm_nn(qk, v_new)
    new_state = state * gl + mm_nn(_t(k_dec), v_new)
    o = o * lax.rsqrt(jnp.mean(o * o, axis=-1, keepdims=True) + RMS_EPS) * norm_w
    return o * _silu(gz), new_state


def _stack_chunks(x, heads):
    chunks = x.shape[0] // G_CHUNK
    parts = []
    for c in range(chunks):
        rows = slice(c * G_CHUNK, (c + 1) * G_CHUNK)
        for hd in range(G_HEADS):
            parts.append(x[rows, hd * LANES:(hd + 1) * LANES] if heads else x[rows, :])
    return jnp.stack(parts)


def _gdn_chunk_shapes(nch):
    b = nch * G_HEADS
    wide = jax.ShapeDtypeStruct((b, G_CHUNK, G_HEAD_DIM), F32)
    return [wide, wide, jax.ShapeDtypeStruct((b, G_CHUNK, G_CHUNK), F32), wide, wide,
            jax.ShapeDtypeStruct((b, 1, G_HEAD_DIM), F32)]


def _gdn_chunk_specs(nbatch):
    wide = pl.BlockSpec((nbatch, G_CHUNK, G_HEAD_DIM), lambda i: (i, 0, 0))
    return [wide, wide, pl.BlockSpec((nbatch, G_CHUNK, G_CHUNK), lambda i: (i, 0, 0)), wide, wide,
            pl.BlockSpec((nbatch, 1, G_HEAD_DIM), lambda i: (i, 0, 0))]


def _gdn_chunk_fwd(conv, proj_g, a_log_row, dt_row, *, name):
    s_len = conv.shape[0]
    cpg = min(GP_CHUNKS, s_len // G_CHUNK)
    rows = cpg * G_CHUNK
    nbatch = cpg * G_HEADS

    def body(c_ref, bg_ref, al_ref, dt_ref, *outs):
        bg = bg_ref[...]
        gc = _seg_cumsum(_gdn_decay(bg, al_ref[...], dt_ref[...]), False)
        res = _gdn_chunk(_stack_chunks(c_ref[:, 0:G_WIDTH], True), _stack_chunks(c_ref[:, G_WIDTH:2 * G_WIDTH], True),
                         _stack_chunks(c_ref[:, 2 * G_WIDTH:], True), _stack_chunks(bg, False), _stack_chunks(gc, False))
        for ref, val in zip(outs, res):
            ref[...] = val

    row = pl.BlockSpec((1, LANES), lambda i: (0, 0))
    return _call(
        body, name=name, grid=(s_len // rows,),
        in_specs=[pl.BlockSpec((rows, 3 * G_WIDTH), lambda i: (i, 0)),
                  pl.BlockSpec((rows, LANES), lambda i: (i, (3 * G_WIDTH + G_WIDTH) // LANES)), row, row],
        out_specs=_gdn_chunk_specs(nbatch) + [pl.BlockSpec((nbatch, G_CHUNK, G_CHUNK), lambda i: (i, 0, 0))],
        out_shape=_gdn_chunk_shapes(s_len // G_CHUNK) + [
            jax.ShapeDtypeStruct((s_len // G_CHUNK * G_HEADS, G_CHUNK, G_CHUNK), F32)],
        compiler_params=_params("parallel"),
    )(conv, proj_g, a_log_row, dt_row)


def _gdn_chunk_bwd(conv, proj_g, a_log_row, dt_row, inv, cots, *, name):
    s_len = conv.shape[0]
    cpg = min(GP_CHUNKS_BWD, s_len // G_CHUNK)
    rows = cpg * G_CHUNK
    nbatch = cpg * G_HEADS

    def unstack(x, heads):
        if heads:
            return jnp.concatenate([jnp.concatenate([x[c * G_HEADS + hd] for hd in range(G_HEADS)], axis=1)
                                    for c in range(cpg)], axis=0)
        return jnp.concatenate([sum(x[c * G_HEADS + hd] for hd in range(G_HEADS)) for c in range(cpg)], axis=0)

    def body(c_ref, bg_ref, al_ref, dt_ref, inv_ref, du, dw, dqk, dqd, dkd, dgl, dc_ref, dbg_ref, dal_ref, ddt_ref):
        @pl.when(pl.program_id(0) == 0)
        def _():
            dal_ref[...] = jnp.zeros_like(dal_ref)
            ddt_ref[...] = jnp.zeros_like(ddt_ref)

        bg = bg_ref[...]
        g_all, decay_vjp = jax.vjp(_gdn_decay, bg, al_ref[...], dt_ref[...])
        gc = _seg_cumsum(g_all, False)
        _, vjp = jax.vjp(_gdn_chunk, _stack_chunks(c_ref[:, 0:G_WIDTH], True),
                         _stack_chunks(c_ref[:, G_WIDTH:2 * G_WIDTH], True), _stack_chunks(c_ref[:, 2 * G_WIDTH:], True),
                         _stack_chunks(bg, False), _stack_chunks(gc, False), inv_ref[...])
        dq, dk, dv, dgb, dgc, _ = vjp((du[...], dw[...], dqk[...], dqd[...], dkd[...], dgl[...]))
        dc_ref[:, 0:G_WIDTH] = unstack(dq, True)
        dc_ref[:, G_WIDTH:2 * G_WIDTH] = unstack(dk, True)
        dc_ref[:, 2 * G_WIDTH:] = unstack(dv, True)
        dga, dal, ddt = decay_vjp(_seg_cumsum(unstack(dgc, False), True))
        dbg_ref[:, 0:LANES] = unstack(dgb, False) + dga
        dbg_ref[:, LANES:] = jnp.zeros((rows, LANES), F32)
        dal_ref[...] += dal
        ddt_ref[...] += ddt

    row = pl.BlockSpec((1, LANES), lambda i: (0, 0))
    return _call(
        body, name=name, grid=(s_len // rows,),
        in_specs=[pl.BlockSpec((rows, 3 * G_WIDTH), lambda i: (i, 0)),
                  pl.BlockSpec((rows, LANES), lambda i: (i, (3 * G_WIDTH + G_WIDTH) // LANES)), row, row,
                  pl.BlockSpec((nbatch, G_CHUNK, G_CHUNK), lambda i: (i, 0, 0))]
        + _gdn_chunk_specs(nbatch),
        out_specs=[pl.BlockSpec((rows, 3 * G_WIDTH), lambda i: (i, 0)), pl.BlockSpec((rows, 2 * LANES), lambda i: (i, 0)),
                   row, row],
        out_shape=[jax.ShapeDtypeStruct((s_len, 3 * G_WIDTH), F32), jax.ShapeDtypeStruct((s_len, 2 * LANES), F32),
                   jax.ShapeDtypeStruct((1, LANES), F32), jax.ShapeDtypeStruct((1, LANES), F32)],
        compiler_params=_params("arbitrary"),
    )(conv, proj_g, a_log_row, dt_row, inv, *cots)


def _gdn_scan_specs(cpg, which):
    nbatch = cpg * G_HEADS
    wide = pl.BlockSpec((nbatch, G_CHUNK, G_HEAD_DIM), lambda i: (which(i), 0, 0))
    return [wide, wide, pl.BlockSpec((nbatch, G_CHUNK, G_CHUNK), lambda i: (which(i), 0, 0)), wide, wide,
            pl.BlockSpec((nbatch, 1, G_HEAD_DIM), lambda i: (which(i), 0, 0))]


def _gz_stack(z_ref, c):
    rows = pl.ds(pl.multiple_of(c * G_CHUNK, G_CHUNK), G_CHUNK)
    return jnp.stack([z_ref[rows, hd * LANES:(hd + 1) * LANES] for hd in range(G_HEADS)])


def _gdn_scan_fwd(chunk_vals, proj_g, norm_w, *, name):
    s_len = proj_g.shape[0]
    nch = s_len // G_CHUNK
    cpg = min(GS_CHUNKS, nch)
    rows = cpg * G_CHUNK

    def body(u_ref, w_ref, qk_ref, qd_ref, kd_ref, gl_ref, z_ref, nw_ref, y_ref, st_ref, state):
        @pl.when(pl.program_id(0) == 0)
        def _():
            state[...] = jnp.zeros_like(state)

        def step(c, carry):
            b = pl.ds(pl.multiple_of(c * G_HEADS, G_HEADS), G_HEADS)
            st = state[...]
            st_ref[b] = st
            y, new_state = _gdn_step(st, u_ref[b], w_ref[b], qk_ref[b], qd_ref[b], kd_ref[b], gl_ref[b],
                                     _gz_stack(z_ref, c), nw_ref[...])
            state[...] = new_state
            rws = pl.ds(pl.multiple_of(c * G_CHUNK, G_CHUNK), G_CHUNK)
            for hd in range(G_HEADS):
                y_ref[rws, hd * LANES:(hd + 1) * LANES] = y[hd].astype(y_ref.dtype)
            return carry

        lax.fori_loop(0, cpg, step, 0, unroll=4)

    return _call(
        body, name=name, grid=(nch // cpg,),
        in_specs=_gdn_scan_specs(cpg, lambda i: i) + [
            pl.BlockSpec((rows, G_WIDTH), lambda i: (i, 3)), pl.BlockSpec((1, G_HEAD_DIM), lambda i: (0, 0))],
        out_specs=[pl.BlockSpec((rows, G_WIDTH), lambda i: (i, 0)),
                   pl.BlockSpec((cpg * G_HEADS, G_HEAD_DIM, G_HEAD_DIM), lambda i: (i, 0, 0))],
        out_shape=[jax.ShapeDtypeStruct((s_len, G_WIDTH), MXU_DTYPE),
                   jax.ShapeDtypeStruct((nch * G_HEADS, G_HEAD_DIM, G_HEAD_DIM), F32)],
        scratch_shapes=[pltpu.VMEM((G_HEADS, G_HEAD_DIM, G_HEAD_DIM), F32)],
        compiler_params=_params("arbitrary"),
    )(*chunk_vals, proj_g, norm_w)


def _gdn_scan_bwd(chunk_vals, states, proj_g, norm_w, dyg, *, name):
    s_len = proj_g.shape[0]
    nch = s_len // G_CHUNK
    cpg = min(GS_CHUNKS, nch)
    rows = cpg * G_CHUNK
    ngrid = nch // cpg

    def cur(i):
        return ngrid - 1 - i

    def body(u_ref, w_ref, qk_ref, qd_ref, kd_ref, gl_ref, st_ref, z_ref, nw_ref, dy_ref,
             du_ref, dw_ref, dqk_ref, dqd_ref, dkd_ref, dgl_ref, dz_ref, dnw_ref, dstate):
        @pl.when(pl.program_id(0) == 0)
        def _():
            dstate[...] = jnp.zeros_like(dstate)
            dnw_ref[...] = jnp.zeros_like(dnw_ref)

        def step(k, carry):
            c = cpg - 1 - k
            b = pl.ds(pl.multiple_of(c * G_HEADS, G_HEADS), G_HEADS)
            _, vjp = jax.vjp(_gdn_step, st_ref[b], u_ref[b], w_ref[b], qk_ref[b], qd_ref[b], kd_ref[b], gl_ref[b],
                             _gz_stack(z_ref, c), nw_ref[...])
            dst, du, dw, dqk, dqd, dkd, dgl, dz, dnw = vjp((_gz_stack(dy_ref, c), dstate[...]))
            dstate[...] = dst
            du_ref[b], dw_ref[b], dqk_ref[b], dqd_ref[b], dkd_ref[b], dgl_ref[b] = du, dw, dqk, dqd, dkd, dgl
            rws = pl.ds(pl.multiple_of(c * G_CHUNK, G_CHUNK), G_CHUNK)
            for hd in range(G_HEADS):
                dz_ref[rws, hd * LANES:(hd + 1) * LANES] = dz[hd]
            dnw_ref[...] += dnw
            return carry

        lax.fori_loop(0, cpg, step, 0, unroll=4)

    gate = pl.BlockSpec((rows, G_WIDTH), lambda i: (cur(i), 3))
    wide = pl.BlockSpec((rows, G_WIDTH), lambda i: (cur(i), 0))
    vec = pl.BlockSpec((1, G_HEAD_DIM), lambda i: (0, 0))
    return _call(
        body, name=name, grid=(ngrid,),
        in_specs=_gdn_scan_specs(cpg, cur) + [
            pl.BlockSpec((cpg * G_HEADS, G_HEAD_DIM, G_HEAD_DIM), lambda i: (cur(i), 0, 0)), gate, vec, wide],
        out_specs=_gdn_scan_specs(cpg, cur) + [wide, vec],
        out_shape=_gdn_chunk_shapes(nch) + [jax.ShapeDtypeStruct((s_len, G_WIDTH), F32),
                                            jax.ShapeDtypeStruct((1, G_HEAD_DIM), F32)],
        scratch_shapes=[pltpu.VMEM((G_HEADS, G_HEAD_DIM, G_HEAD_DIM), F32)],
        compiler_params=_params("arbitrary"),
    )(*chunk_vals, states, proj_g, norm_w, dyg)


def _adamw_math(w, g, m, v):
    m = ADAM_B1 * m + (1.0 - ADAM_B1) * g
    v = ADAM_B2 * v + (1.0 - ADAM_B2) * (g * g)
    m_hat = m / (1.0 - ADAM_B1 ** ADAM_STEP)
    v_hat = v / (1.0 - ADAM_B2 ** ADAM_STEP)
    delta = -ADAM_LR * (m_hat / (jnp.sqrt(v_hat) + ADAM_EPS) + ADAM_WD * w)
    return delta, m, v


def _sum_adamw(own, chip, parts, w, m, v, *, name, rows):
    n_layers, n_rows, n_cols = w.shape
    rows = min(rows, n_rows)
    n_parts = parts[0].shape[0]

    def body(c_ref, *refs):
        own_refs, part_refs = refs[:n_layers], refs[n_layers:2 * n_layers]
        w_ref, m_ref, v_ref, g_ref, d_ref, nm_ref, nv_ref = refs[2 * n_layers:]
        layer = pl.program_id(0)
        g = None
        for l in range(n_layers):
            g_l = own_refs[l][0].astype(F32)
            for k in range(n_parts):
                g_l = g_l + part_refs[l][k].astype(F32)
            g = g_l if g is None else jnp.where(layer == l, g_l, g)
        delta, new_m, new_v = _adamw_math(w_ref[0], g, m_ref[0], v_ref[0])
        g_ref[0], d_ref[0], nm_ref[0], nv_ref[0] = g, delta, new_m, new_v

    blk = pl.BlockSpec((1, rows, n_cols), lambda l, i, c: (l, i, 0))
    grid_spec = pltpu.PrefetchScalarGridSpec(
        num_scalar_prefetch=1, grid=(n_layers, n_rows // rows),
        in_specs=[pl.BlockSpec((1, rows, n_cols), lambda l, i, c: (c[0], i, 0))] * n_layers
        + [pl.BlockSpec((n_parts, rows, n_cols), lambda l, i, c: (0, i, 0))] * n_layers + [blk, blk, blk],
        out_specs=[blk] * 4)
    return _call(
        body, name=name, grid_spec=grid_spec, out_shape=[jax.ShapeDtypeStruct(w.shape, F32)] * 4,
        compiler_params=_params("parallel", "parallel"),
    )(_index_operand(chip), *own, *parts, w, m, v)


def _sum_slots(parts, *, name):
    rows = parts.shape[1]

    def body(p_ref, o_ref):
        g = p_ref[0]
        for k in range(1, N_DEV):
            g = g + p_ref[k]
        o_ref[...] = g

    return _call(body, name=name, grid=(1,),
                 in_specs=[pl.BlockSpec(parts.shape, lambda i: (0, 0, 0))],
                 out_specs=pl.BlockSpec((rows, LANES), lambda i: (0, 0)),
                 out_shape=jax.ShapeDtypeStruct((rows, LANES), F32), compiler_params=_params("arbitrary"))(parts)


def _adamw_packed(w, g, m, v, *, name):
    def body(w_ref, g_ref, m_ref, v_ref, d_ref, nm_ref, nv_ref):
        d_ref[...], nm_ref[...], nv_ref[...] = _adamw_math(w_ref[...], g_ref[...], m_ref[...], v_ref[...])

    blk = pl.BlockSpec(w.shape, lambda i: (0, 0))
    return _call(body, name=name, grid=(1,), in_specs=[blk] * 4, out_specs=[blk] * 3,
                 out_shape=[jax.ShapeDtypeStruct(w.shape, F32)] * 3, compiler_params=_params("arbitrary"))(w, g, m, v)


A_COLS = ((0, 512), (768, 1280), (512, 768))
R_COLS = ((1280, 3328),)
G_COLS = ((3328, 5384),)


def _group_weights(wt_full):
    def take(ranges):
        return jnp.concatenate([wt_full[a:b] for a, b in ranges], axis=0)

    wt_g = jnp.concatenate([take(G_COLS), jnp.zeros((G_PAD, wt_full.shape[1]), wt_full.dtype)], axis=0)
    return take(A_COLS), take(R_COLS), wt_g


def _ungroup_grads(d_a, d_r, d_g):
    return jnp.concatenate([d_a[0:512], d_a[1024:1280], d_a[512:1024], d_r, d_g[:WG - G_PAD]], axis=0)


def _shard_rows(w):
    return jnp.pad(jnp.transpose(w, (0, 2, 1)), ((0, 0), (0, N_ROWS_PAD - N_IN_SHARD), (0, 0)))


def _unshard_rows(wt):
    return jnp.transpose(wt[:, :N_IN_SHARD], (0, 2, 1))


def _owner_blocks(dwt):
    blocks = jnp.pad(dwt.reshape(4, 2, N_IN_SHARD, D_MODEL), ((0, 0), (0, 0), (0, N_ROWS_PAD - N_IN_SHARD), (0, 0)))
    return jnp.transpose(blocks, (1, 0, 2, 3))


def _rope_tables(s_len):
    inv = 1.0 / (ROPE_THETA ** (jnp.arange(0, A_HEAD_DIM, 2, dtype=F32) / A_HEAD_DIM))
    ang = jnp.arange(s_len, dtype=F32)[:, None] * inv[None, :]
    cos, sin = jnp.cos(ang), jnp.sin(ang)
    return jnp.tile(cos, (1, 4)), jnp.tile(jnp.concatenate([-sin, sin], axis=1), (1, 2))


SMALL = ("sinks", "r_conv_w", "r_conv_b", "r_wa", "r_ba", "r_wx", "r_bx", "r_lam", "g_conv_w", "g_a_log", "g_dt_bias",
         "g_norm_w", "ln_g", "ln_b")


def _pack(leaves):
    rows = []
    for leaf in leaves:
        flat = leaf.reshape(-1)
        pad = (-flat.shape[0]) % (8 * LANES)
        rows.append(jnp.pad(flat, (0, pad)).reshape(-1, LANES))
    return jnp.concatenate(rows, axis=0)


def _unpack(packed, shapes):
    out, row = [], 0
    for shape in shapes:
        size = math.prod(shape)
        nrows = -(-size // (8 * LANES)) * 8
        out.append(packed[row:row + nrows].reshape(-1)[:size].reshape(shape))
        row += nrows
    return out


def _lane_row(vals, offset):
    return jnp.pad(vals, (offset, LANES - offset - vals.shape[0])).reshape(1, LANES)


def kernel(x, w_in, sinks, r_conv_w, r_conv_b, r_wa, r_ba, r_wx, r_bx, r_lam, g_conv_w, g_a_log, g_dt_bias, g_norm_w, w_out, ln_g, ln_b, loss_target, m_w_in, m_sinks, m_r_conv_w, m_r_conv_b, m_r_wa, m_r_ba, m_r_wx, m_r_bx, m_r_lam, m_g_conv_w, m_g_a_log, m_g_dt_bias, m_g_norm_w, m_w_out, m_ln_g, m_ln_b, v_w_in, v_sinks, v_r_conv_w, v_r_conv_b, v_r_wa, v_r_ba, v_r_wx, v_r_bx, v_r_lam, v_g_conv_w, v_g_a_log, v_g_dt_bias, v_g_norm_w, v_w_out, v_ln_g, v_ln_b):
    s_len = x.shape[1]
    x0 = x.reshape(s_len, D_MODEL)
    target = loss_target.reshape(s_len, D_MODEL)
    me = 4 * lax.axis_index("x") + 2 * lax.axis_index("y") + lax.axis_index("c")
    core, chip = lax.axis_index("c"), 2 * lax.axis_index("x") + lax.axis_index("y")

    win_pieces = _shard_rows(w_in).astype(MXU_DTYPE).reshape(DEPTH, 2, N_ROWS_PAD // 2, D_MODEL)
    wout_pieces = w_out.astype(MXU_DTYPE).reshape(DEPTH, 2, OUT_SHARD // 2, D_MODEL)
    win0_all, wout0_all, rcw_all, gcw_all = _all_gather(
        [win_pieces[0], wout_pieces[0], r_conv_w[None], g_conv_w[None]], "gather_weights")
    rcw_full = jnp.moveaxis(rcw_all[:, 0], 0, 2).reshape(DEPTH, CONV_WIDTH, R_WIDTH)
    gcw_full = jnp.moveaxis(gcw_all[:, 0], 0, 2).reshape(DEPTH, CONV_WIDTH, 3 * G_WIDTH)
    cos, sin = _rope_tables(s_len)

    def big_weights(win_all, wout_all):
        wt_a, wt_r, wt_g = _group_weights(win_all.reshape(N_DEV, N_ROWS_PAD, D_MODEL)[:, :N_IN_SHARD].reshape(N_IN, D_MODEL))
        wo = wout_all.reshape(D_MODEL, D_MODEL)
        return dict(wt_a=wt_a, wt_r=wt_r, wt_g=wt_g, wo=wo,
                    wo_a=wo[0:A_WIDTH], wo_r=wo[A_WIDTH:A_WIDTH + R_WIDTH], wo_g=wo[A_WIDTH + R_WIDTH:])

    layers = []
    for l in range(DEPTH):
        layers.append(dict(
            sinks_t=jnp.broadcast_to(sinks[l][:, None, None], (A_HEADS, 8, LANES)),
            rcw=rcw_full[l], rcb=r_conv_b[l].reshape(1, R_WIDTH), wa=r_wa[l], ba=r_ba[l].reshape(1, R_WIDTH),
            wx=r_wx[l], bx=r_bx[l].reshape(1, R_WIDTH), lam=r_lam[l].reshape(1, R_WIDTH),
            gcw=gcw_full[l], zero_b=jnp.zeros((1, 3 * G_WIDTH), F32),
            a_log=_lane_row(g_a_log[l], G_HEADS), dt=_lane_row(g_dt_bias[l], G_HEADS),
            norm_w=g_norm_w[l].reshape(1, G_HEAD_DIM), ln_g=ln_g[l].reshape(1, D_MODEL), ln_b=ln_b[l].reshape(1, D_MODEL)))

    saved = []
    xin = xin_lo = x0
    layers[0].update(big_weights(win0_all, wout0_all))
    for l, p in enumerate(layers):
        if l + 1 < DEPTH:
            proj_a, (wout_next,) = _matmul([xin_lo], [p["wt_a"]], name=f"proj_a{l}", tm=1024, tn=1280, b_t=True,
                                           comm=_GatherSend([wout_pieces[l + 1]]))
            proj_r, (win_next_0,) = _matmul([xin_lo], [p["wt_r"]], name=f"proj_r{l}", tm=1024, tn=1024, b_t=True,
                                            comm=_GatherSend([win_pieces[l + 1, 0:1]]))
            proj_g, (win_next_1,) = _matmul([xin_lo], [p["wt_g"]], name=f"proj_g{l}", tm=1024, tn=1152, b_t=True,
                                            comm=_GatherSend([win_pieces[l + 1, 1:2]]))
            forward_next = _GatherForward([win_next_0, win_next_1, wout_next])
        else:
            forward_next = None
            proj_a = _matmul([xin_lo], [p["wt_a"]], name=f"proj_a{l}", tm=1024, tn=1280, b_t=True)
            proj_r = _matmul([xin_lo], [p["wt_r"]], name=f"proj_r{l}", tm=1024, tn=1024, b_t=True)
            proj_g = _matmul([xin_lo], [p["wt_g"]], name=f"proj_g{l}", tm=1024, tn=1152, b_t=True)
        ya = _attn_fwd(proj_a, cos, sin, p["sinks_t"], name=f"attn_fwd{l}")
        xr = _conv_fwd(proj_r, R_WIDTH, p["rcw"], p["rcb"], name=f"rconv_fwd{l}")
        h, yr = _rg_fwd(xr, proj_r, p["wa"], p["ba"], p["wx"], p["bx"], p["lam"], name=f"rglru_fwd{l}")
        conv = _conv_fwd(proj_g, 3 * G_WIDTH, p["gcw"], p["zero_b"], name=f"gconv_fwd{l}")
        *chunk_vals, inv = _gdn_chunk_fwd(conv, proj_g, p["a_log"], p["dt"], name=f"gdn_chunk_fwd{l}")
        yg, states = _gdn_scan_fwd(chunk_vals, proj_g, p["norm_w"], name=f"gdn_scan_fwd{l}")
        if forward_next is None:
            z, xout, xout_lo = _outproj_ln(ya, yr, yg, p["wo"], xin, p["ln_g"], p["ln_b"], name=f"outproj_ln{l}")
        else:
            (z, xout, xout_lo), (win_0, win_1, wout_all) = _outproj_ln(
                ya, yr, yg, p["wo"], xin, p["ln_g"], p["ln_b"], name=f"outproj_ln{l}", comm=forward_next)
            layers[l + 1].update(big_weights(jnp.concatenate([win_0, win_1], axis=1), wout_all))
        saved.append(dict(xin_lo=xin_lo, proj_a=proj_a, proj_r=proj_r, proj_g=proj_g, ya=ya, yr=yr, yg=yg, xr=xr, h=h,
                          conv=conv, chunk_vals=chunk_vals, inv=inv, states=states, z=z))
        xin, xin_lo = xout, xout_lo

    grads = [None] * DEPTH
    dxn = None
    loss_local = None
    for l in reversed(range(DEPTH)):
        p, sv = layers[l], saved[l]
        if dxn is None:
            dz, dz_lo, dln_g, dln_b, loss_local = _ln_bwd(sv["z"], p["ln_g"], name=f"ln_bwd{l}", xn=xin, target=target)
        else:
            dz, dz_lo, dln_g, dln_b = _ln_bwd(sv["z"], p["ln_g"], name=f"ln_bwd{l}", dxn=dxn)
        dya = _matmul([dz_lo], [p["wo_a"]], name=f"dya{l}", tm=1024, tn=512, b_t=True)
        dyr = _matmul([dz_lo], [p["wo_r"]], name=f"dyr{l}", tm=1024, tn=1024, b_t=True)
        dyg = _matmul([dz_lo], [p["wo_g"]], name=f"dyg{l}", tm=1024, tn=512, b_t=True)
        dwo = jnp.concatenate([
            _matmul_tn(sv["ya"], dz_lo, name=f"dwo_a{l}", tm=512, tn=1024, tk=1024),
            _matmul_tn(sv["yr"], dz_lo, name=f"dwo_r{l}", tm=1024, tn=1024, tk=1024),
            _matmul_tn(sv["yg"], dz_lo, name=f"dwo_g{l}", tm=512, tn=1024, tk=1024)], axis=0)

        dproj_a, dsinks_t = _attn_bwd(sv["proj_a"], cos, sin, p["sinks_t"], dya, name=f"attn_bwd{l}")

        dxr, drz, dwa, dba, dwx, dbx, dlam = _rg_bwd(sv["xr"], sv["proj_r"], sv["h"], dyr, p["wa"], p["ba"], p["wx"],
                                                     p["bx"], p["lam"], name=f"rglru_bwd{l}")
        dproj_r, drcw, drcb = _conv_bwd(dxr, sv["proj_r"], R_WIDTH, p["rcw"], [drz], name=f"rconv_bwd{l}")

        scan_out = _gdn_scan_bwd(sv["chunk_vals"], sv["states"], sv["proj_g"], p["norm_w"], dyg, name=f"gdn_scan_bwd{l}")
        dgz, dnorm_w = scan_out[6], scan_out[7]
        dconv, dbg, dal, ddt = _gdn_chunk_bwd(sv["conv"], sv["proj_g"], p["a_log"], p["dt"], sv["inv"], scan_out[:6],
                                              name=f"gdn_chunk_bwd{l}")
        dproj_g, dgcw, _ = _conv_bwd(dconv, sv["proj_g"], 3 * G_WIDTH, p["gcw"], [dgz, dbg], name=f"gconv_bwd{l}")

        grads[l] = dict(
            sinks=dsinks_t[:, :, 0].sum(axis=1), r_conv_w=drcw.reshape(CONV_WIDTH, R_WIDTH),
            r_conv_b=drcb.reshape(R_WIDTH), r_wa=dwa, r_ba=dba.reshape(R_WIDTH), r_wx=dwx, r_bx=dbx.reshape(R_WIDTH),
            r_lam=dlam.reshape(R_WIDTH), g_conv_w=dgcw.reshape(CONV_WIDTH, 3 * G_WIDTH),
            g_a_log=dal[0, G_HEADS:2 * G_HEADS], g_dt_bias=ddt[0, G_HEADS:2 * G_HEADS],
            g_norm_w=dnorm_w.reshape(G_HEAD_DIM), ln_g=dln_g.reshape(D_MODEL), ln_b=dln_b.reshape(D_MODEL))

        if l > 0:
            dwin_a = _matmul_tn(dproj_a, sv["xin_lo"], name=f"dwin_a{l}", tm=640, tn=1024, tk=1024)
            dwin_r = _matmul_tn(dproj_r, sv["xin_lo"], name=f"dwin_r{l}", tm=1024, tn=1024, tk=1024)
        else:
            packed_small = _pack([jnp.stack([grads[k][nm] for k in range(DEPTH)]) for nm in SMALL])
            dwin_a, (sent_small,) = _matmul_tn(
                dproj_a, sv["xin_lo"], name=f"dwin_a{l}", tm=640, tn=1024, tk=1024,
                comm=_GatherSend([packed_small.reshape(4, packed_small.shape[0] // 4, LANES)]))
            dwin_r, (all_small,) = _matmul_tn(dproj_r, sv["xin_lo"], name=f"dwin_r{l}", tm=1024, tn=1024, tk=1024,
                                              comm=_GatherForward([sent_small]))
        dwin = _ungroup_grads(dwin_a, dwin_r,
                              _matmul_tn(dproj_g, sv["xin_lo"], name=f"dwin_g{l}", tm=1152, tn=1024, tk=1024))

        dwin_blocks = _owner_blocks(dwin)[:, :, None].astype(MXU_DTYPE)
        dwout_blocks = jnp.transpose(dwo.reshape(4, 2, OUT_SHARD, D_MODEL), (1, 0, 2, 3))[:, :, None].astype(MXU_DTYPE)
        got_win, got_wout = _swap_cores(
            [dwin_blocks.reshape(2, 8, N_ROWS_PAD // 2, D_MODEL), dwout_blocks.reshape(2, 4, OUT_SHARD, D_MODEL)],
            f"swap_core_grads{l}")
        chip_win = _add_pair(dwin_blocks, got_win.reshape(dwin_blocks.shape[1:]), core, name=f"add_core_grads_w_in{l}",
                             rows=352).reshape(4, N_ROWS_PAD, D_MODEL)
        chip_wout = _add_pair(dwout_blocks, got_wout.reshape(dwout_blocks.shape[1:]), core, name=f"add_core_grads_w_out{l}",
                              rows=256).reshape(4, OUT_SHARD, D_MODEL)
        dxn, (win_parts, wout_parts) = _matmul(
            [dproj_a, dproj_r, dproj_g], [p["wt_a"], p["wt_r"], p["wt_g"]], name=f"dx{l}", tm=512, tn=1024, add=dz,
            add_scale=DEEPNORM_ALPHA, comm=_ChipExchange([chip_win, chip_wout]))
        grads[l].update(chip_win=chip_win, chip_wout=chip_wout, win_parts=win_parts, wout_parts=wout_parts)
    grad_x = dxn.reshape(x.shape)
    loss = lax.psum(loss_local[0, 0], ("x", "y", "c"))

    def stacked(name):
        return jnp.stack([grads[l][name] for l in range(DEPTH)])

    def per_layer(name):
        return [grads[l][name] for l in range(DEPTH)]

    w_in_t = [_unshard_rows(t) for t in _sum_adamw(per_layer("chip_win"), chip, per_layer("win_parts"), _shard_rows(w_in),
                                                   _shard_rows(m_w_in), _shard_rows(v_w_in), name="adamw_w_in", rows=176)]
    g_w_in, d_w_in, nm_w_in, nv_w_in = w_in_t
    g_w_out, d_w_out, nm_w_out, nv_w_out = _sum_adamw(per_layer("chip_wout"), chip, per_layer("wout_parts"), w_out,
                                                      m_w_out, v_w_out, name="adamw_w_out", rows=128)

    small = list(SMALL)
    full_shapes = [stacked(nm).shape for nm in small]
    all_small = all_small.reshape(N_DEV, packed_small.shape[0], LANES)
    g_small = dict(zip(small, _unpack(_sum_slots(all_small, name="sum_small_grads"), full_shapes)))
    g_small["r_conv_w"] = lax.dynamic_slice_in_dim(g_small["r_conv_w"], me * (R_WIDTH // N_DEV), R_WIDTH // N_DEV, axis=2)
    g_small["g_conv_w"] = lax.dynamic_slice_in_dim(g_small["g_conv_w"], me * (3 * G_WIDTH // N_DEV), 3 * G_WIDTH // N_DEV, axis=2)
    given = dict(sinks=(sinks, m_sinks, v_sinks), r_conv_w=(r_conv_w, m_r_conv_w, v_r_conv_w),
                 r_conv_b=(r_conv_b, m_r_conv_b, v_r_conv_b), r_wa=(r_wa, m_r_wa, v_r_wa), r_ba=(r_ba, m_r_ba, v_r_ba),
                 r_wx=(r_wx, m_r_wx, v_r_wx), r_bx=(r_bx, m_r_bx, v_r_bx), r_lam=(r_lam, m_r_lam, v_r_lam),
                 g_conv_w=(g_conv_w, m_g_conv_w, v_g_conv_w), g_a_log=(g_a_log, m_g_a_log, v_g_a_log),
                 g_dt_bias=(g_dt_bias, m_g_dt_bias, v_g_dt_bias), g_norm_w=(g_norm_w, m_g_norm_w, v_g_norm_w),
                 ln_g=(ln_g, m_ln_g, v_ln_g), ln_b=(ln_b, m_ln_b, v_ln_b))
    shard_shapes = [given[nm][0].shape for nm in small]
    packed = [_pack([given[nm][k] for nm in small]) for k in range(3)]
    d_p, nm_p, nv_p = _adamw_packed(packed[0], _pack([g_small[nm] for nm in small]), packed[1], packed[2], name="adamw_small")
    d_small = dict(zip(small, _unpack(d_p, shard_shapes)))
    nm_small = dict(zip(small, _unpack(nm_p, shard_shapes)))
    nv_small = dict(zip(small, _unpack(nv_p, shard_shapes)))

    order = ["w_in"] + small[:12] + ["w_out"] + small[12:]

    def leaf(big_in, big_out, table):
        return [big_in if nm == "w_in" else big_out if nm == "w_out" else table[nm] for nm in order]

    return (loss, grad_x, *leaf(g_w_in, g_w_out, g_small), *leaf(d_w_in, d_w_out, d_small),
            *leaf(nm_w_in, nm_w_out, nm_small), *leaf(nv_w_in, nv_w_out, nv_small))
```

```python
import functools
import math

import jax
import jax.numpy as jnp
from jax import lax
from jax.experimental import pallas as pl
from jax.experimental.pallas import tpu as pltpu

F32 = jnp.float32
MXU_DTYPE = jnp.bfloat16
HIGHEST = lax.Precision.HIGHEST
MESH_ID = pl.DeviceIdType.MESH

N_DEV = 8
DEPTH = 2
D_MODEL = 2048
A_HEADS, A_KV_HEADS, A_HEAD_DIM = 8, 2, 64
A_WIDTH, A_KV_WIDTH = 512, 128
A_BLOCK = 128
ROPE_THETA = 10000.0
R_WIDTH, R_BLOCKS, R_BLOCK_DIM = 1024, 8, 128
R_C = 8.0
CONV_WIDTH = 4
G_HEADS, G_HEAD_DIM, G_WIDTH, G_CHUNK = 4, 128, 512, 64
N_IN = 5384
N_IN_SHARD = N_IN // N_DEV
N_ROWS_PAD = 704
OUT_SHARD = D_MODEL // N_DEV
WA, WR, WG = 1280, 2048, 2304
G_PAD = WG - (3 * G_WIDTH + G_WIDTH + 2 * G_HEADS)
DEEPNORM_ALPHA = (2 * DEPTH) ** 0.25
LN_EPS = 1e-5
RMS_EPS = 1e-6
ADAM_LR, ADAM_B1, ADAM_B2, ADAM_EPS, ADAM_WD, ADAM_STEP = 0.001, 0.9, 0.999, 1e-08, 0.01, 10
NEG = -1e30
VMEM_LIMIT = 56 * 1024 * 1024
LANES = 128


def _call(body, **kw):
    return pl.pallas_call(body, **kw)


def _params(*sem):
    return pltpu.CompilerParams(dimension_semantics=sem, vmem_limit_bytes=VMEM_LIMIT)


def _t(x):
    return jnp.swapaxes(x, -1, -2)


def _raw_dot(a, b, ca, cb, precision=None):
    batch = tuple(range(a.ndim - 2))
    if precision is None:
        a, b = a.astype(MXU_DTYPE), b.astype(MXU_DTYPE)
    return lax.dot_general(a, b, (((ca,), (cb,)), (batch, batch)), precision=precision,
                           preferred_element_type=F32)


def _nn(a, b, precision=None):
    return _raw_dot(a, b, a.ndim - 1, b.ndim - 2, precision)


def _nt(a, b, precision=None):
    return _raw_dot(a, b, a.ndim - 1, b.ndim - 1, precision)


@jax.custom_vjp
def mm_nn(a, b):
    return _nn(a, b)


def _mm_nn_fwd(a, b):
    return _nn(a, b), (a, b)


def _mm_nn_bwd(res, g):
    a, b = res
    return _nt(g, b), _nn(_t(a), g)


mm_nn.defvjp(_mm_nn_fwd, _mm_nn_bwd)


@jax.custom_vjp
def mm_nt(a, b):
    return _nt(a, b)


def _mm_nt_fwd(a, b):
    return _nt(a, b), (a, b)


def _mm_nt_bwd(res, g):
    a, b = res
    return _nn(g, b), _nn(_t(g), a)


mm_nt.defvjp(_mm_nt_fwd, _mm_nt_bwd)


def _split(x):
    hi = x.astype(MXU_DTYPE)
    return hi, (x - hi.astype(F32)).astype(MXU_DTYPE)


def _hmm(a, b, nt=False):
    dot = _nt if nt else _nn
    return dot(a[0], b[0]) + (dot(a[0], b[1]) + dot(a[1], b[0]))


def _silu(x):
    return x * jax.nn.sigmoid(x)


def _softplus(x):
    return jnp.maximum(x, 0.0) + jnp.log1p(jnp.exp(-jnp.abs(x)))


def _one_minus_sq(log_a, a):
    x = 2.0 * log_a
    return jnp.where(x > -0.01, -x * (1.0 + 0.5 * x), 1.0 - a * a)


def _iota(shape, dim):
    return lax.broadcasted_iota(jnp.int32, shape, dim)


def _inv_unit_lower(m):
    shape = m.shape
    row, col = _iota(shape, 1), _iota(shape, 2)
    eye = (row == col).astype(F32)

    def blockdiag(size):
        return (row // size) == (col // size)

    x = -jnp.where(blockdiag(8), m, 0.0)
    xs = _split(x)
    x2s = _split(_hmm(xs, xs))
    x4s = _split(_hmm(x2s, x2s))
    inv = eye + x
    inv = inv + _hmm(_split(inv), x2s)
    inv = inv + _hmm(_split(inv), x4s)
    for size in (8, 16, 32):
        below = jnp.where(blockdiag(2 * size) & jnp.logical_not(blockdiag(size)), m, 0.0)
        invs = _split(inv)
        inv = inv - _hmm(_split(_hmm(invs, _split(below))), invs)
    return inv


@jax.custom_vjp
def _solve2(m, inv, r1, r2):
    invs = _split(inv)
    return _hmm(invs, _split(r1)), _hmm(invs, _split(r2))


def _solve2_fwd(m, inv, r1, r2):
    x1, x2 = _solve2(m, inv, r1, r2)
    return (x1, x2), (inv, x1, x2)


def _solve2_bwd(res, g):
    inv, x1, x2 = res
    inv_ts = _split(_t(inv))
    d1, d2 = _hmm(inv_ts, _split(g[0])), _hmm(inv_ts, _split(g[1]))
    dm = -(_hmm(_split(d1), _split(x1), nt=True) + _hmm(_split(d2), _split(x2), nt=True))
    return dm, jnp.zeros_like(inv), d1, d2


_solve2.defvjp(_solve2_fwd, _solve2_bwd)


def _swap_halves(x):
    n = x.shape[-1]
    lane = _iota(x.shape, x.ndim - 1)
    return jnp.where((lane & 63) < 32, pltpu.roll(x, n - 32, x.ndim - 1), pltpu.roll(x, 32, x.ndim - 1))


def _rope(x, cos, sin):
    reps = x.shape[-1] // LANES
    if reps > 1:
        cos, sin = jnp.tile(cos, (1, reps)), jnp.tile(sin, (1, reps))
    return x * cos + _swap_halves(x) * sin


def _rope_t(d, cos, sin):
    reps = d.shape[-1] // LANES
    if reps > 1:
        cos, sin = jnp.tile(cos, (1, reps)), jnp.tile(sin, (1, reps))
    return d * cos + _swap_halves(d * sin)


def _swap64(x):
    return pltpu.roll(x, 64, x.ndim - 1)


def _mesh_pos():
    return lax.axis_index("x"), lax.axis_index("y"), lax.axis_index("c")


def _all_gather(arrays, name):
    n = len(arrays)
    npieces = [a.shape[0] for a in arrays]
    pmax = max(npieces)

    def body(*refs):
        ins, outs = refs[:n], refs[n:2 * n]
        send_sems, recv_sems, local_sem = refs[2 * n:]
        x, y, c = _mesh_pos()
        me, sibling = (x, y, c), (x, y, 1 - c)
        chips = [(1 - x, y), (x, 1 - y), (1 - x, 1 - y)]

        def slot(a, pos, p):
            return outs[a].at[4 * pos[0] + 2 * pos[1] + pos[2], p]

        def copy(a, p, k, block, to, own=False):
            return pltpu.make_async_remote_copy(
                src_ref=ins[a].at[p] if own else slot(a, block, p), dst_ref=slot(a, block, p),
                send_sem=send_sems.at[a, p, k], recv_sem=recv_sems.at[a, p, k], device_id=to, device_id_type=MESH_ID)

        pieces = [(a, p) for p in range(pmax) for a in range(n) if p < npieces[a]]
        mine = [pltpu.make_async_copy(ins[a].at[p], slot(a, me, p), local_sem.at[a, p]) for a, p in pieces]
        for cp in mine:
            cp.start()
        first = []
        for a, p in pieces:
            first += [copy(a, p, 1 + j, me, (*chip, c), own=True) for j, chip in enumerate(chips)]
            first.append(copy(a, p, 0, me, sibling, own=True))
        for cp in first:
            cp.start()
        passed = []
        for a, p in pieces:
            for j, chip in enumerate(chips):
                copy(a, p, 1 + j, (*chip, c), me).wait_recv()
                cp = copy(a, p, 4 + j, (*chip, c), sibling)
                cp.start()
                passed.append(cp)
        for a, p in pieces:
            copy(a, p, 0, sibling, me).wait_recv()
            for j, chip in enumerate(chips):
                copy(a, p, 4 + j, (*chip, 1 - c), me).wait_recv()
        for cp in first + passed:
            cp.wait_send()
        for cp in mine:
            cp.wait()

    any_spec = pl.BlockSpec(memory_space=pl.ANY)
    return _call(
        body, name=name,
        out_shape=[jax.ShapeDtypeStruct((N_DEV,) + a.shape, a.dtype) for a in arrays],
        in_specs=[any_spec] * n, out_specs=[any_spec] * n,
        scratch_shapes=[pltpu.SemaphoreType.DMA((n, pmax, 7)), pltpu.SemaphoreType.DMA((n, pmax, 7)),
                        pltpu.SemaphoreType.DMA((n, pmax))],
    )(*arrays)


def _swap_cores(arrays, name):
    n = len(arrays)
    pmax = max(a.shape[1] for a in arrays)

    def body(*refs):
        ins, got = refs[:n], refs[n:2 * n]
        send_sems, recv_sems = refs[2 * n:]
        x, y, c = _mesh_pos()
        copies = [pltpu.make_async_remote_copy(
            src_ref=ins[a].at[1 - c, p], dst_ref=got[a].at[p], send_sem=send_sems.at[a, p], recv_sem=recv_sems.at[a, p],
            device_id=(x, y, 1 - c), device_id_type=MESH_ID) for a in range(n) for p in range(arrays[a].shape[1])]
        for cp in copies:
            cp.start()
        for cp in copies:
            cp.wait()

    any_spec = pl.BlockSpec(memory_space=pl.ANY)
    return _call(
        body, name=name, out_shape=[jax.ShapeDtypeStruct(a.shape[1:], a.dtype) for a in arrays],
        in_specs=[any_spec] * n, out_specs=[any_spec] * n,
        scratch_shapes=[pltpu.SemaphoreType.DMA((n, pmax)), pltpu.SemaphoreType.DMA((n, pmax))],
    )(*arrays)


class _ChipExchange:
    aliases = {}

    def __init__(self, arrays):
        self.arrays = list(arrays)
        n = len(self.arrays)
        self.out_shape = [jax.ShapeDtypeStruct((3,) + a.shape[1:], a.dtype) for a in self.arrays]
        self.scratch = [pltpu.SemaphoreType.DMA((n, 3)), pltpu.SemaphoreType.DMA((n, 3))]

    def _copies(self, ins, outs, send_sems, recv_sems):
        x, y, c = _mesh_pos()
        copies = []
        for a in range(len(self.arrays)):
            for k in range(1, 4):
                px, py = x ^ (k >> 1), y ^ (k & 1)
                copies.append(pltpu.make_async_remote_copy(
                    src_ref=ins[a].at[2 * px + py], dst_ref=outs[a].at[k - 1], send_sem=send_sems.at[a, k - 1],
                    recv_sem=recv_sems.at[a, k - 1], device_id=(px, py, c), device_id_type=MESH_ID))
        return copies

    def start(self, ins, outs, send_sems, recv_sems):
        for cp in self._copies(ins, outs, send_sems, recv_sems):
            cp.start()

    def finish(self, ins, outs, send_sems, recv_sems):
        copies = self._copies(ins, outs, send_sems, recv_sems)
        for cp in copies:
            cp.wait_recv()
        for cp in copies:
            cp.wait_send()


def _slot(pos):
    return 4 * pos[0] + 2 * pos[1] + pos[2]


class _GatherSend:
    aliases = {}

    def __init__(self, arrays):
        self.arrays = list(arrays)
        n, pmax = len(self.arrays), max(a.shape[0] for a in self.arrays)
        self.out_shape = [jax.ShapeDtypeStruct((N_DEV,) + a.shape, a.dtype) for a in self.arrays]
        self.scratch = [pltpu.SemaphoreType.DMA((n, pmax, 4)), pltpu.SemaphoreType.DMA((n, pmax, 4)),
                        pltpu.SemaphoreType.DMA((n, pmax))]

    def _copies(self, ins, outs, send_sems, recv_sems, local_sems):
        x, y, c = _mesh_pos()
        peers = [(x, y, 1 - c), (1 - x, y, c), (x, 1 - y, c), (1 - x, 1 - y, c)]
        local, remote = [], []
        for a, arr in enumerate(self.arrays):
            for p in range(arr.shape[0]):
                local.append(pltpu.make_async_copy(ins[a].at[p], outs[a].at[_slot((x, y, c)), p], local_sems.at[a, p]))
                for k, peer in enumerate(peers):
                    remote.append(pltpu.make_async_remote_copy(
                        src_ref=ins[a].at[p], dst_ref=outs[a].at[_slot((x, y, c)), p], send_sem=send_sems.at[a, p, k],
                        recv_sem=recv_sems.at[a, p, k], device_id=peer, device_id_type=MESH_ID))
        return local, remote

    def start(self, *refs):
        local, remote = self._copies(*refs)
        for cp in local + remote:
            cp.start()

    def finish(self, *refs):
        local, remote = self._copies(*refs)
        for cp in remote:
            cp.wait_recv()
        for cp in remote:
            cp.wait_send()
        for cp in local:
            cp.wait()


class _GatherForward:
    def __init__(self, gathered):
        self.arrays = list(gathered)
        n, pmax = len(self.arrays), max(a.shape[1] for a in self.arrays)
        self.out_shape = [jax.ShapeDtypeStruct(a.shape, a.dtype) for a in self.arrays]
        self.aliases = {k: k for k in range(n)}
        self.scratch = [pltpu.SemaphoreType.DMA((n, pmax, 3)), pltpu.SemaphoreType.DMA((n, pmax, 3))]

    def _copies(self, ins, outs, send_sems, recv_sems):
        x, y, c = _mesh_pos()
        copies = []
        for a, arr in enumerate(self.arrays):
            for p in range(arr.shape[1]):
                for j, chip in enumerate([(1 - x, y), (x, 1 - y), (1 - x, 1 - y)]):
                    copies.append(pltpu.make_async_remote_copy(
                        src_ref=ins[a].at[_slot((*chip, c)), p], dst_ref=outs[a].at[_slot((*chip, c)), p],
                        send_sem=send_sems.at[a, p, j], recv_sem=recv_sems.at[a, p, j], device_id=(x, y, 1 - c),
                        device_id_type=MESH_ID))
        return copies

    def start(self, *refs):
        for cp in self._copies(*refs):
            cp.start()

    def finish(self, *refs):
        copies = self._copies(*refs)
        for cp in copies:
            cp.wait_recv()
        for cp in copies:
            cp.wait_send()


def _index_operand(i):
    return jnp.reshape(i, (1,)).astype(jnp.int32)


def _add_pair(pair, other, core, *, name, rows):
    _, n_slots, n_layers, n_rows, n_cols = pair.shape
    rows = min(rows, n_rows)

    def body(c_ref, a_ref, b_ref, o_ref):
        o_ref[...] = (a_ref[0].astype(F32) + b_ref[...].astype(F32)).astype(o_ref.dtype)

    blk = pl.BlockSpec((1, 1, rows, n_cols), lambda s, l, i, c: (s, l, i, 0))
    grid_spec = pltpu.PrefetchScalarGridSpec(
        num_scalar_prefetch=1, grid=(n_slots, n_layers, n_rows // rows),
        in_specs=[pl.BlockSpec((1, 1, 1, rows, n_cols), lambda s, l, i, c: (c[0], s, l, i, 0)), blk], out_specs=blk)
    return _call(body, name=name, grid_spec=grid_spec, out_shape=jax.ShapeDtypeStruct(other.shape, pair.dtype),
                 compiler_params=_params("parallel", "parallel", "parallel"))(_index_operand(core), pair, other)


def _host(body, comm, *, name, grid, in_specs, out_specs, out_shape, args, semantics):
    if comm is None:
        return _call(body, name=name, grid=grid, in_specs=in_specs, out_specs=out_specs, out_shape=out_shape,
                     compiler_params=_params(*semantics))(*args)
    n_in, n_out, n_comm = len(in_specs), len(out_specs), len(comm.arrays)

    def hosted(*refs):
        ins, outs = refs[:n_in], refs[n_in + n_comm:n_in + n_comm + n_out]
        comm_refs = (refs[n_in:n_in + n_comm], refs[n_in + n_comm + n_out:n_in + 2 * n_comm + n_out],
                     *refs[n_in + 2 * n_comm + n_out:])
        ids = [pl.program_id(d) for d in range(len(grid))]
        first, last = ids[0] == 0, ids[0] == grid[0] - 1
        for d in range(1, len(grid)):
            first, last = first & (ids[d] == 0), last & (ids[d] == grid[d] - 1)

        @pl.when(first)
        def _():
            comm.start(*comm_refs)

        body(*ins, *outs)

        @pl.when(last)
        def _():
            comm.finish(*comm_refs)

    any_spec = pl.BlockSpec(memory_space=pl.ANY)
    outs = _call(hosted, name=name, grid=grid, in_specs=list(in_specs) + [any_spec] * n_comm,
                 out_specs=list(out_specs) + [any_spec] * n_comm, out_shape=list(out_shape) + comm.out_shape,
                 input_output_aliases={n_in + k: n_out + v for k, v in comm.aliases.items()},
                 scratch_shapes=comm.scratch, compiler_params=_params(*(("arbitrary",) * len(grid))))(*args, *comm.arrays)
    return outs[:n_out], outs[n_out:]


def _matmul(a_list, b_list, *, name, tm, tn, b_t=False, out_dtype=F32, add=None, add_scale=1.0, comm=None):
    n = len(a_list)
    m_rows, n_cols = a_list[0].shape[0], b_list[0].shape[0 if b_t else 1]
    tm, tn = min(tm, m_rows), min(tn, n_cols)

    def body(*refs):
        a_refs, b_refs = refs[:n], refs[n:2 * n]
        o_ref = refs[-1]
        acc = None
        for a_ref, b_ref in zip(a_refs, b_refs):
            part = lax.dot_general(a_ref[...].astype(MXU_DTYPE), b_ref[...].astype(MXU_DTYPE),
                                   (((1,), (1 if b_t else 0,)), ((), ())), preferred_element_type=F32)
            acc = part if acc is None else acc + part
        if add is not None:
            acc = acc + add_scale * refs[2 * n][...]
        o_ref[...] = acc.astype(o_ref.dtype)

    in_specs = [pl.BlockSpec((tm, a.shape[1]), lambda i, j: (i, 0)) for a in a_list]
    if b_t:
        in_specs += [pl.BlockSpec((tn, b.shape[1]), lambda i, j: (j, 0)) for b in b_list]
    else:
        in_specs += [pl.BlockSpec((b.shape[0], tn), lambda i, j: (0, j)) for b in b_list]
    args = list(a_list) + list(b_list)
    if add is not None:
        in_specs.append(pl.BlockSpec((tm, tn), lambda i, j: (i, j)))
        args.append(add)
    res = _host(body, comm, name=name, grid=(m_rows // tm, n_cols // tn), in_specs=in_specs,
                out_specs=[pl.BlockSpec((tm, tn), lambda i, j: (i, j))],
                out_shape=[jax.ShapeDtypeStruct((m_rows, n_cols), out_dtype)], args=args,
                semantics=("parallel", "arbitrary"))
    return res[0] if comm is None else (res[0][0], res[1])


def _matmul_tn(a, b, *, name, tm, tn, tk, comm=None):
    k_rows, m_rows = a.shape
    n_cols = b.shape[1]
    tm, tn, tk = min(tm, m_rows), min(tn, n_cols), min(tk, k_rows)
    nk = k_rows // tk

    def body(a_ref, b_ref, o_ref):
        @pl.when(pl.program_id(2) == 0)
        def _():
            o_ref[...] = jnp.zeros_like(o_ref)

        o_ref[...] += lax.dot_general(a_ref[...].astype(MXU_DTYPE), b_ref[...].astype(MXU_DTYPE),
                                      (((0,), (0,)), ((), ())), preferred_element_type=F32)

    res = _host(body, comm, name=name, grid=(m_rows // tm, n_cols // tn, nk),
                in_specs=[pl.BlockSpec((tk, tm), lambda i, j, k: (k, i)), pl.BlockSpec((tk, tn), lambda i, j, k: (k, j))],
                out_specs=[pl.BlockSpec((tm, tn), lambda i, j, k: (i, j))],
                out_shape=[jax.ShapeDtypeStruct((m_rows, n_cols), F32)], args=[a, b],
                semantics=("parallel", "parallel", "arbitrary"))
    return res[0] if comm is None else (res[0][0], res[1])


def _outproj_ln(ya, yr, yg, w_out, x, ln_g, ln_b, *, name, comm=None):
    s_len = x.shape[0]
    tm = min(256, s_len)

    def body(ya_ref, yr_ref, yg_ref, w_ref, x_ref, g_ref, b_ref, z_ref, o_ref, lo_ref):
        acc = jnp.dot(ya_ref[...], w_ref[0:A_WIDTH, :], preferred_element_type=F32)
        acc += jnp.dot(yr_ref[...], w_ref[A_WIDTH:A_WIDTH + R_WIDTH, :], preferred_element_type=F32)
        acc += jnp.dot(yg_ref[...], w_ref[A_WIDTH + R_WIDTH:, :], preferred_element_type=F32)
        z = DEEPNORM_ALPHA * x_ref[...] + acc
        z_ref[...] = z
        mu = jnp.mean(z, axis=-1, keepdims=True)
        zc = z - mu
        var = jnp.mean(zc * zc, axis=-1, keepdims=True)
        out = zc * lax.rsqrt(var + LN_EPS) * g_ref[...] + b_ref[...]
        o_ref[...] = out
        lo_ref[...] = out.astype(lo_ref.dtype)

    def rows(width):
        return pl.BlockSpec((tm, width), lambda i: (i, 0))

    def whole(shape):
        return pl.BlockSpec(shape, lambda i: (0, 0))

    return _host(
        body, comm, name=name, grid=(s_len // tm,),
        in_specs=[rows(A_WIDTH), rows(R_WIDTH), rows(G_WIDTH), whole((D_MODEL, D_MODEL)), rows(D_MODEL),
                  whole((1, D_MODEL)), whole((1, D_MODEL))],
        out_specs=[rows(D_MODEL)] * 3,
        out_shape=[jax.ShapeDtypeStruct((s_len, D_MODEL), F32)] * 2 + [jax.ShapeDtypeStruct((s_len, D_MODEL), MXU_DTYPE)],
        args=[ya, yr, yg, w_out, x, ln_g, ln_b], semantics=("parallel",))


def _ln_bwd(z, ln_g, *, name, dxn=None, xn=None, target=None):
    s_len = z.shape[0]
    tm = min(256, s_len)
    top = dxn is None

    def body(*refs):
        if top:
            z_ref, g_ref, xn_ref, t_ref, dz_ref, lo_ref, dg_ref, db_ref, loss_ref = refs
            err = xn_ref[...] - t_ref[...]
            dy = err * (1.0 / D_MODEL)
        else:
            z_ref, g_ref, dy_ref, dz_ref, lo_ref, dg_ref, db_ref = refs
            dy = dy_ref[...]
        first = pl.program_id(0) == 0

        @pl.when(first)
        def _():
            dg_ref[...] = jnp.zeros_like(dg_ref)
            db_ref[...] = jnp.zeros_like(db_ref)
            if top:
                loss_ref[...] = jnp.zeros_like(loss_ref)

        z = z_ref[...]
        mu = jnp.mean(z, axis=-1, keepdims=True)
        zc = z - mu
        rstd = lax.rsqrt(jnp.mean(zc * zc, axis=-1, keepdims=True) + LN_EPS)
        xhat = zc * rstd
        dxh = dy * g_ref[...]
        dz = rstd * (dxh - jnp.mean(dxh, axis=-1, keepdims=True) - xhat * jnp.mean(dxh * xhat, axis=-1, keepdims=True))
        dz_ref[...] = dz
        lo_ref[...] = dz.astype(lo_ref.dtype)
        dg_ref[...] += jnp.sum(dy * xhat, axis=0, keepdims=True)
        db_ref[...] += jnp.sum(dy, axis=0, keepdims=True)
        if top:
            per_row = jnp.sum(err * err, axis=-1, keepdims=True) * (0.5 / D_MODEL)
            loss_ref[...] += jnp.sum(per_row, axis=0, keepdims=True)

    rows = pl.BlockSpec((tm, D_MODEL), lambda i: (i, 0))
    vec = pl.BlockSpec((1, D_MODEL), lambda i: (0, 0))
    in_specs = [rows, vec] + ([rows, rows] if top else [rows])
    args = [z, ln_g] + ([xn, target] if top else [dxn])
    out_specs = [rows, rows, vec, vec]
    out_shape = [jax.ShapeDtypeStruct((s_len, D_MODEL), F32), jax.ShapeDtypeStruct((s_len, D_MODEL), MXU_DTYPE),
                 jax.ShapeDtypeStruct((1, D_MODEL), F32), jax.ShapeDtypeStruct((1, D_MODEL), F32)]
    if top:
        out_specs.append(pl.BlockSpec((1, 1), lambda i: (0, 0)))
        out_shape.append(jax.ShapeDtypeStruct((1, 1), F32))
    return _call(body, name=name, grid=(s_len // tm,), in_specs=in_specs, out_specs=out_specs,
                 out_shape=out_shape, compiler_params=_params("arbitrary"))(*args)


CONV_ROWS = 512
HALO = 8


def _shift_down(x, halo, s):
    if s == 0:
        return x
    ext = jnp.concatenate([halo, x], axis=0)
    return pltpu.roll(ext, s, 0)[HALO:, :]


def _shift_up(x, halo, s):
    if s == 0:
        return x
    ext = jnp.concatenate([x, halo], axis=0)
    return pltpu.roll(ext, ext.shape[0] - s, 0)[:x.shape[0], :]


def _conv_fwd(src, width, w, bias, *, name):
    s_len = src.shape[0]
    rows = min(CONV_ROWS, s_len)
    per = rows // HALO

    def body(x_ref, halo_ref, w_ref, b_ref, o_ref):
        x = x_ref[...]
        halo = jnp.where(pl.program_id(0) == 0, 0.0, halo_ref[...])
        acc = x * w_ref[3:4, :] + b_ref[...]
        for k in range(CONV_WIDTH - 1):
            acc += _shift_down(x, halo, 3 - k) * w_ref[k:k + 1, :]
        o_ref[...] = acc

    return _call(
        body, name=name, grid=(s_len // rows,),
        in_specs=[pl.BlockSpec((rows, width), lambda i: (i, 0)),
                  pl.BlockSpec((HALO, width), lambda i: (jnp.maximum(i * per - 1, 0), 0)),
                  pl.BlockSpec((CONV_WIDTH, width), lambda i: (0, 0)), pl.BlockSpec((1, width), lambda i: (0, 0))],
        out_specs=pl.BlockSpec((rows, width), lambda i: (i, 0)),
        out_shape=jax.ShapeDtypeStruct((s_len, width), F32),
        compiler_params=_params("parallel"),
    )(src, src, w, bias)


def _conv_bwd(dy, src, width, w, passthrough, *, name):
    s_len = src.shape[0]
    rows = min(CONV_ROWS, s_len)
    per = rows // HALO
    nblk = s_len // rows
    extra = [p.shape[1] for p in passthrough]
    total = width + sum(extra)

    def body(*refs):
        dy_ref, dyh_ref, x_ref, xh_ref, w_ref = refs[:5]
        p_refs = refs[5:5 + len(extra)]
        o_ref, dw_ref, db_ref = refs[5 + len(extra):]
        i = pl.program_id(0)

        @pl.when(i == 0)
        def _():
            dw_ref[...] = jnp.zeros_like(dw_ref)
            db_ref[...] = jnp.zeros_like(db_ref)

        dy = dy_ref[...]
        x = x_ref[...]
        dy_halo = jnp.where(i == nblk - 1, 0.0, dyh_ref[...])
        x_halo = jnp.where(i == 0, 0.0, xh_ref[...])
        dx = dy * w_ref[3:4, :]
        dw_ref[3] += jnp.sum(dy * x, axis=0, keepdims=True)
        for k in range(CONV_WIDTH - 1):
            dx += _shift_up(dy, dy_halo, 3 - k) * w_ref[k:k + 1, :]
            dw_ref[k] += jnp.sum(dy * _shift_down(x, x_halo, 3 - k), axis=0, keepdims=True)
        db_ref[...] += jnp.sum(dy, axis=0, keepdims=True)
        o_ref[:, 0:width] = dx.astype(o_ref.dtype)
        off = width
        for p_ref, wd in zip(p_refs, extra):
            o_ref[:, off:off + wd] = p_ref[...].astype(o_ref.dtype)
            off += wd

    in_specs = [pl.BlockSpec((rows, width), lambda i: (i, 0)),
                pl.BlockSpec((HALO, width), lambda i: (jnp.minimum((i + 1) * per, nblk * per - 1), 0)),
                pl.BlockSpec((rows, width), lambda i: (i, 0)),
                pl.BlockSpec((HALO, width), lambda i: (jnp.maximum(i * per - 1, 0), 0)),
                pl.BlockSpec((CONV_WIDTH, width), lambda i: (0, 0))]
    in_specs += [pl.BlockSpec((rows, wd), lambda i: (i, 0)) for wd in extra]
    return _call(
        body, name=name, grid=(nblk,), in_specs=in_specs,
        out_specs=[pl.BlockSpec((rows, total), lambda i: (i, 0)),
                   pl.BlockSpec((CONV_WIDTH, 1, width), lambda i: (0, 0, 0)), pl.BlockSpec((1, width), lambda i: (0, 0))],
        out_shape=[jax.ShapeDtypeStruct((s_len, total), MXU_DTYPE), jax.ShapeDtypeStruct((CONV_WIDTH, 1, width), F32),
                   jax.ShapeDtypeStruct((1, width), F32)],
        compiler_params=_params("arbitrary"),
    )(dy, dy, src, src, w, *passthrough)


def _attn_mask(first):
    i = _iota((A_BLOCK, 2 * A_BLOCK), 0)
    j = _iota((A_BLOCK, 2 * A_BLOCK), 1)
    band = (j > i) & (j <= i + A_BLOCK)
    return band & ((j >= A_BLOCK) | jnp.logical_not(first))


def _attn_group(p, mask, qg, kw, kws, vw, vws, azg, sink0, sink1):
    low = _iota(qg.shape, 1) < A_HEAD_DIM
    first_lane = (_iota((A_BLOCK, LANES), 1) == 0).astype(F32)
    out = None
    for half, sink in ((0, sink0), (1, sink1)):
        kv_head = (2 * p + half) // (A_HEADS // A_KV_HEADS)
        keep = low if half == 0 else jnp.logical_not(low)
        qm = jnp.where(keep, qg, 0.0)
        kk, vv = (kw, vw) if kv_head == half else (kws, vws)
        s = mm_nt(qm, kk) * (A_HEAD_DIM ** -0.5)
        s = jnp.where(mask, s, NEG)
        sk = jnp.sum(jnp.tile(sink, (A_BLOCK // 8, 1)) * first_lane, axis=1, keepdims=True)
        m = lax.stop_gradient(jnp.maximum(jnp.max(s, axis=1, keepdims=True), sk))
        e = jnp.exp(s - m)
        denom = jnp.sum(e, axis=1, keepdims=True) + jnp.exp(sk - m)
        o = mm_nn(e * (1.0 / denom), vv)
        o = jnp.where(keep, o, 0.0)
        out = o if out is None else out + o
    return out * _silu(azg)


def _attn_specs(s_len, rev):
    nb = s_len // A_BLOCK

    def cur(i):
        return nb - 1 - i if rev else i

    def prev(i):
        return jnp.maximum(cur(i) - 1, 0)

    def blk(width, col, which):
        return pl.BlockSpec((A_BLOCK, width), lambda i: (which(i), col))

    return [blk(A_WIDTH, 0, cur), blk(A_WIDTH, 1, cur), blk(LANES, 8, cur), blk(LANES, 9, cur),
            blk(LANES, 8, prev), blk(LANES, 9, prev), blk(LANES, 0, cur), blk(LANES, 0, cur),
            blk(LANES, 0, prev), blk(LANES, 0, prev)], cur


def _attn_fwd(proj_a, cos, sin, sinks_t, *, name):
    s_len = proj_a.shape[0]
    specs, _ = _attn_specs(s_len, False)

    def body(q_ref, az_ref, k_ref, v_ref, kp_ref, vp_ref, c_ref, s_ref, cp_ref, sp_ref, sink_ref, o_ref):
        first = pl.program_id(0) == 0
        mask = _attn_mask(first)
        qr = _rope(q_ref[...], c_ref[...], s_ref[...])
        kw = jnp.concatenate([_rope(kp_ref[...], cp_ref[...], sp_ref[...]), _rope(k_ref[...], c_ref[...], s_ref[...])], 0)
        vw = jnp.concatenate([vp_ref[...], v_ref[...]], 0)
        kws, vws = _swap64(kw), _swap64(vw)
        for p in range(A_WIDTH // LANES):
            cols = slice(p * LANES, (p + 1) * LANES)
            o = _attn_group(p, mask, qr[:, cols], kw, kws, vw, vws, az_ref[:, cols], sink_ref[2 * p], sink_ref[2 * p + 1])
            o_ref[:, cols] = o.astype(o_ref.dtype)

    return _call(
        body, name=name, grid=(s_len // A_BLOCK,),
        in_specs=specs + [pl.BlockSpec((A_HEADS, 8, LANES), lambda i: (0, 0, 0))],
        out_specs=pl.BlockSpec((A_BLOCK, A_WIDTH), lambda i: (i, 0)),
        out_shape=jax.ShapeDtypeStruct((s_len, A_WIDTH), MXU_DTYPE),
        compiler_params=_params("parallel"),
    )(proj_a, proj_a, proj_a, proj_a, proj_a, proj_a, cos, sin, cos, sin, sinks_t)


def _attn_bwd(proj_a, cos, sin, sinks_t, dya, *, name):
    s_len = proj_a.shape[0]
    specs, cur = _attn_specs(s_len, True)

    def body(q_ref, az_ref, k_ref, v_ref, kp_ref, vp_ref, c_ref, s_ref, cp_ref, sp_ref, sink_ref, dy_ref,
             o_ref, dsink_ref, dk_carry, dv_carry):
        i = pl.program_id(0)

        @pl.when(i == 0)
        def _():
            dsink_ref[...] = jnp.zeros_like(dsink_ref)
            dk_carry[...] = jnp.zeros_like(dk_carry)
            dv_carry[...] = jnp.zeros_like(dv_carry)

        first = cur(i) == 0
        mask = _attn_mask(first)
        cos_c, sin_c = c_ref[...], s_ref[...]
        qr = _rope(q_ref[...], cos_c, sin_c)
        kw = jnp.concatenate([_rope(kp_ref[...], cp_ref[...], sp_ref[...]), _rope(k_ref[...], cos_c, sin_c)], 0)
        vw = jnp.concatenate([vp_ref[...], v_ref[...]], 0)
        kws, vws = _swap64(kw), _swap64(vw)
        dkw = jnp.zeros_like(kw)
        dvw = jnp.zeros_like(vw)
        for p in range(A_WIDTH // LANES):
            cols = slice(p * LANES, (p + 1) * LANES)
            _, vjp = jax.vjp(functools.partial(_attn_group, p, mask), qr[:, cols], kw, kws, vw, vws, az_ref[:, cols],
                             sink_ref[2 * p], sink_ref[2 * p + 1])
            dq, dk1, dk2, dv1, dv2, daz, ds0, ds1 = vjp(dy_ref[:, cols])
            dkw += dk1 + _swap64(dk2)
            dvw += dv1 + _swap64(dv2)
            o_ref[:, cols] = _rope_t(dq, cos_c, sin_c).astype(o_ref.dtype)
            o_ref[:, A_WIDTH + p * LANES:A_WIDTH + (p + 1) * LANES] = daz.astype(o_ref.dtype)
            dsink_ref[2 * p] += ds0
            dsink_ref[2 * p + 1] += ds1
        o_ref[:, 2 * A_WIDTH:2 * A_WIDTH + LANES] = _rope_t(dkw[A_BLOCK:, :] + dk_carry[...], cos_c, sin_c).astype(o_ref.dtype)
        o_ref[:, 2 * A_WIDTH + LANES:] = (dvw[A_BLOCK:, :] + dv_carry[...]).astype(o_ref.dtype)
        dk_carry[...] = dkw[:A_BLOCK, :]
        dv_carry[...] = dvw[:A_BLOCK, :]

    return _call(
        body, name=name, grid=(s_len // A_BLOCK,),
        in_specs=specs + [pl.BlockSpec((A_HEADS, 8, LANES), lambda i: (0, 0, 0)),
                          pl.BlockSpec((A_BLOCK, A_WIDTH), lambda i: (cur(i), 0))],
        out_specs=[pl.BlockSpec((A_BLOCK, WA), lambda i: (cur(i), 0)),
                   pl.BlockSpec((A_HEADS, 8, LANES), lambda i: (0, 0, 0))],
        out_shape=[jax.ShapeDtypeStruct((s_len, WA), MXU_DTYPE), jax.ShapeDtypeStruct((A_HEADS, 8, LANES), F32)],
        scratch_shapes=[pltpu.VMEM((A_BLOCK, LANES), F32), pltpu.VMEM((A_BLOCK, LANES), F32)],
        compiler_params=_params("arbitrary"),
    )(proj_a, proj_a, proj_a, proj_a, proj_a, proj_a, cos, sin, cos, sin, sinks_t, dya)


RG_ROWS = 256


def _rg_gates(x, wa, ba, wx, bx, lam):
    r = jax.nn.sigmoid(mm_nn(x, wa) + ba)
    ig = jax.nn.sigmoid(mm_nn(x, wx) + bx)
    log_a = -R_C * r * _softplus(-lam)
    a = jnp.exp(log_a)
    return a, jnp.sqrt(_one_minus_sq(log_a, a)) * (ig * x)


def _rg_param_specs():
    mat = pl.BlockSpec((R_BLOCKS, R_BLOCK_DIM, R_BLOCK_DIM), lambda i: (0, 0, 0))
    vec = pl.BlockSpec((1, R_WIDTH), lambda i: (0, 0))
    return [mat, vec, mat, vec, vec]


def _rg_fwd(xr, proj_r, wa, ba, wx, bx, lam, *, name):
    s_len = xr.shape[0]
    rows = min(RG_ROWS, s_len)

    def body(x_ref, z_ref, wa_ref, ba_ref, wx_ref, bx_ref, lam_ref, h_ref, y_ref, a_buf, u_buf, carry):
        @pl.when(pl.program_id(0) == 0)
        def _():
            carry[...] = jnp.zeros_like(carry)

        for n in range(R_BLOCKS):
            cols = slice(n * R_BLOCK_DIM, (n + 1) * R_BLOCK_DIM)
            a, u = _rg_gates(x_ref[:, cols], wa_ref[n], ba_ref[:, cols], wx_ref[n], bx_ref[:, cols], lam_ref[:, cols])
            a_buf[:, cols] = a
            u_buf[:, cols] = u

        def step(t, h):
            h = a_buf[pl.ds(t, 1), :] * h + u_buf[pl.ds(t, 1), :]
            h_ref[pl.ds(t, 1), :] = h
            return h

        carry[...] = lax.fori_loop(0, rows, step, carry[...], unroll=16)
        y_ref[...] = (h_ref[...] * _silu(z_ref[...])).astype(y_ref.dtype)

    blk = pl.BlockSpec((rows, R_WIDTH), lambda i: (i, 0))
    return _call(
        body, name=name, grid=(s_len // rows,),
        in_specs=[blk, pl.BlockSpec((rows, R_WIDTH), lambda i: (i, 1))] + _rg_param_specs(),
        out_specs=[blk, blk],
        out_shape=[jax.ShapeDtypeStruct((s_len, R_WIDTH), F32), jax.ShapeDtypeStruct((s_len, R_WIDTH), MXU_DTYPE)],
        scratch_shapes=[pltpu.VMEM((rows, R_WIDTH), F32), pltpu.VMEM((rows, R_WIDTH), F32), pltpu.VMEM((1, R_WIDTH), F32)],
        compiler_params=_params("arbitrary"),
    )(xr, proj_r, wa, ba, wx, bx, lam)


def _rg_bwd(xr, proj_r, h, dyr, wa, ba, wx, bx, lam, *, name):
    s_len = xr.shape[0]
    rows = min(RG_ROWS, s_len)
    nblk = s_len // rows
    per = rows // HALO

    def cur(i):
        return nblk - 1 - i

    def body(x_ref, z_ref, h_ref, hh_ref, dy_ref, wa_ref, ba_ref, wx_ref, bx_ref, lam_ref,
             dx_ref, dz_ref, dwa_ref, dba_ref, dwx_ref, dbx_ref, dlam_ref, a_buf, g_buf, carry):
        i = pl.program_id(0)

        @pl.when(i == 0)
        def _():
            carry[...] = jnp.zeros_like(carry)
            for ref in (dwa_ref, dba_ref, dwx_ref, dbx_ref, dlam_ref):
                ref[...] = jnp.zeros_like(ref)

        z = z_ref[...]
        sig = jax.nn.sigmoid(z)
        hval = h_ref[...]
        dy = dy_ref[...]
        dz_ref[...] = dy * hval * (sig * (1.0 + z * (1.0 - sig)))
        g_buf[...] = dy * (z * sig)
        kept = []
        for n in range(R_BLOCKS):
            cols = slice(n * R_BLOCK_DIM, (n + 1) * R_BLOCK_DIM)
            x = x_ref[:, cols]
            r = jax.nn.sigmoid(_nn(x, wa_ref[n]) + ba_ref[:, cols])
            ig = jax.nn.sigmoid(_nn(x, wx_ref[n]) + bx_ref[:, cols])
            sp = _softplus(-lam_ref[:, cols])
            log_a = -R_C * r * sp
            a = jnp.exp(log_a)
            a_buf[:, cols] = a
            kept.append((x, r, ig, sp, a, jnp.sqrt(_one_minus_sq(log_a, a))))

        def step(k, c):
            t = rows - 1 - k
            g = g_buf[pl.ds(t, 1), :] + c
            g_buf[pl.ds(t, 1), :] = g
            return a_buf[pl.ds(t, 1), :] * g

        carry[...] = lax.fori_loop(0, rows, step, carry[...], unroll=16)
        h_halo = jnp.where(cur(i) == 0, 0.0, hh_ref[...])
        dh = g_buf[...]
        da = dh * _shift_down(hval, h_halo, 1)
        for n in range(R_BLOCKS):
            cols = slice(n * R_BLOCK_DIM, (n + 1) * R_BLOCK_DIM)
            x, r, ig, sp, a, s = kept[n]
            du = dh[:, cols]
            dux = du * x
            d_log_a = a * (da[:, cols] - a * (dux * ig) / s)
            d_ga = d_log_a * (-R_C * sp) * (r * (1.0 - r))
            d_gx = dux * s * (ig * (1.0 - ig))
            dx_ref[:, cols] = du * (s * ig) + _nt(d_ga, wa_ref[n]) + _nt(d_gx, wx_ref[n])
            xt = _t(x)
            dwa_ref[n] += _nn(xt, d_ga)
            dwx_ref[n] += _nn(xt, d_gx)
            dba_ref[:, cols] += jnp.sum(d_ga, axis=0, keepdims=True)
            dbx_ref[:, cols] += jnp.sum(d_gx, axis=0, keepdims=True)
            dlam_ref[:, cols] += jnp.sum(d_log_a * r, axis=0, keepdims=True) * (R_C * jax.nn.sigmoid(-lam_ref[:, cols]))

    blk = pl.BlockSpec((rows, R_WIDTH), lambda i: (cur(i), 0))
    mat = pl.BlockSpec((R_BLOCKS, R_BLOCK_DIM, R_BLOCK_DIM), lambda i: (0, 0, 0))
    vec = pl.BlockSpec((1, R_WIDTH), lambda i: (0, 0))
    return _call(
        body, name=name, grid=(nblk,),
        in_specs=[blk, pl.BlockSpec((rows, R_WIDTH), lambda i: (cur(i), 1)), blk,
                  pl.BlockSpec((HALO, R_WIDTH), lambda i: (jnp.maximum(cur(i) * per - 1, 0), 0)), blk] + _rg_param_specs(),
        out_specs=[blk, blk, mat, vec, mat, vec, vec],
        out_shape=[jax.ShapeDtypeStruct((s_len, R_WIDTH), F32)] * 2 + [
            jax.ShapeDtypeStruct((R_BLOCKS, R_BLOCK_DIM, R_BLOCK_DIM), F32), jax.ShapeDtypeStruct((1, R_WIDTH), F32),
            jax.ShapeDtypeStruct((R_BLOCKS, R_BLOCK_DIM, R_BLOCK_DIM), F32), jax.ShapeDtypeStruct((1, R_WIDTH), F32),
            jax.ShapeDtypeStruct((1, R_WIDTH), F32)],
        scratch_shapes=[pltpu.VMEM((rows, R_WIDTH), F32), pltpu.VMEM((rows, R_WIDTH), F32), pltpu.VMEM((1, R_WIDTH), F32)],
        compiler_params=_params("arbitrary"),
    )(xr, proj_r, h, h, dyr, wa, ba, wx, bx, lam)


GP_CHUNKS = 8
GP_CHUNKS_BWD = 4
GS_CHUNKS = 8


def _seg_cumsum(x, reverse):
    rows = x.shape[0]
    r = _iota(x.shape, 0) & (G_CHUNK - 1)
    s = 1
    while s < G_CHUNK:
        if reverse:
            x = x + jnp.where(r < G_CHUNK - s, pltpu.roll(x, rows - s, 0), 0.0)
        else:
            x = x + jnp.where(r >= s, pltpu.roll(x, s, 0), 0.0)
        s *= 2
    return x


def _gdn_decay(ga, a_log_row, dt_row):
    return -jnp.exp(a_log_row) * _softplus(ga + dt_row)


def _gdn_chunk(cq, ck, cv, gb, gc, inv=None):
    shape = cq.shape
    head = _iota(shape, 0) & (G_HEADS - 1)
    lane = _iota(shape, 2)
    q, k, v = _silu(cq), _silu(ck), _silu(cv)
    q = q * lax.rsqrt(jnp.sum(q * q, axis=-1, keepdims=True) + RMS_EPS) * (G_HEAD_DIM ** -0.5)
    k = k * lax.rsqrt(jnp.sum(k * k, axis=-1, keepdims=True) + RMS_EPS)
    beta = jnp.sum(jnp.where(lane == head, jax.nn.sigmoid(gb), 0.0), axis=-1, keepdims=True)
    g = jnp.sum(jnp.where(lane == head + G_HEADS, gc, 0.0), axis=-1, keepdims=True)
    sq = (shape[0], G_CHUNK, G_CHUNK)
    row, col = _iota(sq, 1), _iota(sq, 2)
    g_sq = jnp.broadcast_to(g, sq)
    decay = jnp.where(row >= col, jnp.exp(jnp.minimum(g_sq - _t(g_sq), 0.0)), 0.0)
    g_last = jnp.sum(jnp.where(_iota(g.shape, 1) == G_CHUNK - 1, g, 0.0), axis=1, keepdims=True)
    eg = jnp.exp(g)
    kb, vb = k * beta, v * beta
    m = jnp.where(row > col, mm_nt(kb, k) * decay, 0.0)
    known = inv is not None
    if not known:
        inv = _inv_unit_lower(m)
    u, w = _solve2(m, inv, vb, kb * eg)
    qk = jnp.where(row >= col, mm_nt(q, k) * decay, 0.0)
    q_dec = q * eg
    k_dec = k * jnp.exp(g_last - g)
    gl = jnp.broadcast_to(jnp.exp(g_last), (shape[0], 1, G_HEAD_DIM))
    return (u, w, qk, q_dec, k_dec, gl) if known else (u, w, qk, q_dec, k_dec, gl, inv)


def _gdn_step(state, u, w, qk, q_dec, k_dec, gl, gz, norm_w):
    v_new = u - mm_nn(w, state)
    o = mm_nn(q_dec, state) + mm_nn(qk, v_new)
    new_state = state * gl + mm_nn(_t(k_dec), v_new)
    o = o * lax.rsqrt(jnp.mean(o * o, axis=-1, keepdims=True) + RMS_EPS) * norm_w
    return o * _silu(gz), new_state


def _stack_chunks(x, heads):
    chunks = x.shape[0] // G_CHUNK
    parts = []
    for c in range(chunks):
        rows = slice(c * G_CHUNK, (c + 1) * G_CHUNK)
        for hd in range(G_HEADS):
            parts.append(x[rows, hd * LANES:(hd + 1) * LANES] if heads else x[rows, :])
    return jnp.stack(parts)


def _gdn_chunk_shapes(nch):
    b = nch * G_HEADS
    wide = jax.ShapeDtypeStruct((b, G_CHUNK, G_HEAD_DIM), F32)
    return [wide, wide, jax.ShapeDtypeStruct((b, G_CHUNK, G_CHUNK), F32), wide, wide,
            jax.ShapeDtypeStruct((b, 1, G_HEAD_DIM), F32)]


def _gdn_chunk_specs(nbatch):
    wide = pl.BlockSpec((nbatch, G_CHUNK, G_HEAD_DIM), lambda i: (i, 0, 0))
    return [wide, wide, pl.BlockSpec((nbatch, G_CHUNK, G_CHUNK), lambda i: (i, 0, 0)), wide, wide,
            pl.BlockSpec((nbatch, 1, G_HEAD_DIM), lambda i: (i, 0, 0))]


def _gdn_chunk_fwd(conv, proj_g, a_log_row, dt_row, *, name):
    s_len = conv.shape[0]
    cpg = min(GP_CHUNKS, s_len // G_CHUNK)
    rows = cpg * G_CHUNK
    nbatch = cpg * G_HEADS

    def body(c_ref, bg_ref, al_ref, dt_ref, *outs):
        bg = bg_ref[...]
        gc = _seg_cumsum(_gdn_decay(bg, al_ref[...], dt_ref[...]), False)
        res = _gdn_chunk(_stack_chunks(c_ref[:, 0:G_WIDTH], True), _stack_chunks(c_ref[:, G_WIDTH:2 * G_WIDTH], True),
                         _stack_chunks(c_ref[:, 2 * G_WIDTH:], True), _stack_chunks(bg, False), _stack_chunks(gc, False))
        for ref, val in zip(outs, res):
            ref[...] = val

    row = pl.BlockSpec((1, LANES), lambda i: (0, 0))
    return _call(
        body, name=name, grid=(s_len // rows,),
        in_specs=[pl.BlockSpec((rows, 3 * G_WIDTH), lambda i: (i, 0)),
                  pl.BlockSpec((rows, LANES), lambda i: (i, (3 * G_WIDTH + G_WIDTH) // LANES)), row, row],
        out_specs=_gdn_chunk_specs(nbatch) + [pl.BlockSpec((nbatch, G_CHUNK, G_CHUNK), lambda i: (i, 0, 0))],
        out_shape=_gdn_chunk_shapes(s_len // G_CHUNK) + [
            jax.ShapeDtypeStruct((s_len // G_CHUNK * G_HEADS, G_CHUNK, G_CHUNK), F32)],
        compiler_params=_params("parallel"),
    )(conv, proj_g, a_log_row, dt_row)


def _gdn_chunk_bwd(conv, proj_g, a_log_row, dt_row, inv, cots, *, name):
    s_len = conv.shape[0]
    cpg = min(GP_CHUNKS_BWD, s_len // G_CHUNK)
    rows = cpg * G_CHUNK
    nbatch = cpg * G_HEADS

    def unstack(x, heads):
        if heads:
            return jnp.concatenate([jnp.concatenate([x[c * G_HEADS + hd] for hd in range(G_HEADS)], axis=1)
                                    for c in range(cpg)], axis=0)
        return jnp.concatenate([sum(x[c * G_HEADS + hd] for hd in range(G_HEADS)) for c in range(cpg)], axis=0)

    def body(c_ref, bg_ref, al_ref, dt_ref, inv_ref, du, dw, dqk, dqd, dkd, dgl, dc_ref, dbg_ref, dal_ref, ddt_ref):
        @pl.when(pl.program_id(0) == 0)
        def _():
            dal_ref[...] = jnp.zeros_like(dal_ref)
            ddt_ref[...] = jnp.zeros_like(ddt_ref)

        bg = bg_ref[...]
        g_all, decay_vjp = jax.vjp(_gdn_decay, bg, al_ref[...], dt_ref[...])
        gc = _seg_cumsum(g_all, False)
        _, vjp = jax.vjp(_gdn_chunk, _stack_chunks(c_ref[:, 0:G_WIDTH], True),
                         _stack_chunks(c_ref[:, G_WIDTH:2 * G_WIDTH], True), _stack_chunks(c_ref[:, 2 * G_WIDTH:], True),
                         _stack_chunks(bg, False), _stack_chunks(gc, False), inv_ref[...])
        dq, dk, dv, dgb, dgc, _ = vjp((du[...], dw[...], dqk[...], dqd[...], dkd[...], dgl[...]))
        dc_ref[:, 0:G_WIDTH] = unstack(dq, True)
        dc_ref[:, G_WIDTH:2 * G_WIDTH] = unstack(dk, True)
        dc_ref[:, 2 * G_WIDTH:] = unstack(dv, True)
        dga, dal, ddt = decay_vjp(_seg_cumsum(unstack(dgc, False), True))
        dbg_ref[:, 0:LANES] = unstack(dgb, False) + dga
        dbg_ref[:, LANES:] = jnp.zeros((rows, LANES), F32)
        dal_ref[...] += dal
        ddt_ref[...] += ddt

    row = pl.BlockSpec((1, LANES), lambda i: (0, 0))
    return _call(
        body, name=name, grid=(s_len // rows,),
        in_specs=[pl.BlockSpec((rows, 3 * G_WIDTH), lambda i: (i, 0)),
                  pl.BlockSpec((rows, LANES), lambda i: (i, (3 * G_WIDTH + G_WIDTH) // LANES)), row, row,
                  pl.BlockSpec((nbatch, G_CHUNK, G_CHUNK), lambda i: (i, 0, 0))]
        + _gdn_chunk_specs(nbatch),
        out_specs=[pl.BlockSpec((rows, 3 * G_WIDTH), lambda i: (i, 0)), pl.BlockSpec((rows, 2 * LANES), lambda i: (i, 0)),
                   row, row],
        out_shape=[jax.ShapeDtypeStruct((s_len, 3 * G_WIDTH), F32), jax.ShapeDtypeStruct((s_len, 2 * LANES), F32),
                   jax.ShapeDtypeStruct((1, LANES), F32), jax.ShapeDtypeStruct((1, LANES), F32)],
        compiler_params=_params("arbitrary"),
    )(conv, proj_g, a_log_row, dt_row, inv, *cots)


def _gdn_scan_specs(cpg, which):
    nbatch = cpg * G_HEADS
    wide = pl.BlockSpec((nbatch, G_CHUNK, G_HEAD_DIM), lambda i: (which(i), 0, 0))
    return [wide, wide, pl.BlockSpec((nbatch, G_CHUNK, G_CHUNK), lambda i: (which(i), 0, 0)), wide, wide,
            pl.BlockSpec((nbatch, 1, G_HEAD_DIM), lambda i: (which(i), 0, 0))]


def _gz_stack(z_ref, c):
    rows = pl.ds(pl.multiple_of(c * G_CHUNK, G_CHUNK), G_CHUNK)
    return jnp.stack([z_ref[rows, hd * LANES:(hd + 1) * LANES] for hd in range(G_HEADS)])


def _gdn_scan_fwd(chunk_vals, proj_g, norm_w, *, name):
    s_len = proj_g.shape[0]
    nch = s_len // G_CHUNK
    cpg = min(GS_CHUNKS, nch)
    rows = cpg * G_CHUNK

    def body(u_ref, w_ref, qk_ref, qd_ref, kd_ref, gl_ref, z_ref, nw_ref, y_ref, st_ref, state):
        @pl.when(pl.program_id(0) == 0)
        def _():
            state[...] = jnp.zeros_like(state)

        def step(c, carry):
            b = pl.ds(pl.multiple_of(c * G_HEADS, G_HEADS), G_HEADS)
            st = state[...]
            st_ref[b] = st
            y, new_state = _gdn_step(st, u_ref[b], w_ref[b], qk_ref[b], qd_ref[b], kd_ref[b], gl_ref[b],
                                     _gz_stack(z_ref, c), nw_ref[...])
            state[...] = new_state
            rws = pl.ds(pl.multiple_of(c * G_CHUNK, G_CHUNK), G_CHUNK)
            for hd in range(G_HEADS):
                y_ref[rws, hd * LANES:(hd + 1) * LANES] = y[hd].astype(y_ref.dtype)
            return carry

        lax.fori_loop(0, cpg, step, 0, unroll=8)

    return _call(
        body, name=name, grid=(nch // cpg,),
        in_specs=_gdn_scan_specs(cpg, lambda i: i) + [
            pl.BlockSpec((rows, G_WIDTH), lambda i: (i, 3)), pl.BlockSpec((1, G_HEAD_DIM), lambda i: (0, 0))],
        out_specs=[pl.BlockSpec((rows, G_WIDTH), lambda i: (i, 0)),
                   pl.BlockSpec((cpg * G_HEADS, G_HEAD_DIM, G_HEAD_DIM), lambda i: (i, 0, 0))],
        out_shape=[jax.ShapeDtypeStruct((s_len, G_WIDTH), MXU_DTYPE),
                   jax.ShapeDtypeStruct((nch * G_HEADS, G_HEAD_DIM, G_HEAD_DIM), F32)],
        scratch_shapes=[pltpu.VMEM((G_HEADS, G_HEAD_DIM, G_HEAD_DIM), F32)],
        compiler_params=_params("arbitrary"),
    )(*chunk_vals, proj_g, norm_w)


def _gdn_scan_bwd(chunk_vals, states, proj_g, norm_w, dyg, *, name):
    s_len = proj_g.shape[0]
    nch = s_len // G_CHUNK
    cpg = min(GS_CHUNKS, nch)
    rows = cpg * G_CHUNK
    ngrid = nch // cpg

    def cur(i):
        return ngrid - 1 - i

    def body(u_ref, w_ref, qk_ref, qd_ref, kd_ref, gl_ref, st_ref, z_ref, nw_ref, dy_ref,
             du_ref, dw_ref, dqk_ref, dqd_ref, dkd_ref, dgl_ref, dz_ref, dnw_ref, dstate):
        @pl.when(pl.program_id(0) == 0)
        def _():
            dstate[...] = jnp.zeros_like(dstate)
            dnw_ref[...] = jnp.zeros_like(dnw_ref)

        def step(k, carry):
            c = cpg - 1 - k
            b = pl.ds(pl.multiple_of(c * G_HEADS, G_HEADS), G_HEADS)
            _, vjp = jax.vjp(_gdn_step, st_ref[b], u_ref[b], w_ref[b], qk_ref[b], qd_ref[b], kd_ref[b], gl_ref[b],
                             _gz_stack(z_ref, c), nw_ref[...])
            dst, du, dw, dqk, dqd, dkd, dgl, dz, dnw = vjp((_gz_stack(dy_ref, c), dstate[...]))
            dstate[...] = dst
            du_ref[b], dw_ref[b], dqk_ref[b], dqd_ref[b], dkd_ref[b], dgl_ref[b] = du, dw, dqk, dqd, dkd, dgl
            rws = pl.ds(pl.multiple_of(c * G_CHUNK, G_CHUNK), G_CHUNK)
            for hd in range(G_HEADS):
                dz_ref[rws, hd * LANES:(hd + 1) * LANES] = dz[hd]
            dnw_ref[...] += dnw
            return carry

        lax.fori_loop(0, cpg, step, 0, unroll=8)

    gate = pl.BlockSpec((rows, G_WIDTH), lambda i: (cur(i), 3))
    wide = pl.BlockSpec((rows, G_WIDTH), lambda i: (cur(i), 0))
    vec = pl.BlockSpec((1, G_HEAD_DIM), lambda i: (0, 0))
    return _call(
        body, name=name, grid=(ngrid,),
        in_specs=_gdn_scan_specs(cpg, cur) + [
            pl.BlockSpec((cpg * G_HEADS, G_HEAD_DIM, G_HEAD_DIM), lambda i: (cur(i), 0, 0)), gate, vec, wide],
        out_specs=_gdn_scan_specs(cpg, cur) + [wide, vec],
        out_shape=_gdn_chunk_shapes(nch) + [jax.ShapeDtypeStruct((s_len, G_WIDTH), F32),
                                            jax.ShapeDtypeStruct((1, G_HEAD_DIM), F32)],
        scratch_shapes=[pltpu.VMEM((G_HEADS, G_HEAD_DIM, G_HEAD_DIM), F32)],
        compiler_params=_params("arbitrary"),
    )(*chunk_vals, states, proj_g, norm_w, dyg)


def _adamw_math(w, g, m, v):
    m = ADAM_B1 * m + (1.0 - ADAM_B1) * g
    v = ADAM_B2 * v + (1.0 - ADAM_B2) * (g * g)
    m_hat = m / (1.0 - ADAM_B1 ** ADAM_STEP)
    v_hat = v / (1.0 - ADAM_B2 ** ADAM_STEP)
    delta = -ADAM_LR * (m_hat / (jnp.sqrt(v_hat) + ADAM_EPS) + ADAM_WD * w)
    return delta, m, v


def _sum_adamw(own, chip, parts, w, m, v, *, name, rows):
    n_layers, n_rows, n_cols = w.shape
    rows = min(rows, n_rows)
    n_parts = parts[0].shape[0]

    def body(c_ref, *refs):
        own_refs, part_refs = refs[:n_layers], refs[n_layers:2 * n_layers]
        w_ref, m_ref, v_ref, g_ref, d_ref, nm_ref, nv_ref = refs[2 * n_layers:]
        layer = pl.program_id(0)
        g = None
        for l in range(n_layers):
            g_l = own_refs[l][0].astype(F32)
            for k in range(n_parts):
                g_l = g_l + part_refs[l][k].astype(F32)
            g = g_l if g is None else jnp.where(layer == l, g_l, g)
        delta, new_m, new_v = _adamw_math(w_ref[0], g, m_ref[0], v_ref[0])
        g_ref[0], d_ref[0], nm_ref[0], nv_ref[0] = g, delta, new_m, new_v

    blk = pl.BlockSpec((1, rows, n_cols), lambda l, i, c: (l, i, 0))
    grid_spec = pltpu.PrefetchScalarGridSpec(
        num_scalar_prefetch=1, grid=(n_layers, n_rows // rows),
        in_specs=[pl.BlockSpec((1, rows, n_cols), lambda l, i, c: (c[0], i, 0))] * n_layers
        + [pl.BlockSpec((n_parts, rows, n_cols), lambda l, i, c: (0, i, 0))] * n_layers + [blk, blk, blk],
        out_specs=[blk] * 4)
    return _call(
        body, name=name, grid_spec=grid_spec, out_shape=[jax.ShapeDtypeStruct(w.shape, F32)] * 4,
        compiler_params=_params("parallel", "parallel"),
    )(_index_operand(chip), *own, *parts, w, m, v)


def _sum_slots(parts, *, name):
    rows = parts.shape[1]

    def body(p_ref, o_ref):
        g = p_ref[0]
        for k in range(1, N_DEV):
            g = g + p_ref[k]
        o_ref[...] = g

    return _call(body, name=name, grid=(1,),
                 in_specs=[pl.BlockSpec(parts.shape, lambda i: (0, 0, 0))],
                 out_specs=pl.BlockSpec((rows, LANES), lambda i: (0, 0)),
                 out_shape=jax.ShapeDtypeStruct((rows, LANES), F32), compiler_params=_params("arbitrary"))(parts)


def _adamw_packed(w, g, m, v, *, name):
    def body(w_ref, g_ref, m_ref, v_ref, d_ref, nm_ref, nv_ref):
        d_ref[...], nm_ref[...], nv_ref[...] = _adamw_math(w_ref[...], g_ref[...], m_ref[...], v_ref[...])

    blk = pl.BlockSpec(w.shape, lambda i: (0, 0))
    return _call(body, name=name, grid=(1,), in_specs=[blk] * 4, out_specs=[blk] * 3,
                 out_shape=[jax.ShapeDtypeStruct(w.shape, F32)] * 3, compiler_params=_params("arbitrary"))(w, g, m, v)


A_COLS = ((0, 512), (768, 1280), (512, 768))
R_COLS = ((1280, 3328),)
G_COLS = ((3328, 5384),)


def _group_weights(wt_full):
    def take(ranges):
        return jnp.concatenate([wt_full[a:b] for a, b in ranges], axis=0)

    wt_g = jnp.concatenate([take(G_COLS), jnp.zeros((G_PAD, wt_full.shape[1]), wt_full.dtype)], axis=0)
    return take(A_COLS), take(R_COLS), wt_g


def _ungroup_grads(d_a, d_r, d_g):
    return jnp.concatenate([d_a[0:512], d_a[1024:1280], d_a[512:1024], d_r, d_g[:WG - G_PAD]], axis=0)


def _shard_rows(w):
    return jnp.pad(jnp.transpose(w, (0, 2, 1)), ((0, 0), (0, N_ROWS_PAD - N_IN_SHARD), (0, 0)))


def _unshard_rows(wt):
    return jnp.transpose(wt[:, :N_IN_SHARD], (0, 2, 1))


def _owner_blocks(dwt):
    blocks = jnp.pad(dwt.reshape(4, 2, N_IN_SHARD, D_MODEL), ((0, 0), (0, 0), (0, N_ROWS_PAD - N_IN_SHARD), (0, 0)))
    return jnp.transpose(blocks, (1, 0, 2, 3))


def _rope_tables(s_len):
    inv = 1.0 / (ROPE_THETA ** (jnp.arange(0, A_HEAD_DIM, 2, dtype=F32) / A_HEAD_DIM))
    ang = jnp.arange(s_len, dtype=F32)[:, None] * inv[None, :]
    cos, sin = jnp.cos(ang), jnp.sin(ang)
    return jnp.tile(cos, (1, 4)), jnp.tile(jnp.concatenate([-sin, sin], axis=1), (1, 2))


SMALL = ("sinks", "r_conv_w", "r_conv_b", "r_wa", "r_ba", "r_wx", "r_bx", "r_lam", "g_conv_w", "g_a_log", "g_dt_bias",
         "g_norm_w", "ln_g", "ln_b")


def _pack(leaves):
    rows = []
    for leaf in leaves:
        flat = leaf.reshape(-1)
        pad = (-flat.shape[0]) % (8 * LANES)
        rows.append(jnp.pad(flat, (0, pad)).reshape(-1, LANES))
    return jnp.concatenate(rows, axis=0)


def _unpack(packed, shapes):
    out, row = [], 0
    for shape in shapes:
        size = math.prod(shape)
        nrows = -(-size // (8 * LANES)) * 8
        out.append(packed[row:row + nrows].reshape(-1)[:size].reshape(shape))
        row += nrows
    return out


def _lane_row(vals, offset):
    return jnp.pad(vals, (offset, LANES - offset - vals.shape[0])).reshape(1, LANES)


def kernel(x, w_in, sinks, r_conv_w, r_conv_b, r_wa, r_ba, r_wx, r_bx, r_lam, g_conv_w, g_a_log, g_dt_bias, g_norm_w, w_out, ln_g, ln_b, loss_target, m_w_in, m_sinks, m_r_conv_w, m_r_conv_b, m_r_wa, m_r_ba, m_r_wx, m_r_bx, m_r_lam, m_g_conv_w, m_g_a_log, m_g_dt_bias, m_g_norm_w, m_w_out, m_ln_g, m_ln_b, v_w_in, v_sinks, v_r_conv_w, v_r_conv_b, v_r_wa, v_r_ba, v_r_wx, v_r_bx, v_r_lam, v_g_conv_w, v_g_a_log, v_g_dt_bias, v_g_norm_w, v_w_out, v_ln_g, v_ln_b):
    s_len = x.shape[1]
    x0 = x.reshape(s_len, D_MODEL)
    target = loss_target.reshape(s_len, D_MODEL)
    me = 4 * lax.axis_index("x") + 2 * lax.axis_index("y") + lax.axis_index("c")
    core, chip = lax.axis_index("c"), 2 * lax.axis_index("x") + lax.axis_index("y")

    win_pieces = _shard_rows(w_in).astype(MXU_DTYPE).reshape(DEPTH, 2, N_ROWS_PAD // 2, D_MODEL)
    wout_pieces = w_out.astype(MXU_DTYPE).reshape(DEPTH, 2, OUT_SHARD // 2, D_MODEL)
    win0_all, wout0_all, rcw_all, gcw_all = _all_gather(
        [win_pieces[0], wout_pieces[0], r_conv_w[None], g_conv_w[None]], "gather_weights")
    rcw_full = jnp.moveaxis(rcw_all[:, 0], 0, 2).reshape(DEPTH, CONV_WIDTH, R_WIDTH)
    gcw_full = jnp.moveaxis(gcw_all[:, 0], 0, 2).reshape(DEPTH, CONV_WIDTH, 3 * G_WIDTH)
    cos, sin = _rope_tables(s_len)

    def big_weights(win_all, wout_all):
        wt_a, wt_r, wt_g = _group_weights(win_all.reshape(N_DEV, N_ROWS_PAD, D_MODEL)[:, :N_IN_SHARD].reshape(N_IN, D_MODEL))
        wo = wout_all.reshape(D_MODEL, D_MODEL)
        return dict(wt_a=wt_a, wt_r=wt_r, wt_g=wt_g, wo=wo,
                    wo_a=wo[0:A_WIDTH], wo_r=wo[A_WIDTH:A_WIDTH + R_WIDTH], wo_g=wo[A_WIDTH + R_WIDTH:])

    layers = []
    for l in range(DEPTH):
        layers.append(dict(
            sinks_t=jnp.broadcast_to(sinks[l][:, None, None], (A_HEADS, 8, LANES)),
            rcw=rcw_full[l], rcb=r_conv_b[l].reshape(1, R_WIDTH), wa=r_wa[l], ba=r_ba[l].reshape(1, R_WIDTH),
            wx=r_wx[l], bx=r_bx[l].reshape(1, R_WIDTH), lam=r_lam[l].reshape(1, R_WIDTH),
            gcw=gcw_full[l], zero_b=jnp.zeros((1, 3 * G_WIDTH), F32),
            a_log=_lane_row(g_a_log[l], G_HEADS), dt=_lane_row(g_dt_bias[l], G_HEADS),
            norm_w=g_norm_w[l].reshape(1, G_HEAD_DIM), ln_g=ln_g[l].reshape(1, D_MODEL), ln_b=ln_b[l].reshape(1, D_MODEL)))

    saved = []
    xin = xin_lo = x0
    layers[0].update(big_weights(win0_all, wout0_all))
    for l, p in enumerate(layers):
        if l + 1 < DEPTH:
            proj_a, (wout_next,) = _matmul([xin_lo], [p["wt_a"]], name=f"proj_a{l}", tm=1024, tn=1280, b_t=True,
                                           comm=_GatherSend([wout_pieces[l + 1]]))
            proj_r, (win_next_0,) = _matmul([xin_lo], [p["wt_r"]], name=f"proj_r{l}", tm=1024, tn=1024, b_t=True,
                                            comm=_GatherSend([win_pieces[l + 1, 0:1]]))
            proj_g, (win_next_1,) = _matmul([xin_lo], [p["wt_g"]], name=f"proj_g{l}", tm=1024, tn=1152, b_t=True,
                                            comm=_GatherSend([win_pieces[l + 1, 1:2]]))
            forward_next = _GatherForward([win_next_0, win_next_1, wout_next])
        else:
            forward_next = None
            proj_a = _matmul([xin_lo], [p["wt_a"]], name=f"proj_a{l}", tm=1024, tn=1280, b_t=True)
            proj_r = _matmul([xin_lo], [p["wt_r"]], name=f"proj_r{l}", tm=1024, tn=1024, b_t=True)
            proj_g = _matmul([xin_lo], [p["wt_g"]], name=f"proj_g{l}", tm=1024, tn=1152, b_t=True)
        ya = _attn_fwd(proj_a, cos, sin, p["sinks_t"], name=f"attn_fwd{l}")
        xr = _conv_fwd(proj_r, R_WIDTH, p["rcw"], p["rcb"], name=f"rconv_fwd{l}")
        h, yr = _rg_fwd(xr, proj_r, p["wa"], p["ba"], p["wx"], p["bx"], p["lam"], name=f"rglru_fwd{l}")
        conv = _conv_fwd(proj_g, 3 * G_WIDTH, p["gcw"], p["zero_b"], name=f"gconv_fwd{l}")
        *chunk_vals, inv = _gdn_chunk_fwd(conv, proj_g, p["a_log"], p["dt"], name=f"gdn_chunk_fwd{l}")
        yg, states = _gdn_scan_fwd(chunk_vals, proj_g, p["norm_w"], name=f"gdn_scan_fwd{l}")
        if forward_next is None:
            z, xout, xout_lo = _outproj_ln(ya, yr, yg, p["wo"], xin, p["ln_g"], p["ln_b"], name=f"outproj_ln{l}")
        else:
            (z, xout, xout_lo), (win_0, win_1, wout_all) = _outproj_ln(
                ya, yr, yg, p["wo"], xin, p["ln_g"], p["ln_b"], name=f"outproj_ln{l}", comm=forward_next)
            layers[l + 1].update(big_weights(jnp.concatenate([win_0, win_1], axis=1), wout_all))
        saved.append(dict(xin_lo=xin_lo, proj_a=proj_a, proj_r=proj_r, proj_g=proj_g, ya=ya, yr=yr, yg=yg, xr=xr, h=h,
                          conv=conv, chunk_vals=chunk_vals, inv=inv, states=states, z=z))
        xin, xin_lo = xout, xout_lo

    grads = [None] * DEPTH
    dxn = None
    loss_local = None
    for l in reversed(range(DEPTH)):
        p, sv = layers[l], saved[l]
        if dxn is None:
            dz, dz_lo, dln_g, dln_b, loss_local = _ln_bwd(sv["z"], p["ln_g"], name=f"ln_bwd{l}", xn=xin, target=target)
        else:
            dz, dz_lo, dln_g, dln_b = _ln_bwd(sv["z"], p["ln_g"], name=f"ln_bwd{l}", dxn=dxn)
        dya = _matmul([dz_lo], [p["wo_a"]], name=f"dya{l}", tm=1024, tn=512, b_t=True)
        dyr = _matmul([dz_lo], [p["wo_r"]], name=f"dyr{l}", tm=1024, tn=1024, b_t=True)
        dyg = _matmul([dz_lo], [p["wo_g"]], name=f"dyg{l}", tm=1024, tn=512, b_t=True)
        dwo = jnp.concatenate([
            _matmul_tn(sv["ya"], dz_lo, name=f"dwo_a{l}", tm=512, tn=1024, tk=1024),
            _matmul_tn(sv["yr"], dz_lo, name=f"dwo_r{l}", tm=1024, tn=1024, tk=1024),
            _matmul_tn(sv["yg"], dz_lo, name=f"dwo_g{l}", tm=512, tn=1024, tk=1024)], axis=0)

        dproj_a, dsinks_t = _attn_bwd(sv["proj_a"], cos, sin, p["sinks_t"], dya, name=f"attn_bwd{l}")

        dxr, drz, dwa, dba, dwx, dbx, dlam = _rg_bwd(sv["xr"], sv["proj_r"], sv["h"], dyr, p["wa"], p["ba"], p["wx"],
                                                     p["bx"], p["lam"], name=f"rglru_bwd{l}")
        dproj_r, drcw, drcb = _conv_bwd(dxr, sv["proj_r"], R_WIDTH, p["rcw"], [drz], name=f"rconv_bwd{l}")

        scan_out = _gdn_scan_bwd(sv["chunk_vals"], sv["states"], sv["proj_g"], p["norm_w"], dyg, name=f"gdn_scan_bwd{l}")
        dgz, dnorm_w = scan_out[6], scan_out[7]
        dconv, dbg, dal, ddt = _gdn_chunk_bwd(sv["conv"], sv["proj_g"], p["a_log"], p["dt"], sv["inv"], scan_out[:6],
                                              name=f"gdn_chunk_bwd{l}")
        dproj_g, dgcw, _ = _conv_bwd(dconv, sv["proj_g"], 3 * G_WIDTH, p["gcw"], [dgz, dbg], name=f"gconv_bwd{l}")

        grads[l] = dict(
            sinks=dsinks_t[:, :, 0].sum(axis=1), r_conv_w=drcw.reshape(CONV_WIDTH, R_WIDTH),
            r_conv_b=drcb.reshape(R_WIDTH), r_wa=dwa, r_ba=dba.reshape(R_WIDTH), r_wx=dwx, r_bx=dbx.reshape(R_WIDTH),
            r_lam=dlam.reshape(R_WIDTH), g_conv_w=dgcw.reshape(CONV_WIDTH, 3 * G_WIDTH),
            g_a_log=dal[0, G_HEADS:2 * G_HEADS], g_dt_bias=ddt[0, G_HEADS:2 * G_HEADS],
            g_norm_w=dnorm_w.reshape(G_HEAD_DIM), ln_g=dln_g.reshape(D_MODEL), ln_b=dln_b.reshape(D_MODEL))

        if l > 0:
            dwin_a = _matmul_tn(dproj_a, sv["xin_lo"], name=f"dwin_a{l}", tm=640, tn=1024, tk=1024)
            dwin_r = _matmul_tn(dproj_r, sv["xin_lo"], name=f"dwin_r{l}", tm=1024, tn=1024, tk=1024)
        else:
            packed_small = _pack([jnp.stack([grads[k][nm] for k in range(DEPTH)]) for nm in SMALL])
            dwin_a, (sent_small,) = _matmul_tn(
                dproj_a, sv["xin_lo"], name=f"dwin_a{l}", tm=640, tn=1024, tk=1024,
                comm=_GatherSend([packed_small.reshape(4, packed_small.shape[0] // 4, LANES)]))
            dwin_r, (all_small,) = _matmul_tn(dproj_r, sv["xin_lo"], name=f"dwin_r{l}", tm=1024, tn=1024, tk=1024,
                                              comm=_GatherForward([sent_small]))
        dwin = _ungroup_grads(dwin_a, dwin_r,
                              _matmul_tn(dproj_g, sv["xin_lo"], name=f"dwin_g{l}", tm=1152, tn=1024, tk=1024))

        dwin_blocks = _owner_blocks(dwin)[:, :, None].astype(MXU_DTYPE)
        dwout_blocks = jnp.transpose(dwo.reshape(4, 2, OUT_SHARD, D_MODEL), (1, 0, 2, 3))[:, :, None].astype(MXU_DTYPE)
        got_win, got_wout = _swap_cores(
            [dwin_blocks.reshape(2, 8, N_ROWS_PAD // 2, D_MODEL), dwout_blocks.reshape(2, 4, OUT_SHARD, D_MODEL)],
            f"swap_core_grads{l}")
        chip_win = _add_pair(dwin_blocks, got_win.reshape(dwin_blocks.shape[1:]), core, name=f"add_core_grads_w_in{l}",
                             rows=352).reshape(4, N_ROWS_PAD, D_MODEL)
        chip_wout = _add_pair(dwout_blocks, got_wout.reshape(dwout_blocks.shape[1:]), core, name=f"add_core_grads_w_out{l}",
                              rows=256).reshape(4, OUT_SHARD, D_MODEL)
        dxn, (win_parts, wout_parts) = _matmul(
            [dproj_a, dproj_r, dproj_g], [p["wt_a"], p["wt_r"], p["wt_g"]], name=f"dx{l}", tm=512, tn=1024, add=dz,
            add_scale=DEEPNORM_ALPHA, comm=_ChipExchange([chip_win, chip_wout]))
        grads[l].update(chip_win=chip_win, chip_wout=chip_wout, win_parts=win_parts, wout_parts=wout_parts)
    grad_x = dxn.reshape(x.shape)
    loss = lax.psum(loss_local[0, 0], ("x", "y", "c"))

    def stacked(name):
        return jnp.stack([grads[l][name] for l in range(DEPTH)])

    def per_layer(name):
        return [grads[l][name] for l in range(DEPTH)]

    w_in_t = [_unshard_rows(t) for t in _sum_adamw(per_layer("chip_win"), chip, per_layer("win_parts"), _shard_rows(w_in),
                                                   _shard_rows(m_w_in), _shard_rows(v_w_in), name="adamw_w_in", rows=176)]
    g_w_in, d_w_in, nm_w_in, nv_w_in = w_in_t
    g_w_out, d_w_out, nm_w_out, nv_w_out = _sum_adamw(per_layer("chip_wout"), chip, per_layer("wout_parts"), w_out,
                                                      m_w_out, v_w_out, name="adamw_w_out", rows=128)

    small = list(SMALL)
    full_shapes = [stacked(nm).shape for nm in small]
    all_small = all_small.reshape(N_DEV, packed_small.shape[0], LANES)
    g_small = dict(zip(small, _unpack(_sum_slots(all_small, name="sum_small_grads"), full_shapes)))
    g_small["r_conv_w"] = lax.dynamic_slice_in_dim(g_small["r_conv_w"], me * (R_WIDTH // N_DEV), R_WIDTH // N_DEV, axis=2)
    g_small["g_conv_w"] = lax.dynamic_slice_in_dim(g_small["g_conv_w"], me * (3 * G_WIDTH // N_DEV), 3 * G_WIDTH // N_DEV, axis=2)
    given = dict(sinks=(sinks, m_sinks, v_sinks), r_conv_w=(r_conv_w, m_r_conv_w, v_r_conv_w),
                 r_conv_b=(r_conv_b, m_r_conv_b, v_r_conv_b), r_wa=(r_wa, m_r_wa, v_r_wa), r_ba=(r_ba, m_r_ba, v_r_ba),
                 r_wx=(r_wx, m_r_wx, v_r_wx), r_bx=(r_bx, m_r_bx, v_r_bx), r_lam=(r_lam, m_r_lam, v_r_lam),
                 g_conv_w=(g_conv_w, m_g_conv_w, v_g_conv_w), g_a_log=(g_a_log, m_g_a_log, v_g_a_log),
                 g_dt_bias=(g_dt_bias, m_g_dt_bias, v_g_dt_bias), g_norm_w=(g_norm_w, m_g_norm_w, v_g_norm_w),
                 ln_g=(ln_g, m_ln_g, v_ln_g), ln_b=(ln_b, m_ln_b, v_ln_b))
    shard_shapes = [given[nm][0].shape for nm in small]
    packed = [_pack([given[nm][k] for nm in small]) for k in range(3)]
    d_p, nm_p, nv_p = _adamw_packed(packed[0], _pack([g_small[nm] for nm in small]), packed[1], packed[2], name="adamw_small")
    d_small = dict(zip(small, _unpack(d_p, shard_shapes)))
    nm_small = dict(zip(small, _unpack(nm_p, shard_shapes)))
    nv_small = dict(zip(small, _unpack(nv_p, shard_shapes)))

    order = ["w_in"] + small[:12] + ["w_out"] + small[12:]

    def leaf(big_in, big_out, table):
        return [big_in if nm == "w_in" else big_out if nm == "w_out" else table[nm] for nm in order]

    return (loss, grad_x, *leaf(g_w_in, g_w_out, g_small), *leaf(d_w_in, d_w_out, d_small),
            *leaf(nm_w_in, nm_w_out, nm_small), *leaf(nv_w_in, nv_w_out, nv_small))
```

```python
import functools
import math

import jax
import jax.numpy as jnp
from jax import lax
from jax.experimental import pallas as pl
from jax.experimental.pallas import tpu as pltpu

F32 = jnp.float32
MXU_DTYPE = jnp.bfloat16
HIGHEST = lax.Precision.HIGHEST
MESH_ID = pl.DeviceIdType.MESH

N_DEV = 8
DEPTH = 2
D_MODEL = 2048
A_HEADS, A_KV_HEADS, A_HEAD_DIM = 8, 2, 64
A_WIDTH, A_KV_WIDTH = 512, 128
A_BLOCK = 128
ROPE_THETA = 10000.0
R_WIDTH, R_BLOCKS, R_BLOCK_DIM = 1024, 8, 128
R_C = 8.0
CONV_WIDTH = 4
G_HEADS, G_HEAD_DIM, G_WIDTH, G_CHUNK = 4, 128, 512, 64
N_IN = 5384
N_IN_SHARD = N_IN // N_DEV
N_ROWS_PAD = 704
OUT_SHARD = D_MODEL // N_DEV
WA, WR, WG = 1280, 2048, 2304
G_PAD = WG - (3 * G_WIDTH + G_WIDTH + 2 * G_HEADS)
DEEPNORM_ALPHA = (2 * DEPTH) ** 0.25
LN_EPS = 1e-5
RMS_EPS = 1e-6
ADAM_LR, ADAM_B1, ADAM_B2, ADAM_EPS, ADAM_WD, ADAM_STEP = 0.001, 0.9, 0.999, 1e-08, 0.01, 10
NEG = -1e30
VMEM_LIMIT = 56 * 1024 * 1024
LANES = 128


def _call(body, **kw):
    return pl.pallas_call(body, **kw)


def _params(*sem):
    return pltpu.CompilerParams(dimension_semantics=sem, vmem_limit_bytes=VMEM_LIMIT)


def _t(x):
    return jnp.swapaxes(x, -1, -2)


def _raw_dot(a, b, ca, cb, precision=None):
    batch = tuple(range(a.ndim - 2))
    if precision is None:
        a, b = a.astype(MXU_DTYPE), b.astype(MXU_DTYPE)
    return lax.dot_general(a, b, (((ca,), (cb,)), (batch, batch)), precision=precision,
                           preferred_element_type=F32)


def _nn(a, b, precision=None):
    return _raw_dot(a, b, a.ndim - 1, b.ndim - 2, precision)


def _nt(a, b, precision=None):
    return _raw_dot(a, b, a.ndim - 1, b.ndim - 1, precision)


@jax.custom_vjp
def mm_nn(a, b):
    return _nn(a, b)


def _mm_nn_fwd(a, b):
    return _nn(a, b), (a, b)


def _mm_nn_bwd(res, g):
    a, b = res
    return _nt(g, b), _nn(_t(a), g)


mm_nn.defvjp(_mm_nn_fwd, _mm_nn_bwd)


@jax.custom_vjp
def mm_nt(a, b):
    return _nt(a, b)


def _mm_nt_fwd(a, b):
    return _nt(a, b), (a, b)


def _mm_nt_bwd(res, g):
    a, b = res
    return _nn(g, b), _nn(_t(g), a)


mm_nt.defvjp(_mm_nt_fwd, _mm_nt_bwd)


def _split(x):
    hi = x.astype(MXU_DTYPE)
    return hi, (x - hi.astype(F32)).astype(MXU_DTYPE)


def _hmm(a, b, nt=False):
    dot = _nt if nt else _nn
    return dot(a[0], b[0]) + (dot(a[0], b[1]) + dot(a[1], b[0]))


def _silu(x):
    return x * jax.nn.sigmoid(x)


def _softplus(x):
    return jnp.maximum(x, 0.0) + jnp.log1p(jnp.exp(-jnp.abs(x)))


def _one_minus_sq(log_a, a):
    x = 2.0 * log_a
    return jnp.where(x > -0.01, -x * (1.0 + 0.5 * x), 1.0 - a * a)


def _iota(shape, dim):
    return lax.broadcasted_iota(jnp.int32, shape, dim)


def _inv_unit_lower(m):
    shape = m.shape
    row, col = _iota(shape, 1), _iota(shape, 2)
    eye = (row == col).astype(F32)

    def blockdiag(size):
        return (row // size) == (col // size)

    x = -jnp.where(blockdiag(8), m, 0.0)
    xs = _split(x)
    x2s = _split(_hmm(xs, xs))
    x4s = _split(_hmm(x2s, x2s))
    inv = eye + x
    inv = inv + _hmm(_split(inv), x2s)
    inv = inv + _hmm(_split(inv), x4s)
    for size in (8, 16, 32):
        below = jnp.where(blockdiag(2 * size) & jnp.logical_not(blockdiag(size)), m, 0.0)
        invs = _split(inv)
        inv = inv - _hmm(_split(_hmm(invs, _split(below))), invs)
    return inv


@jax.custom_vjp
def _solve2(m, inv, r1, r2):
    invs = _split(inv)
    return _hmm(invs, _split(r1)), _hmm(invs, _split(r2))


def _solve2_fwd(m, inv, r1, r2):
    x1, x2 = _solve2(m, inv, r1, r2)
    return (x1, x2), (inv, x1, x2)


def _solve2_bwd(res, g):
    inv, x1, x2 = res
    inv_ts = _split(_t(inv))
    d1, d2 = _hmm(inv_ts, _split(g[0])), _hmm(inv_ts, _split(g[1]))
    dm = -(_hmm(_split(d1), _split(x1), nt=True) + _hmm(_split(d2), _split(x2), nt=True))
    return dm, jnp.zeros_like(inv), d1, d2


_solve2.defvjp(_solve2_fwd, _solve2_bwd)


def _swap_halves(x):
    n = x.shape[-1]
    lane = _iota(x.shape, x.ndim - 1)
    return jnp.where((lane & 63) < 32, pltpu.roll(x, n - 32, x.ndim - 1), pltpu.roll(x, 32, x.ndim - 1))


def _rope(x, cos, sin):
    reps = x.shape[-1] // LANES
    if reps > 1:
        cos, sin = jnp.tile(cos, (1, reps)), jnp.tile(sin, (1, reps))
    return x * cos + _swap_halves(x) * sin


def _rope_t(d, cos, sin):
    reps = d.shape[-1] // LANES
    if reps > 1:
        cos, sin = jnp.tile(cos, (1, reps)), jnp.tile(sin, (1, reps))
    return d * cos + _swap_halves(d * sin)


def _swap64(x):
    return pltpu.roll(x, 64, x.ndim - 1)


def _mesh_pos():
    return lax.axis_index("x"), lax.axis_index("y"), lax.axis_index("c")


def _all_gather(arrays, name):
    n = len(arrays)
    npieces = [a.shape[0] for a in arrays]
    pmax = max(npieces)

    def body(*refs):
        ins, outs = refs[:n], refs[n:2 * n]
        send_sems, recv_sems, local_sem = refs[2 * n:]
        x, y, c = _mesh_pos()
        me, sibling = (x, y, c), (x, y, 1 - c)
        chips = [(1 - x, y), (x, 1 - y), (1 - x, 1 - y)]

        def slot(a, pos, p):
            return outs[a].at[4 * pos[0] + 2 * pos[1] + pos[2], p]

        def copy(a, p, k, block, to, own=False):
            return pltpu.make_async_remote_copy(
                src_ref=ins[a].at[p] if own else slot(a, block, p), dst_ref=slot(a, block, p),
                send_sem=send_sems.at[a, p, k], recv_sem=recv_sems.at[a, p, k], device_id=to, device_id_type=MESH_ID)

        pieces = [(a, p) for p in range(pmax) for a in range(n) if p < npieces[a]]
        mine = [pltpu.make_async_copy(ins[a].at[p], slot(a, me, p), local_sem.at[a, p]) for a, p in pieces]
        for cp in mine:
            cp.start()
        first = []
        for a, p in pieces:
            first += [copy(a, p, 1 + j, me, (*chip, c), own=True) for j, chip in enumerate(chips)]
            first.append(copy(a, p, 0, me, sibling, own=True))
        for cp in first:
            cp.start()
        passed = []
        for a, p in pieces:
            for j, chip in enumerate(chips):
                copy(a, p, 1 + j, (*chip, c), me).wait_recv()
                cp = copy(a, p, 4 + j, (*chip, c), sibling)
                cp.start()
                passed.append(cp)
        for a, p in pieces:
            copy(a, p, 0, sibling, me).wait_recv()
            for j, chip in enumerate(chips):
                copy(a, p, 4 + j, (*chip, 1 - c), me).wait_recv()
        for cp in first + passed:
            cp.wait_send()
        for cp in mine:
            cp.wait()

    any_spec = pl.BlockSpec(memory_space=pl.ANY)
    return _call(
        body, name=name,
        out_shape=[jax.ShapeDtypeStruct((N_DEV,) + a.shape, a.dtype) for a in arrays],
        in_specs=[any_spec] * n, out_specs=[any_spec] * n,
        scratch_shapes=[pltpu.SemaphoreType.DMA((n, pmax, 7)), pltpu.SemaphoreType.DMA((n, pmax, 7)),
                        pltpu.SemaphoreType.DMA((n, pmax))],
    )(*arrays)


def _swap_cores(arrays, name):
    n = len(arrays)
    pmax = max(a.shape[1] for a in arrays)

    def body(*refs):
        ins, got = refs[:n], refs[n:2 * n]
        send_sems, recv_sems = refs[2 * n:]
        x, y, c = _mesh_pos()
        copies = [pltpu.make_async_remote_copy(
            src_ref=ins[a].at[1 - c, p], dst_ref=got[a].at[p], send_sem=send_sems.at[a, p], recv_sem=recv_sems.at[a, p],
            device_id=(x, y, 1 - c), device_id_type=MESH_ID) for a in range(n) for p in range(arrays[a].shape[1])]
        for cp in copies:
            cp.start()
        for cp in copies:
            cp.wait()

    any_spec = pl.BlockSpec(memory_space=pl.ANY)
    return _call(
        body, name=name, out_shape=[jax.ShapeDtypeStruct(a.shape[1:], a.dtype) for a in arrays],
        in_specs=[any_spec] * n, out_specs=[any_spec] * n,
        scratch_shapes=[pltpu.SemaphoreType.DMA((n, pmax)), pltpu.SemaphoreType.DMA((n, pmax))],
    )(*arrays)


class _ChipExchange:
    aliases = {}

    def __init__(self, arrays):
        self.arrays = list(arrays)
        n = len(self.arrays)
        self.out_shape = [jax.ShapeDtypeStruct((3,) + a.shape[1:], a.dtype) for a in self.arrays]
        self.scratch = [pltpu.SemaphoreType.DMA((n, 3)), pltpu.SemaphoreType.DMA((n, 3))]

    def _copies(self, ins, outs, send_sems, recv_sems):
        x, y, c = _mesh_pos()
        copies = []
        for a in range(len(self.arrays)):
            for k in range(1, 4):
                px, py = x ^ (k >> 1), y ^ (k & 1)
                copies.append(pltpu.make_async_remote_copy(
                    src_ref=ins[a].at[2 * px + py], dst_ref=outs[a].at[k - 1], send_sem=send_sems.at[a, k - 1],
                    recv_sem=recv_sems.at[a, k - 1], device_id=(px, py, c), device_id_type=MESH_ID))
        return copies

    def start(self, ins, outs, send_sems, recv_sems):
        for cp in self._copies(ins, outs, send_sems, recv_sems):
            cp.start()

    def finish(self, ins, outs, send_sems, recv_sems):
        copies = self._copies(ins, outs, send_sems, recv_sems)
        for cp in copies:
            cp.wait_recv()
        for cp in copies:
            cp.wait_send()


def _slot(pos):
    return 4 * pos[0] + 2 * pos[1] + pos[2]


class _GatherSend:
    aliases = {}

    def __init__(self, arrays):
        self.arrays = list(arrays)
        n, pmax = len(self.arrays), max(a.shape[0] for a in self.arrays)
        self.out_shape = [jax.ShapeDtypeStruct((N_DEV,) + a.shape, a.dtype) for a in self.arrays]
        self.scratch = [pltpu.SemaphoreType.DMA((n, pmax, 4)), pltpu.SemaphoreType.DMA((n, pmax, 4)),
                        pltpu.SemaphoreType.DMA((n, pmax))]

    def _copies(self, ins, outs, send_sems, recv_sems, local_sems):
        x, y, c = _mesh_pos()
        peers = [(x, y, 1 - c), (1 - x, y, c), (x, 1 - y, c), (1 - x, 1 - y, c)]
        local, remote = [], []
        for a, arr in enumerate(self.arrays):
            for p in range(arr.shape[0]):
                local.append(pltpu.make_async_copy(ins[a].at[p], outs[a].at[_slot((x, y, c)), p], local_sems.at[a, p]))
                for k, peer in enumerate(peers):
                    remote.append(pltpu.make_async_remote_copy(
                        src_ref=ins[a].at[p], dst_ref=outs[a].at[_slot((x, y, c)), p], send_sem=send_sems.at[a, p, k],
                        recv_sem=recv_sems.at[a, p, k], device_id=peer, device_id_type=MESH_ID))
        return local, remote

    def start(self, *refs):
        local, remote = self._copies(*refs)
        for cp in local + remote:
            cp.start()

    def finish(self, *refs):
        local, remote = self._copies(*refs)
        for cp in remote:
            cp.wait_recv()
        for cp in remote:
            cp.wait_send()
        for cp in local:
            cp.wait()


class _GatherForward:
    def __init__(self, gathered):
        self.arrays = list(gathered)
        n, pmax = len(self.arrays), max(a.shape[1] for a in self.arrays)
        self.out_shape = [jax.ShapeDtypeStruct(a.shape, a.dtype) for a in self.arrays]
        self.aliases = {k: k for k in range(n)}
        self.scratch = [pltpu.SemaphoreType.DMA((n, pmax, 3)), pltpu.SemaphoreType.DMA((n, pmax, 3))]

    def _copies(self, ins, outs, send_sems, recv_sems):
        x, y, c = _mesh_pos()
        copies = []
        for a, arr in enumerate(self.arrays):
            for p in range(arr.shape[1]):
                for j, chip in enumerate([(1 - x, y), (x, 1 - y), (1 - x, 1 - y)]):
                    copies.append(pltpu.make_async_remote_copy(
                        src_ref=ins[a].at[_slot((*chip, c)), p], dst_ref=outs[a].at[_slot((*chip, c)), p],
                        send_sem=send_sems.at[a, p, j], recv_sem=recv_sems.at[a, p, j], device_id=(x, y, 1 - c),
                        device_id_type=MESH_ID))
        return copies

    def start(self, *refs):
        for cp in self._copies(*refs):
            cp.start()

    def finish(self, *refs):
        copies = self._copies(*refs)
        for cp in copies:
            cp.wait_recv()
        for cp in copies:
            cp.wait_send()


def _index_operand(i):
    return jnp.reshape(i, (1,)).astype(jnp.int32)


def _add_pair(pair, other, core, *, name, rows):
    _, n_slots, n_layers, n_rows, n_cols = pair.shape
    rows = min(rows, n_rows)

    def body(c_ref, a_ref, b_ref, o_ref):
        o_ref[...] = (a_ref[0].astype(F32) + b_ref[...].astype(F32)).astype(o_ref.dtype)

    blk = pl.BlockSpec((1, 1, rows, n_cols), lambda s, l, i, c: (s, l, i, 0))
    grid_spec = pltpu.PrefetchScalarGridSpec(
        num_scalar_prefetch=1, grid=(n_slots, n_layers, n_rows // rows),
        in_specs=[pl.BlockSpec((1, 1, 1, rows, n_cols), lambda s, l, i, c: (c[0], s, l, i, 0)), blk], out_specs=blk)
    return _call(body, name=name, grid_spec=grid_spec, out_shape=jax.ShapeDtypeStruct(other.shape, pair.dtype),
                 compiler_params=_params("parallel", "parallel", "parallel"))(_index_operand(core), pair, other)


def _host(body, comm, *, name, grid, in_specs, out_specs, out_shape, args, semantics):
    if comm is None:
        return _call(body, name=name, grid=grid, in_specs=in_specs, out_specs=out_specs, out_shape=out_shape,
                     compiler_params=_params(*semantics))(*args)
    n_in, n_out, n_comm = len(in_specs), len(out_specs), len(comm.arrays)

    def hosted(*refs):
        ins, outs = refs[:n_in], refs[n_in + n_comm:n_in + n_comm + n_out]
        comm_refs = (refs[n_in:n_in + n_comm], refs[n_in + n_comm + n_out:n_in + 2 * n_comm + n_out],
                     *refs[n_in + 2 * n_comm + n_out:])
        ids = [pl.program_id(d) for d in range(len(grid))]
        first, last = ids[0] == 0, ids[0] == grid[0] - 1
        for d in range(1, len(grid)):
            first, last = first & (ids[d] == 0), last & (ids[d] == grid[d] - 1)

        @pl.when(first)
        def _():
            comm.start(*comm_refs)

        body(*ins, *outs)

        @pl.when(last)
        def _():
            comm.finish(*comm_refs)

    any_spec = pl.BlockSpec(memory_space=pl.ANY)
    outs = _call(hosted, name=name, grid=grid, in_specs=list(in_specs) + [any_spec] * n_comm,
                 out_specs=list(out_specs) + [any_spec] * n_comm, out_shape=list(out_shape) + comm.out_shape,
                 input_output_aliases={n_in + k: n_out + v for k, v in comm.aliases.items()},
                 scratch_shapes=comm.scratch, compiler_params=_params(*(("arbitrary",) * len(grid))))(*args, *comm.arrays)
    return outs[:n_out], outs[n_out:]


def _matmul(a_list, b_list, *, name, tm, tn, b_t=False, out_dtype=F32, add=None, add_scale=1.0, comm=None):
    n = len(a_list)
    m_rows, n_cols = a_list[0].shape[0], b_list[0].shape[0 if b_t else 1]
    tm, tn = min(tm, m_rows), min(tn, n_cols)

    def body(*refs):
        a_refs, b_refs = refs[:n], refs[n:2 * n]
        o_ref = refs[-1]
        acc = None
        for a_ref, b_ref in zip(a_refs, b_refs):
            part = lax.dot_general(a_ref[...].astype(MXU_DTYPE), b_ref[...].astype(MXU_DTYPE),
                                   (((1,), (1 if b_t else 0,)), ((), ())), preferred_element_type=F32)
            acc = part if acc is None else acc + part
        if add is not None:
            acc = acc + add_scale * refs[2 * n][...]
        o_ref[...] = acc.astype(o_ref.dtype)

    in_specs = [pl.BlockSpec((tm, a.shape[1]), lambda i, j: (i, 0)) for a in a_list]
    if b_t:
        in_specs += [pl.BlockSpec((tn, b.shape[1]), lambda i, j: (j, 0)) for b in b_list]
    else:
        in_specs += [pl.BlockSpec((b.shape[0], tn), lambda i, j: (0, j)) for b in b_list]
    args = list(a_list) + list(b_list)
    if add is not None:
        in_specs.append(pl.BlockSpec((tm, tn), lambda i, j: (i, j)))
        args.append(add)
    res = _host(body, comm, name=name, grid=(m_rows // tm, n_cols // tn), in_specs=in_specs,
                out_specs=[pl.BlockSpec((tm, tn), lambda i, j: (i, j))],
                out_shape=[jax.ShapeDtypeStruct((m_rows, n_cols), out_dtype)], args=args,
                semantics=("parallel", "arbitrary"))
    return res[0] if comm is None else (res[0][0], res[1])


def _matmul_tn(a, b, *, name, tm, tn, tk, comm=None):
    k_rows, m_rows = a.shape
    n_cols = b.shape[1]
    tm, tn, tk = min(tm, m_rows), min(tn, n_cols), min(tk, k_rows)
    nk = k_rows // tk

    def body(a_ref, b_ref, o_ref):
        @pl.when(pl.program_id(2) == 0)
        def _():
            o_ref[...] = jnp.zeros_like(o_ref)

        o_ref[...] += lax.dot_general(a_ref[...].astype(MXU_DTYPE), b_ref[...].astype(MXU_DTYPE),
                                      (((0,), (0,)), ((), ())), preferred_element_type=F32)

    res = _host(body, comm, name=name, grid=(m_rows // tm, n_cols // tn, nk),
                in_specs=[pl.BlockSpec((tk, tm), lambda i, j, k: (k, i)), pl.BlockSpec((tk, tn), lambda i, j, k: (k, j))],
                out_specs=[pl.BlockSpec((tm, tn), lambda i, j, k: (i, j))],
                out_shape=[jax.ShapeDtypeStruct((m_rows, n_cols), F32)], args=[a, b],
                semantics=("parallel", "parallel", "arbitrary"))
    return res[0] if comm is None else (res[0][0], res[1])


def _outproj_ln(ya, yr, yg, w_out, x, ln_g, ln_b, *, name, comm=None):
    s_len = x.shape[0]
    tm = min(256, s_len)

    def body(ya_ref, yr_ref, yg_ref, w_ref, x_ref, g_ref, b_ref, z_ref, o_ref, lo_ref):
        acc = jnp.dot(ya_ref[...], w_ref[0:A_WIDTH, :], preferred_element_type=F32)
        acc += jnp.dot(yr_ref[...], w_ref[A_WIDTH:A_WIDTH + R_WIDTH, :], preferred_element_type=F32)
        acc += jnp.dot(yg_ref[...], w_ref[A_WIDTH + R_WIDTH:, :], preferred_element_type=F32)
        z = DEEPNORM_ALPHA * x_ref[...] + acc
        z_ref[...] = z
        mu = jnp.mean(z, axis=-1, keepdims=True)
        zc = z - mu
        var = jnp.mean(zc * zc, axis=-1, keepdims=True)
        out = zc * lax.rsqrt(var + LN_EPS) * g_ref[...] + b_ref[...]
        o_ref[...] = out
        lo_ref[...] = out.astype(lo_ref.dtype)

    def rows(width):
        return pl.BlockSpec((tm, width), lambda i: (i, 0))

    def whole(shape):
        return pl.BlockSpec(shape, lambda i: (0, 0))

    return _host(
        body, comm, name=name, grid=(s_len // tm,),
        in_specs=[rows(A_WIDTH), rows(R_WIDTH), rows(G_WIDTH), whole((D_MODEL, D_MODEL)), rows(D_MODEL),
                  whole((1, D_MODEL)), whole((1, D_MODEL))],
        out_specs=[rows(D_MODEL)] * 3,
        out_shape=[jax.ShapeDtypeStruct((s_len, D_MODEL), F32)] * 2 + [jax.ShapeDtypeStruct((s_len, D_MODEL), MXU_DTYPE)],
        args=[ya, yr, yg, w_out, x, ln_g, ln_b], semantics=("parallel",))


def _ln_bwd(z, ln_g, *, name, dxn=None, xn=None, target=None):
    s_len = z.shape[0]
    tm = min(256, s_len)
    top = dxn is None

    def body(*refs):
        if top:
            z_ref, g_ref, xn_ref, t_ref, dz_ref, lo_ref, dg_ref, db_ref, loss_ref = refs
            err = xn_ref[...] - t_ref[...]
            dy = err * (1.0 / D_MODEL)
        else:
            z_ref, g_ref, dy_ref, dz_ref, lo_ref, dg_ref, db_ref = refs
            dy = dy_ref[...]
        first = pl.program_id(0) == 0

        @pl.when(first)
        def _():
            dg_ref[...] = jnp.zeros_like(dg_ref)
            db_ref[...] = jnp.zeros_like(db_ref)
            if top:
                loss_ref[...] = jnp.zeros_like(loss_ref)

        z = z_ref[...]
        mu = jnp.mean(z, axis=-1, keepdims=True)
        zc = z - mu
        rstd = lax.rsqrt(jnp.mean(zc * zc, axis=-1, keepdims=True) + LN_EPS)
        xhat = zc * rstd
        dxh = dy * g_ref[...]
        dz = rstd * (dxh - jnp.mean(dxh, axis=-1, keepdims=True) - xhat * jnp.mean(dxh * xhat, axis=-1, keepdims=True))
        dz_ref[...] = dz
        lo_ref[...] = dz.astype(lo_ref.dtype)
        dg_ref[...] += jnp.sum(dy * xhat, axis=0, keepdims=True)
        db_ref[...] += jnp.sum(dy, axis=0, keepdims=True)
        if top:
            per_row = jnp.sum(err * err, axis=-1, keepdims=True) * (0.5 / D_MODEL)
            loss_ref[...] += jnp.sum(per_row, axis=0, keepdims=True)

    rows = pl.BlockSpec((tm, D_MODEL), lambda i: (i, 0))
    vec = pl.BlockSpec((1, D_MODEL), lambda i: (0, 0))
    in_specs = [rows, vec] + ([rows, rows] if top else [rows])
    args = [z, ln_g] + ([xn, target] if top else [dxn])
    out_specs = [rows, rows, vec, vec]
    out_shape = [jax.ShapeDtypeStruct((s_len, D_MODEL), F32), jax.ShapeDtypeStruct((s_len, D_MODEL), MXU_DTYPE),
                 jax.ShapeDtypeStruct((1, D_MODEL), F32), jax.ShapeDtypeStruct((1, D_MODEL), F32)]
    if top:
        out_specs.append(pl.BlockSpec((1, 1), lambda i: (0, 0)))
        out_shape.append(jax.ShapeDtypeStruct((1, 1), F32))
    return _call(body, name=name, grid=(s_len // tm,), in_specs=in_specs, out_specs=out_specs,
                 out_shape=out_shape, compiler_params=_params("arbitrary"))(*args)


CONV_ROWS = 512
HALO = 8


def _shift_down(x, halo, s):
    if s == 0:
        return x
    ext = jnp.concatenate([halo, x], axis=0)
    return pltpu.roll(ext, s, 0)[HALO:, :]


def _shift_up(x, halo, s):
    if s == 0:
        return x
    ext = jnp.concatenate([x, halo], axis=0)
    return pltpu.roll(ext, ext.shape[0] - s, 0)[:x.shape[0], :]


def _conv_fwd(src, width, w, bias, *, name):
    s_len = src.shape[0]
    rows = min(CONV_ROWS, s_len)
    per = rows // HALO

    def body(x_ref, halo_ref, w_ref, b_ref, o_ref):
        x = x_ref[...]
        halo = jnp.where(pl.program_id(0) == 0, 0.0, halo_ref[...])
        acc = x * w_ref[3:4, :] + b_ref[...]
        for k in range(CONV_WIDTH - 1):
            acc += _shift_down(x, halo, 3 - k) * w_ref[k:k + 1, :]
        o_ref[...] = acc

    return _call(
        body, name=name, grid=(s_len // rows,),
        in_specs=[pl.BlockSpec((rows, width), lambda i: (i, 0)),
                  pl.BlockSpec((HALO, width), lambda i: (jnp.maximum(i * per - 1, 0), 0)),
                  pl.BlockSpec((CONV_WIDTH, width), lambda i: (0, 0)), pl.BlockSpec((1, width), lambda i: (0, 0))],
        out_specs=pl.BlockSpec((rows, width), lambda i: (i, 0)),
        out_shape=jax.ShapeDtypeStruct((s_len, width), F32),
        compiler_params=_params("parallel"),
    )(src, src, w, bias)


def _conv_bwd(dy, src, width, w, passthrough, *, name):
    s_len = src.shape[0]
    rows = min(CONV_ROWS, s_len)
    per = rows // HALO
    nblk = s_len // rows
    extra = [p.shape[1] for p in passthrough]
    total = width + sum(extra)

    def body(*refs):
        dy_ref, dyh_ref, x_ref, xh_ref, w_ref = refs[:5]
        p_refs = refs[5:5 + len(extra)]
        o_ref, dw_ref, db_ref = refs[5 + len(extra):]
        i = pl.program_id(0)

        @pl.when(i == 0)
        def _():
            dw_ref[...] = jnp.zeros_like(dw_ref)
            db_ref[...] = jnp.zeros_like(db_ref)

        dy = dy_ref[...]
        x = x_ref[...]
        dy_halo = jnp.where(i == nblk - 1, 0.0, dyh_ref[...])
        x_halo = jnp.where(i == 0, 0.0, xh_ref[...])
        dx = dy * w_ref[3:4, :]
        dw_ref[3] += jnp.sum(dy * x, axis=0, keepdims=True)
        for k in range(CONV_WIDTH - 1):
            dx += _shift_up(dy, dy_halo, 3 - k) * w_ref[k:k + 1, :]
            dw_ref[k] += jnp.sum(dy * _shift_down(x, x_halo, 3 - k), axis=0, keepdims=True)
        db_ref[...] += jnp.sum(dy, axis=0, keepdims=True)
        o_ref[:, 0:width] = dx.astype(o_ref.dtype)
        off = width
        for p_ref, wd in zip(p_refs, extra):
            o_ref[:, off:off + wd] = p_ref[...].astype(o_ref.dtype)
            off += wd

    in_specs = [pl.BlockSpec((rows, width), lambda i: (i, 0)),
                pl.BlockSpec((HALO, width), lambda i: (jnp.minimum((i + 1) * per, nblk * per - 1), 0)),
                pl.BlockSpec((rows, width), lambda i: (i, 0)),
                pl.BlockSpec((HALO, width), lambda i: (jnp.maximum(i * per - 1, 0), 0)),
                pl.BlockSpec((CONV_WIDTH, width), lambda i: (0, 0))]
    in_specs += [pl.BlockSpec((rows, wd), lambda i: (i, 0)) for wd in extra]
    return _call(
        body, name=name, grid=(nblk,), in_specs=in_specs,
        out_specs=[pl.BlockSpec((rows, total), lambda i: (i, 0)),
                   pl.BlockSpec((CONV_WIDTH, 1, width), lambda i: (0, 0, 0)), pl.BlockSpec((1, width), lambda i: (0, 0))],
        out_shape=[jax.ShapeDtypeStruct((s_len, total), MXU_DTYPE), jax.ShapeDtypeStruct((CONV_WIDTH, 1, width), F32),
                   jax.ShapeDtypeStruct((1, width), F32)],
        compiler_params=_params("arbitrary"),
    )(dy, dy, src, src, w, *passthrough)


def _attn_mask(first):
    i = _iota((A_BLOCK, 2 * A_BLOCK), 0)
    j = _iota((A_BLOCK, 2 * A_BLOCK), 1)
    band = (j > i) & (j <= i + A_BLOCK)
    return band & ((j >= A_BLOCK) | jnp.logical_not(first))


def _attn_group(p, mask, qg, kw, kws, vw, vws, azg, sink0, sink1):
    low = _iota(qg.shape, 1) < A_HEAD_DIM
    first_lane = (_iota((A_BLOCK, LANES), 1) == 0).astype(F32)
    out = None
    for half, sink in ((0, sink0), (1, sink1)):
        kv_head = (2 * p + half) // (A_HEADS // A_KV_HEADS)
        keep = low if half == 0 else jnp.logical_not(low)
        qm = jnp.where(keep, qg, 0.0)
        kk, vv = (kw, vw) if kv_head == half else (kws, vws)
        s = mm_nt(qm, kk) * (A_HEAD_DIM ** -0.5)
        s = jnp.where(mask, s, NEG)
        sk = jnp.sum(jnp.tile(sink, (A_BLOCK // 8, 1)) * first_lane, axis=1, keepdims=True)
        m = lax.stop_gradient(jnp.maximum(jnp.max(s, axis=1, keepdims=True), sk))
        e = jnp.exp(s - m)
        denom = jnp.sum(e, axis=1, keepdims=True) + jnp.exp(sk - m)
        o = mm_nn(e * (1.0 / denom), vv)
        o = jnp.where(keep, o, 0.0)
        out = o if out is None else out + o
    return out * _silu(azg)


def _attn_specs(s_len, rev):
    nb = s_len // A_BLOCK

    def cur(i):
        return nb - 1 - i if rev else i

    def prev(i):
        return jnp.maximum(cur(i) - 1, 0)

    def blk(width, col, which):
        return pl.BlockSpec((A_BLOCK, width), lambda i: (which(i), col))

    return [blk(A_WIDTH, 0, cur), blk(A_WIDTH, 1, cur), blk(LANES, 8, cur), blk(LANES, 9, cur),
            blk(LANES, 8, prev), blk(LANES, 9, prev), blk(LANES, 0, cur), blk(LANES, 0, cur),
            blk(LANES, 0, prev), blk(LANES, 0, prev)], cur


def _attn_fwd(proj_a, cos, sin, sinks_t, *, name):
    s_len = proj_a.shape[0]
    specs, _ = _attn_specs(s_len, False)

    def body(q_ref, az_ref, k_ref, v_ref, kp_ref, vp_ref, c_ref, s_ref, cp_ref, sp_ref, sink_ref, o_ref):
        first = pl.program_id(0) == 0
        mask = _attn_mask(first)
        qr = _rope(q_ref[...], c_ref[...], s_ref[...])
        kw = jnp.concatenate([_rope(kp_ref[...], cp_ref[...], sp_ref[...]), _rope(k_ref[...], c_ref[...], s_ref[...])], 0)
        vw = jnp.concatenate([vp_ref[...], v_ref[...]], 0)
        kws, vws = _swap64(kw), _swap64(vw)
        for p in range(A_WIDTH // LANES):
            cols = slice(p * LANES, (p + 1) * LANES)
            o = _attn_group(p, mask, qr[:, cols], kw, kws, vw, vws, az_ref[:, cols], sink_ref[2 * p], sink_ref[2 * p + 1])
            o_ref[:, cols] = o.astype(o_ref.dtype)

    return _call(
        body, name=name, grid=(s_len // A_BLOCK,),
        in_specs=specs + [pl.BlockSpec((A_HEADS, 8, LANES), lambda i: (0, 0, 0))],
        out_specs=pl.BlockSpec((A_BLOCK, A_WIDTH), lambda i: (i, 0)),
        out_shape=jax.ShapeDtypeStruct((s_len, A_WIDTH), MXU_DTYPE),
        compiler_params=_params("parallel"),
    )(proj_a, proj_a, proj_a, proj_a, proj_a, proj_a, cos, sin, cos, sin, sinks_t)


def _attn_bwd(proj_a, cos, sin, sinks_t, dya, *, name):
    s_len = proj_a.shape[0]
    specs, cur = _attn_specs(s_len, True)

    def body(q_ref, az_ref, k_ref, v_ref, kp_ref, vp_ref, c_ref, s_ref, cp_ref, sp_ref, sink_ref, dy_ref,
             o_ref, dsink_ref, dk_carry, dv_carry):
        i = pl.program_id(0)

        @pl.when(i == 0)
        def _():
            dsink_ref[...] = jnp.zeros_like(dsink_ref)
            dk_carry[...] = jnp.zeros_like(dk_carry)
            dv_carry[...] = jnp.zeros_like(dv_carry)

        first = cur(i) == 0
        mask = _attn_mask(first)
        cos_c, sin_c = c_ref[...], s_ref[...]
        qr = _rope(q_ref[...], cos_c, sin_c)
        kw = jnp.concatenate([_rope(kp_ref[...], cp_ref[...], sp_ref[...]), _rope(k_ref[...], cos_c, sin_c)], 0)
        vw = jnp.concatenate([vp_ref[...], v_ref[...]], 0)
        kws, vws = _swap64(kw), _swap64(vw)
        dkw = jnp.zeros_like(kw)
        dvw = jnp.zeros_like(vw)
        for p in range(A_WIDTH // LANES):
            cols = slice(p * LANES, (p + 1) * LANES)
            _, vjp = jax.vjp(functools.partial(_attn_group, p, mask), qr[:, cols], kw, kws, vw, vws, az_ref[:, cols],
                             sink_ref[2 * p], sink_ref[2 * p + 1])
            dq, dk1, dk2, dv1, dv2, daz, ds0, ds1 = vjp(dy_ref[:, cols])
            dkw += dk1 + _swap64(dk2)
            dvw += dv1 + _swap64(dv2)
            o_ref[:, cols] = _rope_t(dq, cos_c, sin_c).astype(o_ref.dtype)
            o_ref[:, A_WIDTH + p * LANES:A_WIDTH + (p + 1) * LANES] = daz.astype(o_ref.dtype)
            dsink_ref[2 * p] += ds0
            dsink_ref[2 * p + 1] += ds1
        o_ref[:, 2 * A_WIDTH:2 * A_WIDTH + LANES] = _rope_t(dkw[A_BLOCK:, :] + dk_carry[...], cos_c, sin_c).astype(o_ref.dtype)
        o_ref[:, 2 * A_WIDTH + LANES:] = (dvw[A_BLOCK:, :] + dv_carry[...]).astype(o_ref.dtype)
        dk_carry[...] = dkw[:A_BLOCK, :]
        dv_carry[...] = dvw[:A_BLOCK, :]

    return _call(
        body, name=name, grid=(s_len // A_BLOCK,),
        in_specs=specs + [pl.BlockSpec((A_HEADS, 8, LANES), lambda i: (0, 0, 0)),
                          pl.BlockSpec((A_BLOCK, A_WIDTH), lambda i: (cur(i), 0))],
        out_specs=[pl.BlockSpec((A_BLOCK, WA), lambda i: (cur(i), 0)),
                   pl.BlockSpec((A_HEADS, 8, LANES), lambda i: (0, 0, 0))],
        out_shape=[jax.ShapeDtypeStruct((s_len, WA), MXU_DTYPE), jax.ShapeDtypeStruct((A_HEADS, 8, LANES), F32)],
        scratch_shapes=[pltpu.VMEM((A_BLOCK, LANES), F32), pltpu.VMEM((A_BLOCK, LANES), F32)],
        compiler_params=_params("arbitrary"),
    )(proj_a, proj_a, proj_a, proj_a, proj_a, proj_a, cos, sin, cos, sin, sinks_t, dya)


RG_ROWS = 256


def _rg_gates(x, wa, ba, wx, bx, lam):
    r = jax.nn.sigmoid(mm_nn(x, wa) + ba)
    ig = jax.nn.sigmoid(mm_nn(x, wx) + bx)
    log_a = -R_C * r * _softplus(-lam)
    a = jnp.exp(log_a)
    return a, jnp.sqrt(_one_minus_sq(log_a, a)) * (ig * x)


def _rg_param_specs():
    mat = pl.BlockSpec((R_BLOCKS, R_BLOCK_DIM, R_BLOCK_DIM), lambda i: (0, 0, 0))
    vec = pl.BlockSpec((1, R_WIDTH), lambda i: (0, 0))
    return [mat, vec, mat, vec, vec]


def _rg_fwd(xr, proj_r, wa, ba, wx, bx, lam, *, name):
    s_len = xr.shape[0]
    rows = min(RG_ROWS, s_len)

    def body(x_ref, z_ref, wa_ref, ba_ref, wx_ref, bx_ref, lam_ref, h_ref, y_ref, a_buf, u_buf, carry):
        @pl.when(pl.program_id(0) == 0)
        def _():
            carry[...] = jnp.zeros_like(carry)

        for n in range(R_BLOCKS):
            cols = slice(n * R_BLOCK_DIM, (n + 1) * R_BLOCK_DIM)
            a, u = _rg_gates(x_ref[:, cols], wa_ref[n], ba_ref[:, cols], wx_ref[n], bx_ref[:, cols], lam_ref[:, cols])
            a_buf[:, cols] = a
            u_buf[:, cols] = u

        def step(t, h):
            h = a_buf[pl.ds(t, 1), :] * h + u_buf[pl.ds(t, 1), :]
            h_ref[pl.ds(t, 1), :] = h
            return h

        carry[...] = lax.fori_loop(0, rows, step, carry[...], unroll=16)
        y_ref[...] = (h_ref[...] * _silu(z_ref[...])).astype(y_ref.dtype)

    blk = pl.BlockSpec((rows, R_WIDTH), lambda i: (i, 0))
    return _call(
        body, name=name, grid=(s_len // rows,),
        in_specs=[blk, pl.BlockSpec((rows, R_WIDTH), lambda i: (i, 1))] + _rg_param_specs(),
        out_specs=[blk, blk],
        out_shape=[jax.ShapeDtypeStruct((s_len, R_WIDTH), F32), jax.ShapeDtypeStruct((s_len, R_WIDTH), MXU_DTYPE)],
        scratch_shapes=[pltpu.VMEM((rows, R_WIDTH), F32), pltpu.VMEM((rows, R_WIDTH), F32), pltpu.VMEM((1, R_WIDTH), F32)],
        compiler_params=_params("arbitrary"),
    )(xr, proj_r, wa, ba, wx, bx, lam)


def _rg_bwd(xr, proj_r, h, dyr, wa, ba, wx, bx, lam, *, name):
    s_len = xr.shape[0]
    rows = min(RG_ROWS, s_len)
    nblk = s_len // rows
    per = rows // HALO

    def cur(i):
        return nblk - 1 - i

    def body(x_ref, z_ref, h_ref, hh_ref, dy_ref, wa_ref, ba_ref, wx_ref, bx_ref, lam_ref,
             dx_ref, dz_ref, dwa_ref, dba_ref, dwx_ref, dbx_ref, dlam_ref, a_buf, g_buf, carry):
        i = pl.program_id(0)

        @pl.when(i == 0)
        def _():
            carry[...] = jnp.zeros_like(carry)
            for ref in (dwa_ref, dba_ref, dwx_ref, dbx_ref, dlam_ref):
                ref[...] = jnp.zeros_like(ref)

        z = z_ref[...]
        sig = jax.nn.sigmoid(z)
        hval = h_ref[...]
        dy = dy_ref[...]
        dz_ref[...] = dy * hval * (sig * (1.0 + z * (1.0 - sig)))
        g_buf[...] = dy * (z * sig)
        kept = []
        for n in range(R_BLOCKS):
            cols = slice(n * R_BLOCK_DIM, (n + 1) * R_BLOCK_DIM)
            x = x_ref[:, cols]
            r = jax.nn.sigmoid(_nn(x, wa_ref[n]) + ba_ref[:, cols])
            ig = jax.nn.sigmoid(_nn(x, wx_ref[n]) + bx_ref[:, cols])
            sp = _softplus(-lam_ref[:, cols])
            log_a = -R_C * r * sp
            a = jnp.exp(log_a)
            a_buf[:, cols] = a
            kept.append((x, r, ig, sp, a, jnp.sqrt(_one_minus_sq(log_a, a))))

        def step(k, c):
            t = rows - 1 - k
            g = g_buf[pl.ds(t, 1), :] + c
            g_buf[pl.ds(t, 1), :] = g
            return a_buf[pl.ds(t, 1), :] * g

        carry[...] = lax.fori_loop(0, rows, step, carry[...], unroll=16)
        h_halo = jnp.where(cur(i) == 0, 0.0, hh_ref[...])
        dh = g_buf[...]
        da = dh * _shift_down(hval, h_halo, 1)
        for n in range(R_BLOCKS):
            cols = slice(n * R_BLOCK_DIM, (n + 1) * R_BLOCK_DIM)
            x, r, ig, sp, a, s = kept[n]
            du = dh[:, cols]
            dux = du * x
            d_log_a = a * (da[:, cols] - a * (dux * ig) / s)
            d_ga = d_log_a * (-R_C * sp) * (r * (1.0 - r))
            d_gx = dux * s * (ig * (1.0 - ig))
            dx_ref[:, cols] = du * (s * ig) + _nt(d_ga, wa_ref[n]) + _nt(d_gx, wx_ref[n])
            xt = _t(x)
            dwa_ref[n] += _nn(xt, d_ga)
            dwx_ref[n] += _nn(xt, d_gx)
            dba_ref[:, cols] += jnp.sum(d_ga, axis=0, keepdims=True)
            dbx_ref[:, cols] += jnp.sum(d_gx, axis=0, keepdims=True)
            dlam_ref[:, cols] += jnp.sum(d_log_a * r, axis=0, keepdims=True) * (R_C * jax.nn.sigmoid(-lam_ref[:, cols]))

    blk = pl.BlockSpec((rows, R_WIDTH), lambda i: (cur(i), 0))
    mat = pl.BlockSpec((R_BLOCKS, R_BLOCK_DIM, R_BLOCK_DIM), lambda i: (0, 0, 0))
    vec = pl.BlockSpec((1, R_WIDTH), lambda i: (0, 0))
    return _call(
        body, name=name, grid=(nblk,),
        in_specs=[blk, pl.BlockSpec((rows, R_WIDTH), lambda i: (cur(i), 1)), blk,
                  pl.BlockSpec((HALO, R_WIDTH), lambda i: (jnp.maximum(cur(i) * per - 1, 0), 0)), blk] + _rg_param_specs(),
        out_specs=[blk, blk, mat, vec, mat, vec, vec],
        out_shape=[jax.ShapeDtypeStruct((s_len, R_WIDTH), F32)] * 2 + [
            jax.ShapeDtypeStruct((R_BLOCKS, R_BLOCK_DIM, R_BLOCK_DIM), F32), jax.ShapeDtypeStruct((1, R_WIDTH), F32),
            jax.ShapeDtypeStruct((R_BLOCKS, R_BLOCK_DIM, R_BLOCK_DIM), F32), jax.ShapeDtypeStruct((1, R_WIDTH), F32),
            jax.ShapeDtypeStruct((1, R_WIDTH), F32)],
        scratch_shapes=[pltpu.VMEM((rows, R_WIDTH), F32), pltpu.VMEM((rows, R_WIDTH), F32), pltpu.VMEM((1, R_WIDTH), F32)],
        compiler_params=_params("arbitrary"),
    )(xr, proj_r, h, h, dyr, wa, ba, wx, bx, lam)


GP_CHUNKS = 8
GP_CHUNKS_BWD = 4
GS_CHUNKS = 8


def _seg_cumsum(x, reverse):
    rows = x.shape[0]
    r = _iota(x.shape, 0) & (G_CHUNK - 1)
    s = 1
    while s < G_CHUNK:
        if reverse:
            x = x + jnp.where(r < G_CHUNK - s, pltpu.roll(x, rows - s, 0), 0.0)
        else:
            x = x + jnp.where(r >= s, pltpu.roll(x, s, 0), 0.0)
        s *= 2
    return x


def _gdn_decay(ga, a_log_row, dt_row):
    return -jnp.exp(a_log_row) * _softplus(ga + dt_row)


def _gdn_chunk(cq, ck, cv, gb, gc, inv=None):
    shape = cq.shape
    head = _iota(shape, 0) & (G_HEADS - 1)
    lane = _iota(shape, 2)
    q, k, v = _silu(cq), _silu(ck), _silu(cv)
    q = q * lax.rsqrt(jnp.sum(q * q, axis=-1, keepdims=True) + RMS_EPS) * (G_HEAD_DIM ** -0.5)
    k = k * lax.rsqrt(jnp.sum(k * k, axis=-1, keepdims=True) + RMS_EPS)
    beta = jnp.sum(jnp.where(lane == head, jax.nn.sigmoid(gb), 0.0), axis=-1, keepdims=True)
    g = jnp.sum(jnp.where(lane == head + G_HEADS, gc, 0.0), axis=-1, keepdims=True)
    sq = (shape[0], G_CHUNK, G_CHUNK)
    row, col = _iota(sq, 1), _iota(sq, 2)
    g_sq = jnp.broadcast_to(g, sq)
    decay = jnp.where(row >= col, jnp.exp(jnp.minimum(g_sq - _t(g_sq), 0.0)), 0.0)
    g_last = jnp.sum(jnp.where(_iota(g.shape, 1) == G_CHUNK - 1, g, 0.0), axis=1, keepdims=True)
    eg = jnp.exp(g)
    kb, vb = k * beta, v * beta
    m = jnp.where(row > col, mm_nt(kb, k) * decay, 0.0)
    known = inv is not None
    if not known:
        inv = _inv_unit_lower(m)
    u, w = _solve2(m, inv, vb, kb * eg)
    qk = jnp.where(row >= col, mm_nt(q, k) * decay, 0.0)
    q_dec = q * eg
    k_dec = k * jnp.exp(g_last - g)
    gl = jnp.broadcast_to(jnp.exp(g_last), (shape[0], 1, G_HEAD_DIM))
    return (u, w, qk, q_dec, k_dec, gl) if known else (u, w, qk, q_dec, k_dec, gl, inv)


def _gdn_step(state, u, w, qk, q_dec, k_dec, gl, gz, norm_w):
    v_new = u - mm_nn(w, state)
    o = mm_nn(q_dec, state) + mm_nn(qk, v_new)
    new_state = state * gl + mm_nn(_t(k_dec), v_new)
    o = o * lax.rsqrt(jnp.mean(o * o, axis=-1, keepdims=True) + RMS_EPS) * norm_w
    return o * _silu(gz), new_state


def _stack_chunks(x, heads):
    chunks = x.shape[0] // G_CHUNK
    parts = []
    for c in range(chunks):
        rows = slice(c * G_CHUNK, (c + 1) * G_CHUNK)
        for hd in range(G_HEADS):
            parts.append(x[rows, hd * LANES:(hd + 1) * LANES] if heads else x[rows, :])
    return jnp.stack(parts)


def _gdn_chunk_shapes(nch):
    b = nch * G_HEADS
    wide = jax.ShapeDtypeStruct((b, G_CHUNK, G_HEAD_DIM), F32)
    return [wide, wide, jax.ShapeDtypeStruct((b, G_CHUNK, G_CHUNK), F32), wide, wide,
            jax.ShapeDtypeStruct((b, 1, G_HEAD_DIM), F32)]


def _gdn_chunk_specs(nbatch):
    wide = pl.BlockSpec((nbatch, G_CHUNK, G_HEAD_DIM), lambda i: (i, 0, 0))
    return [wide, wide, pl.BlockSpec((nbatch, G_CHUNK, G_CHUNK), lambda i: (i, 0, 0)), wide, wide,
            pl.BlockSpec((nbatch, 1, G_HEAD_DIM), lambda i: (i, 0, 0))]


def _gdn_chunk_fwd(conv, proj_g, a_log_row, dt_row, *, name):
    s_len = conv.shape[0]
    cpg = min(GP_CHUNKS, s_len // G_CHUNK)
    rows = cpg * G_CHUNK
    nbatch = cpg * G_HEADS

    def body(c_ref, bg_ref, al_ref, dt_ref, *outs):
        bg = bg_ref[...]
        gc = _seg_cumsum(_gdn_decay(bg, al_ref[...], dt_ref[...]), False)
        res = _gdn_chunk(_stack_chunks(c_ref[:, 0:G_WIDTH], True), _stack_chunks(c_ref[:, G_WIDTH:2 * G_WIDTH], True),
                         _stack_chunks(c_ref[:, 2 * G_WIDTH:], True), _stack_chunks(bg, False), _stack_chunks(gc, False))
        for ref, val in zip(outs, res):
            ref[...] = val

    row = pl.BlockSpec((1, LANES), lambda i: (0, 0))
    return _call(
        body, name=name, grid=(s_len // rows,),
        in_specs=[pl.BlockSpec((rows, 3 * G_WIDTH), lambda i: (i, 0)),
                  pl.BlockSpec((rows, LANES), lambda i: (i, (3 * G_WIDTH + G_WIDTH) // LANES)), row, row],
        out_specs=_gdn_chunk_specs(nbatch) + [pl.BlockSpec((nbatch, G_CHUNK, G_CHUNK), lambda i: (i, 0, 0))],
        out_shape=_gdn_chunk_shapes(s_len // G_CHUNK) + [
            jax.ShapeDtypeStruct((s_len // G_CHUNK * G_HEADS, G_CHUNK, G_CHUNK), F32)],
        compiler_params=_params("parallel"),
    )(conv, proj_g, a_log_row, dt_row)


def _gdn_chunk_bwd(conv, proj_g, a_log_row, dt_row, inv, cots, *, name):
    s_len = conv.shape[0]
    cpg = min(GP_CHUNKS_BWD, s_len // G_CHUNK)
    rows = cpg * G_CHUNK
    nbatch = cpg * G_HEADS

    def unstack(x, heads):
        if heads:
            return jnp.concatenate([jnp.concatenate([x[c * G_HEADS + hd] for hd in range(G_HEADS)], axis=1)
                                    for c in range(cpg)], axis=0)
        return jnp.concatenate([sum(x[c * G_HEADS + hd] for hd in range(G_HEADS)) for c in range(cpg)], axis=0)

    def body(c_ref, bg_ref, al_ref, dt_ref, inv_ref, du, dw, dqk, dqd, dkd, dgl, dc_ref, dbg_ref, dal_ref, ddt_ref):
        @pl.when(pl.program_id(0) == 0)
        def _():
            dal_ref[...] = jnp.zeros_like(dal_ref)
            ddt_ref[...] = jnp.zeros_like(ddt_ref)

        bg = bg_ref[...]
        g_all, decay_vjp = jax.vjp(_gdn_decay, bg, al_ref[...], dt_ref[...])
        gc = _seg_cumsum(g_all, False)
        _, vjp = jax.vjp(_gdn_chunk, _stack_chunks(c_ref[:, 0:G_WIDTH], True),
                         _stack_chunks(c_ref[:, G_WIDTH:2 * G_WIDTH], True), _stack_chunks(c_ref[:, 2 * G_WIDTH:], True),
                         _stack_chunks(bg, False), _stack_chunks(gc, False), inv_ref[...])
        dq, dk, dv, dgb, dgc, _ = vjp((du[...], dw[...], dqk[...], dqd[...], dkd[...], dgl[...]))
        dc_ref[:, 0:G_WIDTH] = unstack(dq, True)
        dc_ref[:, G_WIDTH:2 * G_WIDTH] = unstack(dk, True)
        dc_ref[:, 2 * G_WIDTH:] = unstack(dv, True)
        dga, dal, ddt = decay_vjp(_seg_cumsum(unstack(dgc, False), True))
        dbg_ref[:, 0:LANES] = unstack(dgb, False) + dga
        dbg_ref[:, LANES:] = jnp.zeros((rows, LANES), F32)
        dal_ref[...] += dal
        ddt_ref[...] += ddt

    row = pl.BlockSpec((1, LANES), lambda i: (0, 0))
    return _call(
        body, name=name, grid=(s_len // rows,),
        in_specs=[pl.BlockSpec((rows, 3 * G_WIDTH), lambda i: (i, 0)),
                  pl.BlockSpec((rows, LANES), lambda i: (i, (3 * G_WIDTH + G_WIDTH) // LANES)), row, row,
                  pl.BlockSpec((nbatch, G_CHUNK, G_CHUNK), lambda i: (i, 0, 0))]
        + _gdn_chunk_specs(nbatch),
        out_specs=[pl.BlockSpec((rows, 3 * G_WIDTH), lambda i: (i, 0)), pl.BlockSpec((rows, 2 * LANES), lambda i: (i, 0)),
                   row, row],
        out_shape=[jax.ShapeDtypeStruct((s_len, 3 * G_WIDTH), F32), jax.ShapeDtypeStruct((s_len, 2 * LANES), F32),
                   jax.ShapeDtypeStruct((1, LANES), F32), jax.ShapeDtypeStruct((1, LANES), F32)],
        compiler_params=_params("arbitrary"),
    )(conv, proj_g, a_log_row, dt_row, inv, *cots)


def _gdn_scan_specs(cpg, which):
    nbatch = cpg * G_HEADS
    wide = pl.BlockSpec((nbatch, G_CHUNK, G_HEAD_DIM), lambda i: (which(i), 0, 0))
    return [wide, wide, pl.BlockSpec((nbatch, G_CHUNK, G_CHUNK), lambda i: (which(i), 0, 0)), wide, wide,
            pl.BlockSpec((nbatch, 1, G_HEAD_DIM), lambda i: (which(i), 0, 0))]


def _gz_stack(z_ref, c):
    rows = pl.ds(pl.multiple_of(c * G_CHUNK, G_CHUNK), G_CHUNK)
    return jnp.stack([z_ref[rows, hd * LANES:(hd + 1) * LANES] for hd in range(G_HEADS)])


def _gdn_scan_fwd(chunk_vals, proj_g, norm_w, *, name):
    s_len = proj_g.shape[0]
    nch = s_len // G_CHUNK
    cpg = min(GS_CHUNKS, nch)
    rows = cpg * G_CHUNK

    def body(u_ref, w_ref, qk_ref, qd_ref, kd_ref, gl_ref, z_ref, nw_ref, y_ref, st_ref, state):
        @pl.when(pl.program_id(0) == 0)
        def _():
            state[...] = jnp.zeros_like(state)

        def step(c, carry):
            b = pl.ds(pl.multiple_of(c * G_HEADS, G_HEADS), G_HEADS)
            st = state[...]
            st_ref[b] = st
            y, new_state = _gdn_step(st, u_ref[b], w_ref[b], qk_ref[b], qd_ref[b], kd_ref[b], gl_ref[b],
                                     _gz_stack(z_ref, c), nw_ref[...])
            state[...] = new_state
            rws = pl.ds(pl.multiple_of(c * G_CHUNK, G_CHUNK), G_CHUNK)
            for hd in range(G_HEADS):
                y_ref[rws, hd * LANES:(hd + 1) * LANES] = y[hd].astype(y_ref.dtype)
            return carry

        lax.fori_loop(0, cpg, step, 0, unroll=4)

    return _call(
        body, name=name, grid=(nch // cpg,),
        in_specs=_gdn_scan_specs(cpg, lambda i: i) + [
            pl.BlockSpec((rows, G_WIDTH), lambda i: (i, 3)), pl.BlockSpec((1, G_HEAD_DIM), lambda i: (0, 0))],
        out_specs=[pl.BlockSpec((rows, G_WIDTH), lambda i: (i, 0)),
                   pl.BlockSpec((cpg * G_HEADS, G_HEAD_DIM, G_HEAD_DIM), lambda i: (i, 0, 0))],
        out_shape=[jax.ShapeDtypeStruct((s_len, G_WIDTH), MXU_DTYPE),
                   jax.ShapeDtypeStruct((nch * G_HEADS, G_HEAD_DIM, G_HEAD_DIM), F32)],
        scratch_shapes=[pltpu.VMEM((G_HEADS, G_HEAD_DIM, G_HEAD_DIM), F32)],
        compiler_params=_params("arbitrary"),
    )(*chunk_vals, proj_g, norm_w)


def _gdn_scan_bwd(chunk_vals, states, proj_g, norm_w, dyg, *, name):
    s_len = proj_g.shape[0]
    nch = s_len // G_CHUNK
    cpg = min(GS_CHUNKS, nch)
    rows = cpg * G_CHUNK
    ngrid = nch // cpg

    def cur(i):
        return ngrid - 1 - i

    def body(u_ref, w_ref, qk_ref, qd_ref, kd_ref, gl_ref, st_ref, z_ref, nw_ref, dy_ref,
             du_ref, dw_ref, dqk_ref, dqd_ref, dkd_ref, dgl_ref, dz_ref, dnw_ref, dstate):
        @pl.when(pl.program_id(0) == 0)
        def _():
            dstate[...] = jnp.zeros_like(dstate)
            dnw_ref[...] = jnp.zeros_like(dnw_ref)

        def step(k, carry):
            c = cpg - 1 - k
            b = pl.ds(pl.multiple_of(c * G_HEADS, G_HEADS), G_HEADS)
            _, vjp = jax.vjp(_gdn_step, st_ref[b], u_ref[b], w_ref[b], qk_ref[b], qd_ref[b], kd_ref[b], gl_ref[b],
                             _gz_stack(z_ref, c), nw_ref[...])
            dst, du, dw, dqk, dqd, dkd, dgl, dz, dnw = vjp((_gz_stack(dy_ref, c), dstate[...]))
            dstate[...] = dst
            du_ref[b], dw_ref[b], dqk_ref[b], dqd_ref[b], dkd_ref[b], dgl_ref[b] = du, dw, dqk, dqd, dkd, dgl
            rws = pl.ds(pl.multiple_of(c * G_CHUNK, G_CHUNK), G_CHUNK)
            for hd in range(G_HEADS):
                dz_ref[rws, hd * LANES:(hd + 1) * LANES] = dz[hd]
            dnw_ref[...] += dnw
            return carry

        lax.fori_loop(0, cpg, step, 0, unroll=4)

    gate = pl.BlockSpec((rows, G_WIDTH), lambda i: (cur(i), 3))
    wide = pl.BlockSpec((rows, G_WIDTH), lambda i: (cur(i), 0))
    vec = pl.BlockSpec((1, G_HEAD_DIM), lambda i: (0, 0))
    return _call(
        body, name=name, grid=(ngrid,),
        in_specs=_gdn_scan_specs(cpg, cur) + [
            pl.BlockSpec((cpg * G_HEADS, G_HEAD_DIM, G_HEAD_DIM), lambda i: (cur(i), 0, 0)), gate, vec, wide],
        out_specs=_gdn_scan_specs(cpg, cur) + [wide, vec],
        out_shape=_gdn_chunk_shapes(nch) + [jax.ShapeDtypeStruct((s_len, G_WIDTH), F32),
                                            jax.ShapeDtypeStruct((1, G_HEAD_DIM), F32)],
        scratch_shapes=[pltpu.VMEM((G_HEADS, G_HEAD_DIM, G_HEAD_DIM), F32)],
        compiler_params=_params("arbitrary"),
    )(*chunk_vals, states, proj_g, norm_w, dyg)


def _adamw_math(w, g, m, v):
    m = ADAM_B1 * m + (1.0 - ADAM_B1) * g
    v = ADAM_B2 * v + (1.0 - ADAM_B2) * (g * g)
    m_hat = m / (1.0 - ADAM_B1 ** ADAM_STEP)
    v_hat = v / (1.0 - ADAM_B2 ** ADAM_STEP)
    delta = -ADAM_LR * (m_hat / (jnp.sqrt(v_hat) + ADAM_EPS) + ADAM_WD * w)
    return delta, m, v


def _sum_adamw(own, chip, parts, w, m, v, *, name, rows):
    n_layers, n_rows, n_cols = w.shape
    rows = min(rows, n_rows)
    n_parts = parts[0].shape[0]

    def body(c_ref, *refs):
        own_refs, part_refs = refs[:n_layers], refs[n_layers:2 * n_layers]
        w_ref, m_ref, v_ref, g_ref, d_ref, nm_ref, nv_ref = refs[2 * n_layers:]
        layer = pl.program_id(0)
        g = None
        for l in range(n_layers):
            g_l = own_refs[l][0].astype(F32)
            for k in range(n_parts):
                g_l = g_l + part_refs[l][k].astype(F32)
            g = g_l if g is None else jnp.where(layer == l, g_l, g)
        delta, new_m, new_v = _adamw_math(w_ref[0], g, m_ref[0], v_ref[0])
        g_ref[0], d_ref[0], nm_ref[0], nv_ref[0] = g, delta, new_m, new_v

    blk = pl.BlockSpec((1, rows, n_cols), lambda l, i, c: (l, i, 0))
    grid_spec = pltpu.PrefetchScalarGridSpec(
        num_scalar_prefetch=1, grid=(n_layers, n_rows // rows),
        in_specs=[pl.BlockSpec((1, rows, n_cols), lambda l, i, c: (c[0], i, 0))] * n_layers
        + [pl.BlockSpec((n_parts, rows, n_cols), lambda l, i, c: (0, i, 0))] * n_layers + [blk, blk, blk],
        out_specs=[blk] * 4)
    return _call(
        body, name=name, grid_spec=grid_spec, out_shape=[jax.ShapeDtypeStruct(w.shape, F32)] * 4,
        compiler_params=_params("parallel", "parallel"),
    )(_index_operand(chip), *own, *parts, w, m, v)


def _sum_slots(parts, *, name):
    rows = parts.shape[1]

    def body(p_ref, o_ref):
        g = p_ref[0]
        for k in range(1, N_DEV):
            g = g + p_ref[k]
        o_ref[...] = g

    return _call(body, name=name, grid=(1,),
                 in_specs=[pl.BlockSpec(parts.shape, lambda i: (0, 0, 0))],
                 out_specs=pl.BlockSpec((rows, LANES), lambda i: (0, 0)),
                 out_shape=jax.ShapeDtypeStruct((rows, LANES), F32), compiler_params=_params("arbitrary"))(parts)


def _adamw_packed(w, g, m, v, *, name):
    def body(w_ref, g_ref, m_ref, v_ref, d_ref, nm_ref, nv_ref):
        d_ref[...], nm_ref[...], nv_ref[...] = _adamw_math(w_ref[...], g_ref[...], m_ref[...], v_ref[...])

    blk = pl.BlockSpec(w.shape, lambda i: (0, 0))
    return _call(body, name=name, grid=(1,), in_specs=[blk] * 4, out_specs=[blk] * 3,
                 out_shape=[jax.ShapeDtypeStruct(w.shape, F32)] * 3, compiler_params=_params("arbitrary"))(w, g, m, v)


A_COLS = ((0, 512), (768, 1280), (512, 768))
R_COLS = ((1280, 3328),)
G_COLS = ((3328, 5384),)


def _group_weights(wt_full):
    def take(ranges):
        return jnp.concatenate([wt_full[a:b] for a, b in ranges], axis=0)

    wt_g = jnp.concatenate([take(G_COLS), jnp.zeros((G_PAD, wt_full.shape[1]), wt_full.dtype)], axis=0)
    return take(A_COLS), take(R_COLS), wt_g


def _ungroup_grads(d_a, d_r, d_g):
    return jnp.concatenate([d_a[0:512], d_a[1024:1280], d_a[512:1024], d_r, d_g[:WG - G_PAD]], axis=0)


def _shard_rows(w):
    return jnp.pad(jnp.transpose(w, (0, 2, 1)), ((0, 0), (0, N_ROWS_PAD - N_IN_SHARD), (0, 0)))


def _unshard_rows(wt):
    return jnp.transpose(wt[:, :N_IN_SHARD], (0, 2, 1))


def _owner_blocks(dwt):
    blocks = jnp.pad(dwt.reshape(4, 2, N_IN_SHARD, D_MODEL), ((0, 0), (0, 0), (0, N_ROWS_PAD - N_IN_SHARD), (0, 0)))
    return jnp.transpose(blocks, (1, 0, 2, 3))


def _rope_tables(s_len):
    inv = 1.0 / (ROPE_THETA ** (jnp.arange(0, A_HEAD_DIM, 2, dtype=F32) / A_HEAD_DIM))
    ang = jnp.arange(s_len, dtype=F32)[:, None] * inv[None, :]
    cos, sin = jnp.cos(ang), jnp.sin(ang)
    return jnp.tile(cos, (1, 4)), jnp.tile(jnp.concatenate([-sin, sin], axis=1), (1, 2))


SMALL = ("sinks", "r_conv_w", "r_conv_b", "r_wa", "r_ba", "r_wx", "r_bx", "r_lam", "g_conv_w", "g_a_log", "g_dt_bias",
         "g_norm_w", "ln_g", "ln_b")


def _pack(leaves):
    rows = []
    for leaf in leaves:
        flat = leaf.reshape(-1)
        pad = (-flat.shape[0]) % (8 * LANES)
        rows.append(jnp.pad(flat, (0, pad)).reshape(-1, LANES))
    return jnp.concatenate(rows, axis=0)


def _unpack(packed, shapes):
    out, row = [], 0
    for shape in shapes:
        size = math.prod(shape)
        nrows = -(-size // (8 * LANES)) * 8
        out.append(packed[row:row + nrows].reshape(-1)[:size].reshape(shape))
        row += nrows
    return out


def _lane_row(vals, offset):
    return jnp.pad(vals, (offset, LANES - offset - vals.shape[0])).reshape(1, LANES)


def kernel(x, w_in, sinks, r_conv_w, r_conv_b, r_wa, r_ba, r_wx, r_bx, r_lam, g_conv_w, g_a_log, g_dt_bias, g_norm_w, w_out, ln_g, ln_b, loss_target, m_w_in, m_sinks, m_r_conv_w, m_r_conv_b, m_r_wa, m_r_ba, m_r_wx, m_r_bx, m_r_lam, m_g_conv_w, m_g_a_log, m_g_dt_bias, m_g_norm_w, m_w_out, m_ln_g, m_ln_b, v_w_in, v_sinks, v_r_conv_w, v_r_conv_b, v_r_wa, v_r_ba, v_r_wx, v_r_bx, v_r_lam, v_g_conv_w, v_g_a_log, v_g_dt_bias, v_g_norm_w, v_w_out, v_ln_g, v_ln_b):
    s_len = x.shape[1]
    x0 = x.reshape(s_len, D_MODEL)
    target = loss_target.reshape(s_len, D_MODEL)
    me = 4 * lax.axis_index("x") + 2 * lax.axis_index("y") + lax.axis_index("c")
    core, chip = lax.axis_index("c"), 2 * lax.axis_index("x") + lax.axis_index("y")

    win_pieces = _shard_rows(w_in).astype(MXU_DTYPE).reshape(DEPTH, 2, N_ROWS_PAD // 2, D_MODEL)
    wout_pieces = w_out.astype(MXU_DTYPE).reshape(DEPTH, 2, OUT_SHARD // 2, D_MODEL)
    win0_all, wout0_all, rcw_all, gcw_all = _all_gather(
        [win_pieces[0], wout_pieces[0], r_conv_w[None], g_conv_w[None]], "gather_weights")
    rcw_full = jnp.moveaxis(rcw_all[:, 0], 0, 2).reshape(DEPTH, CONV_WIDTH, R_WIDTH)
    gcw_full = jnp.moveaxis(gcw_all[:, 0], 0, 2).reshape(DEPTH, CONV_WIDTH, 3 * G_WIDTH)
    cos, sin = _rope_tables(s_len)

    def big_weights(win_all, wout_all):
        wt_a, wt_r, wt_g = _group_weights(win_all.reshape(N_DEV, N_ROWS_PAD, D_MODEL)[:, :N_IN_SHARD].reshape(N_IN, D_MODEL))
        wo = wout_all.reshape(D_MODEL, D_MODEL)
        return dict(wt_a=wt_a, wt_r=wt_r, wt_g=wt_g, wo=wo,
                    wo_a=wo[0:A_WIDTH], wo_r=wo[A_WIDTH:A_WIDTH + R_WIDTH], wo_g=wo[A_WIDTH + R_WIDTH:])

    layers = []
    for l in range(DEPTH):
        layers.append(dict(
            sinks_t=jnp.broadcast_to(sinks[l][:, None, None], (A_HEADS, 8, LANES)),
            rcw=rcw_full[l], rcb=r_conv_b[l].reshape(1, R_WIDTH), wa=r_wa[l], ba=r_ba[l].reshape(1, R_WIDTH),
            wx=r_wx[l], bx=r_bx[l].reshape(1, R_WIDTH), lam=r_lam[l].reshape(1, R_WIDTH),
            gcw=gcw_full[l], zero_b=jnp.zeros((1, 3 * G_WIDTH), F32),
            a_log=_lane_row(g_a_log[l], G_HEADS), dt=_lane_row(g_dt_bias[l], G_HEADS),
            norm_w=g_norm_w[l].reshape(1, G_HEAD_DIM), ln_g=ln_g[l].reshape(1, D_MODEL), ln_b=ln_b[l].reshape(1, D_MODEL)))

    saved = []
    xin, xin_lo = x0, x0.astype(MXU_DTYPE)
    layers[0].update(big_weights(win0_all, wout0_all))
    for l, p in enumerate(layers):
        if l + 1 < DEPTH:
            proj_a, (wout_next,) = _matmul([xin_lo], [p["wt_a"]], name=f"proj_a{l}", tm=1024, tn=1280, b_t=True,
                                           comm=_GatherSend([wout_pieces[l + 1]]))
            proj_r, (win_next_0,) = _matmul([xin_lo], [p["wt_r"]], name=f"proj_r{l}", tm=1024, tn=1024, b_t=True,
                                            comm=_GatherSend([win_pieces[l + 1, 0:1]]))
            proj_g, (win_next_1,) = _matmul([xin_lo], [p["wt_g"]], name=f"proj_g{l}", tm=1024, tn=1152, b_t=True,
                                            comm=_GatherSend([win_pieces[l + 1, 1:2]]))
            forward_next = _GatherForward([win_next_0, win_next_1, wout_next])
        else:
            forward_next = None
            proj_a = _matmul([xin_lo], [p["wt_a"]], name=f"proj_a{l}", tm=1024, tn=1280, b_t=True)
            proj_r = _matmul([xin_lo], [p["wt_r"]], name=f"proj_r{l}", tm=1024, tn=1024, b_t=True)
            proj_g = _matmul([xin_lo], [p["wt_g"]], name=f"proj_g{l}", tm=1024, tn=1152, b_t=True)
        ya = _attn_fwd(proj_a, cos, sin, p["sinks_t"], name=f"attn_fwd{l}")
        xr = _conv_fwd(proj_r, R_WIDTH, p["rcw"], p["rcb"], name=f"rconv_fwd{l}")
        h, yr = _rg_fwd(xr, proj_r, p["wa"], p["ba"], p["wx"], p["bx"], p["lam"], name=f"rglru_fwd{l}")
        conv = _conv_fwd(proj_g, 3 * G_WIDTH, p["gcw"], p["zero_b"], name=f"gconv_fwd{l}")
        *chunk_vals, inv = _gdn_chunk_fwd(conv, proj_g, p["a_log"], p["dt"], name=f"gdn_chunk_fwd{l}")
        yg, states = _gdn_scan_fwd(chunk_vals, proj_g, p["norm_w"], name=f"gdn_scan_fwd{l}")
        if forward_next is None:
            z, xout, xout_lo = _outproj_ln(ya, yr, yg, p["wo"], xin, p["ln_g"], p["ln_b"], name=f"outproj_ln{l}")
        else:
            (z, xout, xout_lo), (win_0, win_1, wout_all) = _outproj_ln(
                ya, yr, yg, p["wo"], xin, p["ln_g"], p["ln_b"], name=f"outproj_ln{l}", comm=forward_next)
            layers[l + 1].update(big_weights(jnp.concatenate([win_0, win_1], axis=1), wout_all))
        saved.append(dict(xin_lo=xin_lo, proj_a=proj_a, proj_r=proj_r, proj_g=proj_g, ya=ya, yr=yr, yg=yg, xr=xr, h=h,
                          conv=conv, chunk_vals=chunk_vals, inv=inv, states=states, z=z))
        xin, xin_lo = xout, xout_lo

    grads = [None] * DEPTH
    dxn = None
    loss_local = None
    for l in reversed(range(DEPTH)):
        p, sv = layers[l], saved[l]
        if dxn is None:
            dz, dz_lo, dln_g, dln_b, loss_local = _ln_bwd(sv["z"], p["ln_g"], name=f"ln_bwd{l}", xn=xin, target=target)
        else:
            dz, dz_lo, dln_g, dln_b = _ln_bwd(sv["z"], p["ln_g"], name=f"ln_bwd{l}", dxn=dxn)
        dya = _matmul([dz_lo], [p["wo_a"]], name=f"dya{l}", tm=1024, tn=512, b_t=True)
        dyr = _matmul([dz_lo], [p["wo_r"]], name=f"dyr{l}", tm=1024, tn=1024, b_t=True)
        dyg = _matmul([dz_lo], [p["wo_g"]], name=f"dyg{l}", tm=1024, tn=512, b_t=True)
        dwo = jnp.concatenate([
            _matmul_tn(sv["ya"], dz_lo, name=f"dwo_a{l}", tm=512, tn=1024, tk=1024),
            _matmul_tn(sv["yr"], dz_lo, name=f"dwo_r{l}", tm=1024, tn=1024, tk=1024),
            _matmul_tn(sv["yg"], dz_lo, name=f"dwo_g{l}", tm=512, tn=1024, tk=1024)], axis=0)

        dproj_a, dsinks_t = _attn_bwd(sv["proj_a"], cos, sin, p["sinks_t"], dya, name=f"attn_bwd{l}")

        dxr, drz, dwa, dba, dwx, dbx, dlam = _rg_bwd(sv["xr"], sv["proj_r"], sv["h"], dyr, p["wa"], p["ba"], p["wx"],
                                                     p["bx"], p["lam"], name=f"rglru_bwd{l}")
        dproj_r, drcw, drcb = _conv_bwd(dxr, sv["proj_r"], R_WIDTH, p["rcw"], [drz], name=f"rconv_bwd{l}")

        scan_out = _gdn_scan_bwd(sv["chunk_vals"], sv["states"], sv["proj_g"], p["norm_w"], dyg, name=f"gdn_scan_bwd{l}")
        dgz, dnorm_w = scan_out[6], scan_out[7]
        dconv, dbg, dal, ddt = _gdn_chunk_bwd(sv["conv"], sv["proj_g"], p["a_log"], p["dt"], sv["inv"], scan_out[:6],
                                              name=f"gdn_chunk_bwd{l}")
        dproj_g, dgcw, _ = _conv_bwd(dconv, sv["proj_g"], 3 * G_WIDTH, p["gcw"], [dgz, dbg], name=f"gconv_bwd{l}")

        grads[l] = dict(
            sinks=dsinks_t[:, :, 0].sum(axis=1), r_conv_w=drcw.reshape(CONV_WIDTH, R_WIDTH),
            r_conv_b=drcb.reshape(R_WIDTH), r_wa=dwa, r_ba=dba.reshape(R_WIDTH), r_wx=dwx, r_bx=dbx.reshape(R_WIDTH),
            r_lam=dlam.reshape(R_WIDTH), g_conv_w=dgcw.reshape(CONV_WIDTH, 3 * G_WIDTH),
            g_a_log=dal[0, G_HEADS:2 * G_HEADS], g_dt_bias=ddt[0, G_HEADS:2 * G_HEADS],
            g_norm_w=dnorm_w.reshape(G_HEAD_DIM), ln_g=dln_g.reshape(D_MODEL), ln_b=dln_b.reshape(D_MODEL))

        if l > 0:
            dwin_a = _matmul_tn(dproj_a, sv["xin_lo"], name=f"dwin_a{l}", tm=640, tn=1024, tk=1024)
            dwin_r = _matmul_tn(dproj_r, sv["xin_lo"], name=f"dwin_r{l}", tm=1024, tn=1024, tk=1024)
        else:
            packed_small = _pack([jnp.stack([grads[k][nm] for k in range(DEPTH)]) for nm in SMALL])
            dwin_a, (sent_small,) = _matmul_tn(
                dproj_a, sv["xin_lo"], name=f"dwin_a{l}", tm=640, tn=1024, tk=1024,
                comm=_GatherSend([packed_small.reshape(4, packed_small.shape[0] // 4, LANES)]))
            dwin_r, (all_small,) = _matmul_tn(dproj_r, sv["xin_lo"], name=f"dwin_r{l}", tm=1024, tn=1024, tk=1024,
                                              comm=_GatherForward([sent_small]))
        dwin = _ungroup_grads(dwin_a, dwin_r,
                              _matmul_tn(dproj_g, sv["xin_lo"], name=f"dwin_g{l}", tm=1152, tn=1024, tk=1024))

        dwin_blocks = _owner_blocks(dwin)[:, :, None].astype(MXU_DTYPE)
        dwout_blocks = jnp.transpose(dwo.reshape(4, 2, OUT_SHARD, D_MODEL), (1, 0, 2, 3))[:, :, None].astype(MXU_DTYPE)
        got_win, got_wout = _swap_cores(
            [dwin_blocks.reshape(2, 8, N_ROWS_PAD // 2, D_MODEL), dwout_blocks.reshape(2, 4, OUT_SHARD, D_MODEL)],
            f"swap_core_grads{l}")
        chip_win = _add_pair(dwin_blocks, got_win.reshape(dwin_blocks.shape[1:]), core, name=f"add_core_grads_w_in{l}",
                             rows=352).reshape(4, N_ROWS_PAD, D_MODEL)
        chip_wout = _add_pair(dwout_blocks, got_wout.reshape(dwout_blocks.shape[1:]), core, name=f"add_core_grads_w_out{l}",
                              rows=256).reshape(4, OUT_SHARD, D_MODEL)
        dxn, (win_parts, wout_parts) = _matmul(
            [dproj_a, dproj_r, dproj_g], [p["wt_a"], p["wt_r"], p["wt_g"]], name=f"dx{l}", tm=512, tn=1024, add=dz,
            add_scale=DEEPNORM_ALPHA, comm=_ChipExchange([chip_win, chip_wout]))
        grads[l].update(chip_win=chip_win, chip_wout=chip_wout, win_parts=win_parts, wout_parts=wout_parts)
    grad_x = dxn.reshape(x.shape)
    loss = lax.psum(loss_local[0, 0], ("x", "y", "c"))

    def stacked(name):
        return jnp.stack([grads[l][name] for l in range(DEPTH)])

    def per_layer(name):
        return [grads[l][name] for l in range(DEPTH)]

    w_in_t = [_unshard_rows(t) for t in _sum_adamw(per_layer("chip_win"), chip, per_layer("win_parts"), _shard_rows(w_in),
                                                   _shard_rows(m_w_in), _shard_rows(v_w_in), name="adamw_w_in", rows=176)]
    g_w_in, d_w_in, nm_w_in, nv_w_in = w_in_t
    g_w_out, d_w_out, nm_w_out, nv_w_out = _sum_adamw(per_layer("chip_wout"), chip, per_layer("wout_parts"), w_out,
                                                      m_w_out, v_w_out, name="adamw_w_out", rows=128)

    small = list(SMALL)
    full_shapes = [stacked(nm).shape for nm in small]
    all_small = all_small.reshape(N_DEV, packed_small.shape[0], LANES)
    g_small = dict(zip(small, _unpack(_sum_slots(all_small, name="sum_small_grads"), full_shapes)))
    g_small["r_conv_w"] = lax.dynamic_slice_in_dim(g_small["r_conv_w"], me * (R_WIDTH // N_DEV), R_WIDTH // N_DEV, axis=2)
    g_small["g_conv_w"] = lax.dynamic_slice_in_dim(g_small["g_conv_w"], me * (3 * G_WIDTH // N_DEV), 3 * G_WIDTH // N_DEV, axis=2)
    given = dict(sinks=(sinks, m_sinks, v_sinks), r_conv_w=(r_conv_w, m_r_conv_w, v_r_conv_w),
                 r_conv_b=(r_conv_b, m_r_conv_b, v_r_conv_b), r_wa=(r_wa, m_r_wa, v_r_wa), r_ba=(r_ba, m_r_ba, v_r_ba),
                 r_wx=(r_wx, m_r_wx, v_r_wx), r_bx=(r_bx, m_r_bx, v_r_bx), r_lam=(r_lam, m_r_lam, v_r_lam),
                 g_conv_w=(g_conv_w, m_g_conv_w, v_g_conv_w), g_a_log=(g_a_log, m_g_a_log, v_g_a_log),
                 g_dt_bias=(g_dt_bias, m_g_dt_bias, v_g_dt_bias), g_norm_w=(g_norm_w, m_g_norm_w, v_g_norm_w),
                 ln_g=(ln_g, m_ln_g, v_ln_g), ln_b=(ln_b, m_ln_b, v_ln_b))
    shard_shapes = [given[nm][0].shape for nm in small]
    packed = [_pack([given[nm][k] for nm in small]) for k in range(3)]
    d_p, nm_p, nv_p = _adamw_packed(packed[0], _pack([g_small[nm] for nm in small]), packed[1], packed[2], name="adamw_small")
    d_small = dict(zip(small, _unpack(d_p, shard_shapes)))
    nm_small = dict(zip(small, _unpack(nm_p, shard_shapes)))
    nv_small = dict(zip(small, _unpack(nv_p, shard_shapes)))

    order = ["w_in"] + small[:12] + ["w_out"] + small[12:]

    def leaf(big_in, big_out, table):
        return [big_in if nm == "w_in" else big_out if nm == "w_out" else table[nm] for nm in order]

    return (loss, grad_x, *leaf(g_w_in, g_w_out, g_small), *leaf(d_w_in, d_w_out, d_small),
            *leaf(nm_w_in, nm_w_out, nm_small), *leaf(nv_w_in, nv_w_out, nv_small))
```
